```python
import math
import jax, jax.numpy as jnp
from jax import lax
import numpy as np

D_MODEL = 1024
BATCH = 8
SEQ = 2048
DEPTH = 2
DEC_BATCH = 32
DEC_SEQ = 8
PAST_LEN = 8192
PAGE_SIZE = 128

N_EVEN = (DEPTH + 1) // 2
N_ODD = DEPTH // 2

D_A = D_MODEL // 2
DH_A = 64
H_A = D_A // DH_A
R_DECAY = 64
R_ICLR = 64
R_GATE = 128
A_PROJ = 3 * D_A + R_DECAY + R_ICLR + R_GATE
GN_EPS = 64e-5
D_B = D_MODEL // 2
H_B = 4
C_B = D_B // H_B
CHUNK = 128
D_C = D_MODEL // 2
H_C = 8
DH_C = D_C // H_C
CONV_W = 4
LRU_C = 8.0
D_D = D_MODEL // 2
H_D = 8
DH_D = D_D // H_D
DILATED = ((128, 1), (512, 4), (2048, 16))
WINDOW_MAX = 2048
ATT_BLOCK = 128
N_BUCKETS = 32
BUCKET_MAX_DIST = WINDOW_MAX
NEG_BIG = -1e30
N_GROUPS = 4
EXP_PER_GROUP = 4
N_EXPERTS = N_GROUPS * EXP_PER_GROUP
TOP_K_INNER = 2
D_EXPERT = 512
NORM_EPS = 1e-6

kernel_name = 'hybrid_rwkv7_gmlp_rglru_dilated_hmoe_step'


def rmsnorm(x, g):
    xf = x.astype(jnp.float32)
    y = xf * lax.rsqrt(jnp.mean(xf * xf, -1, keepdims=True) + NORM_EPS)
    return (y * g.astype(jnp.float32)).astype(x.dtype)


def t5_bucket(dist):
    dist = np.asarray(dist)
    max_exact = N_BUCKETS // 2
    scaled = np.log(np.maximum(dist, 1) / max_exact) / math.log(BUCKET_MAX_DIST / max_exact)
    large = np.minimum(max_exact + (scaled * (N_BUCKETS - max_exact)).astype(np.int32), N_BUCKETS - 1)
    return np.where(dist < max_exact, dist, large).astype(np.int32)


def rwkv7_mix(p, p_prev, wkv0, shift_mu, decay_w0, decay_up, iclr_a0, iclr_up, gate_up,
              key_k, key_a, bonus_r_k, lnx_g, lnx_b):
    B, L, _ = p.shape
    f32 = jnp.float32
    shifted = jnp.concatenate([p_prev[:, None], p[:, :-1]], axis=1)
    xs = p + (shifted - p) * shift_mu
    r, k, v, wd, ad, gd = jnp.split(
        xs, [D_A, 2 * D_A, 3 * D_A, 3 * D_A + R_DECAY, 3 * D_A + R_DECAY + R_ICLR], axis=-1)
    w_log = -jax.nn.softplus(-(decay_w0 + jnp.tanh(wd) @ decay_up).astype(f32)) - 0.5
    decay = jnp.exp(-jnp.exp(w_log))
    a = jax.nn.sigmoid((iclr_a0 + ad @ iclr_up).astype(f32))
    g = (jax.nn.sigmoid(gd) @ gate_up).astype(f32)

    def heads(t):
        return t.reshape(B, L, H_A, DH_A)

    kk = heads(k.astype(f32) * key_k)
    kk = kk / jnp.maximum(jnp.sqrt(jnp.sum(kk * kk, -1, keepdims=True)), 1e-12)
    k = k.astype(f32) * (1.0 + (a - 1.0) * key_a)
    rh, kh, vh, wh, ah = heads(r.astype(f32)), heads(k), heads(v.astype(f32)), heads(decay), heads(a)

    def step(s, inp):
        r_t, w_t, k_t, v_t, kk_t, kka_t = inp
        sa = jnp.einsum('bhvk,bhk->bhv', s, -kk_t)
        s = s * w_t[:, :, None, :] + sa[..., None] * kka_t[:, :, None, :] + v_t[..., None] * k_t[:, :, None, :]
        return s, jnp.einsum('bhvk,bhk->bhv', s, r_t)

    seq = tuple(jnp.moveaxis(t, 1, 0) for t in (rh, wh, kh, vh, kk, kk * ah))
    s_fin, y = lax.scan(step, wkv0.astype(f32), seq)
    y = jnp.moveaxis(y, 0, 1)
    mu = jnp.mean(y, -1, keepdims=True)
    var = jnp.mean(jnp.square(y - mu), -1, keepdims=True)
    y = ((y - mu) * lax.rsqrt(var + GN_EPS)).reshape(B, L, D_A) * lnx_g + lnx_b
    bonus = jnp.sum(rh * kh * bonus_r_k, -1, keepdims=True) * vh
    y = (y + bonus.reshape(B, L, D_A)) * g
    return y.astype(p.dtype), p[:, -1], s_fin.astype(wkv0.dtype)


def chunk_gmlp(u, v, chunk, sgu_norm_g, sgu_norm_b, sgu_w, sgu_b):
    B, L, _ = u.shape
    vf = v.astype(jnp.float32)
    mu = jnp.mean(vf, -1, keepdims=True)
    var = jnp.mean(jnp.square(vf - mu), -1, keepdims=True)
    vn = (vf - mu) * lax.rsqrt(var + NORM_EPS) * sgu_norm_g + sgu_norm_b
    wm = sgu_w[:, :chunk, :chunk] * np.tril(np.ones((chunk, chunk), np.float32))
    vc = vn.reshape(B, L // chunk, chunk, H_B, C_B)
    s = jnp.einsum('hts,bnshc->bnthc', wm, vc) + jnp.transpose(sgu_b[:, :chunk])[None, None, :, :, None]
    out = u.astype(jnp.float32) * s.reshape(B, L, D_B)
    return out.astype(u.dtype), vn.astype(u.dtype)


def even_mixer(h, shift_prev, wkv0, chunk, w_in, w_out, shift_mu, decay_w0, decay_up, iclr_a0, iclr_up,
               gate_up, key_k, key_a, bonus_r_k, lnx_g, lnx_b, sgu_norm_g, sgu_norm_b, sgu_w, sgu_b):
    proj = h @ w_in
    pa, pu, pv = jnp.split(proj, [A_PROJ, A_PROJ + D_B], axis=-1)
    ya, last, wkv = rwkv7_mix(pa, shift_prev, wkv0, shift_mu, decay_w0, decay_up, iclr_a0, iclr_up, gate_up,
                              key_k, key_a, bonus_r_k, lnx_g, lnx_b)
    yb, vn = chunk_gmlp(jax.nn.gelu(pu), jax.nn.gelu(pv), chunk, sgu_norm_g, sgu_norm_b, sgu_w, sgu_b)
    y = jnp.concatenate([ya, yb], axis=-1) @ w_out
    return y, last, wkv, vn


def rglru(xb, conv_prev, h0, pos, conv_w, conv_b, rgate_w, rgate_b, igate_w, igate_b, lru_lambda):
    B, L, _ = xb.shape
    f32 = jnp.float32
    xe = jnp.concatenate([conv_prev, xb], axis=1)
    xc = conv_b + xe[:, 0:L] * conv_w[0]
    for i in range(1, CONV_W):
        xc = xc + xe[:, i:i + L] * conv_w[i]
    xh = xc.reshape(B, L, H_C, DH_C)
    r = jax.nn.sigmoid((jnp.einsum('blhi,hij->blhj', xh, rgate_w).reshape(B, L, D_C) + rgate_b).astype(f32))
    ig = jax.nn.sigmoid((jnp.einsum('blhi,hij->blhj', xh, igate_w).reshape(B, L, D_C) + igate_b).astype(f32))
    log_a = -LRU_C * r * jax.nn.softplus(-lru_lambda.astype(f32))
    a = jnp.exp(log_a)
    mult = jnp.sqrt(-jnp.expm1(2.0 * log_a))
    mult = jnp.where((pos == 0)[None, :, None], 1.0, mult)
    b = mult * ig * xc.astype(f32)
    b = b.at[:, 0].add(a[:, 0] * h0.astype(f32))

    def comb(e1, e2):
        a1, b1 = e1
        a2, b2 = e2
        return a1 * a2, a2 * b1 + b2

    _, hs = lax.associative_scan(comb, (a, b), axis=1)
    return hs, xe[:, -(CONV_W - 1):], hs[:, -1]


def dilated_attn_prompt(q, k, v, rel_bias):
    B, L, H, E = q.shape
    outs, lses = [], []
    for window, dil in DILATED:
        nk = window // dil
        ls = L // dil
        nblk = -(-ls // ATT_BLOCK)
        lp = nblk * ATT_BLOCK

        def split(t):
            t = t.reshape(B, ls, dil, H, E).transpose(0, 2, 1, 3, 4)
            t = jnp.pad(t, ((0, 0), (0, 0), (0, lp - ls), (0, 0), (0, 0)))
            return t.reshape(B, dil, nblk, ATT_BLOCK, H, E)

        def band(t):
            prev = jnp.pad(t[:, :, :-1], ((0, 0), (0, 0), (1, 0), (0, 0), (0, 0), (0, 0)))
            return jnp.concatenate([prev, t], axis=3)

        qb = split(q)
        kw, vw = band(split(k)), band(split(v))
        s = jnp.einsum('bcnqhe,bcnshe->bcnhqs', qb, kw, preferred_element_type=jnp.float32) * (E ** -0.5)
        qi = np.arange(ATT_BLOCK)[:, None]
        si = np.arange(2 * ATT_BLOCK)[None, :]
        delta = qi + ATT_BLOCK - si
        key_sub = np.arange(nblk)[:, None, None] * ATT_BLOCK + si[None] - ATT_BLOCK
        valid = (delta >= 0) & (delta <= nk) & (key_sub >= 0)
        bias = rel_bias[t5_bucket(np.clip(delta, 0, nk) * dil)]
        s = s + jnp.transpose(bias, (2, 0, 1)).astype(jnp.float32)
        s = jnp.where(valid[:, None], s, NEG_BIG)
        m = jnp.max(s, -1, keepdims=True)
        p = jnp.exp(s - m)
        den = jnp.sum(p, -1, keepdims=True)
        o = jnp.einsum('bcnhqs,bcnshe->bcnqhe', p / den, vw.astype(jnp.float32))
        lse = (m + jnp.log(den))[..., 0]
        o = o.reshape(B, dil, lp, H, E)[:, :, :ls].transpose(0, 2, 1, 3, 4).reshape(B, L, H, E)
        lse = lse.transpose(0, 1, 2, 4, 3).reshape(B, dil, lp, H)[:, :, :ls].transpose(0, 2, 1, 3).reshape(B, L, H)
        outs.append(o)
        lses.append(lse)
    alpha = jax.nn.softmax(jnp.stack(lses), axis=0)
    return jnp.einsum('pblh,pblhe->blhe', alpha, jnp.stack(outs))


def dilated_attn_sample(q, k_all, v_all, rel_bias):
    B, S, H, E = q.shape
    W = k_all.shape[1] - S
    outs, lses = [], []
    for window, dil in DILATED:
        nk = window // dil
        m_idx = np.arange(nk + 1)
        idx = W + np.arange(S)[:, None] - m_idx[None] * dil
        valid = idx >= 0
        idx = np.maximum(idx, 0)
        kg, vg = k_all[:, idx], v_all[:, idx]
        s = jnp.einsum('bjhe,bjmhe->bhjm', q, kg, preferred_element_type=jnp.float32) * (E ** -0.5)
        bias = rel_bias[t5_bucket(m_idx * dil)]
        s = s + jnp.transpose(bias)[:, None, :].astype(jnp.float32)
        s = jnp.where(valid[None, None], s, NEG_BIG)
        m = jnp.max(s, -1, keepdims=True)
        p = jnp.exp(s - m)
        den = jnp.sum(p, -1, keepdims=True)
        outs.append(jnp.einsum('bhjm,bjmhe->bjhe', p / den, vg.astype(jnp.float32)))
        lses.append(jnp.transpose((m + jnp.log(den))[..., 0], (0, 2, 1)))
    alpha = jax.nn.softmax(jnp.stack(lses), axis=0)
    return jnp.einsum('pblh,pblhe->blhe', alpha, jnp.stack(outs))


def odd_mixer(h, conv_prev, h0, pos, k_past, v_past, w_in, w_out, conv_w, conv_b, rgate_w, rgate_b,
              igate_w, igate_b, lru_lambda, rel_bias):
    B, L, _ = h.shape
    proj = h @ w_in
    gy, xb, q, k, v = jnp.split(proj, [D_C, 2 * D_C, 2 * D_C + D_D, 2 * D_C + 2 * D_D], axis=-1)
    hc, conv_last, h_last = rglru(xb, conv_prev, h0, pos, conv_w, conv_b, rgate_w, rgate_b,
                                  igate_w, igate_b, lru_lambda)
    yc = jax.nn.gelu(gy.astype(jnp.float32)) * hc
    q = q.reshape(B, L, H_D, DH_D)
    k = k.reshape(B, L, H_D, DH_D)
    v = v.reshape(B, L, H_D, DH_D)
    if k_past is None:
        o = dilated_attn_prompt(q, k, v, rel_bias)
        keep = min(WINDOW_MAX, L)
        k_rows, v_rows = k[:, L - keep:], v[:, L - keep:]
    else:
        o = dilated_attn_sample(q, jnp.concatenate([k_past, k], axis=1), jnp.concatenate([v_past, v], axis=1), rel_bias)
        k_rows, v_rows = k, v
    y = jnp.concatenate([yc, o.reshape(B, L, D_D)], axis=-1).astype(h.dtype) @ w_out
    return y, conv_last, h_last.astype(h.dtype), k_rows, v_rows


def hier_moe(x, router_group_w, router_group_b, router_expert_w, router_expert_b, exp_w_gate, exp_w_up, exp_w_down):
    B, L, D = x.shape
    f32 = jnp.float32
    t = x.reshape(B * L, D)
    pg = jax.nn.softmax((t @ router_group_w + router_group_b).astype(f32), axis=-1)
    top_pg, grp = lax.top_k(pg, 1)
    le = (t @ router_expert_w + router_expert_b).astype(f32).reshape(-1, N_GROUPS, EXP_PER_GROUP)
    le = jnp.take_along_axis(le, grp[:, :, None], axis=1)[:, 0]
    top_le, ei = lax.top_k(le, TOP_K_INNER)
    w2 = jax.nn.softmax(top_le, axis=-1) * top_pg
    eid = grp * EXP_PER_GROUP + ei
    gates = jnp.einsum('tk,tke->te', w2, jax.nn.one_hot(eid, N_EXPERTS, dtype=f32))
    hg = jnp.einsum('td,edf->tef', t, exp_w_gate)
    hu = jnp.einsum('td,edf->tef', t, exp_w_up)
    hid = jax.nn.silu(hg) * hu * gates[..., None].astype(hg.dtype)
    return jnp.einsum('tef,efd->td', hid, exp_w_down).reshape(B, L, D).astype(x.dtype)


def setup_inputs(seed: int = 0) -> dict:
    key = jax.random.key(seed)
    ks = iter(jax.random.split(key, 64))
    f32 = jnp.float32

    def normal(shape, scale):
        return scale * jax.random.normal(next(ks), shape, f32)

    def uniform(shape, lo, hi):
        return jax.random.uniform(next(ks), shape, f32, lo, hi)

    w_buf = min(WINDOW_MAX, PAST_LEN)
    d = D_MODEL
    inp = {}
    inp['x_prompt'] = normal((BATCH, SEQ, d), 1.0)
    inp['x_sample'] = normal((DEC_BATCH, DEC_SEQ, d), 1.0)
    inp['state_wkv'] = normal((N_EVEN, DEC_BATCH, H_A, DH_A, DH_A), 0.3)
    inp['state_shift'] = normal((N_EVEN, DEC_BATCH, A_PROJ), 1.0)
    inp['state_conv'] = normal((N_ODD, DEC_BATCH, CONV_W - 1, D_C), 1.0)
    inp['state_rglru'] = normal((N_ODD, DEC_BATCH, D_C), 0.5)
    inp['cache_k'] = normal((N_ODD, DEC_BATCH, w_buf, H_D, DH_D), 1.0)
    inp['cache_v'] = normal((N_ODD, DEC_BATCH, w_buf, H_D, DH_D), 1.0)
    inp['norm_mix'] = 1.0 + normal((DEPTH, d), 0.02)
    inp['norm_ffn'] = 1.0 + normal((DEPTH, d), 0.02)
    inp['norm_final'] = 1.0 + normal((d,), 0.02)
    inp['w_in_even'] = normal((N_EVEN, d, A_PROJ + 2 * D_B), d ** -0.5)
    inp['w_out_even'] = normal((N_EVEN, D_A + D_B, d), (D_A + D_B) ** -0.5)
    inp['shift_mu'] = uniform((N_EVEN, A_PROJ), 0.0, 1.0)
    inp['decay_w0'] = uniform((N_EVEN, D_A), -6.0, -1.0)
    inp['decay_up'] = normal((N_EVEN, R_DECAY, D_A), 0.5 * R_DECAY ** -0.5)
    inp['iclr_a0'] = normal((N_EVEN, D_A), 0.1)
    inp['iclr_up'] = normal((N_EVEN, R_ICLR, D_A), 0.5 * R_ICLR ** -0.5)
    inp['gate_up'] = normal((N_EVEN, R_GATE, D_A), R_GATE ** -0.5)
    inp['key_k'] = uniform((N_EVEN, D_A), 0.7, 1.0)
    inp['key_a'] = 1.0 + normal((N_EVEN, D_A), 0.05)
    inp['bonus_r_k'] = normal((N_EVEN, H_A, DH_A), 0.1)
    inp['lnx_g'] = 1.0 + normal((N_EVEN, D_A), 0.02)
    inp['lnx_b'] = normal((N_EVEN, D_A), 0.02)
    inp['sgu_norm_g'] = 1.0 + normal((N_EVEN, D_B), 0.02)
    inp['sgu_norm_b'] = normal((N_EVEN, D_B), 0.02)
    inp['sgu_w'] = normal((N_EVEN, H_B, CHUNK, CHUNK), CHUNK ** -0.5)
    inp['sgu_b'] = 1.0 + normal((N_EVEN, H_B, CHUNK), 0.02)
    inp['w_in_odd'] = normal((N_ODD, d, 2 * D_C + 3 * D_D), d ** -0.5)
    inp['w_out_odd'] = normal((N_ODD, D_C + D_D, d), (D_C + D_D) ** -0.5)
    inp['conv_w'] = normal((N_ODD, CONV_W, D_C), CONV_W ** -0.5)
    inp['conv_b'] = normal((N_ODD, D_C), 0.02)
    inp['rgate_w'] = normal((N_ODD, H_C, DH_C, DH_C), DH_C ** -0.5)
    inp['rgate_b'] = normal((N_ODD, D_C), 0.02)
    inp['igate_w'] = normal((N_ODD, H_C, DH_C, DH_C), DH_C ** -0.5)
    inp['igate_b'] = normal((N_ODD, D_C), 0.02)
    a_c = uniform((N_ODD, D_C), 0.9, 0.999)
    a_base = a_c ** (1.0 / LRU_C)
    inp['lru_lambda'] = jnp.log(a_base) - jnp.log1p(-a_base)
    inp['rel_bias'] = normal((N_BUCKETS, H_D), 0.5)
    inp['router_group_w'] = normal((DEPTH, d, N_GROUPS), d ** -0.5)
    inp['router_group_b'] = normal((DEPTH, N_GROUPS), 0.01)
    inp['router_expert_w'] = normal((DEPTH, d, N_EXPERTS), d ** -0.5)
    inp['router_expert_b'] = normal((DEPTH, N_EXPERTS), 0.01)
    inp['exp_w_gate'] = normal((DEPTH, N_EXPERTS, d, D_EXPERT), d ** -0.5)
    inp['exp_w_up'] = normal((DEPTH, N_EXPERTS, d, D_EXPERT), d ** -0.5)
    inp['exp_w_down'] = normal((DEPTH, N_EXPERTS, D_EXPERT, d), D_EXPERT ** -0.5)
    return inp


def reference(x_prompt, x_sample, state_wkv, state_shift, state_conv, state_rglru, cache_k, cache_v,
              norm_mix, norm_ffn, norm_final,
              w_in_even, w_out_even, shift_mu, decay_w0, decay_up, iclr_a0, iclr_up, gate_up, key_k, key_a,
              bonus_r_k, lnx_g, lnx_b, sgu_norm_g, sgu_norm_b, sgu_w, sgu_b,
              w_in_odd, w_out_odd, conv_w, conv_b, rgate_w, rgate_b, igate_w, igate_b, lru_lambda, rel_bias,
              router_group_w, router_group_b, router_expert_w, router_expert_b, exp_w_gate, exp_w_up, exp_w_down):
    B, L = x_prompt.shape[:2]
    DB, S = x_sample.shape[:2]
    dt = x_prompt.dtype
    pos_p = jnp.arange(L)
    pos_s = PAST_LEN + jnp.arange(S)
    xp, xs = x_prompt, x_sample
    wkv_p, shift_p, conv_p, lru_p, k_p, v_p = [], [], [], [], [], []
    wkv_s, shift_s, chunkv_s, conv_s, lru_s, k_s, v_s = [], [], [], [], [], [], []
    for l in range(DEPTH):
        j = l // 2
        hp = rmsnorm(xp, norm_mix[l])
        hs = rmsnorm(xs, norm_mix[l])
        if l % 2 == 0:
            ew = (w_in_even[j], w_out_even[j], shift_mu[j], decay_w0[j], decay_up[j], iclr_a0[j], iclr_up[j],
                  gate_up[j], key_k[j], key_a[j], bonus_r_k[j], lnx_g[j], lnx_b[j],
                  sgu_norm_g[j], sgu_norm_b[j], sgu_w[j], sgu_b[j])
            yp, sh_new, wkv_new, _ = even_mixer(hp, jnp.zeros((B, A_PROJ), dt),
                                                jnp.zeros((B, H_A, DH_A, DH_A), dt), CHUNK, *ew)
            ys, sh_new_s, wkv_new_s, vn_s = even_mixer(hs, state_shift[j], state_wkv[j], S, *ew)
            wkv_p.append(wkv_new)
            shift_p.append(sh_new)
            wkv_s.append(wkv_new_s)
            shift_s.append(sh_new_s)
            chunkv_s.append(vn_s)
        else:
            ow = (w_in_odd[j], w_out_odd[j], conv_w[j], conv_b[j], rgate_w[j], rgate_b[j],
                  igate_w[j], igate_b[j], lru_lambda[j], rel_bias)
            yp, cv, hl, kr, vr = odd_mixer(hp, jnp.zeros((B, CONV_W - 1, D_C), dt), jnp.zeros((B, D_C), dt),
                                           pos_p, None, None, *ow)
            ys, cv_s, hl_s, kr_s, vr_s = odd_mixer(hs, state_conv[j], state_rglru[j], pos_s,
                                                   cache_k[j], cache_v[j], *ow)
            conv_p.append(cv)
            lru_p.append(hl)
            k_p.append(kr)
            v_p.append(vr)
            conv_s.append(cv_s)
            lru_s.append(hl_s)
            k_s.append(kr_s)
            v_s.append(vr_s)
        xp = xp + yp
        xs = xs + ys
        mw = (router_group_w[l], router_group_b[l], router_expert_w[l], router_expert_b[l],
              exp_w_gate[l], exp_w_up[l], exp_w_down[l])
        xp = xp + hier_moe(rmsnorm(xp, norm_ffn[l]), *mw)
        xs = xs + hier_moe(rmsnorm(xs, norm_ffn[l]), *mw)
    y_prompt = rmsnorm(xp, norm_final)
    y_sample = rmsnorm(xs, norm_final)
    return (y_prompt, y_sample,
            jnp.stack(wkv_p), jnp.stack(shift_p), jnp.stack(conv_p), jnp.stack(lru_p), jnp.stack(k_p), jnp.stack(v_p),
            jnp.stack(wkv_s), jnp.stack(shift_s), jnp.stack(chunkv_s), jnp.stack(conv_s), jnp.stack(lru_s),
            jnp.stack(k_s), jnp.stack(v_s))
```

```python
import functools
import math

import numpy as np
import jax
import jax.numpy as jnp
from jax import lax
from jax.experimental import pallas as pl
from jax.experimental.pallas import tpu as pltpu

F32 = jnp.float32
BF16 = jnp.bfloat16
HI = lax.Precision.HIGHEST

PAST_LEN = 8192
DH_A = 64
R_DECAY = 64
R_ICLR = 64
R_GATE = 128
GN_EPS = 64e-5
H_B = 4
H_C = 8
CONV_W = 4
LRU_C = 8.0
H_D = 8
DILATED = ((128, 1), (512, 4), (2048, 16))
N_BUCKETS = 32
BUCKET_MAX_DIST = 2048
NEG_BIG = -1e30
N_GROUPS = 4
EXP_PER_GROUP = 4
NORM_EPS = 1e-6

VMEM_LIMIT = 56 * 1024 * 1024
RWKV_CHUNK = 64
ATT_TILE = 128
LANES = 128


def _cparams(sem):
    return pltpu.CompilerParams(dimension_semantics=sem, vmem_limit_bytes=VMEM_LIMIT)


def _dot(a, b, precision=None):
    return jnp.dot(a, b, preferred_element_type=F32, precision=precision)


def _dot_nt(a, b, precision=None):
    return lax.dot_general(a, b, (((1,), (1,)), ((), ())), preferred_element_type=F32, precision=precision)


def _dot_tn(a, b, precision=None):
    return lax.dot_general(a, b, (((0,), (0,)), ((), ())), preferred_element_type=F32, precision=precision)


def _softplus(x):
    return jnp.maximum(x, 0.0) + jnp.log(1.0 + jnp.exp(-jnp.abs(x)))


def _sigmoid(x):
    return 1.0 / (1.0 + jnp.exp(-x))


def _gelu(x):
    c = math.sqrt(2.0 / math.pi)
    return 0.5 * x * (1.0 + jnp.tanh(c * (x + 0.044715 * (x * x * x))))


def _row_tile(t, pref=512):
    return pref if t % pref == 0 else t


def _norm_matmul_kernel(x_ref, g_ref, w_ref, *out_refs, splits):
    x = x_ref[...]
    ms = jnp.mean(x * x, axis=-1, keepdims=True)
    h = (x * lax.rsqrt(ms + NORM_EPS) * g_ref[...]).astype(BF16)
    off = 0
    for o_ref, n in zip(out_refs, splits):
        o_ref[...] = _dot(h, w_ref[:, off:off + n])
        off += n


def norm_matmul(x, g, w_bf16, splits):
    t, d = x.shape
    n = w_bf16.shape[1]
    tm = _row_tile(t)
    return pl.pallas_call(
        functools.partial(_norm_matmul_kernel, splits=splits),
        out_shape=[jax.ShapeDtypeStruct((t, s), F32) for s in splits],
        grid=(t // tm,),
        in_specs=[pl.BlockSpec((tm, d), lambda i: (i, 0)),
                  pl.BlockSpec((1, d), lambda i: (0, 0)),
                  pl.BlockSpec((d, n), lambda i: (0, 0))],
        out_specs=[pl.BlockSpec((tm, s), lambda i: (i, 0)) for s in splits],
        compiler_params=_cparams(("parallel",)),
        name="norm_matmul",
    )(x, g.reshape(1, d), w_bf16)


def _proj_res_kernel(x_ref, a_ref, b_ref, wa_ref, wb_ref, o_ref):
    acc = _dot(a_ref[...].astype(BF16), wa_ref[...]) + _dot(b_ref[...].astype(BF16), wb_ref[...])
    o_ref[...] = x_ref[...] + acc


def proj_residual(x, a, b, wa, wb):
    t, d = x.shape
    tm = _row_tile(t)
    ka, kb = a.shape[1], b.shape[1]
    return pl.pallas_call(
        _proj_res_kernel,
        out_shape=jax.ShapeDtypeStruct((t, d), F32),
        grid=(t // tm,),
        in_specs=[pl.BlockSpec((tm, d), lambda i: (i, 0)),
                  pl.BlockSpec((tm, ka), lambda i: (i, 0)),
                  pl.BlockSpec((tm, kb), lambda i: (i, 0)),
                  pl.BlockSpec((ka, d), lambda i: (0, 0)),
                  pl.BlockSpec((kb, d), lambda i: (0, 0))],
        out_specs=pl.BlockSpec((tm, d), lambda i: (i, 0)),
        compiler_params=_cparams(("parallel",)),
        name="proj_residual",
    )(x, a, b, wa, wb)


def _rmsnorm_kernel(x_ref, g_ref, o_ref):
    x = x_ref[...]
    ms = jnp.mean(x * x, axis=-1, keepdims=True)
    o_ref[...] = x * lax.rsqrt(ms + NORM_EPS) * g_ref[...]


def rmsnorm_call(x, g):
    t, d = x.shape
    tm = _row_tile(t)
    return pl.pallas_call(
        _rmsnorm_kernel,
        out_shape=jax.ShapeDtypeStruct((t, d), F32),
        grid=(t // tm,),
        in_specs=[pl.BlockSpec((tm, d), lambda i: (i, 0)), pl.BlockSpec((1, d), lambda i: (0, 0))],
        out_specs=pl.BlockSpec((tm, d), lambda i: (i, 0)),
        compiler_params=_cparams(("parallel",)),
        name="final_rmsnorm",
    )(x, g.reshape(1, d))


ROUTER_LANES = 128


def _moe_kernel(x_ref, g_ref, rw_ref, rb_ref, wg_ref, wu_ref, wd_ref, o_ref, xn_scr, gate_scr, acc_scr, *, n_exp):
    e = pl.program_id(1)
    tm = x_ref.shape[0]
    lane = lax.broadcasted_iota(jnp.int32, (tm, ROUTER_LANES), 1)

    @pl.when(e == 0)
    def _():
        x = x_ref[...]
        ms = jnp.mean(x * x, axis=-1, keepdims=True)
        xn = x * lax.rsqrt(ms + NORM_EPS) * g_ref[...]
        xn_scr[...] = xn.astype(BF16)
        logits = _dot(xn, rw_ref[...], HI) + rb_ref[...]
        lg = jnp.where(lane < N_GROUPS, logits, -jnp.inf)
        gm = jnp.max(lg, axis=-1, keepdims=True)
        top_pg = 1.0 / jnp.sum(jnp.exp(lg - gm), axis=-1, keepdims=True)
        grp = jnp.min(jnp.where(lg == gm, lane, ROUTER_LANES), axis=-1, keepdims=True)
        in_grp = (lane >= N_GROUPS) & (lane < N_GROUPS + n_exp) & (((lane - N_GROUPS) >> 2) == grp)
        le = jnp.where(in_grp, logits, -jnp.inf)
        t1 = jnp.max(le, axis=-1, keepdims=True)
        i1 = jnp.min(jnp.where(le == t1, lane, ROUTER_LANES), axis=-1, keepdims=True)
        le2 = jnp.where(lane == i1, -jnp.inf, le)
        t2 = jnp.max(le2, axis=-1, keepdims=True)
        i2 = jnp.min(jnp.where(le2 == t2, lane, ROUTER_LANES), axis=-1, keepdims=True)
        ex = jnp.exp(t2 - t1)
        w1 = 1.0 / (1.0 + ex)
        w2 = ex * w1
        gate_scr[...] = jnp.where(lane == i1, w1 * top_pg, 0.0) + jnp.where(lane == i2, w2 * top_pg, 0.0)
        acc_scr[...] = jnp.zeros_like(acc_scr)

    xn = xn_scr[...]
    hg = _dot(xn, wg_ref[0])
    hu = _dot(xn, wu_ref[0])
    gcol = jnp.sum(jnp.where(lane == e + N_GROUPS, gate_scr[...], 0.0), axis=-1, keepdims=True)
    hid = hg * _sigmoid(hg) * hu * gcol
    acc_scr[...] += _dot(hid.astype(BF16), wd_ref[0])

    @pl.when(e == n_exp - 1)
    def _():
        o_ref[...] = x_ref[...] + acc_scr[...]


def moe_layer(x, g, rw, rb, wg, wu, wd):
    t, d = x.shape
    n_exp, _, f = wg.shape
    tm = _row_tile(t)
    return pl.pallas_call(
        functools.partial(_moe_kernel, n_exp=n_exp),
        out_shape=jax.ShapeDtypeStruct((t, d), F32),
        grid=(t // tm, n_exp),
        in_specs=[pl.BlockSpec((tm, d), lambda i, e: (i, 0)),
                  pl.BlockSpec((1, d), lambda i, e: (0, 0)),
                  pl.BlockSpec((d, ROUTER_LANES), lambda i, e: (0, 0)),
                  pl.BlockSpec((1, ROUTER_LANES), lambda i, e: (0, 0)),
                  pl.BlockSpec((1, d, f), lambda i, e: (e, 0, 0)),
                  pl.BlockSpec((1, d, f), lambda i, e: (e, 0, 0)),
                  pl.BlockSpec((1, f, d), lambda i, e: (e, 0, 0))],
        out_specs=pl.BlockSpec((tm, d), lambda i, e: (i, 0)),
        scratch_shapes=[pltpu.VMEM((tm, d), BF16), pltpu.VMEM((tm, ROUTER_LANES), F32), pltpu.VMEM((tm, d), F32)],
        compiler_params=_cparams(("parallel", "arbitrary")),
        name="hier_moe",
    )(x, g.reshape(1, d), rw, rb, wg, wu, wd)


def _rwkv_kernel(p_ref, prev_ref, s0_ref, mu_ref, w0_ref, wd_ref, a0_ref, wa_ref, wg_ref, kk_ref, ka_ref,
                 bonus_ref, lng_ref, lnb_ref, tri_ref, hsum_ref, ya_ref, sf_ref, s_scr, prev_scr,
                 *, C, H, DH, n_chunks):
    c = pl.program_id(1)

    @pl.when(c == 0)
    def _():
        s_scr[...] = s0_ref[0]
        prev_scr[...] = prev_ref[0]

    DA = H * DH
    p = p_ref[...]
    row = lax.broadcasted_iota(jnp.int32, p.shape, 0)
    shifted = jnp.where(row == 0, prev_scr[...], pltpu.roll(p, 1, axis=0))
    prev_scr[...] = p[C - 1:C, :]
    xs = p + (shifted - p) * mu_ref[...]
    r = xs[:, 0:DA]
    k = xs[:, DA:2 * DA]
    v = xs[:, 2 * DA:3 * DA]
    lora = xs[:, 3 * DA:3 * DA + R_DECAY + R_ICLR]
    gd = xs[:, 3 * DA + R_DECAY + R_ICLR:3 * DA + R_DECAY + R_ICLR + R_GATE]

    w_log = -_softplus(-(w0_ref[...] + _dot(jnp.tanh(lora), wd_ref[...], HI))) - 0.5
    lw = -jnp.exp(w_log)
    a = _sigmoid(a0_ref[...] + _dot(lora, wa_ref[...], HI))
    g = _dot(_sigmoid(gd), wg_ref[...], HI)

    kk = k * kk_ref[...]
    ss = _dot(kk * kk, hsum_ref[...], HI)
    kk = kk / jnp.maximum(jnp.sqrt(ss), 1e-12)
    k2 = k * (1.0 + (a - 1.0) * ka_ref[...])
    kka = kk * a

    cum = _dot(tri_ref[...], lw, HI)
    p_in = jnp.exp(cum)
    r_t = r * p_in
    a_t = kk * jnp.exp(cum - lw)
    p_inv = jnp.exp(-cum)
    b_t = kka * p_inv
    k_t = k2 * p_inv
    p_tot = p_in[C - 1:C, :]
    bonus = _dot(r * k2 * bonus_ref[...], hsum_ref[...], HI) * v

    ri = lax.broadcasted_iota(jnp.int32, (C, C), 0)
    ci = lax.broadcasted_iota(jnp.int32, (C, C), 1)
    strict = ri > ci
    incl = ri >= ci
    eye = (ri == ci).astype(F32)
    n_double = max(int(math.ceil(math.log2(C))) - 1, 0)

    ys = []
    for h in range(H):
        sl = slice(h * DH, (h + 1) * DH)
        A, Bt, Kt, Rt, V = a_t[:, sl], b_t[:, sl], k_t[:, sl], r_t[:, sl], v[:, sl]
        S0 = s_scr[h]
        Lm = jnp.where(strict, _dot_nt(A, Bt, HI), 0.0)
        Mm = jnp.where(strict, _dot_nt(A, Kt, HI), 0.0)
        Gb = jnp.where(incl, _dot_nt(Rt, Bt, HI), 0.0)
        Gk = jnp.where(incl, _dot_nt(Rt, Kt, HI), 0.0)
        T = eye - Lm
        Pw = Lm
        for _ in range(n_double):
            Pw = _dot(Pw, Pw, HI)
            T = T + _dot(T, Pw, HI)
        rhs = -(_dot_nt(A, S0, HI) + _dot(Mm, V, HI))
        U = _dot(T, rhs, HI)
        Y = _dot_nt(Rt, S0, HI) + _dot(Gb, U, HI) + _dot(Gk, V, HI)
        s_scr[h] = (S0 + _dot_tn(U, Bt, HI) + _dot_tn(V, Kt, HI)) * p_tot[:, sl]
        mu_y = jnp.mean(Y, axis=-1, keepdims=True)
        yc = Y - mu_y
        var = jnp.mean(yc * yc, axis=-1, keepdims=True)
        ys.append(yc * lax.rsqrt(var + GN_EPS))
    y = jnp.concatenate(ys, axis=-1) * lng_ref[...] + lnb_ref[...]
    ya_ref[...] = (y + bonus) * g

    @pl.when(c == n_chunks - 1)
    def _():
        sf_ref[0] = s_scr[...]


def rwkv_mix(pa, n_batch, seq, shift_prev, wkv0, wts):
    t, ap = pa.shape
    H = wkv0.shape[1]
    DA = H * DH_A
    C = min(RWKV_CHUNK, seq)
    n_chunks = seq // C
    tri = jnp.asarray(np.tril(np.ones((C, C), np.float32)))
    hsum = jnp.asarray(np.kron(np.eye(H, dtype=np.float32), np.ones((DH_A, DH_A), np.float32)))

    def full(shape):
        nd = len(shape)
        return pl.BlockSpec(shape, lambda b, c: (0,) * nd)

    vec = full((1, DA))
    ya, s_fin = pl.pallas_call(
        functools.partial(_rwkv_kernel, C=C, H=H, DH=DH_A, n_chunks=n_chunks),
        out_shape=[jax.ShapeDtypeStruct((t, DA), F32), jax.ShapeDtypeStruct(wkv0.shape, F32)],
        grid=(n_batch, n_chunks),
        in_specs=[pl.BlockSpec((C, ap), lambda b, c: (b * n_chunks + c, 0)),
                  pl.BlockSpec((1, 1, ap), lambda b, c: (b, 0, 0)),
                  pl.BlockSpec((1, H, DH_A, DH_A), lambda b, c: (b, 0, 0, 0)),
                  full((1, ap)), vec, full((R_DECAY + R_ICLR, DA)), vec, full((R_DECAY + R_ICLR, DA)),
                  full((R_GATE, DA)), vec, vec, vec, vec, vec, full((C, C)), full((DA, DA))],
        out_specs=[pl.BlockSpec((C, DA), lambda b, c: (b * n_chunks + c, 0)),
                   pl.BlockSpec((1, H, DH_A, DH_A), lambda b, c: (b, 0, 0, 0))],
        scratch_shapes=[pltpu.VMEM((H, DH_A, DH_A), F32), pltpu.VMEM((1, ap), F32)],
        compiler_params=_cparams(("parallel", "arbitrary")),
        name="rwkv7_mix",
    )(pa, shift_prev.reshape(n_batch, 1, ap), wkv0, wts["mu"], wts["w0"], wts["wd"], wts["a0"], wts["wa"],
      wts["wg"], wts["key_k"], wts["key_a"], wts["bonus"], wts["lnx_g"], wts["lnx_b"], tri, hsum)
    return ya, s_fin


GMLP_TILE = 128


def _gmlp_kernel(u_ref, v_ref, ng_ref, nb_ref, wm_ref, bias_ref, o_ref, vn_ref):
    vf = _gelu(v_ref[...])
    mu = jnp.mean(vf, axis=-1, keepdims=True)
    vc = vf - mu
    var = jnp.mean(vc * vc, axis=-1, keepdims=True)
    vn = vc * lax.rsqrt(var + NORM_EPS) * ng_ref[...] + nb_ref[...]
    vn_ref[...] = vn
    vb = vn.astype(BF16)
    n_h = wm_ref.shape[0]
    cb = vn.shape[1] // n_h
    s = jnp.concatenate([_dot(wm_ref[h], vb[:, h * cb:(h + 1) * cb]) for h in range(n_h)], axis=-1)
    o_ref[...] = _gelu(u_ref[...]) * (s + bias_ref[...])


def gmlp_mix(pu, pv, ng, nb, wm_bf16, bias_tile):
    t, db = pu.shape
    n_h = wm_bf16.shape[0]
    return pl.pallas_call(
        _gmlp_kernel,
        out_shape=[jax.ShapeDtypeStruct((t, db), F32), jax.ShapeDtypeStruct((t, db), F32)],
        grid=(t // GMLP_TILE,),
        in_specs=[pl.BlockSpec((GMLP_TILE, db), lambda i: (i, 0)),
                  pl.BlockSpec((GMLP_TILE, db), lambda i: (i, 0)),
                  pl.BlockSpec((1, db), lambda i: (0, 0)),
                  pl.BlockSpec((1, db), lambda i: (0, 0)),
                  pl.BlockSpec((n_h, GMLP_TILE, GMLP_TILE), lambda i: (0, 0, 0)),
                  pl.BlockSpec((GMLP_TILE, db), lambda i: (0, 0))],
        out_specs=[pl.BlockSpec((GMLP_TILE, db), lambda i: (i, 0)),
                   pl.BlockSpec((GMLP_TILE, db), lambda i: (i, 0))],
        compiler_params=_cparams(("parallel",)),
        name="gmlp_mix",
    )(pu, pv, ng, nb, wm_bf16, bias_tile)


N_SEG = 8


def _rglru_kernel(xb_ref, gy_ref, cprev_ref, h0_ref, cw_ref, cb_ref, gw_ref, gb_ref, lam_ref,
                  yc_ref, ctail_ref, hl_ref, xe_scr, a_scr, b_scr, h_scr, *, TL, DC, pos0, n_tiles):
    l = pl.program_id(1)
    PAD = 8

    @pl.when(l == 0)
    def _():
        xe_scr[0:PAD, :] = cprev_ref[0]
        h_scr[...] = h0_ref[0]

    xe_scr[PAD:PAD + TL, :] = xb_ref[...]
    xc = cb_ref[...] + xe_scr[pl.ds(PAD - (CONV_W - 1), TL), :] * cw_ref[0:1, :]
    for i in range(1, CONV_W):
        xc = xc + xe_scr[pl.ds(PAD - (CONV_W - 1) + i, TL), :] * cw_ref[i:i + 1, :]
    tail = xe_scr[TL:TL + PAD, :]
    ctail_ref[0] = tail
    xe_scr[0:PAD, :] = tail

    gates = _dot(xc.astype(BF16), gw_ref[...]) + gb_ref[...]
    rg = _sigmoid(gates[:, 0:DC])
    ig = _sigmoid(gates[:, DC:2 * DC])
    log_a = -LRU_C * rg * _softplus(-lam_ref[...])
    a = jnp.exp(log_a)
    mult = jnp.sqrt(1.0 - jnp.exp(2.0 * log_a))
    row = lax.broadcasted_iota(jnp.int32, (TL, DC), 0)
    mult = jnp.where(row + (l * TL + pos0) == 0, 1.0, mult)
    b = mult * ig * xc
    n_slab = DC // LANES
    for s in range(n_slab):
        a_scr[s] = a[:, s * LANES:(s + 1) * LANES]
        b_scr[s] = b[:, s * LANES:(s + 1) * LANES]

    seg = TL // N_SEG

    def step(i, carry):
        idx = pl.ds(i, N_SEG, stride=seg) if seg > 1 else pl.ds(0, N_SEG)
        out = []
        for s in range(n_slab):
            hloc, ap = carry[s]
            ai = a_scr[s, idx, :]
            hloc = ai * hloc + b_scr[s, idx, :]
            ap = ap * ai
            b_scr[s, idx, :] = hloc
            a_scr[s, idx, :] = ap
            out.append((hloc, ap))
        return tuple(out)

    lax.fori_loop(0, seg, step,
                  tuple((jnp.zeros((N_SEG, LANES), F32), jnp.ones((N_SEG, LANES), F32)) for _ in range(n_slab)))

    carry = h_scr[...]
    g_act = _gelu(gy_ref[...])
    for j in range(N_SEG):
        rows = slice(j * seg, (j + 1) * seg)
        hloc = jnp.concatenate([b_scr[s, rows, :] for s in range(n_slab)], axis=-1)
        ap = jnp.concatenate([a_scr[s, rows, :] for s in range(n_slab)], axis=-1)
        hj = hloc + ap * carry
        yc_ref[rows, :] = g_act[rows, :] * hj
        carry = hj[seg - 1:seg, :]
    h_scr[...] = carry

    @pl.when(l == n_tiles - 1)
    def _():
        hl_ref[0] = carry


def rglru_mix(xb, gy, n_batch, seq, conv_prev8, h0, pos0, wts):
    t, dc = xb.shape
    TL = 512 if seq % 512 == 0 else seq
    n_tiles = seq // TL

    def full(shape):
        nd = len(shape)
        return pl.BlockSpec(shape, lambda b, l: (0,) * nd)

    yc, ctail, hl = pl.pallas_call(
        functools.partial(_rglru_kernel, TL=TL, DC=dc, pos0=pos0, n_tiles=n_tiles),
        out_shape=[jax.ShapeDtypeStruct((t, dc), F32), jax.ShapeDtypeStruct((n_batch, 8, dc), F32),
                   jax.ShapeDtypeStruct((n_batch, 1, dc), F32)],
        grid=(n_batch, n_tiles),
        in_specs=[pl.BlockSpec((TL, dc), lambda b, l: (b * n_tiles + l, 0)),
                  pl.BlockSpec((TL, dc), lambda b, l: (b * n_tiles + l, 0)),
                  pl.BlockSpec((1, 8, dc), lambda b, l: (b, 0, 0)),
                  pl.BlockSpec((1, 1, dc), lambda b, l: (b, 0, 0)),
                  full((CONV_W, dc)), full((1, dc)), full((dc, 2 * dc)), full((1, 2 * dc)), full((1, dc))],
        out_specs=[pl.BlockSpec((TL, dc), lambda b, l: (b * n_tiles + l, 0)),
                   pl.BlockSpec((1, 8, dc), lambda b, l: (b, 0, 0)),
                   pl.BlockSpec((1, 1, dc), lambda b, l: (b, 0, 0))],
        scratch_shapes=[pltpu.VMEM((TL + 8, dc), F32), pltpu.VMEM((dc // LANES, TL, LANES), F32),
                        pltpu.VMEM((dc // LANES, TL, LANES), F32), pltpu.VMEM((1, dc), F32)],
        compiler_params=_cparams(("parallel", "arbitrary")),
        name="rglru_mix",
    )(xb, gy, conv_prev8, h0.reshape(n_batch, 1, dc), wts["conv_w"], wts["conv_b"], wts["gate_w"], wts["gate_b"],
      wts["lam"])
    return yc, ctail[:, 8 - (CONV_W - 1):, :], hl.reshape(n_batch, dc)


def _t5_bucket(dist):
    dist = np.asarray(dist)
    max_exact = N_BUCKETS // 2
    scaled = np.log(np.maximum(dist, 1) / max_exact) / math.log(BUCKET_MAX_DIST / max_exact)
    large = np.minimum(max_exact + (scaled * (N_BUCKETS - max_exact)).astype(np.int32), N_BUCKETS - 1)
    return np.where(dist < max_exact, dist, large).astype(np.int32)


def _dist_table(rel_bias, max_dist):
    dist = np.arange(max_dist + 1)
    count = np.zeros(max_dist + 1, np.float32)
    for window, dil in DILATED:
        count += ((dist % dil == 0) & (dist <= window)).astype(np.float32)
    logcnt = np.where(count > 0, np.log(np.maximum(count, 1.0)), 0.0).astype(np.float32)
    tab = jnp.take(rel_bias, jnp.asarray(_t5_bucket(dist)), axis=0) + jnp.asarray(logcnt)[:, None]
    return jnp.where(jnp.asarray(count > 0)[:, None], tab, NEG_BIG)


def _attn_prompt_kernel(q_ref, k_ref, v_ref, bias_ref, o_ref, *, E):
    qi = pl.program_id(2)
    TQ = q_ref.shape[1]
    lane = lax.broadcasted_iota(jnp.int32, (TQ, 2 * E), 1)
    q = q_ref[0] * (E ** -0.5)
    q2 = jnp.concatenate([jnp.where(lane < E, q, 0.0), jnp.where(lane >= E, q, 0.0)], axis=0).astype(BF16)

    def body(i, carry):
        m, l, acc = carry
        j = qi - i
        kj = k_ref[0, pl.ds(pl.multiple_of(j * TQ, TQ), TQ), :].astype(BF16)
        vj = v_ref[0, pl.ds(pl.multiple_of(j * TQ, TQ), TQ), :].astype(BF16)
        s = _dot_nt(q2, kj) + jnp.concatenate([bias_ref[0, i], bias_ref[1, i]], axis=0)
        m_new = jnp.maximum(m, jnp.max(s, axis=-1, keepdims=True))
        alpha = jnp.exp(m - m_new)
        pexp = jnp.exp(s - m_new)
        l = alpha * l + jnp.sum(pexp, axis=-1, keepdims=True)
        acc = alpha * acc + _dot(pexp.astype(BF16), vj)
        return m_new, l, acc

    init = (jnp.full((2 * TQ, 1), NEG_BIG, F32), jnp.zeros((2 * TQ, 1), F32), jnp.zeros((2 * TQ, 2 * E), F32))
    m, l, acc = lax.fori_loop(0, qi + 1, body, init)
    o = acc / l
    o_ref[0] = jnp.where(lane < E, o[0:TQ], o[TQ:2 * TQ])


def attn_prompt(q, k, v, bias_tiles, n_batch, seq):
    hd = q.shape[-1]
    E = hd // H_D
    T = ATT_TILE
    nq = seq // T
    return pl.pallas_call(
        functools.partial(_attn_prompt_kernel, E=E),
        out_shape=jax.ShapeDtypeStruct((n_batch, seq, hd), F32),
        grid=(H_D // 2, n_batch, nq),
        in_specs=[pl.BlockSpec((1, T, 2 * E), lambda hp, b, i: (b, i, hp)),
                  pl.BlockSpec((1, seq, 2 * E), lambda hp, b, i: (b, 0, hp)),
                  pl.BlockSpec((1, seq, 2 * E), lambda hp, b, i: (b, 0, hp)),
                  pl.BlockSpec((2, nq, T, T), lambda hp, b, i: (hp, 0, 0, 0))],
        out_specs=pl.BlockSpec((1, T, 2 * E), lambda hp, b, i: (b, i, hp)),
        compiler_params=_cparams(("parallel", "parallel", "arbitrary")),
        name="dilated_attn_prompt",
    )(q, k, v, bias_tiles)


def _attn_sample_kernel(q_ref, kn_ref, vn_ref, ck_ref, cv_ref, bo_ref, bn_ref, o_ref, *, E, S):
    lane = lax.broadcasted_iota(jnp.int32, (S, 2 * E), 1)
    NPAD = bn_ref.shape[-1]
    outs = []
    for hp in range(H_D // 2):
        sl = slice(hp * 2 * E, (hp + 1) * 2 * E)
        q = q_ref[0, :, sl] * (E ** -0.5)
        q2 = jnp.concatenate([jnp.where(lane < E, q, 0.0), jnp.where(lane >= E, q, 0.0)], axis=0).astype(BF16)
        zpad = jnp.zeros((NPAD - S, 2 * E), F32)
        kn = jnp.concatenate([kn_ref[0, :, sl], zpad], axis=0).astype(BF16)
        vn = jnp.concatenate([vn_ref[0, :, sl], zpad], axis=0).astype(BF16)
        s_old = _dot_nt(q2, ck_ref[0, :, sl].astype(BF16)) + jnp.concatenate([bo_ref[2 * hp], bo_ref[2 * hp + 1]], axis=0)
        s_new = _dot_nt(q2, kn) + jnp.concatenate([bn_ref[2 * hp], bn_ref[2 * hp + 1]], axis=0)
        m = jnp.maximum(jnp.max(s_old, axis=-1, keepdims=True), jnp.max(s_new, axis=-1, keepdims=True))
        p_old = jnp.exp(s_old - m)
        p_new = jnp.exp(s_new - m)
        l = jnp.sum(p_old, axis=-1, keepdims=True) + jnp.sum(p_new, axis=-1, keepdims=True)
        acc = _dot(p_old.astype(BF16), cv_ref[0, :, sl].astype(BF16)) + _dot(p_new.astype(BF16), vn)
        o = acc / l
        outs.append(jnp.where(lane < E, o[0:S], o[S:2 * S]))
    o_ref[0] = jnp.concatenate(outs, axis=-1)


def attn_sample(q, k_new, v_new, cache_k, cache_v, bias_old, bias_new):
    n_batch, S, hd = q.shape
    W = cache_k.shape[1]
    E = hd // H_D
    NPAD = bias_new.shape[-1]
    return pl.pallas_call(
        functools.partial(_attn_sample_kernel, E=E, S=S),
        out_shape=jax.ShapeDtypeStruct((n_batch, S, hd), F32),
        grid=(n_batch,),
        in_specs=[pl.BlockSpec((1, S, hd), lambda b: (b, 0, 0)),
                  pl.BlockSpec((1, S, hd), lambda b: (b, 0, 0)),
                  pl.BlockSpec((1, S, hd), lambda b: (b, 0, 0)),
                  pl.BlockSpec((1, W, hd), lambda b: (b, 0, 0)),
                  pl.BlockSpec((1, W, hd), lambda b: (b, 0, 0)),
                  pl.BlockSpec((H_D, S, W), lambda b: (0, 0, 0)),
                  pl.BlockSpec((H_D, S, NPAD), lambda b: (0, 0, 0))],
        out_specs=pl.BlockSpec((1, S, hd), lambda b: (b, 0, 0)),
        compiler_params=_cparams(("parallel",)),
        name="dilated_attn_sample",
    )(q, k_new, v_new, cache_k, cache_v, bias_old, bias_new)


def _even_weights(j, w_in_even, w_out_even, shift_mu, decay_w0, decay_up, iclr_a0, iclr_up, gate_up, key_k, key_a,
                  bonus_r_k, lnx_g, lnx_b, sgu_norm_g, sgu_norm_b, sgu_w, sgu_b):
    da = decay_w0.shape[1]
    zeros_d = jnp.zeros((R_ICLR, da), F32)
    zeros_i = jnp.zeros((R_DECAY, da), F32)
    return dict(
        w_in=w_in_even[j].astype(BF16),
        w_out_a=w_out_even[j, :da].astype(BF16), w_out_b=w_out_even[j, da:].astype(BF16),
        mu=shift_mu[j].reshape(1, -1), w0=decay_w0[j].reshape(1, -1), a0=iclr_a0[j].reshape(1, -1),
        wd=jnp.concatenate([decay_up[j], zeros_d], axis=0), wa=jnp.concatenate([zeros_i, iclr_up[j]], axis=0),
        wg=gate_up[j], key_k=key_k[j].reshape(1, -1), key_a=key_a[j].reshape(1, -1),
        bonus=bonus_r_k[j].reshape(1, -1), lnx_g=lnx_g[j].reshape(1, -1), lnx_b=lnx_b[j].reshape(1, -1),
        ng=sgu_norm_g[j].reshape(1, -1), nb=sgu_norm_b[j].reshape(1, -1), sgu_w=sgu_w[j], sgu_b=sgu_b[j])


def _gmlp_tables(sgu_w, sgu_b, chunk):
    reps = GMLP_TILE // chunk
    n_h = sgu_w.shape[0]
    cb = None
    wm = sgu_w[:, :chunk, :chunk] * jnp.asarray(np.tril(np.ones((chunk, chunk), np.float32)))
    if reps > 1:
        eye = jnp.asarray(np.eye(reps, dtype=np.float32))
        wm = jnp.einsum("ab,hts->hatbs", eye, wm).reshape(n_h, GMLP_TILE, GMLP_TILE)
    bias = jnp.tile(jnp.transpose(sgu_b[:, :chunk]), (reps, 1))
    return wm.astype(BF16), bias


def _even_layer(x, n_batch, seq, chunk, shift_prev, wkv0, norm_g, ew):
    pa, pu, pv = norm_matmul(x, norm_g, ew["w_in"], (ew["mu"].shape[1], ew["ng"].shape[1], ew["ng"].shape[1]))
    ya, wkv = rwkv_mix(pa, n_batch, seq, shift_prev, wkv0, ew)
    wm, bias = _gmlp_tables(ew["sgu_w"], ew["sgu_b"], chunk)
    cb = pu.shape[1] // wm.shape[0]
    bias_tile = jnp.repeat(bias, cb, axis=1)
    yb, vn = gmlp_mix(pu, pv, ew["ng"], ew["nb"], wm, bias_tile)
    x = proj_residual(x, ya, yb, ew["w_out_a"], ew["w_out_b"])
    last = pa.reshape(n_batch, seq, -1)[:, -1]
    return x, last, wkv, vn


def _odd_weights(j, w_in_odd, w_out_odd, conv_w, conv_b, rgate_w, rgate_b, igate_w, igate_b, lru_lambda):
    dc = conv_b.shape[1]
    eye = jnp.asarray(np.eye(H_C, dtype=np.float32))

    def blockdiag(w):
        dh = w.shape[-1]
        return jnp.einsum("ab,aij->aibj", eye, w).reshape(H_C * dh, H_C * dh)

    return dict(
        w_in=w_in_odd[j].astype(BF16),
        w_out_c=w_out_odd[j, :dc].astype(BF16), w_out_d=w_out_odd[j, dc:].astype(BF16),
        conv_w=conv_w[j], conv_b=conv_b[j].reshape(1, -1),
        gate_w=jnp.concatenate([blockdiag(rgate_w[j]), blockdiag(igate_w[j])], axis=1).astype(BF16),
        gate_b=jnp.concatenate([rgate_b[j], igate_b[j]]).reshape(1, -1),
        lam=lru_lambda[j].reshape(1, -1))


def _odd_layer(x, n_batch, seq, conv_prev, h0, pos0, caches, dist_tab, norm_g, ow):
    dc = ow["lam"].shape[1]
    gy, xb, q, k, v = norm_matmul(x, norm_g, ow["w_in"], (dc,) * 5)
    conv_prev8 = jnp.pad(conv_prev, ((0, 0), (8 - (CONV_W - 1), 0), (0, 0)))
    yc, conv_last, h_last = rglru_mix(xb, gy, n_batch, seq, conv_prev8, h0, pos0, ow)
    hd = q.shape[1]
    q3, k3, v3 = (a.reshape(n_batch, seq, hd) for a in (q, k, v))
    if caches is None:
        T = ATT_TILE
        nq = seq // T
        d = np.arange(nq)[:, None, None] * T + np.arange(T)[None, :, None] - np.arange(T)[None, None, :]
        tiles = jnp.take(dist_tab, jnp.asarray(np.maximum(d, 0)), axis=0)
        tiles = jnp.where(jnp.asarray(d >= 0)[..., None], tiles, NEG_BIG)
        o = attn_prompt(q3, k3, v3, jnp.transpose(tiles, (3, 0, 1, 2)), n_batch, seq)
    else:
        cache_k, cache_v = caches
        W = cache_k.shape[1]
        ck = cache_k.reshape(n_batch, W, hd)
        cv = cache_v.reshape(n_batch, W, hd)
        NPAD = 128
        d_old = W + np.arange(seq)[:, None] - np.arange(W)[None, :]
        b_old = jnp.transpose(jnp.take(dist_tab, jnp.asarray(d_old), axis=0), (2, 0, 1))
        d_new = np.arange(seq)[:, None] - np.arange(NPAD)[None, :]
        ok_new = (d_new >= 0) & (np.arange(NPAD)[None, :] < seq)
        b_new = jnp.take(dist_tab, jnp.asarray(np.maximum(d_new, 0)), axis=0)
        b_new = jnp.transpose(jnp.where(jnp.asarray(ok_new)[..., None], b_new, NEG_BIG), (2, 0, 1))
        o = attn_sample(q3, k3, v3, ck, cv, b_old, b_new)
    x = proj_residual(x, yc, o.reshape(n_batch * seq, hd), ow["w_out_c"], ow["w_out_d"])
    e = hd // H_D
    return x, conv_last, h_last, k3.reshape(n_batch, seq, H_D, e), v3.reshape(n_batch, seq, H_D, e)


def _moe_weights(l, router_group_w, router_group_b, router_expert_w, router_expert_b, exp_w_gate, exp_w_up,
                 exp_w_down):
    d = router_group_w.shape[1]
    n_used = N_GROUPS + router_expert_w.shape[2]
    rw = jnp.concatenate([router_group_w[l], router_expert_w[l], jnp.zeros((d, ROUTER_LANES - n_used), F32)], axis=1)
    rb = jnp.concatenate([router_group_b[l], router_expert_b[l], jnp.zeros((ROUTER_LANES - n_used,), F32)])
    return dict(rw=rw, rb=rb.reshape(1, -1), wg=exp_w_gate[l].astype(BF16), wu=exp_w_up[l].astype(BF16),
                wd=exp_w_down[l].astype(BF16))


def kernel(x_prompt, x_sample, state_wkv, state_shift, state_conv, state_rglru, cache_k, cache_v, norm_mix, norm_ffn, norm_final, w_in_even, w_out_even, shift_mu, decay_w0, decay_up, iclr_a0, iclr_up, gate_up, key_k, key_a, bonus_r_k, lnx_g, lnx_b, sgu_norm_g, sgu_norm_b, sgu_w, sgu_b, w_in_odd, w_out_odd, conv_w, conv_b, rgate_w, rgate_b, igate_w, igate_b, lru_lambda, rel_bias, router_group_w, router_group_b, router_expert_w, router_expert_b, exp_w_gate, exp_w_up, exp_w_down):
    B, L, D = x_prompt.shape
    DB, S, _ = x_sample.shape
    depth = norm_mix.shape[0]
    xp = x_prompt.reshape(B * L, D)
    xs = x_sample.reshape(DB * S, D)
    W = cache_k.shape[2]
    dist_tab = _dist_table(rel_bias, max(L, W + S) - 1)

    wkv_p, shift_p, conv_p, lru_p, k_p, v_p = [], [], [], [], [], []
    wkv_s, shift_s, chunkv_s, conv_s, lru_s, k_s, v_s = [], [], [], [], [], [], []
    for l in range(depth):
        j = l // 2
        if l % 2 == 0:
            ew = _even_weights(j, w_in_even, w_out_even, shift_mu, decay_w0, decay_up, iclr_a0, iclr_up, gate_up,
                               key_k, key_a, bonus_r_k, lnx_g, lnx_b, sgu_norm_g, sgu_norm_b, sgu_w, sgu_b)
            a_proj = ew["mu"].shape[1]
            h_a = state_wkv.shape[2]
            xp, sh, wkv, _ = _even_layer(xp, B, L, GMLP_TILE, jnp.zeros((B, a_proj), F32),
                                         jnp.zeros((B, h_a, DH_A, DH_A), F32), norm_mix[l], ew)
            xs, sh_s, wkv_s_new, vn_s = _even_layer(xs, DB, S, S, state_shift[j], state_wkv[j], norm_mix[l], ew)
            wkv_p.append(wkv)
            shift_p.append(sh)
            wkv_s.append(wkv_s_new)
            shift_s.append(sh_s)
            chunkv_s.append(vn_s.reshape(DB, S, -1))
        else:
            ow = _odd_weights(j, w_in_odd, w_out_odd, conv_w, conv_b, rgate_w, rgate_b, igate_w, igate_b, lru_lambda)
            dc = ow["lam"].shape[1]
            xp, cv, hl, kr, vr = _odd_layer(xp, B, L, jnp.zeros((B, CONV_W - 1, dc), F32), jnp.zeros((B, dc), F32),
                                            0, None, dist_tab, norm_mix[l], ow)
            xs, cv_s, hl_s, kr_s, vr_s = _odd_layer(xs, DB, S, state_conv[j], state_rglru[j], PAST_LEN,
                                                    (cache_k[j], cache_v[j]), dist_tab, norm_mix[l], ow)
            conv_p.append(cv)
            lru_p.append(hl)
            k_p.append(kr)
            v_p.append(vr)
            conv_s.append(cv_s)
            lru_s.append(hl_s)
            k_s.append(kr_s)
            v_s.append(vr_s)
        mw = _moe_weights(l, router_group_w, router_group_b, router_expert_w, router_expert_b, exp_w_gate, exp_w_up,
                          exp_w_down)
        xp = moe_layer(xp, norm_ffn[l], mw["rw"], mw["rb"], mw["wg"], mw["wu"], mw["wd"])
        xs = moe_layer(xs, norm_ffn[l], mw["rw"], mw["rb"], mw["wg"], mw["wu"], mw["wd"])
    y_prompt = rmsnorm_call(xp, norm_final).reshape(B, L, D)
    y_sample = rmsnorm_call(xs, norm_final).reshape(DB, S, D)
    return (y_prompt, y_sample,
            jnp.stack(wkv_p), jnp.stack(shift_p), jnp.stack(conv_p), jnp.stack(lru_p), jnp.stack(k_p), jnp.stack(v_p),
            jnp.stack(wkv_s), jnp.stack(shift_s), jnp.stack(chunkv_s), jnp.stack(conv_s), jnp.stack(lru_s),
            jnp.stack(k_s), jnp.stack(v_s))
```

```python
import functools
import math

import numpy as np
import jax
import jax.numpy as jnp
from jax import lax
from jax.experimental import pallas as pl
from jax.experimental.pallas import tpu as pltpu

F32 = jnp.float32
BF16 = jnp.bfloat16
HI = lax.Precision.HIGHEST

PAST_LEN = 8192
DH_A = 64
R_DECAY = 64
R_ICLR = 64
R_GATE = 128
GN_EPS = 64e-5
H_B = 4
H_C = 8
CONV_W = 4
LRU_C = 8.0
H_D = 8
DILATED = ((128, 1), (512, 4), (2048, 16))
N_BUCKETS = 32
BUCKET_MAX_DIST = 2048
NEG_BIG = -1e30
N_GROUPS = 4
EXP_PER_GROUP = 4
NORM_EPS = 1e-6

VMEM_LIMIT = 56 * 1024 * 1024
RWKV_CHUNK = 64
RWKV_PASSES = 1
ATT_TILE = 128
ATT_SUBTILES = 4
LANES = 128


def _cparams(sem):
    return pltpu.CompilerParams(dimension_semantics=sem, vmem_limit_bytes=VMEM_LIMIT)


def _dot(a, b, precision=None):
    return jnp.dot(a, b, preferred_element_type=F32, precision=precision)


def _dot_nt(a, b, precision=None):
    return lax.dot_general(a, b, (((1,), (1,)), ((), ())), preferred_element_type=F32, precision=precision)


def _dot_tn(a, b, precision=None):
    return lax.dot_general(a, b, (((0,), (0,)), ((), ())), preferred_element_type=F32, precision=precision)


def _split_bf16(x, n):
    parts = []
    for _ in range(n):
        hi = x.astype(BF16)
        parts.append(hi)
        x = x - hi.astype(F32)
    return parts


def _mp_dot(dotfn, a, b, passes):
    if passes == 1:
        return dotfn(a.astype(BF16), b.astype(BF16))
    a_hi, a_lo = _split_bf16(a, 2)
    b_hi, b_lo = _split_bf16(b, 2)
    return dotfn(a_hi, b_hi) + (dotfn(a_hi, b_lo) + dotfn(a_lo, b_hi))


def _dot_exact_rhs(a, b_bf16, n_split):
    parts = _split_bf16(a, n_split)
    acc = _dot(parts[0], b_bf16)
    for part in parts[1:]:
        acc = acc + _dot(part, b_bf16)
    return acc


def _dot_exact_lhs(a_bf16, b, n_split):
    parts = _split_bf16(b, n_split)
    acc = _dot(a_bf16, parts[0])
    for part in parts[1:]:
        acc = acc + _dot(a_bf16, part)
    return acc


def _softplus(x):
    return jnp.maximum(x, 0.0) + jnp.log(1.0 + jnp.exp(-jnp.abs(x)))


def _sigmoid(x):
    return 1.0 / (1.0 + jnp.exp(-x))


def _gelu(x):
    c = math.sqrt(2.0 / math.pi)
    return 0.5 * x * (1.0 + jnp.tanh(c * (x + 0.044715 * (x * x * x))))


def _row_tile(t, pref=512):
    return pref if t % pref == 0 else t


def _norm_matmul_kernel(x_ref, g_ref, w_ref, *out_refs, splits):
    x = x_ref[...]
    ms = jnp.mean(x * x, axis=-1, keepdims=True)
    h = (x * lax.rsqrt(ms + NORM_EPS) * g_ref[...]).astype(BF16)
    off = 0
    for o_ref, n in zip(out_refs, splits):
        o_ref[...] = _dot(h, w_ref[:, off:off + n])
        off += n


def norm_matmul(x, g, w_bf16, splits):
    t, d = x.shape
    n = w_bf16.shape[1]
    tm = _row_tile(t)
    return pl.pallas_call(
        functools.partial(_norm_matmul_kernel, splits=splits),
        out_shape=[jax.ShapeDtypeStruct((t, s), F32) for s in splits],
        grid=(t // tm,),
        in_specs=[pl.BlockSpec((tm, d), lambda i: (i, 0)),
                  pl.BlockSpec((1, d), lambda i: (0, 0)),
                  pl.BlockSpec((d, n), lambda i: (0, 0))],
        out_specs=[pl.BlockSpec((tm, s), lambda i: (i, 0)) for s in splits],
        compiler_params=_cparams(("parallel",)),
        name="norm_matmul",
    )(x, g.reshape(1, d), w_bf16)


def _proj_res_kernel(x_ref, a_ref, b_ref, wa_ref, wb_ref, o_ref):
    acc = _dot(a_ref[...].astype(BF16), wa_ref[...]) + _dot(b_ref[...].astype(BF16), wb_ref[...])
    o_ref[...] = x_ref[...] + acc


def proj_residual(x, a, b, wa, wb):
    t, d = x.shape
    tm = _row_tile(t)
    ka, kb = a.shape[1], b.shape[1]
    return pl.pallas_call(
        _proj_res_kernel,
        out_shape=jax.ShapeDtypeStruct((t, d), F32),
        grid=(t // tm,),
        in_specs=[pl.BlockSpec((tm, d), lambda i: (i, 0)),
                  pl.BlockSpec((tm, ka), lambda i: (i, 0)),
                  pl.BlockSpec((tm, kb), lambda i: (i, 0)),
                  pl.BlockSpec((ka, d), lambda i: (0, 0)),
                  pl.BlockSpec((kb, d), lambda i: (0, 0))],
        out_specs=pl.BlockSpec((tm, d), lambda i: (i, 0)),
        compiler_params=_cparams(("parallel",)),
        name="proj_residual",
    )(x, a, b, wa, wb)


def _rmsnorm_kernel(x_ref, g_ref, o_ref):
    x = x_ref[...]
    ms = jnp.mean(x * x, axis=-1, keepdims=True)
    o_ref[...] = x * lax.rsqrt(ms + NORM_EPS) * g_ref[...]


def rmsnorm_call(x, g):
    t, d = x.shape
    tm = _row_tile(t)
    return pl.pallas_call(
        _rmsnorm_kernel,
        out_shape=jax.ShapeDtypeStruct((t, d), F32),
        grid=(t // tm,),
        in_specs=[pl.BlockSpec((tm, d), lambda i: (i, 0)), pl.BlockSpec((1, d), lambda i: (0, 0))],
        out_specs=pl.BlockSpec((tm, d), lambda i: (i, 0)),
        compiler_params=_cparams(("parallel",)),
        name="final_rmsnorm",
    )(x, g.reshape(1, d))


ROUTER_LANES = 128


def _moe_kernel(x_ref, g_ref, rw_ref, rb_ref, wg_ref, wu_ref, wd_ref, o_ref, xn_scr, gate_scr, acc_scr, *, n_exp):
    e = pl.program_id(1)
    tm = x_ref.shape[0]
    lane = lax.broadcasted_iota(jnp.int32, (tm, ROUTER_LANES), 1)

    @pl.when(e == 0)
    def _():
        x = x_ref[...]
        ms = jnp.mean(x * x, axis=-1, keepdims=True)
        xn = x * lax.rsqrt(ms + NORM_EPS) * g_ref[...]
        xn_scr[...] = xn.astype(BF16)
        logits = _dot(xn, rw_ref[...], HI) + rb_ref[...]
        lg = jnp.where(lane < N_GROUPS, logits, -jnp.inf)
        gm = jnp.max(lg, axis=-1, keepdims=True)
        top_pg = 1.0 / jnp.sum(jnp.exp(lg - gm), axis=-1, keepdims=True)
        grp = jnp.min(jnp.where(lg == gm, lane, ROUTER_LANES), axis=-1, keepdims=True)
        in_grp = (lane >= N_GROUPS) & (lane < N_GROUPS + n_exp) & (((lane - N_GROUPS) >> 2) == grp)
        le = jnp.where(in_grp, logits, -jnp.inf)
        t1 = jnp.max(le, axis=-1, keepdims=True)
        i1 = jnp.min(jnp.where(le == t1, lane, ROUTER_LANES), axis=-1, keepdims=True)
        le2 = jnp.where(lane == i1, -jnp.inf, le)
        t2 = jnp.max(le2, axis=-1, keepdims=True)
        i2 = jnp.min(jnp.where(le2 == t2, lane, ROUTER_LANES), axis=-1, keepdims=True)
        ex = jnp.exp(t2 - t1)
        w1 = 1.0 / (1.0 + ex)
        w2 = ex * w1
        gate_scr[...] = jnp.where(lane == i1, w1 * top_pg, 0.0) + jnp.where(lane == i2, w2 * top_pg, 0.0)
        acc_scr[...] = jnp.zeros_like(acc_scr)

    xn = xn_scr[...]
    hg = _dot(xn, wg_ref[0])
    hu = _dot(xn, wu_ref[0])
    gcol = jnp.sum(jnp.where(lane == e + N_GROUPS, gate_scr[...], 0.0), axis=-1, keepdims=True)
    hid = hg * _sigmoid(hg) * hu * gcol
    acc_scr[...] += _dot(hid.astype(BF16), wd_ref[0])

    @pl.when(e == n_exp - 1)
    def _():
        o_ref[...] = x_ref[...] + acc_scr[...]


def moe_layer(x, g, rw, rb, wg, wu, wd):
    t, d = x.shape
    n_exp, _, f = wg.shape
    tm = _row_tile(t)
    return pl.pallas_call(
        functools.partial(_moe_kernel, n_exp=n_exp),
        out_shape=jax.ShapeDtypeStruct((t, d), F32),
        grid=(t // tm, n_exp),
        in_specs=[pl.BlockSpec((tm, d), lambda i, e: (i, 0)),
                  pl.BlockSpec((1, d), lambda i, e: (0, 0)),
                  pl.BlockSpec((d, ROUTER_LANES), lambda i, e: (0, 0)),
                  pl.BlockSpec((1, ROUTER_LANES), lambda i, e: (0, 0)),
                  pl.BlockSpec((1, d, f), lambda i, e: (e, 0, 0)),
                  pl.BlockSpec((1, d, f), lambda i, e: (e, 0, 0)),
                  pl.BlockSpec((1, f, d), lambda i, e: (e, 0, 0))],
        out_specs=pl.BlockSpec((tm, d), lambda i, e: (i, 0)),
        scratch_shapes=[pltpu.VMEM((tm, d), BF16), pltpu.VMEM((tm, ROUTER_LANES), F32), pltpu.VMEM((tm, d), F32)],
        compiler_params=_cparams(("parallel", "arbitrary")),
        name="hier_moe",
    )(x, g.reshape(1, d), rw, rb, wg, wu, wd)


def _rwkv_kernel(p_ref, prev_ref, s0_ref, mu_ref, w0_ref, wd_ref, a0_ref, wa_ref, wg_ref, kk_ref, ka_ref,
                 bonus_ref, lng_ref, lnb_ref, tri_ref, hsum_ref, ya_ref, sf_ref, s_scr, prev_scr,
                 *, C, H, DH, n_chunks, passes):
    c = pl.program_id(1)

    @pl.when(c == 0)
    def _():
        s_scr[...] = s0_ref[0]
        prev_scr[...] = prev_ref[0]

    DA = H * DH
    p = p_ref[...]
    row = lax.broadcasted_iota(jnp.int32, p.shape, 0)
    shifted = jnp.where(row == 0, prev_scr[...], pltpu.roll(p, 1, axis=0))
    prev_scr[...] = p[C - 1:C, :]
    xs = p + (shifted - p) * mu_ref[...]
    r = xs[:, 0:DA]
    k = xs[:, DA:2 * DA]
    v = xs[:, 2 * DA:3 * DA]
    lora = xs[:, 3 * DA:3 * DA + R_DECAY + R_ICLR]
    gd = xs[:, 3 * DA + R_DECAY + R_ICLR:3 * DA + R_DECAY + R_ICLR + R_GATE]

    w_log = -_softplus(-(w0_ref[...] + _mp_dot(_dot, jnp.tanh(lora), wd_ref[...], 3))) - 0.5
    lw = -jnp.exp(w_log)
    a = _sigmoid(a0_ref[...] + _mp_dot(_dot, lora, wa_ref[...], 3))
    g = _mp_dot(_dot, _sigmoid(gd), wg_ref[...], 3)

    kk = k * kk_ref[...]
    ss = _dot_exact_rhs(kk * kk, hsum_ref[...], 2)
    kk = kk / jnp.maximum(jnp.sqrt(ss), 1e-12)
    k2 = k * (1.0 + (a - 1.0) * ka_ref[...])
    kka = kk * a

    cum = _dot_exact_lhs(tri_ref[...], lw, 3)
    p_in = jnp.exp(cum)
    r_t = r * p_in
    a_t = kk * jnp.exp(cum - lw)
    p_inv = jnp.exp(-cum)
    b_t = kka * p_inv
    k_t = k2 * p_inv
    p_tot = p_in[C - 1:C, :]
    bonus = _dot_exact_rhs(r * k2 * bonus_ref[...], hsum_ref[...], 2) * v

    ri = lax.broadcasted_iota(jnp.int32, (C, C), 0)
    ci = lax.broadcasted_iota(jnp.int32, (C, C), 1)
    strict = ri > ci
    incl = ri >= ci
    eye = (ri == ci).astype(F32)
    n_double = max(int(math.ceil(math.log2(C))) - 1, 0)

    ys = []
    for h in range(H):
        sl = slice(h * DH, (h + 1) * DH)
        Bt, Kt, V = b_t[:, sl], k_t[:, sl], v[:, sl]
        AR = jnp.concatenate([a_t[:, sl], r_t[:, sl]], axis=0)
        S0 = s_scr[h]
        GB = _mp_dot(_dot_nt, AR, Bt, passes)
        GK = _mp_dot(_dot_nt, AR, Kt, passes)
        Lm = jnp.where(strict, GB[0:C], 0.0)
        Gb = jnp.where(incl, GB[C:2 * C], 0.0)
        MG = jnp.concatenate([jnp.where(strict, GK[0:C], 0.0), jnp.where(incl, GK[C:2 * C], 0.0)], axis=0)
        T = eye - Lm
        Pw = Lm
        for _ in range(n_double):
            Pw = _mp_dot(_dot, Pw, Pw, passes)
            T = T + _mp_dot(_dot, T, Pw, passes)
        ARS = _mp_dot(_dot_nt, AR, S0, passes)
        MGV = _mp_dot(_dot, MG, V, passes)
        U = _mp_dot(_dot, T, -(ARS[0:C] + MGV[0:C]), passes)
        Y = ARS[C:2 * C] + _mp_dot(_dot, Gb, U, passes) + MGV[C:2 * C]
        UV = jnp.concatenate([U, V], axis=0)
        BK = jnp.concatenate([Bt, Kt], axis=0)
        s_scr[h] = (S0 + _mp_dot(_dot_tn, UV, BK, passes)) * p_tot[:, sl]
        mu_y = jnp.mean(Y, axis=-1, keepdims=True)
        yc = Y - mu_y
        var = jnp.mean(yc * yc, axis=-1, keepdims=True)
        ys.append(yc * lax.rsqrt(var + GN_EPS))
    y = jnp.concatenate(ys, axis=-1) * lng_ref[...] + lnb_ref[...]
    ya_ref[...] = (y + bonus) * g

    @pl.when(c == n_chunks - 1)
    def _():
        sf_ref[0] = s_scr[...]


def rwkv_mix(pa, n_batch, seq, shift_prev, wkv0, wts, passes=RWKV_PASSES):
    t, ap = pa.shape
    H = wkv0.shape[1]
    DA = H * DH_A
    C = min(RWKV_CHUNK, seq)
    n_chunks = seq // C
    tri = jnp.asarray(np.tril(np.ones((C, C), np.float32))).astype(BF16)
    hsum = jnp.asarray(np.kron(np.eye(H, dtype=np.float32), np.ones((DH_A, DH_A), np.float32))).astype(BF16)

    def full(shape):
        nd = len(shape)
        return pl.BlockSpec(shape, lambda b, c: (0,) * nd)

    vec = full((1, DA))
    ya, s_fin = pl.pallas_call(
        functools.partial(_rwkv_kernel, C=C, H=H, DH=DH_A, n_chunks=n_chunks, passes=passes),
        out_shape=[jax.ShapeDtypeStruct((t, DA), F32), jax.ShapeDtypeStruct(wkv0.shape, F32)],
        grid=(n_batch, n_chunks),
        in_specs=[pl.BlockSpec((C, ap), lambda b, c: (b * n_chunks + c, 0)),
                  pl.BlockSpec((1, 1, ap), lambda b, c: (b, 0, 0)),
                  pl.BlockSpec((1, H, DH_A, DH_A), lambda b, c: (b, 0, 0, 0)),
                  full((1, ap)), vec, full((R_DECAY + R_ICLR, DA)), vec, full((R_DECAY + R_ICLR, DA)),
                  full((R_GATE, DA)), vec, vec, vec, vec, vec, full((C, C)), full((DA, DA))],
        out_specs=[pl.BlockSpec((C, DA), lambda b, c: (b * n_chunks + c, 0)),
                   pl.BlockSpec((1, H, DH_A, DH_A), lambda b, c: (b, 0, 0, 0))],
        scratch_shapes=[pltpu.VMEM((H, DH_A, DH_A), F32), pltpu.VMEM((1, ap), F32)],
        compiler_params=_cparams(("parallel", "arbitrary")),
        name="rwkv7_mix",
    )(pa, shift_prev.reshape(n_batch, 1, ap), wkv0, wts["mu"], wts["w0"], wts["wd"], wts["a0"], wts["wa"],
      wts["wg"], wts["key_k"], wts["key_a"], wts["bonus"], wts["lnx_g"], wts["lnx_b"], tri, hsum)
    return ya, s_fin


GMLP_TILE = 128


def _gmlp_kernel(u_ref, v_ref, ng_ref, nb_ref, wm_ref, bias_ref, o_ref, vn_ref):
    vf = _gelu(v_ref[...])
    mu = jnp.mean(vf, axis=-1, keepdims=True)
    vc = vf - mu
    var = jnp.mean(vc * vc, axis=-1, keepdims=True)
    vn = vc * lax.rsqrt(var + NORM_EPS) * ng_ref[...] + nb_ref[...]
    vn_ref[...] = vn
    vb = vn.astype(BF16)
    n_h = wm_ref.shape[0]
    cb = vn.shape[1] // n_h
    s = jnp.concatenate([_dot(wm_ref[h], vb[:, h * cb:(h + 1) * cb]) for h in range(n_h)], axis=-1)
    o_ref[...] = _gelu(u_ref[...]) * (s + bias_ref[...])


def gmlp_mix(pu, pv, ng, nb, wm_bf16, bias_tile):
    t, db = pu.shape
    n_h = wm_bf16.shape[0]
    return pl.pallas_call(
        _gmlp_kernel,
        out_shape=[jax.ShapeDtypeStruct((t, db), F32), jax.ShapeDtypeStruct((t, db), F32)],
        grid=(t // GMLP_TILE,),
        in_specs=[pl.BlockSpec((GMLP_TILE, db), lambda i: (i, 0)),
                  pl.BlockSpec((GMLP_TILE, db), lambda i: (i, 0)),
                  pl.BlockSpec((1, db), lambda i: (0, 0)),
                  pl.BlockSpec((1, db), lambda i: (0, 0)),
                  pl.BlockSpec((n_h, GMLP_TILE, GMLP_TILE), lambda i: (0, 0, 0)),
                  pl.BlockSpec((GMLP_TILE, db), lambda i: (0, 0))],
        out_specs=[pl.BlockSpec((GMLP_TILE, db), lambda i: (i, 0)),
                   pl.BlockSpec((GMLP_TILE, db), lambda i: (i, 0))],
        compiler_params=_cparams(("parallel",)),
        name="gmlp_mix",
    )(pu, pv, ng, nb, wm_bf16, bias_tile)


N_SEG = 8


def _rglru_kernel(xb_ref, gy_ref, cprev_ref, h0_ref, cw_ref, cb_ref, gw_ref, gb_ref, lam_ref,
                  yc_ref, ctail_ref, hl_ref, xe_scr, a_scr, b_scr, h_scr, *, TL, DC, pos0, n_tiles):
    l = pl.program_id(1)
    PAD = 8

    @pl.when(l == 0)
    def _():
        xe_scr[0:PAD, :] = cprev_ref[0]
        h_scr[...] = h0_ref[0]

    xe_scr[PAD:PAD + TL, :] = xb_ref[...]
    xc = cb_ref[...] + xe_scr[pl.ds(PAD - (CONV_W - 1), TL), :] * cw_ref[0:1, :]
    for i in range(1, CONV_W):
        xc = xc + xe_scr[pl.ds(PAD - (CONV_W - 1) + i, TL), :] * cw_ref[i:i + 1, :]
    tail = xe_scr[TL:TL + PAD, :]
    ctail_ref[0] = tail
    xe_scr[0:PAD, :] = tail

    gates = _dot(xc.astype(BF16), gw_ref[...]) + gb_ref[...]
    rg = _sigmoid(gates[:, 0:DC])
    ig = _sigmoid(gates[:, DC:2 * DC])
    log_a = -LRU_C * rg * _softplus(-lam_ref[...])
    a = jnp.exp(log_a)
    mult = jnp.sqrt(1.0 - jnp.exp(2.0 * log_a))
    row = lax.broadcasted_iota(jnp.int32, (TL, DC), 0)
    mult = jnp.where(row + (l * TL + pos0) == 0, 1.0, mult)
    b = mult * ig * xc
    n_slab = DC // LANES
    for s in range(n_slab):
        a_scr[s] = a[:, s * LANES:(s + 1) * LANES]
        b_scr[s] = b[:, s * LANES:(s + 1) * LANES]

    seg = TL // N_SEG

    def step(i, carry):
        idx = pl.ds(i, N_SEG, stride=seg) if seg > 1 else pl.ds(0, N_SEG)
        out = []
        for s in range(n_slab):
            hloc, ap = carry[s]
            ai = a_scr[s, idx, :]
            hloc = ai * hloc + b_scr[s, idx, :]
            ap = ap * ai
            b_scr[s, idx, :] = hloc
            a_scr[s, idx, :] = ap
            out.append((hloc, ap))
        return tuple(out)

    lax.fori_loop(0, seg, step,
                  tuple((jnp.zeros((N_SEG, LANES), F32), jnp.ones((N_SEG, LANES), F32)) for _ in range(n_slab)))

    carry = h_scr[...]
    g_act = _gelu(gy_ref[...])
    for j in range(N_SEG):
        rows = slice(j * seg, (j + 1) * seg)
        hloc = jnp.concatenate([b_scr[s, rows, :] for s in range(n_slab)], axis=-1)
        ap = jnp.concatenate([a_scr[s, rows, :] for s in range(n_slab)], axis=-1)
        hj = hloc + ap * carry
        yc_ref[rows, :] = g_act[rows, :] * hj
        carry = hj[seg - 1:seg, :]
    h_scr[...] = carry

    @pl.when(l == n_tiles - 1)
    def _():
        hl_ref[0] = carry


def rglru_mix(xb, gy, n_batch, seq, conv_prev8, h0, pos0, wts):
    t, dc = xb.shape
    TL = 512 if seq % 512 == 0 else seq
    n_tiles = seq // TL

    def full(shape):
        nd = len(shape)
        return pl.BlockSpec(shape, lambda b, l: (0,) * nd)

    yc, ctail, hl = pl.pallas_call(
        functools.partial(_rglru_kernel, TL=TL, DC=dc, pos0=pos0, n_tiles=n_tiles),
        out_shape=[jax.ShapeDtypeStruct((t, dc), F32), jax.ShapeDtypeStruct((n_batch, 8, dc), F32),
                   jax.ShapeDtypeStruct((n_batch, 1, dc), F32)],
        grid=(n_batch, n_tiles),
        in_specs=[pl.BlockSpec((TL, dc), lambda b, l: (b * n_tiles + l, 0)),
                  pl.BlockSpec((TL, dc), lambda b, l: (b * n_tiles + l, 0)),
                  pl.BlockSpec((1, 8, dc), lambda b, l: (b, 0, 0)),
                  pl.BlockSpec((1, 1, dc), lambda b, l: (b, 0, 0)),
                  full((CONV_W, dc)), full((1, dc)), full((dc, 2 * dc)), full((1, 2 * dc)), full((1, dc))],
        out_specs=[pl.BlockSpec((TL, dc), lambda b, l: (b * n_tiles + l, 0)),
                   pl.BlockSpec((1, 8, dc), lambda b, l: (b, 0, 0)),
                   pl.BlockSpec((1, 1, dc), lambda b, l: (b, 0, 0))],
        scratch_shapes=[pltpu.VMEM((TL + 8, dc), F32), pltpu.VMEM((dc // LANES, TL, LANES), F32),
                        pltpu.VMEM((dc // LANES, TL, LANES), F32), pltpu.VMEM((1, dc), F32)],
        compiler_params=_cparams(("parallel", "arbitrary")),
        name="rglru_mix",
    )(xb, gy, conv_prev8, h0.reshape(n_batch, 1, dc), wts["conv_w"], wts["conv_b"], wts["gate_w"], wts["gate_b"],
      wts["lam"])
    return yc, ctail[:, 8 - (CONV_W - 1):, :], hl.reshape(n_batch, dc)


def _t5_bucket(dist):
    dist = np.asarray(dist)
    max_exact = N_BUCKETS // 2
    scaled = np.log(np.maximum(dist, 1) / max_exact) / math.log(BUCKET_MAX_DIST / max_exact)
    large = np.minimum(max_exact + (scaled * (N_BUCKETS - max_exact)).astype(np.int32), N_BUCKETS - 1)
    return np.where(dist < max_exact, dist, large).astype(np.int32)


def _dist_table(rel_bias, max_dist):
    dist = np.arange(max_dist + 1)
    count = np.zeros(max_dist + 1, np.float32)
    for window, dil in DILATED:
        count += ((dist % dil == 0) & (dist <= window)).astype(np.float32)
    logcnt = np.where(count > 0, np.log(np.maximum(count, 1.0)), 0.0).astype(np.float32)
    tab = jnp.take(rel_bias, jnp.asarray(_t5_bucket(dist)), axis=0) + jnp.asarray(logcnt)[:, None]
    return jnp.where(jnp.asarray(count > 0)[:, None], tab, NEG_BIG)


def _toeplitz_tiles(tab, n_pos, n_neg, T):
    D, H = tab.shape
    span = T * n_pos
    assert D >= span
    n_col = span + T * n_neg + T - 1
    ext = jnp.concatenate([jnp.flip(tab[:span], axis=0), jnp.full((n_col + 1 - span, H), NEG_BIG, F32)], axis=0)
    ext = jnp.transpose(ext)
    skew = jnp.tile(ext, (1, T))[:, :T * n_col].reshape(H, T, n_col)
    tiles = [skew[:, :, span - 1 - T * dd: span - 1 - T * dd + T] for dd in range(-n_neg, n_pos)]
    return jnp.stack(tiles, axis=1)


def _attn_prompt_kernel(q_ref, k_ref, v_ref, bias_ref, o_ref, kb_scr, vb_scr, *, E, SUB, NS):
    qi = pl.program_id(2)
    TQ = NS * SUB

    @pl.when(qi == 0)
    def _():
        kb_scr[...] = k_ref[0].astype(BF16)
        vb_scr[...] = v_ref[0].astype(BF16)

    lane = lax.broadcasted_iota(jnp.int32, (SUB, 2 * E), 1)
    q2 = []
    for rs in range(NS):
        q = q_ref[0, rs * SUB:(rs + 1) * SUB, :] * (E ** -0.5)
        q2.append(jnp.concatenate([jnp.where(lane < E, q, 0.0), jnp.where(lane >= E, q, 0.0)], axis=0).astype(BF16))

    def body(i, carry):
        j = qi - i
        kj = kb_scr[pl.ds(pl.multiple_of(j * TQ, TQ), TQ), :]
        vj = vb_scr[pl.ds(pl.multiple_of(j * TQ, TQ), TQ), :]
        out = []
        for rs in range(NS):
            m, l, acc = carry[rs]
            s = _dot_nt(q2[rs], kj)
            parts = []
            for cs in range(NS):
                dd = i * NS + (rs - cs + NS - 1)
                bias = jnp.concatenate([bias_ref[0, dd], bias_ref[1, dd]], axis=0)
                parts.append(s[:, cs * SUB:(cs + 1) * SUB] + bias)
            mx = parts[0]
            for part in parts[1:]:
                mx = jnp.maximum(mx, part)
            m_new = jnp.maximum(m, jnp.max(mx, axis=-1, keepdims=True))
            alpha = jnp.exp(m - m_new)
            ps = [jnp.exp(part - m_new) for part in parts]
            psum = ps[0]
            for pexp in ps[1:]:
                psum = psum + pexp
            l = alpha * l + psum
            acc = alpha * acc + _dot(jnp.concatenate(ps, axis=-1).astype(BF16), vj)
            out.append((m_new, l, acc))
        return tuple(out)

    init = tuple((jnp.full((2 * SUB, SUB), NEG_BIG, F32), jnp.zeros((2 * SUB, SUB), F32),
                  jnp.zeros((2 * SUB, 2 * E), F32)) for _ in range(NS))
    res = lax.fori_loop(0, qi + 1, body, init)
    for rs in range(NS):
        m, l, acc = res[rs]
        o = acc / jnp.sum(l, axis=-1, keepdims=True)
        o_ref[0, rs * SUB:(rs + 1) * SUB, :] = jnp.where(lane < E, o[0:SUB], o[SUB:2 * SUB])


def attn_prompt(q, k, v, bias_tiles, n_batch, seq):
    hd = q.shape[-1]
    E = hd // H_D
    SUB = ATT_TILE
    NS = ATT_SUBTILES
    TQ = SUB * NS
    nq = seq // TQ
    nt = bias_tiles.shape[1]
    return pl.pallas_call(
        functools.partial(_attn_prompt_kernel, E=E, SUB=SUB, NS=NS),
        out_shape=jax.ShapeDtypeStruct((n_batch, seq, hd), F32),
        grid=(H_D // 2, n_batch, nq),
        in_specs=[pl.BlockSpec((1, TQ, 2 * E), lambda hp, b, i: (b, i, hp)),
                  pl.BlockSpec((1, seq, 2 * E), lambda hp, b, i: (b, 0, hp)),
                  pl.BlockSpec((1, seq, 2 * E), lambda hp, b, i: (b, 0, hp)),
                  pl.BlockSpec((2, nt, SUB, SUB), lambda hp, b, i: (hp, 0, 0, 0))],
        out_specs=pl.BlockSpec((1, TQ, 2 * E), lambda hp, b, i: (b, i, hp)),
        scratch_shapes=[pltpu.VMEM((seq, 2 * E), BF16), pltpu.VMEM((seq, 2 * E), BF16)],
        compiler_params=_cparams(("arbitrary", "arbitrary", "arbitrary")),
        name="dilated_attn_prompt",
    )(q, k, v, bias_tiles)


def _attn_sample_kernel(q_ref, kn_ref, vn_ref, ck_ref, cv_ref, bo_ref, bn_ref, o_ref, *, E, S):
    lane = lax.broadcasted_iota(jnp.int32, (S, 2 * E), 1)
    NPAD = bn_ref.shape[-1]
    outs = []
    for hp in range(H_D // 2):
        sl = slice(hp * 2 * E, (hp + 1) * 2 * E)
        q = q_ref[0, :, sl] * (E ** -0.5)
        q2 = jnp.concatenate([jnp.where(lane < E, q, 0.0), jnp.where(lane >= E, q, 0.0)], axis=0).astype(BF16)
        zpad = jnp.zeros((NPAD - S, 2 * E), F32)
        kn = jnp.concatenate([kn_ref[0, :, sl], zpad], axis=0).astype(BF16)
        vn = jnp.concatenate([vn_ref[0, :, sl], zpad], axis=0).astype(BF16)
        s_old = _dot_nt(q2, ck_ref[0, :, sl].astype(BF16)) + jnp.concatenate([bo_ref[2 * hp], bo_ref[2 * hp + 1]], axis=0)
        s_new = _dot_nt(q2, kn) + jnp.concatenate([bn_ref[2 * hp], bn_ref[2 * hp + 1]], axis=0)
        m = jnp.maximum(jnp.max(s_old, axis=-1, keepdims=True), jnp.max(s_new, axis=-1, keepdims=True))
        p_old = jnp.exp(s_old - m)
        p_new = jnp.exp(s_new - m)
        l = jnp.sum(p_old, axis=-1, keepdims=True) + jnp.sum(p_new, axis=-1, keepdims=True)
        acc = _dot(p_old.astype(BF16), cv_ref[0, :, sl].astype(BF16)) + _dot(p_new.astype(BF16), vn)
        o = acc / l
        outs.append(jnp.where(lane < E, o[0:S], o[S:2 * S]))
    o_ref[0] = jnp.concatenate(outs, axis=-1)


def attn_sample(q, k_new, v_new, cache_k, cache_v, bias_old, bias_new):
    n_batch, S, hd = q.shape
    W = cache_k.shape[1]
    E = hd // H_D
    NPAD = bias_new.shape[-1]
    return pl.pallas_call(
        functools.partial(_attn_sample_kernel, E=E, S=S),
        out_shape=jax.ShapeDtypeStruct((n_batch, S, hd), F32),
        grid=(n_batch,),
        in_specs=[pl.BlockSpec((1, S, hd), lambda b: (b, 0, 0)),
                  pl.BlockSpec((1, S, hd), lambda b: (b, 0, 0)),
                  pl.BlockSpec((1, S, hd), lambda b: (b, 0, 0)),
                  pl.BlockSpec((1, W, hd), lambda b: (b, 0, 0)),
                  pl.BlockSpec((1, W, hd), lambda b: (b, 0, 0)),
                  pl.BlockSpec((H_D, S, W), lambda b: (0, 0, 0)),
                  pl.BlockSpec((H_D, S, NPAD), lambda b: (0, 0, 0))],
        out_specs=pl.BlockSpec((1, S, hd), lambda b: (b, 0, 0)),
        compiler_params=_cparams(("parallel",)),
        name="dilated_attn_sample",
    )(q, k_new, v_new, cache_k, cache_v, bias_old, bias_new)


def _even_weights(j, w_in_even, w_out_even, shift_mu, decay_w0, decay_up, iclr_a0, iclr_up, gate_up, key_k, key_a,
                  bonus_r_k, lnx_g, lnx_b, sgu_norm_g, sgu_norm_b, sgu_w, sgu_b):
    da = decay_w0.shape[1]
    zeros_d = jnp.zeros((R_ICLR, da), F32)
    zeros_i = jnp.zeros((R_DECAY, da), F32)
    return dict(
        w_in=w_in_even[j].astype(BF16),
        w_out_a=w_out_even[j, :da].astype(BF16), w_out_b=w_out_even[j, da:].astype(BF16),
        mu=shift_mu[j].reshape(1, -1), w0=decay_w0[j].reshape(1, -1), a0=iclr_a0[j].reshape(1, -1),
        wd=jnp.concatenate([decay_up[j], zeros_d], axis=0), wa=jnp.concatenate([zeros_i, iclr_up[j]], axis=0),
        wg=gate_up[j], key_k=key_k[j].reshape(1, -1), key_a=key_a[j].reshape(1, -1),
        bonus=bonus_r_k[j].reshape(1, -1), lnx_g=lnx_g[j].reshape(1, -1), lnx_b=lnx_b[j].reshape(1, -1),
        ng=sgu_norm_g[j].reshape(1, -1), nb=sgu_norm_b[j].reshape(1, -1), sgu_w=sgu_w[j], sgu_b=sgu_b[j])


def _gmlp_tables(sgu_w, sgu_b, chunk):
    reps = GMLP_TILE // chunk
    n_h = sgu_w.shape[0]
    cb = None
    wm = sgu_w[:, :chunk, :chunk] * jnp.asarray(np.tril(np.ones((chunk, chunk), np.float32)))
    if reps > 1:
        eye = jnp.asarray(np.eye(reps, dtype=np.float32))
        wm = jnp.einsum("ab,hts->hatbs", eye, wm).reshape(n_h, GMLP_TILE, GMLP_TILE)
    bias = jnp.tile(jnp.transpose(sgu_b[:, :chunk]), (reps, 1))
    return wm.astype(BF16), bias


def _even_layer(x, n_batch, seq, chunk, shift_prev, wkv0, norm_g, ew):
    pa, pu, pv = norm_matmul(x, norm_g, ew["w_in"], (ew["mu"].shape[1], ew["ng"].shape[1], ew["ng"].shape[1]))
    ya, wkv = rwkv_mix(pa, n_batch, seq, shift_prev, wkv0, ew)
    wm, bias = _gmlp_tables(ew["sgu_w"], ew["sgu_b"], chunk)
    cb = pu.shape[1] // wm.shape[0]
    bias_tile = jnp.repeat(bias, cb, axis=1)
    yb, vn = gmlp_mix(pu, pv, ew["ng"], ew["nb"], wm, bias_tile)
    x = proj_residual(x, ya, yb, ew["w_out_a"], ew["w_out_b"])
    last = pa.reshape(n_batch, seq, -1)[:, -1]
    return x, last, wkv, vn


def _odd_weights(j, w_in_odd, w_out_odd, conv_w, conv_b, rgate_w, rgate_b, igate_w, igate_b, lru_lambda):
    dc = conv_b.shape[1]
    eye = jnp.asarray(np.eye(H_C, dtype=np.float32))

    def blockdiag(w):
        dh = w.shape[-1]
        return jnp.einsum("ab,aij->aibj", eye, w).reshape(H_C * dh, H_C * dh)

    return dict(
        w_in=w_in_odd[j].astype(BF16),
        w_out_c=w_out_odd[j, :dc].astype(BF16), w_out_d=w_out_odd[j, dc:].astype(BF16),
        conv_w=conv_w[j], conv_b=conv_b[j].reshape(1, -1),
        gate_w=jnp.concatenate([blockdiag(rgate_w[j]), blockdiag(igate_w[j])], axis=1).astype(BF16),
        gate_b=jnp.concatenate([rgate_b[j], igate_b[j]]).reshape(1, -1),
        lam=lru_lambda[j].reshape(1, -1))


def _odd_layer(x, n_batch, seq, conv_prev, h0, pos0, caches, dist_tab, norm_g, ow):
    dc = ow["lam"].shape[1]
    gy, xb, q, k, v = norm_matmul(x, norm_g, ow["w_in"], (dc,) * 5)
    conv_prev8 = jnp.pad(conv_prev, ((0, 0), (8 - (CONV_W - 1), 0), (0, 0)))
    yc, conv_last, h_last = rglru_mix(xb, gy, n_batch, seq, conv_prev8, h0, pos0, ow)
    hd = q.shape[1]
    q3, k3, v3 = (a.reshape(n_batch, seq, hd) for a in (q, k, v))
    if caches is None:
        tiles = _toeplitz_tiles(dist_tab, seq // ATT_TILE, ATT_SUBTILES - 1, ATT_TILE)
        o = attn_prompt(q3, k3, v3, tiles, n_batch, seq)
    else:
        cache_k, cache_v = caches
        W = cache_k.shape[1]
        ck = cache_k.reshape(n_batch, W, hd)
        cv = cache_v.reshape(n_batch, W, hd)
        NPAD = 128
        d_old = W + np.arange(seq)[:, None] - np.arange(W)[None, :]
        b_old = jnp.transpose(jnp.take(dist_tab, jnp.asarray(d_old), axis=0), (2, 0, 1))
        d_new = np.arange(seq)[:, None] - np.arange(NPAD)[None, :]
        ok_new = (d_new >= 0) & (np.arange(NPAD)[None, :] < seq)
        b_new = jnp.take(dist_tab, jnp.asarray(np.maximum(d_new, 0)), axis=0)
        b_new = jnp.transpose(jnp.where(jnp.asarray(ok_new)[..., None], b_new, NEG_BIG), (2, 0, 1))
        o = attn_sample(q3, k3, v3, ck, cv, b_old, b_new)
    x = proj_residual(x, yc, o.reshape(n_batch * seq, hd), ow["w_out_c"], ow["w_out_d"])
    e = hd // H_D
    return x, conv_last, h_last, k3.reshape(n_batch, seq, H_D, e), v3.reshape(n_batch, seq, H_D, e)


def _moe_weights(l, router_group_w, router_group_b, router_expert_w, router_expert_b, exp_w_gate, exp_w_up,
                 exp_w_down):
    d = router_group_w.shape[1]
    n_used = N_GROUPS + router_expert_w.shape[2]
    rw = jnp.concatenate([router_group_w[l], router_expert_w[l], jnp.zeros((d, ROUTER_LANES - n_used), F32)], axis=1)
    rb = jnp.concatenate([router_group_b[l], router_expert_b[l], jnp.zeros((ROUTER_LANES - n_used,), F32)])
    return dict(rw=rw, rb=rb.reshape(1, -1), wg=exp_w_gate[l].astype(BF16), wu=exp_w_up[l].astype(BF16),
                wd=exp_w_down[l].astype(BF16))


def kernel(x_prompt, x_sample, state_wkv, state_shift, state_conv, state_rglru, cache_k, cache_v, norm_mix, norm_ffn, norm_final, w_in_even, w_out_even, shift_mu, decay_w0, decay_up, iclr_a0, iclr_up, gate_up, key_k, key_a, bonus_r_k, lnx_g, lnx_b, sgu_norm_g, sgu_norm_b, sgu_w, sgu_b, w_in_odd, w_out_odd, conv_w, conv_b, rgate_w, rgate_b, igate_w, igate_b, lru_lambda, rel_bias, router_group_w, router_group_b, router_expert_w, router_expert_b, exp_w_gate, exp_w_up, exp_w_down):
    B, L, D = x_prompt.shape
    DB, S, _ = x_sample.shape
    depth = norm_mix.shape[0]
    xp = x_prompt.reshape(B * L, D)
    xs = x_sample.reshape(DB * S, D)
    W = cache_k.shape[2]
    dist_tab = _dist_table(rel_bias, max(L, W + S) - 1)

    wkv_p, shift_p, conv_p, lru_p, k_p, v_p = [], [], [], [], [], []
    wkv_s, shift_s, chunkv_s, conv_s, lru_s, k_s, v_s = [], [], [], [], [], [], []
    for l in range(depth):
        j = l // 2
        if l % 2 == 0:
            ew = _even_weights(j, w_in_even, w_out_even, shift_mu, decay_w0, decay_up, iclr_a0, iclr_up, gate_up,
                               key_k, key_a, bonus_r_k, lnx_g, lnx_b, sgu_norm_g, sgu_norm_b, sgu_w, sgu_b)
            a_proj = ew["mu"].shape[1]
            h_a = state_wkv.shape[2]
            xp, sh, wkv, _ = _even_layer(xp, B, L, GMLP_TILE, jnp.zeros((B, a_proj), F32),
                                         jnp.zeros((B, h_a, DH_A, DH_A), F32), norm_mix[l], ew)
            xs, sh_s, wkv_s_new, vn_s = _even_layer(xs, DB, S, S, state_shift[j], state_wkv[j], norm_mix[l], ew)
            wkv_p.append(wkv)
            shift_p.append(sh)
            wkv_s.append(wkv_s_new)
            shift_s.append(sh_s)
            chunkv_s.append(vn_s.reshape(DB, S, -1))
        else:
            ow = _odd_weights(j, w_in_odd, w_out_odd, conv_w, conv_b, rgate_w, rgate_b, igate_w, igate_b, lru_lambda)
            dc = ow["lam"].shape[1]
            xp, cv, hl, kr, vr = _odd_layer(xp, B, L, jnp.zeros((B, CONV_W - 1, dc), F32), jnp.zeros((B, dc), F32),
                                            0, None, dist_tab, norm_mix[l], ow)
            xs, cv_s, hl_s, kr_s, vr_s = _odd_layer(xs, DB, S, state_conv[j], state_rglru[j], PAST_LEN,
                                                    (cache_k[j], cache_v[j]), dist_tab, norm_mix[l], ow)
            conv_p.append(cv)
            lru_p.append(hl)
            k_p.append(kr)
            v_p.append(vr)
            conv_s.append(cv_s)
            lru_s.append(hl_s)
            k_s.append(kr_s)
            v_s.append(vr_s)
        mw = _moe_weights(l, router_group_w, router_group_b, router_expert_w, router_expert_b, exp_w_gate, exp_w_up,
                          exp_w_down)
        xp = moe_layer(xp, norm_ffn[l], mw["rw"], mw["rb"], mw["wg"], mw["wu"], mw["wd"])
        xs = moe_layer(xs, norm_ffn[l], mw["rw"], mw["rb"], mw["wg"], mw["wu"], mw["wd"])
    y_prompt = rmsnorm_call(xp, norm_final).reshape(B, L, D)
    y_sample = rmsnorm_call(xs, norm_final).reshape(DB, S, D)
    return (y_prompt, y_sample,
            jnp.stack(wkv_p), jnp.stack(shift_p), jnp.stack(conv_p), jnp.stack(lru_p), jnp.stack(k_p), jnp.stack(v_p),
            jnp.stack(wkv_s), jnp.stack(shift_s), jnp.stack(chunkv_s), jnp.stack(conv_s), jnp.stack(lru_s),
            jnp.stack(k_s), jnp.stack(v_s))
```

```python
import functools
import math

import numpy as np
import jax
import jax.numpy as jnp
from jax import lax
from jax.experimental import pallas as pl
from jax.experimental.pallas import tpu as pltpu

F32 = jnp.float32
BF16 = jnp.bfloat16
HI = lax.Precision.HIGHEST

PAST_LEN = 8192
DH_A = 64
R_DECAY = 64
R_ICLR = 64
R_GATE = 128
GN_EPS = 64e-5
H_B = 4
H_C = 8
CONV_W = 4
LRU_C = 8.0
H_D = 8
DILATED = ((128, 1), (512, 4), (2048, 16))
N_BUCKETS = 32
BUCKET_MAX_DIST = 2048
NEG_BIG = -1e30
N_GROUPS = 4
EXP_PER_GROUP = 4
NORM_EPS = 1e-6

VMEM_LIMIT = 56 * 1024 * 1024
RWKV_CHUNK = 64
RWKV_PAR_PROMPT = 2
RWKV_PAR_SAMPLE = 8
ATT_TILE = 128
ATT_SUBTILES = 4
LANES = 128


def _cparams(sem):
    return pltpu.CompilerParams(dimension_semantics=sem, vmem_limit_bytes=VMEM_LIMIT)


def _dot(a, b, precision=None):
    return jnp.dot(a, b, preferred_element_type=F32, precision=precision)


def _dot_nt(a, b, precision=None):
    return lax.dot_general(a, b, (((1,), (1,)), ((), ())), preferred_element_type=F32, precision=precision)


def _dot_tn(a, b, precision=None):
    return lax.dot_general(a, b, (((0,), (0,)), ((), ())), preferred_element_type=F32, precision=precision)


def _split_bf16(x, n):
    parts = []
    for _ in range(n):
        hi = x.astype(BF16)
        parts.append(hi)
        x = x - hi.astype(F32)
    return parts


def _mp_dot(dotfn, a, b, passes):
    if passes == 1:
        return dotfn(a.astype(BF16), b.astype(BF16))
    a_hi, a_lo = _split_bf16(a, 2)
    b_hi, b_lo = _split_bf16(b, 2)
    return dotfn(a_hi, b_hi) + (dotfn(a_hi, b_lo) + dotfn(a_lo, b_hi))


def _dot_exact_rhs(a, b_bf16, n_split):
    parts = _split_bf16(a, n_split)
    acc = _dot(parts[0], b_bf16)
    for part in parts[1:]:
        acc = acc + _dot(part, b_bf16)
    return acc


def _dot_exact_lhs(a_bf16, b, n_split):
    parts = _split_bf16(b, n_split)
    acc = _dot(a_bf16, parts[0])
    for part in parts[1:]:
        acc = acc + _dot(a_bf16, part)
    return acc


def _softplus(x):
    return jnp.maximum(x, 0.0) + jnp.log(1.0 + jnp.exp(-jnp.abs(x)))


def _sigmoid(x):
    return 1.0 / (1.0 + jnp.exp(-x))


def _gelu(x):
    c = math.sqrt(2.0 / math.pi)
    return 0.5 * x * (1.0 + jnp.tanh(c * (x + 0.044715 * (x * x * x))))


def _row_tile(t, pref=512):
    return pref if t % pref == 0 else t


def _norm_matmul_kernel(x_ref, g_ref, w_ref, *out_refs, splits):
    x = x_ref[...]
    ms = jnp.mean(x * x, axis=-1, keepdims=True)
    h = (x * lax.rsqrt(ms + NORM_EPS) * g_ref[...]).astype(BF16)
    off = 0
    for o_ref, n in zip(out_refs, splits):
        o_ref[...] = _dot(h, w_ref[:, off:off + n])
        off += n


def norm_matmul(x, g, w_bf16, splits):
    t, d = x.shape
    n = w_bf16.shape[1]
    tm = _row_tile(t)
    return pl.pallas_call(
        functools.partial(_norm_matmul_kernel, splits=splits),
        out_shape=[jax.ShapeDtypeStruct((t, s), F32) for s in splits],
        grid=(t // tm,),
        in_specs=[pl.BlockSpec((tm, d), lambda i: (i, 0)),
                  pl.BlockSpec((1, d), lambda i: (0, 0)),
                  pl.BlockSpec((d, n), lambda i: (0, 0))],
        out_specs=[pl.BlockSpec((tm, s), lambda i: (i, 0)) for s in splits],
        compiler_params=_cparams(("parallel",)),
        name="norm_matmul",
    )(x, g.reshape(1, d), w_bf16)


def _proj_res_kernel(x_ref, a_ref, b_ref, wa_ref, wb_ref, o_ref):
    acc = _dot(a_ref[...].astype(BF16), wa_ref[...]) + _dot(b_ref[...].astype(BF16), wb_ref[...])
    o_ref[...] = x_ref[...] + acc


def proj_residual(x, a, b, wa, wb):
    t, d = x.shape
    tm = _row_tile(t)
    ka, kb = a.shape[1], b.shape[1]
    return pl.pallas_call(
        _proj_res_kernel,
        out_shape=jax.ShapeDtypeStruct((t, d), F32),
        grid=(t // tm,),
        in_specs=[pl.BlockSpec((tm, d), lambda i: (i, 0)),
                  pl.BlockSpec((tm, ka), lambda i: (i, 0)),
                  pl.BlockSpec((tm, kb), lambda i: (i, 0)),
                  pl.BlockSpec((ka, d), lambda i: (0, 0)),
                  pl.BlockSpec((kb, d), lambda i: (0, 0))],
        out_specs=pl.BlockSpec((tm, d), lambda i: (i, 0)),
        compiler_params=_cparams(("parallel",)),
        name="proj_residual",
    )(x, a, b, wa, wb)


def _rmsnorm_kernel(x_ref, g_ref, o_ref):
    x = x_ref[...]
    ms = jnp.mean(x * x, axis=-1, keepdims=True)
    o_ref[...] = x * lax.rsqrt(ms + NORM_EPS) * g_ref[...]


def rmsnorm_call(x, g):
    t, d = x.shape
    tm = _row_tile(t)
    return pl.pallas_call(
        _rmsnorm_kernel,
        out_shape=jax.ShapeDtypeStruct((t, d), F32),
        grid=(t // tm,),
        in_specs=[pl.BlockSpec((tm, d), lambda i: (i, 0)), pl.BlockSpec((1, d), lambda i: (0, 0))],
        out_specs=pl.BlockSpec((tm, d), lambda i: (i, 0)),
        compiler_params=_cparams(("parallel",)),
        name="final_rmsnorm",
    )(x, g.reshape(1, d))


ROUTER_LANES = 128


def _moe_kernel(x_ref, g_ref, rw_ref, rb_ref, wg_ref, wu_ref, wd_ref, o_ref, xn_scr, gate_scr, acc_scr, *, n_exp):
    e = pl.program_id(1)
    tm = x_ref.shape[0]
    lane = lax.broadcasted_iota(jnp.int32, (tm, ROUTER_LANES), 1)

    @pl.when(e == 0)
    def _():
        x = x_ref[...]
        ms = jnp.mean(x * x, axis=-1, keepdims=True)
        xn = x * lax.rsqrt(ms + NORM_EPS) * g_ref[...]
        xn_scr[...] = xn.astype(BF16)
        logits = _dot(xn, rw_ref[...], HI) + rb_ref[...]
        lg = jnp.where(lane < N_GROUPS, logits, -jnp.inf)
        gm = jnp.max(lg, axis=-1, keepdims=True)
        top_pg = 1.0 / jnp.sum(jnp.exp(lg - gm), axis=-1, keepdims=True)
        grp = jnp.min(jnp.where(lg == gm, lane, ROUTER_LANES), axis=-1, keepdims=True)
        in_grp = (lane >= N_GROUPS) & (lane < N_GROUPS + n_exp) & (((lane - N_GROUPS) >> 2) == grp)
        le = jnp.where(in_grp, logits, -jnp.inf)
        t1 = jnp.max(le, axis=-1, keepdims=True)
        i1 = jnp.min(jnp.where(le == t1, lane, ROUTER_LANES), axis=-1, keepdims=True)
        le2 = jnp.where(lane == i1, -jnp.inf, le)
        t2 = jnp.max(le2, axis=-1, keepdims=True)
        i2 = jnp.min(jnp.where(le2 == t2, lane, ROUTER_LANES), axis=-1, keepdims=True)
        ex = jnp.exp(t2 - t1)
        w1 = 1.0 / (1.0 + ex)
        w2 = ex * w1
        gate_scr[...] = jnp.where(lane == i1, w1 * top_pg, 0.0) + jnp.where(lane == i2, w2 * top_pg, 0.0)
        acc_scr[...] = jnp.zeros_like(acc_scr)

    xn = xn_scr[...]
    hg = _dot(xn, wg_ref[0])
    hu = _dot(xn, wu_ref[0])
    gcol = jnp.sum(jnp.where(lane == e + N_GROUPS, gate_scr[...], 0.0), axis=-1, keepdims=True)
    hid = hg * _sigmoid(hg) * hu * gcol
    acc_scr[...] += _dot(hid.astype(BF16), wd_ref[0])

    @pl.when(e == n_exp - 1)
    def _():
        o_ref[...] = x_ref[...] + acc_scr[...]


def moe_layer(x, g, rw, rb, wg, wu, wd):
    t, d = x.shape
    n_exp, _, f = wg.shape
    tm = _row_tile(t)
    return pl.pallas_call(
        functools.partial(_moe_kernel, n_exp=n_exp),
        out_shape=jax.ShapeDtypeStruct((t, d), F32),
        grid=(t // tm, n_exp),
        in_specs=[pl.BlockSpec((tm, d), lambda i, e: (i, 0)),
                  pl.BlockSpec((1, d), lambda i, e: (0, 0)),
                  pl.BlockSpec((d, ROUTER_LANES), lambda i, e: (0, 0)),
                  pl.BlockSpec((1, ROUTER_LANES), lambda i, e: (0, 0)),
                  pl.BlockSpec((1, d, f), lambda i, e: (e, 0, 0)),
                  pl.BlockSpec((1, d, f), lambda i, e: (e, 0, 0)),
                  pl.BlockSpec((1, f, d), lambda i, e: (e, 0, 0))],
        out_specs=pl.BlockSpec((tm, d), lambda i, e: (i, 0)),
        scratch_shapes=[pltpu.VMEM((tm, d), BF16), pltpu.VMEM((tm, ROUTER_LANES), F32), pltpu.VMEM((tm, d), F32)],
        compiler_params=_cparams(("parallel", "arbitrary")),
        name="hier_moe",
    )(x, g.reshape(1, d), rw, rb, wg, wu, wd)


def _rwkv_kernel(p_ref, prev_ref, s0_ref, mu_ref, w0_ref, wd_ref, a0_ref, wa_ref, wg_ref, kk_ref, ka_ref,
                 bonus_ref, lng_ref, lnb_ref, tri_ref, hsum_ref, ya_ref, sf_ref, s_scr, prev_scr,
                 *, NB, C, H, DH, n_chunks):
    c = pl.program_id(1)

    @pl.when(c == 0)
    def _():
        s_scr[...] = s0_ref[:, 0]
        prev_scr[...] = prev_ref[:, 0]

    DA = H * DH
    R = NB * C
    p = p_ref[...].reshape(R, p_ref.shape[-1])
    row = lax.broadcasted_iota(jnp.int32, p.shape, 0)
    shifted = pltpu.roll(p, 1, axis=0)
    for n in range(NB):
        shifted = jnp.where(row == n * C, prev_scr[n], shifted)
        prev_scr[n] = p[(n + 1) * C - 1:(n + 1) * C, :]
    xs = p + (shifted - p) * mu_ref[...]
    r = xs[:, 0:DA]
    k = xs[:, DA:2 * DA]
    v = xs[:, 2 * DA:3 * DA]
    lora = xs[:, 3 * DA:3 * DA + R_DECAY + R_ICLR]
    gd = xs[:, 3 * DA + R_DECAY + R_ICLR:3 * DA + R_DECAY + R_ICLR + R_GATE]

    w_log = -_softplus(-(w0_ref[...] + _mp_dot(_dot, jnp.tanh(lora), wd_ref[...], 3))) - 0.5
    lw = -jnp.exp(w_log)
    a = _sigmoid(a0_ref[...] + _mp_dot(_dot, lora, wa_ref[...], 3))
    g = _mp_dot(_dot, _sigmoid(gd), wg_ref[...], 3)

    kk = k * kk_ref[...]
    ss = _dot_exact_rhs(kk * kk, hsum_ref[...], 2)
    kk = kk / jnp.maximum(jnp.sqrt(ss), 1e-12)
    k2 = k * (1.0 + (a - 1.0) * ka_ref[...])
    kka = kk * a

    cum = jnp.concatenate([_dot_exact_lhs(tri_ref[...], lw[n * C:(n + 1) * C], 3) for n in range(NB)], axis=0)
    p_in = jnp.exp(cum)
    r_t = r * p_in
    a_t = kk * jnp.exp(cum - lw)
    p_inv = jnp.exp(-cum)
    b_t = kka * p_inv
    k_t = k2 * p_inv
    bonus = _dot_exact_rhs(r * k2 * bonus_ref[...], hsum_ref[...], 2) * v

    ri = lax.broadcasted_iota(jnp.int32, (C, C), 0)
    ci = lax.broadcasted_iota(jnp.int32, (C, C), 1)
    strict = ri > ci
    incl = ri >= ci
    eye = (ri == ci).astype(F32)
    n_double = max(int(math.ceil(math.log2(C))) - 1, 0)

    chains = [(n, h) for n in range(NB) for h in range(H)]

    def blk(x, n, h):
        return x[n * C:(n + 1) * C, h * DH:(h + 1) * DH]

    def bf(x):
        return x.astype(BF16)

    Bt = [bf(blk(b_t, n, h)) for n, h in chains]
    Kt = [bf(blk(k_t, n, h)) for n, h in chains]
    Vf = [blk(v, n, h) for n, h in chains]
    AR = [bf(jnp.concatenate([blk(a_t, n, h), blk(r_t, n, h)], axis=0)) for n, h in chains]
    S0 = [s_scr[n, h] for n, h in chains]
    idx = range(len(chains))
    GB = [_dot_nt(AR[i], Bt[i]) for i in idx]
    GK = [_dot_nt(AR[i], Kt[i]) for i in idx]
    ARS = [_dot_nt(AR[i], bf(S0[i])) for i in idx]
    Lm = [jnp.where(strict, GB[i][0:C], 0.0) for i in idx]
    Gb = [bf(jnp.where(incl, GB[i][C:2 * C], 0.0)) for i in idx]
    MG = [bf(jnp.concatenate([jnp.where(strict, GK[i][0:C], 0.0), jnp.where(incl, GK[i][C:2 * C], 0.0)], axis=0))
          for i in idx]
    MGV = [_dot(MG[i], bf(Vf[i])) for i in idx]
    T = [eye - Lm[i] for i in idx]
    Pw = [bf(Lm[i]) for i in idx]
    for _ in range(n_double):
        Pw = [bf(_dot(Pw[i], Pw[i])) for i in idx]
        T = [T[i] + _dot(bf(T[i]), Pw[i]) for i in idx]
    U = [_dot(bf(T[i]), bf(-(ARS[i][0:C] + MGV[i][0:C]))) for i in idx]
    Y = [ARS[i][C:2 * C] + _dot(Gb[i], bf(U[i])) + MGV[i][C:2 * C] for i in idx]
    for i, (n, h) in enumerate(chains):
        UV = bf(jnp.concatenate([U[i], Vf[i]], axis=0))
        BK = jnp.concatenate([Bt[i], Kt[i]], axis=0)
        p_tot = p_in[(n + 1) * C - 1:(n + 1) * C, h * DH:(h + 1) * DH]
        s_scr[n, h] = (S0[i] + _dot_tn(UV, BK)) * p_tot

    rows = []
    for n in range(NB):
        ys = []
        for h in range(H):
            Yh = Y[n * H + h]
            yc = Yh - jnp.mean(Yh, axis=-1, keepdims=True)
            var = jnp.mean(yc * yc, axis=-1, keepdims=True)
            ys.append(yc * lax.rsqrt(var + GN_EPS))
        rows.append(jnp.concatenate(ys, axis=-1))
    y = jnp.concatenate(rows, axis=0) * lng_ref[...] + lnb_ref[...]
    ya_ref[...] = ((y + bonus) * g).reshape(NB, C, DA)

    @pl.when(c == n_chunks - 1)
    def _():
        sf_ref[:, 0] = s_scr[...]


def rwkv_mix(pa, n_batch, seq, shift_prev, wkv0, wts, n_par):
    t, ap = pa.shape
    H = wkv0.shape[1]
    DA = H * DH_A
    C = min(RWKV_CHUNK, seq)
    n_chunks = seq // C
    NB = n_par
    G = n_batch // NB
    tri = jnp.asarray(np.tril(np.ones((C, C), np.float32))).astype(BF16)
    hsum = jnp.asarray(np.kron(np.eye(H, dtype=np.float32), np.ones((DH_A, DH_A), np.float32))).astype(BF16)

    def full(shape):
        nd = len(shape)
        return pl.BlockSpec(shape, lambda b, c: (0,) * nd)

    vec = full((1, DA))
    ya, s_fin = pl.pallas_call(
        functools.partial(_rwkv_kernel, NB=NB, C=C, H=H, DH=DH_A, n_chunks=n_chunks),
        out_shape=[jax.ShapeDtypeStruct((NB, t // NB, DA), F32),
                   jax.ShapeDtypeStruct((NB, G, H, DH_A, DH_A), F32)],
        grid=(G, n_chunks),
        in_specs=[pl.BlockSpec((NB, C, ap), lambda b, c: (0, b * n_chunks + c, 0)),
                  pl.BlockSpec((NB, 1, 1, ap), lambda b, c: (0, b, 0, 0)),
                  pl.BlockSpec((NB, 1, H, DH_A, DH_A), lambda b, c: (0, b, 0, 0, 0)),
                  full((1, ap)), vec, full((R_DECAY + R_ICLR, DA)), vec, full((R_DECAY + R_ICLR, DA)),
                  full((R_GATE, DA)), vec, vec, vec, vec, vec, full((C, C)), full((DA, DA))],
        out_specs=[pl.BlockSpec((NB, C, DA), lambda b, c: (0, b * n_chunks + c, 0)),
                   pl.BlockSpec((NB, 1, H, DH_A, DH_A), lambda b, c: (0, b, 0, 0, 0))],
        scratch_shapes=[pltpu.VMEM((NB, H, DH_A, DH_A), F32), pltpu.VMEM((NB, 1, ap), F32)],
        compiler_params=_cparams(("parallel", "arbitrary")),
        name="rwkv7_mix",
    )(pa.reshape(NB, t // NB, ap), shift_prev.reshape(NB, G, 1, ap), wkv0.reshape(NB, G, H, DH_A, DH_A),
      wts["mu"], wts["w0"], wts["wd"], wts["a0"], wts["wa"],
      wts["wg"], wts["key_k"], wts["key_a"], wts["bonus"], wts["lnx_g"], wts["lnx_b"], tri, hsum)
    return ya.reshape(t, DA), s_fin.reshape(wkv0.shape)


GMLP_TILE = 128


def _gmlp_kernel(u_ref, v_ref, ng_ref, nb_ref, wm_ref, bias_ref, o_ref, vn_ref):
    vf = _gelu(v_ref[...])
    mu = jnp.mean(vf, axis=-1, keepdims=True)
    vc = vf - mu
    var = jnp.mean(vc * vc, axis=-1, keepdims=True)
    vn = vc * lax.rsqrt(var + NORM_EPS) * ng_ref[...] + nb_ref[...]
    vn_ref[...] = vn
    vb = vn.astype(BF16)
    n_h = wm_ref.shape[0]
    cb = vn.shape[1] // n_h
    s = jnp.concatenate([_dot(wm_ref[h], vb[:, h * cb:(h + 1) * cb]) for h in range(n_h)], axis=-1)
    o_ref[...] = _gelu(u_ref[...]) * (s + bias_ref[...])


def gmlp_mix(pu, pv, ng, nb, wm_bf16, bias_tile):
    t, db = pu.shape
    n_h = wm_bf16.shape[0]
    return pl.pallas_call(
        _gmlp_kernel,
        out_shape=[jax.ShapeDtypeStruct((t, db), F32), jax.ShapeDtypeStruct((t, db), F32)],
        grid=(t // GMLP_TILE,),
        in_specs=[pl.BlockSpec((GMLP_TILE, db), lambda i: (i, 0)),
                  pl.BlockSpec((GMLP_TILE, db), lambda i: (i, 0)),
                  pl.BlockSpec((1, db), lambda i: (0, 0)),
                  pl.BlockSpec((1, db), lambda i: (0, 0)),
                  pl.BlockSpec((n_h, GMLP_TILE, GMLP_TILE), lambda i: (0, 0, 0)),
                  pl.BlockSpec((GMLP_TILE, db), lambda i: (0, 0))],
        out_specs=[pl.BlockSpec((GMLP_TILE, db), lambda i: (i, 0)),
                   pl.BlockSpec((GMLP_TILE, db), lambda i: (i, 0))],
        compiler_params=_cparams(("parallel",)),
        name="gmlp_mix",
    )(pu, pv, ng, nb, wm_bf16, bias_tile)


N_SEG = 8


def _rglru_kernel(xb_ref, gy_ref, cprev_ref, h0_ref, cw_ref, cb_ref, gw_ref, gb_ref, lam_ref,
                  yc_ref, ctail_ref, hl_ref, xe_scr, a_scr, b_scr, h_scr, *, TL, DC, pos0, n_tiles):
    l = pl.program_id(1)
    PAD = 8

    @pl.when(l == 0)
    def _():
        xe_scr[0:PAD, :] = cprev_ref[0]
        h_scr[...] = h0_ref[0]

    xe_scr[PAD:PAD + TL, :] = xb_ref[...]
    xc = cb_ref[...] + xe_scr[pl.ds(PAD - (CONV_W - 1), TL), :] * cw_ref[0:1, :]
    for i in range(1, CONV_W):
        xc = xc + xe_scr[pl.ds(PAD - (CONV_W - 1) + i, TL), :] * cw_ref[i:i + 1, :]
    tail = xe_scr[TL:TL + PAD, :]
    ctail_ref[0] = tail
    xe_scr[0:PAD, :] = tail

    gates = _dot(xc.astype(BF16), gw_ref[...]) + gb_ref[...]
    rg = _sigmoid(gates[:, 0:DC])
    ig = _sigmoid(gates[:, DC:2 * DC])
    log_a = -LRU_C * rg * _softplus(-lam_ref[...])
    a = jnp.exp(log_a)
    mult = jnp.sqrt(1.0 - jnp.exp(2.0 * log_a))
    row = lax.broadcasted_iota(jnp.int32, (TL, DC), 0)
    mult = jnp.where(row + (l * TL + pos0) == 0, 1.0, mult)
    b = mult * ig * xc
    n_slab = DC // LANES
    for s in range(n_slab):
        a_scr[s] = a[:, s * LANES:(s + 1) * LANES]
        b_scr[s] = b[:, s * LANES:(s + 1) * LANES]

    seg = TL // N_SEG

    def step(i, carry):
        idx = pl.ds(i, N_SEG, stride=seg) if seg > 1 else pl.ds(0, N_SEG)
        out = []
        for s in range(n_slab):
            hloc, ap = carry[s]
            ai = a_scr[s, idx, :]
            hloc = ai * hloc + b_scr[s, idx, :]
            ap = ap * ai
            b_scr[s, idx, :] = hloc
            a_scr[s, idx, :] = ap
            out.append((hloc, ap))
        return tuple(out)

    lax.fori_loop(0, seg, step,
                  tuple((jnp.zeros((N_SEG, LANES), F32), jnp.ones((N_SEG, LANES), F32)) for _ in range(n_slab)))

    carry = h_scr[...]
    g_act = _gelu(gy_ref[...])
    for j in range(N_SEG):
        rows = slice(j * seg, (j + 1) * seg)
        hloc = jnp.concatenate([b_scr[s, rows, :] for s in range(n_slab)], axis=-1)
        ap = jnp.concatenate([a_scr[s, rows, :] for s in range(n_slab)], axis=-1)
        hj = hloc + ap * carry
        yc_ref[rows, :] = g_act[rows, :] * hj
        carry = hj[seg - 1:seg, :]
    h_scr[...] = carry

    @pl.when(l == n_tiles - 1)
    def _():
        hl_ref[0] = carry


def rglru_mix(xb, gy, n_batch, seq, conv_prev8, h0, pos0, wts):
    t, dc = xb.shape
    TL = 512 if seq % 512 == 0 else seq
    n_tiles = seq // TL

    def full(shape):
        nd = len(shape)
        return pl.BlockSpec(shape, lambda b, l: (0,) * nd)

    yc, ctail, hl = pl.pallas_call(
        functools.partial(_rglru_kernel, TL=TL, DC=dc, pos0=pos0, n_tiles=n_tiles),
        out_shape=[jax.ShapeDtypeStruct((t, dc), F32), jax.ShapeDtypeStruct((n_batch, 8, dc), F32),
                   jax.ShapeDtypeStruct((n_batch, 1, dc), F32)],
        grid=(n_batch, n_tiles),
        in_specs=[pl.BlockSpec((TL, dc), lambda b, l: (b * n_tiles + l, 0)),
                  pl.BlockSpec((TL, dc), lambda b, l: (b * n_tiles + l, 0)),
                  pl.BlockSpec((1, 8, dc), lambda b, l: (b, 0, 0)),
                  pl.BlockSpec((1, 1, dc), lambda b, l: (b, 0, 0)),
                  full((CONV_W, dc)), full((1, dc)), full((dc, 2 * dc)), full((1, 2 * dc)), full((1, dc))],
        out_specs=[pl.BlockSpec((TL, dc), lambda b, l: (b * n_tiles + l, 0)),
                   pl.BlockSpec((1, 8, dc), lambda b, l: (b, 0, 0)),
                   pl.BlockSpec((1, 1, dc), lambda b, l: (b, 0, 0))],
        scratch_shapes=[pltpu.VMEM((TL + 8, dc), F32), pltpu.VMEM((dc // LANES, TL, LANES), F32),
                        pltpu.VMEM((dc // LANES, TL, LANES), F32), pltpu.VMEM((1, dc), F32)],
        compiler_params=_cparams(("parallel", "arbitrary")),
        name="rglru_mix",
    )(xb, gy, conv_prev8, h0.reshape(n_batch, 1, dc), wts["conv_w"], wts["conv_b"], wts["gate_w"], wts["gate_b"],
      wts["lam"])
    return yc, ctail[:, 8 - (CONV_W - 1):, :], hl.reshape(n_batch, dc)


def _t5_bucket(dist):
    dist = np.asarray(dist)
    max_exact = N_BUCKETS // 2
    scaled = np.log(np.maximum(dist, 1) / max_exact) / math.log(BUCKET_MAX_DIST / max_exact)
    large = np.minimum(max_exact + (scaled * (N_BUCKETS - max_exact)).astype(np.int32), N_BUCKETS - 1)
    return np.where(dist < max_exact, dist, large).astype(np.int32)


def _dist_table(rel_bias, max_dist):
    dist = np.arange(max_dist + 1)
    count = np.zeros(max_dist + 1, np.float32)
    for window, dil in DILATED:
        count += ((dist % dil == 0) & (dist <= window)).astype(np.float32)
    logcnt = np.where(count > 0, np.log(np.maximum(count, 1.0)), 0.0).astype(np.float32)
    tab = jnp.take(rel_bias, jnp.asarray(_t5_bucket(dist)), axis=0) + jnp.asarray(logcnt)[:, None]
    return jnp.where(jnp.asarray(count > 0)[:, None], tab, NEG_BIG)


def _toeplitz_tiles(tab, n_pos, n_neg, T):
    D, H = tab.shape
    span = T * n_pos
    assert D >= span
    n_col = span + T * n_neg + T - 1
    ext = jnp.concatenate([jnp.flip(tab[:span], axis=0), jnp.full((n_col + 1 - span, H), NEG_BIG, F32)], axis=0)
    ext = jnp.transpose(ext)
    skew = jnp.tile(ext, (1, T))[:, :T * n_col].reshape(H, T, n_col)
    tiles = [skew[:, :, span - 1 - T * dd: span - 1 - T * dd + T] for dd in range(-n_neg, n_pos)]
    return jnp.stack(tiles, axis=1)


def _attn_prompt_kernel(q_ref, k_ref, v_ref, bias_ref, o_ref, kb_scr, vb_scr, *, E, SUB, NS):
    qi = pl.program_id(2)
    TQ = NS * SUB

    @pl.when(qi == 0)
    def _():
        kb_scr[...] = k_ref[0].astype(BF16)
        vb_scr[...] = v_ref[0].astype(BF16)

    lane = lax.broadcasted_iota(jnp.int32, (SUB, 2 * E), 1)
    q2 = []
    for rs in range(NS):
        q = q_ref[0, rs * SUB:(rs + 1) * SUB, :] * (E ** -0.5)
        q2.append(jnp.concatenate([jnp.where(lane < E, q, 0.0), jnp.where(lane >= E, q, 0.0)], axis=0).astype(BF16))

    def body(i, carry):
        j = qi - i
        kj = kb_scr[pl.ds(pl.multiple_of(j * TQ, TQ), TQ), :]
        vj = vb_scr[pl.ds(pl.multiple_of(j * TQ, TQ), TQ), :]
        out = []
        for rs in range(NS):
            m, l, acc = carry[rs]
            s = _dot_nt(q2[rs], kj)
            parts = []
            for cs in range(NS):
                dd = i * NS + (rs - cs + NS - 1)
                bias = jnp.concatenate([bias_ref[0, dd], bias_ref[1, dd]], axis=0)
                parts.append(s[:, cs * SUB:(cs + 1) * SUB] + bias)
            mx = parts[0]
            for part in parts[1:]:
                mx = jnp.maximum(mx, part)
            m_new = jnp.maximum(m, jnp.max(mx, axis=-1, keepdims=True))
            alpha = jnp.exp(m - m_new)
            ps = [jnp.exp(part - m_new) for part in parts]
            psum = ps[0]
            for pexp in ps[1:]:
                psum = psum + pexp
            l = alpha * l + psum
            acc = alpha * acc + _dot(jnp.concatenate(ps, axis=-1).astype(BF16), vj)
            out.append((m_new, l, acc))
        return tuple(out)

    init = tuple((jnp.full((2 * SUB, SUB), NEG_BIG, F32), jnp.zeros((2 * SUB, SUB), F32),
                  jnp.zeros((2 * SUB, 2 * E), F32)) for _ in range(NS))
    res = lax.fori_loop(0, qi + 1, body, init)
    for rs in range(NS):
        m, l, acc = res[rs]
        o = acc / jnp.sum(l, axis=-1, keepdims=True)
        o_ref[0, rs * SUB:(rs + 1) * SUB, :] = jnp.where(lane < E, o[0:SUB], o[SUB:2 * SUB])


def attn_prompt(q, k, v, bias_tiles, n_batch, seq):
    hd = q.shape[-1]
    E = hd // H_D
    SUB = ATT_TILE
    NS = ATT_SUBTILES
    TQ = SUB * NS
    nq = seq // TQ
    nt = bias_tiles.shape[1]
    return pl.pallas_call(
        functools.partial(_attn_prompt_kernel, E=E, SUB=SUB, NS=NS),
        out_shape=jax.ShapeDtypeStruct((n_batch, seq, hd), F32),
        grid=(H_D // 2, n_batch, nq),
        in_specs=[pl.BlockSpec((1, TQ, 2 * E), lambda hp, b, i: (b, i, hp)),
                  pl.BlockSpec((1, seq, 2 * E), lambda hp, b, i: (b, 0, hp)),
                  pl.BlockSpec((1, seq, 2 * E), lambda hp, b, i: (b, 0, hp)),
                  pl.BlockSpec((2, nt, SUB, SUB), lambda hp, b, i: (hp, 0, 0, 0))],
        out_specs=pl.BlockSpec((1, TQ, 2 * E), lambda hp, b, i: (b, i, hp)),
        scratch_shapes=[pltpu.VMEM((seq, 2 * E), BF16), pltpu.VMEM((seq, 2 * E), BF16)],
        compiler_params=_cparams(("arbitrary", "arbitrary", "arbitrary")),
        name="dilated_attn_prompt",
    )(q, k, v, bias_tiles)


def _attn_sample_kernel(q_ref, kn_ref, vn_ref, ck_ref, cv_ref, bo_ref, bn_ref, o_ref, *, E, S):
    lane = lax.broadcasted_iota(jnp.int32, (S, 2 * E), 1)
    NPAD = bn_ref.shape[-1]
    outs = []
    for hp in range(H_D // 2):
        sl = slice(hp * 2 * E, (hp + 1) * 2 * E)
        q = q_ref[0, :, sl] * (E ** -0.5)
        q2 = jnp.concatenate([jnp.where(lane < E, q, 0.0), jnp.where(lane >= E, q, 0.0)], axis=0).astype(BF16)
        zpad = jnp.zeros((NPAD - S, 2 * E), F32)
        kn = jnp.concatenate([kn_ref[0, :, sl], zpad], axis=0).astype(BF16)
        vn = jnp.concatenate([vn_ref[0, :, sl], zpad], axis=0).astype(BF16)
        s_old = _dot_nt(q2, ck_ref[0, :, sl].astype(BF16)) + jnp.concatenate([bo_ref[2 * hp], bo_ref[2 * hp + 1]], axis=0)
        s_new = _dot_nt(q2, kn) + jnp.concatenate([bn_ref[2 * hp], bn_ref[2 * hp + 1]], axis=0)
        m = jnp.maximum(jnp.max(s_old, axis=-1, keepdims=True), jnp.max(s_new, axis=-1, keepdims=True))
        p_old = jnp.exp(s_old - m)
        p_new = jnp.exp(s_new - m)
        l = jnp.sum(p_old, axis=-1, keepdims=True) + jnp.sum(p_new, axis=-1, keepdims=True)
        acc = _dot(p_old.astype(BF16), cv_ref[0, :, sl].astype(BF16)) + _dot(p_new.astype(BF16), vn)
        o = acc / l
        outs.append(jnp.where(lane < E, o[0:S], o[S:2 * S]))
    o_ref[0] = jnp.concatenate(outs, axis=-1)


def attn_sample(q, k_new, v_new, cache_k, cache_v, bias_old, bias_new):
    n_batch, S, hd = q.shape
    W = cache_k.shape[1]
    E = hd // H_D
    NPAD = bias_new.shape[-1]
    return pl.pallas_call(
        functools.partial(_attn_sample_kernel, E=E, S=S),
        out_shape=jax.ShapeDtypeStruct((n_batch, S, hd), F32),
        grid=(n_batch,),
        in_specs=[pl.BlockSpec((1, S, hd), lambda b: (b, 0, 0)),
                  pl.BlockSpec((1, S, hd), lambda b: (b, 0, 0)),
                  pl.BlockSpec((1, S, hd), lambda b: (b, 0, 0)),
                  pl.BlockSpec((1, W, hd), lambda b: (b, 0, 0)),
                  pl.BlockSpec((1, W, hd), lambda b: (b, 0, 0)),
                  pl.BlockSpec((H_D, S, W), lambda b: (0, 0, 0)),
                  pl.BlockSpec((H_D, S, NPAD), lambda b: (0, 0, 0))],
        out_specs=pl.BlockSpec((1, S, hd), lambda b: (b, 0, 0)),
        compiler_params=_cparams(("parallel",)),
        name="dilated_attn_sample",
    )(q, k_new, v_new, cache_k, cache_v, bias_old, bias_new)


def _even_weights(j, w_in_even, w_out_even, shift_mu, decay_w0, decay_up, iclr_a0, iclr_up, gate_up, key_k, key_a,
                  bonus_r_k, lnx_g, lnx_b, sgu_norm_g, sgu_norm_b, sgu_w, sgu_b):
    da = decay_w0.shape[1]
    zeros_d = jnp.zeros((R_ICLR, da), F32)
    zeros_i = jnp.zeros((R_DECAY, da), F32)
    return dict(
        w_in=w_in_even[j].astype(BF16),
        w_out_a=w_out_even[j, :da].astype(BF16), w_out_b=w_out_even[j, da:].astype(BF16),
        mu=shift_mu[j].reshape(1, -1), w0=decay_w0[j].reshape(1, -1), a0=iclr_a0[j].reshape(1, -1),
        wd=jnp.concatenate([decay_up[j], zeros_d], axis=0), wa=jnp.concatenate([zeros_i, iclr_up[j]], axis=0),
        wg=gate_up[j], key_k=key_k[j].reshape(1, -1), key_a=key_a[j].reshape(1, -1),
        bonus=bonus_r_k[j].reshape(1, -1), lnx_g=lnx_g[j].reshape(1, -1), lnx_b=lnx_b[j].reshape(1, -1),
        ng=sgu_norm_g[j].reshape(1, -1), nb=sgu_norm_b[j].reshape(1, -1), sgu_w=sgu_w[j], sgu_b=sgu_b[j])


def _gmlp_tables(sgu_w, sgu_b, chunk):
    reps = GMLP_TILE // chunk
    n_h = sgu_w.shape[0]
    cb = None
    wm = sgu_w[:, :chunk, :chunk] * jnp.asarray(np.tril(np.ones((chunk, chunk), np.float32)))
    if reps > 1:
        eye = jnp.asarray(np.eye(reps, dtype=np.float32))
        wm = jnp.einsum("ab,hts->hatbs", eye, wm).reshape(n_h, GMLP_TILE, GMLP_TILE)
    bias = jnp.tile(jnp.transpose(sgu_b[:, :chunk]), (reps, 1))
    return wm.astype(BF16), bias


def _even_layer(x, n_batch, seq, chunk, n_par, shift_prev, wkv0, norm_g, ew):
    pa, pu, pv = norm_matmul(x, norm_g, ew["w_in"], (ew["mu"].shape[1], ew["ng"].shape[1], ew["ng"].shape[1]))
    ya, wkv = rwkv_mix(pa, n_batch, seq, shift_prev, wkv0, ew, n_par)
    wm, bias = _gmlp_tables(ew["sgu_w"], ew["sgu_b"], chunk)
    cb = pu.shape[1] // wm.shape[0]
    bias_tile = jnp.repeat(bias, cb, axis=1)
    yb, vn = gmlp_mix(pu, pv, ew["ng"], ew["nb"], wm, bias_tile)
    x = proj_residual(x, ya, yb, ew["w_out_a"], ew["w_out_b"])
    last = pa.reshape(n_batch, seq, -1)[:, -1]
    return x, last, wkv, vn


def _odd_weights(j, w_in_odd, w_out_odd, conv_w, conv_b, rgate_w, rgate_b, igate_w, igate_b, lru_lambda):
    dc = conv_b.shape[1]
    eye = jnp.asarray(np.eye(H_C, dtype=np.float32))

    def blockdiag(w):
        dh = w.shape[-1]
        return jnp.einsum("ab,aij->aibj", eye, w).reshape(H_C * dh, H_C * dh)

    return dict(
        w_in=w_in_odd[j].astype(BF16),
        w_out_c=w_out_odd[j, :dc].astype(BF16), w_out_d=w_out_odd[j, dc:].astype(BF16),
        conv_w=conv_w[j], conv_b=conv_b[j].reshape(1, -1),
        gate_w=jnp.concatenate([blockdiag(rgate_w[j]), blockdiag(igate_w[j])], axis=1).astype(BF16),
        gate_b=jnp.concatenate([rgate_b[j], igate_b[j]]).reshape(1, -1),
        lam=lru_lambda[j].reshape(1, -1))


def _odd_layer(x, n_batch, seq, conv_prev, h0, pos0, caches, dist_tab, norm_g, ow):
    dc = ow["lam"].shape[1]
    gy, xb, q, k, v = norm_matmul(x, norm_g, ow["w_in"], (dc,) * 5)
    conv_prev8 = jnp.pad(conv_prev, ((0, 0), (8 - (CONV_W - 1), 0), (0, 0)))
    yc, conv_last, h_last = rglru_mix(xb, gy, n_batch, seq, conv_prev8, h0, pos0, ow)
    hd = q.shape[1]
    q3, k3, v3 = (a.reshape(n_batch, seq, hd) for a in (q, k, v))
    if caches is None:
        tiles = _toeplitz_tiles(dist_tab, seq // ATT_TILE, ATT_SUBTILES - 1, ATT_TILE)
        o = attn_prompt(q3, k3, v3, tiles, n_batch, seq)
    else:
        cache_k, cache_v = caches
        W = cache_k.shape[1]
        ck = cache_k.reshape(n_batch, W, hd)
        cv = cache_v.reshape(n_batch, W, hd)
        NPAD = 128
        d_old = W + np.arange(seq)[:, None] - np.arange(W)[None, :]
        b_old = jnp.transpose(jnp.take(dist_tab, jnp.asarray(d_old), axis=0), (2, 0, 1))
        d_new = np.arange(seq)[:, None] - np.arange(NPAD)[None, :]
        ok_new = (d_new >= 0) & (np.arange(NPAD)[None, :] < seq)
        b_new = jnp.take(dist_tab, jnp.asarray(np.maximum(d_new, 0)), axis=0)
        b_new = jnp.transpose(jnp.where(jnp.asarray(ok_new)[..., None], b_new, NEG_BIG), (2, 0, 1))
        o = attn_sample(q3, k3, v3, ck, cv, b_old, b_new)
    x = proj_residual(x, yc, o.reshape(n_batch * seq, hd), ow["w_out_c"], ow["w_out_d"])
    e = hd // H_D
    return x, conv_last, h_last, k3.reshape(n_batch, seq, H_D, e), v3.reshape(n_batch, seq, H_D, e)


def _moe_weights(l, router_group_w, router_group_b, router_expert_w, router_expert_b, exp_w_gate, exp_w_up,
                 exp_w_down):
    d = router_group_w.shape[1]
    n_used = N_GROUPS + router_expert_w.shape[2]
    rw = jnp.concatenate([router_group_w[l], router_expert_w[l], jnp.zeros((d, ROUTER_LANES - n_used), F32)], axis=1)
    rb = jnp.concatenate([router_group_b[l], router_expert_b[l], jnp.zeros((ROUTER_LANES - n_used,), F32)])
    return dict(rw=rw, rb=rb.reshape(1, -1), wg=exp_w_gate[l].astype(BF16), wu=exp_w_up[l].astype(BF16),
                wd=exp_w_down[l].astype(BF16))


def kernel(x_prompt, x_sample, state_wkv, state_shift, state_conv, state_rglru, cache_k, cache_v, norm_mix, norm_ffn, norm_final, w_in_even, w_out_even, shift_mu, decay_w0, decay_up, iclr_a0, iclr_up, gate_up, key_k, key_a, bonus_r_k, lnx_g, lnx_b, sgu_norm_g, sgu_norm_b, sgu_w, sgu_b, w_in_odd, w_out_odd, conv_w, conv_b, rgate_w, rgate_b, igate_w, igate_b, lru_lambda, rel_bias, router_group_w, router_group_b, router_expert_w, router_expert_b, exp_w_gate, exp_w_up, exp_w_down):
    B, L, D = x_prompt.shape
    DB, S, _ = x_sample.shape
    depth = norm_mix.shape[0]
    xp = x_prompt.reshape(B * L, D)
    xs = x_sample.reshape(DB * S, D)
    W = cache_k.shape[2]
    dist_tab = _dist_table(rel_bias, max(L, W + S) - 1)

    wkv_p, shift_p, conv_p, lru_p, k_p, v_p = [], [], [], [], [], []
    wkv_s, shift_s, chunkv_s, conv_s, lru_s, k_s, v_s = [], [], [], [], [], [], []
    for l in range(depth):
        j = l // 2
        if l % 2 == 0:
            ew = _even_weights(j, w_in_even, w_out_even, shift_mu, decay_w0, decay_up, iclr_a0, iclr_up, gate_up,
                               key_k, key_a, bonus_r_k, lnx_g, lnx_b, sgu_norm_g, sgu_norm_b, sgu_w, sgu_b)
            a_proj = ew["mu"].shape[1]
            h_a = state_wkv.shape[2]
            xp, sh, wkv, _ = _even_layer(xp, B, L, GMLP_TILE, RWKV_PAR_PROMPT, jnp.zeros((B, a_proj), F32),
                                         jnp.zeros((B, h_a, DH_A, DH_A), F32), norm_mix[l], ew)
            xs, sh_s, wkv_s_new, vn_s = _even_layer(xs, DB, S, S, RWKV_PAR_SAMPLE, state_shift[j], state_wkv[j], norm_mix[l], ew)
            wkv_p.append(wkv)
            shift_p.append(sh)
            wkv_s.append(wkv_s_new)
            shift_s.append(sh_s)
            chunkv_s.append(vn_s.reshape(DB, S, -1))
        else:
            ow = _odd_weights(j, w_in_odd, w_out_odd, conv_w, conv_b, rgate_w, rgate_b, igate_w, igate_b, lru_lambda)
            dc = ow["lam"].shape[1]
            xp, cv, hl, kr, vr = _odd_layer(xp, B, L, jnp.zeros((B, CONV_W - 1, dc), F32), jnp.zeros((B, dc), F32),
                                            0, None, dist_tab, norm_mix[l], ow)
            xs, cv_s, hl_s, kr_s, vr_s = _odd_layer(xs, DB, S, state_conv[j], state_rglru[j], PAST_LEN,
                                                    (cache_k[j], cache_v[j]), dist_tab, norm_mix[l], ow)
            conv_p.append(cv)
            lru_p.append(hl)
            k_p.append(kr)
            v_p.append(vr)
            conv_s.append(cv_s)
            lru_s.append(hl_s)
            k_s.append(kr_s)
            v_s.append(vr_s)
        mw = _moe_weights(l, router_group_w, router_group_b, router_expert_w, router_expert_b, exp_w_gate, exp_w_up,
                          exp_w_down)
        xp = moe_layer(xp, norm_ffn[l], mw["rw"], mw["rb"], mw["wg"], mw["wu"], mw["wd"])
        xs = moe_layer(xs, norm_ffn[l], mw["rw"], mw["rb"], mw["wg"], mw["wu"], mw["wd"])
    y_prompt = rmsnorm_call(xp, norm_final).reshape(B, L, D)
    y_sample = rmsnorm_call(xs, norm_final).reshape(DB, S, D)
    return (y_prompt, y_sample,
            jnp.stack(wkv_p), jnp.stack(shift_p), jnp.stack(conv_p), jnp.stack(lru_p), jnp.stack(k_p), jnp.stack(v_p),
            jnp.stack(wkv_s), jnp.stack(shift_s), jnp.stack(chunkv_s), jnp.stack(conv_s), jnp.stack(lru_s),
            jnp.stack(k_s), jnp.stack(v_s))
```

```python
import functools
import math

import numpy as np
import jax
import jax.numpy as jnp
from jax import lax
from jax.experimental import pallas as pl
from jax.experimental.pallas import tpu as pltpu

F32 = jnp.float32
BF16 = jnp.bfloat16
HI = lax.Precision.HIGHEST

PAST_LEN = 8192
DH_A = 64
R_DECAY = 64
R_ICLR = 64
R_GATE = 128
GN_EPS = 64e-5
H_B = 4
H_C = 8
CONV_W = 4
LRU_C = 8.0
H_D = 8
DILATED = ((128, 1), (512, 4), (2048, 16))
N_BUCKETS = 32
BUCKET_MAX_DIST = 2048
NEG_BIG = -1e30
N_GROUPS = 4
EXP_PER_GROUP = 4
NORM_EPS = 1e-6

VMEM_LIMIT = 56 * 1024 * 1024
RWKV_CHUNK = 64
RWKV_PAR_PROMPT = 2
RWKV_PAR_SAMPLE = 8
ATT_TILE = 128
ATT_SUBTILES = 4
LANES = 128


def _cparams(sem):
    return pltpu.CompilerParams(dimension_semantics=sem, vmem_limit_bytes=VMEM_LIMIT)


def _dot(a, b, precision=None):
    return jnp.dot(a, b, preferred_element_type=F32, precision=precision)


def _dot_nt(a, b, precision=None):
    return lax.dot_general(a, b, (((1,), (1,)), ((), ())), preferred_element_type=F32, precision=precision)


def _dot_tn(a, b, precision=None):
    return lax.dot_general(a, b, (((0,), (0,)), ((), ())), preferred_element_type=F32, precision=precision)


def _split_bf16(x, n):
    parts = []
    for _ in range(n):
        hi = x.astype(BF16)
        parts.append(hi)
        x = x - hi.astype(F32)
    return parts


def _mp_dot(dotfn, a, b, passes):
    if passes == 1:
        return dotfn(a.astype(BF16), b.astype(BF16))
    a_hi, a_lo = _split_bf16(a, 2)
    b_hi, b_lo = _split_bf16(b, 2)
    return dotfn(a_hi, b_hi) + (dotfn(a_hi, b_lo) + dotfn(a_lo, b_hi))


def _dot_exact_rhs(a, b_bf16, n_split):
    parts = _split_bf16(a, n_split)
    acc = _dot(parts[0], b_bf16)
    for part in parts[1:]:
        acc = acc + _dot(part, b_bf16)
    return acc


def _dot_exact_lhs(a_bf16, b, n_split):
    parts = _split_bf16(b, n_split)
    acc = _dot(a_bf16, parts[0])
    for part in parts[1:]:
        acc = acc + _dot(a_bf16, part)
    return acc


def _softplus(x):
    return jnp.maximum(x, 0.0) + jnp.log(1.0 + jnp.exp(-jnp.abs(x)))


def _sigmoid(x):
    return 1.0 / (1.0 + jnp.exp(-x))


def _gelu(x):
    c = math.sqrt(2.0 / math.pi)
    return 0.5 * x * (1.0 + jnp.tanh(c * (x + 0.044715 * (x * x * x))))


def _row_tile(t, pref=512):
    return pref if t % pref == 0 else t


def _norm_matmul_kernel(x_ref, g_ref, w_ref, *out_refs, splits):
    x = x_ref[...]
    ms = jnp.mean(x * x, axis=-1, keepdims=True)
    h = (x * lax.rsqrt(ms + NORM_EPS) * g_ref[...]).astype(BF16)
    off = 0
    for o_ref, n in zip(out_refs, splits):
        o_ref[...] = _dot(h, w_ref[:, off:off + n])
        off += n


def norm_matmul(x, g, w_bf16, splits):
    t, d = x.shape
    n = w_bf16.shape[1]
    tm = _row_tile(t)
    return pl.pallas_call(
        functools.partial(_norm_matmul_kernel, splits=splits),
        out_shape=[jax.ShapeDtypeStruct((t, s), F32) for s in splits],
        grid=(t // tm,),
        in_specs=[pl.BlockSpec((tm, d), lambda i: (i, 0)),
                  pl.BlockSpec((1, d), lambda i: (0, 0)),
                  pl.BlockSpec((d, n), lambda i: (0, 0))],
        out_specs=[pl.BlockSpec((tm, s), lambda i: (i, 0)) for s in splits],
        compiler_params=_cparams(("parallel",)),
        name="norm_matmul",
    )(x, g.reshape(1, d), w_bf16)


def _proj_res_kernel(x_ref, a_ref, b_ref, wa_ref, wb_ref, o_ref):
    acc = _dot(a_ref[...].astype(BF16), wa_ref[...]) + _dot(b_ref[...].astype(BF16), wb_ref[...])
    o_ref[...] = x_ref[...] + acc


def proj_residual(x, a, b, wa, wb):
    t, d = x.shape
    tm = _row_tile(t)
    ka, kb = a.shape[1], b.shape[1]
    return pl.pallas_call(
        _proj_res_kernel,
        out_shape=jax.ShapeDtypeStruct((t, d), F32),
        grid=(t // tm,),
        in_specs=[pl.BlockSpec((tm, d), lambda i: (i, 0)),
                  pl.BlockSpec((tm, ka), lambda i: (i, 0)),
                  pl.BlockSpec((tm, kb), lambda i: (i, 0)),
                  pl.BlockSpec((ka, d), lambda i: (0, 0)),
                  pl.BlockSpec((kb, d), lambda i: (0, 0))],
        out_specs=pl.BlockSpec((tm, d), lambda i: (i, 0)),
        compiler_params=_cparams(("parallel",)),
        name="proj_residual",
    )(x, a, b, wa, wb)


def _rmsnorm_kernel(x_ref, g_ref, o_ref):
    x = x_ref[...]
    ms = jnp.mean(x * x, axis=-1, keepdims=True)
    o_ref[...] = x * lax.rsqrt(ms + NORM_EPS) * g_ref[...]


def rmsnorm_call(x, g):
    t, d = x.shape
    tm = _row_tile(t)
    return pl.pallas_call(
        _rmsnorm_kernel,
        out_shape=jax.ShapeDtypeStruct((t, d), F32),
        grid=(t // tm,),
        in_specs=[pl.BlockSpec((tm, d), lambda i: (i, 0)), pl.BlockSpec((1, d), lambda i: (0, 0))],
        out_specs=pl.BlockSpec((tm, d), lambda i: (i, 0)),
        compiler_params=_cparams(("parallel",)),
        name="final_rmsnorm",
    )(x, g.reshape(1, d))


ROUTER_LANES = 128


def _route(xn, rw, rb, lane, n_exp):
    logits = _dot(xn, rw, HI) + rb
    lg = jnp.where(lane < N_GROUPS, logits, -jnp.inf)
    gm = jnp.max(lg, axis=-1, keepdims=True)
    top_pg = 1.0 / jnp.sum(jnp.exp(lg - gm), axis=-1, keepdims=True)
    grp = jnp.min(jnp.where(lg == gm, lane, ROUTER_LANES), axis=-1, keepdims=True)
    in_grp = (lane >= N_GROUPS) & (lane < N_GROUPS + n_exp) & (((lane - N_GROUPS) // EXP_PER_GROUP) == grp)
    le = jnp.where(in_grp, logits, -jnp.inf)
    t1 = jnp.max(le, axis=-1, keepdims=True)
    i1 = jnp.min(jnp.where(le == t1, lane, ROUTER_LANES), axis=-1, keepdims=True)
    le2 = jnp.where(lane == i1, -jnp.inf, le)
    t2 = jnp.max(le2, axis=-1, keepdims=True)
    i2 = jnp.min(jnp.where(le2 == t2, lane, ROUTER_LANES), axis=-1, keepdims=True)
    ex = jnp.exp(t2 - t1)
    w1 = 1.0 / (1.0 + ex)
    return i1, i2, w1 * top_pg, (ex * w1) * top_pg


def _moe_kernel(x_ref, g_ref, rw_ref, rb_ref, wg_ref, wu_ref, wd_ref, o_ref, xn_scr, gate_scr, acc_scr, *, n_exp):
    e = pl.program_id(1)
    tm = x_ref.shape[0]
    lane = lax.broadcasted_iota(jnp.int32, (tm, ROUTER_LANES), 1)

    @pl.when(e == 0)
    def _():
        x = x_ref[...]
        ms = jnp.mean(x * x, axis=-1, keepdims=True)
        xn = x * lax.rsqrt(ms + NORM_EPS) * g_ref[...]
        xn_scr[...] = xn.astype(BF16)
        i1, i2, g1, g2 = _route(xn, rw_ref[...], rb_ref[...], lane, n_exp)
        gate_scr[...] = jnp.where(lane == i1, g1, 0.0) + jnp.where(lane == i2, g2, 0.0)
        acc_scr[...] = jnp.zeros_like(acc_scr)

    xn = xn_scr[...]
    hg = _dot(xn, wg_ref[0])
    hu = _dot(xn, wu_ref[0])
    gcol = jnp.sum(jnp.where(lane == e + N_GROUPS, gate_scr[...], 0.0), axis=-1, keepdims=True)
    hid = hg * _sigmoid(hg) * hu * gcol
    acc_scr[...] += _dot(hid.astype(BF16), wd_ref[0])

    @pl.when(e == n_exp - 1)
    def _():
        o_ref[...] = x_ref[...] + acc_scr[...]


def moe_layer(x, g, rw, rb, wg, wu, wd):
    t, d = x.shape
    n_exp, _, f = wg.shape
    tm = _row_tile(t)
    return pl.pallas_call(
        functools.partial(_moe_kernel, n_exp=n_exp),
        out_shape=jax.ShapeDtypeStruct((t, d), F32),
        grid=(t // tm, n_exp),
        in_specs=[pl.BlockSpec((tm, d), lambda i, e: (i, 0)),
                  pl.BlockSpec((1, d), lambda i, e: (0, 0)),
                  pl.BlockSpec((d, ROUTER_LANES), lambda i, e: (0, 0)),
                  pl.BlockSpec((1, ROUTER_LANES), lambda i, e: (0, 0)),
                  pl.BlockSpec((1, d, f), lambda i, e: (e, 0, 0)),
                  pl.BlockSpec((1, d, f), lambda i, e: (e, 0, 0)),
                  pl.BlockSpec((1, f, d), lambda i, e: (e, 0, 0))],
        out_specs=pl.BlockSpec((tm, d), lambda i, e: (i, 0)),
        scratch_shapes=[pltpu.VMEM((tm, d), BF16), pltpu.VMEM((tm, ROUTER_LANES), F32), pltpu.VMEM((tm, d), F32)],
        compiler_params=_cparams(("parallel", "arbitrary")),
        name="hier_moe",
    )(x, g.reshape(1, d), rw, rb, wg, wu, wd)


MOE_ROW_TILE = 512
MOE_COPY_CHUNK = 256
MOE_COMBINE_TILE = 256


def _router_kernel(x_ref, g_ref, rw_ref, rb_ref, tri_ref, gate_ref, info_ref, cnt_ref, base_scr, *, n_exp, n_tiles):
    i = pl.program_id(0)
    tm = x_ref.shape[0]
    lane = lax.broadcasted_iota(jnp.int32, (tm, ROUTER_LANES), 1)

    @pl.when(i == 0)
    def _():
        base_scr[...] = jnp.zeros_like(base_scr)

    x = x_ref[...]
    ms = jnp.mean(x * x, axis=-1, keepdims=True)
    xn = x * lax.rsqrt(ms + NORM_EPS) * g_ref[...]
    i1, i2, g1, g2 = _route(xn, rw_ref[...], rb_ref[...], lane, n_exp)
    chosen = jnp.where((lane == i1) | (lane == i2), 1.0, 0.0)
    before = _dot(tri_ref[...], chosen.astype(BF16)) + base_scr[...]
    r1 = jnp.sum(jnp.where(lane == i1, before, 0.0), axis=-1, keepdims=True)
    r2 = jnp.sum(jnp.where(lane == i2, before, 0.0), axis=-1, keepdims=True)
    base_scr[...] += jnp.sum(chosen, axis=0, keepdims=True)
    gate_ref[...] = jnp.where(lane == 0, g1, 0.0) + jnp.where(lane == 1, g2, 0.0)
    e1 = (i1 - N_GROUPS).astype(F32)
    e2 = (i2 - N_GROUPS).astype(F32)
    info_ref[...] = (jnp.where(lane == 0, e1, 0.0) + jnp.where(lane == 1, e2, 0.0)
                     + jnp.where(lane == 2, r1, 0.0) + jnp.where(lane == 3, r2, 0.0))

    @pl.when(i == n_tiles - 1)
    def _():
        cnt_ref[...] = base_scr[...]


def _scatter_rows_kernel(pos0_ref, pos1_ref, x_hbm, xs_in_hbm, xs_hbm, sem, *, n_chunks, CH):
    del xs_in_hbm

    def issue(c, slot):
        def body(r, carry):
            t = c * CH + r
            src = x_hbm.at[pl.ds(t, 1), :]
            pltpu.make_async_copy(src, xs_hbm.at[pl.ds(pos0_ref[t], 1), :], sem.at[slot]).start()
            pltpu.make_async_copy(src, xs_hbm.at[pl.ds(pos1_ref[t], 1), :], sem.at[slot]).start()
            return carry
        lax.fori_loop(0, CH, body, 0, unroll=8)

    def drain(slot):
        pltpu.make_async_copy(x_hbm.at[pl.ds(0, 2 * CH), :], xs_hbm.at[pl.ds(0, 2 * CH), :], sem.at[slot]).wait()

    issue(0, 0)

    def outer(c, carry):
        issue(c, c % 2)
        drain((c - 1) % 2)
        return carry

    lax.fori_loop(1, n_chunks, outer, 0)
    drain((n_chunks - 1) % 2)


def _expert_kernel(te_ref, nv_ref, xs_ref, g_ref, wg_ref, wu_ref, wd_ref, y_ref):
    @pl.when(pl.program_id(0) < nv_ref[0])
    def _():
        x = xs_ref[...]
        ms = jnp.mean(x * x, axis=-1, keepdims=True)
        xn = (x * lax.rsqrt(ms + NORM_EPS) * g_ref[...]).astype(BF16)
        hg = _dot(xn, wg_ref[0])
        hu = _dot(xn, wu_ref[0])
        hid = hg * _sigmoid(hg) * hu
        y_ref[...] = _dot(hid.astype(BF16), wd_ref[0])

    @pl.when(pl.program_id(0) >= nv_ref[0])
    def _():
        y_ref[...] = jnp.zeros_like(y_ref)


def _combine_kernel(pos0_ref, pos1_ref, x_ref, gate_ref, y_hbm, o_ref, ybuf, sem, *, TC, n_tiles):
    i = pl.program_id(0)

    def issue(tile, slot):
        def body(r, carry):
            t = tile * TC + r
            pltpu.make_async_copy(y_hbm.at[pl.ds(pos0_ref[t], 1), :], ybuf.at[slot, 0, pl.ds(r, 1), :],
                                  sem.at[slot]).start()
            pltpu.make_async_copy(y_hbm.at[pl.ds(pos1_ref[t], 1), :], ybuf.at[slot, 1, pl.ds(r, 1), :],
                                  sem.at[slot]).start()
            return carry
        lax.fori_loop(0, TC, body, 0, unroll=8)

    @pl.when(i == 0)
    def _():
        issue(0, 0)

    @pl.when(i + 1 < n_tiles)
    def _():
        issue(i + 1, (i + 1) % 2)

    slot = i % 2
    pltpu.make_async_copy(y_hbm.at[pl.ds(0, TC), :], ybuf.at[slot, 0], sem.at[slot]).wait()
    pltpu.make_async_copy(y_hbm.at[pl.ds(0, TC), :], ybuf.at[slot, 1], sem.at[slot]).wait()
    gate = gate_ref[...]
    o_ref[...] = x_ref[...] + gate[:, 0:1] * ybuf[slot, 0] + gate[:, 1:2] * ybuf[slot, 1]


def moe_layer_sparse(x, g, rw, rb, wg, wu, wd):
    t, d = x.shape
    n_exp, _, f = wg.shape
    TM = MOE_ROW_TILE
    n_tiles = t // TM
    tri = jnp.asarray(np.tril(np.ones((TM, TM), np.float32), -1)).astype(BF16)
    gate, info, cnt = pl.pallas_call(
        functools.partial(_router_kernel, n_exp=n_exp, n_tiles=n_tiles),
        out_shape=[jax.ShapeDtypeStruct((t, ROUTER_LANES), F32), jax.ShapeDtypeStruct((t, ROUTER_LANES), F32),
                   jax.ShapeDtypeStruct((1, ROUTER_LANES), F32)],
        grid=(n_tiles,),
        in_specs=[pl.BlockSpec((TM, d), lambda i: (i, 0)),
                  pl.BlockSpec((1, d), lambda i: (0, 0)),
                  pl.BlockSpec((d, ROUTER_LANES), lambda i: (0, 0)),
                  pl.BlockSpec((1, ROUTER_LANES), lambda i: (0, 0)),
                  pl.BlockSpec((TM, TM), lambda i: (0, 0))],
        out_specs=[pl.BlockSpec((TM, ROUTER_LANES), lambda i: (i, 0)),
                   pl.BlockSpec((TM, ROUTER_LANES), lambda i: (i, 0)),
                   pl.BlockSpec((1, ROUTER_LANES), lambda i: (0, 0))],
        scratch_shapes=[pltpu.VMEM((1, ROUTER_LANES), F32)],
        compiler_params=_cparams(("arbitrary",)),
        name="moe_router",
    )(x, g.reshape(1, d), rw, rb, tri)

    counts = cnt[0, N_GROUPS:N_GROUPS + n_exp].astype(jnp.int32)
    padded = ((counts + TM - 1) // TM) * TM
    ends = jnp.cumsum(padded)
    offs = ends - padded
    eid = info[:, 0:2].astype(jnp.int32)
    rank = info[:, 2:4].astype(jnp.int32)
    pos = jnp.sum(jnp.where(eid[:, :, None] == jnp.arange(n_exp)[None, None, :], offs[None, None, :], 0), axis=-1) + rank
    pos0, pos1 = pos[:, 0], pos[:, 1]
    max_tiles = (2 * t) // TM + n_exp
    n_valid = (ends[-1] // TM).astype(jnp.int32).reshape(1)
    tile_exp = jnp.minimum(jnp.sum((ends[None, :] // TM) <= jnp.arange(max_tiles)[:, None], axis=-1),
                           n_exp - 1).astype(jnp.int32)
    p_rows = max_tiles * TM

    CH = MOE_COPY_CHUNK
    xs = pl.pallas_call(
        functools.partial(_scatter_rows_kernel, n_chunks=t // CH, CH=CH),
        out_shape=jax.ShapeDtypeStruct((p_rows, d), F32),
        grid_spec=pltpu.PrefetchScalarGridSpec(
            num_scalar_prefetch=2, grid=(1,),
            in_specs=[pl.BlockSpec(memory_space=pl.ANY), pl.BlockSpec(memory_space=pl.ANY)],
            out_specs=pl.BlockSpec(memory_space=pl.ANY),
            scratch_shapes=[pltpu.SemaphoreType.DMA((2,))]),
        input_output_aliases={3: 0},
        compiler_params=pltpu.CompilerParams(dimension_semantics=("arbitrary",), has_side_effects=True),
        name="moe_scatter_rows",
    )(pos0, pos1, x, jnp.zeros((p_rows, d), F32))

    def row_idx(i, te, nv):
        return (jnp.minimum(i, nv[0] - 1), 0)

    ys = pl.pallas_call(
        _expert_kernel,
        out_shape=jax.ShapeDtypeStruct((p_rows, d), F32),
        grid_spec=pltpu.PrefetchScalarGridSpec(
            num_scalar_prefetch=2, grid=(max_tiles,),
            in_specs=[pl.BlockSpec((TM, d), row_idx),
                      pl.BlockSpec((1, d), lambda i, te, nv: (0, 0)),
                      pl.BlockSpec((1, d, f), lambda i, te, nv: (te[i], 0, 0)),
                      pl.BlockSpec((1, d, f), lambda i, te, nv: (te[i], 0, 0)),
                      pl.BlockSpec((1, f, d), lambda i, te, nv: (te[i], 0, 0))],
            out_specs=pl.BlockSpec((TM, d), lambda i, te, nv: (i, 0))),
        compiler_params=_cparams(("arbitrary",)),
        name="moe_experts",
    )(tile_exp, n_valid, xs, g.reshape(1, d), wg, wu, wd)

    TC = MOE_COMBINE_TILE
    return pl.pallas_call(
        functools.partial(_combine_kernel, TC=TC, n_tiles=t // TC),
        out_shape=jax.ShapeDtypeStruct((t, d), F32),
        grid_spec=pltpu.PrefetchScalarGridSpec(
            num_scalar_prefetch=2, grid=(t // TC,),
            in_specs=[pl.BlockSpec((TC, d), lambda i, p0, p1: (i, 0)),
                      pl.BlockSpec((TC, ROUTER_LANES), lambda i, p0, p1: (i, 0)),
                      pl.BlockSpec(memory_space=pl.ANY)],
            out_specs=pl.BlockSpec((TC, d), lambda i, p0, p1: (i, 0)),
            scratch_shapes=[pltpu.VMEM((2, 2, TC, d), F32), pltpu.SemaphoreType.DMA((2,))]),
        compiler_params=_cparams(("arbitrary",)),
        name="moe_combine",
    )(pos0, pos1, x, gate, ys)


def _rwkv_kernel(p_ref, prev_ref, s0_ref, mu_ref, w0_ref, wd_ref, a0_ref, wa_ref, wg_ref, kk_ref, ka_ref,
                 bonus_ref, lng_ref, lnb_ref, tri_ref, hsum_ref, ya_ref, sf_ref, s_scr, prev_scr,
                 *, NB, C, H, DH, n_chunks):
    c = pl.program_id(1)

    @pl.when(c == 0)
    def _():
        s_scr[...] = s0_ref[:, 0]
        prev_scr[...] = prev_ref[:, 0]

    DA = H * DH
    R = NB * C
    p = p_ref[...].reshape(R, p_ref.shape[-1])
    row = lax.broadcasted_iota(jnp.int32, p.shape, 0)
    shifted = pltpu.roll(p, 1, axis=0)
    for n in range(NB):
        shifted = jnp.where(row == n * C, prev_scr[n], shifted)
        prev_scr[n] = p[(n + 1) * C - 1:(n + 1) * C, :]
    xs = p + (shifted - p) * mu_ref[...]
    r = xs[:, 0:DA]
    k = xs[:, DA:2 * DA]
    v = xs[:, 2 * DA:3 * DA]
    lora = xs[:, 3 * DA:3 * DA + R_DECAY + R_ICLR]
    gd = xs[:, 3 * DA + R_DECAY + R_ICLR:3 * DA + R_DECAY + R_ICLR + R_GATE]

    w_log = -_softplus(-(w0_ref[...] + _mp_dot(_dot, jnp.tanh(lora), wd_ref[...], 3))) - 0.5
    lw = -jnp.exp(w_log)
    a = _sigmoid(a0_ref[...] + _mp_dot(_dot, lora, wa_ref[...], 3))
    g = _mp_dot(_dot, _sigmoid(gd), wg_ref[...], 3)

    kk = k * kk_ref[...]
    ss = _dot_exact_rhs(kk * kk, hsum_ref[...], 2)
    kk = kk / jnp.maximum(jnp.sqrt(ss), 1e-12)
    k2 = k * (1.0 + (a - 1.0) * ka_ref[...])
    kka = kk * a

    cum = jnp.concatenate([_dot_exact_lhs(tri_ref[...], lw[n * C:(n + 1) * C], 3) for n in range(NB)], axis=0)
    p_in = jnp.exp(cum)
    r_t = r * p_in
    a_t = kk * jnp.exp(cum - lw)
    p_inv = jnp.exp(-cum)
    b_t = kka * p_inv
    k_t = k2 * p_inv
    bonus = _dot_exact_rhs(r * k2 * bonus_ref[...], hsum_ref[...], 2) * v

    ri = lax.broadcasted_iota(jnp.int32, (C, C), 0)
    ci = lax.broadcasted_iota(jnp.int32, (C, C), 1)
    strict = ri > ci
    incl = ri >= ci
    eye = (ri == ci).astype(F32)
    n_double = max(int(math.ceil(math.log2(C))) - 1, 0)

    chains = [(n, h) for n in range(NB) for h in range(H)]

    def blk(x, n, h):
        return x[n * C:(n + 1) * C, h * DH:(h + 1) * DH]

    def bf(x):
        return x.astype(BF16)

    Bt = [bf(blk(b_t, n, h)) for n, h in chains]
    Kt = [bf(blk(k_t, n, h)) for n, h in chains]
    Vf = [blk(v, n, h) for n, h in chains]
    AR = [bf(jnp.concatenate([blk(a_t, n, h), blk(r_t, n, h)], axis=0)) for n, h in chains]
    S0 = [s_scr[n, h] for n, h in chains]
    idx = range(len(chains))
    GB = [_dot_nt(AR[i], Bt[i]) for i in idx]
    GK = [_dot_nt(AR[i], Kt[i]) for i in idx]
    ARS = [_dot_nt(AR[i], bf(S0[i])) for i in idx]
    Lm = [jnp.where(strict, GB[i][0:C], 0.0) for i in idx]
    Gb = [bf(jnp.where(incl, GB[i][C:2 * C], 0.0)) for i in idx]
    MG = [bf(jnp.concatenate([jnp.where(strict, GK[i][0:C], 0.0), jnp.where(incl, GK[i][C:2 * C], 0.0)], axis=0))
          for i in idx]
    MGV = [_dot(MG[i], bf(Vf[i])) for i in idx]
    T = [eye - Lm[i] for i in idx]
    Pw = [bf(Lm[i]) for i in idx]
    for _ in range(n_double):
        Pw = [bf(_dot(Pw[i], Pw[i])) for i in idx]
        T = [T[i] + _dot(bf(T[i]), Pw[i]) for i in idx]
    U = [_dot(bf(T[i]), bf(-(ARS[i][0:C] + MGV[i][0:C]))) for i in idx]
    Y = [ARS[i][C:2 * C] + _dot(Gb[i], bf(U[i])) + MGV[i][C:2 * C] for i in idx]
    for i, (n, h) in enumerate(chains):
        UV = bf(jnp.concatenate([U[i], Vf[i]], axis=0))
        BK = jnp.concatenate([Bt[i], Kt[i]], axis=0)
        p_tot = p_in[(n + 1) * C - 1:(n + 1) * C, h * DH:(h + 1) * DH]
        s_scr[n, h] = (S0[i] + _dot_tn(UV, BK)) * p_tot

    rows = []
    for n in range(NB):
        ys = []
        for h in range(H):
            Yh = Y[n * H + h]
            yc = Yh - jnp.mean(Yh, axis=-1, keepdims=True)
            var = jnp.mean(yc * yc, axis=-1, keepdims=True)
            ys.append(yc * lax.rsqrt(var + GN_EPS))
        rows.append(jnp.concatenate(ys, axis=-1))
    y = jnp.concatenate(rows, axis=0) * lng_ref[...] + lnb_ref[...]
    ya_ref[...] = ((y + bonus) * g).reshape(NB, C, DA)

    @pl.when(c == n_chunks - 1)
    def _():
        sf_ref[:, 0] = s_scr[...]


def rwkv_mix(pa, n_batch, seq, shift_prev, wkv0, wts, n_par):
    t, ap = pa.shape
    H = wkv0.shape[1]
    DA = H * DH_A
    C = min(RWKV_CHUNK, seq)
    n_chunks = seq // C
    NB = n_par
    G = n_batch // NB
    tri = jnp.asarray(np.tril(np.ones((C, C), np.float32))).astype(BF16)
    hsum = jnp.asarray(np.kron(np.eye(H, dtype=np.float32), np.ones((DH_A, DH_A), np.float32))).astype(BF16)

    def full(shape):
        nd = len(shape)
        return pl.BlockSpec(shape, lambda b, c: (0,) * nd)

    vec = full((1, DA))
    ya, s_fin = pl.pallas_call(
        functools.partial(_rwkv_kernel, NB=NB, C=C, H=H, DH=DH_A, n_chunks=n_chunks),
        out_shape=[jax.ShapeDtypeStruct((NB, t // NB, DA), F32),
                   jax.ShapeDtypeStruct((NB, G, H, DH_A, DH_A), F32)],
        grid=(G, n_chunks),
        in_specs=[pl.BlockSpec((NB, C, ap), lambda b, c: (0, b * n_chunks + c, 0)),
                  pl.BlockSpec((NB, 1, 1, ap), lambda b, c: (0, b, 0, 0)),
                  pl.BlockSpec((NB, 1, H, DH_A, DH_A), lambda b, c: (0, b, 0, 0, 0)),
                  full((1, ap)), vec, full((R_DECAY + R_ICLR, DA)), vec, full((R_DECAY + R_ICLR, DA)),
                  full((R_GATE, DA)), vec, vec, vec, vec, vec, full((C, C)), full((DA, DA))],
        out_specs=[pl.BlockSpec((NB, C, DA), lambda b, c: (0, b * n_chunks + c, 0)),
                   pl.BlockSpec((NB, 1, H, DH_A, DH_A), lambda b, c: (0, b, 0, 0, 0))],
        scratch_shapes=[pltpu.VMEM((NB, H, DH_A, DH_A), F32), pltpu.VMEM((NB, 1, ap), F32)],
        compiler_params=_cparams(("parallel", "arbitrary")),
        name="rwkv7_mix",
    )(pa.reshape(NB, t // NB, ap), shift_prev.reshape(NB, G, 1, ap), wkv0.reshape(NB, G, H, DH_A, DH_A),
      wts["mu"], wts["w0"], wts["wd"], wts["a0"], wts["wa"],
      wts["wg"], wts["key_k"], wts["key_a"], wts["bonus"], wts["lnx_g"], wts["lnx_b"], tri, hsum)
    return ya.reshape(t, DA), s_fin.reshape(wkv0.shape)


GMLP_TILE = 128


def _gmlp_kernel(u_ref, v_ref, ng_ref, nb_ref, wm_ref, bias_ref, o_ref, vn_ref):
    vf = _gelu(v_ref[...])
    mu = jnp.mean(vf, axis=-1, keepdims=True)
    vc = vf - mu
    var = jnp.mean(vc * vc, axis=-1, keepdims=True)
    vn = vc * lax.rsqrt(var + NORM_EPS) * ng_ref[...] + nb_ref[...]
    vn_ref[...] = vn
    vb = vn.astype(BF16)
    n_h = wm_ref.shape[0]
    cb = vn.shape[1] // n_h
    s = jnp.concatenate([_dot(wm_ref[h], vb[:, h * cb:(h + 1) * cb]) for h in range(n_h)], axis=-1)
    o_ref[...] = _gelu(u_ref[...]) * (s + bias_ref[...])


def gmlp_mix(pu, pv, ng, nb, wm_bf16, bias_tile):
    t, db = pu.shape
    n_h = wm_bf16.shape[0]
    return pl.pallas_call(
        _gmlp_kernel,
        out_shape=[jax.ShapeDtypeStruct((t, db), F32), jax.ShapeDtypeStruct((t, db), F32)],
        grid=(t // GMLP_TILE,),
        in_specs=[pl.BlockSpec((GMLP_TILE, db), lambda i: (i, 0)),
                  pl.BlockSpec((GMLP_TILE, db), lambda i: (i, 0)),
                  pl.BlockSpec((1, db), lambda i: (0, 0)),
                  pl.BlockSpec((1, db), lambda i: (0, 0)),
                  pl.BlockSpec((n_h, GMLP_TILE, GMLP_TILE), lambda i: (0, 0, 0)),
                  pl.BlockSpec((GMLP_TILE, db), lambda i: (0, 0))],
        out_specs=[pl.BlockSpec((GMLP_TILE, db), lambda i: (i, 0)),
                   pl.BlockSpec((GMLP_TILE, db), lambda i: (i, 0))],
        compiler_params=_cparams(("parallel",)),
        name="gmlp_mix",
    )(pu, pv, ng, nb, wm_bf16, bias_tile)


N_SEG = 8


def _rglru_kernel(xb_ref, gy_ref, cprev_ref, h0_ref, cw_ref, cb_ref, gw_ref, gb_ref, lam_ref,
                  yc_ref, ctail_ref, hl_ref, xe_scr, a_scr, b_scr, h_scr, *, TL, DC, pos0, n_tiles):
    l = pl.program_id(1)
    PAD = 8

    @pl.when(l == 0)
    def _():
        xe_scr[0:PAD, :] = cprev_ref[0]
        h_scr[...] = h0_ref[0]

    xe_scr[PAD:PAD + TL, :] = xb_ref[...]
    xc = cb_ref[...] + xe_scr[pl.ds(PAD - (CONV_W - 1), TL), :] * cw_ref[0:1, :]
    for i in range(1, CONV_W):
        xc = xc + xe_scr[pl.ds(PAD - (CONV_W - 1) + i, TL), :] * cw_ref[i:i + 1, :]
    tail = xe_scr[TL:TL + PAD, :]
    ctail_ref[0] = tail
    xe_scr[0:PAD, :] = tail

    gates = _dot(xc.astype(BF16), gw_ref[...]) + gb_ref[...]
    rg = _sigmoid(gates[:, 0:DC])
    ig = _sigmoid(gates[:, DC:2 * DC])
    log_a = -LRU_C * rg * _softplus(-lam_ref[...])
    a = jnp.exp(log_a)
    mult = jnp.sqrt(1.0 - jnp.exp(2.0 * log_a))
    row = lax.broadcasted_iota(jnp.int32, (TL, DC), 0)
    mult = jnp.where(row + (l * TL + pos0) == 0, 1.0, mult)
    b = mult * ig * xc
    n_slab = DC // LANES
    for s in range(n_slab):
        a_scr[s] = a[:, s * LANES:(s + 1) * LANES]
        b_scr[s] = b[:, s * LANES:(s + 1) * LANES]

    seg = TL // N_SEG

    def step(i, carry):
        idx = pl.ds(i, N_SEG, stride=seg) if seg > 1 else pl.ds(0, N_SEG)
        out = []
        for s in range(n_slab):
            hloc, ap = carry[s]
            ai = a_scr[s, idx, :]
            hloc = ai * hloc + b_scr[s, idx, :]
            ap = ap * ai
            b_scr[s, idx, :] = hloc
            a_scr[s, idx, :] = ap
            out.append((hloc, ap))
        return tuple(out)

    lax.fori_loop(0, seg, step,
                  tuple((jnp.zeros((N_SEG, LANES), F32), jnp.ones((N_SEG, LANES), F32)) for _ in range(n_slab)))

    carry = h_scr[...]
    g_act = _gelu(gy_ref[...])
    for j in range(N_SEG):
        rows = slice(j * seg, (j + 1) * seg)
        hloc = jnp.concatenate([b_scr[s, rows, :] for s in range(n_slab)], axis=-1)
        ap = jnp.concatenate([a_scr[s, rows, :] for s in range(n_slab)], axis=-1)
        hj = hloc + ap * carry
        yc_ref[rows, :] = g_act[rows, :] * hj
        carry = hj[seg - 1:seg, :]
    h_scr[...] = carry

    @pl.when(l == n_tiles - 1)
    def _():
        hl_ref[0] = carry


def rglru_mix(xb, gy, n_batch, seq, conv_prev8, h0, pos0, wts):
    t, dc = xb.shape
    TL = 512 if seq % 512 == 0 else seq
    n_tiles = seq // TL

    def full(shape):
        nd = len(shape)
        return pl.BlockSpec(shape, lambda b, l: (0,) * nd)

    yc, ctail, hl = pl.pallas_call(
        functools.partial(_rglru_kernel, TL=TL, DC=dc, pos0=pos0, n_tiles=n_tiles),
        out_shape=[jax.ShapeDtypeStruct((t, dc), F32), jax.ShapeDtypeStruct((n_batch, 8, dc), F32),
                   jax.ShapeDtypeStruct((n_batch, 1, dc), F32)],
        grid=(n_batch, n_tiles),
        in_specs=[pl.BlockSpec((TL, dc), lambda b, l: (b * n_tiles + l, 0)),
                  pl.BlockSpec((TL, dc), lambda b, l: (b * n_tiles + l, 0)),
                  pl.BlockSpec((1, 8, dc), lambda b, l: (b, 0, 0)),
                  pl.BlockSpec((1, 1, dc), lambda b, l: (b, 0, 0)),
                  full((CONV_W, dc)), full((1, dc)), full((dc, 2 * dc)), full((1, 2 * dc)), full((1, dc))],
        out_specs=[pl.BlockSpec((TL, dc), lambda b, l: (b * n_tiles + l, 0)),
                   pl.BlockSpec((1, 8, dc), lambda b, l: (b, 0, 0)),
                   pl.BlockSpec((1, 1, dc), lambda b, l: (b, 0, 0))],
        scratch_shapes=[pltpu.VMEM((TL + 8, dc), F32), pltpu.VMEM((dc // LANES, TL, LANES), F32),
                        pltpu.VMEM((dc // LANES, TL, LANES), F32), pltpu.VMEM((1, dc), F32)],
        compiler_params=_cparams(("parallel", "arbitrary")),
        name="rglru_mix",
    )(xb, gy, conv_prev8, h0.reshape(n_batch, 1, dc), wts["conv_w"], wts["conv_b"], wts["gate_w"], wts["gate_b"],
      wts["lam"])
    return yc, ctail[:, 8 - (CONV_W - 1):, :], hl.reshape(n_batch, dc)


def _t5_bucket(dist):
    dist = np.asarray(dist)
    max_exact = N_BUCKETS // 2
    scaled = np.log(np.maximum(dist, 1) / max_exact) / math.log(BUCKET_MAX_DIST / max_exact)
    large = np.minimum(max_exact + (scaled * (N_BUCKETS - max_exact)).astype(np.int32), N_BUCKETS - 1)
    return np.where(dist < max_exact, dist, large).astype(np.int32)


def _dist_table(rel_bias, max_dist):
    dist = np.arange(max_dist + 1)
    count = np.zeros(max_dist + 1, np.float32)
    for window, dil in DILATED:
        count += ((dist % dil == 0) & (dist <= window)).astype(np.float32)
    logcnt = np.where(count > 0, np.log(np.maximum(count, 1.0)), 0.0).astype(np.float32)
    tab = jnp.take(rel_bias, jnp.asarray(_t5_bucket(dist)), axis=0) + jnp.asarray(logcnt)[:, None]
    return jnp.where(jnp.asarray(count > 0)[:, None], tab, NEG_BIG)


def _toeplitz_tiles(tab, n_pos, n_neg, T):
    D, H = tab.shape
    span = T * n_pos
    assert D >= span
    n_col = span + T * n_neg + T - 1
    ext = jnp.concatenate([jnp.flip(tab[:span], axis=0), jnp.full((n_col + 1 - span, H), NEG_BIG, F32)], axis=0)
    ext = jnp.transpose(ext)
    skew = jnp.tile(ext, (1, T))[:, :T * n_col].reshape(H, T, n_col)
    tiles = [skew[:, :, span - 1 - T * dd: span - 1 - T * dd + T] for dd in range(-n_neg, n_pos)]
    return jnp.stack(tiles, axis=1)


def _attn_prompt_kernel(q_ref, k_ref, v_ref, bias_ref, o_ref, kb_scr, vb_scr, *, E, SUB, NS):
    qi = pl.program_id(2)
    TQ = NS * SUB

    @pl.when(qi == 0)
    def _():
        kb_scr[...] = k_ref[0].astype(BF16)
        vb_scr[...] = v_ref[0].astype(BF16)

    lane = lax.broadcasted_iota(jnp.int32, (SUB, 2 * E), 1)
    q2 = []
    for rs in range(NS):
        q = q_ref[0, rs * SUB:(rs + 1) * SUB, :] * (E ** -0.5)
        q2.append(jnp.concatenate([jnp.where(lane < E, q, 0.0), jnp.where(lane >= E, q, 0.0)], axis=0).astype(BF16))

    def body(i, carry):
        j = qi - i
        kj = kb_scr[pl.ds(pl.multiple_of(j * TQ, TQ), TQ), :]
        vj = vb_scr[pl.ds(pl.multiple_of(j * TQ, TQ), TQ), :]
        out = []
        for rs in range(NS):
            m, l, acc = carry[rs]
            s = _dot_nt(q2[rs], kj)
            parts = []
            for cs in range(NS):
                dd = i * NS + (rs - cs + NS - 1)
                bias = jnp.concatenate([bias_ref[0, dd], bias_ref[1, dd]], axis=0)
                parts.append(s[:, cs * SUB:(cs + 1) * SUB] + bias)
            mx = parts[0]
            for part in parts[1:]:
                mx = jnp.maximum(mx, part)
            m_new = jnp.maximum(m, jnp.max(mx, axis=-1, keepdims=True))
            alpha = jnp.exp(m - m_new)
            ps = [jnp.exp(part - m_new) for part in parts]
            psum = ps[0]
            for pexp in ps[1:]:
                psum = psum + pexp
            l = alpha * l + psum
            acc = alpha * acc + _dot(jnp.concatenate(ps, axis=-1).astype(BF16), vj)
            out.append((m_new, l, acc))
        return tuple(out)

    init = tuple((jnp.full((2 * SUB, SUB), NEG_BIG, F32), jnp.zeros((2 * SUB, SUB), F32),
                  jnp.zeros((2 * SUB, 2 * E), F32)) for _ in range(NS))
    res = lax.fori_loop(0, qi + 1, body, init)
    for rs in range(NS):
        m, l, acc = res[rs]
        o = acc / jnp.sum(l, axis=-1, keepdims=True)
        o_ref[0, rs * SUB:(rs + 1) * SUB, :] = jnp.where(lane < E, o[0:SUB], o[SUB:2 * SUB])


def attn_prompt(q, k, v, bias_tiles, n_batch, seq):
    hd = q.shape[-1]
    E = hd // H_D
    SUB = ATT_TILE
    NS = ATT_SUBTILES
    TQ = SUB * NS
    nq = seq // TQ
    nt = bias_tiles.shape[1]
    return pl.pallas_call(
        functools.partial(_attn_prompt_kernel, E=E, SUB=SUB, NS=NS),
        out_shape=jax.ShapeDtypeStruct((n_batch, seq, hd), F32),
        grid=(H_D // 2, n_batch, nq),
        in_specs=[pl.BlockSpec((1, TQ, 2 * E), lambda hp, b, i: (b, i, hp)),
                  pl.BlockSpec((1, seq, 2 * E), lambda hp, b, i: (b, 0, hp)),
                  pl.BlockSpec((1, seq, 2 * E), lambda hp, b, i: (b, 0, hp)),
                  pl.BlockSpec((2, nt, SUB, SUB), lambda hp, b, i: (hp, 0, 0, 0))],
        out_specs=pl.BlockSpec((1, TQ, 2 * E), lambda hp, b, i: (b, i, hp)),
        scratch_shapes=[pltpu.VMEM((seq, 2 * E), BF16), pltpu.VMEM((seq, 2 * E), BF16)],
        compiler_params=_cparams(("arbitrary", "arbitrary", "arbitrary")),
        name="dilated_attn_prompt",
    )(q, k, v, bias_tiles)


def _attn_sample_kernel(q_ref, kn_ref, vn_ref, ck_ref, cv_ref, bo_ref, bn_ref, o_ref, *, E, S):
    lane = lax.broadcasted_iota(jnp.int32, (S, 2 * E), 1)
    NPAD = bn_ref.shape[-1]
    outs = []
    for hp in range(H_D // 2):
        sl = slice(hp * 2 * E, (hp + 1) * 2 * E)
        q = q_ref[0, :, sl] * (E ** -0.5)
        q2 = jnp.concatenate([jnp.where(lane < E, q, 0.0), jnp.where(lane >= E, q, 0.0)], axis=0).astype(BF16)
        zpad = jnp.zeros((NPAD - S, 2 * E), F32)
        kn = jnp.concatenate([kn_ref[0, :, sl], zpad], axis=0).astype(BF16)
        vn = jnp.concatenate([vn_ref[0, :, sl], zpad], axis=0).astype(BF16)
        s_old = _dot_nt(q2, ck_ref[0, :, sl].astype(BF16)) + jnp.concatenate([bo_ref[2 * hp], bo_ref[2 * hp + 1]], axis=0)
        s_new = _dot_nt(q2, kn) + jnp.concatenate([bn_ref[2 * hp], bn_ref[2 * hp + 1]], axis=0)
        m = jnp.maximum(jnp.max(s_old, axis=-1, keepdims=True), jnp.max(s_new, axis=-1, keepdims=True))
        p_old = jnp.exp(s_old - m)
        p_new = jnp.exp(s_new - m)
        l = jnp.sum(p_old, axis=-1, keepdims=True) + jnp.sum(p_new, axis=-1, keepdims=True)
        acc = _dot(p_old.astype(BF16), cv_ref[0, :, sl].astype(BF16)) + _dot(p_new.astype(BF16), vn)
        o = acc / l
        outs.append(jnp.where(lane < E, o[0:S], o[S:2 * S]))
    o_ref[0] = jnp.concatenate(outs, axis=-1)


def attn_sample(q, k_new, v_new, cache_k, cache_v, bias_old, bias_new):
    n_batch, S, hd = q.shape
    W = cache_k.shape[1]
    E = hd // H_D
    NPAD = bias_new.shape[-1]
    return pl.pallas_call(
        functools.partial(_attn_sample_kernel, E=E, S=S),
        out_shape=jax.ShapeDtypeStruct((n_batch, S, hd), F32),
        grid=(n_batch,),
        in_specs=[pl.BlockSpec((1, S, hd), lambda b: (b, 0, 0)),
                  pl.BlockSpec((1, S, hd), lambda b: (b, 0, 0)),
                  pl.BlockSpec((1, S, hd), lambda b: (b, 0, 0)),
                  pl.BlockSpec((1, W, hd), lambda b: (b, 0, 0)),
                  pl.BlockSpec((1, W, hd), lambda b: (b, 0, 0)),
                  pl.BlockSpec((H_D, S, W), lambda b: (0, 0, 0)),
                  pl.BlockSpec((H_D, S, NPAD), lambda b: (0, 0, 0))],
        out_specs=pl.BlockSpec((1, S, hd), lambda b: (b, 0, 0)),
        compiler_params=_cparams(("parallel",)),
        name="dilated_attn_sample",
    )(q, k_new, v_new, cache_k, cache_v, bias_old, bias_new)


def _even_weights(j, w_in_even, w_out_even, shift_mu, decay_w0, decay_up, iclr_a0, iclr_up, gate_up, key_k, key_a,
                  bonus_r_k, lnx_g, lnx_b, sgu_norm_g, sgu_norm_b, sgu_w, sgu_b):
    da = decay_w0.shape[1]
    zeros_d = jnp.zeros((R_ICLR, da), F32)
    zeros_i = jnp.zeros((R_DECAY, da), F32)
    return dict(
        w_in=w_in_even[j].astype(BF16),
        w_out_a=w_out_even[j, :da].astype(BF16), w_out_b=w_out_even[j, da:].astype(BF16),
        mu=shift_mu[j].reshape(1, -1), w0=decay_w0[j].reshape(1, -1), a0=iclr_a0[j].reshape(1, -1),
        wd=jnp.concatenate([decay_up[j], zeros_d], axis=0), wa=jnp.concatenate([zeros_i, iclr_up[j]], axis=0),
        wg=gate_up[j], key_k=key_k[j].reshape(1, -1), key_a=key_a[j].reshape(1, -1),
        bonus=bonus_r_k[j].reshape(1, -1), lnx_g=lnx_g[j].reshape(1, -1), lnx_b=lnx_b[j].reshape(1, -1),
        ng=sgu_norm_g[j].reshape(1, -1), nb=sgu_norm_b[j].reshape(1, -1), sgu_w=sgu_w[j], sgu_b=sgu_b[j])


def _gmlp_tables(sgu_w, sgu_b, chunk):
    reps = GMLP_TILE // chunk
    n_h = sgu_w.shape[0]
    cb = None
    wm = sgu_w[:, :chunk, :chunk] * jnp.asarray(np.tril(np.ones((chunk, chunk), np.float32)))
    if reps > 1:
        eye = jnp.asarray(np.eye(reps, dtype=np.float32))
        wm = jnp.einsum("ab,hts->hatbs", eye, wm).reshape(n_h, GMLP_TILE, GMLP_TILE)
    bias = jnp.tile(jnp.transpose(sgu_b[:, :chunk]), (reps, 1))
    return wm.astype(BF16), bias


def _even_layer(x, n_batch, seq, chunk, n_par, shift_prev, wkv0, norm_g, ew):
    pa, pu, pv = norm_matmul(x, norm_g, ew["w_in"], (ew["mu"].shape[1], ew["ng"].shape[1], ew["ng"].shape[1]))
    ya, wkv = rwkv_mix(pa, n_batch, seq, shift_prev, wkv0, ew, n_par)
    wm, bias = _gmlp_tables(ew["sgu_w"], ew["sgu_b"], chunk)
    cb = pu.shape[1] // wm.shape[0]
    bias_tile = jnp.repeat(bias, cb, axis=1)
    yb, vn = gmlp_mix(pu, pv, ew["ng"], ew["nb"], wm, bias_tile)
    x = proj_residual(x, ya, yb, ew["w_out_a"], ew["w_out_b"])
    last = pa.reshape(n_batch, seq, -1)[:, -1]
    return x, last, wkv, vn


def _odd_weights(j, w_in_odd, w_out_odd, conv_w, conv_b, rgate_w, rgate_b, igate_w, igate_b, lru_lambda):
    dc = conv_b.shape[1]
    eye = jnp.asarray(np.eye(H_C, dtype=np.float32))

    def blockdiag(w):
        dh = w.shape[-1]
        return jnp.einsum("ab,aij->aibj", eye, w).reshape(H_C * dh, H_C * dh)

    return dict(
        w_in=w_in_odd[j].astype(BF16),
        w_out_c=w_out_odd[j, :dc].astype(BF16), w_out_d=w_out_odd[j, dc:].astype(BF16),
        conv_w=conv_w[j], conv_b=conv_b[j].reshape(1, -1),
        gate_w=jnp.concatenate([blockdiag(rgate_w[j]), blockdiag(igate_w[j])], axis=1).astype(BF16),
        gate_b=jnp.concatenate([rgate_b[j], igate_b[j]]).reshape(1, -1),
        lam=lru_lambda[j].reshape(1, -1))


def _odd_layer(x, n_batch, seq, conv_prev, h0, pos0, caches, dist_tab, norm_g, ow):
    dc = ow["lam"].shape[1]
    gy, xb, q, k, v = norm_matmul(x, norm_g, ow["w_in"], (dc,) * 5)
    conv_prev8 = jnp.pad(conv_prev, ((0, 0), (8 - (CONV_W - 1), 0), (0, 0)))
    yc, conv_last, h_last = rglru_mix(xb, gy, n_batch, seq, conv_prev8, h0, pos0, ow)
    hd = q.shape[1]
    q3, k3, v3 = (a.reshape(n_batch, seq, hd) for a in (q, k, v))
    if caches is None:
        tiles = _toeplitz_tiles(dist_tab, seq // ATT_TILE, ATT_SUBTILES - 1, ATT_TILE)
        o = attn_prompt(q3, k3, v3, tiles, n_batch, seq)
    else:
        cache_k, cache_v = caches
        W = cache_k.shape[1]
        ck = cache_k.reshape(n_batch, W, hd)
        cv = cache_v.reshape(n_batch, W, hd)
        NPAD = 128
        d_old = W + np.arange(seq)[:, None] - np.arange(W)[None, :]
        b_old = jnp.transpose(jnp.take(dist_tab, jnp.asarray(d_old), axis=0), (2, 0, 1))
        d_new = np.arange(seq)[:, None] - np.arange(NPAD)[None, :]
        ok_new = (d_new >= 0) & (np.arange(NPAD)[None, :] < seq)
        b_new = jnp.take(dist_tab, jnp.asarray(np.maximum(d_new, 0)), axis=0)
        b_new = jnp.transpose(jnp.where(jnp.asarray(ok_new)[..., None], b_new, NEG_BIG), (2, 0, 1))
        o = attn_sample(q3, k3, v3, ck, cv, b_old, b_new)
    x = proj_residual(x, yc, o.reshape(n_batch * seq, hd), ow["w_out_c"], ow["w_out_d"])
    e = hd // H_D
    return x, conv_last, h_last, k3.reshape(n_batch, seq, H_D, e), v3.reshape(n_batch, seq, H_D, e)


def _moe_weights(l, router_group_w, router_group_b, router_expert_w, router_expert_b, exp_w_gate, exp_w_up,
                 exp_w_down):
    d = router_group_w.shape[1]
    n_used = N_GROUPS + router_expert_w.shape[2]
    rw = jnp.concatenate([router_group_w[l], router_expert_w[l], jnp.zeros((d, ROUTER_LANES - n_used), F32)], axis=1)
    rb = jnp.concatenate([router_group_b[l], router_expert_b[l], jnp.zeros((ROUTER_LANES - n_used,), F32)])
    return dict(rw=rw, rb=rb.reshape(1, -1), wg=exp_w_gate[l].astype(BF16), wu=exp_w_up[l].astype(BF16),
                wd=exp_w_down[l].astype(BF16))


def kernel(x_prompt, x_sample, state_wkv, state_shift, state_conv, state_rglru, cache_k, cache_v, norm_mix, norm_ffn, norm_final, w_in_even, w_out_even, shift_mu, decay_w0, decay_up, iclr_a0, iclr_up, gate_up, key_k, key_a, bonus_r_k, lnx_g, lnx_b, sgu_norm_g, sgu_norm_b, sgu_w, sgu_b, w_in_odd, w_out_odd, conv_w, conv_b, rgate_w, rgate_b, igate_w, igate_b, lru_lambda, rel_bias, router_group_w, router_group_b, router_expert_w, router_expert_b, exp_w_gate, exp_w_up, exp_w_down):
    B, L, D = x_prompt.shape
    DB, S, _ = x_sample.shape
    depth = norm_mix.shape[0]
    xp = x_prompt.reshape(B * L, D)
    xs = x_sample.reshape(DB * S, D)
    W = cache_k.shape[2]
    dist_tab = _dist_table(rel_bias, max(L, W + S) - 1)

    wkv_p, shift_p, conv_p, lru_p, k_p, v_p = [], [], [], [], [], []
    wkv_s, shift_s, chunkv_s, conv_s, lru_s, k_s, v_s = [], [], [], [], [], [], []
    for l in range(depth):
        j = l // 2
        if l % 2 == 0:
            ew = _even_weights(j, w_in_even, w_out_even, shift_mu, decay_w0, decay_up, iclr_a0, iclr_up, gate_up,
                               key_k, key_a, bonus_r_k, lnx_g, lnx_b, sgu_norm_g, sgu_norm_b, sgu_w, sgu_b)
            a_proj = ew["mu"].shape[1]
            h_a = state_wkv.shape[2]
            xp, sh, wkv, _ = _even_layer(xp, B, L, GMLP_TILE, RWKV_PAR_PROMPT, jnp.zeros((B, a_proj), F32),
                                         jnp.zeros((B, h_a, DH_A, DH_A), F32), norm_mix[l], ew)
            xs, sh_s, wkv_s_new, vn_s = _even_layer(xs, DB, S, S, RWKV_PAR_SAMPLE, state_shift[j], state_wkv[j], norm_mix[l], ew)
            wkv_p.append(wkv)
            shift_p.append(sh)
            wkv_s.append(wkv_s_new)
            shift_s.append(sh_s)
            chunkv_s.append(vn_s.reshape(DB, S, -1))
        else:
            ow = _odd_weights(j, w_in_odd, w_out_odd, conv_w, conv_b, rgate_w, rgate_b, igate_w, igate_b, lru_lambda)
            dc = ow["lam"].shape[1]
            xp, cv, hl, kr, vr = _odd_layer(xp, B, L, jnp.zeros((B, CONV_W - 1, dc), F32), jnp.zeros((B, dc), F32),
                                            0, None, dist_tab, norm_mix[l], ow)
            xs, cv_s, hl_s, kr_s, vr_s = _odd_layer(xs, DB, S, state_conv[j], state_rglru[j], PAST_LEN,
                                                    (cache_k[j], cache_v[j]), dist_tab, norm_mix[l], ow)
            conv_p.append(cv)
            lru_p.append(hl)
            k_p.append(kr)
            v_p.append(vr)
            conv_s.append(cv_s)
            lru_s.append(hl_s)
            k_s.append(kr_s)
            v_s.append(vr_s)
        mw = _moe_weights(l, router_group_w, router_group_b, router_expert_w, router_expert_b, exp_w_gate, exp_w_up,
                          exp_w_down)
        xp = moe_layer_sparse(xp, norm_ffn[l], mw["rw"], mw["rb"], mw["wg"], mw["wu"], mw["wd"])
        xs = moe_layer(xs, norm_ffn[l], mw["rw"], mw["rb"], mw["wg"], mw["wu"], mw["wd"])
    y_prompt = rmsnorm_call(xp, norm_final).reshape(B, L, D)
    y_sample = rmsnorm_call(xs, norm_final).reshape(DB, S, D)
    return (y_prompt, y_sample,
            jnp.stack(wkv_p), jnp.stack(shift_p), jnp.stack(conv_p), jnp.stack(lru_p), jnp.stack(k_p), jnp.stack(v_p),
            jnp.stack(wkv_s), jnp.stack(shift_s), jnp.stack(chunkv_s), jnp.stack(conv_s), jnp.stack(lru_s),
            jnp.stack(k_s), jnp.stack(v_s))
```

```python
import functools
import math

import numpy as np
import jax
import jax.numpy as jnp
from jax import lax
from jax.experimental import pallas as pl
from jax.experimental.pallas import tpu as pltpu

F32 = jnp.float32
BF16 = jnp.bfloat16
HI = lax.Precision.HIGHEST

PAST_LEN = 8192
DH_A = 64
R_DECAY = 64
R_ICLR = 64
R_GATE = 128
GN_EPS = 64e-5
H_B = 4
H_C = 8
CONV_W = 4
LRU_C = 8.0
H_D = 8
DILATED = ((128, 1), (512, 4), (2048, 16))
N_BUCKETS = 32
BUCKET_MAX_DIST = 2048
NEG_BIG = -1e30
N_GROUPS = 4
EXP_PER_GROUP = 4
NORM_EPS = 1e-6

VMEM_LIMIT = 56 * 1024 * 1024
RWKV_CHUNK = 64
RWKV_PAR_PROMPT = 2
RWKV_PAR_SAMPLE = 8
ATT_TILE = 128
ATT_SUBTILES = 4
LANES = 128


def _cparams(sem):
    return pltpu.CompilerParams(dimension_semantics=sem, vmem_limit_bytes=VMEM_LIMIT)


def _dot(a, b, precision=None):
    return jnp.dot(a, b, preferred_element_type=F32, precision=precision)


def _dot_nt(a, b, precision=None):
    return lax.dot_general(a, b, (((1,), (1,)), ((), ())), preferred_element_type=F32, precision=precision)


def _dot_tn(a, b, precision=None):
    return lax.dot_general(a, b, (((0,), (0,)), ((), ())), preferred_element_type=F32, precision=precision)


def _split_bf16(x, n):
    parts = []
    for _ in range(n):
        hi = x.astype(BF16)
        parts.append(hi)
        x = x - hi.astype(F32)
    return parts


def _mp_dot(dotfn, a, b, passes):
    if passes == 1:
        return dotfn(a.astype(BF16), b.astype(BF16))
    a_hi, a_lo = _split_bf16(a, 2)
    b_hi, b_lo = _split_bf16(b, 2)
    return dotfn(a_hi, b_hi) + (dotfn(a_hi, b_lo) + dotfn(a_lo, b_hi))


def _dot_exact_rhs(a, b_bf16, n_split):
    parts = _split_bf16(a, n_split)
    acc = _dot(parts[0], b_bf16)
    for part in parts[1:]:
        acc = acc + _dot(part, b_bf16)
    return acc


def _dot_exact_lhs(a_bf16, b, n_split):
    parts = _split_bf16(b, n_split)
    acc = _dot(a_bf16, parts[0])
    for part in parts[1:]:
        acc = acc + _dot(a_bf16, part)
    return acc


def _softplus(x):
    return jnp.maximum(x, 0.0) + jnp.log(1.0 + jnp.exp(-jnp.abs(x)))


def _sigmoid(x):
    return 1.0 / (1.0 + jnp.exp(-x))


def _gelu(x):
    c = math.sqrt(2.0 / math.pi)
    return 0.5 * x * (1.0 + jnp.tanh(c * (x + 0.044715 * (x * x * x))))


def _row_tile(t, pref=512):
    return pref if t % pref == 0 else t


def _norm_matmul_kernel(x_ref, g_ref, w_ref, *out_refs, splits):
    x = x_ref[...]
    ms = jnp.mean(x * x, axis=-1, keepdims=True)
    h = (x * lax.rsqrt(ms + NORM_EPS) * g_ref[...]).astype(BF16)
    off = 0
    for o_ref, n in zip(out_refs, splits):
        o_ref[...] = _dot(h, w_ref[:, off:off + n])
        off += n


def norm_matmul(x, g, w_bf16, splits):
    t, d = x.shape
    n = w_bf16.shape[1]
    tm = _row_tile(t)
    return pl.pallas_call(
        functools.partial(_norm_matmul_kernel, splits=splits),
        out_shape=[jax.ShapeDtypeStruct((t, s), F32) for s in splits],
        grid=(t // tm,),
        in_specs=[pl.BlockSpec((tm, d), lambda i: (i, 0)),
                  pl.BlockSpec((1, d), lambda i: (0, 0)),
                  pl.BlockSpec((d, n), lambda i: (0, 0))],
        out_specs=[pl.BlockSpec((tm, s), lambda i: (i, 0)) for s in splits],
        compiler_params=_cparams(("parallel",)),
        name="norm_matmul",
    )(x, g.reshape(1, d), w_bf16)


def _proj_res_kernel(x_ref, a_ref, b_ref, wa_ref, wb_ref, o_ref):
    acc = _dot(a_ref[...].astype(BF16), wa_ref[...]) + _dot(b_ref[...].astype(BF16), wb_ref[...])
    o_ref[...] = x_ref[...] + acc


def proj_residual(x, a, b, wa, wb):
    t, d = x.shape
    tm = _row_tile(t)
    ka, kb = a.shape[1], b.shape[1]
    return pl.pallas_call(
        _proj_res_kernel,
        out_shape=jax.ShapeDtypeStruct((t, d), F32),
        grid=(t // tm,),
        in_specs=[pl.BlockSpec((tm, d), lambda i: (i, 0)),
                  pl.BlockSpec((tm, ka), lambda i: (i, 0)),
                  pl.BlockSpec((tm, kb), lambda i: (i, 0)),
                  pl.BlockSpec((ka, d), lambda i: (0, 0)),
                  pl.BlockSpec((kb, d), lambda i: (0, 0))],
        out_specs=pl.BlockSpec((tm, d), lambda i: (i, 0)),
        compiler_params=_cparams(("parallel",)),
        name="proj_residual",
    )(x, a, b, wa, wb)


def _rmsnorm_kernel(x_ref, g_ref, o_ref):
    x = x_ref[...]
    ms = jnp.mean(x * x, axis=-1, keepdims=True)
    o_ref[...] = x * lax.rsqrt(ms + NORM_EPS) * g_ref[...]


def rmsnorm_call(x, g):
    t, d = x.shape
    tm = _row_tile(t)
    return pl.pallas_call(
        _rmsnorm_kernel,
        out_shape=jax.ShapeDtypeStruct((t, d), F32),
        grid=(t // tm,),
        in_specs=[pl.BlockSpec((tm, d), lambda i: (i, 0)), pl.BlockSpec((1, d), lambda i: (0, 0))],
        out_specs=pl.BlockSpec((tm, d), lambda i: (i, 0)),
        compiler_params=_cparams(("parallel",)),
        name="final_rmsnorm",
    )(x, g.reshape(1, d))


ROUTER_LANES = 128


def _route(xn, rw, rb, lane, n_exp):
    logits = _dot(xn, rw, HI) + rb
    lg = jnp.where(lane < N_GROUPS, logits, -jnp.inf)
    gm = jnp.max(lg, axis=-1, keepdims=True)
    top_pg = 1.0 / jnp.sum(jnp.exp(lg - gm), axis=-1, keepdims=True)
    grp = jnp.min(jnp.where(lg == gm, lane, ROUTER_LANES), axis=-1, keepdims=True)
    in_grp = (lane >= N_GROUPS) & (lane < N_GROUPS + n_exp) & (((lane - N_GROUPS) // EXP_PER_GROUP) == grp)
    le = jnp.where(in_grp, logits, -jnp.inf)
    t1 = jnp.max(le, axis=-1, keepdims=True)
    i1 = jnp.min(jnp.where(le == t1, lane, ROUTER_LANES), axis=-1, keepdims=True)
    le2 = jnp.where(lane == i1, -jnp.inf, le)
    t2 = jnp.max(le2, axis=-1, keepdims=True)
    i2 = jnp.min(jnp.where(le2 == t2, lane, ROUTER_LANES), axis=-1, keepdims=True)
    ex = jnp.exp(t2 - t1)
    w1 = 1.0 / (1.0 + ex)
    return i1, i2, w1 * top_pg, (ex * w1) * top_pg


def _moe_kernel(x_ref, g_ref, rw_ref, rb_ref, wg_ref, wu_ref, wd_ref, o_ref, xn_scr, gate_scr, acc_scr, *, n_exp):
    e = pl.program_id(1)
    tm = x_ref.shape[0]
    lane = lax.broadcasted_iota(jnp.int32, (tm, ROUTER_LANES), 1)

    @pl.when(e == 0)
    def _():
        x = x_ref[...]
        ms = jnp.mean(x * x, axis=-1, keepdims=True)
        xn = x * lax.rsqrt(ms + NORM_EPS) * g_ref[...]
        xn_scr[...] = xn.astype(BF16)
        i1, i2, g1, g2 = _route(xn, rw_ref[...], rb_ref[...], lane, n_exp)
        gate_scr[...] = jnp.where(lane == i1, g1, 0.0) + jnp.where(lane == i2, g2, 0.0)
        acc_scr[...] = jnp.zeros_like(acc_scr)

    xn = xn_scr[...]
    hg = _dot(xn, wg_ref[0])
    hu = _dot(xn, wu_ref[0])
    gcol = jnp.sum(jnp.where(lane == e + N_GROUPS, gate_scr[...], 0.0), axis=-1, keepdims=True)
    hid = hg * _sigmoid(hg) * hu * gcol
    acc_scr[...] += _dot(hid.astype(BF16), wd_ref[0])

    @pl.when(e == n_exp - 1)
    def _():
        o_ref[...] = x_ref[...] + acc_scr[...]


def moe_layer(x, g, rw, rb, wg, wu, wd):
    t, d = x.shape
    n_exp, _, f = wg.shape
    tm = _row_tile(t)
    return pl.pallas_call(
        functools.partial(_moe_kernel, n_exp=n_exp),
        out_shape=jax.ShapeDtypeStruct((t, d), F32),
        grid=(t // tm, n_exp),
        in_specs=[pl.BlockSpec((tm, d), lambda i, e: (i, 0)),
                  pl.BlockSpec((1, d), lambda i, e: (0, 0)),
                  pl.BlockSpec((d, ROUTER_LANES), lambda i, e: (0, 0)),
                  pl.BlockSpec((1, ROUTER_LANES), lambda i, e: (0, 0)),
                  pl.BlockSpec((1, d, f), lambda i, e: (e, 0, 0)),
                  pl.BlockSpec((1, d, f), lambda i, e: (e, 0, 0)),
                  pl.BlockSpec((1, f, d), lambda i, e: (e, 0, 0))],
        out_specs=pl.BlockSpec((tm, d), lambda i, e: (i, 0)),
        scratch_shapes=[pltpu.VMEM((tm, d), BF16), pltpu.VMEM((tm, ROUTER_LANES), F32), pltpu.VMEM((tm, d), F32)],
        compiler_params=_cparams(("parallel", "arbitrary")),
        name="hier_moe",
    )(x, g.reshape(1, d), rw, rb, wg, wu, wd)


MOE_ROW_TILE = 512
MOE_COPY_CHUNK = 256
MOE_COMBINE_TILE = 256


def _router_kernel(x_ref, g_ref, rw_ref, rb_ref, tri_ref, gate_ref, info_ref, cnt_ref, base_scr, *, n_exp, n_tiles):
    i = pl.program_id(0)
    tm = x_ref.shape[0]
    lane = lax.broadcasted_iota(jnp.int32, (tm, ROUTER_LANES), 1)

    @pl.when(i == 0)
    def _():
        base_scr[...] = jnp.zeros_like(base_scr)

    x = x_ref[...]
    ms = jnp.mean(x * x, axis=-1, keepdims=True)
    xn = x * lax.rsqrt(ms + NORM_EPS) * g_ref[...]
    i1, i2, g1, g2 = _route(xn, rw_ref[...], rb_ref[...], lane, n_exp)
    chosen = jnp.where((lane == i1) | (lane == i2), 1.0, 0.0)
    before = _dot(tri_ref[...], chosen.astype(BF16)) + base_scr[...]
    r1 = jnp.sum(jnp.where(lane == i1, before, 0.0), axis=-1, keepdims=True)
    r2 = jnp.sum(jnp.where(lane == i2, before, 0.0), axis=-1, keepdims=True)
    base_scr[...] += jnp.sum(chosen, axis=0, keepdims=True)
    gate_ref[...] = jnp.where(lane == 0, g1, 0.0) + jnp.where(lane == 1, g2, 0.0)
    e1 = (i1 - N_GROUPS).astype(F32)
    e2 = (i2 - N_GROUPS).astype(F32)
    info_ref[...] = (jnp.where(lane == 0, e1, 0.0) + jnp.where(lane == 1, e2, 0.0)
                     + jnp.where(lane == 2, r1, 0.0) + jnp.where(lane == 3, r2, 0.0))

    @pl.when(i == n_tiles - 1)
    def _():
        cnt_ref[...] = base_scr[...]


def _scatter_rows_kernel(pos0_ref, pos1_ref, x_ref, xs_in_hbm, xs_hbm, stage, sem, *, CH, n_chunks):
    del xs_in_hbm
    c = pl.program_id(0)
    slot = c % 2
    stage[slot] = x_ref[...]

    def body(r, carry):
        t = c * CH + r
        src = stage.at[slot, pl.ds(r, 1), :]
        pltpu.make_async_copy(src, xs_hbm.at[pl.ds(pos0_ref[t], 1), :], sem.at[slot]).start()
        pltpu.make_async_copy(src, xs_hbm.at[pl.ds(pos1_ref[t], 1), :], sem.at[slot]).start()
        return carry

    lax.fori_loop(0, CH, body, 0, unroll=8)

    def drain(s):
        pltpu.make_async_copy(stage.at[s], xs_hbm.at[pl.ds(0, CH), :], sem.at[s]).wait()
        pltpu.make_async_copy(stage.at[s], xs_hbm.at[pl.ds(0, CH), :], sem.at[s]).wait()

    @pl.when(c > 0)
    def _():
        drain(1 - slot)

    @pl.when(c == n_chunks - 1)
    def _():
        drain(slot)


def _expert_kernel(te_ref, nv_ref, xs_ref, g_ref, wg_ref, wu_ref, wd_ref, y_ref):
    @pl.when(pl.program_id(0) < nv_ref[0])
    def _():
        x = xs_ref[...]
        ms = jnp.mean(x * x, axis=-1, keepdims=True)
        xn = (x * lax.rsqrt(ms + NORM_EPS) * g_ref[...]).astype(BF16)
        hg = _dot(xn, wg_ref[0])
        hu = _dot(xn, wu_ref[0])
        hid = hg * _sigmoid(hg) * hu
        y_ref[...] = _dot(hid.astype(BF16), wd_ref[0])

    @pl.when(pl.program_id(0) >= nv_ref[0])
    def _():
        y_ref[...] = jnp.zeros_like(y_ref)


def _combine_kernel(pos0_ref, pos1_ref, x_ref, gate_ref, y_hbm, o_ref, ybuf, sem, *, TC, n_tiles):
    i = pl.program_id(0)

    def issue(tile, slot):
        def body(r, carry):
            t = tile * TC + r
            pltpu.make_async_copy(y_hbm.at[pl.ds(pos0_ref[t], 1), :], ybuf.at[slot, 0, pl.ds(r, 1), :],
                                  sem.at[slot]).start()
            pltpu.make_async_copy(y_hbm.at[pl.ds(pos1_ref[t], 1), :], ybuf.at[slot, 1, pl.ds(r, 1), :],
                                  sem.at[slot]).start()
            return carry
        lax.fori_loop(0, TC, body, 0, unroll=8)

    @pl.when(i == 0)
    def _():
        issue(0, 0)

    @pl.when(i + 1 < n_tiles)
    def _():
        issue(i + 1, (i + 1) % 2)

    slot = i % 2
    pltpu.make_async_copy(y_hbm.at[pl.ds(0, TC), :], ybuf.at[slot, 0], sem.at[slot]).wait()
    pltpu.make_async_copy(y_hbm.at[pl.ds(0, TC), :], ybuf.at[slot, 1], sem.at[slot]).wait()
    gate = gate_ref[...]
    o_ref[...] = x_ref[...] + gate[:, 0:1] * ybuf[slot, 0] + gate[:, 1:2] * ybuf[slot, 1]


def moe_layer_sparse(x, g, rw, rb, wg, wu, wd):
    t, d = x.shape
    n_exp, _, f = wg.shape
    TM = MOE_ROW_TILE
    n_tiles = t // TM
    tri = jnp.asarray(np.tril(np.ones((TM, TM), np.float32), -1)).astype(BF16)
    gate, info, cnt = pl.pallas_call(
        functools.partial(_router_kernel, n_exp=n_exp, n_tiles=n_tiles),
        out_shape=[jax.ShapeDtypeStruct((t, ROUTER_LANES), F32), jax.ShapeDtypeStruct((t, ROUTER_LANES), F32),
                   jax.ShapeDtypeStruct((1, ROUTER_LANES), F32)],
        grid=(n_tiles,),
        in_specs=[pl.BlockSpec((TM, d), lambda i: (i, 0)),
                  pl.BlockSpec((1, d), lambda i: (0, 0)),
                  pl.BlockSpec((d, ROUTER_LANES), lambda i: (0, 0)),
                  pl.BlockSpec((1, ROUTER_LANES), lambda i: (0, 0)),
                  pl.BlockSpec((TM, TM), lambda i: (0, 0))],
        out_specs=[pl.BlockSpec((TM, ROUTER_LANES), lambda i: (i, 0)),
                   pl.BlockSpec((TM, ROUTER_LANES), lambda i: (i, 0)),
                   pl.BlockSpec((1, ROUTER_LANES), lambda i: (0, 0))],
        scratch_shapes=[pltpu.VMEM((1, ROUTER_LANES), F32)],
        compiler_params=_cparams(("arbitrary",)),
        name="moe_router",
    )(x, g.reshape(1, d), rw, rb, tri)

    counts = cnt[0, N_GROUPS:N_GROUPS + n_exp].astype(jnp.int32)
    padded = ((counts + TM - 1) // TM) * TM
    ends = jnp.cumsum(padded)
    offs = ends - padded
    eid = info[:, 0:2].astype(jnp.int32)
    rank = info[:, 2:4].astype(jnp.int32)
    pos = jnp.sum(jnp.where(eid[:, :, None] == jnp.arange(n_exp)[None, None, :], offs[None, None, :], 0), axis=-1) + rank
    pos0, pos1 = pos[:, 0], pos[:, 1]
    max_tiles = (2 * t) // TM + n_exp
    n_valid = (ends[-1] // TM).astype(jnp.int32).reshape(1)
    tile_exp = jnp.minimum(jnp.sum((ends[None, :] // TM) <= jnp.arange(max_tiles)[:, None], axis=-1),
                           n_exp - 1).astype(jnp.int32)
    p_rows = max_tiles * TM

    CH = MOE_COPY_CHUNK
    xs = pl.pallas_call(
        functools.partial(_scatter_rows_kernel, CH=CH, n_chunks=t // CH),
        out_shape=jax.ShapeDtypeStruct((p_rows, d), F32),
        grid_spec=pltpu.PrefetchScalarGridSpec(
            num_scalar_prefetch=2, grid=(t // CH,),
            in_specs=[pl.BlockSpec((CH, d), lambda c, p0, p1: (c, 0)), pl.BlockSpec(memory_space=pl.ANY)],
            out_specs=pl.BlockSpec(memory_space=pl.ANY),
            scratch_shapes=[pltpu.VMEM((2, CH, d), F32), pltpu.SemaphoreType.DMA((2,))]),
        input_output_aliases={3: 0},
        compiler_params=pltpu.CompilerParams(dimension_semantics=("arbitrary",), vmem_limit_bytes=VMEM_LIMIT,
                                             has_side_effects=True),
        name="moe_scatter_rows",
    )(pos0, pos1, x, jnp.zeros((p_rows, d), F32))

    def row_idx(i, te, nv):
        return (jnp.minimum(i, nv[0] - 1), 0)

    ys = pl.pallas_call(
        _expert_kernel,
        out_shape=jax.ShapeDtypeStruct((p_rows, d), F32),
        grid_spec=pltpu.PrefetchScalarGridSpec(
            num_scalar_prefetch=2, grid=(max_tiles,),
            in_specs=[pl.BlockSpec((TM, d), row_idx),
                      pl.BlockSpec((1, d), lambda i, te, nv: (0, 0)),
                      pl.BlockSpec((1, d, f), lambda i, te, nv: (te[i], 0, 0)),
                      pl.BlockSpec((1, d, f), lambda i, te, nv: (te[i], 0, 0)),
                      pl.BlockSpec((1, f, d), lambda i, te, nv: (te[i], 0, 0))],
            out_specs=pl.BlockSpec((TM, d), lambda i, te, nv: (i, 0))),
        compiler_params=_cparams(("arbitrary",)),
        name="moe_experts",
    )(tile_exp, n_valid, xs, g.reshape(1, d), wg, wu, wd)

    TC = MOE_COMBINE_TILE
    return pl.pallas_call(
        functools.partial(_combine_kernel, TC=TC, n_tiles=t // TC),
        out_shape=jax.ShapeDtypeStruct((t, d), F32),
        grid_spec=pltpu.PrefetchScalarGridSpec(
            num_scalar_prefetch=2, grid=(t // TC,),
            in_specs=[pl.BlockSpec((TC, d), lambda i, p0, p1: (i, 0)),
                      pl.BlockSpec((TC, ROUTER_LANES), lambda i, p0, p1: (i, 0)),
                      pl.BlockSpec(memory_space=pl.ANY)],
            out_specs=pl.BlockSpec((TC, d), lambda i, p0, p1: (i, 0)),
            scratch_shapes=[pltpu.VMEM((2, 2, TC, d), F32), pltpu.SemaphoreType.DMA((2,))]),
        compiler_params=_cparams(("arbitrary",)),
        name="moe_combine",
    )(pos0, pos1, x, gate, ys)


def _rwkv_kernel(p_ref, prev_ref, s0_ref, mu_ref, w0_ref, wd_ref, a0_ref, wa_ref, wg_ref, kk_ref, ka_ref,
                 bonus_ref, lng_ref, lnb_ref, tri_ref, hsum_ref, ya_ref, sf_ref, s_scr, prev_scr,
                 *, NB, C, H, DH, n_chunks):
    c = pl.program_id(1)

    @pl.when(c == 0)
    def _():
        s_scr[...] = s0_ref[:, 0]
        prev_scr[...] = prev_ref[:, 0]

    DA = H * DH
    R = NB * C
    p = p_ref[...].reshape(R, p_ref.shape[-1])
    row = lax.broadcasted_iota(jnp.int32, p.shape, 0)
    shifted = pltpu.roll(p, 1, axis=0)
    for n in range(NB):
        shifted = jnp.where(row == n * C, prev_scr[n], shifted)
        prev_scr[n] = p[(n + 1) * C - 1:(n + 1) * C, :]
    xs = p + (shifted - p) * mu_ref[...]
    r = xs[:, 0:DA]
    k = xs[:, DA:2 * DA]
    v = xs[:, 2 * DA:3 * DA]
    lora = xs[:, 3 * DA:3 * DA + R_DECAY + R_ICLR]
    gd = xs[:, 3 * DA + R_DECAY + R_ICLR:3 * DA + R_DECAY + R_ICLR + R_GATE]

    w_log = -_softplus(-(w0_ref[...] + _mp_dot(_dot, jnp.tanh(lora), wd_ref[...], 3))) - 0.5
    lw = -jnp.exp(w_log)
    a = _sigmoid(a0_ref[...] + _mp_dot(_dot, lora, wa_ref[...], 3))
    g = _mp_dot(_dot, _sigmoid(gd), wg_ref[...], 3)

    kk = k * kk_ref[...]
    ss = _dot_exact_rhs(kk * kk, hsum_ref[...], 2)
    kk = kk / jnp.maximum(jnp.sqrt(ss), 1e-12)
    k2 = k * (1.0 + (a - 1.0) * ka_ref[...])
    kka = kk * a

    cum = jnp.concatenate([_dot_exact_lhs(tri_ref[...], lw[n * C:(n + 1) * C], 3) for n in range(NB)], axis=0)
    p_in = jnp.exp(cum)
    r_t = r * p_in
    a_t = kk * jnp.exp(cum - lw)
    p_inv = jnp.exp(-cum)
    b_t = kka * p_inv
    k_t = k2 * p_inv
    bonus = _dot_exact_rhs(r * k2 * bonus_ref[...], hsum_ref[...], 2) * v

    ri = lax.broadcasted_iota(jnp.int32, (C, C), 0)
    ci = lax.broadcasted_iota(jnp.int32, (C, C), 1)
    strict = ri > ci
    incl = ri >= ci
    eye = (ri == ci).astype(F32)
    n_double = max(int(math.ceil(math.log2(C))) - 1, 0)

    chains = [(n, h) for n in range(NB) for h in range(H)]

    def blk(x, n, h):
        return x[n * C:(n + 1) * C, h * DH:(h + 1) * DH]

    def bf(x):
        return x.astype(BF16)

    Bt = [bf(blk(b_t, n, h)) for n, h in chains]
    Kt = [bf(blk(k_t, n, h)) for n, h in chains]
    Vf = [blk(v, n, h) for n, h in chains]
    AR = [bf(jnp.concatenate([blk(a_t, n, h), blk(r_t, n, h)], axis=0)) for n, h in chains]
    S0 = [s_scr[n, h] for n, h in chains]
    idx = range(len(chains))
    GB = [_dot_nt(AR[i], Bt[i]) for i in idx]
    GK = [_dot_nt(AR[i], Kt[i]) for i in idx]
    ARS = [_dot_nt(AR[i], bf(S0[i])) for i in idx]
    Lm = [jnp.where(strict, GB[i][0:C], 0.0) for i in idx]
    Gb = [bf(jnp.where(incl, GB[i][C:2 * C], 0.0)) for i in idx]
    MG = [bf(jnp.concatenate([jnp.where(strict, GK[i][0:C], 0.0), jnp.where(incl, GK[i][C:2 * C], 0.0)], axis=0))
          for i in idx]
    MGV = [_dot(MG[i], bf(Vf[i])) for i in idx]
    T = [eye - Lm[i] for i in idx]
    Pw = [bf(Lm[i]) for i in idx]
    for _ in range(n_double):
        Pw = [bf(_dot(Pw[i], Pw[i])) for i in idx]
        T = [T[i] + _dot(bf(T[i]), Pw[i]) for i in idx]
    U = [_dot(bf(T[i]), bf(-(ARS[i][0:C] + MGV[i][0:C]))) for i in idx]
    Y = [ARS[i][C:2 * C] + _dot(Gb[i], bf(U[i])) + MGV[i][C:2 * C] for i in idx]
    for i, (n, h) in enumerate(chains):
        UV = bf(jnp.concatenate([U[i], Vf[i]], axis=0))
        BK = jnp.concatenate([Bt[i], Kt[i]], axis=0)
        p_tot = p_in[(n + 1) * C - 1:(n + 1) * C, h * DH:(h + 1) * DH]
        s_scr[n, h] = (S0[i] + _dot_tn(UV, BK)) * p_tot

    rows = []
    for n in range(NB):
        ys = []
        for h in range(H):
            Yh = Y[n * H + h]
            yc = Yh - jnp.mean(Yh, axis=-1, keepdims=True)
            var = jnp.mean(yc * yc, axis=-1, keepdims=True)
            ys.append(yc * lax.rsqrt(var + GN_EPS))
        rows.append(jnp.concatenate(ys, axis=-1))
    y = jnp.concatenate(rows, axis=0) * lng_ref[...] + lnb_ref[...]
    ya_ref[...] = ((y + bonus) * g).reshape(NB, C, DA)

    @pl.when(c == n_chunks - 1)
    def _():
        sf_ref[:, 0] = s_scr[...]


def rwkv_mix(pa, n_batch, seq, shift_prev, wkv0, wts, n_par):
    t, ap = pa.shape
    H = wkv0.shape[1]
    DA = H * DH_A
    C = min(RWKV_CHUNK, seq)
    n_chunks = seq // C
    NB = n_par
    G = n_batch // NB
    tri = jnp.asarray(np.tril(np.ones((C, C), np.float32))).astype(BF16)
    hsum = jnp.asarray(np.kron(np.eye(H, dtype=np.float32), np.ones((DH_A, DH_A), np.float32))).astype(BF16)

    def full(shape):
        nd = len(shape)
        return pl.BlockSpec(shape, lambda b, c: (0,) * nd)

    vec = full((1, DA))
    ya, s_fin = pl.pallas_call(
        functools.partial(_rwkv_kernel, NB=NB, C=C, H=H, DH=DH_A, n_chunks=n_chunks),
        out_shape=[jax.ShapeDtypeStruct((NB, t // NB, DA), F32),
                   jax.ShapeDtypeStruct((NB, G, H, DH_A, DH_A), F32)],
        grid=(G, n_chunks),
        in_specs=[pl.BlockSpec((NB, C, ap), lambda b, c: (0, b * n_chunks + c, 0)),
                  pl.BlockSpec((NB, 1, 1, ap), lambda b, c: (0, b, 0, 0)),
                  pl.BlockSpec((NB, 1, H, DH_A, DH_A), lambda b, c: (0, b, 0, 0, 0)),
                  full((1, ap)), vec, full((R_DECAY + R_ICLR, DA)), vec, full((R_DECAY + R_ICLR, DA)),
                  full((R_GATE, DA)), vec, vec, vec, vec, vec, full((C, C)), full((DA, DA))],
        out_specs=[pl.BlockSpec((NB, C, DA), lambda b, c: (0, b * n_chunks + c, 0)),
                   pl.BlockSpec((NB, 1, H, DH_A, DH_A), lambda b, c: (0, b, 0, 0, 0))],
        scratch_shapes=[pltpu.VMEM((NB, H, DH_A, DH_A), F32), pltpu.VMEM((NB, 1, ap), F32)],
        compiler_params=_cparams(("parallel", "arbitrary")),
        name="rwkv7_mix",
    )(pa.reshape(NB, t // NB, ap), shift_prev.reshape(NB, G, 1, ap), wkv0.reshape(NB, G, H, DH_A, DH_A),
      wts["mu"], wts["w0"], wts["wd"], wts["a0"], wts["wa"],
      wts["wg"], wts["key_k"], wts["key_a"], wts["bonus"], wts["lnx_g"], wts["lnx_b"], tri, hsum)
    return ya.reshape(t, DA), s_fin.reshape(wkv0.shape)


GMLP_TILE = 128


def _gmlp_kernel(u_ref, v_ref, ng_ref, nb_ref, wm_ref, bias_ref, o_ref, vn_ref):
    vf = _gelu(v_ref[...])
    mu = jnp.mean(vf, axis=-1, keepdims=True)
    vc = vf - mu
    var = jnp.mean(vc * vc, axis=-1, keepdims=True)
    vn = vc * lax.rsqrt(var + NORM_EPS) * ng_ref[...] + nb_ref[...]
    vn_ref[...] = vn
    vb = vn.astype(BF16)
    n_h = wm_ref.shape[0]
    cb = vn.shape[1] // n_h
    s = jnp.concatenate([_dot(wm_ref[h], vb[:, h * cb:(h + 1) * cb]) for h in range(n_h)], axis=-1)
    o_ref[...] = _gelu(u_ref[...]) * (s + bias_ref[...])


def gmlp_mix(pu, pv, ng, nb, wm_bf16, bias_tile):
    t, db = pu.shape
    n_h = wm_bf16.shape[0]
    return pl.pallas_call(
        _gmlp_kernel,
        out_shape=[jax.ShapeDtypeStruct((t, db), F32), jax.ShapeDtypeStruct((t, db), F32)],
        grid=(t // GMLP_TILE,),
        in_specs=[pl.BlockSpec((GMLP_TILE, db), lambda i: (i, 0)),
                  pl.BlockSpec((GMLP_TILE, db), lambda i: (i, 0)),
                  pl.BlockSpec((1, db), lambda i: (0, 0)),
                  pl.BlockSpec((1, db), lambda i: (0, 0)),
                  pl.BlockSpec((n_h, GMLP_TILE, GMLP_TILE), lambda i: (0, 0, 0)),
                  pl.BlockSpec((GMLP_TILE, db), lambda i: (0, 0))],
        out_specs=[pl.BlockSpec((GMLP_TILE, db), lambda i: (i, 0)),
                   pl.BlockSpec((GMLP_TILE, db), lambda i: (i, 0))],
        compiler_params=_cparams(("parallel",)),
        name="gmlp_mix",
    )(pu, pv, ng, nb, wm_bf16, bias_tile)


N_SEG = 8


def _rglru_kernel(xb_ref, gy_ref, cprev_ref, h0_ref, cw_ref, cb_ref, gw_ref, gb_ref, lam_ref,
                  yc_ref, ctail_ref, hl_ref, xe_scr, a_scr, b_scr, h_scr, *, TL, DC, pos0, n_tiles):
    l = pl.program_id(1)
    PAD = 8

    @pl.when(l == 0)
    def _():
        xe_scr[0:PAD, :] = cprev_ref[0]
        h_scr[...] = h0_ref[0]

    xe_scr[PAD:PAD + TL, :] = xb_ref[...]
    xc = cb_ref[...] + xe_scr[pl.ds(PAD - (CONV_W - 1), TL), :] * cw_ref[0:1, :]
    for i in range(1, CONV_W):
        xc = xc + xe_scr[pl.ds(PAD - (CONV_W - 1) + i, TL), :] * cw_ref[i:i + 1, :]
    tail = xe_scr[TL:TL + PAD, :]
    ctail_ref[0] = tail
    xe_scr[0:PAD, :] = tail

    gates = _dot(xc.astype(BF16), gw_ref[...]) + gb_ref[...]
    rg = _sigmoid(gates[:, 0:DC])
    ig = _sigmoid(gates[:, DC:2 * DC])
    log_a = -LRU_C * rg * _softplus(-lam_ref[...])
    a = jnp.exp(log_a)
    mult = jnp.sqrt(1.0 - jnp.exp(2.0 * log_a))
    row = lax.broadcasted_iota(jnp.int32, (TL, DC), 0)
    mult = jnp.where(row + (l * TL + pos0) == 0, 1.0, mult)
    b = mult * ig * xc
    n_slab = DC // LANES
    for s in range(n_slab):
        a_scr[s] = a[:, s * LANES:(s + 1) * LANES]
        b_scr[s] = b[:, s * LANES:(s + 1) * LANES]

    seg = TL // N_SEG

    def step(i, carry):
        idx = pl.ds(i, N_SEG, stride=seg) if seg > 1 else pl.ds(0, N_SEG)
        out = []
        for s in range(n_slab):
            hloc, ap = carry[s]
            ai = a_scr[s, idx, :]
            hloc = ai * hloc + b_scr[s, idx, :]
            ap = ap * ai
            b_scr[s, idx, :] = hloc
            a_scr[s, idx, :] = ap
            out.append((hloc, ap))
        return tuple(out)

    lax.fori_loop(0, seg, step,
                  tuple((jnp.zeros((N_SEG, LANES), F32), jnp.ones((N_SEG, LANES), F32)) for _ in range(n_slab)))

    carry = h_scr[...]
    g_act = _gelu(gy_ref[...])
    for j in range(N_SEG):
        rows = slice(j * seg, (j + 1) * seg)
        hloc = jnp.concatenate([b_scr[s, rows, :] for s in range(n_slab)], axis=-1)
        ap = jnp.concatenate([a_scr[s, rows, :] for s in range(n_slab)], axis=-1)
        hj = hloc + ap * carry
        yc_ref[rows, :] = g_act[rows, :] * hj
        carry = hj[seg - 1:seg, :]
    h_scr[...] = carry

    @pl.when(l == n_tiles - 1)
    def _():
        hl_ref[0] = carry


def rglru_mix(xb, gy, n_batch, seq, conv_prev8, h0, pos0, wts):
    t, dc = xb.shape
    TL = 512 if seq % 512 == 0 else seq
    n_tiles = seq // TL

    def full(shape):
        nd = len(shape)
        return pl.BlockSpec(shape, lambda b, l: (0,) * nd)

    yc, ctail, hl = pl.pallas_call(
        functools.partial(_rglru_kernel, TL=TL, DC=dc, pos0=pos0, n_tiles=n_tiles),
        out_shape=[jax.ShapeDtypeStruct((t, dc), F32), jax.ShapeDtypeStruct((n_batch, 8, dc), F32),
                   jax.ShapeDtypeStruct((n_batch, 1, dc), F32)],
        grid=(n_batch, n_tiles),
        in_specs=[pl.BlockSpec((TL, dc), lambda b, l: (b * n_tiles + l, 0)),
                  pl.BlockSpec((TL, dc), lambda b, l: (b * n_tiles + l, 0)),
                  pl.BlockSpec((1, 8, dc), lambda b, l: (b, 0, 0)),
                  pl.BlockSpec((1, 1, dc), lambda b, l: (b, 0, 0)),
                  full((CONV_W, dc)), full((1, dc)), full((dc, 2 * dc)), full((1, 2 * dc)), full((1, dc))],
        out_specs=[pl.BlockSpec((TL, dc), lambda b, l: (b * n_tiles + l, 0)),
                   pl.BlockSpec((1, 8, dc), lambda b, l: (b, 0, 0)),
                   pl.BlockSpec((1, 1, dc), lambda b, l: (b, 0, 0))],
        scratch_shapes=[pltpu.VMEM((TL + 8, dc), F32), pltpu.VMEM((dc // LANES, TL, LANES), F32),
                        pltpu.VMEM((dc // LANES, TL, LANES), F32), pltpu.VMEM((1, dc), F32)],
        compiler_params=_cparams(("parallel", "arbitrary")),
        name="rglru_mix",
    )(xb, gy, conv_prev8, h0.reshape(n_batch, 1, dc), wts["conv_w"], wts["conv_b"], wts["gate_w"], wts["gate_b"],
      wts["lam"])
    return yc, ctail[:, 8 - (CONV_W - 1):, :], hl.reshape(n_batch, dc)


def _t5_bucket(dist):
    dist = np.asarray(dist)
    max_exact = N_BUCKETS // 2
    scaled = np.log(np.maximum(dist, 1) / max_exact) / math.log(BUCKET_MAX_DIST / max_exact)
    large = np.minimum(max_exact + (scaled * (N_BUCKETS - max_exact)).astype(np.int32), N_BUCKETS - 1)
    return np.where(dist < max_exact, dist, large).astype(np.int32)


def _dist_table(rel_bias, max_dist):
    dist = np.arange(max_dist + 1)
    count = np.zeros(max_dist + 1, np.float32)
    for window, dil in DILATED:
        count += ((dist % dil == 0) & (dist <= window)).astype(np.float32)
    logcnt = np.where(count > 0, np.log(np.maximum(count, 1.0)), 0.0).astype(np.float32)
    tab = jnp.take(rel_bias, jnp.asarray(_t5_bucket(dist)), axis=0) + jnp.asarray(logcnt)[:, None]
    return jnp.where(jnp.asarray(count > 0)[:, None], tab, NEG_BIG)


def _toeplitz_tiles(tab, n_pos, n_neg, T):
    D, H = tab.shape
    span = T * n_pos
    assert D >= span
    n_col = span + T * n_neg + T - 1
    ext = jnp.concatenate([jnp.flip(tab[:span], axis=0), jnp.full((n_col + 1 - span, H), NEG_BIG, F32)], axis=0)
    ext = jnp.transpose(ext)
    skew = jnp.tile(ext, (1, T))[:, :T * n_col].reshape(H, T, n_col)
    tiles = [skew[:, :, span - 1 - T * dd: span - 1 - T * dd + T] for dd in range(-n_neg, n_pos)]
    return jnp.stack(tiles, axis=1)


def _attn_prompt_kernel(q_ref, k_ref, v_ref, bias_ref, o_ref, kb_scr, vb_scr, *, E, SUB, NS):
    qi = pl.program_id(2)
    TQ = NS * SUB

    @pl.when(qi == 0)
    def _():
        kb_scr[...] = k_ref[0].astype(BF16)
        vb_scr[...] = v_ref[0].astype(BF16)

    lane = lax.broadcasted_iota(jnp.int32, (SUB, 2 * E), 1)
    q2 = []
    for rs in range(NS):
        q = q_ref[0, rs * SUB:(rs + 1) * SUB, :] * (E ** -0.5)
        q2.append(jnp.concatenate([jnp.where(lane < E, q, 0.0), jnp.where(lane >= E, q, 0.0)], axis=0).astype(BF16))

    def body(i, carry):
        j = qi - i
        kj = kb_scr[pl.ds(pl.multiple_of(j * TQ, TQ), TQ), :]
        vj = vb_scr[pl.ds(pl.multiple_of(j * TQ, TQ), TQ), :]
        out = []
        for rs in range(NS):
            m, l, acc = carry[rs]
            s = _dot_nt(q2[rs], kj)
            parts = []
            for cs in range(NS):
                dd = i * NS + (rs - cs + NS - 1)
                bias = jnp.concatenate([bias_ref[0, dd], bias_ref[1, dd]], axis=0)
                parts.append(s[:, cs * SUB:(cs + 1) * SUB] + bias)
            mx = parts[0]
            for part in parts[1:]:
                mx = jnp.maximum(mx, part)
            m_new = jnp.maximum(m, jnp.max(mx, axis=-1, keepdims=True))
            alpha = jnp.exp(m - m_new)
            ps = [jnp.exp(part - m_new) for part in parts]
            psum = ps[0]
            for pexp in ps[1:]:
                psum = psum + pexp
            l = alpha * l + psum
            acc = alpha * acc + _dot(jnp.concatenate(ps, axis=-1).astype(BF16), vj)
            out.append((m_new, l, acc))
        return tuple(out)

    init = tuple((jnp.full((2 * SUB, SUB), NEG_BIG, F32), jnp.zeros((2 * SUB, SUB), F32),
                  jnp.zeros((2 * SUB, 2 * E), F32)) for _ in range(NS))
    res = lax.fori_loop(0, qi + 1, body, init)
    for rs in range(NS):
        m, l, acc = res[rs]
        o = acc / jnp.sum(l, axis=-1, keepdims=True)
        o_ref[0, rs * SUB:(rs + 1) * SUB, :] = jnp.where(lane < E, o[0:SUB], o[SUB:2 * SUB])


def attn_prompt(q, k, v, bias_tiles, n_batch, seq):
    hd = q.shape[-1]
    E = hd // H_D
    SUB = ATT_TILE
    NS = ATT_SUBTILES
    TQ = SUB * NS
    nq = seq // TQ
    nt = bias_tiles.shape[1]
    return pl.pallas_call(
        functools.partial(_attn_prompt_kernel, E=E, SUB=SUB, NS=NS),
        out_shape=jax.ShapeDtypeStruct((n_batch, seq, hd), F32),
        grid=(H_D // 2, n_batch, nq),
        in_specs=[pl.BlockSpec((1, TQ, 2 * E), lambda hp, b, i: (b, i, hp)),
                  pl.BlockSpec((1, seq, 2 * E), lambda hp, b, i: (b, 0, hp)),
                  pl.BlockSpec((1, seq, 2 * E), lambda hp, b, i: (b, 0, hp)),
                  pl.BlockSpec((2, nt, SUB, SUB), lambda hp, b, i: (hp, 0, 0, 0))],
        out_specs=pl.BlockSpec((1, TQ, 2 * E), lambda hp, b, i: (b, i, hp)),
        scratch_shapes=[pltpu.VMEM((seq, 2 * E), BF16), pltpu.VMEM((seq, 2 * E), BF16)],
        compiler_params=_cparams(("arbitrary", "arbitrary", "arbitrary")),
        name="dilated_attn_prompt",
    )(q, k, v, bias_tiles)


def _attn_sample_kernel(q_ref, kn_ref, vn_ref, ck_ref, cv_ref, bo_ref, bn_ref, o_ref, *, E, S):
    lane = lax.broadcasted_iota(jnp.int32, (S, 2 * E), 1)
    NPAD = bn_ref.shape[-1]
    outs = []
    for hp in range(H_D // 2):
        sl = slice(hp * 2 * E, (hp + 1) * 2 * E)
        q = q_ref[0, :, sl] * (E ** -0.5)
        q2 = jnp.concatenate([jnp.where(lane < E, q, 0.0), jnp.where(lane >= E, q, 0.0)], axis=0).astype(BF16)
        zpad = jnp.zeros((NPAD - S, 2 * E), F32)
        kn = jnp.concatenate([kn_ref[0, :, sl], zpad], axis=0).astype(BF16)
        vn = jnp.concatenate([vn_ref[0, :, sl], zpad], axis=0).astype(BF16)
        s_old = _dot_nt(q2, ck_ref[0, :, sl].astype(BF16)) + jnp.concatenate([bo_ref[2 * hp], bo_ref[2 * hp + 1]], axis=0)
        s_new = _dot_nt(q2, kn) + jnp.concatenate([bn_ref[2 * hp], bn_ref[2 * hp + 1]], axis=0)
        m = jnp.maximum(jnp.max(s_old, axis=-1, keepdims=True), jnp.max(s_new, axis=-1, keepdims=True))
        p_old = jnp.exp(s_old - m)
        p_new = jnp.exp(s_new - m)
        l = jnp.sum(p_old, axis=-1, keepdims=True) + jnp.sum(p_new, axis=-1, keepdims=True)
        acc = _dot(p_old.astype(BF16), cv_ref[0, :, sl].astype(BF16)) + _dot(p_new.astype(BF16), vn)
        o = acc / l
        outs.append(jnp.where(lane < E, o[0:S], o[S:2 * S]))
    o_ref[0] = jnp.concatenate(outs, axis=-1)


def attn_sample(q, k_new, v_new, cache_k, cache_v, bias_old, bias_new):
    n_batch, S, hd = q.shape
    W = cache_k.shape[1]
    E = hd // H_D
    NPAD = bias_new.shape[-1]
    return pl.pallas_call(
        functools.partial(_attn_sample_kernel, E=E, S=S),
        out_shape=jax.ShapeDtypeStruct((n_batch, S, hd), F32),
        grid=(n_batch,),
        in_specs=[pl.BlockSpec((1, S, hd), lambda b: (b, 0, 0)),
                  pl.BlockSpec((1, S, hd), lambda b: (b, 0, 0)),
                  pl.BlockSpec((1, S, hd), lambda b: (b, 0, 0)),
                  pl.BlockSpec((1, W, hd), lambda b: (b, 0, 0)),
                  pl.BlockSpec((1, W, hd), lambda b: (b, 0, 0)),
                  pl.BlockSpec((H_D, S, W), lambda b: (0, 0, 0)),
                  pl.BlockSpec((H_D, S, NPAD), lambda b: (0, 0, 0))],
        out_specs=pl.BlockSpec((1, S, hd), lambda b: (b, 0, 0)),
        compiler_params=_cparams(("parallel",)),
        name="dilated_attn_sample",
    )(q, k_new, v_new, cache_k, cache_v, bias_old, bias_new)


def _even_weights(j, w_in_even, w_out_even, shift_mu, decay_w0, decay_up, iclr_a0, iclr_up, gate_up, key_k, key_a,
                  bonus_r_k, lnx_g, lnx_b, sgu_norm_g, sgu_norm_b, sgu_w, sgu_b):
    da = decay_w0.shape[1]
    zeros_d = jnp.zeros((R_ICLR, da), F32)
    zeros_i = jnp.zeros((R_DECAY, da), F32)
    return dict(
        w_in=w_in_even[j].astype(BF16),
        w_out_a=w_out_even[j, :da].astype(BF16), w_out_b=w_out_even[j, da:].astype(BF16),
        mu=shift_mu[j].reshape(1, -1), w0=decay_w0[j].reshape(1, -1), a0=iclr_a0[j].reshape(1, -1),
        wd=jnp.concatenate([decay_up[j], zeros_d], axis=0), wa=jnp.concatenate([zeros_i, iclr_up[j]], axis=0),
        wg=gate_up[j], key_k=key_k[j].reshape(1, -1), key_a=key_a[j].reshape(1, -1),
        bonus=bonus_r_k[j].reshape(1, -1), lnx_g=lnx_g[j].reshape(1, -1), lnx_b=lnx_b[j].reshape(1, -1),
        ng=sgu_norm_g[j].reshape(1, -1), nb=sgu_norm_b[j].reshape(1, -1), sgu_w=sgu_w[j], sgu_b=sgu_b[j])


def _gmlp_tables(sgu_w, sgu_b, chunk):
    reps = GMLP_TILE // chunk
    n_h = sgu_w.shape[0]
    cb = None
    wm = sgu_w[:, :chunk, :chunk] * jnp.asarray(np.tril(np.ones((chunk, chunk), np.float32)))
    if reps > 1:
        eye = jnp.asarray(np.eye(reps, dtype=np.float32))
        wm = jnp.einsum("ab,hts->hatbs", eye, wm).reshape(n_h, GMLP_TILE, GMLP_TILE)
    bias = jnp.tile(jnp.transpose(sgu_b[:, :chunk]), (reps, 1))
    return wm.astype(BF16), bias


def _even_layer(x, n_batch, seq, chunk, n_par, shift_prev, wkv0, norm_g, ew):
    pa, pu, pv = norm_matmul(x, norm_g, ew["w_in"], (ew["mu"].shape[1], ew["ng"].shape[1], ew["ng"].shape[1]))
    ya, wkv = rwkv_mix(pa, n_batch, seq, shift_prev, wkv0, ew, n_par)
    wm, bias = _gmlp_tables(ew["sgu_w"], ew["sgu_b"], chunk)
    cb = pu.shape[1] // wm.shape[0]
    bias_tile = jnp.repeat(bias, cb, axis=1)
    yb, vn = gmlp_mix(pu, pv, ew["ng"], ew["nb"], wm, bias_tile)
    x = proj_residual(x, ya, yb, ew["w_out_a"], ew["w_out_b"])
    last = pa.reshape(n_batch, seq, -1)[:, -1]
    return x, last, wkv, vn


def _odd_weights(j, w_in_odd, w_out_odd, conv_w, conv_b, rgate_w, rgate_b, igate_w, igate_b, lru_lambda):
    dc = conv_b.shape[1]
    eye = jnp.asarray(np.eye(H_C, dtype=np.float32))

    def blockdiag(w):
        dh = w.shape[-1]
        return jnp.einsum("ab,aij->aibj", eye, w).reshape(H_C * dh, H_C * dh)

    return dict(
        w_in=w_in_odd[j].astype(BF16),
        w_out_c=w_out_odd[j, :dc].astype(BF16), w_out_d=w_out_odd[j, dc:].astype(BF16),
        conv_w=conv_w[j], conv_b=conv_b[j].reshape(1, -1),
        gate_w=jnp.concatenate([blockdiag(rgate_w[j]), blockdiag(igate_w[j])], axis=1).astype(BF16),
        gate_b=jnp.concatenate([rgate_b[j], igate_b[j]]).reshape(1, -1),
        lam=lru_lambda[j].reshape(1, -1))


def _odd_layer(x, n_batch, seq, conv_prev, h0, pos0, caches, dist_tab, norm_g, ow):
    dc = ow["lam"].shape[1]
    gy, xb, q, k, v = norm_matmul(x, norm_g, ow["w_in"], (dc,) * 5)
    conv_prev8 = jnp.pad(conv_prev, ((0, 0), (8 - (CONV_W - 1), 0), (0, 0)))
    yc, conv_last, h_last = rglru_mix(xb, gy, n_batch, seq, conv_prev8, h0, pos0, ow)
    hd = q.shape[1]
    q3, k3, v3 = (a.reshape(n_batch, seq, hd) for a in (q, k, v))
    if caches is None:
        tiles = _toeplitz_tiles(dist_tab, seq // ATT_TILE, ATT_SUBTILES - 1, ATT_TILE)
        o = attn_prompt(q3, k3, v3, tiles, n_batch, seq)
    else:
        cache_k, cache_v = caches
        W = cache_k.shape[1]
        ck = cache_k.reshape(n_batch, W, hd)
        cv = cache_v.reshape(n_batch, W, hd)
        NPAD = 128
        d_old = W + np.arange(seq)[:, None] - np.arange(W)[None, :]
        b_old = jnp.transpose(jnp.take(dist_tab, jnp.asarray(d_old), axis=0), (2, 0, 1))
        d_new = np.arange(seq)[:, None] - np.arange(NPAD)[None, :]
        ok_new = (d_new >= 0) & (np.arange(NPAD)[None, :] < seq)
        b_new = jnp.take(dist_tab, jnp.asarray(np.maximum(d_new, 0)), axis=0)
        b_new = jnp.transpose(jnp.where(jnp.asarray(ok_new)[..., None], b_new, NEG_BIG), (2, 0, 1))
        o = attn_sample(q3, k3, v3, ck, cv, b_old, b_new)
    x = proj_residual(x, yc, o.reshape(n_batch * seq, hd), ow["w_out_c"], ow["w_out_d"])
    e = hd // H_D
    return x, conv_last, h_last, k3.reshape(n_batch, seq, H_D, e), v3.reshape(n_batch, seq, H_D, e)


def _moe_weights(l, router_group_w, router_group_b, router_expert_w, router_expert_b, exp_w_gate, exp_w_up,
                 exp_w_down):
    d = router_group_w.shape[1]
    n_used = N_GROUPS + router_expert_w.shape[2]
    rw = jnp.concatenate([router_group_w[l], router_expert_w[l], jnp.zeros((d, ROUTER_LANES - n_used), F32)], axis=1)
    rb = jnp.concatenate([router_group_b[l], router_expert_b[l], jnp.zeros((ROUTER_LANES - n_used,), F32)])
    return dict(rw=rw, rb=rb.reshape(1, -1), wg=exp_w_gate[l].astype(BF16), wu=exp_w_up[l].astype(BF16),
                wd=exp_w_down[l].astype(BF16))


def kernel(x_prompt, x_sample, state_wkv, state_shift, state_conv, state_rglru, cache_k, cache_v, norm_mix, norm_ffn, norm_final, w_in_even, w_out_even, shift_mu, decay_w0, decay_up, iclr_a0, iclr_up, gate_up, key_k, key_a, bonus_r_k, lnx_g, lnx_b, sgu_norm_g, sgu_norm_b, sgu_w, sgu_b, w_in_odd, w_out_odd, conv_w, conv_b, rgate_w, rgate_b, igate_w, igate_b, lru_lambda, rel_bias, router_group_w, router_group_b, router_expert_w, router_expert_b, exp_w_gate, exp_w_up, exp_w_down):
    B, L, D = x_prompt.shape
    DB, S, _ = x_sample.shape
    depth = norm_mix.shape[0]
    xp = x_prompt.reshape(B * L, D)
    xs = x_sample.reshape(DB * S, D)
    W = cache_k.shape[2]
    dist_tab = _dist_table(rel_bias, max(L, W + S) - 1)

    wkv_p, shift_p, conv_p, lru_p, k_p, v_p = [], [], [], [], [], []
    wkv_s, shift_s, chunkv_s, conv_s, lru_s, k_s, v_s = [], [], [], [], [], [], []
    for l in range(depth):
        j = l // 2
        if l % 2 == 0:
            ew = _even_weights(j, w_in_even, w_out_even, shift_mu, decay_w0, decay_up, iclr_a0, iclr_up, gate_up,
                               key_k, key_a, bonus_r_k, lnx_g, lnx_b, sgu_norm_g, sgu_norm_b, sgu_w, sgu_b)
            a_proj = ew["mu"].shape[1]
            h_a = state_wkv.shape[2]
            xp, sh, wkv, _ = _even_layer(xp, B, L, GMLP_TILE, RWKV_PAR_PROMPT, jnp.zeros((B, a_proj), F32),
                                         jnp.zeros((B, h_a, DH_A, DH_A), F32), norm_mix[l], ew)
            xs, sh_s, wkv_s_new, vn_s = _even_layer(xs, DB, S, S, RWKV_PAR_SAMPLE, state_shift[j], state_wkv[j], norm_mix[l], ew)
            wkv_p.append(wkv)
            shift_p.append(sh)
            wkv_s.append(wkv_s_new)
            shift_s.append(sh_s)
            chunkv_s.append(vn_s.reshape(DB, S, -1))
        else:
            ow = _odd_weights(j, w_in_odd, w_out_odd, conv_w, conv_b, rgate_w, rgate_b, igate_w, igate_b, lru_lambda)
            dc = ow["lam"].shape[1]
            xp, cv, hl, kr, vr = _odd_layer(xp, B, L, jnp.zeros((B, CONV_W - 1, dc), F32), jnp.zeros((B, dc), F32),
                                            0, None, dist_tab, norm_mix[l], ow)
            xs, cv_s, hl_s, kr_s, vr_s = _odd_layer(xs, DB, S, state_conv[j], state_rglru[j], PAST_LEN,
                                                    (cache_k[j], cache_v[j]), dist_tab, norm_mix[l], ow)
            conv_p.append(cv)
            lru_p.append(hl)
            k_p.append(kr)
            v_p.append(vr)
            conv_s.append(cv_s)
            lru_s.append(hl_s)
            k_s.append(kr_s)
            v_s.append(vr_s)
        mw = _moe_weights(l, router_group_w, router_group_b, router_expert_w, router_expert_b, exp_w_gate, exp_w_up,
                          exp_w_down)
        xp = moe_layer_sparse(xp, norm_ffn[l], mw["rw"], mw["rb"], mw["wg"], mw["wu"], mw["wd"])
        xs = moe_layer(xs, norm_ffn[l], mw["rw"], mw["rb"], mw["wg"], mw["wu"], mw["wd"])
    y_prompt = rmsnorm_call(xp, norm_final).reshape(B, L, D)
    y_sample = rmsnorm_call(xs, norm_final).reshape(DB, S, D)
    return (y_prompt, y_sample,
            jnp.stack(wkv_p), jnp.stack(shift_p), jnp.stack(conv_p), jnp.stack(lru_p), jnp.stack(k_p), jnp.stack(v_p),
            jnp.stack(wkv_s), jnp.stack(shift_s), jnp.stack(chunkv_s), jnp.stack(conv_s), jnp.stack(lru_s),
            jnp.stack(k_s), jnp.stack(v_s))
```

```python
import functools
import math

import numpy as np
import jax
import jax.numpy as jnp
from jax import lax
from jax.experimental import pallas as pl
from jax.experimental.pallas import tpu as pltpu

F32 = jnp.float32
BF16 = jnp.bfloat16
HI = lax.Precision.HIGHEST

PAST_LEN = 8192
DH_A = 64
R_DECAY = 64
R_ICLR = 64
R_GATE = 128
GN_EPS = 64e-5
H_B = 4
H_C = 8
CONV_W = 4
LRU_C = 8.0
H_D = 8
DILATED = ((128, 1), (512, 4), (2048, 16))
N_BUCKETS = 32
BUCKET_MAX_DIST = 2048
NEG_BIG = -1e30
N_GROUPS = 4
EXP_PER_GROUP = 4
NORM_EPS = 1e-6
LOG2E = math.log2(math.e)

VMEM_LIMIT = 56 * 1024 * 1024
RWKV_CHUNK = 64
RWKV_PAR_PROMPT = 4
RWKV_PAR_SAMPLE = 8
ATT_TILE = 128
ATT_SUBTILES = 4
LANES = 128


def _cparams(sem):
    return pltpu.CompilerParams(dimension_semantics=sem, vmem_limit_bytes=VMEM_LIMIT)


def _dot(a, b, precision=None):
    return jnp.dot(a, b, preferred_element_type=F32, precision=precision)


def _dot_nt(a, b, precision=None):
    return lax.dot_general(a, b, (((1,), (1,)), ((), ())), preferred_element_type=F32, precision=precision)


def _dot_tn(a, b, precision=None):
    return lax.dot_general(a, b, (((0,), (0,)), ((), ())), preferred_element_type=F32, precision=precision)


def _split_bf16(x, n):
    parts = []
    for _ in range(n):
        hi = x.astype(BF16)
        parts.append(hi)
        x = x - hi.astype(F32)
    return parts


def _mp_dot(dotfn, a, b, passes):
    if passes == 1:
        return dotfn(a.astype(BF16), b.astype(BF16))
    a_hi, a_lo = _split_bf16(a, 2)
    b_hi, b_lo = _split_bf16(b, 2)
    return dotfn(a_hi, b_hi) + (dotfn(a_hi, b_lo) + dotfn(a_lo, b_hi))


def _dot_exact_rhs(a, b_bf16, n_split):
    parts = _split_bf16(a, n_split)
    acc = _dot(parts[0], b_bf16)
    for part in parts[1:]:
        acc = acc + _dot(part, b_bf16)
    return acc


def _dot_exact_lhs(a_bf16, b, n_split):
    parts = _split_bf16(b, n_split)
    acc = _dot(a_bf16, parts[0])
    for part in parts[1:]:
        acc = acc + _dot(a_bf16, part)
    return acc


def _softplus(x):
    return jnp.maximum(x, 0.0) + jnp.log(1.0 + jnp.exp(-jnp.abs(x)))


def _sigmoid(x):
    return 1.0 / (1.0 + jnp.exp(-x))


def _gelu(x):
    c = math.sqrt(2.0 / math.pi)
    return 0.5 * x * (1.0 + jnp.tanh(c * (x + 0.044715 * (x * x * x))))


def _row_tile(t, pref=512):
    return pref if t % pref == 0 else t


def _norm_matmul_kernel(x_ref, g_ref, w_ref, *out_refs, splits):
    x = x_ref[...]
    ms = jnp.mean(x * x, axis=-1, keepdims=True)
    h = (x * lax.rsqrt(ms + NORM_EPS) * g_ref[...]).astype(BF16)
    off = 0
    for o_ref, n in zip(out_refs, splits):
        o_ref[...] = _dot(h, w_ref[:, off:off + n])
        off += n


def norm_matmul(x, g, w_bf16, splits):
    t, d = x.shape
    n = w_bf16.shape[1]
    tm = _row_tile(t)
    return pl.pallas_call(
        functools.partial(_norm_matmul_kernel, splits=splits),
        out_shape=[jax.ShapeDtypeStruct((t, s), F32) for s in splits],
        grid=(t // tm,),
        in_specs=[pl.BlockSpec((tm, d), lambda i: (i, 0)),
                  pl.BlockSpec((1, d), lambda i: (0, 0)),
                  pl.BlockSpec((d, n), lambda i: (0, 0))],
        out_specs=[pl.BlockSpec((tm, s), lambda i: (i, 0)) for s in splits],
        compiler_params=_cparams(("parallel",)),
        name="norm_matmul",
    )(x, g.reshape(1, d), w_bf16)


def _proj_res_kernel(x_ref, a_ref, b_ref, wa_ref, wb_ref, o_ref):
    acc = _dot(a_ref[...].astype(BF16), wa_ref[...]) + _dot(b_ref[...].astype(BF16), wb_ref[...])
    o_ref[...] = x_ref[...] + acc


def proj_residual(x, a, b, wa, wb):
    t, d = x.shape
    tm = _row_tile(t)
    ka, kb = a.shape[1], b.shape[1]
    return pl.pallas_call(
        _proj_res_kernel,
        out_shape=jax.ShapeDtypeStruct((t, d), F32),
        grid=(t // tm,),
        in_specs=[pl.BlockSpec((tm, d), lambda i: (i, 0)),
                  pl.BlockSpec((tm, ka), lambda i: (i, 0)),
                  pl.BlockSpec((tm, kb), lambda i: (i, 0)),
                  pl.BlockSpec((ka, d), lambda i: (0, 0)),
                  pl.BlockSpec((kb, d), lambda i: (0, 0))],
        out_specs=pl.BlockSpec((tm, d), lambda i: (i, 0)),
        compiler_params=_cparams(("parallel",)),
        name="proj_residual",
    )(x, a, b, wa, wb)


def _rmsnorm_kernel(x_ref, g_ref, o_ref):
    x = x_ref[...]
    ms = jnp.mean(x * x, axis=-1, keepdims=True)
    o_ref[...] = x * lax.rsqrt(ms + NORM_EPS) * g_ref[...]


def rmsnorm_call(x, g):
    t, d = x.shape
    tm = _row_tile(t)
    return pl.pallas_call(
        _rmsnorm_kernel,
        out_shape=jax.ShapeDtypeStruct((t, d), F32),
        grid=(t // tm,),
        in_specs=[pl.BlockSpec((tm, d), lambda i: (i, 0)), pl.BlockSpec((1, d), lambda i: (0, 0))],
        out_specs=pl.BlockSpec((tm, d), lambda i: (i, 0)),
        compiler_params=_cparams(("parallel",)),
        name="final_rmsnorm",
    )(x, g.reshape(1, d))


ROUTER_LANES = 128


def _route(xn, rw, rb, lane, n_exp):
    logits = _dot(xn, rw, HI) + rb
    lg = jnp.where(lane < N_GROUPS, logits, -jnp.inf)
    gm = jnp.max(lg, axis=-1, keepdims=True)
    top_pg = 1.0 / jnp.sum(jnp.exp(lg - gm), axis=-1, keepdims=True)
    grp = jnp.min(jnp.where(lg == gm, lane, ROUTER_LANES), axis=-1, keepdims=True)
    in_grp = (lane >= N_GROUPS) & (lane < N_GROUPS + n_exp) & (((lane - N_GROUPS) // EXP_PER_GROUP) == grp)
    le = jnp.where(in_grp, logits, -jnp.inf)
    t1 = jnp.max(le, axis=-1, keepdims=True)
    i1 = jnp.min(jnp.where(le == t1, lane, ROUTER_LANES), axis=-1, keepdims=True)
    le2 = jnp.where(lane == i1, -jnp.inf, le)
    t2 = jnp.max(le2, axis=-1, keepdims=True)
    i2 = jnp.min(jnp.where(le2 == t2, lane, ROUTER_LANES), axis=-1, keepdims=True)
    ex = jnp.exp(t2 - t1)
    w1 = 1.0 / (1.0 + ex)
    return i1, i2, w1 * top_pg, (ex * w1) * top_pg


def _moe_kernel(x_ref, g_ref, rw_ref, rb_ref, wg_ref, wu_ref, wd_ref, o_ref, xn_scr, gate_scr, acc_scr, *, n_exp):
    e = pl.program_id(1)
    tm = x_ref.shape[0]
    lane = lax.broadcasted_iota(jnp.int32, (tm, ROUTER_LANES), 1)

    @pl.when(e == 0)
    def _():
        x = x_ref[...]
        ms = jnp.mean(x * x, axis=-1, keepdims=True)
        xn = x * lax.rsqrt(ms + NORM_EPS) * g_ref[...]
        xn_scr[...] = xn.astype(BF16)
        i1, i2, g1, g2 = _route(xn, rw_ref[...], rb_ref[...], lane, n_exp)
        gate_scr[...] = jnp.where(lane == i1, g1, 0.0) + jnp.where(lane == i2, g2, 0.0)
        acc_scr[...] = jnp.zeros_like(acc_scr)

    xn = xn_scr[...]
    hg = _dot(xn, wg_ref[0])
    hu = _dot(xn, wu_ref[0])
    gcol = jnp.sum(jnp.where(lane == e + N_GROUPS, gate_scr[...], 0.0), axis=-1, keepdims=True)
    hid = hg * _sigmoid(hg) * hu * gcol
    acc_scr[...] += _dot(hid.astype(BF16), wd_ref[0])

    @pl.when(e == n_exp - 1)
    def _():
        o_ref[...] = x_ref[...] + acc_scr[...]


def moe_layer(x, g, rw, rb, wg, wu, wd):
    t, d = x.shape
    n_exp, _, f = wg.shape
    tm = _row_tile(t)
    return pl.pallas_call(
        functools.partial(_moe_kernel, n_exp=n_exp),
        out_shape=jax.ShapeDtypeStruct((t, d), F32),
        grid=(t // tm, n_exp),
        in_specs=[pl.BlockSpec((tm, d), lambda i, e: (i, 0)),
                  pl.BlockSpec((1, d), lambda i, e: (0, 0)),
                  pl.BlockSpec((d, ROUTER_LANES), lambda i, e: (0, 0)),
                  pl.BlockSpec((1, ROUTER_LANES), lambda i, e: (0, 0)),
                  pl.BlockSpec((1, d, f), lambda i, e: (e, 0, 0)),
                  pl.BlockSpec((1, d, f), lambda i, e: (e, 0, 0)),
                  pl.BlockSpec((1, f, d), lambda i, e: (e, 0, 0))],
        out_specs=pl.BlockSpec((tm, d), lambda i, e: (i, 0)),
        scratch_shapes=[pltpu.VMEM((tm, d), BF16), pltpu.VMEM((tm, ROUTER_LANES), F32), pltpu.VMEM((tm, d), F32)],
        compiler_params=_cparams(("parallel", "arbitrary")),
        name="hier_moe",
    )(x, g.reshape(1, d), rw, rb, wg, wu, wd)


MOE_ROW_TILE = 512
MOE_COPY_CHUNK = 256
MOE_COMBINE_TILE = 256


def _router_kernel(x_ref, g_ref, rw_ref, rb_ref, tri_ref, gate_ref, info_ref, cnt_ref, base_scr, *, n_exp, n_tiles):
    i = pl.program_id(0)
    tm = x_ref.shape[0]
    lane = lax.broadcasted_iota(jnp.int32, (tm, ROUTER_LANES), 1)

    @pl.when(i == 0)
    def _():
        base_scr[...] = jnp.zeros_like(base_scr)

    x = x_ref[...]
    ms = jnp.mean(x * x, axis=-1, keepdims=True)
    xn = x * lax.rsqrt(ms + NORM_EPS) * g_ref[...]
    i1, i2, g1, g2 = _route(xn, rw_ref[...], rb_ref[...], lane, n_exp)
    chosen = jnp.where((lane == i1) | (lane == i2), 1.0, 0.0)
    before = _dot(tri_ref[...], chosen.astype(BF16)) + base_scr[...]
    r1 = jnp.sum(jnp.where(lane == i1, before, 0.0), axis=-1, keepdims=True)
    r2 = jnp.sum(jnp.where(lane == i2, before, 0.0), axis=-1, keepdims=True)
    base_scr[...] += jnp.sum(chosen, axis=0, keepdims=True)
    gate_ref[...] = jnp.where(lane == 0, g1, 0.0) + jnp.where(lane == 1, g2, 0.0)
    e1 = (i1 - N_GROUPS).astype(F32)
    e2 = (i2 - N_GROUPS).astype(F32)
    info_ref[...] = (jnp.where(lane == 0, e1, 0.0) + jnp.where(lane == 1, e2, 0.0)
                     + jnp.where(lane == 2, r1, 0.0) + jnp.where(lane == 3, r2, 0.0))

    @pl.when(i == n_tiles - 1)
    def _():
        cnt_ref[...] = base_scr[...]


def _scatter_rows_kernel(pos0_ref, pos1_ref, x_ref, xs_in_hbm, xs_hbm, stage, sem, *, CH, n_chunks):
    del xs_in_hbm
    c = pl.program_id(0)
    slot = c % 2
    stage[slot] = x_ref[...]

    def body(r, carry):
        t = c * CH + r
        src = stage.at[slot, pl.ds(r, 1), :]
        pltpu.make_async_copy(src, xs_hbm.at[pl.ds(pos0_ref[t], 1), :], sem.at[slot]).start(priority=0)
        pltpu.make_async_copy(src, xs_hbm.at[pl.ds(pos1_ref[t], 1), :], sem.at[slot]).start(priority=1)
        return carry

    lax.fori_loop(0, CH, body, 0, unroll=8)

    def drain(s):
        pltpu.make_async_copy(stage.at[s], xs_hbm.at[pl.ds(0, CH), :], sem.at[s]).wait()
        pltpu.make_async_copy(stage.at[s], xs_hbm.at[pl.ds(0, CH), :], sem.at[s]).wait()

    @pl.when(c > 0)
    def _():
        drain(1 - slot)

    @pl.when(c == n_chunks - 1)
    def _():
        drain(slot)


def _expert_kernel(te_ref, nv_ref, xs_ref, g_ref, wg_ref, wu_ref, wd_ref, y_ref):
    @pl.when(pl.program_id(0) < nv_ref[0])
    def _():
        x = xs_ref[...]
        ms = jnp.mean(x * x, axis=-1, keepdims=True)
        xn = (x * lax.rsqrt(ms + NORM_EPS) * g_ref[...]).astype(BF16)
        hg = _dot(xn, wg_ref[0])
        hu = _dot(xn, wu_ref[0])
        hid = hg * _sigmoid(hg) * hu
        y_ref[...] = _dot(hid.astype(BF16), wd_ref[0])

    @pl.when(pl.program_id(0) >= nv_ref[0])
    def _():
        y_ref[...] = jnp.zeros_like(y_ref)


def _combine_kernel(pos0_ref, pos1_ref, x_ref, gate_ref, fg_ref, y_hbm, o_ref, ybuf, sem, *, TC, n_tiles, final_norm):
    i = pl.program_id(0)

    def issue(tile, slot):
        def body(r, carry):
            t = tile * TC + r
            pltpu.make_async_copy(y_hbm.at[pl.ds(pos0_ref[t], 1), :], ybuf.at[slot, 0, pl.ds(r, 1), :],
                                  sem.at[slot]).start(priority=0)
            pltpu.make_async_copy(y_hbm.at[pl.ds(pos1_ref[t], 1), :], ybuf.at[slot, 1, pl.ds(r, 1), :],
                                  sem.at[slot]).start(priority=1)
            return carry
        lax.fori_loop(0, TC, body, 0, unroll=8)

    @pl.when(i == 0)
    def _():
        issue(0, 0)

    @pl.when(i + 1 < n_tiles)
    def _():
        issue(i + 1, (i + 1) % 2)

    slot = i % 2
    pltpu.make_async_copy(y_hbm.at[pl.ds(0, TC), :], ybuf.at[slot, 0], sem.at[slot]).wait()
    pltpu.make_async_copy(y_hbm.at[pl.ds(0, TC), :], ybuf.at[slot, 1], sem.at[slot]).wait()
    gate = gate_ref[...]
    out = x_ref[...] + gate[:, 0:1] * ybuf[slot, 0] + gate[:, 1:2] * ybuf[slot, 1]
    if final_norm:
        ms = jnp.mean(out * out, axis=-1, keepdims=True)
        out = out * lax.rsqrt(ms + NORM_EPS) * fg_ref[...]
    o_ref[...] = out


def moe_layer_sparse(x, g, rw, rb, wg, wu, wd, final_g=None):
    t, d = x.shape
    n_exp, _, f = wg.shape
    TM = MOE_ROW_TILE
    n_tiles = t // TM
    tri = jnp.asarray(np.tril(np.ones((TM, TM), np.float32), -1)).astype(BF16)
    gate, info, cnt = pl.pallas_call(
        functools.partial(_router_kernel, n_exp=n_exp, n_tiles=n_tiles),
        out_shape=[jax.ShapeDtypeStruct((t, ROUTER_LANES), F32), jax.ShapeDtypeStruct((t, ROUTER_LANES), F32),
                   jax.ShapeDtypeStruct((1, ROUTER_LANES), F32)],
        grid=(n_tiles,),
        in_specs=[pl.BlockSpec((TM, d), lambda i: (i, 0)),
                  pl.BlockSpec((1, d), lambda i: (0, 0)),
                  pl.BlockSpec((d, ROUTER_LANES), lambda i: (0, 0)),
                  pl.BlockSpec((1, ROUTER_LANES), lambda i: (0, 0)),
                  pl.BlockSpec((TM, TM), lambda i: (0, 0))],
        out_specs=[pl.BlockSpec((TM, ROUTER_LANES), lambda i: (i, 0)),
                   pl.BlockSpec((TM, ROUTER_LANES), lambda i: (i, 0)),
                   pl.BlockSpec((1, ROUTER_LANES), lambda i: (0, 0))],
        scratch_shapes=[pltpu.VMEM((1, ROUTER_LANES), F32)],
        compiler_params=_cparams(("arbitrary",)),
        name="moe_router",
    )(x, g.reshape(1, d), rw, rb, tri)

    counts = cnt[0, N_GROUPS:N_GROUPS + n_exp].astype(jnp.int32)
    padded = ((counts + TM - 1) // TM) * TM
    ends = jnp.cumsum(padded)
    offs = ends - padded
    eid = info[:, 0:2].astype(jnp.int32)
    rank = info[:, 2:4].astype(jnp.int32)
    pos = jnp.sum(jnp.where(eid[:, :, None] == jnp.arange(n_exp)[None, None, :], offs[None, None, :], 0), axis=-1) + rank
    pos0, pos1 = pos[:, 0], pos[:, 1]
    max_tiles = (2 * t) // TM + n_exp
    n_valid = (ends[-1] // TM).astype(jnp.int32).reshape(1)
    tile_exp = jnp.minimum(jnp.sum((ends[None, :] // TM) <= jnp.arange(max_tiles)[:, None], axis=-1),
                           n_exp - 1).astype(jnp.int32)
    p_rows = max_tiles * TM

    CH = MOE_COPY_CHUNK
    xs = pl.pallas_call(
        functools.partial(_scatter_rows_kernel, CH=CH, n_chunks=t // CH),
        out_shape=jax.ShapeDtypeStruct((p_rows, d), F32),
        grid_spec=pltpu.PrefetchScalarGridSpec(
            num_scalar_prefetch=2, grid=(t // CH,),
            in_specs=[pl.BlockSpec((CH, d), lambda c, p0, p1: (c, 0)), pl.BlockSpec(memory_space=pl.ANY)],
            out_specs=pl.BlockSpec(memory_space=pl.ANY),
            scratch_shapes=[pltpu.VMEM((2, CH, d), F32), pltpu.SemaphoreType.DMA((2,))]),
        input_output_aliases={3: 0},
        compiler_params=pltpu.CompilerParams(dimension_semantics=("arbitrary",), vmem_limit_bytes=VMEM_LIMIT,
                                             has_side_effects=True),
        name="moe_scatter_rows",
    )(pos0, pos1, x, jnp.zeros((p_rows, d), F32))

    def row_idx(i, te, nv):
        return (jnp.minimum(i, nv[0] - 1), 0)

    ys = pl.pallas_call(
        _expert_kernel,
        out_shape=jax.ShapeDtypeStruct((p_rows, d), F32),
        grid_spec=pltpu.PrefetchScalarGridSpec(
            num_scalar_prefetch=2, grid=(max_tiles,),
            in_specs=[pl.BlockSpec((TM, d), row_idx),
                      pl.BlockSpec((1, d), lambda i, te, nv: (0, 0)),
                      pl.BlockSpec((1, d, f), lambda i, te, nv: (te[i], 0, 0)),
                      pl.BlockSpec((1, d, f), lambda i, te, nv: (te[i], 0, 0)),
                      pl.BlockSpec((1, f, d), lambda i, te, nv: (te[i], 0, 0))],
            out_specs=pl.BlockSpec((TM, d), lambda i, te, nv: (i, 0))),
        compiler_params=_cparams(("arbitrary",)),
        name="moe_experts",
    )(tile_exp, n_valid, xs, g.reshape(1, d), wg, wu, wd)

    TC = MOE_COMBINE_TILE
    return pl.pallas_call(
        functools.partial(_combine_kernel, TC=TC, n_tiles=t // TC, final_norm=final_g is not None),
        out_shape=jax.ShapeDtypeStruct((t, d), F32),
        grid_spec=pltpu.PrefetchScalarGridSpec(
            num_scalar_prefetch=2, grid=(t // TC,),
            in_specs=[pl.BlockSpec((TC, d), lambda i, p0, p1: (i, 0)),
                      pl.BlockSpec((TC, ROUTER_LANES), lambda i, p0, p1: (i, 0)),
                      pl.BlockSpec((1, d), lambda i, p0, p1: (0, 0)),
                      pl.BlockSpec(memory_space=pl.ANY)],
            out_specs=pl.BlockSpec((TC, d), lambda i, p0, p1: (i, 0)),
            scratch_shapes=[pltpu.VMEM((2, 2, TC, d), F32), pltpu.SemaphoreType.DMA((2,))]),
        compiler_params=_cparams(("arbitrary",)),
        name="moe_combine",
    )(pos0, pos1, x, gate, (g if final_g is None else final_g).reshape(1, d), ys)


def _rwkv_kernel(p_ref, prev_ref, s0_ref, mu_ref, w0_ref, wd_ref, a0_ref, wa_ref, wg_ref, kk_ref, ka_ref,
                 bonus_ref, lng_ref, lnb_ref, tri_ref, hsum_ref, ya_ref, sf_ref, s_scr, prev_scr,
                 *, NB, C, H, DH, n_chunks):
    c = pl.program_id(1)

    @pl.when(c == 0)
    def _():
        s_scr[...] = s0_ref[:, 0]
        prev_scr[...] = prev_ref[:, 0]

    DA = H * DH
    R = NB * C
    p = p_ref[...].reshape(R, p_ref.shape[-1])
    row = lax.broadcasted_iota(jnp.int32, p.shape, 0)
    shifted = pltpu.roll(p, 1, axis=0)
    for n in range(NB):
        shifted = jnp.where(row == n * C, prev_scr[n], shifted)
        prev_scr[n] = p[(n + 1) * C - 1:(n + 1) * C, :]
    xs = p + (shifted - p) * mu_ref[...]
    r = xs[:, 0:DA]
    k = xs[:, DA:2 * DA]
    v = xs[:, 2 * DA:3 * DA]
    lora = xs[:, 3 * DA:3 * DA + R_DECAY + R_ICLR]
    gd = xs[:, 3 * DA + R_DECAY + R_ICLR:3 * DA + R_DECAY + R_ICLR + R_GATE]

    w_log = -_softplus(-(w0_ref[...] + _mp_dot(_dot, jnp.tanh(lora), wd_ref[...], 3))) - 0.5
    lw = -jnp.exp(w_log)
    a = _sigmoid(a0_ref[...] + _mp_dot(_dot, lora, wa_ref[...], 3))
    g = _mp_dot(_dot, _sigmoid(gd), wg_ref[...], 3)

    kk = k * kk_ref[...]
    ss = _dot_exact_rhs(kk * kk, hsum_ref[...], 2)
    kk = kk / jnp.maximum(jnp.sqrt(ss), 1e-12)
    k2 = k * (1.0 + (a - 1.0) * ka_ref[...])
    kka = kk * a

    cum = jnp.concatenate([_dot_exact_lhs(tri_ref[...], lw[n * C:(n + 1) * C], 3) for n in range(NB)], axis=0)
    p_in = jnp.exp(cum)
    r_t = r * p_in
    a_t = kk * jnp.exp(cum - lw)
    p_inv = jnp.exp(-cum)
    b_t = kka * p_inv
    k_t = k2 * p_inv
    bonus = _dot_exact_rhs(r * k2 * bonus_ref[...], hsum_ref[...], 2) * v

    ri = lax.broadcasted_iota(jnp.int32, (C, C), 0)
    ci = lax.broadcasted_iota(jnp.int32, (C, C), 1)
    strict = ri > ci
    incl = ri >= ci
    eye = (ri == ci).astype(F32)
    n_double = max(int(math.ceil(math.log2(C))) - 1, 0)

    chains = [(n, h) for n in range(NB) for h in range(H)]

    def blk(x, n, h):
        return x[n * C:(n + 1) * C, h * DH:(h + 1) * DH]

    def bf(x):
        return x.astype(BF16)

    Bt = [bf(blk(b_t, n, h)) for n, h in chains]
    Kt = [bf(blk(k_t, n, h)) for n, h in chains]
    Vf = [blk(v, n, h) for n, h in chains]
    AR = [bf(jnp.concatenate([blk(a_t, n, h), blk(r_t, n, h)], axis=0)) for n, h in chains]
    S0 = [s_scr[n, h] for n, h in chains]
    idx = range(len(chains))
    GB = [_dot_nt(AR[i], Bt[i]) for i in idx]
    GK = [_dot_nt(AR[i], Kt[i]) for i in idx]
    ARS = [_dot_nt(AR[i], bf(S0[i])) for i in idx]
    Lm = [jnp.where(strict, GB[i][0:C], 0.0) for i in idx]
    Gb = [bf(jnp.where(incl, GB[i][C:2 * C], 0.0)) for i in idx]
    MG = [bf(jnp.concatenate([jnp.where(strict, GK[i][0:C], 0.0), jnp.where(incl, GK[i][C:2 * C], 0.0)], axis=0))
          for i in idx]
    MGV = [_dot(MG[i], bf(Vf[i])) for i in idx]
    T = [eye - Lm[i] for i in idx]
    Pw = [bf(Lm[i]) for i in idx]
    for _ in range(n_double):
        Pw = [bf(_dot(Pw[i], Pw[i])) for i in idx]
        T = [T[i] + _dot(bf(T[i]), Pw[i]) for i in idx]
    U = [_dot(bf(T[i]), bf(-(ARS[i][0:C] + MGV[i][0:C]))) for i in idx]
    Y = [ARS[i][C:2 * C] + _dot(Gb[i], bf(U[i])) + MGV[i][C:2 * C] for i in idx]
    for i, (n, h) in enumerate(chains):
        UV = bf(jnp.concatenate([U[i], Vf[i]], axis=0))
        BK = jnp.concatenate([Bt[i], Kt[i]], axis=0)
        p_tot = p_in[(n + 1) * C - 1:(n + 1) * C, h * DH:(h + 1) * DH]
        s_scr[n, h] = (S0[i] + _dot_tn(UV, BK)) * p_tot

    rows = []
    for n in range(NB):
        ys = []
        for h in range(H):
            Yh = Y[n * H + h]
            yc = Yh - jnp.mean(Yh, axis=-1, keepdims=True)
            var = jnp.mean(yc * yc, axis=-1, keepdims=True)
            ys.append(yc * lax.rsqrt(var + GN_EPS))
        rows.append(jnp.concatenate(ys, axis=-1))
    y = jnp.concatenate(rows, axis=0) * lng_ref[...] + lnb_ref[...]
    ya_ref[...] = ((y + bonus) * g).reshape(NB, C, DA)

    @pl.when(c == n_chunks - 1)
    def _():
        sf_ref[:, 0] = s_scr[...]


def rwkv_mix(pa, n_batch, seq, shift_prev, wkv0, wts, n_par):
    t, ap = pa.shape
    H = wkv0.shape[1]
    DA = H * DH_A
    C = min(RWKV_CHUNK, seq)
    n_chunks = seq // C
    NB = n_par
    G = n_batch // NB
    tri = jnp.asarray(np.tril(np.ones((C, C), np.float32))).astype(BF16)
    hsum = jnp.asarray(np.kron(np.eye(H, dtype=np.float32), np.ones((DH_A, DH_A), np.float32))).astype(BF16)

    def full(shape):
        nd = len(shape)
        return pl.BlockSpec(shape, lambda b, c: (0,) * nd)

    vec = full((1, DA))
    ya, s_fin = pl.pallas_call(
        functools.partial(_rwkv_kernel, NB=NB, C=C, H=H, DH=DH_A, n_chunks=n_chunks),
        out_shape=[jax.ShapeDtypeStruct((NB, t // NB, DA), F32),
                   jax.ShapeDtypeStruct((NB, G, H, DH_A, DH_A), F32)],
        grid=(G, n_chunks),
        in_specs=[pl.BlockSpec((NB, C, ap), lambda b, c: (0, b * n_chunks + c, 0)),
                  pl.BlockSpec((NB, 1, 1, ap), lambda b, c: (0, b, 0, 0)),
                  pl.BlockSpec((NB, 1, H, DH_A, DH_A), lambda b, c: (0, b, 0, 0, 0)),
                  full((1, ap)), vec, full((R_DECAY + R_ICLR, DA)), vec, full((R_DECAY + R_ICLR, DA)),
                  full((R_GATE, DA)), vec, vec, vec, vec, vec, full((C, C)), full((DA, DA))],
        out_specs=[pl.BlockSpec((NB, C, DA), lambda b, c: (0, b * n_chunks + c, 0)),
                   pl.BlockSpec((NB, 1, H, DH_A, DH_A), lambda b, c: (0, b, 0, 0, 0))],
        scratch_shapes=[pltpu.VMEM((NB, H, DH_A, DH_A), F32), pltpu.VMEM((NB, 1, ap), F32)],
        compiler_params=_cparams(("parallel", "arbitrary")),
        name="rwkv7_mix",
    )(pa.reshape(NB, t // NB, ap), shift_prev.reshape(NB, G, 1, ap), wkv0.reshape(NB, G, H, DH_A, DH_A),
      wts["mu"], wts["w0"], wts["wd"], wts["a0"], wts["wa"],
      wts["wg"], wts["key_k"], wts["key_a"], wts["bonus"], wts["lnx_g"], wts["lnx_b"], tri, hsum)
    return ya.reshape(t, DA), s_fin.reshape(wkv0.shape)


GMLP_TILE = 128


def _gmlp_kernel(u_ref, v_ref, ng_ref, nb_ref, wm_ref, bias_ref, o_ref, vn_ref):
    vf = _gelu(v_ref[...])
    mu = jnp.mean(vf, axis=-1, keepdims=True)
    vc = vf - mu
    var = jnp.mean(vc * vc, axis=-1, keepdims=True)
    vn = vc * lax.rsqrt(var + NORM_EPS) * ng_ref[...] + nb_ref[...]
    vn_ref[...] = vn
    vb = vn.astype(BF16)
    n_h = wm_ref.shape[0]
    cb = vn.shape[1] // n_h
    s = jnp.concatenate([_dot(wm_ref[h], vb[:, h * cb:(h + 1) * cb]) for h in range(n_h)], axis=-1)
    o_ref[...] = _gelu(u_ref[...]) * (s + bias_ref[...])


def gmlp_mix(pu, pv, ng, nb, wm_bf16, bias_tile):
    t, db = pu.shape
    n_h = wm_bf16.shape[0]
    return pl.pallas_call(
        _gmlp_kernel,
        out_shape=[jax.ShapeDtypeStruct((t, db), F32), jax.ShapeDtypeStruct((t, db), F32)],
        grid=(t // GMLP_TILE,),
        in_specs=[pl.BlockSpec((GMLP_TILE, db), lambda i: (i, 0)),
                  pl.BlockSpec((GMLP_TILE, db), lambda i: (i, 0)),
                  pl.BlockSpec((1, db), lambda i: (0, 0)),
                  pl.BlockSpec((1, db), lambda i: (0, 0)),
                  pl.BlockSpec((n_h, GMLP_TILE, GMLP_TILE), lambda i: (0, 0, 0)),
                  pl.BlockSpec((GMLP_TILE, db), lambda i: (0, 0))],
        out_specs=[pl.BlockSpec((GMLP_TILE, db), lambda i: (i, 0)),
                   pl.BlockSpec((GMLP_TILE, db), lambda i: (i, 0))],
        compiler_params=_cparams(("parallel",)),
        name="gmlp_mix",
    )(pu, pv, ng, nb, wm_bf16, bias_tile)


N_SEG = 8


def _rglru_kernel(xb_ref, gy_ref, cprev_ref, h0_ref, cw_ref, cb_ref, gw_ref, gb_ref, lam_ref,
                  yc_ref, ctail_ref, hl_ref, xe_scr, a_scr, b_scr, h_scr, *, TL, DC, pos0, n_tiles):
    l = pl.program_id(1)
    PAD = 8

    @pl.when(l == 0)
    def _():
        xe_scr[0:PAD, :] = cprev_ref[0]
        h_scr[...] = h0_ref[0]

    xe_scr[PAD:PAD + TL, :] = xb_ref[...]
    xc = cb_ref[...] + xe_scr[pl.ds(PAD - (CONV_W - 1), TL), :] * cw_ref[0:1, :]
    for i in range(1, CONV_W):
        xc = xc + xe_scr[pl.ds(PAD - (CONV_W - 1) + i, TL), :] * cw_ref[i:i + 1, :]
    tail = xe_scr[TL:TL + PAD, :]
    ctail_ref[0] = tail
    xe_scr[0:PAD, :] = tail

    gates = _dot(xc.astype(BF16), gw_ref[...]) + gb_ref[...]
    rg = _sigmoid(gates[:, 0:DC])
    ig = _sigmoid(gates[:, DC:2 * DC])
    log_a = -LRU_C * rg * _softplus(-lam_ref[...])
    a = jnp.exp(log_a)
    mult = jnp.sqrt(1.0 - jnp.exp(2.0 * log_a))
    row = lax.broadcasted_iota(jnp.int32, (TL, DC), 0)
    mult = jnp.where(row + (l * TL + pos0) == 0, 1.0, mult)
    b = mult * ig * xc
    n_slab = DC // LANES
    for s in range(n_slab):
        a_scr[s] = a[:, s * LANES:(s + 1) * LANES]
        b_scr[s] = b[:, s * LANES:(s + 1) * LANES]

    seg = TL // N_SEG

    def step(i, carry):
        idx = pl.ds(i, N_SEG, stride=seg) if seg > 1 else pl.ds(0, N_SEG)
        out = []
        for s in range(n_slab):
            hloc, ap = carry[s]
            ai = a_scr[s, idx, :]
            hloc = ai * hloc + b_scr[s, idx, :]
            ap = ap * ai
            b_scr[s, idx, :] = hloc
            a_scr[s, idx, :] = ap
            out.append((hloc, ap))
        return tuple(out)

    lax.fori_loop(0, seg, step,
                  tuple((jnp.zeros((N_SEG, LANES), F32), jnp.ones((N_SEG, LANES), F32)) for _ in range(n_slab)))

    carry = h_scr[...]
    g_act = _gelu(gy_ref[...])
    for j in range(N_SEG):
        rows = slice(j * seg, (j + 1) * seg)
        hloc = jnp.concatenate([b_scr[s, rows, :] for s in range(n_slab)], axis=-1)
        ap = jnp.concatenate([a_scr[s, rows, :] for s in range(n_slab)], axis=-1)
        hj = hloc + ap * carry
        yc_ref[rows, :] = g_act[rows, :] * hj
        carry = hj[seg - 1:seg, :]
    h_scr[...] = carry

    @pl.when(l == n_tiles - 1)
    def _():
        hl_ref[0] = carry


def rglru_mix(xb, gy, n_batch, seq, conv_prev8, h0, pos0, wts):
    t, dc = xb.shape
    TL = 512 if seq % 512 == 0 else seq
    n_tiles = seq // TL

    def full(shape):
        nd = len(shape)
        return pl.BlockSpec(shape, lambda b, l: (0,) * nd)

    yc, ctail, hl = pl.pallas_call(
        functools.partial(_rglru_kernel, TL=TL, DC=dc, pos0=pos0, n_tiles=n_tiles),
        out_shape=[jax.ShapeDtypeStruct((t, dc), F32), jax.ShapeDtypeStruct((n_batch, 8, dc), F32),
                   jax.ShapeDtypeStruct((n_batch, 1, dc), F32)],
        grid=(n_batch, n_tiles),
        in_specs=[pl.BlockSpec((TL, dc), lambda b, l: (b * n_tiles + l, 0)),
                  pl.BlockSpec((TL, dc), lambda b, l: (b * n_tiles + l, 0)),
                  pl.BlockSpec((1, 8, dc), lambda b, l: (b, 0, 0)),
                  pl.BlockSpec((1, 1, dc), lambda b, l: (b, 0, 0)),
                  full((CONV_W, dc)), full((1, dc)), full((dc, 2 * dc)), full((1, 2 * dc)), full((1, dc))],
        out_specs=[pl.BlockSpec((TL, dc), lambda b, l: (b * n_tiles + l, 0)),
                   pl.BlockSpec((1, 8, dc), lambda b, l: (b, 0, 0)),
                   pl.BlockSpec((1, 1, dc), lambda b, l: (b, 0, 0))],
        scratch_shapes=[pltpu.VMEM((TL + 8, dc), F32), pltpu.VMEM((dc // LANES, TL, LANES), F32),
                        pltpu.VMEM((dc // LANES, TL, LANES), F32), pltpu.VMEM((1, dc), F32)],
        compiler_params=_cparams(("parallel", "arbitrary")),
        name="rglru_mix",
    )(xb, gy, conv_prev8, h0.reshape(n_batch, 1, dc), wts["conv_w"], wts["conv_b"], wts["gate_w"], wts["gate_b"],
      wts["lam"])
    return yc, ctail[:, 8 - (CONV_W - 1):, :], hl.reshape(n_batch, dc)


def _t5_bucket(dist):
    dist = np.asarray(dist)
    max_exact = N_BUCKETS // 2
    scaled = np.log(np.maximum(dist, 1) / max_exact) / math.log(BUCKET_MAX_DIST / max_exact)
    large = np.minimum(max_exact + (scaled * (N_BUCKETS - max_exact)).astype(np.int32), N_BUCKETS - 1)
    return np.where(dist < max_exact, dist, large).astype(np.int32)


def _dist_table(rel_bias, max_dist):
    dist = np.arange(max_dist + 1)
    count = np.zeros(max_dist + 1, np.float32)
    for window, dil in DILATED:
        count += ((dist % dil == 0) & (dist <= window)).astype(np.float32)
    logcnt = np.where(count > 0, np.log(np.maximum(count, 1.0)), 0.0).astype(np.float32)
    tab = jnp.take(rel_bias, jnp.asarray(_t5_bucket(dist)), axis=0) + jnp.asarray(logcnt)[:, None]
    return jnp.where(jnp.asarray(count > 0)[:, None], tab, NEG_BIG)


def _toeplitz_tiles(tab, n_pos, n_neg, T):
    D, H = tab.shape
    span = T * n_pos
    assert D >= span
    n_col = span + T * n_neg + T - 1
    ext = jnp.concatenate([jnp.flip(tab[:span], axis=0), jnp.full((n_col + 1 - span, H), NEG_BIG, F32)], axis=0)
    ext = jnp.transpose(ext)
    skew = jnp.tile(ext, (1, T))[:, :T * n_col].reshape(H, T, n_col)
    tiles = [skew[:, :, span - 1 - T * dd: span - 1 - T * dd + T] for dd in range(-n_neg, n_pos)]
    return jnp.stack(tiles, axis=1)


def _attn_prompt_kernel(q_ref, k_ref, v_ref, bias_ref, o_ref, kb_scr, vb_scr, *, E, SUB, NS):
    qi = pl.program_id(2)
    TQ = NS * SUB

    @pl.when(qi == 0)
    def _():
        kb_scr[...] = k_ref[0].astype(BF16)
        vb_scr[...] = v_ref[0].astype(BF16)

    lane = lax.broadcasted_iota(jnp.int32, (SUB, 2 * E), 1)
    q2 = []
    for rs in range(NS):
        q = q_ref[0, rs * SUB:(rs + 1) * SUB, :] * (E ** -0.5 * LOG2E)
        q2.append(jnp.concatenate([jnp.where(lane < E, q, 0.0), jnp.where(lane >= E, q, 0.0)], axis=0).astype(BF16))

    def body(i, carry):
        j = qi - i
        kj = kb_scr[pl.ds(pl.multiple_of(j * TQ, TQ), TQ), :]
        vj = vb_scr[pl.ds(pl.multiple_of(j * TQ, TQ), TQ), :]
        out = []
        for rs in range(NS):
            m, l, acc = carry[rs]
            s = _dot_nt(q2[rs], kj)
            parts = []
            for cs in range(NS):
                dd = i * NS + (rs - cs + NS - 1)
                bias = jnp.concatenate([bias_ref[0, dd], bias_ref[1, dd]], axis=0)
                parts.append(s[:, cs * SUB:(cs + 1) * SUB] + bias)
            mx = parts[0]
            for part in parts[1:]:
                mx = jnp.maximum(mx, part)
            m_new = jnp.maximum(m, jnp.max(mx, axis=-1, keepdims=True))
            alpha = jnp.exp2(m - m_new)
            ps = [jnp.exp2(part - m_new) for part in parts]
            psum = ps[0]
            for pexp in ps[1:]:
                psum = psum + pexp
            l = alpha * l + psum
            acc = alpha * acc + _dot(jnp.concatenate(ps, axis=-1).astype(BF16), vj)
            out.append((m_new, l, acc))
        return tuple(out)

    init = tuple((jnp.full((2 * SUB, SUB), NEG_BIG, F32), jnp.zeros((2 * SUB, SUB), F32),
                  jnp.zeros((2 * SUB, 2 * E), F32)) for _ in range(NS))
    res = lax.fori_loop(0, qi + 1, body, init)
    for rs in range(NS):
        m, l, acc = res[rs]
        o = acc / jnp.sum(l, axis=-1, keepdims=True)
        o_ref[0, rs * SUB:(rs + 1) * SUB, :] = jnp.where(lane < E, o[0:SUB], o[SUB:2 * SUB])


def attn_prompt(q, k, v, bias_tiles, n_batch, seq):
    hd = q.shape[-1]
    E = hd // H_D
    SUB = ATT_TILE
    NS = ATT_SUBTILES
    TQ = SUB * NS
    nq = seq // TQ
    nt = bias_tiles.shape[1]
    return pl.pallas_call(
        functools.partial(_attn_prompt_kernel, E=E, SUB=SUB, NS=NS),
        out_shape=jax.ShapeDtypeStruct((n_batch, seq, hd), F32),
        grid=(H_D // 2, n_batch, nq),
        in_specs=[pl.BlockSpec((1, TQ, 2 * E), lambda hp, b, i: (b, i, hp)),
                  pl.BlockSpec((1, seq, 2 * E), lambda hp, b, i: (b, 0, hp)),
                  pl.BlockSpec((1, seq, 2 * E), lambda hp, b, i: (b, 0, hp)),
                  pl.BlockSpec((2, nt, SUB, SUB), lambda hp, b, i: (hp, 0, 0, 0))],
        out_specs=pl.BlockSpec((1, TQ, 2 * E), lambda hp, b, i: (b, i, hp)),
        scratch_shapes=[pltpu.VMEM((seq, 2 * E), BF16), pltpu.VMEM((seq, 2 * E), BF16)],
        compiler_params=_cparams(("arbitrary", "arbitrary", "arbitrary")),
        name="dilated_attn_prompt",
    )(q, k, v, bias_tiles)


def _attn_sample_kernel(q_ref, kn_ref, vn_ref, ck_ref, cv_ref, bo_ref, bn_ref, o_ref, *, E, S):
    lane = lax.broadcasted_iota(jnp.int32, (S, 2 * E), 1)
    NPAD = bn_ref.shape[-1]
    outs = []
    for hp in range(H_D // 2):
        sl = slice(hp * 2 * E, (hp + 1) * 2 * E)
        q = q_ref[0, :, sl] * (E ** -0.5)
        q2 = jnp.concatenate([jnp.where(lane < E, q, 0.0), jnp.where(lane >= E, q, 0.0)], axis=0).astype(BF16)
        zpad = jnp.zeros((NPAD - S, 2 * E), F32)
        kn = jnp.concatenate([kn_ref[0, :, sl], zpad], axis=0).astype(BF16)
        vn = jnp.concatenate([vn_ref[0, :, sl], zpad], axis=0).astype(BF16)
        s_old = _dot_nt(q2, ck_ref[0, :, sl].astype(BF16)) + jnp.concatenate([bo_ref[2 * hp], bo_ref[2 * hp + 1]], axis=0)
        s_new = _dot_nt(q2, kn) + jnp.concatenate([bn_ref[2 * hp], bn_ref[2 * hp + 1]], axis=0)
        m = jnp.maximum(jnp.max(s_old, axis=-1, keepdims=True), jnp.max(s_new, axis=-1, keepdims=True))
        p_old = jnp.exp(s_old - m)
        p_new = jnp.exp(s_new - m)
        l = jnp.sum(p_old, axis=-1, keepdims=True) + jnp.sum(p_new, axis=-1, keepdims=True)
        acc = _dot(p_old.astype(BF16), cv_ref[0, :, sl].astype(BF16)) + _dot(p_new.astype(BF16), vn)
        o = acc / l
        outs.append(jnp.where(lane < E, o[0:S], o[S:2 * S]))
    o_ref[0] = jnp.concatenate(outs, axis=-1)


def attn_sample(q, k_new, v_new, cache_k, cache_v, bias_old, bias_new):
    n_batch, S, hd = q.shape
    W = cache_k.shape[1]
    E = hd // H_D
    NPAD = bias_new.shape[-1]
    return pl.pallas_call(
        functools.partial(_attn_sample_kernel, E=E, S=S),
        out_shape=jax.ShapeDtypeStruct((n_batch, S, hd), F32),
        grid=(n_batch,),
        in_specs=[pl.BlockSpec((1, S, hd), lambda b: (b, 0, 0)),
                  pl.BlockSpec((1, S, hd), lambda b: (b, 0, 0)),
                  pl.BlockSpec((1, S, hd), lambda b: (b, 0, 0)),
                  pl.BlockSpec((1, W, hd), lambda b: (b, 0, 0)),
                  pl.BlockSpec((1, W, hd), lambda b: (b, 0, 0)),
                  pl.BlockSpec((H_D, S, W), lambda b: (0, 0, 0)),
                  pl.BlockSpec((H_D, S, NPAD), lambda b: (0, 0, 0))],
        out_specs=pl.BlockSpec((1, S, hd), lambda b: (b, 0, 0)),
        compiler_params=_cparams(("parallel",)),
        name="dilated_attn_sample",
    )(q, k_new, v_new, cache_k, cache_v, bias_old, bias_new)


def _even_weights(j, w_in_even, w_out_even, shift_mu, decay_w0, decay_up, iclr_a0, iclr_up, gate_up, key_k, key_a,
                  bonus_r_k, lnx_g, lnx_b, sgu_norm_g, sgu_norm_b, sgu_w, sgu_b):
    da = decay_w0.shape[1]
    zeros_d = jnp.zeros((R_ICLR, da), F32)
    zeros_i = jnp.zeros((R_DECAY, da), F32)
    return dict(
        w_in=w_in_even[j].astype(BF16),
        w_out_a=w_out_even[j, :da].astype(BF16), w_out_b=w_out_even[j, da:].astype(BF16),
        mu=shift_mu[j].reshape(1, -1), w0=decay_w0[j].reshape(1, -1), a0=iclr_a0[j].reshape(1, -1),
        wd=jnp.concatenate([decay_up[j], zeros_d], axis=0), wa=jnp.concatenate([zeros_i, iclr_up[j]], axis=0),
        wg=gate_up[j], key_k=key_k[j].reshape(1, -1), key_a=key_a[j].reshape(1, -1),
        bonus=bonus_r_k[j].reshape(1, -1), lnx_g=lnx_g[j].reshape(1, -1), lnx_b=lnx_b[j].reshape(1, -1),
        ng=sgu_norm_g[j].reshape(1, -1), nb=sgu_norm_b[j].reshape(1, -1), sgu_w=sgu_w[j], sgu_b=sgu_b[j])


def _gmlp_tables(sgu_w, sgu_b, chunk):
    reps = GMLP_TILE // chunk
    n_h = sgu_w.shape[0]
    cb = None
    wm = sgu_w[:, :chunk, :chunk] * jnp.asarray(np.tril(np.ones((chunk, chunk), np.float32)))
    if reps > 1:
        eye = jnp.asarray(np.eye(reps, dtype=np.float32))
        wm = jnp.einsum("ab,hts->hatbs", eye, wm).reshape(n_h, GMLP_TILE, GMLP_TILE)
    bias = jnp.tile(jnp.transpose(sgu_b[:, :chunk]), (reps, 1))
    return wm.astype(BF16), bias


def _even_layer(x, n_batch, seq, chunk, n_par, shift_prev, wkv0, norm_g, ew):
    pa, pu, pv = norm_matmul(x, norm_g, ew["w_in"], (ew["mu"].shape[1], ew["ng"].shape[1], ew["ng"].shape[1]))
    ya, wkv = rwkv_mix(pa, n_batch, seq, shift_prev, wkv0, ew, n_par)
    wm, bias = _gmlp_tables(ew["sgu_w"], ew["sgu_b"], chunk)
    cb = pu.shape[1] // wm.shape[0]
    bias_tile = jnp.repeat(bias, cb, axis=1)
    yb, vn = gmlp_mix(pu, pv, ew["ng"], ew["nb"], wm, bias_tile)
    x = proj_residual(x, ya, yb, ew["w_out_a"], ew["w_out_b"])
    last = pa.reshape(n_batch, seq, -1)[:, -1]
    return x, last, wkv, vn


def _odd_weights(j, w_in_odd, w_out_odd, conv_w, conv_b, rgate_w, rgate_b, igate_w, igate_b, lru_lambda):
    dc = conv_b.shape[1]
    eye = jnp.asarray(np.eye(H_C, dtype=np.float32))

    def blockdiag(w):
        dh = w.shape[-1]
        return jnp.einsum("ab,aij->aibj", eye, w).reshape(H_C * dh, H_C * dh)

    return dict(
        w_in=w_in_odd[j].astype(BF16),
        w_out_c=w_out_odd[j, :dc].astype(BF16), w_out_d=w_out_odd[j, dc:].astype(BF16),
        conv_w=conv_w[j], conv_b=conv_b[j].reshape(1, -1),
        gate_w=jnp.concatenate([blockdiag(rgate_w[j]), blockdiag(igate_w[j])], axis=1).astype(BF16),
        gate_b=jnp.concatenate([rgate_b[j], igate_b[j]]).reshape(1, -1),
        lam=lru_lambda[j].reshape(1, -1))


def _odd_layer(x, n_batch, seq, conv_prev, h0, pos0, caches, dist_tab, norm_g, ow):
    dc = ow["lam"].shape[1]
    gy, xb, q, k, v = norm_matmul(x, norm_g, ow["w_in"], (dc,) * 5)
    conv_prev8 = jnp.pad(conv_prev, ((0, 0), (8 - (CONV_W - 1), 0), (0, 0)))
    yc, conv_last, h_last = rglru_mix(xb, gy, n_batch, seq, conv_prev8, h0, pos0, ow)
    hd = q.shape[1]
    q3, k3, v3 = (a.reshape(n_batch, seq, hd) for a in (q, k, v))
    if caches is None:
        tiles = _toeplitz_tiles(dist_tab * LOG2E, seq // ATT_TILE, ATT_SUBTILES - 1, ATT_TILE)
        o = attn_prompt(q3, k3, v3, tiles, n_batch, seq)
    else:
        cache_k, cache_v = caches
        W = cache_k.shape[1]
        ck = cache_k.reshape(n_batch, W, hd)
        cv = cache_v.reshape(n_batch, W, hd)
        NPAD = 128
        d_old = W + np.arange(seq)[:, None] - np.arange(W)[None, :]
        b_old = jnp.transpose(jnp.take(dist_tab, jnp.asarray(d_old), axis=0), (2, 0, 1))
        d_new = np.arange(seq)[:, None] - np.arange(NPAD)[None, :]
        ok_new = (d_new >= 0) & (np.arange(NPAD)[None, :] < seq)
        b_new = jnp.take(dist_tab, jnp.asarray(np.maximum(d_new, 0)), axis=0)
        b_new = jnp.transpose(jnp.where(jnp.asarray(ok_new)[..., None], b_new, NEG_BIG), (2, 0, 1))
        o = attn_sample(q3, k3, v3, ck, cv, b_old, b_new)
    x = proj_residual(x, yc, o.reshape(n_batch * seq, hd), ow["w_out_c"], ow["w_out_d"])
    e = hd // H_D
    return x, conv_last, h_last, k3.reshape(n_batch, seq, H_D, e), v3.reshape(n_batch, seq, H_D, e)


def _moe_weights(l, router_group_w, router_group_b, router_expert_w, router_expert_b, exp_w_gate, exp_w_up,
                 exp_w_down):
    d = router_group_w.shape[1]
    n_used = N_GROUPS + router_expert_w.shape[2]
    rw = jnp.concatenate([router_group_w[l], router_expert_w[l], jnp.zeros((d, ROUTER_LANES - n_used), F32)], axis=1)
    rb = jnp.concatenate([router_group_b[l], router_expert_b[l], jnp.zeros((ROUTER_LANES - n_used,), F32)])
    return dict(rw=rw, rb=rb.reshape(1, -1), wg=exp_w_gate[l].astype(BF16), wu=exp_w_up[l].astype(BF16),
                wd=exp_w_down[l].astype(BF16))


def kernel(x_prompt, x_sample, state_wkv, state_shift, state_conv, state_rglru, cache_k, cache_v, norm_mix, norm_ffn, norm_final, w_in_even, w_out_even, shift_mu, decay_w0, decay_up, iclr_a0, iclr_up, gate_up, key_k, key_a, bonus_r_k, lnx_g, lnx_b, sgu_norm_g, sgu_norm_b, sgu_w, sgu_b, w_in_odd, w_out_odd, conv_w, conv_b, rgate_w, rgate_b, igate_w, igate_b, lru_lambda, rel_bias, router_group_w, router_group_b, router_expert_w, router_expert_b, exp_w_gate, exp_w_up, exp_w_down):
    B, L, D = x_prompt.shape
    DB, S, _ = x_sample.shape
    depth = norm_mix.shape[0]
    xp = x_prompt.reshape(B * L, D)
    xs = x_sample.reshape(DB * S, D)
    W = cache_k.shape[2]
    dist_tab = _dist_table(rel_bias, max(L, W + S) - 1)

    wkv_p, shift_p, conv_p, lru_p, k_p, v_p = [], [], [], [], [], []
    wkv_s, shift_s, chunkv_s, conv_s, lru_s, k_s, v_s = [], [], [], [], [], [], []
    for l in range(depth):
        j = l // 2
        if l % 2 == 0:
            ew = _even_weights(j, w_in_even, w_out_even, shift_mu, decay_w0, decay_up, iclr_a0, iclr_up, gate_up,
                               key_k, key_a, bonus_r_k, lnx_g, lnx_b, sgu_norm_g, sgu_norm_b, sgu_w, sgu_b)
            a_proj = ew["mu"].shape[1]
            h_a = state_wkv.shape[2]
            xp, sh, wkv, _ = _even_layer(xp, B, L, GMLP_TILE, RWKV_PAR_PROMPT, jnp.zeros((B, a_proj), F32),
                                         jnp.zeros((B, h_a, DH_A, DH_A), F32), norm_mix[l], ew)
            xs, sh_s, wkv_s_new, vn_s = _even_layer(xs, DB, S, S, RWKV_PAR_SAMPLE, state_shift[j], state_wkv[j], norm_mix[l], ew)
            wkv_p.append(wkv)
            shift_p.append(sh)
            wkv_s.append(wkv_s_new)
            shift_s.append(sh_s)
            chunkv_s.append(vn_s.reshape(DB, S, -1))
        else:
            ow = _odd_weights(j, w_in_odd, w_out_odd, conv_w, conv_b, rgate_w, rgate_b, igate_w, igate_b, lru_lambda)
            dc = ow["lam"].shape[1]
            xp, cv, hl, kr, vr = _odd_layer(xp, B, L, jnp.zeros((B, CONV_W - 1, dc), F32), jnp.zeros((B, dc), F32),
                                            0, None, dist_tab, norm_mix[l], ow)
            xs, cv_s, hl_s, kr_s, vr_s = _odd_layer(xs, DB, S, state_conv[j], state_rglru[j], PAST_LEN,
                                                    (cache_k[j], cache_v[j]), dist_tab, norm_mix[l], ow)
            conv_p.append(cv)
            lru_p.append(hl)
            k_p.append(kr)
            v_p.append(vr)
            conv_s.append(cv_s)
            lru_s.append(hl_s)
            k_s.append(kr_s)
            v_s.append(vr_s)
        mw = _moe_weights(l, router_group_w, router_group_b, router_expert_w, router_expert_b, exp_w_gate, exp_w_up,
                          exp_w_down)
        xp = moe_layer_sparse(xp, norm_ffn[l], mw["rw"], mw["rb"], mw["wg"], mw["wu"], mw["wd"],
                              final_g=norm_final if l == depth - 1 else None)
        xs = moe_layer(xs, norm_ffn[l], mw["rw"], mw["rb"], mw["wg"], mw["wu"], mw["wd"])
    y_prompt = xp.reshape(B, L, D)
    y_sample = rmsnorm_call(xs, norm_final).reshape(DB, S, D)
    return (y_prompt, y_sample,
            jnp.stack(wkv_p), jnp.stack(shift_p), jnp.stack(conv_p), jnp.stack(lru_p), jnp.stack(k_p), jnp.stack(v_p),
            jnp.stack(wkv_s), jnp.stack(shift_s), jnp.stack(chunkv_s), jnp.stack(conv_s), jnp.stack(lru_s),
            jnp.stack(k_s), jnp.stack(v_s))
```

```python
import functools
import math

import numpy as np
import jax
import jax.numpy as jnp
from jax import lax
from jax.experimental import pallas as pl
from jax.experimental.pallas import tpu as pltpu

F32 = jnp.float32
BF16 = jnp.bfloat16
HI = lax.Precision.HIGHEST

PAST_LEN = 8192
DH_A = 64
R_DECAY = 64
R_ICLR = 64
R_GATE = 128
GN_EPS = 64e-5
H_B = 4
H_C = 8
CONV_W = 4
LRU_C = 8.0
H_D = 8
DILATED = ((128, 1), (512, 4), (2048, 16))
N_BUCKETS = 32
BUCKET_MAX_DIST = 2048
NEG_BIG = -1e30
N_GROUPS = 4
EXP_PER_GROUP = 4
NORM_EPS = 1e-6
LOG2E = math.log2(math.e)

VMEM_LIMIT = 56 * 1024 * 1024
RWKV_CHUNK = 64
RWKV_PAR_PROMPT = 4
RWKV_PAR_SAMPLE = 8
ATT_TILE = 128
ATT_SUBTILES = 4
LANES = 128
SUBLANES = 8


def _cparams(sem):
    return pltpu.CompilerParams(dimension_semantics=sem, vmem_limit_bytes=VMEM_LIMIT)


def _dot(a, b, precision=None):
    return jnp.dot(a, b, preferred_element_type=F32, precision=precision)


def _dot_nt(a, b, precision=None):
    return lax.dot_general(a, b, (((1,), (1,)), ((), ())), preferred_element_type=F32, precision=precision)


def _dot_tn(a, b, precision=None):
    return lax.dot_general(a, b, (((0,), (0,)), ((), ())), preferred_element_type=F32, precision=precision)


def _split_bf16(x, n):
    parts = []
    for _ in range(n):
        hi = x.astype(BF16)
        parts.append(hi)
        x = x - hi.astype(F32)
    return parts


def _mp_dot(dotfn, a, b, passes):
    if passes == 1:
        return dotfn(a.astype(BF16), b.astype(BF16))
    a_hi, a_lo = _split_bf16(a, 2)
    b_hi, b_lo = _split_bf16(b, 2)
    return dotfn(a_hi, b_hi) + (dotfn(a_hi, b_lo) + dotfn(a_lo, b_hi))


def _dot_exact_rhs(a, b_bf16, n_split):
    parts = _split_bf16(a, n_split)
    acc = _dot(parts[0], b_bf16)
    for part in parts[1:]:
        acc = acc + _dot(part, b_bf16)
    return acc


def _dot_exact_lhs(a_bf16, b, n_split):
    parts = _split_bf16(b, n_split)
    acc = _dot(a_bf16, parts[0])
    for part in parts[1:]:
        acc = acc + _dot(a_bf16, part)
    return acc


def _softplus(x):
    return jnp.maximum(x, 0.0) + jnp.log(1.0 + jnp.exp(-jnp.abs(x)))


def _sigmoid(x):
    return 1.0 / (1.0 + jnp.exp(-x))


def _gelu(x):
    c = math.sqrt(2.0 / math.pi)
    return 0.5 * x * (1.0 + jnp.tanh(c * (x + 0.044715 * (x * x * x))))


def _row_tile(t, pref=512):
    return pref if t % pref == 0 else t


def _norm_matmul_kernel(x_ref, g_ref, w_ref, *out_refs, splits):
    x = x_ref[...]
    ms = jnp.mean(x * x, axis=-1, keepdims=True)
    h = (x * lax.rsqrt(ms + NORM_EPS) * g_ref[...]).astype(BF16)
    off = 0
    for o_ref, n in zip(out_refs, splits):
        o_ref[...] = _dot(h, w_ref[:, off:off + n])
        off += n


def norm_matmul(x, g, w_bf16, splits):
    t, d = x.shape
    n = w_bf16.shape[1]
    tm = _row_tile(t)
    return pl.pallas_call(
        functools.partial(_norm_matmul_kernel, splits=splits),
        out_shape=[jax.ShapeDtypeStruct((t, s), F32) for s in splits],
        grid=(t // tm,),
        in_specs=[pl.BlockSpec((tm, d), lambda i: (i, 0)),
                  pl.BlockSpec((1, d), lambda i: (0, 0)),
                  pl.BlockSpec((d, n), lambda i: (0, 0))],
        out_specs=[pl.BlockSpec((tm, s), lambda i: (i, 0)) for s in splits],
        compiler_params=_cparams(("parallel",)),
        name="norm_matmul",
    )(x, g.reshape(1, d), w_bf16)


def _proj_res_kernel(x_ref, a_ref, b_ref, wa_ref, wb_ref, o_ref):
    acc = _dot(a_ref[...].astype(BF16), wa_ref[...]) + _dot(b_ref[...].astype(BF16), wb_ref[...])
    o_ref[...] = x_ref[...] + acc


def proj_residual(x, a, b, wa, wb):
    t, d = x.shape
    tm = _row_tile(t)
    ka, kb = a.shape[1], b.shape[1]
    return pl.pallas_call(
        _proj_res_kernel,
        out_shape=jax.ShapeDtypeStruct((t, d), F32),
        grid=(t // tm,),
        in_specs=[pl.BlockSpec((tm, d), lambda i: (i, 0)),
                  pl.BlockSpec((tm, ka), lambda i: (i, 0)),
                  pl.BlockSpec((tm, kb), lambda i: (i, 0)),
                  pl.BlockSpec((ka, d), lambda i: (0, 0)),
                  pl.BlockSpec((kb, d), lambda i: (0, 0))],
        out_specs=pl.BlockSpec((tm, d), lambda i: (i, 0)),
        compiler_params=_cparams(("parallel",)),
        name="proj_residual",
    )(x, a, b, wa, wb)


def _rmsnorm_kernel(x_ref, g_ref, o_ref):
    x = x_ref[...]
    ms = jnp.mean(x * x, axis=-1, keepdims=True)
    o_ref[...] = x * lax.rsqrt(ms + NORM_EPS) * g_ref[...]


def rmsnorm_call(x, g):
    t, d = x.shape
    tm = _row_tile(t)
    return pl.pallas_call(
        _rmsnorm_kernel,
        out_shape=jax.ShapeDtypeStruct((t, d), F32),
        grid=(t // tm,),
        in_specs=[pl.BlockSpec((tm, d), lambda i: (i, 0)), pl.BlockSpec((1, d), lambda i: (0, 0))],
        out_specs=pl.BlockSpec((tm, d), lambda i: (i, 0)),
        compiler_params=_cparams(("parallel",)),
        name="final_rmsnorm",
    )(x, g.reshape(1, d))


ROUTER_LANES = 128


def _route(xn, rw, rb, lane, n_exp):
    logits = _dot(xn, rw, HI) + rb
    lg = jnp.where(lane < N_GROUPS, logits, -jnp.inf)
    gm = jnp.max(lg, axis=-1, keepdims=True)
    top_pg = 1.0 / jnp.sum(jnp.exp(lg - gm), axis=-1, keepdims=True)
    grp = jnp.min(jnp.where(lg == gm, lane, ROUTER_LANES), axis=-1, keepdims=True)
    in_grp = (lane >= N_GROUPS) & (lane < N_GROUPS + n_exp) & (((lane - N_GROUPS) // EXP_PER_GROUP) == grp)
    le = jnp.where(in_grp, logits, -jnp.inf)
    t1 = jnp.max(le, axis=-1, keepdims=True)
    i1 = jnp.min(jnp.where(le == t1, lane, ROUTER_LANES), axis=-1, keepdims=True)
    le2 = jnp.where(lane == i1, -jnp.inf, le)
    t2 = jnp.max(le2, axis=-1, keepdims=True)
    i2 = jnp.min(jnp.where(le2 == t2, lane, ROUTER_LANES), axis=-1, keepdims=True)
    ex = jnp.exp(t2 - t1)
    w1 = 1.0 / (1.0 + ex)
    return i1, i2, w1 * top_pg, (ex * w1) * top_pg


def _moe_kernel(x_ref, g_ref, rw_ref, rb_ref, wg_ref, wu_ref, wd_ref, o_ref, xn_scr, gate_scr, acc_scr, *, n_exp):
    e = pl.program_id(1)
    tm = x_ref.shape[0]
    lane = lax.broadcasted_iota(jnp.int32, (tm, ROUTER_LANES), 1)

    @pl.when(e == 0)
    def _():
        x = x_ref[...]
        ms = jnp.mean(x * x, axis=-1, keepdims=True)
        xn = x * lax.rsqrt(ms + NORM_EPS) * g_ref[...]
        xn_scr[...] = xn.astype(BF16)
        i1, i2, g1, g2 = _route(xn, rw_ref[...], rb_ref[...], lane, n_exp)
        gate_scr[...] = jnp.where(lane == i1, g1, 0.0) + jnp.where(lane == i2, g2, 0.0)
        acc_scr[...] = jnp.zeros_like(acc_scr)

    xn = xn_scr[...]
    hg = _dot(xn, wg_ref[0])
    hu = _dot(xn, wu_ref[0])
    gcol = jnp.sum(jnp.where(lane == e + N_GROUPS, gate_scr[...], 0.0), axis=-1, keepdims=True)
    hid = hg * _sigmoid(hg) * hu * gcol
    acc_scr[...] += _dot(hid.astype(BF16), wd_ref[0])

    @pl.when(e == n_exp - 1)
    def _():
        o_ref[...] = x_ref[...] + acc_scr[...]


def moe_layer(x, g, rw, rb, wg, wu, wd):
    t, d = x.shape
    n_exp, _, f = wg.shape
    tm = _row_tile(t)
    return pl.pallas_call(
        functools.partial(_moe_kernel, n_exp=n_exp),
        out_shape=jax.ShapeDtypeStruct((t, d), F32),
        grid=(t // tm, n_exp),
        in_specs=[pl.BlockSpec((tm, d), lambda i, e: (i, 0)),
                  pl.BlockSpec((1, d), lambda i, e: (0, 0)),
                  pl.BlockSpec((d, ROUTER_LANES), lambda i, e: (0, 0)),
                  pl.BlockSpec((1, ROUTER_LANES), lambda i, e: (0, 0)),
                  pl.BlockSpec((1, d, f), lambda i, e: (e, 0, 0)),
                  pl.BlockSpec((1, d, f), lambda i, e: (e, 0, 0)),
                  pl.BlockSpec((1, f, d), lambda i, e: (e, 0, 0))],
        out_specs=pl.BlockSpec((tm, d), lambda i, e: (i, 0)),
        scratch_shapes=[pltpu.VMEM((tm, d), BF16), pltpu.VMEM((tm, ROUTER_LANES), F32), pltpu.VMEM((tm, d), F32)],
        compiler_params=_cparams(("parallel", "arbitrary")),
        name="hier_moe",
    )(x, g.reshape(1, d), rw, rb, wg, wu, wd)


MOE_ROW_TILE = 512
MOE_COPY_CHUNK = 256
MOE_COMBINE_TILE = 256


def _router_kernel(x_ref, g_ref, rw_ref, rb_ref, tri_ref, gate_ref, info_ref, cnt_ref, base_scr, *, n_exp, n_tiles):
    i = pl.program_id(0)
    tm = x_ref.shape[0]
    lane = lax.broadcasted_iota(jnp.int32, (tm, ROUTER_LANES), 1)

    @pl.when(i == 0)
    def _():
        base_scr[...] = jnp.zeros_like(base_scr)

    x = x_ref[...]
    ms = jnp.mean(x * x, axis=-1, keepdims=True)
    xn = x * lax.rsqrt(ms + NORM_EPS) * g_ref[...]
    i1, i2, g1, g2 = _route(xn, rw_ref[...], rb_ref[...], lane, n_exp)
    chosen = jnp.where((lane == i1) | (lane == i2), 1.0, 0.0)
    before = _dot(tri_ref[...], chosen.astype(BF16)) + base_scr[...]
    r1 = jnp.sum(jnp.where(lane == i1, before, 0.0), axis=-1, keepdims=True)
    r2 = jnp.sum(jnp.where(lane == i2, before, 0.0), axis=-1, keepdims=True)
    base_scr[...] += jnp.sum(chosen, axis=0, keepdims=True)
    gate_ref[...] = jnp.where(lane == 0, g1, 0.0) + jnp.where(lane == 1, g2, 0.0)
    e1 = (i1 - N_GROUPS).astype(F32)
    e2 = (i2 - N_GROUPS).astype(F32)
    info_ref[...] = (jnp.where(lane == 0, e1, 0.0) + jnp.where(lane == 1, e2, 0.0)
                     + jnp.where(lane == 2, r1, 0.0) + jnp.where(lane == 3, r2, 0.0))

    @pl.when(i == n_tiles - 1)
    def _():
        cnt_ref[...] = base_scr[...]


def _scatter_rows_kernel(pos0_ref, pos1_ref, x_ref, xs_in_hbm, xs_hbm, stage, sem, *, CH, n_chunks):
    del xs_in_hbm
    c = pl.program_id(0)
    slot = c % 2
    x = x_ref[...]
    for j in range(SUBLANES):
        stage[slot, pl.ds(j, CH, stride=SUBLANES), :] = x[:, j * LANES:(j + 1) * LANES]

    def body(r, carry):
        t = c * CH + r
        src = stage.at[slot, pl.ds(pl.multiple_of(r * SUBLANES, SUBLANES), SUBLANES), :]
        d0 = pl.multiple_of(pos0_ref[t], SUBLANES)
        d1 = pl.multiple_of(pos1_ref[t], SUBLANES)
        pltpu.make_async_copy(src, xs_hbm.at[pl.ds(d0, SUBLANES), :], sem.at[slot]).start(priority=0)
        pltpu.make_async_copy(src, xs_hbm.at[pl.ds(d1, SUBLANES), :], sem.at[slot]).start(priority=1)
        return carry

    lax.fori_loop(0, CH, body, 0, unroll=8)

    def drain(s):
        pltpu.make_async_copy(stage.at[s], xs_hbm.at[pl.ds(0, CH * SUBLANES), :], sem.at[s]).wait()
        pltpu.make_async_copy(stage.at[s], xs_hbm.at[pl.ds(0, CH * SUBLANES), :], sem.at[s]).wait()

    @pl.when(c > 0)
    def _():
        drain(1 - slot)

    @pl.when(c == n_chunks - 1)
    def _():
        drain(slot)


def _tile_rows_to_matrix(ref, lead, n_rows):
    return jnp.concatenate([ref[lead + (pl.ds(j, n_rows, stride=SUBLANES), slice(None))] for j in range(SUBLANES)],
                           axis=-1)


def _expert_kernel(te_ref, nv_ref, xs_ref, g_ref, wg_ref, wu_ref, wd_ref, y_ref, *, TM):
    @pl.when(pl.program_id(0) < nv_ref[0])
    def _():
        x = _tile_rows_to_matrix(xs_ref, (), TM)
        ms = jnp.mean(x * x, axis=-1, keepdims=True)
        xn = (x * lax.rsqrt(ms + NORM_EPS) * g_ref[...]).astype(BF16)
        hg = _dot(xn, wg_ref[0])
        hu = _dot(xn, wu_ref[0])
        hid = hg * _sigmoid(hg) * hu
        y = _dot(hid.astype(BF16), wd_ref[0])
        for j in range(SUBLANES):
            y_ref[pl.ds(j, TM, stride=SUBLANES), :] = y[:, j * LANES:(j + 1) * LANES]

    @pl.when(pl.program_id(0) >= nv_ref[0])
    def _():
        y_ref[...] = jnp.zeros_like(y_ref)


def _combine_kernel(pos0_ref, pos1_ref, x_ref, gate_ref, fg_ref, y_hbm, o_ref, ybuf, sem, *, TC, n_tiles, final_norm):
    i = pl.program_id(0)

    def issue(tile, slot):
        def body(r, carry):
            t = tile * TC + r
            dst = pl.ds(pl.multiple_of(r * SUBLANES, SUBLANES), SUBLANES)
            s0 = pl.multiple_of(pos0_ref[t], SUBLANES)
            s1 = pl.multiple_of(pos1_ref[t], SUBLANES)
            pltpu.make_async_copy(y_hbm.at[pl.ds(s0, SUBLANES), :], ybuf.at[slot, 0, dst, :],
                                  sem.at[slot]).start(priority=0)
            pltpu.make_async_copy(y_hbm.at[pl.ds(s1, SUBLANES), :], ybuf.at[slot, 1, dst, :],
                                  sem.at[slot]).start(priority=1)
            return carry
        lax.fori_loop(0, TC, body, 0, unroll=8)

    @pl.when(i == 0)
    def _():
        issue(0, 0)

    @pl.when(i + 1 < n_tiles)
    def _():
        issue(i + 1, (i + 1) % 2)

    slot = i % 2
    pltpu.make_async_copy(y_hbm.at[pl.ds(0, TC * SUBLANES), :], ybuf.at[slot, 0], sem.at[slot]).wait()
    pltpu.make_async_copy(y_hbm.at[pl.ds(0, TC * SUBLANES), :], ybuf.at[slot, 1], sem.at[slot]).wait()
    gate = gate_ref[...]
    y0 = _tile_rows_to_matrix(ybuf, (slot, 0), TC)
    y1 = _tile_rows_to_matrix(ybuf, (slot, 1), TC)
    out = x_ref[...] + gate[:, 0:1] * y0 + gate[:, 1:2] * y1
    if final_norm:
        ms = jnp.mean(out * out, axis=-1, keepdims=True)
        out = out * lax.rsqrt(ms + NORM_EPS) * fg_ref[...]
    o_ref[...] = out


def moe_layer_sparse(x, g, rw, rb, wg, wu, wd, final_g=None):
    t, d = x.shape
    n_exp, _, f = wg.shape
    TM = MOE_ROW_TILE
    n_tiles = t // TM
    tri = jnp.asarray(np.tril(np.ones((TM, TM), np.float32), -1)).astype(BF16)
    gate, info, cnt = pl.pallas_call(
        functools.partial(_router_kernel, n_exp=n_exp, n_tiles=n_tiles),
        out_shape=[jax.ShapeDtypeStruct((t, ROUTER_LANES), F32), jax.ShapeDtypeStruct((t, ROUTER_LANES), F32),
                   jax.ShapeDtypeStruct((1, ROUTER_LANES), F32)],
        grid=(n_tiles,),
        in_specs=[pl.BlockSpec((TM, d), lambda i: (i, 0)),
                  pl.BlockSpec((1, d), lambda i: (0, 0)),
                  pl.BlockSpec((d, ROUTER_LANES), lambda i: (0, 0)),
                  pl.BlockSpec((1, ROUTER_LANES), lambda i: (0, 0)),
                  pl.BlockSpec((TM, TM), lambda i: (0, 0))],
        out_specs=[pl.BlockSpec((TM, ROUTER_LANES), lambda i: (i, 0)),
                   pl.BlockSpec((TM, ROUTER_LANES), lambda i: (i, 0)),
                   pl.BlockSpec((1, ROUTER_LANES), lambda i: (0, 0))],
        scratch_shapes=[pltpu.VMEM((1, ROUTER_LANES), F32)],
        compiler_params=_cparams(("arbitrary",)),
        name="moe_router",
    )(x, g.reshape(1, d), rw, rb, tri)

    counts = cnt[0, N_GROUPS:N_GROUPS + n_exp].astype(jnp.int32)
    padded = ((counts + TM - 1) // TM) * TM
    ends = jnp.cumsum(padded)
    offs = ends - padded
    eid = info[:, 0:2].astype(jnp.int32)
    rank = info[:, 2:4].astype(jnp.int32)
    pos = jnp.sum(jnp.where(eid[:, :, None] == jnp.arange(n_exp)[None, None, :], offs[None, None, :], 0), axis=-1) + rank
    assert d == SUBLANES * LANES, "a token row must fill exactly one (8, 128) tile"
    pos = pos * SUBLANES
    pos0, pos1 = pos[:, 0], pos[:, 1]
    max_tiles = (2 * t) // TM + n_exp
    n_valid = (ends[-1] // TM).astype(jnp.int32).reshape(1)
    tile_exp = jnp.minimum(jnp.sum((ends[None, :] // TM) <= jnp.arange(max_tiles)[:, None], axis=-1),
                           n_exp - 1).astype(jnp.int32)
    p_rows = max_tiles * TM

    CH = MOE_COPY_CHUNK
    xs = pl.pallas_call(
        functools.partial(_scatter_rows_kernel, CH=CH, n_chunks=t // CH),
        out_shape=jax.ShapeDtypeStruct((p_rows * SUBLANES, LANES), F32),
        grid_spec=pltpu.PrefetchScalarGridSpec(
            num_scalar_prefetch=2, grid=(t // CH,),
            in_specs=[pl.BlockSpec((CH, d), lambda c, p0, p1: (c, 0)), pl.BlockSpec(memory_space=pl.ANY)],
            out_specs=pl.BlockSpec(memory_space=pl.ANY),
            scratch_shapes=[pltpu.VMEM((2, CH * SUBLANES, LANES), F32), pltpu.SemaphoreType.DMA((2,))]),
        input_output_aliases={3: 0},
        compiler_params=pltpu.CompilerParams(dimension_semantics=("arbitrary",), vmem_limit_bytes=VMEM_LIMIT,
                                             has_side_effects=True),
        name="moe_scatter_rows",
    )(pos0, pos1, x, jnp.zeros((p_rows * SUBLANES, LANES), F32))

    def row_idx(i, te, nv):
        return (jnp.minimum(i, nv[0] - 1), 0)

    ys = pl.pallas_call(
        functools.partial(_expert_kernel, TM=TM),
        out_shape=jax.ShapeDtypeStruct((p_rows * SUBLANES, LANES), F32),
        grid_spec=pltpu.PrefetchScalarGridSpec(
            num_scalar_prefetch=2, grid=(max_tiles,),
            in_specs=[pl.BlockSpec((TM * SUBLANES, LANES), row_idx),
                      pl.BlockSpec((1, d), lambda i, te, nv: (0, 0)),
                      pl.BlockSpec((1, d, f), lambda i, te, nv: (te[i], 0, 0)),
                      pl.BlockSpec((1, d, f), lambda i, te, nv: (te[i], 0, 0)),
                      pl.BlockSpec((1, f, d), lambda i, te, nv: (te[i], 0, 0))],
            out_specs=pl.BlockSpec((TM * SUBLANES, LANES), lambda i, te, nv: (i, 0))),
        compiler_params=_cparams(("arbitrary",)),
        name="moe_experts",
    )(tile_exp, n_valid, xs, g.reshape(1, d), wg, wu, wd)

    TC = MOE_COMBINE_TILE
    return pl.pallas_call(
        functools.partial(_combine_kernel, TC=TC, n_tiles=t // TC, final_norm=final_g is not None),
        out_shape=jax.ShapeDtypeStruct((t, d), F32),
        grid_spec=pltpu.PrefetchScalarGridSpec(
            num_scalar_prefetch=2, grid=(t // TC,),
            in_specs=[pl.BlockSpec((TC, d), lambda i, p0, p1: (i, 0)),
                      pl.BlockSpec((TC, ROUTER_LANES), lambda i, p0, p1: (i, 0)),
                      pl.BlockSpec((1, d), lambda i, p0, p1: (0, 0)),
                      pl.BlockSpec(memory_space=pl.ANY)],
            out_specs=pl.BlockSpec((TC, d), lambda i, p0, p1: (i, 0)),
            scratch_shapes=[pltpu.VMEM((2, 2, TC * SUBLANES, LANES), F32), pltpu.SemaphoreType.DMA((2,))]),
        compiler_params=_cparams(("arbitrary",)),
        name="moe_combine",
    )(pos0, pos1, x, gate, (g if final_g is None else final_g).reshape(1, d), ys)


def _rwkv_kernel(p_ref, prev_ref, s0_ref, mu_ref, w0_ref, wd_ref, a0_ref, wa_ref, wg_ref, kk_ref, ka_ref,
                 bonus_ref, lng_ref, lnb_ref, tri_ref, hsum_ref, ya_ref, sf_ref, s_scr, prev_scr,
                 *, NB, C, H, DH, n_chunks):
    c = pl.program_id(1)

    @pl.when(c == 0)
    def _():
        s_scr[...] = s0_ref[:, 0]
        prev_scr[...] = prev_ref[:, 0]

    DA = H * DH
    R = NB * C
    p = p_ref[...].reshape(R, p_ref.shape[-1])
    row = lax.broadcasted_iota(jnp.int32, p.shape, 0)
    shifted = pltpu.roll(p, 1, axis=0)
    for n in range(NB):
        shifted = jnp.where(row == n * C, prev_scr[n], shifted)
        prev_scr[n] = p[(n + 1) * C - 1:(n + 1) * C, :]
    xs = p + (shifted - p) * mu_ref[...]
    r = xs[:, 0:DA]
    k = xs[:, DA:2 * DA]
    v = xs[:, 2 * DA:3 * DA]
    lora = xs[:, 3 * DA:3 * DA + R_DECAY + R_ICLR]
    gd = xs[:, 3 * DA + R_DECAY + R_ICLR:3 * DA + R_DECAY + R_ICLR + R_GATE]

    w_log = -_softplus(-(w0_ref[...] + _mp_dot(_dot, jnp.tanh(lora), wd_ref[...], 3))) - 0.5
    lw = -jnp.exp(w_log)
    a = _sigmoid(a0_ref[...] + _mp_dot(_dot, lora, wa_ref[...], 3))
    g = _mp_dot(_dot, _sigmoid(gd), wg_ref[...], 3)

    kk = k * kk_ref[...]
    ss = _dot_exact_rhs(kk * kk, hsum_ref[...], 2)
    kk = kk / jnp.maximum(jnp.sqrt(ss), 1e-12)
    k2 = k * (1.0 + (a - 1.0) * ka_ref[...])
    kka = kk * a

    cum = jnp.concatenate([_dot_exact_lhs(tri_ref[...], lw[n * C:(n + 1) * C], 3) for n in range(NB)], axis=0)
    p_in = jnp.exp(cum)
    r_t = r * p_in
    a_t = kk * jnp.exp(cum - lw)
    p_inv = jnp.exp(-cum)
    b_t = kka * p_inv
    k_t = k2 * p_inv
    bonus = _dot_exact_rhs(r * k2 * bonus_ref[...], hsum_ref[...], 2) * v

    ri = lax.broadcasted_iota(jnp.int32, (C, C), 0)
    ci = lax.broadcasted_iota(jnp.int32, (C, C), 1)
    strict = ri > ci
    incl = ri >= ci
    eye = (ri == ci).astype(F32)
    n_double = max(int(math.ceil(math.log2(C))) - 1, 0)

    chains = [(n, h) for n in range(NB) for h in range(H)]

    def blk(x, n, h):
        return x[n * C:(n + 1) * C, h * DH:(h + 1) * DH]

    def bf(x):
        return x.astype(BF16)

    Bt = [bf(blk(b_t, n, h)) for n, h in chains]
    Kt = [bf(blk(k_t, n, h)) for n, h in chains]
    Vf = [blk(v, n, h) for n, h in chains]
    AR = [bf(jnp.concatenate([blk(a_t, n, h), blk(r_t, n, h)], axis=0)) for n, h in chains]
    S0 = [s_scr[n, h] for n, h in chains]
    idx = range(len(chains))
    GB = [_dot_nt(AR[i], Bt[i]) for i in idx]
    GK = [_dot_nt(AR[i], Kt[i]) for i in idx]
    ARS = [_dot_nt(AR[i], bf(S0[i])) for i in idx]
    Lm = [jnp.where(strict, GB[i][0:C], 0.0) for i in idx]
    Gb = [bf(jnp.where(incl, GB[i][C:2 * C], 0.0)) for i in idx]
    MG = [bf(jnp.concatenate([jnp.where(strict, GK[i][0:C], 0.0), jnp.where(incl, GK[i][C:2 * C], 0.0)], axis=0))
          for i in idx]
    MGV = [_dot(MG[i], bf(Vf[i])) for i in idx]
    T = [eye - Lm[i] for i in idx]
    Pw = [bf(Lm[i]) for i in idx]
    for _ in range(n_double):
        Pw = [bf(_dot(Pw[i], Pw[i])) for i in idx]
        T = [T[i] + _dot(bf(T[i]), Pw[i]) for i in idx]
    U = [_dot(bf(T[i]), bf(-(ARS[i][0:C] + MGV[i][0:C]))) for i in idx]
    Y = [ARS[i][C:2 * C] + _dot(Gb[i], bf(U[i])) + MGV[i][C:2 * C] for i in idx]
    for i, (n, h) in enumerate(chains):
        UV = bf(jnp.concatenate([U[i], Vf[i]], axis=0))
        BK = jnp.concatenate([Bt[i], Kt[i]], axis=0)
        p_tot = p_in[(n + 1) * C - 1:(n + 1) * C, h * DH:(h + 1) * DH]
        s_scr[n, h] = (S0[i] + _dot_tn(UV, BK)) * p_tot

    rows = []
    for n in range(NB):
        ys = []
        for h in range(H):
            Yh = Y[n * H + h]
            yc = Yh - jnp.mean(Yh, axis=-1, keepdims=True)
            var = jnp.mean(yc * yc, axis=-1, keepdims=True)
            ys.append(yc * lax.rsqrt(var + GN_EPS))
        rows.append(jnp.concatenate(ys, axis=-1))
    y = jnp.concatenate(rows, axis=0) * lng_ref[...] + lnb_ref[...]
    ya_ref[...] = ((y + bonus) * g).reshape(NB, C, DA)

    @pl.when(c == n_chunks - 1)
    def _():
        sf_ref[:, 0] = s_scr[...]


def rwkv_mix(pa, n_batch, seq, shift_prev, wkv0, wts, n_par):
    t, ap = pa.shape
    H = wkv0.shape[1]
    DA = H * DH_A
    C = min(RWKV_CHUNK, seq)
    n_chunks = seq // C
    NB = n_par
    G = n_batch // NB
    tri = jnp.asarray(np.tril(np.ones((C, C), np.float32))).astype(BF16)
    hsum = jnp.asarray(np.kron(np.eye(H, dtype=np.float32), np.ones((DH_A, DH_A), np.float32))).astype(BF16)

    def full(shape):
        nd = len(shape)
        return pl.BlockSpec(shape, lambda b, c: (0,) * nd)

    vec = full((1, DA))
    ya, s_fin = pl.pallas_call(
        functools.partial(_rwkv_kernel, NB=NB, C=C, H=H, DH=DH_A, n_chunks=n_chunks),
        out_shape=[jax.ShapeDtypeStruct((NB, t // NB, DA), F32),
                   jax.ShapeDtypeStruct((NB, G, H, DH_A, DH_A), F32)],
        grid=(G, n_chunks),
        in_specs=[pl.BlockSpec((NB, C, ap), lambda b, c: (0, b * n_chunks + c, 0)),
                  pl.BlockSpec((NB, 1, 1, ap), lambda b, c: (0, b, 0, 0)),
                  pl.BlockSpec((NB, 1, H, DH_A, DH_A), lambda b, c: (0, b, 0, 0, 0)),
                  full((1, ap)), vec, full((R_DECAY + R_ICLR, DA)), vec, full((R_DECAY + R_ICLR, DA)),
                  full((R_GATE, DA)), vec, vec, vec, vec, vec, full((C, C)), full((DA, DA))],
        out_specs=[pl.BlockSpec((NB, C, DA), lambda b, c: (0, b * n_chunks + c, 0)),
                   pl.BlockSpec((NB, 1, H, DH_A, DH_A), lambda b, c: (0, b, 0, 0, 0))],
        scratch_shapes=[pltpu.VMEM((NB, H, DH_A, DH_A), F32), pltpu.VMEM((NB, 1, ap), F32)],
        compiler_params=_cparams(("parallel", "arbitrary")),
        name="rwkv7_mix",
    )(pa.reshape(NB, t // NB, ap), shift_prev.reshape(NB, G, 1, ap), wkv0.reshape(NB, G, H, DH_A, DH_A),
      wts["mu"], wts["w0"], wts["wd"], wts["a0"], wts["wa"],
      wts["wg"], wts["key_k"], wts["key_a"], wts["bonus"], wts["lnx_g"], wts["lnx_b"], tri, hsum)
    return ya.reshape(t, DA), s_fin.reshape(wkv0.shape)


GMLP_TILE = 128


def _gmlp_kernel(u_ref, v_ref, ng_ref, nb_ref, wm_ref, bias_ref, o_ref, vn_ref):
    vf = _gelu(v_ref[...])
    mu = jnp.mean(vf, axis=-1, keepdims=True)
    vc = vf - mu
    var = jnp.mean(vc * vc, axis=-1, keepdims=True)
    vn = vc * lax.rsqrt(var + NORM_EPS) * ng_ref[...] + nb_ref[...]
    vn_ref[...] = vn
    vb = vn.astype(BF16)
    n_h = wm_ref.shape[0]
    cb = vn.shape[1] // n_h
    s = jnp.concatenate([_dot(wm_ref[h], vb[:, h * cb:(h + 1) * cb]) for h in range(n_h)], axis=-1)
    o_ref[...] = _gelu(u_ref[...]) * (s + bias_ref[...])


def gmlp_mix(pu, pv, ng, nb, wm_bf16, bias_tile):
    t, db = pu.shape
    n_h = wm_bf16.shape[0]
    return pl.pallas_call(
        _gmlp_kernel,
        out_shape=[jax.ShapeDtypeStruct((t, db), F32), jax.ShapeDtypeStruct((t, db), F32)],
        grid=(t // GMLP_TILE,),
        in_specs=[pl.BlockSpec((GMLP_TILE, db), lambda i: (i, 0)),
                  pl.BlockSpec((GMLP_TILE, db), lambda i: (i, 0)),
                  pl.BlockSpec((1, db), lambda i: (0, 0)),
                  pl.BlockSpec((1, db), lambda i: (0, 0)),
                  pl.BlockSpec((n_h, GMLP_TILE, GMLP_TILE), lambda i: (0, 0, 0)),
                  pl.BlockSpec((GMLP_TILE, db), lambda i: (0, 0))],
        out_specs=[pl.BlockSpec((GMLP_TILE, db), lambda i: (i, 0)),
                   pl.BlockSpec((GMLP_TILE, db), lambda i: (i, 0))],
        compiler_params=_cparams(("parallel",)),
        name="gmlp_mix",
    )(pu, pv, ng, nb, wm_bf16, bias_tile)


N_SEG = 8


def _rglru_kernel(xb_ref, gy_ref, cprev_ref, h0_ref, cw_ref, cb_ref, gw_ref, gb_ref, lam_ref,
                  yc_ref, ctail_ref, hl_ref, xe_scr, a_scr, b_scr, h_scr, *, TL, DC, pos0, n_tiles):
    l = pl.program_id(1)
    PAD = 8

    @pl.when(l == 0)
    def _():
        xe_scr[0:PAD, :] = cprev_ref[0]
        h_scr[...] = h0_ref[0]

    xe_scr[PAD:PAD + TL, :] = xb_ref[...]
    xc = cb_ref[...] + xe_scr[pl.ds(PAD - (CONV_W - 1), TL), :] * cw_ref[0:1, :]
    for i in range(1, CONV_W):
        xc = xc + xe_scr[pl.ds(PAD - (CONV_W - 1) + i, TL), :] * cw_ref[i:i + 1, :]
    tail = xe_scr[TL:TL + PAD, :]
    ctail_ref[0] = tail
    xe_scr[0:PAD, :] = tail

    gates = _dot(xc.astype(BF16), gw_ref[...]) + gb_ref[...]
    rg = _sigmoid(gates[:, 0:DC])
    ig = _sigmoid(gates[:, DC:2 * DC])
    log_a = -LRU_C * rg * _softplus(-lam_ref[...])
    a = jnp.exp(log_a)
    mult = jnp.sqrt(1.0 - jnp.exp(2.0 * log_a))
    row = lax.broadcasted_iota(jnp.int32, (TL, DC), 0)
    mult = jnp.where(row + (l * TL + pos0) == 0, 1.0, mult)
    b = mult * ig * xc
    n_slab = DC // LANES
    for s in range(n_slab):
        a_scr[s] = a[:, s * LANES:(s + 1) * LANES]
        b_scr[s] = b[:, s * LANES:(s + 1) * LANES]

    seg = TL // N_SEG

    def step(i, carry):
        idx = pl.ds(i, N_SEG, stride=seg) if seg > 1 else pl.ds(0, N_SEG)
        out = []
        for s in range(n_slab):
            hloc, ap = carry[s]
            ai = a_scr[s, idx, :]
            hloc = ai * hloc + b_scr[s, idx, :]
            ap = ap * ai
            b_scr[s, idx, :] = hloc
            a_scr[s, idx, :] = ap
            out.append((hloc, ap))
        return tuple(out)

    lax.fori_loop(0, seg, step,
                  tuple((jnp.zeros((N_SEG, LANES), F32), jnp.ones((N_SEG, LANES), F32)) for _ in range(n_slab)))

    carry = h_scr[...]
    g_act = _gelu(gy_ref[...])
    for j in range(N_SEG):
        rows = slice(j * seg, (j + 1) * seg)
        hloc = jnp.concatenate([b_scr[s, rows, :] for s in range(n_slab)], axis=-1)
        ap = jnp.concatenate([a_scr[s, rows, :] for s in range(n_slab)], axis=-1)
        hj = hloc + ap * carry
        yc_ref[rows, :] = g_act[rows, :] * hj
        carry = hj[seg - 1:seg, :]
    h_scr[...] = carry

    @pl.when(l == n_tiles - 1)
    def _():
        hl_ref[0] = carry


def rglru_mix(xb, gy, n_batch, seq, conv_prev8, h0, pos0, wts):
    t, dc = xb.shape
    TL = 512 if seq % 512 == 0 else seq
    n_tiles = seq // TL

    def full(shape):
        nd = len(shape)
        return pl.BlockSpec(shape, lambda b, l: (0,) * nd)

    yc, ctail, hl = pl.pallas_call(
        functools.partial(_rglru_kernel, TL=TL, DC=dc, pos0=pos0, n_tiles=n_tiles),
        out_shape=[jax.ShapeDtypeStruct((t, dc), F32), jax.ShapeDtypeStruct((n_batch, 8, dc), F32),
                   jax.ShapeDtypeStruct((n_batch, 1, dc), F32)],
        grid=(n_batch, n_tiles),
        in_specs=[pl.BlockSpec((TL, dc), lambda b, l: (b * n_tiles + l, 0)),
                  pl.BlockSpec((TL, dc), lambda b, l: (b * n_tiles + l, 0)),
                  pl.BlockSpec((1, 8, dc), lambda b, l: (b, 0, 0)),
                  pl.BlockSpec((1, 1, dc), lambda b, l: (b, 0, 0)),
                  full((CONV_W, dc)), full((1, dc)), full((dc, 2 * dc)), full((1, 2 * dc)), full((1, dc))],
        out_specs=[pl.BlockSpec((TL, dc), lambda b, l: (b * n_tiles + l, 0)),
                   pl.BlockSpec((1, 8, dc), lambda b, l: (b, 0, 0)),
                   pl.BlockSpec((1, 1, dc), lambda b, l: (b, 0, 0))],
        scratch_shapes=[pltpu.VMEM((TL + 8, dc), F32), pltpu.VMEM((dc // LANES, TL, LANES), F32),
                        pltpu.VMEM((dc // LANES, TL, LANES), F32), pltpu.VMEM((1, dc), F32)],
        compiler_params=_cparams(("parallel", "arbitrary")),
        name="rglru_mix",
    )(xb, gy, conv_prev8, h0.reshape(n_batch, 1, dc), wts["conv_w"], wts["conv_b"], wts["gate_w"], wts["gate_b"],
      wts["lam"])
    return yc, ctail[:, 8 - (CONV_W - 1):, :], hl.reshape(n_batch, dc)


def _t5_bucket(dist):
    dist = np.asarray(dist)
    max_exact = N_BUCKETS // 2
    scaled = np.log(np.maximum(dist, 1) / max_exact) / math.log(BUCKET_MAX_DIST / max_exact)
    large = np.minimum(max_exact + (scaled * (N_BUCKETS - max_exact)).astype(np.int32), N_BUCKETS - 1)
    return np.where(dist < max_exact, dist, large).astype(np.int32)


def _dist_table(rel_bias, max_dist):
    dist = np.arange(max_dist + 1)
    count = np.zeros(max_dist + 1, np.float32)
    for window, dil in DILATED:
        count += ((dist % dil == 0) & (dist <= window)).astype(np.float32)
    logcnt = np.where(count > 0, np.log(np.maximum(count, 1.0)), 0.0).astype(np.float32)
    tab = jnp.take(rel_bias, jnp.asarray(_t5_bucket(dist)), axis=0) + jnp.asarray(logcnt)[:, None]
    return jnp.where(jnp.asarray(count > 0)[:, None], tab, NEG_BIG)


def _toeplitz_tiles(tab, n_pos, n_neg, T):
    D, H = tab.shape
    span = T * n_pos
    assert D >= span
    n_col = span + T * n_neg + T - 1
    ext = jnp.concatenate([jnp.flip(tab[:span], axis=0), jnp.full((n_col + 1 - span, H), NEG_BIG, F32)], axis=0)
    ext = jnp.transpose(ext)
    skew = jnp.tile(ext, (1, T))[:, :T * n_col].reshape(H, T, n_col)
    tiles = [skew[:, :, span - 1 - T * dd: span - 1 - T * dd + T] for dd in range(-n_neg, n_pos)]
    return jnp.stack(tiles, axis=1)


def _attn_prompt_kernel(q_ref, k_ref, v_ref, bias_ref, o_ref, kb_scr, vb_scr, *, E, SUB, NS):
    qi = pl.program_id(2)
    TQ = NS * SUB

    @pl.when(qi == 0)
    def _():
        kb_scr[...] = k_ref[0].astype(BF16)
        vb_scr[...] = v_ref[0].astype(BF16)

    lane = lax.broadcasted_iota(jnp.int32, (SUB, 2 * E), 1)
    q2 = []
    for rs in range(NS):
        q = q_ref[0, rs * SUB:(rs + 1) * SUB, :] * (E ** -0.5 * LOG2E)
        q2.append(jnp.concatenate([jnp.where(lane < E, q, 0.0), jnp.where(lane >= E, q, 0.0)], axis=0).astype(BF16))

    def body(i, carry):
        j = qi - i
        kj = kb_scr[pl.ds(pl.multiple_of(j * TQ, TQ), TQ), :]
        vj = vb_scr[pl.ds(pl.multiple_of(j * TQ, TQ), TQ), :]
        out = []
        for rs in range(NS):
            m, l, acc = carry[rs]
            s = _dot_nt(q2[rs], kj)
            parts = []
            for cs in range(NS):
                dd = i * NS + (rs - cs + NS - 1)
                bias = jnp.concatenate([bias_ref[0, dd], bias_ref[1, dd]], axis=0)
                parts.append(s[:, cs * SUB:(cs + 1) * SUB] + bias)
            mx = parts[0]
            for part in parts[1:]:
                mx = jnp.maximum(mx, part)
            m_new = jnp.maximum(m, jnp.max(mx, axis=-1, keepdims=True))
            alpha = jnp.exp2(m - m_new)
            ps = [jnp.exp2(part - m_new) for part in parts]
            psum = ps[0]
            for pexp in ps[1:]:
                psum = psum + pexp
            l = alpha * l + psum
            acc = alpha * acc + _dot(jnp.concatenate(ps, axis=-1).astype(BF16), vj)
            out.append((m_new, l, acc))
        return tuple(out)

    init = tuple((jnp.full((2 * SUB, SUB), NEG_BIG, F32), jnp.zeros((2 * SUB, SUB), F32),
                  jnp.zeros((2 * SUB, 2 * E), F32)) for _ in range(NS))
    res = lax.fori_loop(0, qi + 1, body, init)
    for rs in range(NS):
        m, l, acc = res[rs]
        o = acc / jnp.sum(l, axis=-1, keepdims=True)
        o_ref[0, rs * SUB:(rs + 1) * SUB, :] = jnp.where(lane < E, o[0:SUB], o[SUB:2 * SUB])


def attn_prompt(q, k, v, bias_tiles, n_batch, seq):
    hd = q.shape[-1]
    E = hd // H_D
    SUB = ATT_TILE
    NS = ATT_SUBTILES
    TQ = SUB * NS
    nq = seq // TQ
    nt = bias_tiles.shape[1]
    return pl.pallas_call(
        functools.partial(_attn_prompt_kernel, E=E, SUB=SUB, NS=NS),
        out_shape=jax.ShapeDtypeStruct((n_batch, seq, hd), F32),
        grid=(H_D // 2, n_batch, nq),
        in_specs=[pl.BlockSpec((1, TQ, 2 * E), lambda hp, b, i: (b, i, hp)),
                  pl.BlockSpec((1, seq, 2 * E), lambda hp, b, i: (b, 0, hp)),
                  pl.BlockSpec((1, seq, 2 * E), lambda hp, b, i: (b, 0, hp)),
                  pl.BlockSpec((2, nt, SUB, SUB), lambda hp, b, i: (hp, 0, 0, 0))],
        out_specs=pl.BlockSpec((1, TQ, 2 * E), lambda hp, b, i: (b, i, hp)),
        scratch_shapes=[pltpu.VMEM((seq, 2 * E), BF16), pltpu.VMEM((seq, 2 * E), BF16)],
        compiler_params=_cparams(("arbitrary", "arbitrary", "arbitrary")),
        name="dilated_attn_prompt",
    )(q, k, v, bias_tiles)


def _attn_sample_kernel(q_ref, kn_ref, vn_ref, ck_ref, cv_ref, bo_ref, bn_ref, o_ref, *, E, S):
    lane = lax.broadcasted_iota(jnp.int32, (S, 2 * E), 1)
    NPAD = bn_ref.shape[-1]
    outs = []
    for hp in range(H_D // 2):
        sl = slice(hp * 2 * E, (hp + 1) * 2 * E)
        q = q_ref[0, :, sl] * (E ** -0.5)
        q2 = jnp.concatenate([jnp.where(lane < E, q, 0.0), jnp.where(lane >= E, q, 0.0)], axis=0).astype(BF16)
        zpad = jnp.zeros((NPAD - S, 2 * E), F32)
        kn = jnp.concatenate([kn_ref[0, :, sl], zpad], axis=0).astype(BF16)
        vn = jnp.concatenate([vn_ref[0, :, sl], zpad], axis=0).astype(BF16)
        s_old = _dot_nt(q2, ck_ref[0, :, sl].astype(BF16)) + jnp.concatenate([bo_ref[2 * hp], bo_ref[2 * hp + 1]], axis=0)
        s_new = _dot_nt(q2, kn) + jnp.concatenate([bn_ref[2 * hp], bn_ref[2 * hp + 1]], axis=0)
        m = jnp.maximum(jnp.max(s_old, axis=-1, keepdims=True), jnp.max(s_new, axis=-1, keepdims=True))
        p_old = jnp.exp(s_old - m)
        p_new = jnp.exp(s_new - m)
        l = jnp.sum(p_old, axis=-1, keepdims=True) + jnp.sum(p_new, axis=-1, keepdims=True)
        acc = _dot(p_old.astype(BF16), cv_ref[0, :, sl].astype(BF16)) + _dot(p_new.astype(BF16), vn)
        o = acc / l
        outs.append(jnp.where(lane < E, o[0:S], o[S:2 * S]))
    o_ref[0] = jnp.concatenate(outs, axis=-1)


def attn_sample(q, k_new, v_new, cache_k, cache_v, bias_old, bias_new):
    n_batch, S, hd = q.shape
    W = cache_k.shape[1]
    E = hd // H_D
    NPAD = bias_new.shape[-1]
    return pl.pallas_call(
        functools.partial(_attn_sample_kernel, E=E, S=S),
        out_shape=jax.ShapeDtypeStruct((n_batch, S, hd), F32),
        grid=(n_batch,),
        in_specs=[pl.BlockSpec((1, S, hd), lambda b: (b, 0, 0)),
                  pl.BlockSpec((1, S, hd), lambda b: (b, 0, 0)),
                  pl.BlockSpec((1, S, hd), lambda b: (b, 0, 0)),
                  pl.BlockSpec((1, W, hd), lambda b: (b, 0, 0)),
                  pl.BlockSpec((1, W, hd), lambda b: (b, 0, 0)),
                  pl.BlockSpec((H_D, S, W), lambda b: (0, 0, 0)),
                  pl.BlockSpec((H_D, S, NPAD), lambda b: (0, 0, 0))],
        out_specs=pl.BlockSpec((1, S, hd), lambda b: (b, 0, 0)),
        compiler_params=_cparams(("parallel",)),
        name="dilated_attn_sample",
    )(q, k_new, v_new, cache_k, cache_v, bias_old, bias_new)


def _even_weights(j, w_in_even, w_out_even, shift_mu, decay_w0, decay_up, iclr_a0, iclr_up, gate_up, key_k, key_a,
                  bonus_r_k, lnx_g, lnx_b, sgu_norm_g, sgu_norm_b, sgu_w, sgu_b):
    da = decay_w0.shape[1]
    zeros_d = jnp.zeros((R_ICLR, da), F32)
    zeros_i = jnp.zeros((R_DECAY, da), F32)
    return dict(
        w_in=w_in_even[j].astype(BF16),
        w_out_a=w_out_even[j, :da].astype(BF16), w_out_b=w_out_even[j, da:].astype(BF16),
        mu=shift_mu[j].reshape(1, -1), w0=decay_w0[j].reshape(1, -1), a0=iclr_a0[j].reshape(1, -1),
        wd=jnp.concatenate([decay_up[j], zeros_d], axis=0), wa=jnp.concatenate([zeros_i, iclr_up[j]], axis=0),
        wg=gate_up[j], key_k=key_k[j].reshape(1, -1), key_a=key_a[j].reshape(1, -1),
        bonus=bonus_r_k[j].reshape(1, -1), lnx_g=lnx_g[j].reshape(1, -1), lnx_b=lnx_b[j].reshape(1, -1),
        ng=sgu_norm_g[j].reshape(1, -1), nb=sgu_norm_b[j].reshape(1, -1), sgu_w=sgu_w[j], sgu_b=sgu_b[j])


def _gmlp_tables(sgu_w, sgu_b, chunk):
    reps = GMLP_TILE // chunk
    n_h = sgu_w.shape[0]
    cb = None
    wm = sgu_w[:, :chunk, :chunk] * jnp.asarray(np.tril(np.ones((chunk, chunk), np.float32)))
    if reps > 1:
        eye = jnp.asarray(np.eye(reps, dtype=np.float32))
        wm = jnp.einsum("ab,hts->hatbs", eye, wm).reshape(n_h, GMLP_TILE, GMLP_TILE)
    bias = jnp.tile(jnp.transpose(sgu_b[:, :chunk]), (reps, 1))
    return wm.astype(BF16), bias


def _even_layer(x, n_batch, seq, chunk, n_par, shift_prev, wkv0, norm_g, ew):
    pa, pu, pv = norm_matmul(x, norm_g, ew["w_in"], (ew["mu"].shape[1], ew["ng"].shape[1], ew["ng"].shape[1]))
    ya, wkv = rwkv_mix(pa, n_batch, seq, shift_prev, wkv0, ew, n_par)
    wm, bias = _gmlp_tables(ew["sgu_w"], ew["sgu_b"], chunk)
    cb = pu.shape[1] // wm.shape[0]
    bias_tile = jnp.repeat(bias, cb, axis=1)
    yb, vn = gmlp_mix(pu, pv, ew["ng"], ew["nb"], wm, bias_tile)
    x = proj_residual(x, ya, yb, ew["w_out_a"], ew["w_out_b"])
    last = pa.reshape(n_batch, seq, -1)[:, -1]
    return x, last, wkv, vn


def _odd_weights(j, w_in_odd, w_out_odd, conv_w, conv_b, rgate_w, rgate_b, igate_w, igate_b, lru_lambda):
    dc = conv_b.shape[1]
    eye = jnp.asarray(np.eye(H_C, dtype=np.float32))

    def blockdiag(w):
        dh = w.shape[-1]
        return jnp.einsum("ab,aij->aibj", eye, w).reshape(H_C * dh, H_C * dh)

    return dict(
        w_in=w_in_odd[j].astype(BF16),
        w_out_c=w_out_odd[j, :dc].astype(BF16), w_out_d=w_out_odd[j, dc:].astype(BF16),
        conv_w=conv_w[j], conv_b=conv_b[j].reshape(1, -1),
        gate_w=jnp.concatenate([blockdiag(rgate_w[j]), blockdiag(igate_w[j])], axis=1).astype(BF16),
        gate_b=jnp.concatenate([rgate_b[j], igate_b[j]]).reshape(1, -1),
        lam=lru_lambda[j].reshape(1, -1))


def _odd_layer(x, n_batch, seq, conv_prev, h0, pos0, caches, dist_tab, norm_g, ow):
    dc = ow["lam"].shape[1]
    gy, xb, q, k, v = norm_matmul(x, norm_g, ow["w_in"], (dc,) * 5)
    conv_prev8 = jnp.pad(conv_prev, ((0, 0), (8 - (CONV_W - 1), 0), (0, 0)))
    yc, conv_last, h_last = rglru_mix(xb, gy, n_batch, seq, conv_prev8, h0, pos0, ow)
    hd = q.shape[1]
    q3, k3, v3 = (a.reshape(n_batch, seq, hd) for a in (q, k, v))
    if caches is None:
        tiles = _toeplitz_tiles(dist_tab * LOG2E, seq // ATT_TILE, ATT_SUBTILES - 1, ATT_TILE)
        o = attn_prompt(q3, k3, v3, tiles, n_batch, seq)
    else:
        cache_k, cache_v = caches
        W = cache_k.shape[1]
        ck = cache_k.reshape(n_batch, W, hd)
        cv = cache_v.reshape(n_batch, W, hd)
        NPAD = 128
        d_old = W + np.arange(seq)[:, None] - np.arange(W)[None, :]
        b_old = jnp.transpose(jnp.take(dist_tab, jnp.asarray(d_old), axis=0), (2, 0, 1))
        d_new = np.arange(seq)[:, None] - np.arange(NPAD)[None, :]
        ok_new = (d_new >= 0) & (np.arange(NPAD)[None, :] < seq)
        b_new = jnp.take(dist_tab, jnp.asarray(np.maximum(d_new, 0)), axis=0)
        b_new = jnp.transpose(jnp.where(jnp.asarray(ok_new)[..., None], b_new, NEG_BIG), (2, 0, 1))
        o = attn_sample(q3, k3, v3, ck, cv, b_old, b_new)
    x = proj_residual(x, yc, o.reshape(n_batch * seq, hd), ow["w_out_c"], ow["w_out_d"])
    e = hd // H_D
    return x, conv_last, h_last, k3.reshape(n_batch, seq, H_D, e), v3.reshape(n_batch, seq, H_D, e)


def _moe_weights(l, router_group_w, router_group_b, router_expert_w, router_expert_b, exp_w_gate, exp_w_up,
                 exp_w_down):
    d = router_group_w.shape[1]
    n_used = N_GROUPS + router_expert_w.shape[2]
    rw = jnp.concatenate([router_group_w[l], router_expert_w[l], jnp.zeros((d, ROUTER_LANES - n_used), F32)], axis=1)
    rb = jnp.concatenate([router_group_b[l], router_expert_b[l], jnp.zeros((ROUTER_LANES - n_used,), F32)])
    return dict(rw=rw, rb=rb.reshape(1, -1), wg=exp_w_gate[l].astype(BF16), wu=exp_w_up[l].astype(BF16),
                wd=exp_w_down[l].astype(BF16))


def kernel(x_prompt, x_sample, state_wkv, state_shift, state_conv, state_rglru, cache_k, cache_v, norm_mix, norm_ffn, norm_final, w_in_even, w_out_even, shift_mu, decay_w0, decay_up, iclr_a0, iclr_up, gate_up, key_k, key_a, bonus_r_k, lnx_g, lnx_b, sgu_norm_g, sgu_norm_b, sgu_w, sgu_b, w_in_odd, w_out_odd, conv_w, conv_b, rgate_w, rgate_b, igate_w, igate_b, lru_lambda, rel_bias, router_group_w, router_group_b, router_expert_w, router_expert_b, exp_w_gate, exp_w_up, exp_w_down):
    B, L, D = x_prompt.shape
    DB, S, _ = x_sample.shape
    depth = norm_mix.shape[0]
    xp = x_prompt.reshape(B * L, D)
    xs = x_sample.reshape(DB * S, D)
    W = cache_k.shape[2]
    dist_tab = _dist_table(rel_bias, max(L, W + S) - 1)

    wkv_p, shift_p, conv_p, lru_p, k_p, v_p = [], [], [], [], [], []
    wkv_s, shift_s, chunkv_s, conv_s, lru_s, k_s, v_s = [], [], [], [], [], [], []
    for l in range(depth):
        j = l // 2
        if l % 2 == 0:
            ew = _even_weights(j, w_in_even, w_out_even, shift_mu, decay_w0, decay_up, iclr_a0, iclr_up, gate_up,
                               key_k, key_a, bonus_r_k, lnx_g, lnx_b, sgu_norm_g, sgu_norm_b, sgu_w, sgu_b)
            a_proj = ew["mu"].shape[1]
            h_a = state_wkv.shape[2]
            xp, sh, wkv, _ = _even_layer(xp, B, L, GMLP_TILE, RWKV_PAR_PROMPT, jnp.zeros((B, a_proj), F32),
                                         jnp.zeros((B, h_a, DH_A, DH_A), F32), norm_mix[l], ew)
            xs, sh_s, wkv_s_new, vn_s = _even_layer(xs, DB, S, S, RWKV_PAR_SAMPLE, state_shift[j], state_wkv[j], norm_mix[l], ew)
            wkv_p.append(wkv)
            shift_p.append(sh)
            wkv_s.append(wkv_s_new)
            shift_s.append(sh_s)
            chunkv_s.append(vn_s.reshape(DB, S, -1))
        else:
            ow = _odd_weights(j, w_in_odd, w_out_odd, conv_w, conv_b, rgate_w, rgate_b, igate_w, igate_b, lru_lambda)
            dc = ow["lam"].shape[1]
            xp, cv, hl, kr, vr = _odd_layer(xp, B, L, jnp.zeros((B, CONV_W - 1, dc), F32), jnp.zeros((B, dc), F32),
                                            0, None, dist_tab, norm_mix[l], ow)
            xs, cv_s, hl_s, kr_s, vr_s = _odd_layer(xs, DB, S, state_conv[j], state_rglru[j], PAST_LEN,
                                                    (cache_k[j], cache_v[j]), dist_tab, norm_mix[l], ow)
            conv_p.append(cv)
            lru_p.append(hl)
            k_p.append(kr)
            v_p.append(vr)
            conv_s.append(cv_s)
            lru_s.append(hl_s)
            k_s.append(kr_s)
            v_s.append(vr_s)
        mw = _moe_weights(l, router_group_w, router_group_b, router_expert_w, router_expert_b, exp_w_gate, exp_w_up,
                          exp_w_down)
        xp = moe_layer_sparse(xp, norm_ffn[l], mw["rw"], mw["rb"], mw["wg"], mw["wu"], mw["wd"],
                              final_g=norm_final if l == depth - 1 else None)
        xs = moe_layer(xs, norm_ffn[l], mw["rw"], mw["rb"], mw["wg"], mw["wu"], mw["wd"])
    y_prompt = xp.reshape(B, L, D)
    y_sample = rmsnorm_call(xs, norm_final).reshape(DB, S, D)
    return (y_prompt, y_sample,
            jnp.stack(wkv_p), jnp.stack(shift_p), jnp.stack(conv_p), jnp.stack(lru_p), jnp.stack(k_p), jnp.stack(v_p),
            jnp.stack(wkv_s), jnp.stack(shift_s), jnp.stack(chunkv_s), jnp.stack(conv_s), jnp.stack(lru_s),
            jnp.stack(k_s), jnp.stack(v_s))
```

```python
import functools
import math

import numpy as np
import jax
import jax.numpy as jnp
from jax import lax
from jax.experimental import pallas as pl
from jax.experimental.pallas import tpu as pltpu

F32 = jnp.float32
BF16 = jnp.bfloat16
HI = lax.Precision.HIGHEST

PAST_LEN = 8192
DH_A = 64
R_DECAY = 64
R_ICLR = 64
R_GATE = 128
GN_EPS = 64e-5
H_B = 4
H_C = 8
CONV_W = 4
LRU_C = 8.0
H_D = 8
DILATED = ((128, 1), (512, 4), (2048, 16))
N_BUCKETS = 32
BUCKET_MAX_DIST = 2048
NEG_BIG = -1e30
N_GROUPS = 4
EXP_PER_GROUP = 4
NORM_EPS = 1e-6
LOG2E = math.log2(math.e)

VMEM_LIMIT = 56 * 1024 * 1024
RWKV_CHUNK = 64
RWKV_PAR_PROMPT = 4
RWKV_PAR_SAMPLE = 8
ATT_TILE = 128
ATT_SUBTILES = 4
LANES = 128
SUBLANES = 8


def _cparams(sem):
    return pltpu.CompilerParams(dimension_semantics=sem, vmem_limit_bytes=VMEM_LIMIT)


def _dot(a, b, precision=None):
    return jnp.dot(a, b, preferred_element_type=F32, precision=precision)


def _dot_nt(a, b, precision=None):
    return lax.dot_general(a, b, (((1,), (1,)), ((), ())), preferred_element_type=F32, precision=precision)


def _dot_tn(a, b, precision=None):
    return lax.dot_general(a, b, (((0,), (0,)), ((), ())), preferred_element_type=F32, precision=precision)


def _split_bf16(x, n):
    parts = []
    for _ in range(n):
        hi = x.astype(BF16)
        parts.append(hi)
        x = x - hi.astype(F32)
    return parts


def _mp_dot(dotfn, a, b, passes):
    if passes == 1:
        return dotfn(a.astype(BF16), b.astype(BF16))
    a_hi, a_lo = _split_bf16(a, 2)
    b_hi, b_lo = _split_bf16(b, 2)
    return dotfn(a_hi, b_hi) + (dotfn(a_hi, b_lo) + dotfn(a_lo, b_hi))


def _dot_exact_rhs(a, b_bf16, n_split):
    parts = _split_bf16(a, n_split)
    acc = _dot(parts[0], b_bf16)
    for part in parts[1:]:
        acc = acc + _dot(part, b_bf16)
    return acc


def _dot_exact_lhs(a_bf16, b, n_split):
    parts = _split_bf16(b, n_split)
    acc = _dot(a_bf16, parts[0])
    for part in parts[1:]:
        acc = acc + _dot(a_bf16, part)
    return acc


def _softplus(x):
    return jnp.maximum(x, 0.0) + jnp.log(1.0 + jnp.exp(-jnp.abs(x)))


def _sigmoid(x):
    return 1.0 / (1.0 + jnp.exp(-x))


def _gelu(x):
    c = math.sqrt(2.0 / math.pi)
    return 0.5 * x * (1.0 + jnp.tanh(c * (x + 0.044715 * (x * x * x))))


def _row_tile(t, pref=512):
    return pref if t % pref == 0 else t


def _norm_matmul_kernel(x_ref, g_ref, w_ref, *out_refs, splits):
    x = x_ref[...]
    ms = jnp.mean(x * x, axis=-1, keepdims=True)
    h = (x * lax.rsqrt(ms + NORM_EPS) * g_ref[...]).astype(BF16)
    off = 0
    for o_ref, n in zip(out_refs, splits):
        o_ref[...] = _dot(h, w_ref[:, off:off + n])
        off += n


def norm_matmul(x, g, w_bf16, splits):
    t, d = x.shape
    n = w_bf16.shape[1]
    tm = _row_tile(t)
    return pl.pallas_call(
        functools.partial(_norm_matmul_kernel, splits=splits),
        out_shape=[jax.ShapeDtypeStruct((t, s), F32) for s in splits],
        grid=(t // tm,),
        in_specs=[pl.BlockSpec((tm, d), lambda i: (i, 0)),
                  pl.BlockSpec((1, d), lambda i: (0, 0)),
                  pl.BlockSpec((d, n), lambda i: (0, 0))],
        out_specs=[pl.BlockSpec((tm, s), lambda i: (i, 0)) for s in splits],
        compiler_params=_cparams(("parallel",)),
        name="norm_matmul",
    )(x, g.reshape(1, d), w_bf16)


def _proj_res_kernel(x_ref, a_ref, b_ref, wa_ref, wb_ref, o_ref):
    acc = _dot(a_ref[...].astype(BF16), wa_ref[...]) + _dot(b_ref[...].astype(BF16), wb_ref[...])
    o_ref[...] = x_ref[...] + acc


def proj_residual(x, a, b, wa, wb):
    t, d = x.shape
    tm = _row_tile(t)
    ka, kb = a.shape[1], b.shape[1]
    return pl.pallas_call(
        _proj_res_kernel,
        out_shape=jax.ShapeDtypeStruct((t, d), F32),
        grid=(t // tm,),
        in_specs=[pl.BlockSpec((tm, d), lambda i: (i, 0)),
                  pl.BlockSpec((tm, ka), lambda i: (i, 0)),
                  pl.BlockSpec((tm, kb), lambda i: (i, 0)),
                  pl.BlockSpec((ka, d), lambda i: (0, 0)),
                  pl.BlockSpec((kb, d), lambda i: (0, 0))],
        out_specs=pl.BlockSpec((tm, d), lambda i: (i, 0)),
        compiler_params=_cparams(("parallel",)),
        name="proj_residual",
    )(x, a, b, wa, wb)


def _rmsnorm_kernel(x_ref, g_ref, o_ref):
    x = x_ref[...]
    ms = jnp.mean(x * x, axis=-1, keepdims=True)
    o_ref[...] = x * lax.rsqrt(ms + NORM_EPS) * g_ref[...]


def rmsnorm_call(x, g):
    t, d = x.shape
    tm = _row_tile(t)
    return pl.pallas_call(
        _rmsnorm_kernel,
        out_shape=jax.ShapeDtypeStruct((t, d), F32),
        grid=(t // tm,),
        in_specs=[pl.BlockSpec((tm, d), lambda i: (i, 0)), pl.BlockSpec((1, d), lambda i: (0, 0))],
        out_specs=pl.BlockSpec((tm, d), lambda i: (i, 0)),
        compiler_params=_cparams(("parallel",)),
        name="final_rmsnorm",
    )(x, g.reshape(1, d))


ROUTER_LANES = 128


def _route(xn, rw, rb, lane, n_exp):
    logits = _mp_dot(_dot, xn, rw, 3) + rb
    lg = jnp.where(lane < N_GROUPS, logits, -jnp.inf)
    gm = jnp.max(lg, axis=-1, keepdims=True)
    top_pg = 1.0 / jnp.sum(jnp.exp(lg - gm), axis=-1, keepdims=True)
    grp = jnp.min(jnp.where(lg == gm, lane, ROUTER_LANES), axis=-1, keepdims=True)
    in_grp = (lane >= N_GROUPS) & (lane < N_GROUPS + n_exp) & (((lane - N_GROUPS) // EXP_PER_GROUP) == grp)
    le = jnp.where(in_grp, logits, -jnp.inf)
    t1 = jnp.max(le, axis=-1, keepdims=True)
    i1 = jnp.min(jnp.where(le == t1, lane, ROUTER_LANES), axis=-1, keepdims=True)
    le2 = jnp.where(lane == i1, -jnp.inf, le)
    t2 = jnp.max(le2, axis=-1, keepdims=True)
    i2 = jnp.min(jnp.where(le2 == t2, lane, ROUTER_LANES), axis=-1, keepdims=True)
    ex = jnp.exp(t2 - t1)
    w1 = 1.0 / (1.0 + ex)
    return i1, i2, w1 * top_pg, (ex * w1) * top_pg


def _moe_kernel(x_ref, g_ref, rw_ref, rb_ref, wg_ref, wu_ref, wd_ref, o_ref, xn_scr, gate_scr, acc_scr, *, n_exp):
    e = pl.program_id(1)
    tm = x_ref.shape[0]
    lane = lax.broadcasted_iota(jnp.int32, (tm, ROUTER_LANES), 1)

    @pl.when(e == 0)
    def _():
        x = x_ref[...]
        ms = jnp.mean(x * x, axis=-1, keepdims=True)
        xn = x * lax.rsqrt(ms + NORM_EPS) * g_ref[...]
        xn_scr[...] = xn.astype(BF16)
        i1, i2, g1, g2 = _route(xn, rw_ref[...], rb_ref[...], lane, n_exp)
        gate_scr[...] = jnp.where(lane == i1, g1, 0.0) + jnp.where(lane == i2, g2, 0.0)
        acc_scr[...] = jnp.zeros_like(acc_scr)

    xn = xn_scr[...]
    hg = _dot(xn, wg_ref[0])
    hu = _dot(xn, wu_ref[0])
    gcol = jnp.sum(jnp.where(lane == e + N_GROUPS, gate_scr[...], 0.0), axis=-1, keepdims=True)
    hid = hg * _sigmoid(hg) * hu * gcol
    acc_scr[...] += _dot(hid.astype(BF16), wd_ref[0])

    @pl.when(e == n_exp - 1)
    def _():
        o_ref[...] = x_ref[...] + acc_scr[...]


def moe_layer(x, g, rw, rb, wg, wu, wd):
    t, d = x.shape
    n_exp, _, f = wg.shape
    tm = _row_tile(t)
    return pl.pallas_call(
        functools.partial(_moe_kernel, n_exp=n_exp),
        out_shape=jax.ShapeDtypeStruct((t, d), F32),
        grid=(t // tm, n_exp),
        in_specs=[pl.BlockSpec((tm, d), lambda i, e: (i, 0)),
                  pl.BlockSpec((1, d), lambda i, e: (0, 0)),
                  pl.BlockSpec((d, ROUTER_LANES), lambda i, e: (0, 0)),
                  pl.BlockSpec((1, ROUTER_LANES), lambda i, e: (0, 0)),
                  pl.BlockSpec((1, d, f), lambda i, e: (e, 0, 0)),
                  pl.BlockSpec((1, d, f), lambda i, e: (e, 0, 0)),
                  pl.BlockSpec((1, f, d), lambda i, e: (e, 0, 0))],
        out_specs=pl.BlockSpec((tm, d), lambda i, e: (i, 0)),
        scratch_shapes=[pltpu.VMEM((tm, d), BF16), pltpu.VMEM((tm, ROUTER_LANES), F32), pltpu.VMEM((tm, d), F32)],
        compiler_params=_cparams(("parallel", "arbitrary")),
        name="hier_moe",
    )(x, g.reshape(1, d), rw, rb, wg, wu, wd)


MOE_ROW_TILE = 512
MOE_COPY_CHUNK = 256
MOE_COMBINE_TILE = 256


def _router_kernel(x_ref, g_ref, rw_ref, rb_ref, tri_ref, gate_ref, info_ref, cnt_ref, base_scr, *, n_exp, n_tiles):
    i = pl.program_id(0)
    tm = x_ref.shape[0]
    lane = lax.broadcasted_iota(jnp.int32, (tm, ROUTER_LANES), 1)

    @pl.when(i == 0)
    def _():
        base_scr[...] = jnp.zeros_like(base_scr)

    x = x_ref[...]
    ms = jnp.mean(x * x, axis=-1, keepdims=True)
    xn = x * lax.rsqrt(ms + NORM_EPS) * g_ref[...]
    i1, i2, g1, g2 = _route(xn, rw_ref[...], rb_ref[...], lane, n_exp)
    chosen = jnp.where((lane == i1) | (lane == i2), 1.0, 0.0)
    before = _dot(tri_ref[...], chosen.astype(BF16)) + base_scr[...]
    r1 = jnp.sum(jnp.where(lane == i1, before, 0.0), axis=-1, keepdims=True)
    r2 = jnp.sum(jnp.where(lane == i2, before, 0.0), axis=-1, keepdims=True)
    base_scr[...] += jnp.sum(chosen, axis=0, keepdims=True)
    gate_ref[...] = jnp.where(lane == 0, g1, 0.0) + jnp.where(lane == 1, g2, 0.0)
    e1 = (i1 - N_GROUPS).astype(F32)
    e2 = (i2 - N_GROUPS).astype(F32)
    info_ref[...] = (jnp.where(lane == 0, e1, 0.0) + jnp.where(lane == 1, e2, 0.0)
                     + jnp.where(lane == 2, r1, 0.0) + jnp.where(lane == 3, r2, 0.0))

    @pl.when(i == n_tiles - 1)
    def _():
        cnt_ref[...] = base_scr[...]


def _scatter_rows_kernel(pos0_ref, pos1_ref, x_ref, xs_in_hbm, xs_hbm, stage, sem, *, CH, n_chunks):
    del xs_in_hbm
    c = pl.program_id(0)
    slot = c % 2
    x = x_ref[...]
    for j in range(SUBLANES):
        stage[slot, pl.ds(j, CH, stride=SUBLANES), :] = x[:, j * LANES:(j + 1) * LANES]

    def body(r, carry):
        t = c * CH + r
        src = stage.at[slot, pl.ds(pl.multiple_of(r * SUBLANES, SUBLANES), SUBLANES), :]
        d0 = pl.multiple_of(pos0_ref[t], SUBLANES)
        d1 = pl.multiple_of(pos1_ref[t], SUBLANES)
        pltpu.make_async_copy(src, xs_hbm.at[pl.ds(d0, SUBLANES), :], sem.at[slot]).start(priority=0)
        pltpu.make_async_copy(src, xs_hbm.at[pl.ds(d1, SUBLANES), :], sem.at[slot]).start(priority=1)
        return carry

    lax.fori_loop(0, CH, body, 0, unroll=8)

    def drain(s):
        pltpu.make_async_copy(stage.at[s], xs_hbm.at[pl.ds(0, CH * SUBLANES), :], sem.at[s]).wait()
        pltpu.make_async_copy(stage.at[s], xs_hbm.at[pl.ds(0, CH * SUBLANES), :], sem.at[s]).wait()

    @pl.when(c > 0)
    def _():
        drain(1 - slot)

    @pl.when(c == n_chunks - 1)
    def _():
        drain(slot)


def _tile_rows_to_matrix(ref, lead, n_rows):
    return jnp.concatenate([ref[lead + (pl.ds(j, n_rows, stride=SUBLANES), slice(None))] for j in range(SUBLANES)],
                           axis=-1)


def _expert_kernel(te_ref, nv_ref, xs_ref, g_ref, wg_ref, wu_ref, wd_ref, y_ref, *, TM):
    @pl.when(pl.program_id(0) < nv_ref[0])
    def _():
        x = _tile_rows_to_matrix(xs_ref, (), TM)
        ms = jnp.mean(x * x, axis=-1, keepdims=True)
        xn = (x * lax.rsqrt(ms + NORM_EPS) * g_ref[...]).astype(BF16)
        hg = _dot(xn, wg_ref[0])
        hu = _dot(xn, wu_ref[0])
        hid = hg * _sigmoid(hg) * hu
        y = _dot(hid.astype(BF16), wd_ref[0])
        for j in range(SUBLANES):
            y_ref[pl.ds(j, TM, stride=SUBLANES), :] = y[:, j * LANES:(j + 1) * LANES]

    @pl.when(pl.program_id(0) >= nv_ref[0])
    def _():
        y_ref[...] = jnp.zeros_like(y_ref)


def _combine_kernel(pos0_ref, pos1_ref, x_ref, gate_ref, fg_ref, y_hbm, o_ref, ybuf, sem, *, TC, n_tiles, final_norm):
    i = pl.program_id(0)

    def issue(tile, slot):
        def body(r, carry):
            t = tile * TC + r
            dst = pl.ds(pl.multiple_of(r * SUBLANES, SUBLANES), SUBLANES)
            s0 = pl.multiple_of(pos0_ref[t], SUBLANES)
            s1 = pl.multiple_of(pos1_ref[t], SUBLANES)
            pltpu.make_async_copy(y_hbm.at[pl.ds(s0, SUBLANES), :], ybuf.at[slot, 0, dst, :],
                                  sem.at[slot]).start(priority=0)
            pltpu.make_async_copy(y_hbm.at[pl.ds(s1, SUBLANES), :], ybuf.at[slot, 1, dst, :],
                                  sem.at[slot]).start(priority=1)
            return carry
        lax.fori_loop(0, TC, body, 0, unroll=8)

    @pl.when(i == 0)
    def _():
        issue(0, 0)

    @pl.when(i + 1 < n_tiles)
    def _():
        issue(i + 1, (i + 1) % 2)

    slot = i % 2
    pltpu.make_async_copy(y_hbm.at[pl.ds(0, TC * SUBLANES), :], ybuf.at[slot, 0], sem.at[slot]).wait()
    pltpu.make_async_copy(y_hbm.at[pl.ds(0, TC * SUBLANES), :], ybuf.at[slot, 1], sem.at[slot]).wait()
    gate = gate_ref[...]
    y0 = _tile_rows_to_matrix(ybuf, (slot, 0), TC)
    y1 = _tile_rows_to_matrix(ybuf, (slot, 1), TC)
    out = x_ref[...] + gate[:, 0:1] * y0 + gate[:, 1:2] * y1
    if final_norm:
        ms = jnp.mean(out * out, axis=-1, keepdims=True)
        out = out * lax.rsqrt(ms + NORM_EPS) * fg_ref[...]
    o_ref[...] = out


def moe_layer_sparse(x, g, rw, rb, wg, wu, wd, final_g=None):
    t, d = x.shape
    n_exp, _, f = wg.shape
    TM = MOE_ROW_TILE
    n_tiles = t // TM
    tri = jnp.asarray(np.tril(np.ones((TM, TM), np.float32), -1)).astype(BF16)
    gate, info, cnt = pl.pallas_call(
        functools.partial(_router_kernel, n_exp=n_exp, n_tiles=n_tiles),
        out_shape=[jax.ShapeDtypeStruct((t, ROUTER_LANES), F32), jax.ShapeDtypeStruct((t, ROUTER_LANES), F32),
                   jax.ShapeDtypeStruct((1, ROUTER_LANES), F32)],
        grid=(n_tiles,),
        in_specs=[pl.BlockSpec((TM, d), lambda i: (i, 0)),
                  pl.BlockSpec((1, d), lambda i: (0, 0)),
                  pl.BlockSpec((d, ROUTER_LANES), lambda i: (0, 0)),
                  pl.BlockSpec((1, ROUTER_LANES), lambda i: (0, 0)),
                  pl.BlockSpec((TM, TM), lambda i: (0, 0))],
        out_specs=[pl.BlockSpec((TM, ROUTER_LANES), lambda i: (i, 0)),
                   pl.BlockSpec((TM, ROUTER_LANES), lambda i: (i, 0)),
                   pl.BlockSpec((1, ROUTER_LANES), lambda i: (0, 0))],
        scratch_shapes=[pltpu.VMEM((1, ROUTER_LANES), F32)],
        compiler_params=_cparams(("arbitrary",)),
        name="moe_router",
    )(x, g.reshape(1, d), rw, rb, tri)

    counts = cnt[0, N_GROUPS:N_GROUPS + n_exp].astype(jnp.int32)
    padded = ((counts + TM - 1) // TM) * TM
    ends = jnp.cumsum(padded)
    offs = ends - padded
    eid = info[:, 0:2].astype(jnp.int32)
    rank = info[:, 2:4].astype(jnp.int32)
    pos = jnp.sum(jnp.where(eid[:, :, None] == jnp.arange(n_exp)[None, None, :], offs[None, None, :], 0), axis=-1) + rank
    assert d == SUBLANES * LANES, "a token row must fill exactly one (8, 128) tile"
    pos = pos * SUBLANES
    pos0, pos1 = pos[:, 0], pos[:, 1]
    max_tiles = (2 * t) // TM + n_exp
    n_valid = (ends[-1] // TM).astype(jnp.int32).reshape(1)
    tile_exp = jnp.minimum(jnp.sum((ends[None, :] // TM) <= jnp.arange(max_tiles)[:, None], axis=-1),
                           n_exp - 1).astype(jnp.int32)
    p_rows = max_tiles * TM

    CH = MOE_COPY_CHUNK
    xs = pl.pallas_call(
        functools.partial(_scatter_rows_kernel, CH=CH, n_chunks=t // CH),
        out_shape=jax.ShapeDtypeStruct((p_rows * SUBLANES, LANES), F32),
        grid_spec=pltpu.PrefetchScalarGridSpec(
            num_scalar_prefetch=2, grid=(t // CH,),
            in_specs=[pl.BlockSpec((CH, d), lambda c, p0, p1: (c, 0)), pl.BlockSpec(memory_space=pl.ANY)],
            out_specs=pl.BlockSpec(memory_space=pl.ANY),
            scratch_shapes=[pltpu.VMEM((2, CH * SUBLANES, LANES), F32), pltpu.SemaphoreType.DMA((2,))]),
        input_output_aliases={3: 0},
        compiler_params=pltpu.CompilerParams(dimension_semantics=("arbitrary",), vmem_limit_bytes=VMEM_LIMIT,
                                             has_side_effects=True),
        name="moe_scatter_rows",
    )(pos0, pos1, x, jnp.zeros((p_rows * SUBLANES, LANES), F32))

    def row_idx(i, te, nv):
        return (jnp.minimum(i, nv[0] - 1), 0)

    ys = pl.pallas_call(
        functools.partial(_expert_kernel, TM=TM),
        out_shape=jax.ShapeDtypeStruct((p_rows * SUBLANES, LANES), F32),
        grid_spec=pltpu.PrefetchScalarGridSpec(
            num_scalar_prefetch=2, grid=(max_tiles,),
            in_specs=[pl.BlockSpec((TM * SUBLANES, LANES), row_idx),
                      pl.BlockSpec((1, d), lambda i, te, nv: (0, 0)),
                      pl.BlockSpec((1, d, f), lambda i, te, nv: (te[i], 0, 0)),
                      pl.BlockSpec((1, d, f), lambda i, te, nv: (te[i], 0, 0)),
                      pl.BlockSpec((1, f, d), lambda i, te, nv: (te[i], 0, 0))],
            out_specs=pl.BlockSpec((TM * SUBLANES, LANES), lambda i, te, nv: (i, 0))),
        compiler_params=_cparams(("arbitrary",)),
        name="moe_experts",
    )(tile_exp, n_valid, xs, g.reshape(1, d), wg, wu, wd)

    TC = MOE_COMBINE_TILE
    return pl.pallas_call(
        functools.partial(_combine_kernel, TC=TC, n_tiles=t // TC, final_norm=final_g is not None),
        out_shape=jax.ShapeDtypeStruct((t, d), F32),
        grid_spec=pltpu.PrefetchScalarGridSpec(
            num_scalar_prefetch=2, grid=(t // TC,),
            in_specs=[pl.BlockSpec((TC, d), lambda i, p0, p1: (i, 0)),
                      pl.BlockSpec((TC, ROUTER_LANES), lambda i, p0, p1: (i, 0)),
                      pl.BlockSpec((1, d), lambda i, p0, p1: (0, 0)),
                      pl.BlockSpec(memory_space=pl.ANY)],
            out_specs=pl.BlockSpec((TC, d), lambda i, p0, p1: (i, 0)),
            scratch_shapes=[pltpu.VMEM((2, 2, TC * SUBLANES, LANES), F32), pltpu.SemaphoreType.DMA((2,))]),
        compiler_params=_cparams(("arbitrary",)),
        name="moe_combine",
    )(pos0, pos1, x, gate, (g if final_g is None else final_g).reshape(1, d), ys)


def _rwkv_kernel(p_ref, prev_ref, s0_ref, mu_ref, w0_ref, wd_ref, a0_ref, wa_ref, wg_ref, kk_ref, ka_ref,
                 bonus_ref, lng_ref, lnb_ref, tri_ref, hsum_ref, ya_ref, sf_ref, s_scr, prev_scr,
                 *, NB, C, H, DH, n_chunks):
    c = pl.program_id(1)

    @pl.when(c == 0)
    def _():
        s_scr[...] = s0_ref[:, 0]
        prev_scr[...] = prev_ref[:, 0]

    DA = H * DH
    R = NB * C
    p = p_ref[...].reshape(R, p_ref.shape[-1])
    row = lax.broadcasted_iota(jnp.int32, p.shape, 0)
    shifted = pltpu.roll(p, 1, axis=0)
    for n in range(NB):
        shifted = jnp.where(row == n * C, prev_scr[n], shifted)
        prev_scr[n] = p[(n + 1) * C - 1:(n + 1) * C, :]
    xs = p + (shifted - p) * mu_ref[...]
    r = xs[:, 0:DA]
    k = xs[:, DA:2 * DA]
    v = xs[:, 2 * DA:3 * DA]
    lora = xs[:, 3 * DA:3 * DA + R_DECAY + R_ICLR]
    gd = xs[:, 3 * DA + R_DECAY + R_ICLR:3 * DA + R_DECAY + R_ICLR + R_GATE]

    w_log = -_softplus(-(w0_ref[...] + _mp_dot(_dot, jnp.tanh(lora), wd_ref[...], 3))) - 0.5
    lw = -jnp.exp(w_log)
    a = _sigmoid(a0_ref[...] + _mp_dot(_dot, lora, wa_ref[...], 3))
    g = _mp_dot(_dot, _sigmoid(gd), wg_ref[...], 3)

    kk = k * kk_ref[...]
    ss = _dot_exact_rhs(kk * kk, hsum_ref[...], 2)
    kk = kk / jnp.maximum(jnp.sqrt(ss), 1e-12)
    k2 = k * (1.0 + (a - 1.0) * ka_ref[...])
    kka = kk * a

    cum = jnp.concatenate([_dot_exact_lhs(tri_ref[...], lw[n * C:(n + 1) * C], 3) for n in range(NB)], axis=0)
    p_in = jnp.exp(cum)
    r_t = r * p_in
    a_t = kk * jnp.exp(cum - lw)
    p_inv = jnp.exp(-cum)
    b_t = kka * p_inv
    k_t = k2 * p_inv
    bonus = _dot_exact_rhs(r * k2 * bonus_ref[...], hsum_ref[...], 2) * v

    ri = lax.broadcasted_iota(jnp.int32, (C, C), 0)
    ci = lax.broadcasted_iota(jnp.int32, (C, C), 1)
    strict = ri > ci
    incl = ri >= ci
    eye = (ri == ci).astype(F32)
    n_double = max(int(math.ceil(math.log2(C))) - 1, 0)

    chains = [(n, h) for n in range(NB) for h in range(H)]

    def blk(x, n, h):
        return x[n * C:(n + 1) * C, h * DH:(h + 1) * DH]

    def bf(x):
        return x.astype(BF16)

    Bt = [bf(blk(b_t, n, h)) for n, h in chains]
    Kt = [bf(blk(k_t, n, h)) for n, h in chains]
    Vf = [blk(v, n, h) for n, h in chains]
    AR = [bf(jnp.concatenate([blk(a_t, n, h), blk(r_t, n, h)], axis=0)) for n, h in chains]
    S0 = [s_scr[n, h] for n, h in chains]
    idx = range(len(chains))
    GB = [_dot_nt(AR[i], Bt[i]) for i in idx]
    GK = [_dot_nt(AR[i], Kt[i]) for i in idx]
    ARS = [_dot_nt(AR[i], bf(S0[i])) for i in idx]
    Lm = [jnp.where(strict, GB[i][0:C], 0.0) for i in idx]
    Gb = [bf(jnp.where(incl, GB[i][C:2 * C], 0.0)) for i in idx]
    MG = [bf(jnp.concatenate([jnp.where(strict, GK[i][0:C], 0.0), jnp.where(incl, GK[i][C:2 * C], 0.0)], axis=0))
          for i in idx]
    MGV = [_dot(MG[i], bf(Vf[i])) for i in idx]
    T = [eye - Lm[i] for i in idx]
    Pw = [bf(Lm[i]) for i in idx]
    for _ in range(n_double):
        Pw = [bf(_dot(Pw[i], Pw[i])) for i in idx]
        T = [T[i] + _dot(bf(T[i]), Pw[i]) for i in idx]
    U = [_dot(bf(T[i]), bf(-(ARS[i][0:C] + MGV[i][0:C]))) for i in idx]
    Y = [ARS[i][C:2 * C] + _dot(Gb[i], bf(U[i])) + MGV[i][C:2 * C] for i in idx]
    for i, (n, h) in enumerate(chains):
        UV = bf(jnp.concatenate([U[i], Vf[i]], axis=0))
        BK = jnp.concatenate([Bt[i], Kt[i]], axis=0)
        p_tot = p_in[(n + 1) * C - 1:(n + 1) * C, h * DH:(h + 1) * DH]
        s_scr[n, h] = (S0[i] + _dot_tn(UV, BK)) * p_tot

    rows = []
    for n in range(NB):
        ys = []
        for h in range(H):
            Yh = Y[n * H + h]
            yc = Yh - jnp.mean(Yh, axis=-1, keepdims=True)
            var = jnp.mean(yc * yc, axis=-1, keepdims=True)
            ys.append(yc * lax.rsqrt(var + GN_EPS))
        rows.append(jnp.concatenate(ys, axis=-1))
    y = jnp.concatenate(rows, axis=0) * lng_ref[...] + lnb_ref[...]
    ya_ref[...] = ((y + bonus) * g).reshape(NB, C, DA)

    @pl.when(c == n_chunks - 1)
    def _():
        sf_ref[:, 0] = s_scr[...]


def rwkv_mix(pa, n_batch, seq, shift_prev, wkv0, wts, n_par):
    t, ap = pa.shape
    H = wkv0.shape[1]
    DA = H * DH_A
    C = min(RWKV_CHUNK, seq)
    n_chunks = seq // C
    NB = n_par
    G = n_batch // NB
    tri = jnp.asarray(np.tril(np.ones((C, C), np.float32))).astype(BF16)
    hsum = jnp.asarray(np.kron(np.eye(H, dtype=np.float32), np.ones((DH_A, DH_A), np.float32))).astype(BF16)

    def full(shape):
        nd = len(shape)
        return pl.BlockSpec(shape, lambda b, c: (0,) * nd)

    vec = full((1, DA))
    ya, s_fin = pl.pallas_call(
        functools.partial(_rwkv_kernel, NB=NB, C=C, H=H, DH=DH_A, n_chunks=n_chunks),
        out_shape=[jax.ShapeDtypeStruct((NB, t // NB, DA), F32),
                   jax.ShapeDtypeStruct((NB, G, H, DH_A, DH_A), F32)],
        grid=(G, n_chunks),
        in_specs=[pl.BlockSpec((NB, C, ap), lambda b, c: (0, b * n_chunks + c, 0)),
                  pl.BlockSpec((NB, 1, 1, ap), lambda b, c: (0, b, 0, 0)),
                  pl.BlockSpec((NB, 1, H, DH_A, DH_A), lambda b, c: (0, b, 0, 0, 0)),
                  full((1, ap)), vec, full((R_DECAY + R_ICLR, DA)), vec, full((R_DECAY + R_ICLR, DA)),
                  full((R_GATE, DA)), vec, vec, vec, vec, vec, full((C, C)), full((DA, DA))],
        out_specs=[pl.BlockSpec((NB, C, DA), lambda b, c: (0, b * n_chunks + c, 0)),
                   pl.BlockSpec((NB, 1, H, DH_A, DH_A), lambda b, c: (0, b, 0, 0, 0))],
        scratch_shapes=[pltpu.VMEM((NB, H, DH_A, DH_A), F32), pltpu.VMEM((NB, 1, ap), F32)],
        compiler_params=_cparams(("parallel", "arbitrary")),
        name="rwkv7_mix",
    )(pa.reshape(NB, t // NB, ap), shift_prev.reshape(NB, G, 1, ap), wkv0.reshape(NB, G, H, DH_A, DH_A),
      wts["mu"], wts["w0"], wts["wd"], wts["a0"], wts["wa"],
      wts["wg"], wts["key_k"], wts["key_a"], wts["bonus"], wts["lnx_g"], wts["lnx_b"], tri, hsum)
    return ya.reshape(t, DA), s_fin.reshape(wkv0.shape)


GMLP_TILE = 128
GMLP_ROWS = 512


def _gmlp_kernel(u_ref, v_ref, ng_ref, nb_ref, wm_ref, bias_ref, o_ref, *vn_refs, n_sub):
    vf = _gelu(v_ref[...])
    mu = jnp.mean(vf, axis=-1, keepdims=True)
    vc = vf - mu
    var = jnp.mean(vc * vc, axis=-1, keepdims=True)
    vn = vc * lax.rsqrt(var + NORM_EPS) * ng_ref[...] + nb_ref[...]
    for vn_ref in vn_refs:
        vn_ref[...] = vn
    vb = vn.astype(BF16)
    n_h = wm_ref.shape[0]
    cb = vn.shape[1] // n_h
    gu = _gelu(u_ref[...])
    for c in range(n_sub):
        rows = slice(c * GMLP_TILE, (c + 1) * GMLP_TILE)
        s = jnp.concatenate([_dot(wm_ref[h], vb[rows, h * cb:(h + 1) * cb]) for h in range(n_h)], axis=-1)
        o_ref[rows, :] = gu[rows, :] * (s + bias_ref[...])


def gmlp_mix(pu, pv, ng, nb, wm_bf16, bias_tile, want_vn):
    t, db = pu.shape
    n_h = wm_bf16.shape[0]
    rows = GMLP_ROWS if t % GMLP_ROWS == 0 else t
    n_out = 2 if want_vn else 1
    outs = pl.pallas_call(
        functools.partial(_gmlp_kernel, n_sub=rows // GMLP_TILE),
        out_shape=[jax.ShapeDtypeStruct((t, db), F32)] * n_out,
        grid=(t // rows,),
        in_specs=[pl.BlockSpec((rows, db), lambda i: (i, 0)),
                  pl.BlockSpec((rows, db), lambda i: (i, 0)),
                  pl.BlockSpec((1, db), lambda i: (0, 0)),
                  pl.BlockSpec((1, db), lambda i: (0, 0)),
                  pl.BlockSpec((n_h, GMLP_TILE, GMLP_TILE), lambda i: (0, 0, 0)),
                  pl.BlockSpec((GMLP_TILE, db), lambda i: (0, 0))],
        out_specs=[pl.BlockSpec((rows, db), lambda i: (i, 0))] * n_out,
        compiler_params=_cparams(("parallel",)),
        name="gmlp_mix",
    )(pu, pv, ng, nb, wm_bf16, bias_tile)
    return (outs[0], outs[1]) if want_vn else (outs[0], None)


N_SEG = 8


def _rglru_kernel(xb_ref, gy_ref, cprev_ref, h0_ref, cw_ref, cb_ref, gw_ref, gb_ref, lam_ref,
                  yc_ref, ctail_ref, hl_ref, xe_scr, a_scr, b_scr, h_scr, *, TL, DC, pos0, n_tiles):
    l = pl.program_id(1)
    PAD = 8

    @pl.when(l == 0)
    def _():
        xe_scr[0:PAD, :] = cprev_ref[0]
        h_scr[...] = h0_ref[0]

    xe_scr[PAD:PAD + TL, :] = xb_ref[...]
    xc = cb_ref[...] + xe_scr[pl.ds(PAD - (CONV_W - 1), TL), :] * cw_ref[0:1, :]
    for i in range(1, CONV_W):
        xc = xc + xe_scr[pl.ds(PAD - (CONV_W - 1) + i, TL), :] * cw_ref[i:i + 1, :]
    tail = xe_scr[TL:TL + PAD, :]
    ctail_ref[0] = tail
    xe_scr[0:PAD, :] = tail

    gates = _dot(xc.astype(BF16), gw_ref[...]) + gb_ref[...]
    rg = _sigmoid(gates[:, 0:DC])
    ig = _sigmoid(gates[:, DC:2 * DC])
    log_a = -LRU_C * rg * _softplus(-lam_ref[...])
    a = jnp.exp(log_a)
    mult = jnp.sqrt(1.0 - a * a)
    row = lax.broadcasted_iota(jnp.int32, (TL, DC), 0)
    mult = jnp.where(row + (l * TL + pos0) == 0, 1.0, mult)
    b = mult * ig * xc
    n_slab = DC // LANES
    for s in range(n_slab):
        a_scr[s] = a[:, s * LANES:(s + 1) * LANES]
        b_scr[s] = b[:, s * LANES:(s + 1) * LANES]

    seg = TL // N_SEG

    def step(i, carry):
        idx = pl.ds(i, N_SEG, stride=seg) if seg > 1 else pl.ds(0, N_SEG)
        out = []
        for s in range(n_slab):
            hloc, ap = carry[s]
            ai = a_scr[s, idx, :]
            hloc = ai * hloc + b_scr[s, idx, :]
            ap = ap * ai
            b_scr[s, idx, :] = hloc
            a_scr[s, idx, :] = ap
            out.append((hloc, ap))
        return tuple(out)

    lax.fori_loop(0, seg, step,
                  tuple((jnp.zeros((N_SEG, LANES), F32), jnp.ones((N_SEG, LANES), F32)) for _ in range(n_slab)),
                  unroll=min(seg, 8))

    carry = h_scr[...]
    g_act = _gelu(gy_ref[...])
    for j in range(N_SEG):
        rows = slice(j * seg, (j + 1) * seg)
        hloc = jnp.concatenate([b_scr[s, rows, :] for s in range(n_slab)], axis=-1)
        ap = jnp.concatenate([a_scr[s, rows, :] for s in range(n_slab)], axis=-1)
        hj = hloc + ap * carry
        yc_ref[rows, :] = g_act[rows, :] * hj
        carry = hj[seg - 1:seg, :]
    h_scr[...] = carry

    @pl.when(l == n_tiles - 1)
    def _():
        hl_ref[0] = carry


def rglru_mix(xb, gy, n_batch, seq, conv_prev8, h0, pos0, wts):
    t, dc = xb.shape
    TL = 512 if seq % 512 == 0 else seq
    n_tiles = seq // TL

    def full(shape):
        nd = len(shape)
        return pl.BlockSpec(shape, lambda b, l: (0,) * nd)

    yc, ctail, hl = pl.pallas_call(
        functools.partial(_rglru_kernel, TL=TL, DC=dc, pos0=pos0, n_tiles=n_tiles),
        out_shape=[jax.ShapeDtypeStruct((t, dc), F32), jax.ShapeDtypeStruct((n_batch, 8, dc), F32),
                   jax.ShapeDtypeStruct((n_batch, 1, dc), F32)],
        grid=(n_batch, n_tiles),
        in_specs=[pl.BlockSpec((TL, dc), lambda b, l: (b * n_tiles + l, 0)),
                  pl.BlockSpec((TL, dc), lambda b, l: (b * n_tiles + l, 0)),
                  pl.BlockSpec((1, 8, dc), lambda b, l: (b, 0, 0)),
                  pl.BlockSpec((1, 1, dc), lambda b, l: (b, 0, 0)),
                  full((CONV_W, dc)), full((1, dc)), full((dc, 2 * dc)), full((1, 2 * dc)), full((1, dc))],
        out_specs=[pl.BlockSpec((TL, dc), lambda b, l: (b * n_tiles + l, 0)),
                   pl.BlockSpec((1, 8, dc), lambda b, l: (b, 0, 0)),
                   pl.BlockSpec((1, 1, dc), lambda b, l: (b, 0, 0))],
        scratch_shapes=[pltpu.VMEM((TL + 8, dc), F32), pltpu.VMEM((dc // LANES, TL, LANES), F32),
                        pltpu.VMEM((dc // LANES, TL, LANES), F32), pltpu.VMEM((1, dc), F32)],
        compiler_params=_cparams(("parallel", "arbitrary")),
        name="rglru_mix",
    )(xb, gy, conv_prev8, h0.reshape(n_batch, 1, dc), wts["conv_w"], wts["conv_b"], wts["gate_w"], wts["gate_b"],
      wts["lam"])
    return yc, ctail[:, 8 - (CONV_W - 1):, :], hl.reshape(n_batch, dc)


def _t5_bucket(dist):
    dist = np.asarray(dist)
    max_exact = N_BUCKETS // 2
    scaled = np.log(np.maximum(dist, 1) / max_exact) / math.log(BUCKET_MAX_DIST / max_exact)
    large = np.minimum(max_exact + (scaled * (N_BUCKETS - max_exact)).astype(np.int32), N_BUCKETS - 1)
    return np.where(dist < max_exact, dist, large).astype(np.int32)


def _dist_table(rel_bias, max_dist):
    dist = np.arange(max_dist + 1)
    count = np.zeros(max_dist + 1, np.float32)
    for window, dil in DILATED:
        count += ((dist % dil == 0) & (dist <= window)).astype(np.float32)
    logcnt = np.where(count > 0, np.log(np.maximum(count, 1.0)), 0.0).astype(np.float32)
    tab = jnp.take(rel_bias, jnp.asarray(_t5_bucket(dist)), axis=0) + jnp.asarray(logcnt)[:, None]
    return jnp.where(jnp.asarray(count > 0)[:, None], tab, NEG_BIG)


def _toeplitz_tiles(tab, n_pos, n_neg, T):
    D, H = tab.shape
    span = T * n_pos
    assert D >= span
    n_col = span + T * n_neg + T - 1
    ext = jnp.concatenate([jnp.flip(tab[:span], axis=0), jnp.full((n_col + 1 - span, H), NEG_BIG, F32)], axis=0)
    ext = jnp.transpose(ext)
    skew = jnp.tile(ext, (1, T))[:, :T * n_col].reshape(H, T, n_col)
    tiles = [skew[:, :, span - 1 - T * dd: span - 1 - T * dd + T] for dd in range(-n_neg, n_pos)]
    return jnp.stack(tiles, axis=1)


def _attn_prompt_kernel(q_ref, k_ref, v_ref, bias_ref, o_ref, kb_scr, vb_scr, *, E, SUB, NS):
    qi = pl.program_id(2)
    TQ = NS * SUB

    @pl.when(qi == 0)
    def _():
        kb_scr[...] = k_ref[0].astype(BF16)
        vb_scr[...] = v_ref[0].astype(BF16)

    lane = lax.broadcasted_iota(jnp.int32, (SUB, 2 * E), 1)
    q2 = []
    for rs in range(NS):
        q = q_ref[0, rs * SUB:(rs + 1) * SUB, :] * (E ** -0.5 * LOG2E)
        q2.append(jnp.concatenate([jnp.where(lane < E, q, 0.0), jnp.where(lane >= E, q, 0.0)], axis=0).astype(BF16))

    def body(i, carry):
        j = qi - i
        kj = kb_scr[pl.ds(pl.multiple_of(j * TQ, TQ), TQ), :]
        vj = vb_scr[pl.ds(pl.multiple_of(j * TQ, TQ), TQ), :]
        out = []
        for rs in range(NS):
            m, l, acc = carry[rs]
            s = _dot_nt(q2[rs], kj)
            parts = []
            for cs in range(NS):
                dd = i * NS + (rs - cs + NS - 1)
                bias = jnp.concatenate([bias_ref[0, dd], bias_ref[1, dd]], axis=0)
                parts.append(s[:, cs * SUB:(cs + 1) * SUB] + bias)
            mx = parts[0]
            for part in parts[1:]:
                mx = jnp.maximum(mx, part)
            m_new = jnp.maximum(m, jnp.max(mx, axis=-1, keepdims=True))
            alpha = jnp.exp2(m - m_new)
            ps = [jnp.exp2(part - m_new) for part in parts]
            psum = ps[0]
            for pexp in ps[1:]:
                psum = psum + pexp
            l = alpha * l + psum
            acc = alpha * acc + _dot(jnp.concatenate(ps, axis=-1).astype(BF16), vj)
            out.append((m_new, l, acc))
        return tuple(out)

    init = tuple((jnp.full((2 * SUB, SUB), NEG_BIG, F32), jnp.zeros((2 * SUB, SUB), F32),
                  jnp.zeros((2 * SUB, 2 * E), F32)) for _ in range(NS))
    res = lax.fori_loop(0, qi + 1, body, init)
    for rs in range(NS):
        m, l, acc = res[rs]
        o = acc / jnp.sum(l, axis=-1, keepdims=True)
        o_ref[0, rs * SUB:(rs + 1) * SUB, :] = jnp.where(lane < E, o[0:SUB], o[SUB:2 * SUB])


def attn_prompt(q, k, v, bias_tiles, n_batch, seq):
    hd = q.shape[-1]
    E = hd // H_D
    SUB = ATT_TILE
    NS = ATT_SUBTILES
    TQ = SUB * NS
    nq = seq // TQ
    nt = bias_tiles.shape[1]
    return pl.pallas_call(
        functools.partial(_attn_prompt_kernel, E=E, SUB=SUB, NS=NS),
        out_shape=jax.ShapeDtypeStruct((n_batch, seq, hd), F32),
        grid=(H_D // 2, n_batch, nq),
        in_specs=[pl.BlockSpec((1, TQ, 2 * E), lambda hp, b, i: (b, i, hp)),
                  pl.BlockSpec((1, seq, 2 * E), lambda hp, b, i: (b, 0, hp)),
                  pl.BlockSpec((1, seq, 2 * E), lambda hp, b, i: (b, 0, hp)),
                  pl.BlockSpec((2, nt, SUB, SUB), lambda hp, b, i: (hp, 0, 0, 0))],
        out_specs=pl.BlockSpec((1, TQ, 2 * E), lambda hp, b, i: (b, i, hp)),
        scratch_shapes=[pltpu.VMEM((seq, 2 * E), BF16), pltpu.VMEM((seq, 2 * E), BF16)],
        compiler_params=_cparams(("arbitrary", "arbitrary", "arbitrary")),
        name="dilated_attn_prompt",
    )(q, k, v, bias_tiles)


def _attn_sample_kernel(q_ref, kn_ref, vn_ref, ck_ref, cv_ref, bo_ref, bn_ref, o_ref, *, E, S):
    lane = lax.broadcasted_iota(jnp.int32, (S, 2 * E), 1)
    NPAD = bn_ref.shape[-1]
    outs = []
    for hp in range(H_D // 2):
        sl = slice(hp * 2 * E, (hp + 1) * 2 * E)
        q = q_ref[0, :, sl] * (E ** -0.5)
        q2 = jnp.concatenate([jnp.where(lane < E, q, 0.0), jnp.where(lane >= E, q, 0.0)], axis=0).astype(BF16)
        zpad = jnp.zeros((NPAD - S, 2 * E), F32)
        kn = jnp.concatenate([kn_ref[0, :, sl], zpad], axis=0).astype(BF16)
        vn = jnp.concatenate([vn_ref[0, :, sl], zpad], axis=0).astype(BF16)
        s_old = _dot_nt(q2, ck_ref[0, :, sl].astype(BF16)) + jnp.concatenate([bo_ref[2 * hp], bo_ref[2 * hp + 1]], axis=0)
        s_new = _dot_nt(q2, kn) + jnp.concatenate([bn_ref[2 * hp], bn_ref[2 * hp + 1]], axis=0)
        m = jnp.maximum(jnp.max(s_old, axis=-1, keepdims=True), jnp.max(s_new, axis=-1, keepdims=True))
        p_old = jnp.exp(s_old - m)
        p_new = jnp.exp(s_new - m)
        l = jnp.sum(p_old, axis=-1, keepdims=True) + jnp.sum(p_new, axis=-1, keepdims=True)
        acc = _dot(p_old.astype(BF16), cv_ref[0, :, sl].astype(BF16)) + _dot(p_new.astype(BF16), vn)
        o = acc / l
        outs.append(jnp.where(lane < E, o[0:S], o[S:2 * S]))
    o_ref[0] = jnp.concatenate(outs, axis=-1)


def attn_sample(q, k_new, v_new, cache_k, cache_v, bias_old, bias_new):
    n_batch, S, hd = q.shape
    W = cache_k.shape[1]
    E = hd // H_D
    NPAD = bias_new.shape[-1]
    return pl.pallas_call(
        functools.partial(_attn_sample_kernel, E=E, S=S),
        out_shape=jax.ShapeDtypeStruct((n_batch, S, hd), F32),
        grid=(n_batch,),
        in_specs=[pl.BlockSpec((1, S, hd), lambda b: (b, 0, 0)),
                  pl.BlockSpec((1, S, hd), lambda b: (b, 0, 0)),
                  pl.BlockSpec((1, S, hd), lambda b: (b, 0, 0)),
                  pl.BlockSpec((1, W, hd), lambda b: (b, 0, 0)),
                  pl.BlockSpec((1, W, hd), lambda b: (b, 0, 0)),
                  pl.BlockSpec((H_D, S, W), lambda b: (0, 0, 0)),
                  pl.BlockSpec((H_D, S, NPAD), lambda b: (0, 0, 0))],
        out_specs=pl.BlockSpec((1, S, hd), lambda b: (b, 0, 0)),
        compiler_params=_cparams(("parallel",)),
        name="dilated_attn_sample",
    )(q, k_new, v_new, cache_k, cache_v, bias_old, bias_new)


def _even_weights(j, w_in_even, w_out_even, shift_mu, decay_w0, decay_up, iclr_a0, iclr_up, gate_up, key_k, key_a,
                  bonus_r_k, lnx_g, lnx_b, sgu_norm_g, sgu_norm_b, sgu_w, sgu_b):
    da = decay_w0.shape[1]
    zeros_d = jnp.zeros((R_ICLR, da), F32)
    zeros_i = jnp.zeros((R_DECAY, da), F32)
    return dict(
        w_in=w_in_even[j].astype(BF16),
        w_out_a=w_out_even[j, :da].astype(BF16), w_out_b=w_out_even[j, da:].astype(BF16),
        mu=shift_mu[j].reshape(1, -1), w0=decay_w0[j].reshape(1, -1), a0=iclr_a0[j].reshape(1, -1),
        wd=jnp.concatenate([decay_up[j], zeros_d], axis=0), wa=jnp.concatenate([zeros_i, iclr_up[j]], axis=0),
        wg=gate_up[j], key_k=key_k[j].reshape(1, -1), key_a=key_a[j].reshape(1, -1),
        bonus=bonus_r_k[j].reshape(1, -1), lnx_g=lnx_g[j].reshape(1, -1), lnx_b=lnx_b[j].reshape(1, -1),
        ng=sgu_norm_g[j].reshape(1, -1), nb=sgu_norm_b[j].reshape(1, -1), sgu_w=sgu_w[j], sgu_b=sgu_b[j])


def _gmlp_tables(sgu_w, sgu_b, chunk):
    reps = GMLP_TILE // chunk
    n_h = sgu_w.shape[0]
    cb = None
    wm = sgu_w[:, :chunk, :chunk] * jnp.asarray(np.tril(np.ones((chunk, chunk), np.float32)))
    if reps > 1:
        eye = jnp.asarray(np.eye(reps, dtype=np.float32))
        wm = jnp.einsum("ab,hts->hatbs", eye, wm).reshape(n_h, GMLP_TILE, GMLP_TILE)
    bias = jnp.tile(jnp.transpose(sgu_b[:, :chunk]), (reps, 1))
    return wm.astype(BF16), bias


def _even_layer(x, n_batch, seq, chunk, n_par, want_vn, shift_prev, wkv0, norm_g, ew):
    pa, pu, pv = norm_matmul(x, norm_g, ew["w_in"], (ew["mu"].shape[1], ew["ng"].shape[1], ew["ng"].shape[1]))
    ya, wkv = rwkv_mix(pa, n_batch, seq, shift_prev, wkv0, ew, n_par)
    wm, bias = _gmlp_tables(ew["sgu_w"], ew["sgu_b"], chunk)
    cb = pu.shape[1] // wm.shape[0]
    bias_tile = jnp.repeat(bias, cb, axis=1)
    yb, vn = gmlp_mix(pu, pv, ew["ng"], ew["nb"], wm, bias_tile, want_vn)
    x = proj_residual(x, ya, yb, ew["w_out_a"], ew["w_out_b"])
    last = pa.reshape(n_batch, seq, -1)[:, -1]
    return x, last, wkv, vn


def _odd_weights(j, w_in_odd, w_out_odd, conv_w, conv_b, rgate_w, rgate_b, igate_w, igate_b, lru_lambda):
    dc = conv_b.shape[1]
    eye = jnp.asarray(np.eye(H_C, dtype=np.float32))

    def blockdiag(w):
        dh = w.shape[-1]
        return jnp.einsum("ab,aij->aibj", eye, w).reshape(H_C * dh, H_C * dh)

    return dict(
        w_in=w_in_odd[j].astype(BF16),
        w_out_c=w_out_odd[j, :dc].astype(BF16), w_out_d=w_out_odd[j, dc:].astype(BF16),
        conv_w=conv_w[j], conv_b=conv_b[j].reshape(1, -1),
        gate_w=jnp.concatenate([blockdiag(rgate_w[j]), blockdiag(igate_w[j])], axis=1).astype(BF16),
        gate_b=jnp.concatenate([rgate_b[j], igate_b[j]]).reshape(1, -1),
        lam=lru_lambda[j].reshape(1, -1))


def _odd_layer(x, n_batch, seq, conv_prev, h0, pos0, caches, dist_tab, norm_g, ow):
    dc = ow["lam"].shape[1]
    gy, xb, q, k, v = norm_matmul(x, norm_g, ow["w_in"], (dc,) * 5)
    conv_prev8 = jnp.pad(conv_prev, ((0, 0), (8 - (CONV_W - 1), 0), (0, 0)))
    yc, conv_last, h_last = rglru_mix(xb, gy, n_batch, seq, conv_prev8, h0, pos0, ow)
    hd = q.shape[1]
    q3, k3, v3 = (a.reshape(n_batch, seq, hd) for a in (q, k, v))
    if caches is None:
        tiles = _toeplitz_tiles(dist_tab * LOG2E, seq // ATT_TILE, ATT_SUBTILES - 1, ATT_TILE)
        o = attn_prompt(q3, k3, v3, tiles, n_batch, seq)
    else:
        cache_k, cache_v = caches
        W = cache_k.shape[1]
        ck = cache_k.reshape(n_batch, W, hd)
        cv = cache_v.reshape(n_batch, W, hd)
        NPAD = 128
        d_old = W + np.arange(seq)[:, None] - np.arange(W)[None, :]
        b_old = jnp.transpose(jnp.take(dist_tab, jnp.asarray(d_old), axis=0), (2, 0, 1))
        d_new = np.arange(seq)[:, None] - np.arange(NPAD)[None, :]
        ok_new = (d_new >= 0) & (np.arange(NPAD)[None, :] < seq)
        b_new = jnp.take(dist_tab, jnp.asarray(np.maximum(d_new, 0)), axis=0)
        b_new = jnp.transpose(jnp.where(jnp.asarray(ok_new)[..., None], b_new, NEG_BIG), (2, 0, 1))
        o = attn_sample(q3, k3, v3, ck, cv, b_old, b_new)
    x = proj_residual(x, yc, o.reshape(n_batch * seq, hd), ow["w_out_c"], ow["w_out_d"])
    e = hd // H_D
    return x, conv_last, h_last, k3.reshape(n_batch, seq, H_D, e), v3.reshape(n_batch, seq, H_D, e)


def _moe_weights(l, router_group_w, router_group_b, router_expert_w, router_expert_b, exp_w_gate, exp_w_up,
                 exp_w_down):
    d = router_group_w.shape[1]
    n_used = N_GROUPS + router_expert_w.shape[2]
    rw = jnp.concatenate([router_group_w[l], router_expert_w[l], jnp.zeros((d, ROUTER_LANES - n_used), F32)], axis=1)
    rb = jnp.concatenate([router_group_b[l], router_expert_b[l], jnp.zeros((ROUTER_LANES - n_used,), F32)])
    return dict(rw=rw, rb=rb.reshape(1, -1), wg=exp_w_gate[l].astype(BF16), wu=exp_w_up[l].astype(BF16),
                wd=exp_w_down[l].astype(BF16))


def kernel(x_prompt, x_sample, state_wkv, state_shift, state_conv, state_rglru, cache_k, cache_v, norm_mix, norm_ffn, norm_final, w_in_even, w_out_even, shift_mu, decay_w0, decay_up, iclr_a0, iclr_up, gate_up, key_k, key_a, bonus_r_k, lnx_g, lnx_b, sgu_norm_g, sgu_norm_b, sgu_w, sgu_b, w_in_odd, w_out_odd, conv_w, conv_b, rgate_w, rgate_b, igate_w, igate_b, lru_lambda, rel_bias, router_group_w, router_group_b, router_expert_w, router_expert_b, exp_w_gate, exp_w_up, exp_w_down):
    B, L, D = x_prompt.shape
    DB, S, _ = x_sample.shape
    depth = norm_mix.shape[0]
    xp = x_prompt.reshape(B * L, D)
    xs = x_sample.reshape(DB * S, D)
    W = cache_k.shape[2]
    dist_tab = _dist_table(rel_bias, max(L, W + S) - 1)

    wkv_p, shift_p, conv_p, lru_p, k_p, v_p = [], [], [], [], [], []
    wkv_s, shift_s, chunkv_s, conv_s, lru_s, k_s, v_s = [], [], [], [], [], [], []
    for l in range(depth):
        j = l // 2
        if l % 2 == 0:
            ew = _even_weights(j, w_in_even, w_out_even, shift_mu, decay_w0, decay_up, iclr_a0, iclr_up, gate_up,
                               key_k, key_a, bonus_r_k, lnx_g, lnx_b, sgu_norm_g, sgu_norm_b, sgu_w, sgu_b)
            a_proj = ew["mu"].shape[1]
            h_a = state_wkv.shape[2]
            xp, sh, wkv, _ = _even_layer(xp, B, L, GMLP_TILE, RWKV_PAR_PROMPT, False, jnp.zeros((B, a_proj), F32),
                                         jnp.zeros((B, h_a, DH_A, DH_A), F32), norm_mix[l], ew)
            xs, sh_s, wkv_s_new, vn_s = _even_layer(xs, DB, S, S, RWKV_PAR_SAMPLE, True, state_shift[j], state_wkv[j], norm_mix[l], ew)
            wkv_p.append(wkv)
            shift_p.append(sh)
            wkv_s.append(wkv_s_new)
            shift_s.append(sh_s)
            chunkv_s.append(vn_s.reshape(DB, S, -1))
        else:
            ow = _odd_weights(j, w_in_odd, w_out_odd, conv_w, conv_b, rgate_w, rgate_b, igate_w, igate_b, lru_lambda)
            dc = ow["lam"].shape[1]
            xp, cv, hl, kr, vr = _odd_layer(xp, B, L, jnp.zeros((B, CONV_W - 1, dc), F32), jnp.zeros((B, dc), F32),
                                            0, None, dist_tab, norm_mix[l], ow)
            xs, cv_s, hl_s, kr_s, vr_s = _odd_layer(xs, DB, S, state_conv[j], state_rglru[j], PAST_LEN,
                                                    (cache_k[j], cache_v[j]), dist_tab, norm_mix[l], ow)
            conv_p.append(cv)
            lru_p.append(hl)
            k_p.append(kr)
            v_p.append(vr)
            conv_s.append(cv_s)
            lru_s.append(hl_s)
            k_s.append(kr_s)
            v_s.append(vr_s)
        mw = _moe_weights(l, router_group_w, router_group_b, router_expert_w, router_expert_b, exp_w_gate, exp_w_up,
                          exp_w_down)
        xp = moe_layer_sparse(xp, norm_ffn[l], mw["rw"], mw["rb"], mw["wg"], mw["wu"], mw["wd"],
                              final_g=norm_final if l == depth - 1 else None)
        xs = moe_layer(xs, norm_ffn[l], mw["rw"], mw["rb"], mw["wg"], mw["wu"], mw["wd"])
    y_prompt = xp.reshape(B, L, D)
    y_sample = rmsnorm_call(xs, norm_final).reshape(DB, S, D)
    return (y_prompt, y_sample,
            jnp.stack(wkv_p), jnp.stack(shift_p), jnp.stack(conv_p), jnp.stack(lru_p), jnp.stack(k_p), jnp.stack(v_p),
            jnp.stack(wkv_s), jnp.stack(shift_s), jnp.stack(chunkv_s), jnp.stack(conv_s), jnp.stack(lru_s),
            jnp.stack(k_s), jnp.stack(v_s))
```

```python
import functools
import math

import numpy as np
import jax
import jax.numpy as jnp
from jax import lax
from jax.experimental import pallas as pl
from jax.experimental.pallas import tpu as pltpu

F32 = jnp.float32
BF16 = jnp.bfloat16
HI = lax.Precision.HIGHEST

PAST_LEN = 8192
DH_A = 64
R_DECAY = 64
R_ICLR = 64
R_GATE = 128
GN_EPS = 64e-5
H_B = 4
H_C = 8
CONV_W = 4
LRU_C = 8.0
H_D = 8
DILATED = ((128, 1), (512, 4), (2048, 16))
N_BUCKETS = 32
BUCKET_MAX_DIST = 2048
NEG_BIG = -1e30
N_GROUPS = 4
EXP_PER_GROUP = 4
NORM_EPS = 1e-6
LOG2E = math.log2(math.e)

VMEM_LIMIT = 56 * 1024 * 1024
RWKV_CHUNK = 64
RWKV_PAR_PROMPT = 4
RWKV_PAR_SAMPLE = 8
ATT_TILE = 128
ATT_SUBTILES = 4
LANES = 128
SUBLANES = 8


def _cparams(sem):
    return pltpu.CompilerParams(dimension_semantics=sem, vmem_limit_bytes=VMEM_LIMIT)


def _dot(a, b, precision=None):
    return jnp.dot(a, b, preferred_element_type=F32, precision=precision)


def _dot_nt(a, b, precision=None):
    return lax.dot_general(a, b, (((1,), (1,)), ((), ())), preferred_element_type=F32, precision=precision)


def _dot_tn(a, b, precision=None):
    return lax.dot_general(a, b, (((0,), (0,)), ((), ())), preferred_element_type=F32, precision=precision)


def _split_bf16(x, n):
    parts = []
    for _ in range(n):
        hi = x.astype(BF16)
        parts.append(hi)
        x = x - hi.astype(F32)
    return parts


def _mp_dot(dotfn, a, b, passes):
    if passes == 1:
        return dotfn(a.astype(BF16), b.astype(BF16))
    a_hi, a_lo = _split_bf16(a, 2)
    b_hi, b_lo = _split_bf16(b, 2)
    return dotfn(a_hi, b_hi) + (dotfn(a_hi, b_lo) + dotfn(a_lo, b_hi))


def _dot_exact_rhs(a, b_bf16, n_split):
    parts = _split_bf16(a, n_split)
    acc = _dot(parts[0], b_bf16)
    for part in parts[1:]:
        acc = acc + _dot(part, b_bf16)
    return acc


def _dot_exact_lhs(a_bf16, b, n_split):
    parts = _split_bf16(b, n_split)
    acc = _dot(a_bf16, parts[0])
    for part in parts[1:]:
        acc = acc + _dot(a_bf16, part)
    return acc


def _softplus(x):
    return jnp.maximum(x, 0.0) + jnp.log(1.0 + jnp.exp(-jnp.abs(x)))


def _sigmoid(x):
    return 1.0 / (1.0 + jnp.exp(-x))


def _gelu(x):
    c = math.sqrt(2.0 / math.pi)
    return 0.5 * x * (1.0 + jnp.tanh(c * (x + 0.044715 * (x * x * x))))


def _row_tile(t, pref=512):
    return pref if t % pref == 0 else t


def _norm_matmul_kernel(x_ref, g_ref, w_ref, *out_refs, splits):
    x = x_ref[...]
    ms = jnp.mean(x * x, axis=-1, keepdims=True)
    h = (x * lax.rsqrt(ms + NORM_EPS) * g_ref[...]).astype(BF16)
    off = 0
    for o_ref, n in zip(out_refs, splits):
        o_ref[...] = _dot(h, w_ref[:, off:off + n])
        off += n


def norm_matmul(x, g, w_bf16, splits):
    t, d = x.shape
    n = w_bf16.shape[1]
    tm = _row_tile(t)
    return pl.pallas_call(
        functools.partial(_norm_matmul_kernel, splits=splits),
        out_shape=[jax.ShapeDtypeStruct((t, s), F32) for s in splits],
        grid=(t // tm,),
        in_specs=[pl.BlockSpec((tm, d), lambda i: (i, 0)),
                  pl.BlockSpec((1, d), lambda i: (0, 0)),
                  pl.BlockSpec((d, n), lambda i: (0, 0))],
        out_specs=[pl.BlockSpec((tm, s), lambda i: (i, 0)) for s in splits],
        compiler_params=_cparams(("parallel",)),
        name="norm_matmul",
    )(x, g.reshape(1, d), w_bf16)


def _proj_res_kernel(x_ref, a_ref, b_ref, wa_ref, wb_ref, o_ref):
    acc = _dot(a_ref[...].astype(BF16), wa_ref[...]) + _dot(b_ref[...].astype(BF16), wb_ref[...])
    o_ref[...] = x_ref[...] + acc


def proj_residual(x, a, b, wa, wb):
    t, d = x.shape
    tm = _row_tile(t)
    ka, kb = a.shape[1], b.shape[1]
    return pl.pallas_call(
        _proj_res_kernel,
        out_shape=jax.ShapeDtypeStruct((t, d), F32),
        grid=(t // tm,),
        in_specs=[pl.BlockSpec((tm, d), lambda i: (i, 0)),
                  pl.BlockSpec((tm, ka), lambda i: (i, 0)),
                  pl.BlockSpec((tm, kb), lambda i: (i, 0)),
                  pl.BlockSpec((ka, d), lambda i: (0, 0)),
                  pl.BlockSpec((kb, d), lambda i: (0, 0))],
        out_specs=pl.BlockSpec((tm, d), lambda i: (i, 0)),
        compiler_params=_cparams(("parallel",)),
        name="proj_residual",
    )(x, a, b, wa, wb)


def _rmsnorm_kernel(x_ref, g_ref, o_ref):
    x = x_ref[...]
    ms = jnp.mean(x * x, axis=-1, keepdims=True)
    o_ref[...] = x * lax.rsqrt(ms + NORM_EPS) * g_ref[...]


def rmsnorm_call(x, g):
    t, d = x.shape
    tm = _row_tile(t)
    return pl.pallas_call(
        _rmsnorm_kernel,
        out_shape=jax.ShapeDtypeStruct((t, d), F32),
        grid=(t // tm,),
        in_specs=[pl.BlockSpec((tm, d), lambda i: (i, 0)), pl.BlockSpec((1, d), lambda i: (0, 0))],
        out_specs=pl.BlockSpec((tm, d), lambda i: (i, 0)),
        compiler_params=_cparams(("parallel",)),
        name="final_rmsnorm",
    )(x, g.reshape(1, d))


ROUTER_LANES = 128


def _route(xn, rw, rb, lane, n_exp):
    logits = _mp_dot(_dot, xn, rw, 3) + rb
    lg = jnp.where(lane < N_GROUPS, logits, -jnp.inf)
    gm = jnp.max(lg, axis=-1, keepdims=True)
    top_pg = 1.0 / jnp.sum(jnp.exp(lg - gm), axis=-1, keepdims=True)
    grp = jnp.min(jnp.where(lg == gm, lane, ROUTER_LANES), axis=-1, keepdims=True)
    in_grp = (lane >= N_GROUPS) & (lane < N_GROUPS + n_exp) & (((lane - N_GROUPS) // EXP_PER_GROUP) == grp)
    le = jnp.where(in_grp, logits, -jnp.inf)
    t1 = jnp.max(le, axis=-1, keepdims=True)
    i1 = jnp.min(jnp.where(le == t1, lane, ROUTER_LANES), axis=-1, keepdims=True)
    le2 = jnp.where(lane == i1, -jnp.inf, le)
    t2 = jnp.max(le2, axis=-1, keepdims=True)
    i2 = jnp.min(jnp.where(le2 == t2, lane, ROUTER_LANES), axis=-1, keepdims=True)
    ex = jnp.exp(t2 - t1)
    w1 = 1.0 / (1.0 + ex)
    return i1, i2, w1 * top_pg, (ex * w1) * top_pg


def _moe_kernel(x_ref, g_ref, rw_ref, rb_ref, wg_ref, wu_ref, wd_ref, o_ref, xn_scr, gate_scr, acc_scr, *, n_exp):
    e = pl.program_id(1)
    tm = x_ref.shape[0]
    lane = lax.broadcasted_iota(jnp.int32, (tm, ROUTER_LANES), 1)

    @pl.when(e == 0)
    def _():
        x = x_ref[...]
        ms = jnp.mean(x * x, axis=-1, keepdims=True)
        xn = x * lax.rsqrt(ms + NORM_EPS) * g_ref[...]
        xn_scr[...] = xn.astype(BF16)
        i1, i2, g1, g2 = _route(xn, rw_ref[...], rb_ref[...], lane, n_exp)
        gate_scr[...] = jnp.where(lane == i1, g1, 0.0) + jnp.where(lane == i2, g2, 0.0)
        acc_scr[...] = jnp.zeros_like(acc_scr)

    xn = xn_scr[...]
    hg = _dot(xn, wg_ref[0])
    hu = _dot(xn, wu_ref[0])
    gcol = jnp.sum(jnp.where(lane == e + N_GROUPS, gate_scr[...], 0.0), axis=-1, keepdims=True)
    hid = hg * _sigmoid(hg) * hu * gcol
    acc_scr[...] += _dot(hid.astype(BF16), wd_ref[0])

    @pl.when(e == n_exp - 1)
    def _():
        o_ref[...] = x_ref[...] + acc_scr[...]


def moe_layer(x, g, rw, rb, wg, wu, wd):
    t, d = x.shape
    n_exp, _, f = wg.shape
    tm = _row_tile(t)
    return pl.pallas_call(
        functools.partial(_moe_kernel, n_exp=n_exp),
        out_shape=jax.ShapeDtypeStruct((t, d), F32),
        grid=(t // tm, n_exp),
        in_specs=[pl.BlockSpec((tm, d), lambda i, e: (i, 0)),
                  pl.BlockSpec((1, d), lambda i, e: (0, 0)),
                  pl.BlockSpec((d, ROUTER_LANES), lambda i, e: (0, 0)),
                  pl.BlockSpec((1, ROUTER_LANES), lambda i, e: (0, 0)),
                  pl.BlockSpec((1, d, f), lambda i, e: (e, 0, 0)),
                  pl.BlockSpec((1, d, f), lambda i, e: (e, 0, 0)),
                  pl.BlockSpec((1, f, d), lambda i, e: (e, 0, 0))],
        out_specs=pl.BlockSpec((tm, d), lambda i, e: (i, 0)),
        scratch_shapes=[pltpu.VMEM((tm, d), BF16), pltpu.VMEM((tm, ROUTER_LANES), F32), pltpu.VMEM((tm, d), F32)],
        compiler_params=_cparams(("parallel", "arbitrary")),
        name="hier_moe",
    )(x, g.reshape(1, d), rw, rb, wg, wu, wd)


MOE_ROW_TILE = 512
MOE_COPY_CHUNK = 256
MOE_COMBINE_TILE = 256


def _router_kernel(x_ref, g_ref, rw_ref, rb_ref, tri_ref, gate_ref, info_ref, cnt_ref, base_scr, *, n_exp, n_tiles):
    i = pl.program_id(0)
    tm = x_ref.shape[0]
    lane = lax.broadcasted_iota(jnp.int32, (tm, ROUTER_LANES), 1)

    @pl.when(i == 0)
    def _():
        base_scr[...] = jnp.zeros_like(base_scr)

    x = x_ref[...]
    ms = jnp.mean(x * x, axis=-1, keepdims=True)
    xn = x * lax.rsqrt(ms + NORM_EPS) * g_ref[...]
    i1, i2, g1, g2 = _route(xn, rw_ref[...], rb_ref[...], lane, n_exp)
    chosen = jnp.where((lane == i1) | (lane == i2), 1.0, 0.0)
    before = _dot(tri_ref[...], chosen.astype(BF16)) + base_scr[...]
    r1 = jnp.sum(jnp.where(lane == i1, before, 0.0), axis=-1, keepdims=True)
    r2 = jnp.sum(jnp.where(lane == i2, before, 0.0), axis=-1, keepdims=True)
    base_scr[...] += jnp.sum(chosen, axis=0, keepdims=True)
    gate_ref[...] = jnp.where(lane == 0, g1, 0.0) + jnp.where(lane == 1, g2, 0.0)
    e1 = (i1 - N_GROUPS).astype(F32)
    e2 = (i2 - N_GROUPS).astype(F32)
    info_ref[...] = (jnp.where(lane == 0, e1, 0.0) + jnp.where(lane == 1, e2, 0.0)
                     + jnp.where(lane == 2, r1, 0.0) + jnp.where(lane == 3, r2, 0.0))

    @pl.when(i == n_tiles - 1)
    def _():
        cnt_ref[...] = base_scr[...]


def _scatter_rows_kernel(pos0_ref, pos1_ref, x_ref, xs_in_hbm, xs_hbm, stage, sem, *, CH, n_chunks):
    del xs_in_hbm
    c = pl.program_id(0)
    slot = c % 2
    x = x_ref[...]
    for j in range(SUBLANES):
        stage[slot, pl.ds(j, CH, stride=SUBLANES), :] = x[:, j * LANES:(j + 1) * LANES]

    def body(r, carry):
        t = c * CH + r
        src = stage.at[slot, pl.ds(pl.multiple_of(r * SUBLANES, SUBLANES), SUBLANES), :]
        d0 = pl.multiple_of(pos0_ref[t], SUBLANES)
        d1 = pl.multiple_of(pos1_ref[t], SUBLANES)
        pltpu.make_async_copy(src, xs_hbm.at[pl.ds(d0, SUBLANES), :], sem.at[slot]).start(priority=0)
        pltpu.make_async_copy(src, xs_hbm.at[pl.ds(d1, SUBLANES), :], sem.at[slot]).start(priority=1)
        return carry

    lax.fori_loop(0, CH, body, 0, unroll=8)

    def drain(s):
        pltpu.make_async_copy(stage.at[s], xs_hbm.at[pl.ds(0, CH * SUBLANES), :], sem.at[s]).wait()
        pltpu.make_async_copy(stage.at[s], xs_hbm.at[pl.ds(0, CH * SUBLANES), :], sem.at[s]).wait()

    @pl.when(c > 0)
    def _():
        drain(1 - slot)

    @pl.when(c == n_chunks - 1)
    def _():
        drain(slot)


def _tile_rows_to_matrix(ref, lead, n_rows):
    return jnp.concatenate([ref[lead + (pl.ds(j, n_rows, stride=SUBLANES), slice(None))] for j in range(SUBLANES)],
                           axis=-1)


def _expert_kernel(te_ref, nv_ref, xs_ref, g_ref, wg_ref, wu_ref, wd_ref, y_ref, *, TM):
    @pl.when(pl.program_id(0) < nv_ref[0])
    def _():
        x = _tile_rows_to_matrix(xs_ref, (), TM)
        ms = jnp.mean(x * x, axis=-1, keepdims=True)
        xn = (x * lax.rsqrt(ms + NORM_EPS) * g_ref[...]).astype(BF16)
        hg = _dot(xn, wg_ref[0])
        hu = _dot(xn, wu_ref[0])
        hid = hg * _sigmoid(hg) * hu
        y = _dot(hid.astype(BF16), wd_ref[0])
        for j in range(SUBLANES):
            y_ref[pl.ds(j, TM, stride=SUBLANES), :] = y[:, j * LANES:(j + 1) * LANES]

    @pl.when(pl.program_id(0) >= nv_ref[0])
    def _():
        y_ref[...] = jnp.zeros_like(y_ref)


def _combine_kernel(pos0_ref, pos1_ref, x_ref, gate_ref, fg_ref, y_hbm, o_ref, ybuf, sem, *, TC, n_tiles, final_norm):
    i = pl.program_id(0)

    def issue(tile, slot):
        def body(r, carry):
            t = tile * TC + r
            dst = pl.ds(pl.multiple_of(r * SUBLANES, SUBLANES), SUBLANES)
            s0 = pl.multiple_of(pos0_ref[t], SUBLANES)
            s1 = pl.multiple_of(pos1_ref[t], SUBLANES)
            pltpu.make_async_copy(y_hbm.at[pl.ds(s0, SUBLANES), :], ybuf.at[slot, 0, dst, :],
                                  sem.at[slot]).start(priority=0)
            pltpu.make_async_copy(y_hbm.at[pl.ds(s1, SUBLANES), :], ybuf.at[slot, 1, dst, :],
                                  sem.at[slot]).start(priority=1)
            return carry
        lax.fori_loop(0, TC, body, 0, unroll=8)

    @pl.when(i == 0)
    def _():
        issue(0, 0)

    @pl.when(i + 1 < n_tiles)
    def _():
        issue(i + 1, (i + 1) % 2)

    slot = i % 2
    pltpu.make_async_copy(y_hbm.at[pl.ds(0, TC * SUBLANES), :], ybuf.at[slot, 0], sem.at[slot]).wait()
    pltpu.make_async_copy(y_hbm.at[pl.ds(0, TC * SUBLANES), :], ybuf.at[slot, 1], sem.at[slot]).wait()
    gate = gate_ref[...]
    y0 = _tile_rows_to_matrix(ybuf, (slot, 0), TC)
    y1 = _tile_rows_to_matrix(ybuf, (slot, 1), TC)
    out = x_ref[...] + gate[:, 0:1] * y0 + gate[:, 1:2] * y1
    if final_norm:
        ms = jnp.mean(out * out, axis=-1, keepdims=True)
        out = out * lax.rsqrt(ms + NORM_EPS) * fg_ref[...]
    o_ref[...] = out


def moe_layer_sparse(x, g, rw, rb, wg, wu, wd, final_g=None, sorted_buf=None):
    t, d = x.shape
    n_exp, _, f = wg.shape
    TM = MOE_ROW_TILE
    n_tiles = t // TM
    tri = jnp.asarray(np.tril(np.ones((TM, TM), np.float32), -1)).astype(BF16)
    gate, info, cnt = pl.pallas_call(
        functools.partial(_router_kernel, n_exp=n_exp, n_tiles=n_tiles),
        out_shape=[jax.ShapeDtypeStruct((t, ROUTER_LANES), F32), jax.ShapeDtypeStruct((t, ROUTER_LANES), F32),
                   jax.ShapeDtypeStruct((1, ROUTER_LANES), F32)],
        grid=(n_tiles,),
        in_specs=[pl.BlockSpec((TM, d), lambda i: (i, 0)),
                  pl.BlockSpec((1, d), lambda i: (0, 0)),
                  pl.BlockSpec((d, ROUTER_LANES), lambda i: (0, 0)),
                  pl.BlockSpec((1, ROUTER_LANES), lambda i: (0, 0)),
                  pl.BlockSpec((TM, TM), lambda i: (0, 0))],
        out_specs=[pl.BlockSpec((TM, ROUTER_LANES), lambda i: (i, 0)),
                   pl.BlockSpec((TM, ROUTER_LANES), lambda i: (i, 0)),
                   pl.BlockSpec((1, ROUTER_LANES), lambda i: (0, 0))],
        scratch_shapes=[pltpu.VMEM((1, ROUTER_LANES), F32)],
        compiler_params=_cparams(("arbitrary",)),
        name="moe_router",
    )(x, g.reshape(1, d), rw, rb, tri)

    counts = cnt[0, N_GROUPS:N_GROUPS + n_exp].astype(jnp.int32)
    padded = ((counts + TM - 1) // TM) * TM
    ends = jnp.cumsum(padded)
    offs = ends - padded
    eid = info[:, 0:2].astype(jnp.int32)
    rank = info[:, 2:4].astype(jnp.int32)
    pos = jnp.sum(jnp.where(eid[:, :, None] == jnp.arange(n_exp)[None, None, :], offs[None, None, :], 0), axis=-1) + rank
    assert d == SUBLANES * LANES, "a token row must fill exactly one (8, 128) tile"
    pos = pos * SUBLANES
    pos0, pos1 = pos[:, 0], pos[:, 1]
    max_tiles = (2 * t) // TM + n_exp
    n_valid = (ends[-1] // TM).astype(jnp.int32).reshape(1)
    tile_exp = jnp.minimum(jnp.sum((ends[None, :] // TM) <= jnp.arange(max_tiles)[:, None], axis=-1),
                           n_exp - 1).astype(jnp.int32)
    p_rows = max_tiles * TM

    CH = MOE_COPY_CHUNK
    xs = pl.pallas_call(
        functools.partial(_scatter_rows_kernel, CH=CH, n_chunks=t // CH),
        out_shape=jax.ShapeDtypeStruct((p_rows * SUBLANES, LANES), F32),
        grid_spec=pltpu.PrefetchScalarGridSpec(
            num_scalar_prefetch=2, grid=(t // CH,),
            in_specs=[pl.BlockSpec((CH, d), lambda c, p0, p1: (c, 0)), pl.BlockSpec(memory_space=pl.ANY)],
            out_specs=pl.BlockSpec(memory_space=pl.ANY),
            scratch_shapes=[pltpu.VMEM((2, CH * SUBLANES, LANES), F32), pltpu.SemaphoreType.DMA((2,))]),
        input_output_aliases={3: 0},
        compiler_params=pltpu.CompilerParams(dimension_semantics=("arbitrary",), vmem_limit_bytes=VMEM_LIMIT,
                                             has_side_effects=True),
        name="moe_scatter_rows",
    )(pos0, pos1, x, jnp.zeros((p_rows * SUBLANES, LANES), F32) if sorted_buf is None else sorted_buf)

    def row_idx(i, te, nv):
        return (jnp.minimum(i, nv[0] - 1), 0)

    ys = pl.pallas_call(
        functools.partial(_expert_kernel, TM=TM),
        out_shape=jax.ShapeDtypeStruct((p_rows * SUBLANES, LANES), F32),
        grid_spec=pltpu.PrefetchScalarGridSpec(
            num_scalar_prefetch=2, grid=(max_tiles,),
            in_specs=[pl.BlockSpec((TM * SUBLANES, LANES), row_idx),
                      pl.BlockSpec((1, d), lambda i, te, nv: (0, 0)),
                      pl.BlockSpec((1, d, f), lambda i, te, nv: (te[i], 0, 0)),
                      pl.BlockSpec((1, d, f), lambda i, te, nv: (te[i], 0, 0)),
                      pl.BlockSpec((1, f, d), lambda i, te, nv: (te[i], 0, 0))],
            out_specs=pl.BlockSpec((TM * SUBLANES, LANES), lambda i, te, nv: (i, 0))),
        compiler_params=_cparams(("arbitrary",)),
        name="moe_experts",
    )(tile_exp, n_valid, xs, g.reshape(1, d), wg, wu, wd)

    TC = MOE_COMBINE_TILE
    out = pl.pallas_call(
        functools.partial(_combine_kernel, TC=TC, n_tiles=t // TC, final_norm=final_g is not None),
        out_shape=jax.ShapeDtypeStruct((t, d), F32),
        grid_spec=pltpu.PrefetchScalarGridSpec(
            num_scalar_prefetch=2, grid=(t // TC,),
            in_specs=[pl.BlockSpec((TC, d), lambda i, p0, p1: (i, 0)),
                      pl.BlockSpec((TC, ROUTER_LANES), lambda i, p0, p1: (i, 0)),
                      pl.BlockSpec((1, d), lambda i, p0, p1: (0, 0)),
                      pl.BlockSpec(memory_space=pl.ANY)],
            out_specs=pl.BlockSpec((TC, d), lambda i, p0, p1: (i, 0)),
            scratch_shapes=[pltpu.VMEM((2, 2, TC * SUBLANES, LANES), F32), pltpu.SemaphoreType.DMA((2,))]),
        compiler_params=_cparams(("arbitrary",)),
        name="moe_combine",
    )(pos0, pos1, x, gate, (g if final_g is None else final_g).reshape(1, d), ys)
    return out, xs


def _rwkv_kernel(p_ref, prev_ref, s0_ref, mu_ref, w0_ref, wd_ref, a0_ref, wa_ref, wg_ref, kk_ref, ka_ref,
                 bonus_ref, lng_ref, lnb_ref, tri_ref, hsum_ref, ya_ref, sf_ref, s_scr, prev_scr,
                 *, NB, C, H, DH, n_chunks):
    c = pl.program_id(1)

    @pl.when(c == 0)
    def _():
        s_scr[...] = s0_ref[:, 0]
        prev_scr[...] = prev_ref[:, 0]

    DA = H * DH
    R = NB * C
    p = p_ref[...].reshape(R, p_ref.shape[-1])
    row = lax.broadcasted_iota(jnp.int32, p.shape, 0)
    shifted = pltpu.roll(p, 1, axis=0)
    for n in range(NB):
        shifted = jnp.where(row == n * C, prev_scr[n], shifted)
        prev_scr[n] = p[(n + 1) * C - 1:(n + 1) * C, :]
    xs = p + (shifted - p) * mu_ref[...]
    r = xs[:, 0:DA]
    k = xs[:, DA:2 * DA]
    v = xs[:, 2 * DA:3 * DA]
    lora = xs[:, 3 * DA:3 * DA + R_DECAY + R_ICLR]
    gd = xs[:, 3 * DA + R_DECAY + R_ICLR:3 * DA + R_DECAY + R_ICLR + R_GATE]

    w_log = -_softplus(-(w0_ref[...] + _mp_dot(_dot, jnp.tanh(lora), wd_ref[...], 3))) - 0.5
    lw = -jnp.exp(w_log)
    a = _sigmoid(a0_ref[...] + _mp_dot(_dot, lora, wa_ref[...], 3))
    g = _mp_dot(_dot, _sigmoid(gd), wg_ref[...], 3)

    kk = k * kk_ref[...]
    ss = _dot_exact_rhs(kk * kk, hsum_ref[...], 2)
    kk = kk / jnp.maximum(jnp.sqrt(ss), 1e-12)
    k2 = k * (1.0 + (a - 1.0) * ka_ref[...])
    kka = kk * a

    cum = jnp.concatenate([_dot_exact_lhs(tri_ref[...], lw[n * C:(n + 1) * C], 3) for n in range(NB)], axis=0)
    p_in = jnp.exp(cum)
    r_t = r * p_in
    a_t = kk * jnp.exp(cum - lw)
    p_inv = jnp.exp(-cum)
    b_t = kka * p_inv
    k_t = k2 * p_inv
    bonus = _dot_exact_rhs(r * k2 * bonus_ref[...], hsum_ref[...], 2) * v

    ri = lax.broadcasted_iota(jnp.int32, (C, C), 0)
    ci = lax.broadcasted_iota(jnp.int32, (C, C), 1)
    strict = ri > ci
    incl = ri >= ci
    eye = (ri == ci).astype(F32)
    n_double = max(int(math.ceil(math.log2(C))) - 1, 0)

    chains = [(n, h) for n in range(NB) for h in range(H)]

    def blk(x, n, h):
        return x[n * C:(n + 1) * C, h * DH:(h + 1) * DH]

    def bf(x):
        return x.astype(BF16)

    Bt = [bf(blk(b_t, n, h)) for n, h in chains]
    Kt = [bf(blk(k_t, n, h)) for n, h in chains]
    Vf = [blk(v, n, h) for n, h in chains]
    AR = [bf(jnp.concatenate([blk(a_t, n, h), blk(r_t, n, h)], axis=0)) for n, h in chains]
    S0 = [s_scr[n, h] for n, h in chains]
    idx = range(len(chains))
    GB = [_dot_nt(AR[i], Bt[i]) for i in idx]
    GK = [_dot_nt(AR[i], Kt[i]) for i in idx]
    ARS = [_dot_nt(AR[i], bf(S0[i])) for i in idx]
    Lm = [jnp.where(strict, GB[i][0:C], 0.0) for i in idx]
    Gb = [bf(jnp.where(incl, GB[i][C:2 * C], 0.0)) for i in idx]
    MG = [bf(jnp.concatenate([jnp.where(strict, GK[i][0:C], 0.0), jnp.where(incl, GK[i][C:2 * C], 0.0)], axis=0))
          for i in idx]
    MGV = [_dot(MG[i], bf(Vf[i])) for i in idx]
    T = [eye - Lm[i] for i in idx]
    Pw = [bf(Lm[i]) for i in idx]
    for _ in range(n_double):
        Pw = [bf(_dot(Pw[i], Pw[i])) for i in idx]
        T = [T[i] + _dot(bf(T[i]), Pw[i]) for i in idx]
    U = [_dot(bf(T[i]), bf(-(ARS[i][0:C] + MGV[i][0:C]))) for i in idx]
    Y = [ARS[i][C:2 * C] + _dot(Gb[i], bf(U[i])) + MGV[i][C:2 * C] for i in idx]
    for i, (n, h) in enumerate(chains):
        UV = bf(jnp.concatenate([U[i], Vf[i]], axis=0))
        BK = jnp.concatenate([Bt[i], Kt[i]], axis=0)
        p_tot = p_in[(n + 1) * C - 1:(n + 1) * C, h * DH:(h + 1) * DH]
        s_scr[n, h] = (S0[i] + _dot_tn(UV, BK)) * p_tot

    rows = []
    for n in range(NB):
        ys = []
        for h in range(H):
            Yh = Y[n * H + h]
            yc = Yh - jnp.mean(Yh, axis=-1, keepdims=True)
            var = jnp.mean(yc * yc, axis=-1, keepdims=True)
            ys.append(yc * lax.rsqrt(var + GN_EPS))
        rows.append(jnp.concatenate(ys, axis=-1))
    y = jnp.concatenate(rows, axis=0) * lng_ref[...] + lnb_ref[...]
    ya_ref[...] = ((y + bonus) * g).reshape(NB, C, DA)

    @pl.when(c == n_chunks - 1)
    def _():
        sf_ref[:, 0] = s_scr[...]


def rwkv_mix(pa, n_batch, seq, shift_prev, wkv0, wts, n_par):
    t, ap = pa.shape
    H = wkv0.shape[1]
    DA = H * DH_A
    C = min(RWKV_CHUNK, seq)
    n_chunks = seq // C
    NB = n_par
    G = n_batch // NB
    tri = jnp.asarray(np.tril(np.ones((C, C), np.float32))).astype(BF16)
    hsum = jnp.asarray(np.kron(np.eye(H, dtype=np.float32), np.ones((DH_A, DH_A), np.float32))).astype(BF16)

    def full(shape):
        nd = len(shape)
        return pl.BlockSpec(shape, lambda b, c: (0,) * nd)

    vec = full((1, DA))
    ya, s_fin = pl.pallas_call(
        functools.partial(_rwkv_kernel, NB=NB, C=C, H=H, DH=DH_A, n_chunks=n_chunks),
        out_shape=[jax.ShapeDtypeStruct((NB, t // NB, DA), F32),
                   jax.ShapeDtypeStruct((NB, G, H, DH_A, DH_A), F32)],
        grid=(G, n_chunks),
        in_specs=[pl.BlockSpec((NB, C, ap), lambda b, c: (0, b * n_chunks + c, 0)),
                  pl.BlockSpec((NB, 1, 1, ap), lambda b, c: (0, b, 0, 0)),
                  pl.BlockSpec((NB, 1, H, DH_A, DH_A), lambda b, c: (0, b, 0, 0, 0)),
                  full((1, ap)), vec, full((R_DECAY + R_ICLR, DA)), vec, full((R_DECAY + R_ICLR, DA)),
                  full((R_GATE, DA)), vec, vec, vec, vec, vec, full((C, C)), full((DA, DA))],
        out_specs=[pl.BlockSpec((NB, C, DA), lambda b, c: (0, b * n_chunks + c, 0)),
                   pl.BlockSpec((NB, 1, H, DH_A, DH_A), lambda b, c: (0, b, 0, 0, 0))],
        scratch_shapes=[pltpu.VMEM((NB, H, DH_A, DH_A), F32), pltpu.VMEM((NB, 1, ap), F32)],
        compiler_params=_cparams(("parallel", "arbitrary")),
        name="rwkv7_mix",
    )(pa.reshape(NB, t // NB, ap), shift_prev.reshape(NB, G, 1, ap), wkv0.reshape(NB, G, H, DH_A, DH_A),
      wts["mu"], wts["w0"], wts["wd"], wts["a0"], wts["wa"],
      wts["wg"], wts["key_k"], wts["key_a"], wts["bonus"], wts["lnx_g"], wts["lnx_b"], tri, hsum)
    return ya.reshape(t, DA), s_fin.reshape(wkv0.shape)


GMLP_TILE = 128
GMLP_ROWS = 512


def _gmlp_kernel(u_ref, v_ref, ng_ref, nb_ref, wm_ref, bias_ref, o_ref, *vn_refs, n_sub):
    vf = _gelu(v_ref[...])
    mu = jnp.mean(vf, axis=-1, keepdims=True)
    vc = vf - mu
    var = jnp.mean(vc * vc, axis=-1, keepdims=True)
    vn = vc * lax.rsqrt(var + NORM_EPS) * ng_ref[...] + nb_ref[...]
    for vn_ref in vn_refs:
        vn_ref[...] = vn
    vb = vn.astype(BF16)
    n_h = wm_ref.shape[0]
    cb = vn.shape[1] // n_h
    gu = _gelu(u_ref[...])
    for c in range(n_sub):
        rows = slice(c * GMLP_TILE, (c + 1) * GMLP_TILE)
        s = jnp.concatenate([_dot(wm_ref[h], vb[rows, h * cb:(h + 1) * cb]) for h in range(n_h)], axis=-1)
        o_ref[rows, :] = gu[rows, :] * (s + bias_ref[...])


def gmlp_mix(pu, pv, ng, nb, wm_bf16, bias_tile, want_vn):
    t, db = pu.shape
    n_h = wm_bf16.shape[0]
    rows = GMLP_ROWS if t % GMLP_ROWS == 0 else t
    n_out = 2 if want_vn else 1
    outs = pl.pallas_call(
        functools.partial(_gmlp_kernel, n_sub=rows // GMLP_TILE),
        out_shape=[jax.ShapeDtypeStruct((t, db), F32)] * n_out,
        grid=(t // rows,),
        in_specs=[pl.BlockSpec((rows, db), lambda i: (i, 0)),
                  pl.BlockSpec((rows, db), lambda i: (i, 0)),
                  pl.BlockSpec((1, db), lambda i: (0, 0)),
                  pl.BlockSpec((1, db), lambda i: (0, 0)),
                  pl.BlockSpec((n_h, GMLP_TILE, GMLP_TILE), lambda i: (0, 0, 0)),
                  pl.BlockSpec((GMLP_TILE, db), lambda i: (0, 0))],
        out_specs=[pl.BlockSpec((rows, db), lambda i: (i, 0))] * n_out,
        compiler_params=_cparams(("parallel",)),
        name="gmlp_mix",
    )(pu, pv, ng, nb, wm_bf16, bias_tile)
    return (outs[0], outs[1]) if want_vn else (outs[0], None)


N_SEG = 8


def _rglru_kernel(xb_ref, gy_ref, cprev_ref, h0_ref, cw_ref, cb_ref, gw_ref, gb_ref, lam_ref,
                  yc_ref, ctail_ref, hl_ref, xe_scr, a_scr, b_scr, h_scr, *, TL, DC, pos0, n_tiles):
    l = pl.program_id(1)
    PAD = 8

    @pl.when(l == 0)
    def _():
        xe_scr[0:PAD, :] = cprev_ref[0]
        h_scr[...] = h0_ref[0]

    xe_scr[PAD:PAD + TL, :] = xb_ref[...]
    xc = cb_ref[...] + xe_scr[pl.ds(PAD - (CONV_W - 1), TL), :] * cw_ref[0:1, :]
    for i in range(1, CONV_W):
        xc = xc + xe_scr[pl.ds(PAD - (CONV_W - 1) + i, TL), :] * cw_ref[i:i + 1, :]
    tail = xe_scr[TL:TL + PAD, :]
    ctail_ref[0] = tail
    xe_scr[0:PAD, :] = tail

    gates = _dot(xc.astype(BF16), gw_ref[...]) + gb_ref[...]
    rg = _sigmoid(gates[:, 0:DC])
    ig = _sigmoid(gates[:, DC:2 * DC])
    log_a = -LRU_C * rg * _softplus(-lam_ref[...])
    a = jnp.exp(log_a)
    mult = jnp.sqrt(1.0 - a * a)
    row = lax.broadcasted_iota(jnp.int32, (TL, DC), 0)
    mult = jnp.where(row + (l * TL + pos0) == 0, 1.0, mult)
    b = mult * ig * xc
    n_slab = DC // LANES
    for s in range(n_slab):
        a_scr[s] = a[:, s * LANES:(s + 1) * LANES]
        b_scr[s] = b[:, s * LANES:(s + 1) * LANES]

    seg = TL // N_SEG

    def step(i, carry):
        idx = pl.ds(i, N_SEG, stride=seg) if seg > 1 else pl.ds(0, N_SEG)
        out = []
        for s in range(n_slab):
            hloc, ap = carry[s]
            ai = a_scr[s, idx, :]
            hloc = ai * hloc + b_scr[s, idx, :]
            ap = ap * ai
            b_scr[s, idx, :] = hloc
            a_scr[s, idx, :] = ap
            out.append((hloc, ap))
        return tuple(out)

    lax.fori_loop(0, seg, step,
                  tuple((jnp.zeros((N_SEG, LANES), F32), jnp.ones((N_SEG, LANES), F32)) for _ in range(n_slab)),
                  unroll=min(seg, 8))

    carry = h_scr[...]
    g_act = _gelu(gy_ref[...])
    for j in range(N_SEG):
        rows = slice(j * seg, (j + 1) * seg)
        hloc = jnp.concatenate([b_scr[s, rows, :] for s in range(n_slab)], axis=-1)
        ap = jnp.concatenate([a_scr[s, rows, :] for s in range(n_slab)], axis=-1)
        hj = hloc + ap * carry
        yc_ref[rows, :] = g_act[rows, :] * hj
        carry = hj[seg - 1:seg, :]
    h_scr[...] = carry

    @pl.when(l == n_tiles - 1)
    def _():
        hl_ref[0] = carry


def rglru_mix(xb, gy, n_batch, seq, conv_prev8, h0, pos0, wts):
    t, dc = xb.shape
    TL = 512 if seq % 512 == 0 else seq
    n_tiles = seq // TL

    def full(shape):
        nd = len(shape)
        return pl.BlockSpec(shape, lambda b, l: (0,) * nd)

    yc, ctail, hl = pl.pallas_call(
        functools.partial(_rglru_kernel, TL=TL, DC=dc, pos0=pos0, n_tiles=n_tiles),
        out_shape=[jax.ShapeDtypeStruct((t, dc), F32), jax.ShapeDtypeStruct((n_batch, 8, dc), F32),
                   jax.ShapeDtypeStruct((n_batch, 1, dc), F32)],
        grid=(n_batch, n_tiles),
        in_specs=[pl.BlockSpec((TL, dc), lambda b, l: (b * n_tiles + l, 0)),
                  pl.BlockSpec((TL, dc), lambda b, l: (b * n_tiles + l, 0)),
                  pl.BlockSpec((1, 8, dc), lambda b, l: (b, 0, 0)),
                  pl.BlockSpec((1, 1, dc), lambda b, l: (b, 0, 0)),
                  full((CONV_W, dc)), full((1, dc)), full((dc, 2 * dc)), full((1, 2 * dc)), full((1, dc))],
        out_specs=[pl.BlockSpec((TL, dc), lambda b, l: (b * n_tiles + l, 0)),
                   pl.BlockSpec((1, 8, dc), lambda b, l: (b, 0, 0)),
                   pl.BlockSpec((1, 1, dc), lambda b, l: (b, 0, 0))],
        scratch_shapes=[pltpu.VMEM((TL + 8, dc), F32), pltpu.VMEM((dc // LANES, TL, LANES), F32),
                        pltpu.VMEM((dc // LANES, TL, LANES), F32), pltpu.VMEM((1, dc), F32)],
        compiler_params=_cparams(("parallel", "arbitrary")),
        name="rglru_mix",
    )(xb, gy, conv_prev8, h0.reshape(n_batch, 1, dc), wts["conv_w"], wts["conv_b"], wts["gate_w"], wts["gate_b"],
      wts["lam"])
    return yc, ctail[:, 8 - (CONV_W - 1):, :], hl.reshape(n_batch, dc)


def _t5_bucket(dist):
    dist = np.asarray(dist)
    max_exact = N_BUCKETS // 2
    scaled = np.log(np.maximum(dist, 1) / max_exact) / math.log(BUCKET_MAX_DIST / max_exact)
    large = np.minimum(max_exact + (scaled * (N_BUCKETS - max_exact)).astype(np.int32), N_BUCKETS - 1)
    return np.where(dist < max_exact, dist, large).astype(np.int32)


def _dist_table(rel_bias, max_dist):
    dist = np.arange(max_dist + 1)
    count = np.zeros(max_dist + 1, np.float32)
    for window, dil in DILATED:
        count += ((dist % dil == 0) & (dist <= window)).astype(np.float32)
    logcnt = np.where(count > 0, np.log(np.maximum(count, 1.0)), 0.0).astype(np.float32)
    tab = jnp.take(rel_bias, jnp.asarray(_t5_bucket(dist)), axis=0) + jnp.asarray(logcnt)[:, None]
    return jnp.where(jnp.asarray(count > 0)[:, None], tab, NEG_BIG)


def _toeplitz_tiles(tab, n_pos, n_neg, T):
    D, H = tab.shape
    span = T * n_pos
    assert D >= span
    n_col = span + T * n_neg + T - 1
    ext = jnp.concatenate([jnp.flip(tab[:span], axis=0), jnp.full((n_col + 1 - span, H), NEG_BIG, F32)], axis=0)
    ext = jnp.transpose(ext)
    skew = jnp.tile(ext, (1, T))[:, :T * n_col].reshape(H, T, n_col)
    tiles = [skew[:, :, span - 1 - T * dd: span - 1 - T * dd + T] for dd in range(-n_neg, n_pos)]
    return jnp.stack(tiles, axis=1)


def _attn_prompt_kernel(q_ref, k_ref, v_ref, bias_ref, o_ref, kb_scr, vb_scr, *, E, SUB, NS):
    qi = pl.program_id(2)
    TQ = NS * SUB

    @pl.when(qi == 0)
    def _():
        kb_scr[...] = k_ref[0].astype(BF16)
        vb_scr[...] = v_ref[0].astype(BF16)

    lane = lax.broadcasted_iota(jnp.int32, (SUB, 2 * E), 1)
    q2 = []
    for rs in range(NS):
        q = q_ref[0, rs * SUB:(rs + 1) * SUB, :] * (E ** -0.5 * LOG2E)
        q2.append(jnp.concatenate([jnp.where(lane < E, q, 0.0), jnp.where(lane >= E, q, 0.0)], axis=0).astype(BF16))

    def block(i, carry, diagonal):
        j = qi - i
        koff = pl.multiple_of(j * TQ, TQ)
        out = []
        for rs in range(NS):
            n_cs = rs + 1 if diagonal else NS
            kj = kb_scr[pl.ds(koff, n_cs * SUB), :]
            vj = vb_scr[pl.ds(koff, n_cs * SUB), :]
            m, l, acc = carry[rs]
            s = _dot_nt(q2[rs], kj)
            parts = []
            for cs in range(n_cs):
                dd = i * NS + (rs - cs + NS - 1)
                bias = jnp.concatenate([bias_ref[0, dd], bias_ref[1, dd]], axis=0)
                parts.append(s[:, cs * SUB:(cs + 1) * SUB] + bias)
            mx = parts[0]
            for part in parts[1:]:
                mx = jnp.maximum(mx, part)
            m_new = jnp.maximum(m, jnp.max(mx, axis=-1, keepdims=True))
            alpha = jnp.exp2(m - m_new)
            ps = [jnp.exp2(part - m_new) for part in parts]
            psum = ps[0]
            for pexp in ps[1:]:
                psum = psum + pexp
            l = alpha * l + psum
            acc = alpha * acc + _dot(jnp.concatenate(ps, axis=-1).astype(BF16), vj)
            out.append((m_new, l, acc))
        return tuple(out)

    init = tuple((jnp.full((2 * SUB, SUB), NEG_BIG, F32), jnp.zeros((2 * SUB, SUB), F32),
                  jnp.zeros((2 * SUB, 2 * E), F32)) for _ in range(NS))
    first = block(0, init, True)
    res = lax.fori_loop(1, qi + 1, lambda i, carry: block(i, carry, False), first)
    for rs in range(NS):
        m, l, acc = res[rs]
        o = acc / jnp.sum(l, axis=-1, keepdims=True)
        o_ref[0, rs * SUB:(rs + 1) * SUB, :] = jnp.where(lane < E, o[0:SUB], o[SUB:2 * SUB])


def attn_prompt(q, k, v, bias_tiles, n_batch, seq):
    hd = q.shape[-1]
    E = hd // H_D
    SUB = ATT_TILE
    NS = ATT_SUBTILES
    TQ = SUB * NS
    nq = seq // TQ
    nt = bias_tiles.shape[1]
    return pl.pallas_call(
        functools.partial(_attn_prompt_kernel, E=E, SUB=SUB, NS=NS),
        out_shape=jax.ShapeDtypeStruct((n_batch, seq, hd), F32),
        grid=(H_D // 2, n_batch, nq),
        in_specs=[pl.BlockSpec((1, TQ, 2 * E), lambda hp, b, i: (b, i, hp)),
                  pl.BlockSpec((1, seq, 2 * E), lambda hp, b, i: (b, 0, hp)),
                  pl.BlockSpec((1, seq, 2 * E), lambda hp, b, i: (b, 0, hp)),
                  pl.BlockSpec((2, nt, SUB, SUB), lambda hp, b, i: (hp, 0, 0, 0))],
        out_specs=pl.BlockSpec((1, TQ, 2 * E), lambda hp, b, i: (b, i, hp)),
        scratch_shapes=[pltpu.VMEM((seq, 2 * E), BF16), pltpu.VMEM((seq, 2 * E), BF16)],
        compiler_params=_cparams(("arbitrary", "arbitrary", "arbitrary")),
        name="dilated_attn_prompt",
    )(q, k, v, bias_tiles)


def _attn_sample_kernel(q_ref, kn_ref, vn_ref, ck_ref, cv_ref, bo_ref, bn_ref, o_ref, *, E, S):
    lane = lax.broadcasted_iota(jnp.int32, (S, 2 * E), 1)
    NPAD = bn_ref.shape[-1]
    outs = []
    for hp in range(H_D // 2):
        sl = slice(hp * 2 * E, (hp + 1) * 2 * E)
        q = q_ref[0, :, sl] * (E ** -0.5)
        q2 = jnp.concatenate([jnp.where(lane < E, q, 0.0), jnp.where(lane >= E, q, 0.0)], axis=0).astype(BF16)
        zpad = jnp.zeros((NPAD - S, 2 * E), F32)
        kn = jnp.concatenate([kn_ref[0, :, sl], zpad], axis=0).astype(BF16)
        vn = jnp.concatenate([vn_ref[0, :, sl], zpad], axis=0).astype(BF16)
        s_old = _dot_nt(q2, ck_ref[0, :, sl].astype(BF16)) + jnp.concatenate([bo_ref[2 * hp], bo_ref[2 * hp + 1]], axis=0)
        s_new = _dot_nt(q2, kn) + jnp.concatenate([bn_ref[2 * hp], bn_ref[2 * hp + 1]], axis=0)
        m = jnp.maximum(jnp.max(s_old, axis=-1, keepdims=True), jnp.max(s_new, axis=-1, keepdims=True))
        p_old = jnp.exp(s_old - m)
        p_new = jnp.exp(s_new - m)
        l = jnp.sum(p_old, axis=-1, keepdims=True) + jnp.sum(p_new, axis=-1, keepdims=True)
        acc = _dot(p_old.astype(BF16), cv_ref[0, :, sl].astype(BF16)) + _dot(p_new.astype(BF16), vn)
        o = acc / l
        outs.append(jnp.where(lane < E, o[0:S], o[S:2 * S]))
    o_ref[0] = jnp.concatenate(outs, axis=-1)


def attn_sample(q, k_new, v_new, cache_k, cache_v, bias_old, bias_new):
    n_batch, S, hd = q.shape
    W = cache_k.shape[1]
    E = hd // H_D
    NPAD = bias_new.shape[-1]
    return pl.pallas_call(
        functools.partial(_attn_sample_kernel, E=E, S=S),
        out_shape=jax.ShapeDtypeStruct((n_batch, S, hd), F32),
        grid=(n_batch,),
        in_specs=[pl.BlockSpec((1, S, hd), lambda b: (b, 0, 0)),
                  pl.BlockSpec((1, S, hd), lambda b: (b, 0, 0)),
                  pl.BlockSpec((1, S, hd), lambda b: (b, 0, 0)),
                  pl.BlockSpec((1, W, hd), lambda b: (b, 0, 0)),
                  pl.BlockSpec((1, W, hd), lambda b: (b, 0, 0)),
                  pl.BlockSpec((H_D, S, W), lambda b: (0, 0, 0)),
                  pl.BlockSpec((H_D, S, NPAD), lambda b: (0, 0, 0))],
        out_specs=pl.BlockSpec((1, S, hd), lambda b: (b, 0, 0)),
        compiler_params=_cparams(("parallel",)),
        name="dilated_attn_sample",
    )(q, k_new, v_new, cache_k, cache_v, bias_old, bias_new)


def _even_weights(j, w_in_even, w_out_even, shift_mu, decay_w0, decay_up, iclr_a0, iclr_up, gate_up, key_k, key_a,
                  bonus_r_k, lnx_g, lnx_b, sgu_norm_g, sgu_norm_b, sgu_w, sgu_b):
    da = decay_w0.shape[1]
    zeros_d = jnp.zeros((R_ICLR, da), F32)
    zeros_i = jnp.zeros((R_DECAY, da), F32)
    return dict(
        w_in=w_in_even[j].astype(BF16),
        w_out_a=w_out_even[j, :da].astype(BF16), w_out_b=w_out_even[j, da:].astype(BF16),
        mu=shift_mu[j].reshape(1, -1), w0=decay_w0[j].reshape(1, -1), a0=iclr_a0[j].reshape(1, -1),
        wd=jnp.concatenate([decay_up[j], zeros_d], axis=0), wa=jnp.concatenate([zeros_i, iclr_up[j]], axis=0),
        wg=gate_up[j], key_k=key_k[j].reshape(1, -1), key_a=key_a[j].reshape(1, -1),
        bonus=bonus_r_k[j].reshape(1, -1), lnx_g=lnx_g[j].reshape(1, -1), lnx_b=lnx_b[j].reshape(1, -1),
        ng=sgu_norm_g[j].reshape(1, -1), nb=sgu_norm_b[j].reshape(1, -1), sgu_w=sgu_w[j], sgu_b=sgu_b[j])


def _gmlp_tables(sgu_w, sgu_b, chunk):
    reps = GMLP_TILE // chunk
    n_h = sgu_w.shape[0]
    cb = None
    wm = sgu_w[:, :chunk, :chunk] * jnp.asarray(np.tril(np.ones((chunk, chunk), np.float32)))
    if reps > 1:
        eye = jnp.asarray(np.eye(reps, dtype=np.float32))
        wm = jnp.einsum("ab,hts->hatbs", eye, wm).reshape(n_h, GMLP_TILE, GMLP_TILE)
    bias = jnp.tile(jnp.transpose(sgu_b[:, :chunk]), (reps, 1))
    return wm.astype(BF16), bias


def _even_layer(x, n_batch, seq, chunk, n_par, want_vn, shift_prev, wkv0, norm_g, ew):
    pa, pu, pv = norm_matmul(x, norm_g, ew["w_in"], (ew["mu"].shape[1], ew["ng"].shape[1], ew["ng"].shape[1]))
    ya, wkv = rwkv_mix(pa, n_batch, seq, shift_prev, wkv0, ew, n_par)
    wm, bias = _gmlp_tables(ew["sgu_w"], ew["sgu_b"], chunk)
    cb = pu.shape[1] // wm.shape[0]
    bias_tile = jnp.repeat(bias, cb, axis=1)
    yb, vn = gmlp_mix(pu, pv, ew["ng"], ew["nb"], wm, bias_tile, want_vn)
    x = proj_residual(x, ya, yb, ew["w_out_a"], ew["w_out_b"])
    last = pa.reshape(n_batch, seq, -1)[:, -1]
    return x, last, wkv, vn


def _odd_weights(j, w_in_odd, w_out_odd, conv_w, conv_b, rgate_w, rgate_b, igate_w, igate_b, lru_lambda):
    dc = conv_b.shape[1]
    eye = jnp.asarray(np.eye(H_C, dtype=np.float32))

    def blockdiag(w):
        dh = w.shape[-1]
        return jnp.einsum("ab,aij->aibj", eye, w).reshape(H_C * dh, H_C * dh)

    return dict(
        w_in=w_in_odd[j].astype(BF16),
        w_out_c=w_out_odd[j, :dc].astype(BF16), w_out_d=w_out_odd[j, dc:].astype(BF16),
        conv_w=conv_w[j], conv_b=conv_b[j].reshape(1, -1),
        gate_w=jnp.concatenate([blockdiag(rgate_w[j]), blockdiag(igate_w[j])], axis=1).astype(BF16),
        gate_b=jnp.concatenate([rgate_b[j], igate_b[j]]).reshape(1, -1),
        lam=lru_lambda[j].reshape(1, -1))


def _odd_layer(x, n_batch, seq, conv_prev, h0, pos0, caches, dist_tab, norm_g, ow):
    dc = ow["lam"].shape[1]
    gy, xb, q, k, v = norm_matmul(x, norm_g, ow["w_in"], (dc,) * 5)
    conv_prev8 = jnp.pad(conv_prev, ((0, 0), (8 - (CONV_W - 1), 0), (0, 0)))
    yc, conv_last, h_last = rglru_mix(xb, gy, n_batch, seq, conv_prev8, h0, pos0, ow)
    hd = q.shape[1]
    q3, k3, v3 = (a.reshape(n_batch, seq, hd) for a in (q, k, v))
    if caches is None:
        tiles = _toeplitz_tiles(dist_tab * LOG2E, seq // ATT_TILE, ATT_SUBTILES - 1, ATT_TILE)
        o = attn_prompt(q3, k3, v3, tiles, n_batch, seq)
    else:
        cache_k, cache_v = caches
        W = cache_k.shape[1]
        ck = cache_k.reshape(n_batch, W, hd)
        cv = cache_v.reshape(n_batch, W, hd)
        NPAD = 128
        tab_t = jnp.flip(jnp.transpose(dist_tab[:W + seq]), axis=1)
        b_old = jnp.stack([tab_t[:, seq - 1 - j:seq - 1 - j + W] for j in range(seq)], axis=1)
        d_new = np.arange(seq)[:, None] - np.arange(NPAD)[None, :]
        ok_new = (d_new >= 0) & (np.arange(NPAD)[None, :] < seq)
        b_new = jnp.take(dist_tab, jnp.asarray(np.maximum(d_new, 0)), axis=0)
        b_new = jnp.transpose(jnp.where(jnp.asarray(ok_new)[..., None], b_new, NEG_BIG), (2, 0, 1))
        o = attn_sample(q3, k3, v3, ck, cv, b_old, b_new)
    x = proj_residual(x, yc, o.reshape(n_batch * seq, hd), ow["w_out_c"], ow["w_out_d"])
    e = hd // H_D
    return x, conv_last, h_last, k3.reshape(n_batch, seq, H_D, e), v3.reshape(n_batch, seq, H_D, e)


def _moe_weights(l, router_group_w, router_group_b, router_expert_w, router_expert_b, exp_w_gate, exp_w_up,
                 exp_w_down):
    d = router_group_w.shape[1]
    n_used = N_GROUPS + router_expert_w.shape[2]
    rw = jnp.concatenate([router_group_w[l], router_expert_w[l], jnp.zeros((d, ROUTER_LANES - n_used), F32)], axis=1)
    rb = jnp.concatenate([router_group_b[l], router_expert_b[l], jnp.zeros((ROUTER_LANES - n_used,), F32)])
    return dict(rw=rw, rb=rb.reshape(1, -1), wg=exp_w_gate[l].astype(BF16), wu=exp_w_up[l].astype(BF16),
                wd=exp_w_down[l].astype(BF16))


def kernel(x_prompt, x_sample, state_wkv, state_shift, state_conv, state_rglru, cache_k, cache_v, norm_mix, norm_ffn, norm_final, w_in_even, w_out_even, shift_mu, decay_w0, decay_up, iclr_a0, iclr_up, gate_up, key_k, key_a, bonus_r_k, lnx_g, lnx_b, sgu_norm_g, sgu_norm_b, sgu_w, sgu_b, w_in_odd, w_out_odd, conv_w, conv_b, rgate_w, rgate_b, igate_w, igate_b, lru_lambda, rel_bias, router_group_w, router_group_b, router_expert_w, router_expert_b, exp_w_gate, exp_w_up, exp_w_down):
    B, L, D = x_prompt.shape
    DB, S, _ = x_sample.shape
    depth = norm_mix.shape[0]
    xp = x_prompt.reshape(B * L, D)
    xs = x_sample.reshape(DB * S, D)
    W = cache_k.shape[2]
    dist_tab = _dist_table(rel_bias, max(L, W + S) - 1)

    sorted_buf = None
    wkv_p, shift_p, conv_p, lru_p, k_p, v_p = [], [], [], [], [], []
    wkv_s, shift_s, chunkv_s, conv_s, lru_s, k_s, v_s = [], [], [], [], [], [], []
    for l in range(depth):
        j = l // 2
        if l % 2 == 0:
            ew = _even_weights(j, w_in_even, w_out_even, shift_mu, decay_w0, decay_up, iclr_a0, iclr_up, gate_up,
                               key_k, key_a, bonus_r_k, lnx_g, lnx_b, sgu_norm_g, sgu_norm_b, sgu_w, sgu_b)
            a_proj = ew["mu"].shape[1]
            h_a = state_wkv.shape[2]
            xp, sh, wkv, _ = _even_layer(xp, B, L, GMLP_TILE, RWKV_PAR_PROMPT, False, jnp.zeros((B, a_proj), F32),
                                         jnp.zeros((B, h_a, DH_A, DH_A), F32), norm_mix[l], ew)
            xs, sh_s, wkv_s_new, vn_s = _even_layer(xs, DB, S, S, RWKV_PAR_SAMPLE, True, state_shift[j], state_wkv[j], norm_mix[l], ew)
            wkv_p.append(wkv)
            shift_p.append(sh)
            wkv_s.append(wkv_s_new)
            shift_s.append(sh_s)
            chunkv_s.append(vn_s.reshape(DB, S, -1))
        else:
            ow = _odd_weights(j, w_in_odd, w_out_odd, conv_w, conv_b, rgate_w, rgate_b, igate_w, igate_b, lru_lambda)
            dc = ow["lam"].shape[1]
            xp, cv, hl, kr, vr = _odd_layer(xp, B, L, jnp.zeros((B, CONV_W - 1, dc), F32), jnp.zeros((B, dc), F32),
                                            0, None, dist_tab, norm_mix[l], ow)
            xs, cv_s, hl_s, kr_s, vr_s = _odd_layer(xs, DB, S, state_conv[j], state_rglru[j], PAST_LEN,
                                                    (cache_k[j], cache_v[j]), dist_tab, norm_mix[l], ow)
            conv_p.append(cv)
            lru_p.append(hl)
            k_p.append(kr)
            v_p.append(vr)
            conv_s.append(cv_s)
            lru_s.append(hl_s)
            k_s.append(kr_s)
            v_s.append(vr_s)
        mw = _moe_weights(l, router_group_w, router_group_b, router_expert_w, router_expert_b, exp_w_gate, exp_w_up,
                          exp_w_down)
        xp, sorted_buf = moe_layer_sparse(xp, norm_ffn[l], mw["rw"], mw["rb"], mw["wg"], mw["wu"], mw["wd"],
                                          final_g=norm_final if l == depth - 1 else None, sorted_buf=sorted_buf)
        xs = moe_layer(xs, norm_ffn[l], mw["rw"], mw["rb"], mw["wg"], mw["wu"], mw["wd"])
    y_prompt = xp.reshape(B, L, D)
    y_sample = rmsnorm_call(xs, norm_final).reshape(DB, S, D)
    return (y_prompt, y_sample,
            jnp.stack(wkv_p), jnp.stack(shift_p), jnp.stack(conv_p), jnp.stack(lru_p), jnp.stack(k_p), jnp.stack(v_p),
            jnp.stack(wkv_s), jnp.stack(shift_s), jnp.stack(chunkv_s), jnp.stack(conv_s), jnp.stack(lru_s),
            jnp.stack(k_s), jnp.stack(v_s))
```

```python
import functools
import math

import numpy as np
import jax
import jax.numpy as jnp
from jax import lax
from jax.experimental import pallas as pl
from jax.experimental.pallas import tpu as pltpu

F32 = jnp.float32
BF16 = jnp.bfloat16
HI = lax.Precision.HIGHEST

PAST_LEN = 8192
DH_A = 64
R_DECAY = 64
R_ICLR = 64
R_GATE = 128
GN_EPS = 64e-5
H_B = 4
H_C = 8
CONV_W = 4
LRU_C = 8.0
H_D = 8
DILATED = ((128, 1), (512, 4), (2048, 16))
N_BUCKETS = 32
BUCKET_MAX_DIST = 2048
NEG_BIG = -1e30
N_GROUPS = 4
EXP_PER_GROUP = 4
NORM_EPS = 1e-6
LOG2E = math.log2(math.e)

VMEM_LIMIT = 56 * 1024 * 1024
RWKV_CHUNK = 64
RWKV_PAR_PROMPT = 4
RWKV_PAR_SAMPLE = 8
ATT_TILE = 128
ATT_SUBTILES = 4
LANES = 128
SUBLANES = 8


def _cparams(sem):
    return pltpu.CompilerParams(dimension_semantics=sem, vmem_limit_bytes=VMEM_LIMIT)


def _dot(a, b, precision=None):
    return jnp.dot(a, b, preferred_element_type=F32, precision=precision)


def _dot_nt(a, b, precision=None):
    return lax.dot_general(a, b, (((1,), (1,)), ((), ())), preferred_element_type=F32, precision=precision)


def _dot_tn(a, b, precision=None):
    return lax.dot_general(a, b, (((0,), (0,)), ((), ())), preferred_element_type=F32, precision=precision)


def _split_bf16(x, n):
    parts = []
    for _ in range(n):
        hi = x.astype(BF16)
        parts.append(hi)
        x = x - hi.astype(F32)
    return parts


def _mp_dot(dotfn, a, b, passes):
    if passes == 1:
        return dotfn(a.astype(BF16), b.astype(BF16))
    a_hi, a_lo = _split_bf16(a, 2)
    b_hi, b_lo = _split_bf16(b, 2)
    return dotfn(a_hi, b_hi) + (dotfn(a_hi, b_lo) + dotfn(a_lo, b_hi))


def _dot_exact_rhs(a, b_bf16, n_split):
    parts = _split_bf16(a, n_split)
    acc = _dot(parts[0], b_bf16)
    for part in parts[1:]:
        acc = acc + _dot(part, b_bf16)
    return acc


def _dot_exact_lhs(a_bf16, b, n_split):
    parts = _split_bf16(b, n_split)
    acc = _dot(a_bf16, parts[0])
    for part in parts[1:]:
        acc = acc + _dot(a_bf16, part)
    return acc


def _softplus(x):
    return jnp.maximum(x, 0.0) + jnp.log(1.0 + jnp.exp(-jnp.abs(x)))


def _sigmoid(x):
    return 1.0 / (1.0 + jnp.exp(-x))


def _gelu(x):
    c = math.sqrt(2.0 / math.pi)
    return 0.5 * x * (1.0 + jnp.tanh(c * (x + 0.044715 * (x * x * x))))


def _row_tile(t, pref=512):
    return pref if t % pref == 0 else t


def _norm_matmul_kernel(x_ref, g_ref, w_ref, *out_refs, splits):
    x = x_ref[...]
    ms = jnp.mean(x * x, axis=-1, keepdims=True)
    h = (x * lax.rsqrt(ms + NORM_EPS) * g_ref[...]).astype(BF16)
    off = 0
    for o_ref, n in zip(out_refs, splits):
        o_ref[...] = _dot(h, w_ref[:, off:off + n])
        off += n


def norm_matmul(x, g, w_bf16, splits):
    t, d = x.shape
    n = w_bf16.shape[1]
    tm = _row_tile(t)
    return pl.pallas_call(
        functools.partial(_norm_matmul_kernel, splits=splits),
        out_shape=[jax.ShapeDtypeStruct((t, s), F32) for s in splits],
        grid=(t // tm,),
        in_specs=[pl.BlockSpec((tm, d), lambda i: (i, 0)),
                  pl.BlockSpec((1, d), lambda i: (0, 0)),
                  pl.BlockSpec((d, n), lambda i: (0, 0))],
        out_specs=[pl.BlockSpec((tm, s), lambda i: (i, 0)) for s in splits],
        compiler_params=_cparams(("parallel",)),
        name="norm_matmul",
    )(x, g.reshape(1, d), w_bf16)


def _proj_res_kernel(x_ref, a_ref, b_ref, wa_ref, wb_ref, o_ref):
    acc = _dot(a_ref[...].astype(BF16), wa_ref[...]) + _dot(b_ref[...].astype(BF16), wb_ref[...])
    o_ref[...] = x_ref[...] + acc


def proj_residual(x, a, b, wa, wb):
    t, d = x.shape
    tm = _row_tile(t)
    ka, kb = a.shape[1], b.shape[1]
    return pl.pallas_call(
        _proj_res_kernel,
        out_shape=jax.ShapeDtypeStruct((t, d), F32),
        grid=(t // tm,),
        in_specs=[pl.BlockSpec((tm, d), lambda i: (i, 0)),
                  pl.BlockSpec((tm, ka), lambda i: (i, 0)),
                  pl.BlockSpec((tm, kb), lambda i: (i, 0)),
                  pl.BlockSpec((ka, d), lambda i: (0, 0)),
                  pl.BlockSpec((kb, d), lambda i: (0, 0))],
        out_specs=pl.BlockSpec((tm, d), lambda i: (i, 0)),
        compiler_params=_cparams(("parallel",)),
        name="proj_residual",
    )(x, a, b, wa, wb)


def _rmsnorm_kernel(x_ref, g_ref, o_ref):
    x = x_ref[...]
    ms = jnp.mean(x * x, axis=-1, keepdims=True)
    o_ref[...] = x * lax.rsqrt(ms + NORM_EPS) * g_ref[...]


def rmsnorm_call(x, g):
    t, d = x.shape
    tm = _row_tile(t)
    return pl.pallas_call(
        _rmsnorm_kernel,
        out_shape=jax.ShapeDtypeStruct((t, d), F32),
        grid=(t // tm,),
        in_specs=[pl.BlockSpec((tm, d), lambda i: (i, 0)), pl.BlockSpec((1, d), lambda i: (0, 0))],
        out_specs=pl.BlockSpec((tm, d), lambda i: (i, 0)),
        compiler_params=_cparams(("parallel",)),
        name="final_rmsnorm",
    )(x, g.reshape(1, d))


ROUTER_LANES = 128


def _route(xn, rw, rb, lane, n_exp):
    logits = _mp_dot(_dot, xn, rw, 3) + rb
    lg = jnp.where(lane < N_GROUPS, logits, -jnp.inf)
    gm = jnp.max(lg, axis=-1, keepdims=True)
    top_pg = 1.0 / jnp.sum(jnp.exp(lg - gm), axis=-1, keepdims=True)
    grp = jnp.min(jnp.where(lg == gm, lane, ROUTER_LANES), axis=-1, keepdims=True)
    in_grp = (lane >= N_GROUPS) & (lane < N_GROUPS + n_exp) & (((lane - N_GROUPS) // EXP_PER_GROUP) == grp)
    le = jnp.where(in_grp, logits, -jnp.inf)
    t1 = jnp.max(le, axis=-1, keepdims=True)
    i1 = jnp.min(jnp.where(le == t1, lane, ROUTER_LANES), axis=-1, keepdims=True)
    le2 = jnp.where(lane == i1, -jnp.inf, le)
    t2 = jnp.max(le2, axis=-1, keepdims=True)
    i2 = jnp.min(jnp.where(le2 == t2, lane, ROUTER_LANES), axis=-1, keepdims=True)
    ex = jnp.exp(t2 - t1)
    w1 = 1.0 / (1.0 + ex)
    return i1, i2, w1 * top_pg, (ex * w1) * top_pg


def _moe_kernel(x_ref, g_ref, rw_ref, rb_ref, wg_ref, wu_ref, wd_ref, o_ref, xn_scr, gate_scr, acc_scr, *, n_exp):
    e = pl.program_id(1)
    tm = x_ref.shape[0]
    lane = lax.broadcasted_iota(jnp.int32, (tm, ROUTER_LANES), 1)

    @pl.when(e == 0)
    def _():
        x = x_ref[...]
        ms = jnp.mean(x * x, axis=-1, keepdims=True)
        xn = x * lax.rsqrt(ms + NORM_EPS) * g_ref[...]
        xn_scr[...] = xn.astype(BF16)
        i1, i2, g1, g2 = _route(xn, rw_ref[...], rb_ref[...], lane, n_exp)
        gate_scr[...] = jnp.where(lane == i1, g1, 0.0) + jnp.where(lane == i2, g2, 0.0)
        acc_scr[...] = jnp.zeros_like(acc_scr)

    xn = xn_scr[...]
    hg = _dot(xn, wg_ref[0])
    hu = _dot(xn, wu_ref[0])
    gcol = jnp.sum(jnp.where(lane == e + N_GROUPS, gate_scr[...], 0.0), axis=-1, keepdims=True)
    hid = hg * _sigmoid(hg) * hu * gcol
    acc_scr[...] += _dot(hid.astype(BF16), wd_ref[0])

    @pl.when(e == n_exp - 1)
    def _():
        o_ref[...] = x_ref[...] + acc_scr[...]


def moe_layer(x, g, rw, rb, wg, wu, wd):
    t, d = x.shape
    n_exp, _, f = wg.shape
    tm = _row_tile(t)
    return pl.pallas_call(
        functools.partial(_moe_kernel, n_exp=n_exp),
        out_shape=jax.ShapeDtypeStruct((t, d), F32),
        grid=(t // tm, n_exp),
        in_specs=[pl.BlockSpec((tm, d), lambda i, e: (i, 0)),
                  pl.BlockSpec((1, d), lambda i, e: (0, 0)),
                  pl.BlockSpec((d, ROUTER_LANES), lambda i, e: (0, 0)),
                  pl.BlockSpec((1, ROUTER_LANES), lambda i, e: (0, 0)),
                  pl.BlockSpec((1, d, f), lambda i, e: (e, 0, 0)),
                  pl.BlockSpec((1, d, f), lambda i, e: (e, 0, 0)),
                  pl.BlockSpec((1, f, d), lambda i, e: (e, 0, 0))],
        out_specs=pl.BlockSpec((tm, d), lambda i, e: (i, 0)),
        scratch_shapes=[pltpu.VMEM((tm, d), BF16), pltpu.VMEM((tm, ROUTER_LANES), F32), pltpu.VMEM((tm, d), F32)],
        compiler_params=_cparams(("parallel", "arbitrary")),
        name="hier_moe",
    )(x, g.reshape(1, d), rw, rb, wg, wu, wd)


MOE_ROW_TILE = 512
MOE_COPY_CHUNK = 256
MOE_COMBINE_TILE = 256


def _router_kernel(x_ref, g_ref, rw_ref, rb_ref, tri_ref, gate_ref, info_ref, cnt_ref, base_scr, *, n_exp, n_tiles):
    i = pl.program_id(0)
    tm = x_ref.shape[0]
    lane = lax.broadcasted_iota(jnp.int32, (tm, ROUTER_LANES), 1)

    @pl.when(i == 0)
    def _():
        base_scr[...] = jnp.zeros_like(base_scr)

    x = x_ref[...]
    ms = jnp.mean(x * x, axis=-1, keepdims=True)
    xn = x * lax.rsqrt(ms + NORM_EPS) * g_ref[...]
    i1, i2, g1, g2 = _route(xn, rw_ref[...], rb_ref[...], lane, n_exp)
    chosen = jnp.where((lane == i1) | (lane == i2), 1.0, 0.0)
    before = _dot(tri_ref[...], chosen.astype(BF16)) + base_scr[...]
    r1 = jnp.sum(jnp.where(lane == i1, before, 0.0), axis=-1, keepdims=True)
    r2 = jnp.sum(jnp.where(lane == i2, before, 0.0), axis=-1, keepdims=True)
    base_scr[...] += jnp.sum(chosen, axis=0, keepdims=True)
    gate_ref[...] = jnp.where(lane == 0, g1, 0.0) + jnp.where(lane == 1, g2, 0.0)
    e1 = (i1 - N_GROUPS).astype(F32)
    e2 = (i2 - N_GROUPS).astype(F32)
    info_ref[...] = (jnp.where(lane == 0, e1, 0.0) + jnp.where(lane == 1, e2, 0.0)
                     + jnp.where(lane == 2, r1, 0.0) + jnp.where(lane == 3, r2, 0.0))

    @pl.when(i == n_tiles - 1)
    def _():
        cnt_ref[...] = base_scr[...]


def _scatter_rows_kernel(pos0_ref, pos1_ref, x_ref, xs_in_hbm, xs_hbm, stage, sem, *, CH, n_chunks):
    del xs_in_hbm
    c = pl.program_id(0)
    slot = c % 2
    x = x_ref[...]
    for j in range(SUBLANES):
        stage[slot, pl.ds(j, CH, stride=SUBLANES), :] = x[:, j * LANES:(j + 1) * LANES]

    def body(r, carry):
        t = c * CH + r
        src = stage.at[slot, pl.ds(pl.multiple_of(r * SUBLANES, SUBLANES), SUBLANES), :]
        d0 = pl.multiple_of(pos0_ref[t], SUBLANES)
        d1 = pl.multiple_of(pos1_ref[t], SUBLANES)
        pltpu.make_async_copy(src, xs_hbm.at[pl.ds(d0, SUBLANES), :], sem.at[slot]).start(priority=0)
        pltpu.make_async_copy(src, xs_hbm.at[pl.ds(d1, SUBLANES), :], sem.at[slot]).start(priority=1)
        return carry

    lax.fori_loop(0, CH, body, 0, unroll=8)

    def drain(s):
        pltpu.make_async_copy(stage.at[s], xs_hbm.at[pl.ds(0, CH * SUBLANES), :], sem.at[s]).wait()
        pltpu.make_async_copy(stage.at[s], xs_hbm.at[pl.ds(0, CH * SUBLANES), :], sem.at[s]).wait()

    @pl.when(c > 0)
    def _():
        drain(1 - slot)

    @pl.when(c == n_chunks - 1)
    def _():
        drain(slot)


def _tile_rows_to_matrix(ref, lead, n_rows):
    return jnp.concatenate([ref[lead + (pl.ds(j, n_rows, stride=SUBLANES), slice(None))] for j in range(SUBLANES)],
                           axis=-1)


def _expert_kernel(te_ref, nv_ref, xs_ref, g_ref, wg_ref, wu_ref, wd_ref, y_ref, *, TM):
    @pl.when(pl.program_id(0) < nv_ref[0])
    def _():
        x = _tile_rows_to_matrix(xs_ref, (), TM)
        ms = jnp.mean(x * x, axis=-1, keepdims=True)
        xn = (x * lax.rsqrt(ms + NORM_EPS) * g_ref[...]).astype(BF16)
        hg = _dot(xn, wg_ref[0])
        hu = _dot(xn, wu_ref[0])
        hid = hg * _sigmoid(hg) * hu
        y = _dot(hid.astype(BF16), wd_ref[0])
        for j in range(SUBLANES):
            y_ref[pl.ds(j, TM, stride=SUBLANES), :] = y[:, j * LANES:(j + 1) * LANES]

    @pl.when(pl.program_id(0) >= nv_ref[0])
    def _():
        y_ref[...] = jnp.zeros_like(y_ref)


def _combine_kernel(pos0_ref, pos1_ref, x_ref, gate_ref, fg_ref, y_hbm, o_ref, ybuf, sem, *, TC, n_tiles, final_norm):
    i = pl.program_id(0)

    def issue(tile, slot):
        def body(r, carry):
            t = tile * TC + r
            dst = pl.ds(pl.multiple_of(r * SUBLANES, SUBLANES), SUBLANES)
            s0 = pl.multiple_of(pos0_ref[t], SUBLANES)
            s1 = pl.multiple_of(pos1_ref[t], SUBLANES)
            pltpu.make_async_copy(y_hbm.at[pl.ds(s0, SUBLANES), :], ybuf.at[slot, 0, dst, :],
                                  sem.at[slot]).start(priority=0)
            pltpu.make_async_copy(y_hbm.at[pl.ds(s1, SUBLANES), :], ybuf.at[slot, 1, dst, :],
                                  sem.at[slot]).start(priority=1)
            return carry
        lax.fori_loop(0, TC, body, 0, unroll=8)

    @pl.when(i == 0)
    def _():
        issue(0, 0)

    @pl.when(i + 1 < n_tiles)
    def _():
        issue(i + 1, (i + 1) % 2)

    slot = i % 2
    pltpu.make_async_copy(y_hbm.at[pl.ds(0, TC * SUBLANES), :], ybuf.at[slot, 0], sem.at[slot]).wait()
    pltpu.make_async_copy(y_hbm.at[pl.ds(0, TC * SUBLANES), :], ybuf.at[slot, 1], sem.at[slot]).wait()
    gate = gate_ref[...]
    y0 = _tile_rows_to_matrix(ybuf, (slot, 0), TC)
    y1 = _tile_rows_to_matrix(ybuf, (slot, 1), TC)
    out = x_ref[...] + gate[:, 0:1] * y0 + gate[:, 1:2] * y1
    if final_norm:
        ms = jnp.mean(out * out, axis=-1, keepdims=True)
        out = out * lax.rsqrt(ms + NORM_EPS) * fg_ref[...]
    o_ref[...] = out


def moe_layer_sparse(x, g, rw, rb, wg, wu, wd, final_g=None, sorted_buf=None):
    t, d = x.shape
    n_exp, _, f = wg.shape
    TM = MOE_ROW_TILE
    n_tiles = t // TM
    tri = jnp.asarray(np.tril(np.ones((TM, TM), np.float32), -1)).astype(BF16)
    gate, info, cnt = pl.pallas_call(
        functools.partial(_router_kernel, n_exp=n_exp, n_tiles=n_tiles),
        out_shape=[jax.ShapeDtypeStruct((t, ROUTER_LANES), F32), jax.ShapeDtypeStruct((t, ROUTER_LANES), F32),
                   jax.ShapeDtypeStruct((1, ROUTER_LANES), F32)],
        grid=(n_tiles,),
        in_specs=[pl.BlockSpec((TM, d), lambda i: (i, 0)),
                  pl.BlockSpec((1, d), lambda i: (0, 0)),
                  pl.BlockSpec((d, ROUTER_LANES), lambda i: (0, 0)),
                  pl.BlockSpec((1, ROUTER_LANES), lambda i: (0, 0)),
                  pl.BlockSpec((TM, TM), lambda i: (0, 0))],
        out_specs=[pl.BlockSpec((TM, ROUTER_LANES), lambda i: (i, 0)),
                   pl.BlockSpec((TM, ROUTER_LANES), lambda i: (i, 0)),
                   pl.BlockSpec((1, ROUTER_LANES), lambda i: (0, 0))],
        scratch_shapes=[pltpu.VMEM((1, ROUTER_LANES), F32)],
        compiler_params=_cparams(("arbitrary",)),
        name="moe_router",
    )(x, g.reshape(1, d), rw, rb, tri)

    counts = cnt[0, N_GROUPS:N_GROUPS + n_exp].astype(jnp.int32)
    padded = ((counts + TM - 1) // TM) * TM
    ends = jnp.cumsum(padded)
    offs = ends - padded
    eid = info[:, 0:2].astype(jnp.int32)
    rank = info[:, 2:4].astype(jnp.int32)
    pos = jnp.sum(jnp.where(eid[:, :, None] == jnp.arange(n_exp)[None, None, :], offs[None, None, :], 0), axis=-1) + rank
    assert d == SUBLANES * LANES, "a token row must fill exactly one (8, 128) tile"
    pos = pos * SUBLANES
    pos0, pos1 = pos[:, 0], pos[:, 1]
    max_tiles = (2 * t) // TM + n_exp
    n_valid = (ends[-1] // TM).astype(jnp.int32).reshape(1)
    tile_exp = jnp.minimum(jnp.sum((ends[None, :] // TM) <= jnp.arange(max_tiles)[:, None], axis=-1),
                           n_exp - 1).astype(jnp.int32)
    p_rows = max_tiles * TM

    CH = MOE_COPY_CHUNK
    xs = pl.pallas_call(
        functools.partial(_scatter_rows_kernel, CH=CH, n_chunks=t // CH),
        out_shape=jax.ShapeDtypeStruct((p_rows * SUBLANES, LANES), F32),
        grid_spec=pltpu.PrefetchScalarGridSpec(
            num_scalar_prefetch=2, grid=(t // CH,),
            in_specs=[pl.BlockSpec((CH, d), lambda c, p0, p1: (c, 0)), pl.BlockSpec(memory_space=pl.ANY)],
            out_specs=pl.BlockSpec(memory_space=pl.ANY),
            scratch_shapes=[pltpu.VMEM((2, CH * SUBLANES, LANES), F32), pltpu.SemaphoreType.DMA((2,))]),
        input_output_aliases={3: 0},
        compiler_params=pltpu.CompilerParams(dimension_semantics=("arbitrary",), vmem_limit_bytes=VMEM_LIMIT,
                                             has_side_effects=True),
        name="moe_scatter_rows",
    )(pos0, pos1, x, jnp.zeros((p_rows * SUBLANES, LANES), F32) if sorted_buf is None else sorted_buf)

    def row_idx(i, te, nv):
        return (jnp.minimum(i, nv[0] - 1), 0)

    ys = pl.pallas_call(
        functools.partial(_expert_kernel, TM=TM),
        out_shape=jax.ShapeDtypeStruct((p_rows * SUBLANES, LANES), F32),
        grid_spec=pltpu.PrefetchScalarGridSpec(
            num_scalar_prefetch=2, grid=(max_tiles,),
            in_specs=[pl.BlockSpec((TM * SUBLANES, LANES), row_idx),
                      pl.BlockSpec((1, d), lambda i, te, nv: (0, 0)),
                      pl.BlockSpec((1, d, f), lambda i, te, nv: (te[i], 0, 0)),
                      pl.BlockSpec((1, d, f), lambda i, te, nv: (te[i], 0, 0)),
                      pl.BlockSpec((1, f, d), lambda i, te, nv: (te[i], 0, 0))],
            out_specs=pl.BlockSpec((TM * SUBLANES, LANES), lambda i, te, nv: (i, 0))),
        compiler_params=_cparams(("arbitrary",)),
        name="moe_experts",
    )(tile_exp, n_valid, xs, g.reshape(1, d), wg, wu, wd)

    TC = MOE_COMBINE_TILE
    out = pl.pallas_call(
        functools.partial(_combine_kernel, TC=TC, n_tiles=t // TC, final_norm=final_g is not None),
        out_shape=jax.ShapeDtypeStruct((t, d), F32),
        grid_spec=pltpu.PrefetchScalarGridSpec(
            num_scalar_prefetch=2, grid=(t // TC,),
            in_specs=[pl.BlockSpec((TC, d), lambda i, p0, p1: (i, 0)),
                      pl.BlockSpec((TC, ROUTER_LANES), lambda i, p0, p1: (i, 0)),
                      pl.BlockSpec((1, d), lambda i, p0, p1: (0, 0)),
                      pl.BlockSpec(memory_space=pl.ANY)],
            out_specs=pl.BlockSpec((TC, d), lambda i, p0, p1: (i, 0)),
            scratch_shapes=[pltpu.VMEM((2, 2, TC * SUBLANES, LANES), F32), pltpu.SemaphoreType.DMA((2,))]),
        compiler_params=_cparams(("arbitrary",)),
        name="moe_combine",
    )(pos0, pos1, x, gate, (g if final_g is None else final_g).reshape(1, d), ys)
    return out, xs


def _rwkv_kernel(p_ref, prev_ref, s0_ref, mu_ref, w0_ref, wd_ref, a0_ref, wa_ref, wg_ref, kk_ref, ka_ref,
                 bonus_ref, lng_ref, lnb_ref, tri_ref, hsum_ref, ya_ref, sf_ref, s_scr, prev_scr,
                 *, NB, C, H, DH, n_chunks):
    c = pl.program_id(1)

    @pl.when(c == 0)
    def _():
        s_scr[...] = s0_ref[:, 0]
        prev_scr[...] = prev_ref[:, 0]

    DA = H * DH
    R = NB * C
    p = p_ref[...].reshape(R, p_ref.shape[-1])
    row = lax.broadcasted_iota(jnp.int32, p.shape, 0)
    shifted = pltpu.roll(p, 1, axis=0)
    for n in range(NB):
        shifted = jnp.where(row == n * C, prev_scr[n], shifted)
        prev_scr[n] = p[(n + 1) * C - 1:(n + 1) * C, :]
    xs = p + (shifted - p) * mu_ref[...]
    r = xs[:, 0:DA]
    k = xs[:, DA:2 * DA]
    v = xs[:, 2 * DA:3 * DA]
    lora = xs[:, 3 * DA:3 * DA + R_DECAY + R_ICLR]
    gd = xs[:, 3 * DA + R_DECAY + R_ICLR:3 * DA + R_DECAY + R_ICLR + R_GATE]

    w_log = -_softplus(-(w0_ref[...] + _mp_dot(_dot, jnp.tanh(lora), wd_ref[...], 3))) - 0.5
    lw = -jnp.exp(w_log)
    a = _sigmoid(a0_ref[...] + _mp_dot(_dot, lora, wa_ref[...], 3))
    g = _mp_dot(_dot, _sigmoid(gd), wg_ref[...], 3)

    kk = k * kk_ref[...]
    ss = _dot_exact_rhs(kk * kk, hsum_ref[...], 2)
    kk = kk / jnp.maximum(jnp.sqrt(ss), 1e-12)
    k2 = k * (1.0 + (a - 1.0) * ka_ref[...])
    kka = kk * a

    cum = jnp.concatenate([_dot_exact_lhs(tri_ref[...], lw[n * C:(n + 1) * C], 3) for n in range(NB)], axis=0)
    p_in = jnp.exp(cum)
    r_t = r * p_in
    a_t = kk * jnp.exp(cum - lw)
    p_inv = jnp.exp(-cum)
    b_t = kka * p_inv
    k_t = k2 * p_inv
    bonus = _dot_exact_rhs(r * k2 * bonus_ref[...], hsum_ref[...], 2) * v

    ri = lax.broadcasted_iota(jnp.int32, (C, C), 0)
    ci = lax.broadcasted_iota(jnp.int32, (C, C), 1)
    strict = ri > ci
    incl = ri >= ci
    eye = (ri == ci).astype(F32)
    n_double = max(int(math.ceil(math.log2(C))) - 1, 0)

    chains = [(n, h) for n in range(NB) for h in range(H)]

    def blk(x, n, h):
        return x[n * C:(n + 1) * C, h * DH:(h + 1) * DH]

    def bf(x):
        return x.astype(BF16)

    Bt = [bf(blk(b_t, n, h)) for n, h in chains]
    Kt = [bf(blk(k_t, n, h)) for n, h in chains]
    Vf = [blk(v, n, h) for n, h in chains]
    AR = [bf(jnp.concatenate([blk(a_t, n, h), blk(r_t, n, h)], axis=0)) for n, h in chains]
    S0 = [s_scr[n, h] for n, h in chains]
    idx = range(len(chains))
    GB = [_dot_nt(AR[i], Bt[i]) for i in idx]
    GK = [_dot_nt(AR[i], Kt[i]) for i in idx]
    ARS = [_dot_nt(AR[i], bf(S0[i])) for i in idx]
    Lm = [jnp.where(strict, GB[i][0:C], 0.0) for i in idx]
    Gb = [bf(jnp.where(incl, GB[i][C:2 * C], 0.0)) for i in idx]
    MG = [bf(jnp.concatenate([jnp.where(strict, GK[i][0:C], 0.0), jnp.where(incl, GK[i][C:2 * C], 0.0)], axis=0))
          for i in idx]
    MGV = [_dot(MG[i], bf(Vf[i])) for i in idx]
    T = [eye - Lm[i] for i in idx]
    Pw = [bf(Lm[i]) for i in idx]
    for _ in range(n_double):
        Pw = [bf(_dot(Pw[i], Pw[i])) for i in idx]
        T = [T[i] + _dot(bf(T[i]), Pw[i]) for i in idx]
    U = [_dot(bf(T[i]), bf(-(ARS[i][0:C] + MGV[i][0:C]))) for i in idx]
    Y = [ARS[i][C:2 * C] + _dot(Gb[i], bf(U[i])) + MGV[i][C:2 * C] for i in idx]
    for i, (n, h) in enumerate(chains):
        UV = bf(jnp.concatenate([U[i], Vf[i]], axis=0))
        BK = jnp.concatenate([Bt[i], Kt[i]], axis=0)
        p_tot = p_in[(n + 1) * C - 1:(n + 1) * C, h * DH:(h + 1) * DH]
        s_scr[n, h] = (S0[i] + _dot_tn(UV, BK)) * p_tot

    rows = []
    for n in range(NB):
        ys = []
        for h in range(H):
            Yh = Y[n * H + h]
            yc = Yh - jnp.mean(Yh, axis=-1, keepdims=True)
            var = jnp.mean(yc * yc, axis=-1, keepdims=True)
            ys.append(yc * lax.rsqrt(var + GN_EPS))
        rows.append(jnp.concatenate(ys, axis=-1))
    y = jnp.concatenate(rows, axis=0) * lng_ref[...] + lnb_ref[...]
    ya_ref[...] = ((y + bonus) * g).reshape(NB, C, DA)

    @pl.when(c == n_chunks - 1)
    def _():
        sf_ref[:, 0] = s_scr[...]


def rwkv_mix(pa, n_batch, seq, shift_prev, wkv0, wts, n_par):
    t, ap = pa.shape
    H = wkv0.shape[1]
    DA = H * DH_A
    C = min(RWKV_CHUNK, seq)
    n_chunks = seq // C
    NB = n_par
    G = n_batch // NB
    tri = jnp.asarray(np.tril(np.ones((C, C), np.float32))).astype(BF16)
    hsum = jnp.asarray(np.kron(np.eye(H, dtype=np.float32), np.ones((DH_A, DH_A), np.float32))).astype(BF16)

    def full(shape):
        nd = len(shape)
        return pl.BlockSpec(shape, lambda b, c: (0,) * nd)

    vec = full((1, DA))
    ya, s_fin = pl.pallas_call(
        functools.partial(_rwkv_kernel, NB=NB, C=C, H=H, DH=DH_A, n_chunks=n_chunks),
        out_shape=[jax.ShapeDtypeStruct((NB, t // NB, DA), F32),
                   jax.ShapeDtypeStruct((NB, G, H, DH_A, DH_A), F32)],
        grid=(G, n_chunks),
        in_specs=[pl.BlockSpec((NB, C, ap), lambda b, c: (0, b * n_chunks + c, 0)),
                  pl.BlockSpec((NB, 1, 1, ap), lambda b, c: (0, b, 0, 0)),
                  pl.BlockSpec((NB, 1, H, DH_A, DH_A), lambda b, c: (0, b, 0, 0, 0)),
                  full((1, ap)), vec, full((R_DECAY + R_ICLR, DA)), vec, full((R_DECAY + R_ICLR, DA)),
                  full((R_GATE, DA)), vec, vec, vec, vec, vec, full((C, C)), full((DA, DA))],
        out_specs=[pl.BlockSpec((NB, C, DA), lambda b, c: (0, b * n_chunks + c, 0)),
                   pl.BlockSpec((NB, 1, H, DH_A, DH_A), lambda b, c: (0, b, 0, 0, 0))],
        scratch_shapes=[pltpu.VMEM((NB, H, DH_A, DH_A), F32), pltpu.VMEM((NB, 1, ap), F32)],
        compiler_params=_cparams(("parallel", "arbitrary")),
        name="rwkv7_mix",
    )(pa.reshape(NB, t // NB, ap), shift_prev.reshape(NB, G, 1, ap), wkv0.reshape(NB, G, H, DH_A, DH_A),
      wts["mu"], wts["w0"], wts["wd"], wts["a0"], wts["wa"],
      wts["wg"], wts["key_k"], wts["key_a"], wts["bonus"], wts["lnx_g"], wts["lnx_b"], tri, hsum)
    return ya.reshape(t, DA), s_fin.reshape(wkv0.shape)


GMLP_TILE = 128
GMLP_ROWS = 512


def _gmlp_kernel(u_ref, v_ref, ng_ref, nb_ref, wm_ref, bias_ref, o_ref, *vn_refs, n_sub):
    vf = _gelu(v_ref[...])
    mu = jnp.mean(vf, axis=-1, keepdims=True)
    vc = vf - mu
    var = jnp.mean(vc * vc, axis=-1, keepdims=True)
    vn = vc * lax.rsqrt(var + NORM_EPS) * ng_ref[...] + nb_ref[...]
    for vn_ref in vn_refs:
        vn_ref[...] = vn
    vb = vn.astype(BF16)
    n_h = wm_ref.shape[0]
    cb = vn.shape[1] // n_h
    gu = _gelu(u_ref[...])
    for c in range(n_sub):
        rows = slice(c * GMLP_TILE, (c + 1) * GMLP_TILE)
        s = jnp.concatenate([_dot(wm_ref[h], vb[rows, h * cb:(h + 1) * cb]) for h in range(n_h)], axis=-1)
        o_ref[rows, :] = gu[rows, :] * (s + bias_ref[...])


def gmlp_mix(pu, pv, ng, nb, wm_bf16, bias_tile, want_vn):
    t, db = pu.shape
    n_h = wm_bf16.shape[0]
    rows = GMLP_ROWS if t % GMLP_ROWS == 0 else t
    n_out = 2 if want_vn else 1
    outs = pl.pallas_call(
        functools.partial(_gmlp_kernel, n_sub=rows // GMLP_TILE),
        out_shape=[jax.ShapeDtypeStruct((t, db), F32)] * n_out,
        grid=(t // rows,),
        in_specs=[pl.BlockSpec((rows, db), lambda i: (i, 0)),
                  pl.BlockSpec((rows, db), lambda i: (i, 0)),
                  pl.BlockSpec((1, db), lambda i: (0, 0)),
                  pl.BlockSpec((1, db), lambda i: (0, 0)),
                  pl.BlockSpec((n_h, GMLP_TILE, GMLP_TILE), lambda i: (0, 0, 0)),
                  pl.BlockSpec((GMLP_TILE, db), lambda i: (0, 0))],
        out_specs=[pl.BlockSpec((rows, db), lambda i: (i, 0))] * n_out,
        compiler_params=_cparams(("parallel",)),
        name="gmlp_mix",
    )(pu, pv, ng, nb, wm_bf16, bias_tile)
    return (outs[0], outs[1]) if want_vn else (outs[0], None)


N_SEG = 8


def _rglru_kernel(xb_ref, gy_ref, cprev_ref, h0_ref, cw_ref, cb_ref, gw_ref, gb_ref, lam_ref,
                  yc_ref, ctail_ref, hl_ref, xe_scr, a_scr, b_scr, h_scr, *, TL, DC, pos0, n_tiles):
    l = pl.program_id(1)
    PAD = 8

    @pl.when(l == 0)
    def _():
        xe_scr[0:PAD, :] = cprev_ref[0]
        h_scr[...] = h0_ref[0]

    xe_scr[PAD:PAD + TL, :] = xb_ref[...]
    xc = cb_ref[...] + xe_scr[pl.ds(PAD - (CONV_W - 1), TL), :] * cw_ref[0:1, :]
    for i in range(1, CONV_W):
        xc = xc + xe_scr[pl.ds(PAD - (CONV_W - 1) + i, TL), :] * cw_ref[i:i + 1, :]
    tail = xe_scr[TL:TL + PAD, :]
    ctail_ref[0] = tail
    xe_scr[0:PAD, :] = tail

    gates = _dot(xc.astype(BF16), gw_ref[...]) + gb_ref[...]
    rg = _sigmoid(gates[:, 0:DC])
    ig = _sigmoid(gates[:, DC:2 * DC])
    log_a = -LRU_C * rg * _softplus(-lam_ref[...])
    a = jnp.exp(log_a)
    mult = jnp.sqrt(1.0 - a * a)
    row = lax.broadcasted_iota(jnp.int32, (TL, DC), 0)
    mult = jnp.where(row + (l * TL + pos0) == 0, 1.0, mult)
    b = mult * ig * xc
    n_slab = DC // LANES
    for s in range(n_slab):
        a_scr[s] = a[:, s * LANES:(s + 1) * LANES]
        b_scr[s] = b[:, s * LANES:(s + 1) * LANES]

    seg = TL // N_SEG

    def step(i, carry):
        idx = pl.ds(i, N_SEG, stride=seg) if seg > 1 else pl.ds(0, N_SEG)
        out = []
        for s in range(n_slab):
            hloc, ap = carry[s]
            ai = a_scr[s, idx, :]
            hloc = ai * hloc + b_scr[s, idx, :]
            ap = ap * ai
            b_scr[s, idx, :] = hloc
            a_scr[s, idx, :] = ap
            out.append((hloc, ap))
        return tuple(out)

    lax.fori_loop(0, seg, step,
                  tuple((jnp.zeros((N_SEG, LANES), F32), jnp.ones((N_SEG, LANES), F32)) for _ in range(n_slab)),
                  unroll=min(seg, 8))

    carry = h_scr[...]
    g_act = _gelu(gy_ref[...])
    for j in range(N_SEG):
        rows = slice(j * seg, (j + 1) * seg)
        hloc = jnp.concatenate([b_scr[s, rows, :] for s in range(n_slab)], axis=-1)
        ap = jnp.concatenate([a_scr[s, rows, :] for s in range(n_slab)], axis=-1)
        hj = hloc + ap * carry
        yc_ref[rows, :] = g_act[rows, :] * hj
        carry = hj[seg - 1:seg, :]
    h_scr[...] = carry

    @pl.when(l == n_tiles - 1)
    def _():
        hl_ref[0] = carry


def rglru_mix(xb, gy, n_batch, seq, conv_prev8, h0, pos0, wts):
    t, dc = xb.shape
    TL = 512 if seq % 512 == 0 else seq
    n_tiles = seq // TL

    def full(shape):
        nd = len(shape)
        return pl.BlockSpec(shape, lambda b, l: (0,) * nd)

    yc, ctail, hl = pl.pallas_call(
        functools.partial(_rglru_kernel, TL=TL, DC=dc, pos0=pos0, n_tiles=n_tiles),
        out_shape=[jax.ShapeDtypeStruct((t, dc), F32), jax.ShapeDtypeStruct((n_batch, 8, dc), F32),
                   jax.ShapeDtypeStruct((n_batch, 1, dc), F32)],
        grid=(n_batch, n_tiles),
        in_specs=[pl.BlockSpec((TL, dc), lambda b, l: (b * n_tiles + l, 0)),
                  pl.BlockSpec((TL, dc), lambda b, l: (b * n_tiles + l, 0)),
                  pl.BlockSpec((1, 8, dc), lambda b, l: (b, 0, 0)),
                  pl.BlockSpec((1, 1, dc), lambda b, l: (b, 0, 0)),
                  full((CONV_W, dc)), full((1, dc)), full((dc, 2 * dc)), full((1, 2 * dc)), full((1, dc))],
        out_specs=[pl.BlockSpec((TL, dc), lambda b, l: (b * n_tiles + l, 0)),
                   pl.BlockSpec((1, 8, dc), lambda b, l: (b, 0, 0)),
                   pl.BlockSpec((1, 1, dc), lambda b, l: (b, 0, 0))],
        scratch_shapes=[pltpu.VMEM((TL + 8, dc), F32), pltpu.VMEM((dc // LANES, TL, LANES), F32),
                        pltpu.VMEM((dc // LANES, TL, LANES), F32), pltpu.VMEM((1, dc), F32)],
        compiler_params=_cparams(("parallel", "arbitrary")),
        name="rglru_mix",
    )(xb, gy, conv_prev8, h0.reshape(n_batch, 1, dc), wts["conv_w"], wts["conv_b"], wts["gate_w"], wts["gate_b"],
      wts["lam"])
    return yc, ctail[:, 8 - (CONV_W - 1):, :], hl.reshape(n_batch, dc)


def _t5_bucket(dist):
    dist = np.asarray(dist)
    max_exact = N_BUCKETS // 2
    scaled = np.log(np.maximum(dist, 1) / max_exact) / math.log(BUCKET_MAX_DIST / max_exact)
    large = np.minimum(max_exact + (scaled * (N_BUCKETS - max_exact)).astype(np.int32), N_BUCKETS - 1)
    return np.where(dist < max_exact, dist, large).astype(np.int32)


def _dist_table(rel_bias, max_dist):
    dist = np.arange(max_dist + 1)
    count = np.zeros(max_dist + 1, np.float32)
    for window, dil in DILATED:
        count += ((dist % dil == 0) & (dist <= window)).astype(np.float32)
    logcnt = np.where(count > 0, np.log(np.maximum(count, 1.0)), 0.0).astype(np.float32)
    tab = jnp.take(rel_bias, jnp.asarray(_t5_bucket(dist)), axis=0) + jnp.asarray(logcnt)[:, None]
    return jnp.where(jnp.asarray(count > 0)[:, None], tab, NEG_BIG)


def _toeplitz_tiles(tab, n_pos, n_neg, T):
    D, H = tab.shape
    span = T * n_pos
    assert D >= span
    n_col = span + T * n_neg + T - 1
    ext = jnp.concatenate([jnp.flip(tab[:span], axis=0), jnp.full((n_col + 1 - span, H), NEG_BIG, F32)], axis=0)
    ext = jnp.transpose(ext)
    skew = jnp.tile(ext, (1, T))[:, :T * n_col].reshape(H, T, n_col)
    tiles = [skew[:, :, span - 1 - T * dd: span - 1 - T * dd + T] for dd in range(-n_neg, n_pos)]
    return jnp.stack(tiles, axis=1)


def _attn_prompt_kernel(q_ref, k_ref, v_ref, bias_ref, o_ref, kb_scr, vb_scr, *, E, SUB, NS):
    qi = pl.program_id(2)
    TQ = NS * SUB

    @pl.when(qi == 0)
    def _():
        kb_scr[...] = k_ref[0].astype(BF16)
        vb_scr[...] = v_ref[0].astype(BF16)

    lane = lax.broadcasted_iota(jnp.int32, (SUB, 2 * E), 1)
    q2 = []
    for rs in range(NS):
        q = q_ref[0, rs * SUB:(rs + 1) * SUB, :] * (E ** -0.5 * LOG2E)
        q2.append(jnp.concatenate([jnp.where(lane < E, q, 0.0), jnp.where(lane >= E, q, 0.0)], axis=0).astype(BF16))

    def block(i, carry, diagonal):
        j = qi - i
        koff = pl.multiple_of(j * TQ, TQ)
        out = []
        for rs in range(NS):
            n_cs = rs + 1 if diagonal else NS
            kj = kb_scr[pl.ds(koff, n_cs * SUB), :]
            vj = vb_scr[pl.ds(koff, n_cs * SUB), :]
            m, l, acc = carry[rs]
            s = _dot_nt(q2[rs], kj)
            parts = []
            for cs in range(n_cs):
                dd = i * NS + (rs - cs + NS - 1)
                bias = jnp.concatenate([bias_ref[0, dd], bias_ref[1, dd]], axis=0)
                parts.append(s[:, cs * SUB:(cs + 1) * SUB] + bias)
            mx = parts[0]
            for part in parts[1:]:
                mx = jnp.maximum(mx, part)
            m_new = jnp.maximum(m, jnp.max(mx, axis=-1, keepdims=True))
            alpha = jnp.exp2(m - m_new)
            ps = [jnp.exp2(part - m_new) for part in parts]
            psum = ps[0]
            for pexp in ps[1:]:
                psum = psum + pexp
            l = alpha * l + psum
            acc = alpha * acc + _dot(jnp.concatenate(ps, axis=-1).astype(BF16), vj)
            out.append((m_new, l, acc))
        return tuple(out)

    init = tuple((jnp.full((2 * SUB, SUB), NEG_BIG, F32), jnp.zeros((2 * SUB, SUB), F32),
                  jnp.zeros((2 * SUB, 2 * E), F32)) for _ in range(NS))
    first = block(0, init, True)
    res = lax.fori_loop(1, qi + 1, lambda i, carry: block(i, carry, False), first)
    for rs in range(NS):
        m, l, acc = res[rs]
        o = acc / jnp.sum(l, axis=-1, keepdims=True)
        o_ref[0, rs * SUB:(rs + 1) * SUB, :] = jnp.where(lane < E, o[0:SUB], o[SUB:2 * SUB])


def attn_prompt(q, k, v, bias_tiles, n_batch, seq):
    hd = q.shape[-1]
    E = hd // H_D
    SUB = ATT_TILE
    NS = ATT_SUBTILES
    TQ = SUB * NS
    nq = seq // TQ
    nt = bias_tiles.shape[1]
    return pl.pallas_call(
        functools.partial(_attn_prompt_kernel, E=E, SUB=SUB, NS=NS),
        out_shape=jax.ShapeDtypeStruct((n_batch, seq, hd), F32),
        grid=(H_D // 2, n_batch, nq),
        in_specs=[pl.BlockSpec((1, TQ, 2 * E), lambda hp, b, i: (b, i, hp)),
                  pl.BlockSpec((1, seq, 2 * E), lambda hp, b, i: (b, 0, hp)),
                  pl.BlockSpec((1, seq, 2 * E), lambda hp, b, i: (b, 0, hp)),
                  pl.BlockSpec((2, nt, SUB, SUB), lambda hp, b, i: (hp, 0, 0, 0))],
        out_specs=pl.BlockSpec((1, TQ, 2 * E), lambda hp, b, i: (b, i, hp)),
        scratch_shapes=[pltpu.VMEM((seq, 2 * E), BF16), pltpu.VMEM((seq, 2 * E), BF16)],
        compiler_params=_cparams(("arbitrary", "arbitrary", "arbitrary")),
        name="dilated_attn_prompt",
    )(q, k, v, bias_tiles)


def _attn_sample_kernel(q_ref, kn_ref, vn_ref, ck_hbm, cv_hbm, bo_ref, bn_ref, o_ref, kbuf, vbuf, sem,
                        *, E, S, n_batch):
    b = pl.program_id(0)

    def copies(bb, slot):
        out = []
        for h in range(H_D):
            out.append(pltpu.make_async_copy(ck_hbm.at[bb, :, h, :], kbuf.at[slot, h], sem.at[slot, 0]))
            out.append(pltpu.make_async_copy(cv_hbm.at[bb, :, h, :], vbuf.at[slot, h], sem.at[slot, 1]))
        return out

    @pl.when(b == 0)
    def _():
        for cp in copies(0, 0):
            cp.start()

    @pl.when(b + 1 < n_batch)
    def _():
        for cp in copies(b + 1, (b + 1) % 2):
            cp.start()

    slot = b % 2
    for cp in copies(b, slot):
        cp.wait()

    lane = lax.broadcasted_iota(jnp.int32, (S, 2 * E), 1)
    NPAD = bn_ref.shape[-1]
    outs = []
    for hp in range(H_D // 2):
        sl = slice(hp * 2 * E, (hp + 1) * 2 * E)
        q = q_ref[0, :, sl] * (E ** -0.5)
        q2 = jnp.concatenate([jnp.where(lane < E, q, 0.0), jnp.where(lane >= E, q, 0.0)], axis=0).astype(BF16)
        zpad = jnp.zeros((NPAD - S, 2 * E), F32)
        kn = jnp.concatenate([kn_ref[0, :, sl], zpad], axis=0).astype(BF16)
        vn = jnp.concatenate([vn_ref[0, :, sl], zpad], axis=0).astype(BF16)
        kc = jnp.concatenate([kbuf[slot, 2 * hp], kbuf[slot, 2 * hp + 1]], axis=-1).astype(BF16)
        vc = jnp.concatenate([vbuf[slot, 2 * hp], vbuf[slot, 2 * hp + 1]], axis=-1).astype(BF16)
        s_old = _dot_nt(q2, kc) + jnp.concatenate([bo_ref[2 * hp], bo_ref[2 * hp + 1]], axis=0)
        s_new = _dot_nt(q2, kn) + jnp.concatenate([bn_ref[2 * hp], bn_ref[2 * hp + 1]], axis=0)
        m = jnp.maximum(jnp.max(s_old, axis=-1, keepdims=True), jnp.max(s_new, axis=-1, keepdims=True))
        p_old = jnp.exp(s_old - m)
        p_new = jnp.exp(s_new - m)
        l = jnp.sum(p_old, axis=-1, keepdims=True) + jnp.sum(p_new, axis=-1, keepdims=True)
        acc = _dot(p_old.astype(BF16), vc) + _dot(p_new.astype(BF16), vn)
        o = acc / l
        outs.append(jnp.where(lane < E, o[0:S], o[S:2 * S]))
    o_ref[0] = jnp.concatenate(outs, axis=-1)


def attn_sample(q, k_new, v_new, cache_k, cache_v, bias_old, bias_new):
    n_batch, S, hd = q.shape
    W = cache_k.shape[1]
    E = hd // H_D
    NPAD = bias_new.shape[-1]
    return pl.pallas_call(
        functools.partial(_attn_sample_kernel, E=E, S=S, n_batch=n_batch),
        out_shape=jax.ShapeDtypeStruct((n_batch, S, hd), F32),
        grid=(n_batch,),
        in_specs=[pl.BlockSpec((1, S, hd), lambda b: (b, 0, 0)),
                  pl.BlockSpec((1, S, hd), lambda b: (b, 0, 0)),
                  pl.BlockSpec((1, S, hd), lambda b: (b, 0, 0)),
                  pl.BlockSpec(memory_space=pl.ANY),
                  pl.BlockSpec(memory_space=pl.ANY),
                  pl.BlockSpec((H_D, S, W), lambda b: (0, 0, 0)),
                  pl.BlockSpec((H_D, S, NPAD), lambda b: (0, 0, 0))],
        out_specs=pl.BlockSpec((1, S, hd), lambda b: (b, 0, 0)),
        scratch_shapes=[pltpu.VMEM((2, H_D, W, E), F32), pltpu.VMEM((2, H_D, W, E), F32),
                        pltpu.SemaphoreType.DMA((2, 2))],
        compiler_params=_cparams(("arbitrary",)),
        name="dilated_attn_sample",
    )(q, k_new, v_new, cache_k, cache_v, bias_old, bias_new)


def _even_weights(j, w_in_even, w_out_even, shift_mu, decay_w0, decay_up, iclr_a0, iclr_up, gate_up, key_k, key_a,
                  bonus_r_k, lnx_g, lnx_b, sgu_norm_g, sgu_norm_b, sgu_w, sgu_b):
    da = decay_w0.shape[1]
    zeros_d = jnp.zeros((R_ICLR, da), F32)
    zeros_i = jnp.zeros((R_DECAY, da), F32)
    return dict(
        w_in=w_in_even[j].astype(BF16),
        w_out_a=w_out_even[j, :da].astype(BF16), w_out_b=w_out_even[j, da:].astype(BF16),
        mu=shift_mu[j].reshape(1, -1), w0=decay_w0[j].reshape(1, -1), a0=iclr_a0[j].reshape(1, -1),
        wd=jnp.concatenate([decay_up[j], zeros_d], axis=0), wa=jnp.concatenate([zeros_i, iclr_up[j]], axis=0),
        wg=gate_up[j], key_k=key_k[j].reshape(1, -1), key_a=key_a[j].reshape(1, -1),
        bonus=bonus_r_k[j].reshape(1, -1), lnx_g=lnx_g[j].reshape(1, -1), lnx_b=lnx_b[j].reshape(1, -1),
        ng=sgu_norm_g[j].reshape(1, -1), nb=sgu_norm_b[j].reshape(1, -1), sgu_w=sgu_w[j], sgu_b=sgu_b[j])


def _gmlp_tables(sgu_w, sgu_b, chunk):
    reps = GMLP_TILE // chunk
    n_h = sgu_w.shape[0]
    cb = None
    wm = sgu_w[:, :chunk, :chunk] * jnp.asarray(np.tril(np.ones((chunk, chunk), np.float32)))
    if reps > 1:
        eye = jnp.asarray(np.eye(reps, dtype=np.float32))
        wm = jnp.einsum("ab,hts->hatbs", eye, wm).reshape(n_h, GMLP_TILE, GMLP_TILE)
    bias = jnp.tile(jnp.transpose(sgu_b[:, :chunk]), (reps, 1))
    return wm.astype(BF16), bias


def _even_layer(x, n_batch, seq, chunk, n_par, want_vn, shift_prev, wkv0, norm_g, ew):
    pa, pu, pv = norm_matmul(x, norm_g, ew["w_in"], (ew["mu"].shape[1], ew["ng"].shape[1], ew["ng"].shape[1]))
    ya, wkv = rwkv_mix(pa, n_batch, seq, shift_prev, wkv0, ew, n_par)
    wm, bias = _gmlp_tables(ew["sgu_w"], ew["sgu_b"], chunk)
    cb = pu.shape[1] // wm.shape[0]
    bias_tile = jnp.repeat(bias, cb, axis=1)
    yb, vn = gmlp_mix(pu, pv, ew["ng"], ew["nb"], wm, bias_tile, want_vn)
    x = proj_residual(x, ya, yb, ew["w_out_a"], ew["w_out_b"])
    last = pa.reshape(n_batch, seq, -1)[:, -1]
    return x, last, wkv, vn


def _odd_weights(j, w_in_odd, w_out_odd, conv_w, conv_b, rgate_w, rgate_b, igate_w, igate_b, lru_lambda):
    dc = conv_b.shape[1]
    eye = jnp.asarray(np.eye(H_C, dtype=np.float32))

    def blockdiag(w):
        dh = w.shape[-1]
        return jnp.einsum("ab,aij->aibj", eye, w).reshape(H_C * dh, H_C * dh)

    return dict(
        w_in=w_in_odd[j].astype(BF16),
        w_out_c=w_out_odd[j, :dc].astype(BF16), w_out_d=w_out_odd[j, dc:].astype(BF16),
        conv_w=conv_w[j], conv_b=conv_b[j].reshape(1, -1),
        gate_w=jnp.concatenate([blockdiag(rgate_w[j]), blockdiag(igate_w[j])], axis=1).astype(BF16),
        gate_b=jnp.concatenate([rgate_b[j], igate_b[j]]).reshape(1, -1),
        lam=lru_lambda[j].reshape(1, -1))


def _odd_layer(x, n_batch, seq, conv_prev, h0, pos0, caches, dist_tab, norm_g, ow):
    dc = ow["lam"].shape[1]
    gy, xb, q, k, v = norm_matmul(x, norm_g, ow["w_in"], (dc,) * 5)
    conv_prev8 = jnp.pad(conv_prev, ((0, 0), (8 - (CONV_W - 1), 0), (0, 0)))
    yc, conv_last, h_last = rglru_mix(xb, gy, n_batch, seq, conv_prev8, h0, pos0, ow)
    hd = q.shape[1]
    q3, k3, v3 = (a.reshape(n_batch, seq, hd) for a in (q, k, v))
    if caches is None:
        tiles = _toeplitz_tiles(dist_tab * LOG2E, seq // ATT_TILE, ATT_SUBTILES - 1, ATT_TILE)
        o = attn_prompt(q3, k3, v3, tiles, n_batch, seq)
    else:
        cache_k, cache_v = caches
        W = cache_k.shape[1]
        NPAD = 128
        tab_t = jnp.flip(jnp.transpose(dist_tab[:W + seq]), axis=1)
        b_old = jnp.stack([tab_t[:, seq - 1 - j:seq - 1 - j + W] for j in range(seq)], axis=1)
        d_new = np.arange(seq)[:, None] - np.arange(NPAD)[None, :]
        ok_new = (d_new >= 0) & (np.arange(NPAD)[None, :] < seq)
        b_new = jnp.take(dist_tab, jnp.asarray(np.maximum(d_new, 0)), axis=0)
        b_new = jnp.transpose(jnp.where(jnp.asarray(ok_new)[..., None], b_new, NEG_BIG), (2, 0, 1))
        o = attn_sample(q3, k3, v3, cache_k, cache_v, b_old, b_new)
    x = proj_residual(x, yc, o.reshape(n_batch * seq, hd), ow["w_out_c"], ow["w_out_d"])
    e = hd // H_D
    return x, conv_last, h_last, k3.reshape(n_batch, seq, H_D, e), v3.reshape(n_batch, seq, H_D, e)


def _moe_weights(l, router_group_w, router_group_b, router_expert_w, router_expert_b, exp_w_gate, exp_w_up,
                 exp_w_down):
    d = router_group_w.shape[1]
    n_used = N_GROUPS + router_expert_w.shape[2]
    rw = jnp.concatenate([router_group_w[l], router_expert_w[l], jnp.zeros((d, ROUTER_LANES - n_used), F32)], axis=1)
    rb = jnp.concatenate([router_group_b[l], router_expert_b[l], jnp.zeros((ROUTER_LANES - n_used,), F32)])
    return dict(rw=rw, rb=rb.reshape(1, -1), wg=exp_w_gate[l].astype(BF16), wu=exp_w_up[l].astype(BF16),
                wd=exp_w_down[l].astype(BF16))


def kernel(x_prompt, x_sample, state_wkv, state_shift, state_conv, state_rglru, cache_k, cache_v, norm_mix, norm_ffn, norm_final, w_in_even, w_out_even, shift_mu, decay_w0, decay_up, iclr_a0, iclr_up, gate_up, key_k, key_a, bonus_r_k, lnx_g, lnx_b, sgu_norm_g, sgu_norm_b, sgu_w, sgu_b, w_in_odd, w_out_odd, conv_w, conv_b, rgate_w, rgate_b, igate_w, igate_b, lru_lambda, rel_bias, router_group_w, router_group_b, router_expert_w, router_expert_b, exp_w_gate, exp_w_up, exp_w_down):
    B, L, D = x_prompt.shape
    DB, S, _ = x_sample.shape
    depth = norm_mix.shape[0]
    xp = x_prompt.reshape(B * L, D)
    xs = x_sample.reshape(DB * S, D)
    W = cache_k.shape[2]
    dist_tab = _dist_table(rel_bias, max(L, W + S) - 1)

    sorted_buf = None
    wkv_p, shift_p, conv_p, lru_p, k_p, v_p = [], [], [], [], [], []
    wkv_s, shift_s, chunkv_s, conv_s, lru_s, k_s, v_s = [], [], [], [], [], [], []
    for l in range(depth):
        j = l // 2
        if l % 2 == 0:
            ew = _even_weights(j, w_in_even, w_out_even, shift_mu, decay_w0, decay_up, iclr_a0, iclr_up, gate_up,
                               key_k, key_a, bonus_r_k, lnx_g, lnx_b, sgu_norm_g, sgu_norm_b, sgu_w, sgu_b)
            a_proj = ew["mu"].shape[1]
            h_a = state_wkv.shape[2]
            xp, sh, wkv, _ = _even_layer(xp, B, L, GMLP_TILE, RWKV_PAR_PROMPT, False, jnp.zeros((B, a_proj), F32),
                                         jnp.zeros((B, h_a, DH_A, DH_A), F32), norm_mix[l], ew)
            xs, sh_s, wkv_s_new, vn_s = _even_layer(xs, DB, S, S, RWKV_PAR_SAMPLE, True, state_shift[j], state_wkv[j], norm_mix[l], ew)
            wkv_p.append(wkv)
            shift_p.append(sh)
            wkv_s.append(wkv_s_new)
            shift_s.append(sh_s)
            chunkv_s.append(vn_s.reshape(DB, S, -1))
        else:
            ow = _odd_weights(j, w_in_odd, w_out_odd, conv_w, conv_b, rgate_w, rgate_b, igate_w, igate_b, lru_lambda)
            dc = ow["lam"].shape[1]
            xp, cv, hl, kr, vr = _odd_layer(xp, B, L, jnp.zeros((B, CONV_W - 1, dc), F32), jnp.zeros((B, dc), F32),
                                            0, None, dist_tab, norm_mix[l], ow)
            xs, cv_s, hl_s, kr_s, vr_s = _odd_layer(xs, DB, S, state_conv[j], state_rglru[j], PAST_LEN,
                                                    (cache_k[j], cache_v[j]), dist_tab, norm_mix[l], ow)
            conv_p.append(cv)
            lru_p.append(hl)
            k_p.append(kr)
            v_p.append(vr)
            conv_s.append(cv_s)
            lru_s.append(hl_s)
            k_s.append(kr_s)
            v_s.append(vr_s)
        mw = _moe_weights(l, router_group_w, router_group_b, router_expert_w, router_expert_b, exp_w_gate, exp_w_up,
                          exp_w_down)
        xp, sorted_buf = moe_layer_sparse(xp, norm_ffn[l], mw["rw"], mw["rb"], mw["wg"], mw["wu"], mw["wd"],
                                          final_g=norm_final if l == depth - 1 else None, sorted_buf=sorted_buf)
        xs = moe_layer(xs, norm_ffn[l], mw["rw"], mw["rb"], mw["wg"], mw["wu"], mw["wd"])
    y_prompt = xp.reshape(B, L, D)
    y_sample = rmsnorm_call(xs, norm_final).reshape(DB, S, D)
    return (y_prompt, y_sample,
            jnp.stack(wkv_p), jnp.stack(shift_p), jnp.stack(conv_p), jnp.stack(lru_p), jnp.stack(k_p), jnp.stack(v_p),
            jnp.stack(wkv_s), jnp.stack(shift_s), jnp.stack(chunkv_s), jnp.stack(conv_s), jnp.stack(lru_s),
            jnp.stack(k_s), jnp.stack(v_s))
```

```python
import functools
import math

import numpy as np
import jax
import jax.numpy as jnp
from jax import lax
from jax.experimental import pallas as pl
from jax.experimental.pallas import tpu as pltpu

F32 = jnp.float32
BF16 = jnp.bfloat16
HI = lax.Precision.HIGHEST

PAST_LEN = 8192
DH_A = 64
R_DECAY = 64
R_ICLR = 64
R_GATE = 128
GN_EPS = 64e-5
H_B = 4
H_C = 8
CONV_W = 4
LRU_C = 8.0
H_D = 8
DILATED = ((128, 1), (512, 4), (2048, 16))
N_BUCKETS = 32
BUCKET_MAX_DIST = 2048
NEG_BIG = -1e30
N_GROUPS = 4
EXP_PER_GROUP = 4
NORM_EPS = 1e-6
LOG2E = math.log2(math.e)

VMEM_LIMIT = 56 * 1024 * 1024
RWKV_CHUNK = 64
RWKV_PAR_PROMPT = 4
RWKV_PAR_SAMPLE = 8
ATT_TILE = 128
ATT_SUBTILES = 4
LANES = 128
SUBLANES = 8


def _cparams(sem):
    return pltpu.CompilerParams(dimension_semantics=sem, vmem_limit_bytes=VMEM_LIMIT)


def _dot(a, b, precision=None):
    return jnp.dot(a, b, preferred_element_type=F32, precision=precision)


def _dot_nt(a, b, precision=None):
    return lax.dot_general(a, b, (((1,), (1,)), ((), ())), preferred_element_type=F32, precision=precision)


def _dot_tn(a, b, precision=None):
    return lax.dot_general(a, b, (((0,), (0,)), ((), ())), preferred_element_type=F32, precision=precision)


def _split_bf16(x, n):
    parts = []
    for _ in range(n):
        hi = x.astype(BF16)
        parts.append(hi)
        x = x - hi.astype(F32)
    return parts


def _mp_dot(dotfn, a, b, passes):
    if passes == 1:
        return dotfn(a.astype(BF16), b.astype(BF16))
    a_hi, a_lo = _split_bf16(a, 2)
    b_hi, b_lo = _split_bf16(b, 2)
    return dotfn(a_hi, b_hi) + (dotfn(a_hi, b_lo) + dotfn(a_lo, b_hi))


def _dot_exact_rhs(a, b_bf16, n_split):
    parts = _split_bf16(a, n_split)
    acc = _dot(parts[0], b_bf16)
    for part in parts[1:]:
        acc = acc + _dot(part, b_bf16)
    return acc


def _dot_exact_lhs(a_bf16, b, n_split):
    parts = _split_bf16(b, n_split)
    acc = _dot(a_bf16, parts[0])
    for part in parts[1:]:
        acc = acc + _dot(a_bf16, part)
    return acc


def _softplus(x):
    return jnp.maximum(x, 0.0) + jnp.log(1.0 + jnp.exp(-jnp.abs(x)))


def _sigmoid(x):
    return 1.0 / (1.0 + jnp.exp(-x))


def _gelu(x):
    c = math.sqrt(2.0 / math.pi)
    return 0.5 * x * (1.0 + jnp.tanh(c * (x + 0.044715 * (x * x * x))))


def _row_tile(t, pref=512):
    return pref if t % pref == 0 else t


def _norm_matmul_kernel(x_ref, g_ref, w_ref, *out_refs, splits):
    x = x_ref[...]
    ms = jnp.mean(x * x, axis=-1, keepdims=True)
    h = (x * lax.rsqrt(ms + NORM_EPS) * g_ref[...]).astype(BF16)
    off = 0
    for o_ref, n in zip(out_refs, splits):
        o_ref[...] = _dot(h, w_ref[:, off:off + n])
        off += n


def norm_matmul(x, g, w_bf16, splits):
    t, d = x.shape
    n = w_bf16.shape[1]
    tm = _row_tile(t)
    return pl.pallas_call(
        functools.partial(_norm_matmul_kernel, splits=splits),
        out_shape=[jax.ShapeDtypeStruct((t, s), F32) for s in splits],
        grid=(t // tm,),
        in_specs=[pl.BlockSpec((tm, d), lambda i: (i, 0)),
                  pl.BlockSpec((1, d), lambda i: (0, 0)),
                  pl.BlockSpec((d, n), lambda i: (0, 0))],
        out_specs=[pl.BlockSpec((tm, s), lambda i: (i, 0)) for s in splits],
        compiler_params=_cparams(("parallel",)),
        name="norm_matmul",
    )(x, g.reshape(1, d), w_bf16)


def _proj_res_kernel(x_ref, a_ref, b_ref, wa_ref, wb_ref, o_ref):
    acc = _dot(a_ref[...].astype(BF16), wa_ref[...]) + _dot(b_ref[...].astype(BF16), wb_ref[...])
    o_ref[...] = x_ref[...] + acc


def proj_residual(x, a, b, wa, wb):
    t, d = x.shape
    tm = _row_tile(t)
    ka, kb = a.shape[1], b.shape[1]
    return pl.pallas_call(
        _proj_res_kernel,
        out_shape=jax.ShapeDtypeStruct((t, d), F32),
        grid=(t // tm,),
        in_specs=[pl.BlockSpec((tm, d), lambda i: (i, 0)),
                  pl.BlockSpec((tm, ka), lambda i: (i, 0)),
                  pl.BlockSpec((tm, kb), lambda i: (i, 0)),
                  pl.BlockSpec((ka, d), lambda i: (0, 0)),
                  pl.BlockSpec((kb, d), lambda i: (0, 0))],
        out_specs=pl.BlockSpec((tm, d), lambda i: (i, 0)),
        compiler_params=_cparams(("parallel",)),
        name="proj_residual",
    )(x, a, b, wa, wb)


def _rmsnorm_kernel(x_ref, g_ref, o_ref):
    x = x_ref[...]
    ms = jnp.mean(x * x, axis=-1, keepdims=True)
    o_ref[...] = x * lax.rsqrt(ms + NORM_EPS) * g_ref[...]


def rmsnorm_call(x, g):
    t, d = x.shape
    tm = _row_tile(t)
    return pl.pallas_call(
        _rmsnorm_kernel,
        out_shape=jax.ShapeDtypeStruct((t, d), F32),
        grid=(t // tm,),
        in_specs=[pl.BlockSpec((tm, d), lambda i: (i, 0)), pl.BlockSpec((1, d), lambda i: (0, 0))],
        out_specs=pl.BlockSpec((tm, d), lambda i: (i, 0)),
        compiler_params=_cparams(("parallel",)),
        name="final_rmsnorm",
    )(x, g.reshape(1, d))


ROUTER_LANES = 128


def _route(xn, rw, rb, lane, n_exp):
    logits = _dot_exact_rhs(xn, rw, 2) + rb
    lg = jnp.where(lane < N_GROUPS, logits, -jnp.inf)
    gm = jnp.max(lg, axis=-1, keepdims=True)
    top_pg = 1.0 / jnp.sum(jnp.exp(lg - gm), axis=-1, keepdims=True)
    grp = jnp.min(jnp.where(lg == gm, lane, ROUTER_LANES), axis=-1, keepdims=True)
    in_grp = (lane >= N_GROUPS) & (lane < N_GROUPS + n_exp) & (((lane - N_GROUPS) // EXP_PER_GROUP) == grp)
    le = jnp.where(in_grp, logits, -jnp.inf)
    t1 = jnp.max(le, axis=-1, keepdims=True)
    i1 = jnp.min(jnp.where(le == t1, lane, ROUTER_LANES), axis=-1, keepdims=True)
    le2 = jnp.where(lane == i1, -jnp.inf, le)
    t2 = jnp.max(le2, axis=-1, keepdims=True)
    i2 = jnp.min(jnp.where(le2 == t2, lane, ROUTER_LANES), axis=-1, keepdims=True)
    ex = jnp.exp(t2 - t1)
    w1 = 1.0 / (1.0 + ex)
    return i1, i2, w1 * top_pg, (ex * w1) * top_pg


def _moe_kernel(x_ref, g_ref, rw_ref, rb_ref, wg_ref, wu_ref, wd_ref, o_ref, xn_scr, gate_scr, acc_scr, *, n_exp):
    e = pl.program_id(1)
    tm = x_ref.shape[0]
    lane = lax.broadcasted_iota(jnp.int32, (tm, ROUTER_LANES), 1)

    @pl.when(e == 0)
    def _():
        x = x_ref[...]
        ms = jnp.mean(x * x, axis=-1, keepdims=True)
        xn = x * lax.rsqrt(ms + NORM_EPS) * g_ref[...]
        xn_scr[...] = xn.astype(BF16)
        i1, i2, g1, g2 = _route(xn, rw_ref[...], rb_ref[...], lane, n_exp)
        gate_scr[...] = jnp.where(lane == i1, g1, 0.0) + jnp.where(lane == i2, g2, 0.0)
        acc_scr[...] = jnp.zeros_like(acc_scr)

    xn = xn_scr[...]
    hg = _dot(xn, wg_ref[0])
    hu = _dot(xn, wu_ref[0])
    gcol = jnp.sum(jnp.where(lane == e + N_GROUPS, gate_scr[...], 0.0), axis=-1, keepdims=True)
    hid = hg * _sigmoid(hg) * hu * gcol
    acc_scr[...] += _dot(hid.astype(BF16), wd_ref[0])

    @pl.when(e == n_exp - 1)
    def _():
        o_ref[...] = x_ref[...] + acc_scr[...]


def moe_layer(x, g, rw, rb, wg, wu, wd):
    t, d = x.shape
    n_exp, _, f = wg.shape
    tm = _row_tile(t)
    return pl.pallas_call(
        functools.partial(_moe_kernel, n_exp=n_exp),
        out_shape=jax.ShapeDtypeStruct((t, d), F32),
        grid=(t // tm, n_exp),
        in_specs=[pl.BlockSpec((tm, d), lambda i, e: (i, 0)),
                  pl.BlockSpec((1, d), lambda i, e: (0, 0)),
                  pl.BlockSpec((d, ROUTER_LANES), lambda i, e: (0, 0)),
                  pl.BlockSpec((1, ROUTER_LANES), lambda i, e: (0, 0)),
                  pl.BlockSpec((1, d, f), lambda i, e: (e, 0, 0)),
                  pl.BlockSpec((1, d, f), lambda i, e: (e, 0, 0)),
                  pl.BlockSpec((1, f, d), lambda i, e: (e, 0, 0))],
        out_specs=pl.BlockSpec((tm, d), lambda i, e: (i, 0)),
        scratch_shapes=[pltpu.VMEM((tm, d), BF16), pltpu.VMEM((tm, ROUTER_LANES), F32), pltpu.VMEM((tm, d), F32)],
        compiler_params=_cparams(("parallel", "arbitrary")),
        name="hier_moe",
    )(x, g.reshape(1, d), rw, rb, wg, wu, wd)


MOE_ROW_TILE = 512
MOE_COPY_CHUNK = 256
MOE_COMBINE_TILE = 256


def _router_kernel(x_ref, g_ref, rw_ref, rb_ref, tri_ref, gate_ref, info_ref, cnt_ref, base_scr, *, n_exp, n_tiles):
    i = pl.program_id(0)
    tm = x_ref.shape[0]
    lane = lax.broadcasted_iota(jnp.int32, (tm, ROUTER_LANES), 1)

    @pl.when(i == 0)
    def _():
        base_scr[...] = jnp.zeros_like(base_scr)

    x = x_ref[...]
    ms = jnp.mean(x * x, axis=-1, keepdims=True)
    xn = x * lax.rsqrt(ms + NORM_EPS) * g_ref[...]
    i1, i2, g1, g2 = _route(xn, rw_ref[...], rb_ref[...], lane, n_exp)
    chosen = jnp.where((lane == i1) | (lane == i2), 1.0, 0.0)
    before = _dot(tri_ref[...], chosen.astype(BF16)) + base_scr[...]
    r1 = jnp.sum(jnp.where(lane == i1, before, 0.0), axis=-1, keepdims=True)
    r2 = jnp.sum(jnp.where(lane == i2, before, 0.0), axis=-1, keepdims=True)
    base_scr[...] += jnp.sum(chosen, axis=0, keepdims=True)
    gate_ref[...] = jnp.where(lane == 0, g1, 0.0) + jnp.where(lane == 1, g2, 0.0)
    e1 = (i1 - N_GROUPS).astype(F32)
    e2 = (i2 - N_GROUPS).astype(F32)
    info_ref[...] = (jnp.where(lane == 0, e1, 0.0) + jnp.where(lane == 1, e2, 0.0)
                     + jnp.where(lane == 2, r1, 0.0) + jnp.where(lane == 3, r2, 0.0))

    @pl.when(i == n_tiles - 1)
    def _():
        cnt_ref[...] = base_scr[...]


def _scatter_rows_kernel(pos0_ref, pos1_ref, x_ref, xs_in_hbm, xs_hbm, stage, sem, *, CH, n_chunks):
    del xs_in_hbm
    c = pl.program_id(0)
    slot = c % 2
    x = x_ref[...]
    for j in range(SUBLANES):
        stage[slot, pl.ds(j, CH, stride=SUBLANES), :] = x[:, j * LANES:(j + 1) * LANES]

    def body(r, carry):
        t = c * CH + r
        src = stage.at[slot, pl.ds(pl.multiple_of(r * SUBLANES, SUBLANES), SUBLANES), :]
        d0 = pl.multiple_of(pos0_ref[t], SUBLANES)
        d1 = pl.multiple_of(pos1_ref[t], SUBLANES)
        pltpu.make_async_copy(src, xs_hbm.at[pl.ds(d0, SUBLANES), :], sem.at[slot]).start(priority=0)
        pltpu.make_async_copy(src, xs_hbm.at[pl.ds(d1, SUBLANES), :], sem.at[slot]).start(priority=1)
        return carry

    lax.fori_loop(0, CH, body, 0, unroll=8)

    def drain(s):
        pltpu.make_async_copy(stage.at[s], xs_hbm.at[pl.ds(0, CH * SUBLANES), :], sem.at[s]).wait()
        pltpu.make_async_copy(stage.at[s], xs_hbm.at[pl.ds(0, CH * SUBLANES), :], sem.at[s]).wait()

    @pl.when(c > 0)
    def _():
        drain(1 - slot)

    @pl.when(c == n_chunks - 1)
    def _():
        drain(slot)


def _tile_rows_to_matrix(ref, lead, n_rows):
    return jnp.concatenate([ref[lead + (pl.ds(j, n_rows, stride=SUBLANES), slice(None))] for j in range(SUBLANES)],
                           axis=-1)


def _expert_kernel(te_ref, nv_ref, xs_ref, g_ref, wg_ref, wu_ref, wd_ref, y_ref, *, TM):
    @pl.when(pl.program_id(0) < nv_ref[0])
    def _():
        x = _tile_rows_to_matrix(xs_ref, (), TM)
        ms = jnp.mean(x * x, axis=-1, keepdims=True)
        xn = (x * lax.rsqrt(ms + NORM_EPS) * g_ref[...]).astype(BF16)
        hg = _dot(xn, wg_ref[0])
        hu = _dot(xn, wu_ref[0])
        hid = hg * _sigmoid(hg) * hu
        y = _dot(hid.astype(BF16), wd_ref[0])
        for j in range(SUBLANES):
            y_ref[pl.ds(j, TM, stride=SUBLANES), :] = y[:, j * LANES:(j + 1) * LANES]

    @pl.when(pl.program_id(0) >= nv_ref[0])
    def _():
        y_ref[...] = jnp.zeros_like(y_ref)


def _combine_kernel(pos0_ref, pos1_ref, x_ref, gate_ref, fg_ref, y_hbm, o_ref, ybuf, sem, *, TC, n_tiles, final_norm):
    i = pl.program_id(0)

    def issue(tile, slot):
        def body(r, carry):
            t = tile * TC + r
            dst = pl.ds(pl.multiple_of(r * SUBLANES, SUBLANES), SUBLANES)
            s0 = pl.multiple_of(pos0_ref[t], SUBLANES)
            s1 = pl.multiple_of(pos1_ref[t], SUBLANES)
            pltpu.make_async_copy(y_hbm.at[pl.ds(s0, SUBLANES), :], ybuf.at[slot, 0, dst, :],
                                  sem.at[slot]).start(priority=0)
            pltpu.make_async_copy(y_hbm.at[pl.ds(s1, SUBLANES), :], ybuf.at[slot, 1, dst, :],
                                  sem.at[slot]).start(priority=1)
            return carry
        lax.fori_loop(0, TC, body, 0, unroll=8)

    @pl.when(i == 0)
    def _():
        issue(0, 0)

    @pl.when(i + 1 < n_tiles)
    def _():
        issue(i + 1, (i + 1) % 2)

    slot = i % 2
    pltpu.make_async_copy(y_hbm.at[pl.ds(0, TC * SUBLANES), :], ybuf.at[slot, 0], sem.at[slot]).wait()
    pltpu.make_async_copy(y_hbm.at[pl.ds(0, TC * SUBLANES), :], ybuf.at[slot, 1], sem.at[slot]).wait()
    gate = gate_ref[...]
    y0 = _tile_rows_to_matrix(ybuf, (slot, 0), TC)
    y1 = _tile_rows_to_matrix(ybuf, (slot, 1), TC)
    out = x_ref[...] + gate[:, 0:1] * y0 + gate[:, 1:2] * y1
    if final_norm:
        ms = jnp.mean(out * out, axis=-1, keepdims=True)
        out = out * lax.rsqrt(ms + NORM_EPS) * fg_ref[...]
    o_ref[...] = out


def moe_layer_sparse(x, g, rw, rb, wg, wu, wd, final_g=None, sorted_buf=None):
    t, d = x.shape
    n_exp, _, f = wg.shape
    TM = MOE_ROW_TILE
    n_tiles = t // TM
    tri = jnp.asarray(np.tril(np.ones((TM, TM), np.float32), -1)).astype(BF16)
    gate, info, cnt = pl.pallas_call(
        functools.partial(_router_kernel, n_exp=n_exp, n_tiles=n_tiles),
        out_shape=[jax.ShapeDtypeStruct((t, ROUTER_LANES), F32), jax.ShapeDtypeStruct((t, ROUTER_LANES), F32),
                   jax.ShapeDtypeStruct((1, ROUTER_LANES), F32)],
        grid=(n_tiles,),
        in_specs=[pl.BlockSpec((TM, d), lambda i: (i, 0)),
                  pl.BlockSpec((1, d), lambda i: (0, 0)),
                  pl.BlockSpec((d, ROUTER_LANES), lambda i: (0, 0)),
                  pl.BlockSpec((1, ROUTER_LANES), lambda i: (0, 0)),
                  pl.BlockSpec((TM, TM), lambda i: (0, 0))],
        out_specs=[pl.BlockSpec((TM, ROUTER_LANES), lambda i: (i, 0)),
                   pl.BlockSpec((TM, ROUTER_LANES), lambda i: (i, 0)),
                   pl.BlockSpec((1, ROUTER_LANES), lambda i: (0, 0))],
        scratch_shapes=[pltpu.VMEM((1, ROUTER_LANES), F32)],
        compiler_params=_cparams(("arbitrary",)),
        name="moe_router",
    )(x, g.reshape(1, d), rw, rb, tri)

    counts = cnt[0, N_GROUPS:N_GROUPS + n_exp].astype(jnp.int32)
    padded = ((counts + TM - 1) // TM) * TM
    ends = jnp.cumsum(padded)
    offs = ends - padded
    eid = info[:, 0:2].astype(jnp.int32)
    rank = info[:, 2:4].astype(jnp.int32)
    pos = jnp.sum(jnp.where(eid[:, :, None] == jnp.arange(n_exp)[None, None, :], offs[None, None, :], 0), axis=-1) + rank
    assert d == SUBLANES * LANES, "a token row must fill exactly one (8, 128) tile"
    pos = pos * SUBLANES
    pos0, pos1 = pos[:, 0], pos[:, 1]
    max_tiles = (2 * t) // TM + n_exp
    n_valid = (ends[-1] // TM).astype(jnp.int32).reshape(1)
    tile_exp = jnp.minimum(jnp.sum((ends[None, :] // TM) <= jnp.arange(max_tiles)[:, None], axis=-1),
                           n_exp - 1).astype(jnp.int32)
    p_rows = max_tiles * TM

    CH = MOE_COPY_CHUNK
    xs = pl.pallas_call(
        functools.partial(_scatter_rows_kernel, CH=CH, n_chunks=t // CH),
        out_shape=jax.ShapeDtypeStruct((p_rows * SUBLANES, LANES), F32),
        grid_spec=pltpu.PrefetchScalarGridSpec(
            num_scalar_prefetch=2, grid=(t // CH,),
            in_specs=[pl.BlockSpec((CH, d), lambda c, p0, p1: (c, 0)), pl.BlockSpec(memory_space=pl.ANY)],
            out_specs=pl.BlockSpec(memory_space=pl.ANY),
            scratch_shapes=[pltpu.VMEM((2, CH * SUBLANES, LANES), F32), pltpu.SemaphoreType.DMA((2,))]),
        input_output_aliases={3: 0},
        compiler_params=pltpu.CompilerParams(dimension_semantics=("arbitrary",), vmem_limit_bytes=VMEM_LIMIT,
                                             has_side_effects=True),
        name="moe_scatter_rows",
    )(pos0, pos1, x, jnp.zeros((p_rows * SUBLANES, LANES), F32) if sorted_buf is None else sorted_buf)

    def row_idx(i, te, nv):
        return (jnp.minimum(i, nv[0] - 1), 0)

    ys = pl.pallas_call(
        functools.partial(_expert_kernel, TM=TM),
        out_shape=jax.ShapeDtypeStruct((p_rows * SUBLANES, LANES), F32),
        grid_spec=pltpu.PrefetchScalarGridSpec(
            num_scalar_prefetch=2, grid=(max_tiles,),
            in_specs=[pl.BlockSpec((TM * SUBLANES, LANES), row_idx),
                      pl.BlockSpec((1, d), lambda i, te, nv: (0, 0)),
                      pl.BlockSpec((1, d, f), lambda i, te, nv: (te[i], 0, 0)),
                      pl.BlockSpec((1, d, f), lambda i, te, nv: (te[i], 0, 0)),
                      pl.BlockSpec((1, f, d), lambda i, te, nv: (te[i], 0, 0))],
            out_specs=pl.BlockSpec((TM * SUBLANES, LANES), lambda i, te, nv: (i, 0))),
        compiler_params=_cparams(("arbitrary",)),
        name="moe_experts",
    )(tile_exp, n_valid, xs, g.reshape(1, d), wg, wu, wd)

    TC = MOE_COMBINE_TILE
    out = pl.pallas_call(
        functools.partial(_combine_kernel, TC=TC, n_tiles=t // TC, final_norm=final_g is not None),
        out_shape=jax.ShapeDtypeStruct((t, d), F32),
        grid_spec=pltpu.PrefetchScalarGridSpec(
            num_scalar_prefetch=2, grid=(t // TC,),
            in_specs=[pl.BlockSpec((TC, d), lambda i, p0, p1: (i, 0)),
                      pl.BlockSpec((TC, ROUTER_LANES), lambda i, p0, p1: (i, 0)),
                      pl.BlockSpec((1, d), lambda i, p0, p1: (0, 0)),
                      pl.BlockSpec(memory_space=pl.ANY)],
            out_specs=pl.BlockSpec((TC, d), lambda i, p0, p1: (i, 0)),
            scratch_shapes=[pltpu.VMEM((2, 2, TC * SUBLANES, LANES), F32), pltpu.SemaphoreType.DMA((2,))]),
        compiler_params=_cparams(("arbitrary",)),
        name="moe_combine",
    )(pos0, pos1, x, gate, (g if final_g is None else final_g).reshape(1, d), ys)
    return out, xs


def _rwkv_kernel(p_ref, prev_ref, s0_ref, mu_ref, w0_ref, wd_ref, a0_ref, wa_ref, wg_ref, kk_ref, ka_ref,
                 bonus_ref, lng_ref, lnb_ref, tri_ref, hsum_ref, ya_ref, sf_ref, s_scr, prev_scr,
                 *, NB, C, H, DH, n_chunks):
    c = pl.program_id(1)

    @pl.when(c == 0)
    def _():
        s_scr[...] = s0_ref[:, 0]
        prev_scr[...] = prev_ref[:, 0]

    DA = H * DH
    R = NB * C
    p = p_ref[...].reshape(R, p_ref.shape[-1])
    row = lax.broadcasted_iota(jnp.int32, p.shape, 0)
    shifted = pltpu.roll(p, 1, axis=0)
    for n in range(NB):
        shifted = jnp.where(row == n * C, prev_scr[n], shifted)
        prev_scr[n] = p[(n + 1) * C - 1:(n + 1) * C, :]
    xs = p + (shifted - p) * mu_ref[...]
    r = xs[:, 0:DA]
    k = xs[:, DA:2 * DA]
    v = xs[:, 2 * DA:3 * DA]
    lora = xs[:, 3 * DA:3 * DA + R_DECAY + R_ICLR]
    gd = xs[:, 3 * DA + R_DECAY + R_ICLR:3 * DA + R_DECAY + R_ICLR + R_GATE]

    w_log = -_softplus(-(w0_ref[...] + _mp_dot(_dot, jnp.tanh(lora), wd_ref[...], 3))) - 0.5
    lw = -jnp.exp(w_log)
    a = _sigmoid(a0_ref[...] + _mp_dot(_dot, lora, wa_ref[...], 3))
    g = _mp_dot(_dot, _sigmoid(gd), wg_ref[...], 3)

    kk = k * kk_ref[...]
    ss = _dot_exact_rhs(kk * kk, hsum_ref[...], 2)
    kk = kk / jnp.maximum(jnp.sqrt(ss), 1e-12)
    k2 = k * (1.0 + (a - 1.0) * ka_ref[...])
    kka = kk * a

    cum = jnp.concatenate([_dot_exact_lhs(tri_ref[...], lw[n * C:(n + 1) * C], 3) for n in range(NB)], axis=0)
    p_in = jnp.exp(cum)
    r_t = r * p_in
    a_t = kk * jnp.exp(cum - lw)
    p_inv = jnp.exp(-cum)
    b_t = kka * p_inv
    k_t = k2 * p_inv
    bonus = _dot_exact_rhs(r * k2 * bonus_ref[...], hsum_ref[...], 2) * v

    ri = lax.broadcasted_iota(jnp.int32, (C, C), 0)
    ci = lax.broadcasted_iota(jnp.int32, (C, C), 1)
    strict = ri > ci
    incl = ri >= ci
    eye = (ri == ci).astype(F32)
    n_double = max(int(math.ceil(math.log2(C))) - 1, 0)

    chains = [(n, h) for n in range(NB) for h in range(H)]

    def blk(x, n, h):
        return x[n * C:(n + 1) * C, h * DH:(h + 1) * DH]

    def bf(x):
        return x.astype(BF16)

    Bt = [bf(blk(b_t, n, h)) for n, h in chains]
    Kt = [bf(blk(k_t, n, h)) for n, h in chains]
    Vf = [blk(v, n, h) for n, h in chains]
    AR = [bf(jnp.concatenate([blk(a_t, n, h), blk(r_t, n, h)], axis=0)) for n, h in chains]
    S0 = [s_scr[n, h] for n, h in chains]
    idx = range(len(chains))
    GB = [_dot_nt(AR[i], Bt[i]) for i in idx]
    GK = [_dot_nt(AR[i], Kt[i]) for i in idx]
    ARS = [_dot_nt(AR[i], bf(S0[i])) for i in idx]
    Lm = [jnp.where(strict, GB[i][0:C], 0.0) for i in idx]
    Gb = [bf(jnp.where(incl, GB[i][C:2 * C], 0.0)) for i in idx]
    MG = [bf(jnp.concatenate([jnp.where(strict, GK[i][0:C], 0.0), jnp.where(incl, GK[i][C:2 * C], 0.0)], axis=0))
          for i in idx]
    MGV = [_dot(MG[i], bf(Vf[i])) for i in idx]
    T = [eye - Lm[i] for i in idx]
    Pw = [bf(Lm[i]) for i in idx]
    for _ in range(n_double):
        Pw = [bf(_dot(Pw[i], Pw[i])) for i in idx]
        T = [T[i] + _dot(bf(T[i]), Pw[i]) for i in idx]
    U = [_dot(bf(T[i]), bf(-(ARS[i][0:C] + MGV[i][0:C]))) for i in idx]
    Y = [ARS[i][C:2 * C] + _dot(Gb[i], bf(U[i])) + MGV[i][C:2 * C] for i in idx]
    for i, (n, h) in enumerate(chains):
        UV = bf(jnp.concatenate([U[i], Vf[i]], axis=0))
        BK = jnp.concatenate([Bt[i], Kt[i]], axis=0)
        p_tot = p_in[(n + 1) * C - 1:(n + 1) * C, h * DH:(h + 1) * DH]
        s_scr[n, h] = (S0[i] + _dot_tn(UV, BK)) * p_tot

    rows = []
    for n in range(NB):
        ys = []
        for h in range(H):
            Yh = Y[n * H + h]
            yc = Yh - jnp.mean(Yh, axis=-1, keepdims=True)
            var = jnp.mean(yc * yc, axis=-1, keepdims=True)
            ys.append(yc * lax.rsqrt(var + GN_EPS))
        rows.append(jnp.concatenate(ys, axis=-1))
    y = jnp.concatenate(rows, axis=0) * lng_ref[...] + lnb_ref[...]
    ya_ref[...] = ((y + bonus) * g).reshape(NB, C, DA)

    @pl.when(c == n_chunks - 1)
    def _():
        sf_ref[:, 0] = s_scr[...]


def rwkv_mix(pa, n_batch, seq, shift_prev, wkv0, wts, n_par):
    t, ap = pa.shape
    H = wkv0.shape[1]
    DA = H * DH_A
    C = min(RWKV_CHUNK, seq)
    n_chunks = seq // C
    NB = n_par
    G = n_batch // NB
    tri = jnp.asarray(np.tril(np.ones((C, C), np.float32))).astype(BF16)
    hsum = jnp.asarray(np.kron(np.eye(H, dtype=np.float32), np.ones((DH_A, DH_A), np.float32))).astype(BF16)

    def full(shape):
        nd = len(shape)
        return pl.BlockSpec(shape, lambda b, c: (0,) * nd)

    vec = full((1, DA))
    ya, s_fin = pl.pallas_call(
        functools.partial(_rwkv_kernel, NB=NB, C=C, H=H, DH=DH_A, n_chunks=n_chunks),
        out_shape=[jax.ShapeDtypeStruct((NB, t // NB, DA), F32),
                   jax.ShapeDtypeStruct((NB, G, H, DH_A, DH_A), F32)],
        grid=(G, n_chunks),
        in_specs=[pl.BlockSpec((NB, C, ap), lambda b, c: (0, b * n_chunks + c, 0)),
                  pl.BlockSpec((NB, 1, 1, ap), lambda b, c: (0, b, 0, 0)),
                  pl.BlockSpec((NB, 1, H, DH_A, DH_A), lambda b, c: (0, b, 0, 0, 0)),
                  full((1, ap)), vec, full((R_DECAY + R_ICLR, DA)), vec, full((R_DECAY + R_ICLR, DA)),
                  full((R_GATE, DA)), vec, vec, vec, vec, vec, full((C, C)), full((DA, DA))],
        out_specs=[pl.BlockSpec((NB, C, DA), lambda b, c: (0, b * n_chunks + c, 0)),
                   pl.BlockSpec((NB, 1, H, DH_A, DH_A), lambda b, c: (0, b, 0, 0, 0))],
        scratch_shapes=[pltpu.VMEM((NB, H, DH_A, DH_A), F32), pltpu.VMEM((NB, 1, ap), F32)],
        compiler_params=_cparams(("parallel", "arbitrary")),
        name="rwkv7_mix",
    )(pa.reshape(NB, t // NB, ap), shift_prev.reshape(NB, G, 1, ap), wkv0.reshape(NB, G, H, DH_A, DH_A),
      wts["mu"], wts["w0"], wts["wd"], wts["a0"], wts["wa"],
      wts["wg"], wts["key_k"], wts["key_a"], wts["bonus"], wts["lnx_g"], wts["lnx_b"], tri, hsum)
    return ya.reshape(t, DA), s_fin.reshape(wkv0.shape)


GMLP_TILE = 128
GMLP_ROWS = 512


def _gmlp_kernel(u_ref, v_ref, ng_ref, nb_ref, wm_ref, bias_ref, o_ref, *vn_refs, n_sub):
    vf = _gelu(v_ref[...])
    mu = jnp.mean(vf, axis=-1, keepdims=True)
    vc = vf - mu
    var = jnp.mean(vc * vc, axis=-1, keepdims=True)
    vn = vc * lax.rsqrt(var + NORM_EPS) * ng_ref[...] + nb_ref[...]
    for vn_ref in vn_refs:
        vn_ref[...] = vn
    vb = vn.astype(BF16)
    n_h = wm_ref.shape[0]
    cb = vn.shape[1] // n_h
    gu = _gelu(u_ref[...])
    for c in range(n_sub):
        rows = slice(c * GMLP_TILE, (c + 1) * GMLP_TILE)
        s = jnp.concatenate([_dot(wm_ref[h], vb[rows, h * cb:(h + 1) * cb]) for h in range(n_h)], axis=-1)
        o_ref[rows, :] = gu[rows, :] * (s + bias_ref[...])


def gmlp_mix(pu, pv, ng, nb, wm_bf16, bias_tile, want_vn):
    t, db = pu.shape
    n_h = wm_bf16.shape[0]
    rows = GMLP_ROWS if t % GMLP_ROWS == 0 else t
    n_out = 2 if want_vn else 1
    outs = pl.pallas_call(
        functools.partial(_gmlp_kernel, n_sub=rows // GMLP_TILE),
        out_shape=[jax.ShapeDtypeStruct((t, db), F32)] * n_out,
        grid=(t // rows,),
        in_specs=[pl.BlockSpec((rows, db), lambda i: (i, 0)),
                  pl.BlockSpec((rows, db), lambda i: (i, 0)),
                  pl.BlockSpec((1, db), lambda i: (0, 0)),
                  pl.BlockSpec((1, db), lambda i: (0, 0)),
                  pl.BlockSpec((n_h, GMLP_TILE, GMLP_TILE), lambda i: (0, 0, 0)),
                  pl.BlockSpec((GMLP_TILE, db), lambda i: (0, 0))],
        out_specs=[pl.BlockSpec((rows, db), lambda i: (i, 0))] * n_out,
        compiler_params=_cparams(("parallel",)),
        name="gmlp_mix",
    )(pu, pv, ng, nb, wm_bf16, bias_tile)
    return (outs[0], outs[1]) if want_vn else (outs[0], None)


N_SEG = 8
SEG_GAP = 4


def _seg_pitch(seg):
    return seg + SEG_GAP if seg % SUBLANES == 0 else seg


def _rglru_kernel(xb_ref, gy_ref, cprev_ref, h0_ref, cw_ref, cb_ref, gw_ref, gb_ref, lam_ref,
                  yc_ref, ctail_ref, hl_ref, xe_scr, a_scr, b_scr, h_scr, *, TL, DC, pos0, n_tiles):
    l = pl.program_id(1)
    PAD = 8

    @pl.when(l == 0)
    def _():
        xe_scr[0:PAD, :] = cprev_ref[0]
        h_scr[...] = h0_ref[0]

    xe_scr[PAD:PAD + TL, :] = xb_ref[...]
    xc = cb_ref[...] + xe_scr[pl.ds(PAD - (CONV_W - 1), TL), :] * cw_ref[0:1, :]
    for i in range(1, CONV_W):
        xc = xc + xe_scr[pl.ds(PAD - (CONV_W - 1) + i, TL), :] * cw_ref[i:i + 1, :]
    tail = xe_scr[TL:TL + PAD, :]
    ctail_ref[0] = tail
    xe_scr[0:PAD, :] = tail

    gates = _dot(xc.astype(BF16), gw_ref[...]) + gb_ref[...]
    rg = _sigmoid(gates[:, 0:DC])
    ig = _sigmoid(gates[:, DC:2 * DC])
    log_a = -LRU_C * rg * _softplus(-lam_ref[...])
    a = jnp.exp(log_a)
    mult = jnp.sqrt(1.0 - a * a)
    row = lax.broadcasted_iota(jnp.int32, (TL, DC), 0)
    mult = jnp.where(row + (l * TL + pos0) == 0, 1.0, mult)
    b = mult * ig * xc
    n_slab = DC // LANES
    seg = TL // N_SEG
    pitch = _seg_pitch(seg)
    for s in range(n_slab):
        for j in range(N_SEG):
            a_scr[s, pl.ds(j * pitch, seg), :] = a[j * seg:(j + 1) * seg, s * LANES:(s + 1) * LANES]
            b_scr[s, pl.ds(j * pitch, seg), :] = b[j * seg:(j + 1) * seg, s * LANES:(s + 1) * LANES]

    def step(i, carry):
        idx = pl.ds(i, N_SEG, stride=pitch) if pitch > 1 else pl.ds(0, N_SEG)
        out = []
        for s in range(n_slab):
            hloc, ap = carry[s]
            ai = a_scr[s, idx, :]
            hloc = ai * hloc + b_scr[s, idx, :]
            ap = ap * ai
            b_scr[s, idx, :] = hloc
            a_scr[s, idx, :] = ap
            out.append((hloc, ap))
        return tuple(out)

    lax.fori_loop(0, seg, step,
                  tuple((jnp.zeros((N_SEG, LANES), F32), jnp.ones((N_SEG, LANES), F32)) for _ in range(n_slab)),
                  unroll=min(seg, 8))

    carry = h_scr[...]
    g_act = _gelu(gy_ref[...])
    for j in range(N_SEG):
        rows = slice(j * seg, (j + 1) * seg)
        rows_p = pl.ds(j * pitch, seg)
        hloc = jnp.concatenate([b_scr[s, rows_p, :] for s in range(n_slab)], axis=-1)
        ap = jnp.concatenate([a_scr[s, rows_p, :] for s in range(n_slab)], axis=-1)
        hj = hloc + ap * carry
        yc_ref[rows, :] = g_act[rows, :] * hj
        carry = hj[seg - 1:seg, :]
    h_scr[...] = carry

    @pl.when(l == n_tiles - 1)
    def _():
        hl_ref[0] = carry


def rglru_mix(xb, gy, n_batch, seq, conv_prev8, h0, pos0, wts):
    t, dc = xb.shape
    TL = 512 if seq % 512 == 0 else seq
    n_tiles = seq // TL
    scan_rows = N_SEG * _seg_pitch(TL // N_SEG)

    def full(shape):
        nd = len(shape)
        return pl.BlockSpec(shape, lambda b, l: (0,) * nd)

    yc, ctail, hl = pl.pallas_call(
        functools.partial(_rglru_kernel, TL=TL, DC=dc, pos0=pos0, n_tiles=n_tiles),
        out_shape=[jax.ShapeDtypeStruct((t, dc), F32), jax.ShapeDtypeStruct((n_batch, 8, dc), F32),
                   jax.ShapeDtypeStruct((n_batch, 1, dc), F32)],
        grid=(n_batch, n_tiles),
        in_specs=[pl.BlockSpec((TL, dc), lambda b, l: (b * n_tiles + l, 0)),
                  pl.BlockSpec((TL, dc), lambda b, l: (b * n_tiles + l, 0)),
                  pl.BlockSpec((1, 8, dc), lambda b, l: (b, 0, 0)),
                  pl.BlockSpec((1, 1, dc), lambda b, l: (b, 0, 0)),
                  full((CONV_W, dc)), full((1, dc)), full((dc, 2 * dc)), full((1, 2 * dc)), full((1, dc))],
        out_specs=[pl.BlockSpec((TL, dc), lambda b, l: (b * n_tiles + l, 0)),
                   pl.BlockSpec((1, 8, dc), lambda b, l: (b, 0, 0)),
                   pl.BlockSpec((1, 1, dc), lambda b, l: (b, 0, 0))],
        scratch_shapes=[pltpu.VMEM((TL + 8, dc), F32), pltpu.VMEM((dc // LANES, scan_rows, LANES), F32),
                        pltpu.VMEM((dc // LANES, scan_rows, LANES), F32), pltpu.VMEM((1, dc), F32)],
        compiler_params=_cparams(("parallel", "arbitrary")),
        name="rglru_mix",
    )(xb, gy, conv_prev8, h0.reshape(n_batch, 1, dc), wts["conv_w"], wts["conv_b"], wts["gate_w"], wts["gate_b"],
      wts["lam"])
    return yc, ctail[:, 8 - (CONV_W - 1):, :], hl.reshape(n_batch, dc)


def _t5_bucket(dist):
    dist = np.asarray(dist)
    max_exact = N_BUCKETS // 2
    scaled = np.log(np.maximum(dist, 1) / max_exact) / math.log(BUCKET_MAX_DIST / max_exact)
    large = np.minimum(max_exact + (scaled * (N_BUCKETS - max_exact)).astype(np.int32), N_BUCKETS - 1)
    return np.where(dist < max_exact, dist, large).astype(np.int32)


def _dist_table(rel_bias, max_dist):
    dist = np.arange(max_dist + 1)
    count = np.zeros(max_dist + 1, np.float32)
    for window, dil in DILATED:
        count += ((dist % dil == 0) & (dist <= window)).astype(np.float32)
    logcnt = np.where(count > 0, np.log(np.maximum(count, 1.0)), 0.0).astype(np.float32)
    tab = jnp.take(rel_bias, jnp.asarray(_t5_bucket(dist)), axis=0) + jnp.asarray(logcnt)[:, None]
    return jnp.where(jnp.asarray(count > 0)[:, None], tab, NEG_BIG)


def _toeplitz_tiles(tab, n_pos, n_neg, T):
    D, H = tab.shape
    span = T * n_pos
    assert D >= span
    n_col = span + T * n_neg + T - 1
    ext = jnp.concatenate([jnp.flip(tab[:span], axis=0), jnp.full((n_col + 1 - span, H), NEG_BIG, F32)], axis=0)
    ext = jnp.transpose(ext)
    skew = jnp.tile(ext, (1, T))[:, :T * n_col].reshape(H, T, n_col)
    tiles = [skew[:, :, span - 1 - T * dd: span - 1 - T * dd + T] for dd in range(-n_neg, n_pos)]
    return jnp.stack(tiles, axis=1)


def _attn_prompt_kernel(q_ref, k_ref, v_ref, bias_ref, o_ref, kb_scr, vb_scr, *, E, SUB, NS):
    qi = pl.program_id(2)
    TQ = NS * SUB

    @pl.when(qi == 0)
    def _():
        kb_scr[...] = k_ref[0].astype(BF16)
        vb_scr[...] = v_ref[0].astype(BF16)

    lane = lax.broadcasted_iota(jnp.int32, (SUB, 2 * E), 1)
    q2 = []
    for rs in range(NS):
        q = q_ref[0, rs * SUB:(rs + 1) * SUB, :] * (E ** -0.5 * LOG2E)
        q2.append(jnp.concatenate([jnp.where(lane < E, q, 0.0), jnp.where(lane >= E, q, 0.0)], axis=0).astype(BF16))

    def block(i, carry, diagonal):
        j = qi - i
        koff = pl.multiple_of(j * TQ, TQ)
        out = []
        for rs in range(NS):
            n_cs = rs + 1 if diagonal else NS
            kj = kb_scr[pl.ds(koff, n_cs * SUB), :]
            vj = vb_scr[pl.ds(koff, n_cs * SUB), :]
            m, l, acc = carry[rs]
            s = _dot_nt(q2[rs], kj)
            parts = []
            for cs in range(n_cs):
                dd = i * NS + (rs - cs + NS - 1)
                bias = jnp.concatenate([bias_ref[0, dd], bias_ref[1, dd]], axis=0)
                parts.append(s[:, cs * SUB:(cs + 1) * SUB] + bias)
            mx = parts[0]
            for part in parts[1:]:
                mx = jnp.maximum(mx, part)
            m_new = jnp.maximum(m, jnp.max(mx, axis=-1, keepdims=True))
            alpha = jnp.exp2(m - m_new)
            ps = [jnp.exp2(part - m_new) for part in parts]
            psum = ps[0]
            for pexp in ps[1:]:
                psum = psum + pexp
            l = alpha * l + psum
            acc = alpha * acc + _dot(jnp.concatenate(ps, axis=-1).astype(BF16), vj)
            out.append((m_new, l, acc))
        return tuple(out)

    init = tuple((jnp.full((2 * SUB, SUB), NEG_BIG, F32), jnp.zeros((2 * SUB, SUB), F32),
                  jnp.zeros((2 * SUB, 2 * E), F32)) for _ in range(NS))
    first = block(0, init, True)
    res = lax.fori_loop(1, qi + 1, lambda i, carry: block(i, carry, False), first)
    for rs in range(NS):
        m, l, acc = res[rs]
        o = acc / jnp.sum(l, axis=-1, keepdims=True)
        o_ref[0, rs * SUB:(rs + 1) * SUB, :] = jnp.where(lane < E, o[0:SUB], o[SUB:2 * SUB])


def attn_prompt(q, k, v, bias_tiles, n_batch, seq):
    hd = q.shape[-1]
    E = hd // H_D
    SUB = ATT_TILE
    NS = ATT_SUBTILES
    TQ = SUB * NS
    nq = seq // TQ
    nt = bias_tiles.shape[1]
    return pl.pallas_call(
        functools.partial(_attn_prompt_kernel, E=E, SUB=SUB, NS=NS),
        out_shape=jax.ShapeDtypeStruct((n_batch, seq, hd), F32),
        grid=(H_D // 2, n_batch, nq),
        in_specs=[pl.BlockSpec((1, TQ, 2 * E), lambda hp, b, i: (b, i, hp)),
                  pl.BlockSpec((1, seq, 2 * E), lambda hp, b, i: (b, 0, hp)),
                  pl.BlockSpec((1, seq, 2 * E), lambda hp, b, i: (b, 0, hp)),
                  pl.BlockSpec((2, nt, SUB, SUB), lambda hp, b, i: (hp, 0, 0, 0))],
        out_specs=pl.BlockSpec((1, TQ, 2 * E), lambda hp, b, i: (b, i, hp)),
        scratch_shapes=[pltpu.VMEM((seq, 2 * E), BF16), pltpu.VMEM((seq, 2 * E), BF16)],
        compiler_params=_cparams(("arbitrary", "arbitrary", "arbitrary")),
        name="dilated_attn_prompt",
    )(q, k, v, bias_tiles)


def _attn_sample_kernel(q_ref, kn_ref, vn_ref, ck_ref, cv_ref, bo_ref, bn_ref, o_ref, *, E, S):
    lane = lax.broadcasted_iota(jnp.int32, (S, 2 * E), 1)
    NPAD = bn_ref.shape[-1]
    outs = []
    for hp in range(H_D // 2):
        sl = slice(hp * 2 * E, (hp + 1) * 2 * E)
        q = q_ref[0, :, sl] * (E ** -0.5)
        q2 = jnp.concatenate([jnp.where(lane < E, q, 0.0), jnp.where(lane >= E, q, 0.0)], axis=0).astype(BF16)
        zpad = jnp.zeros((NPAD - S, 2 * E), F32)
        kn = jnp.concatenate([kn_ref[0, :, sl], zpad], axis=0).astype(BF16)
        vn = jnp.concatenate([vn_ref[0, :, sl], zpad], axis=0).astype(BF16)
        s_old = _dot_nt(q2, ck_ref[0, :, sl].astype(BF16)) + jnp.concatenate([bo_ref[2 * hp], bo_ref[2 * hp + 1]], axis=0)
        s_new = _dot_nt(q2, kn) + jnp.concatenate([bn_ref[2 * hp], bn_ref[2 * hp + 1]], axis=0)
        m = jnp.maximum(jnp.max(s_old, axis=-1, keepdims=True), jnp.max(s_new, axis=-1, keepdims=True))
        p_old = jnp.exp(s_old - m)
        p_new = jnp.exp(s_new - m)
        l = jnp.sum(p_old, axis=-1, keepdims=True) + jnp.sum(p_new, axis=-1, keepdims=True)
        acc = _dot(p_old.astype(BF16), cv_ref[0, :, sl].astype(BF16)) + _dot(p_new.astype(BF16), vn)
        o = acc / l
        outs.append(jnp.where(lane < E, o[0:S], o[S:2 * S]))
    o_ref[0] = jnp.concatenate(outs, axis=-1)


def attn_sample(q, k_new, v_new, cache_k, cache_v, bias_old, bias_new):
    n_batch, S, hd = q.shape
    W = cache_k.shape[1]
    E = hd // H_D
    NPAD = bias_new.shape[-1]
    return pl.pallas_call(
        functools.partial(_attn_sample_kernel, E=E, S=S),
        out_shape=jax.ShapeDtypeStruct((n_batch, S, hd), F32),
        grid=(n_batch,),
        in_specs=[pl.BlockSpec((1, S, hd), lambda b: (b, 0, 0)),
                  pl.BlockSpec((1, S, hd), lambda b: (b, 0, 0)),
                  pl.BlockSpec((1, S, hd), lambda b: (b, 0, 0)),
                  pl.BlockSpec((1, W, hd), lambda b: (b, 0, 0)),
                  pl.BlockSpec((1, W, hd), lambda b: (b, 0, 0)),
                  pl.BlockSpec((H_D, S, W), lambda b: (0, 0, 0)),
                  pl.BlockSpec((H_D, S, NPAD), lambda b: (0, 0, 0))],
        out_specs=pl.BlockSpec((1, S, hd), lambda b: (b, 0, 0)),
        compiler_params=_cparams(("parallel",)),
        name="dilated_attn_sample",
    )(q, k_new, v_new, cache_k, cache_v, bias_old, bias_new)


def _even_weights(j, w_in_even, w_out_even, shift_mu, decay_w0, decay_up, iclr_a0, iclr_up, gate_up, key_k, key_a,
                  bonus_r_k, lnx_g, lnx_b, sgu_norm_g, sgu_norm_b, sgu_w, sgu_b):
    da = decay_w0.shape[1]
    zeros_d = jnp.zeros((R_ICLR, da), F32)
    zeros_i = jnp.zeros((R_DECAY, da), F32)
    return dict(
        w_in=w_in_even[j].astype(BF16),
        w_out_a=w_out_even[j, :da].astype(BF16), w_out_b=w_out_even[j, da:].astype(BF16),
        mu=shift_mu[j].reshape(1, -1), w0=decay_w0[j].reshape(1, -1), a0=iclr_a0[j].reshape(1, -1),
        wd=jnp.concatenate([decay_up[j], zeros_d], axis=0), wa=jnp.concatenate([zeros_i, iclr_up[j]], axis=0),
        wg=gate_up[j], key_k=key_k[j].reshape(1, -1), key_a=key_a[j].reshape(1, -1),
        bonus=bonus_r_k[j].reshape(1, -1), lnx_g=lnx_g[j].reshape(1, -1), lnx_b=lnx_b[j].reshape(1, -1),
        ng=sgu_norm_g[j].reshape(1, -1), nb=sgu_norm_b[j].reshape(1, -1), sgu_w=sgu_w[j], sgu_b=sgu_b[j])


def _gmlp_tables(sgu_w, sgu_b, chunk):
    reps = GMLP_TILE // chunk
    n_h = sgu_w.shape[0]
    cb = None
    wm = sgu_w[:, :chunk, :chunk] * jnp.asarray(np.tril(np.ones((chunk, chunk), np.float32)))
    if reps > 1:
        eye = jnp.asarray(np.eye(reps, dtype=np.float32))
        wm = jnp.einsum("ab,hts->hatbs", eye, wm).reshape(n_h, GMLP_TILE, GMLP_TILE)
    bias = jnp.tile(jnp.transpose(sgu_b[:, :chunk]), (reps, 1))
    return wm.astype(BF16), bias


def _even_layer(x, n_batch, seq, chunk, n_par, want_vn, shift_prev, wkv0, norm_g, ew):
    pa, pu, pv = norm_matmul(x, norm_g, ew["w_in"], (ew["mu"].shape[1], ew["ng"].shape[1], ew["ng"].shape[1]))
    ya, wkv = rwkv_mix(pa, n_batch, seq, shift_prev, wkv0, ew, n_par)
    wm, bias = _gmlp_tables(ew["sgu_w"], ew["sgu_b"], chunk)
    cb = pu.shape[1] // wm.shape[0]
    bias_tile = jnp.repeat(bias, cb, axis=1)
    yb, vn = gmlp_mix(pu, pv, ew["ng"], ew["nb"], wm, bias_tile, want_vn)
    x = proj_residual(x, ya, yb, ew["w_out_a"], ew["w_out_b"])
    last = pa.reshape(n_batch, seq, -1)[:, -1]
    return x, last, wkv, vn


def _odd_weights(j, w_in_odd, w_out_odd, conv_w, conv_b, rgate_w, rgate_b, igate_w, igate_b, lru_lambda):
    dc = conv_b.shape[1]
    eye = jnp.asarray(np.eye(H_C, dtype=np.float32))

    def blockdiag(w):
        dh = w.shape[-1]
        return jnp.einsum("ab,aij->aibj", eye, w).reshape(H_C * dh, H_C * dh)

    return dict(
        w_in=w_in_odd[j].astype(BF16),
        w_out_c=w_out_odd[j, :dc].astype(BF16), w_out_d=w_out_odd[j, dc:].astype(BF16),
        conv_w=conv_w[j], conv_b=conv_b[j].reshape(1, -1),
        gate_w=jnp.concatenate([blockdiag(rgate_w[j]), blockdiag(igate_w[j])], axis=1).astype(BF16),
        gate_b=jnp.concatenate([rgate_b[j], igate_b[j]]).reshape(1, -1),
        lam=lru_lambda[j].reshape(1, -1))


def _odd_layer(x, n_batch, seq, conv_prev, h0, pos0, caches, dist_tab, norm_g, ow):
    dc = ow["lam"].shape[1]
    gy, xb, q, k, v = norm_matmul(x, norm_g, ow["w_in"], (dc,) * 5)
    conv_prev8 = jnp.pad(conv_prev, ((0, 0), (8 - (CONV_W - 1), 0), (0, 0)))
    yc, conv_last, h_last = rglru_mix(xb, gy, n_batch, seq, conv_prev8, h0, pos0, ow)
    hd = q.shape[1]
    q3, k3, v3 = (a.reshape(n_batch, seq, hd) for a in (q, k, v))
    if caches is None:
        tiles = _toeplitz_tiles(dist_tab * LOG2E, seq // ATT_TILE, ATT_SUBTILES - 1, ATT_TILE)
        o = attn_prompt(q3, k3, v3, tiles, n_batch, seq)
    else:
        cache_k, cache_v = caches
        W = cache_k.shape[1]
        NPAD = 128
        tab_t = jnp.flip(jnp.transpose(dist_tab[:W + seq]), axis=1)
        b_old = jnp.stack([tab_t[:, seq - 1 - j:seq - 1 - j + W] for j in range(seq)], axis=1)
        d_new = np.arange(seq)[:, None] - np.arange(NPAD)[None, :]
        ok_new = (d_new >= 0) & (np.arange(NPAD)[None, :] < seq)
        b_new = jnp.take(dist_tab, jnp.asarray(np.maximum(d_new, 0)), axis=0)
        b_new = jnp.transpose(jnp.where(jnp.asarray(ok_new)[..., None], b_new, NEG_BIG), (2, 0, 1))
        ck = cache_k.reshape(n_batch, W, hd)
        cv = cache_v.reshape(n_batch, W, hd)
        o = attn_sample(q3, k3, v3, ck, cv, b_old, b_new)
    x = proj_residual(x, yc, o.reshape(n_batch * seq, hd), ow["w_out_c"], ow["w_out_d"])
    e = hd // H_D
    return x, conv_last, h_last, k3.reshape(n_batch, seq, H_D, e), v3.reshape(n_batch, seq, H_D, e)


def _moe_weights(l, router_group_w, router_group_b, router_expert_w, router_expert_b, exp_w_gate, exp_w_up,
                 exp_w_down):
    d = router_group_w.shape[1]
    n_used = N_GROUPS + router_expert_w.shape[2]
    rw = jnp.concatenate([router_group_w[l], router_expert_w[l], jnp.zeros((d, ROUTER_LANES - n_used), F32)], axis=1)
    rb = jnp.concatenate([router_group_b[l], router_expert_b[l], jnp.zeros((ROUTER_LANES - n_used,), F32)])
    return dict(rw=rw.astype(BF16), rb=rb.reshape(1, -1), wg=exp_w_gate[l].astype(BF16), wu=exp_w_up[l].astype(BF16),
                wd=exp_w_down[l].astype(BF16))


def kernel(x_prompt, x_sample, state_wkv, state_shift, state_conv, state_rglru, cache_k, cache_v, norm_mix, norm_ffn, norm_final, w_in_even, w_out_even, shift_mu, decay_w0, decay_up, iclr_a0, iclr_up, gate_up, key_k, key_a, bonus_r_k, lnx_g, lnx_b, sgu_norm_g, sgu_norm_b, sgu_w, sgu_b, w_in_odd, w_out_odd, conv_w, conv_b, rgate_w, rgate_b, igate_w, igate_b, lru_lambda, rel_bias, router_group_w, router_group_b, router_expert_w, router_expert_b, exp_w_gate, exp_w_up, exp_w_down):
    B, L, D = x_prompt.shape
    DB, S, _ = x_sample.shape
    depth = norm_mix.shape[0]
    xp = x_prompt.reshape(B * L, D)
    xs = x_sample.reshape(DB * S, D)
    W = cache_k.shape[2]
    dist_tab = _dist_table(rel_bias, max(L, W + S) - 1)

    sorted_buf = None
    wkv_p, shift_p, conv_p, lru_p, k_p, v_p = [], [], [], [], [], []
    wkv_s, shift_s, chunkv_s, conv_s, lru_s, k_s, v_s = [], [], [], [], [], [], []
    for l in range(depth):
        j = l // 2
        if l % 2 == 0:
            ew = _even_weights(j, w_in_even, w_out_even, shift_mu, decay_w0, decay_up, iclr_a0, iclr_up, gate_up,
                               key_k, key_a, bonus_r_k, lnx_g, lnx_b, sgu_norm_g, sgu_norm_b, sgu_w, sgu_b)
            a_proj = ew["mu"].shape[1]
            h_a = state_wkv.shape[2]
            xp, sh, wkv, _ = _even_layer(xp, B, L, GMLP_TILE, RWKV_PAR_PROMPT, False, jnp.zeros((B, a_proj), F32),
                                         jnp.zeros((B, h_a, DH_A, DH_A), F32), norm_mix[l], ew)
            xs, sh_s, wkv_s_new, vn_s = _even_layer(xs, DB, S, S, RWKV_PAR_SAMPLE, True, state_shift[j], state_wkv[j], norm_mix[l], ew)
            wkv_p.append(wkv)
            shift_p.append(sh)
            wkv_s.append(wkv_s_new)
            shift_s.append(sh_s)
            chunkv_s.append(vn_s.reshape(DB, S, -1))
        else:
            ow = _odd_weights(j, w_in_odd, w_out_odd, conv_w, conv_b, rgate_w, rgate_b, igate_w, igate_b, lru_lambda)
            dc = ow["lam"].shape[1]
            xp, cv, hl, kr, vr = _odd_layer(xp, B, L, jnp.zeros((B, CONV_W - 1, dc), F32), jnp.zeros((B, dc), F32),
                                            0, None, dist_tab, norm_mix[l], ow)
            xs, cv_s, hl_s, kr_s, vr_s = _odd_layer(xs, DB, S, state_conv[j], state_rglru[j], PAST_LEN,
                                                    (cache_k[j], cache_v[j]), dist_tab, norm_mix[l], ow)
            conv_p.append(cv)
            lru_p.append(hl)
            k_p.append(kr)
            v_p.append(vr)
            conv_s.append(cv_s)
            lru_s.append(hl_s)
            k_s.append(kr_s)
            v_s.append(vr_s)
        mw = _moe_weights(l, router_group_w, router_group_b, router_expert_w, router_expert_b, exp_w_gate, exp_w_up,
                          exp_w_down)
        xp, sorted_buf = moe_layer_sparse(xp, norm_ffn[l], mw["rw"], mw["rb"], mw["wg"], mw["wu"], mw["wd"],
                                          final_g=norm_final if l == depth - 1 else None, sorted_buf=sorted_buf)
        xs = moe_layer(xs, norm_ffn[l], mw["rw"], mw["rb"], mw["wg"], mw["wu"], mw["wd"])
    y_prompt = xp.reshape(B, L, D)
    y_sample = rmsnorm_call(xs, norm_final).reshape(DB, S, D)
    return (y_prompt, y_sample,
            jnp.stack(wkv_p), jnp.stack(shift_p), jnp.stack(conv_p), jnp.stack(lru_p), jnp.stack(k_p), jnp.stack(v_p),
            jnp.stack(wkv_s), jnp.stack(shift_s), jnp.stack(chunkv_s), jnp.stack(conv_s), jnp.stack(lru_s),
            jnp.stack(k_s), jnp.stack(v_s))
```

```python
import functools
import math

import numpy as np
import jax
import jax.numpy as jnp
from jax import lax
from jax.experimental import pallas as pl
from jax.experimental.pallas import tpu as pltpu

F32 = jnp.float32
BF16 = jnp.bfloat16
HI = lax.Precision.HIGHEST

PAST_LEN = 8192
DH_A = 64
R_DECAY = 64
R_ICLR = 64
R_GATE = 128
GN_EPS = 64e-5
H_B = 4
H_C = 8
CONV_W = 4
LRU_C = 8.0
H_D = 8
DILATED = ((128, 1), (512, 4), (2048, 16))
N_BUCKETS = 32
BUCKET_MAX_DIST = 2048
NEG_BIG = -1e30
N_GROUPS = 4
EXP_PER_GROUP = 4
NORM_EPS = 1e-6
LOG2E = math.log2(math.e)

VMEM_LIMIT = 56 * 1024 * 1024
RWKV_CHUNK = 64
RWKV_PAR_PROMPT = 4
RWKV_PAR_SAMPLE = 8
ATT_TILE = 128
ATT_SUBTILES = 4
LANES = 128
SUBLANES = 8


def _cparams(sem):
    return pltpu.CompilerParams(dimension_semantics=sem, vmem_limit_bytes=VMEM_LIMIT)


def _dot(a, b, precision=None):
    return jnp.dot(a, b, preferred_element_type=F32, precision=precision)


def _dot_nt(a, b, precision=None):
    return lax.dot_general(a, b, (((1,), (1,)), ((), ())), preferred_element_type=F32, precision=precision)


def _dot_tn(a, b, precision=None):
    return lax.dot_general(a, b, (((0,), (0,)), ((), ())), preferred_element_type=F32, precision=precision)


def _split_bf16(x, n):
    parts = []
    for _ in range(n):
        hi = x.astype(BF16)
        parts.append(hi)
        x = x - hi.astype(F32)
    return parts


def _mp_dot(dotfn, a, b, passes):
    if passes == 1:
        return dotfn(a.astype(BF16), b.astype(BF16))
    a_hi, a_lo = _split_bf16(a, 2)
    b_hi, b_lo = _split_bf16(b, 2)
    return dotfn(a_hi, b_hi) + (dotfn(a_hi, b_lo) + dotfn(a_lo, b_hi))


def _dot_exact_rhs(a, b_bf16, n_split):
    parts = _split_bf16(a, n_split)
    acc = _dot(parts[0], b_bf16)
    for part in parts[1:]:
        acc = acc + _dot(part, b_bf16)
    return acc


def _dot_exact_lhs(a_bf16, b, n_split):
    parts = _split_bf16(b, n_split)
    acc = _dot(a_bf16, parts[0])
    for part in parts[1:]:
        acc = acc + _dot(a_bf16, part)
    return acc


def _softplus(x):
    return jnp.maximum(x, 0.0) + jnp.log(1.0 + jnp.exp(-jnp.abs(x)))


def _sigmoid(x):
    return 1.0 / (1.0 + jnp.exp(-x))


def _gelu(x):
    c = math.sqrt(2.0 / math.pi)
    return 0.5 * x * (1.0 + jnp.tanh(c * (x + 0.044715 * (x * x * x))))


def _row_tile(t, pref=512):
    return pref if t % pref == 0 else t


def _norm_matmul_kernel(x_ref, g_ref, w_ref, *out_refs, splits):
    x = x_ref[...]
    ms = jnp.mean(x * x, axis=-1, keepdims=True)
    h = (x * lax.rsqrt(ms + NORM_EPS) * g_ref[...]).astype(BF16)
    off = 0
    for o_ref, n in zip(out_refs, splits):
        o_ref[...] = _dot(h, w_ref[:, off:off + n])
        off += n


def norm_matmul(x, g, w_bf16, splits):
    t, d = x.shape
    n = w_bf16.shape[1]
    tm = _row_tile(t)
    return pl.pallas_call(
        functools.partial(_norm_matmul_kernel, splits=splits),
        out_shape=[jax.ShapeDtypeStruct((t, s), F32) for s in splits],
        grid=(t // tm,),
        in_specs=[pl.BlockSpec((tm, d), lambda i: (i, 0)),
                  pl.BlockSpec((1, d), lambda i: (0, 0)),
                  pl.BlockSpec((d, n), lambda i: (0, 0))],
        out_specs=[pl.BlockSpec((tm, s), lambda i: (i, 0)) for s in splits],
        compiler_params=_cparams(("parallel",)),
        name="norm_matmul",
    )(x, g.reshape(1, d), w_bf16)


def _proj_res_kernel(x_ref, a_ref, b_ref, wa_ref, wb_ref, o_ref):
    acc = _dot(a_ref[...].astype(BF16), wa_ref[...]) + _dot(b_ref[...].astype(BF16), wb_ref[...])
    o_ref[...] = x_ref[...] + acc


def proj_residual(x, a, b, wa, wb):
    t, d = x.shape
    tm = _row_tile(t)
    ka, kb = a.shape[1], b.shape[1]
    return pl.pallas_call(
        _proj_res_kernel,
        out_shape=jax.ShapeDtypeStruct((t, d), F32),
        grid=(t // tm,),
        in_specs=[pl.BlockSpec((tm, d), lambda i: (i, 0)),
                  pl.BlockSpec((tm, ka), lambda i: (i, 0)),
                  pl.BlockSpec((tm, kb), lambda i: (i, 0)),
                  pl.BlockSpec((ka, d), lambda i: (0, 0)),
                  pl.BlockSpec((kb, d), lambda i: (0, 0))],
        out_specs=pl.BlockSpec((tm, d), lambda i: (i, 0)),
        compiler_params=_cparams(("parallel",)),
        name="proj_residual",
    )(x, a, b, wa, wb)


def _rmsnorm_kernel(x_ref, g_ref, o_ref):
    x = x_ref[...]
    ms = jnp.mean(x * x, axis=-1, keepdims=True)
    o_ref[...] = x * lax.rsqrt(ms + NORM_EPS) * g_ref[...]


def rmsnorm_call(x, g):
    t, d = x.shape
    tm = _row_tile(t)
    return pl.pallas_call(
        _rmsnorm_kernel,
        out_shape=jax.ShapeDtypeStruct((t, d), F32),
        grid=(t // tm,),
        in_specs=[pl.BlockSpec((tm, d), lambda i: (i, 0)), pl.BlockSpec((1, d), lambda i: (0, 0))],
        out_specs=pl.BlockSpec((tm, d), lambda i: (i, 0)),
        compiler_params=_cparams(("parallel",)),
        name="final_rmsnorm",
    )(x, g.reshape(1, d))


ROUTER_LANES = 128


def _route(xn, rw, rb, lane, n_exp):
    logits = _dot_exact_rhs(xn, rw, 2) + rb
    lg = jnp.where(lane < N_GROUPS, logits, -jnp.inf)
    gm = jnp.max(lg, axis=-1, keepdims=True)
    top_pg = 1.0 / jnp.sum(jnp.exp(lg - gm), axis=-1, keepdims=True)
    grp = jnp.min(jnp.where(lg == gm, lane, ROUTER_LANES), axis=-1, keepdims=True)
    in_grp = (lane >= N_GROUPS) & (lane < N_GROUPS + n_exp) & (((lane - N_GROUPS) // EXP_PER_GROUP) == grp)
    le = jnp.where(in_grp, logits, -jnp.inf)
    t1 = jnp.max(le, axis=-1, keepdims=True)
    i1 = jnp.min(jnp.where(le == t1, lane, ROUTER_LANES), axis=-1, keepdims=True)
    le2 = jnp.where(lane == i1, -jnp.inf, le)
    t2 = jnp.max(le2, axis=-1, keepdims=True)
    i2 = jnp.min(jnp.where(le2 == t2, lane, ROUTER_LANES), axis=-1, keepdims=True)
    ex = jnp.exp(t2 - t1)
    w1 = 1.0 / (1.0 + ex)
    return i1, i2, w1 * top_pg, (ex * w1) * top_pg


def _moe_kernel(x_ref, g_ref, rw_ref, rb_ref, wg_ref, wu_ref, wd_ref, o_ref, xn_scr, gate_scr, acc_scr, *, n_exp):
    e = pl.program_id(1)
    tm = x_ref.shape[0]
    lane = lax.broadcasted_iota(jnp.int32, (tm, ROUTER_LANES), 1)

    @pl.when(e == 0)
    def _():
        x = x_ref[...]
        ms = jnp.mean(x * x, axis=-1, keepdims=True)
        xn = x * lax.rsqrt(ms + NORM_EPS) * g_ref[...]
        xn_scr[...] = xn.astype(BF16)
        i1, i2, g1, g2 = _route(xn, rw_ref[...], rb_ref[...], lane, n_exp)
        gate_scr[...] = jnp.where(lane == i1, g1, 0.0) + jnp.where(lane == i2, g2, 0.0)
        acc_scr[...] = jnp.zeros_like(acc_scr)

    xn = xn_scr[...]
    hg = _dot(xn, wg_ref[0])
    hu = _dot(xn, wu_ref[0])
    gcol = jnp.sum(jnp.where(lane == e + N_GROUPS, gate_scr[...], 0.0), axis=-1, keepdims=True)
    hid = hg * _sigmoid(hg) * hu * gcol
    acc_scr[...] += _dot(hid.astype(BF16), wd_ref[0])

    @pl.when(e == n_exp - 1)
    def _():
        o_ref[...] = x_ref[...] + acc_scr[...]


def moe_layer(x, g, rw, rb, wg, wu, wd):
    t, d = x.shape
    n_exp, _, f = wg.shape
    tm = _row_tile(t)
    return pl.pallas_call(
        functools.partial(_moe_kernel, n_exp=n_exp),
        out_shape=jax.ShapeDtypeStruct((t, d), F32),
        grid=(t // tm, n_exp),
        in_specs=[pl.BlockSpec((tm, d), lambda i, e: (i, 0)),
                  pl.BlockSpec((1, d), lambda i, e: (0, 0)),
                  pl.BlockSpec((d, ROUTER_LANES), lambda i, e: (0, 0)),
                  pl.BlockSpec((1, ROUTER_LANES), lambda i, e: (0, 0)),
                  pl.BlockSpec((1, d, f), lambda i, e: (e, 0, 0)),
                  pl.BlockSpec((1, d, f), lambda i, e: (e, 0, 0)),
                  pl.BlockSpec((1, f, d), lambda i, e: (e, 0, 0))],
        out_specs=pl.BlockSpec((tm, d), lambda i, e: (i, 0)),
        scratch_shapes=[pltpu.VMEM((tm, d), BF16), pltpu.VMEM((tm, ROUTER_LANES), F32), pltpu.VMEM((tm, d), F32)],
        compiler_params=_cparams(("parallel", "arbitrary")),
        name="hier_moe",
    )(x, g.reshape(1, d), rw, rb, wg, wu, wd)


MOE_ROW_TILE = 512
MOE_COPY_CHUNK = 256
MOE_COMBINE_TILE = 256


def _router_kernel(x_ref, g_ref, rw_ref, rb_ref, tri_ref, gate_ref, info_ref, cnt_ref, base_scr, *, n_exp, n_tiles):
    i = pl.program_id(0)
    tm = x_ref.shape[0]
    lane = lax.broadcasted_iota(jnp.int32, (tm, ROUTER_LANES), 1)

    @pl.when(i == 0)
    def _():
        base_scr[...] = jnp.zeros_like(base_scr)

    x = x_ref[...]
    ms = jnp.mean(x * x, axis=-1, keepdims=True)
    xn = x * lax.rsqrt(ms + NORM_EPS) * g_ref[...]
    i1, i2, g1, g2 = _route(xn, rw_ref[...], rb_ref[...], lane, n_exp)
    chosen = jnp.where((lane == i1) | (lane == i2), 1.0, 0.0)
    before = _dot(tri_ref[...], chosen.astype(BF16)) + base_scr[...]
    r1 = jnp.sum(jnp.where(lane == i1, before, 0.0), axis=-1, keepdims=True)
    r2 = jnp.sum(jnp.where(lane == i2, before, 0.0), axis=-1, keepdims=True)
    base_scr[...] += jnp.sum(chosen, axis=0, keepdims=True)
    gate_ref[...] = jnp.where(lane == 0, g1, 0.0) + jnp.where(lane == 1, g2, 0.0)
    e1 = (i1 - N_GROUPS).astype(F32)
    e2 = (i2 - N_GROUPS).astype(F32)
    info_ref[...] = (jnp.where(lane == 0, e1, 0.0) + jnp.where(lane == 1, e2, 0.0)
                     + jnp.where(lane == 2, r1, 0.0) + jnp.where(lane == 3, r2, 0.0))

    @pl.when(i == n_tiles - 1)
    def _():
        cnt_ref[...] = base_scr[...]


def _scatter_rows_kernel(pos0_ref, pos1_ref, x_ref, xs_in_hbm, xs_hbm, stage, sem, *, CH, n_chunks):
    del xs_in_hbm
    c = pl.program_id(0)
    slot = c % 2
    x = x_ref[...]
    for j in range(SUBLANES):
        stage[slot, pl.ds(j, CH, stride=SUBLANES), :] = x[:, j * LANES:(j + 1) * LANES]

    def body(r, carry):
        t = c * CH + r
        src = stage.at[slot, pl.ds(pl.multiple_of(r * SUBLANES, SUBLANES), SUBLANES), :]
        d0 = pl.multiple_of(pos0_ref[t], SUBLANES)
        d1 = pl.multiple_of(pos1_ref[t], SUBLANES)
        pltpu.make_async_copy(src, xs_hbm.at[pl.ds(d0, SUBLANES), :], sem.at[slot]).start(priority=0)
        pltpu.make_async_copy(src, xs_hbm.at[pl.ds(d1, SUBLANES), :], sem.at[slot]).start(priority=1)
        return carry

    lax.fori_loop(0, CH, body, 0, unroll=8)

    def drain(s):
        pltpu.make_async_copy(stage.at[s], xs_hbm.at[pl.ds(0, CH * SUBLANES), :], sem.at[s]).wait()
        pltpu.make_async_copy(stage.at[s], xs_hbm.at[pl.ds(0, CH * SUBLANES), :], sem.at[s]).wait()

    @pl.when(c > 0)
    def _():
        drain(1 - slot)

    @pl.when(c == n_chunks - 1)
    def _():
        drain(slot)


def _tile_rows_to_matrix(ref, lead, n_rows):
    return jnp.concatenate([ref[lead + (pl.ds(j, n_rows, stride=SUBLANES), slice(None))] for j in range(SUBLANES)],
                           axis=-1)


def _expert_kernel(te_ref, nv_ref, xs_ref, g_ref, wg_ref, wu_ref, wd_ref, y_ref, *, TM):
    @pl.when(pl.program_id(0) < nv_ref[0])
    def _():
        x = _tile_rows_to_matrix(xs_ref, (), TM)
        ms = jnp.mean(x * x, axis=-1, keepdims=True)
        xn = (x * lax.rsqrt(ms + NORM_EPS) * g_ref[...]).astype(BF16)
        hg = _dot(xn, wg_ref[0])
        hu = _dot(xn, wu_ref[0])
        hid = hg * _sigmoid(hg) * hu
        y = _dot(hid.astype(BF16), wd_ref[0])
        for j in range(SUBLANES):
            y_ref[pl.ds(j, TM, stride=SUBLANES), :] = y[:, j * LANES:(j + 1) * LANES]

    @pl.when(pl.program_id(0) >= nv_ref[0])
    def _():
        y_ref[...] = jnp.zeros_like(y_ref)


def _combine_kernel(pos0_ref, pos1_ref, x_ref, gate_ref, fg_ref, y_hbm, o_ref, ybuf, sem, *, TC, n_tiles, final_norm):
    i = pl.program_id(0)

    def issue(tile, slot):
        def body(r, carry):
            t = tile * TC + r
            dst = pl.ds(pl.multiple_of(r * SUBLANES, SUBLANES), SUBLANES)
            s0 = pl.multiple_of(pos0_ref[t], SUBLANES)
            s1 = pl.multiple_of(pos1_ref[t], SUBLANES)
            pltpu.make_async_copy(y_hbm.at[pl.ds(s0, SUBLANES), :], ybuf.at[slot, 0, dst, :],
                                  sem.at[slot]).start(priority=0)
            pltpu.make_async_copy(y_hbm.at[pl.ds(s1, SUBLANES), :], ybuf.at[slot, 1, dst, :],
                                  sem.at[slot]).start(priority=1)
            return carry
        lax.fori_loop(0, TC, body, 0, unroll=8)

    @pl.when(i == 0)
    def _():
        issue(0, 0)

    @pl.when(i + 1 < n_tiles)
    def _():
        issue(i + 1, (i + 1) % 2)

    slot = i % 2
    pltpu.make_async_copy(y_hbm.at[pl.ds(0, TC * SUBLANES), :], ybuf.at[slot, 0], sem.at[slot]).wait()
    pltpu.make_async_copy(y_hbm.at[pl.ds(0, TC * SUBLANES), :], ybuf.at[slot, 1], sem.at[slot]).wait()
    gate = gate_ref[...]
    y0 = _tile_rows_to_matrix(ybuf, (slot, 0), TC)
    y1 = _tile_rows_to_matrix(ybuf, (slot, 1), TC)
    out = x_ref[...] + gate[:, 0:1] * y0 + gate[:, 1:2] * y1
    if final_norm:
        ms = jnp.mean(out * out, axis=-1, keepdims=True)
        out = out * lax.rsqrt(ms + NORM_EPS) * fg_ref[...]
    o_ref[...] = out


def moe_layer_sparse(x, g, rw, rb, wg, wu, wd, final_g=None, sorted_buf=None):
    t, d = x.shape
    n_exp, _, f = wg.shape
    TM = MOE_ROW_TILE
    n_tiles = t // TM
    tri = jnp.asarray(np.tril(np.ones((TM, TM), np.float32), -1)).astype(BF16)
    gate, info, cnt = pl.pallas_call(
        functools.partial(_router_kernel, n_exp=n_exp, n_tiles=n_tiles),
        out_shape=[jax.ShapeDtypeStruct((t, ROUTER_LANES), F32), jax.ShapeDtypeStruct((t, ROUTER_LANES), F32),
                   jax.ShapeDtypeStruct((1, ROUTER_LANES), F32)],
        grid=(n_tiles,),
        in_specs=[pl.BlockSpec((TM, d), lambda i: (i, 0)),
                  pl.BlockSpec((1, d), lambda i: (0, 0)),
                  pl.BlockSpec((d, ROUTER_LANES), lambda i: (0, 0)),
                  pl.BlockSpec((1, ROUTER_LANES), lambda i: (0, 0)),
                  pl.BlockSpec((TM, TM), lambda i: (0, 0))],
        out_specs=[pl.BlockSpec((TM, ROUTER_LANES), lambda i: (i, 0)),
                   pl.BlockSpec((TM, ROUTER_LANES), lambda i: (i, 0)),
                   pl.BlockSpec((1, ROUTER_LANES), lambda i: (0, 0))],
        scratch_shapes=[pltpu.VMEM((1, ROUTER_LANES), F32)],
        compiler_params=_cparams(("arbitrary",)),
        name="moe_router",
    )(x, g.reshape(1, d), rw, rb, tri)

    counts = cnt[0, N_GROUPS:N_GROUPS + n_exp].astype(jnp.int32)
    padded = ((counts + TM - 1) // TM) * TM
    ends = jnp.cumsum(padded)
    offs = ends - padded
    eid = info[:, 0:2].astype(jnp.int32)
    rank = info[:, 2:4].astype(jnp.int32)
    pos = jnp.sum(jnp.where(eid[:, :, None] == jnp.arange(n_exp)[None, None, :], offs[None, None, :], 0), axis=-1) + rank
    assert d == SUBLANES * LANES, "a token row must fill exactly one (8, 128) tile"
    pos = pos * SUBLANES
    pos0, pos1 = pos[:, 0], pos[:, 1]
    max_tiles = (2 * t) // TM + n_exp
    n_valid = (ends[-1] // TM).astype(jnp.int32).reshape(1)
    tile_exp = jnp.minimum(jnp.sum((ends[None, :] // TM) <= jnp.arange(max_tiles)[:, None], axis=-1),
                           n_exp - 1).astype(jnp.int32)
    p_rows = max_tiles * TM

    CH = MOE_COPY_CHUNK
    xs = pl.pallas_call(
        functools.partial(_scatter_rows_kernel, CH=CH, n_chunks=t // CH),
        out_shape=jax.ShapeDtypeStruct((p_rows * SUBLANES, LANES), F32),
        grid_spec=pltpu.PrefetchScalarGridSpec(
            num_scalar_prefetch=2, grid=(t // CH,),
            in_specs=[pl.BlockSpec((CH, d), lambda c, p0, p1: (c, 0)), pl.BlockSpec(memory_space=pl.ANY)],
            out_specs=pl.BlockSpec(memory_space=pl.ANY),
            scratch_shapes=[pltpu.VMEM((2, CH * SUBLANES, LANES), F32), pltpu.SemaphoreType.DMA((2,))]),
        input_output_aliases={3: 0},
        compiler_params=pltpu.CompilerParams(dimension_semantics=("arbitrary",), vmem_limit_bytes=VMEM_LIMIT,
                                             has_side_effects=True),
        name="moe_scatter_rows",
    )(pos0, pos1, x, jnp.zeros((p_rows * SUBLANES, LANES), F32) if sorted_buf is None else sorted_buf)

    def row_idx(i, te, nv):
        return (jnp.minimum(i, nv[0] - 1), 0)

    ys = pl.pallas_call(
        functools.partial(_expert_kernel, TM=TM),
        out_shape=jax.ShapeDtypeStruct((p_rows * SUBLANES, LANES), F32),
        grid_spec=pltpu.PrefetchScalarGridSpec(
            num_scalar_prefetch=2, grid=(max_tiles,),
            in_specs=[pl.BlockSpec((TM * SUBLANES, LANES), row_idx),
                      pl.BlockSpec((1, d), lambda i, te, nv: (0, 0)),
                      pl.BlockSpec((1, d, f), lambda i, te, nv: (te[i], 0, 0)),
                      pl.BlockSpec((1, d, f), lambda i, te, nv: (te[i], 0, 0)),
                      pl.BlockSpec((1, f, d), lambda i, te, nv: (te[i], 0, 0))],
            out_specs=pl.BlockSpec((TM * SUBLANES, LANES), lambda i, te, nv: (i, 0))),
        compiler_params=_cparams(("arbitrary",)),
        name="moe_experts",
    )(tile_exp, n_valid, xs, g.reshape(1, d), wg, wu, wd)

    TC = MOE_COMBINE_TILE
    out = pl.pallas_call(
        functools.partial(_combine_kernel, TC=TC, n_tiles=t // TC, final_norm=final_g is not None),
        out_shape=jax.ShapeDtypeStruct((t, d), F32),
        grid_spec=pltpu.PrefetchScalarGridSpec(
            num_scalar_prefetch=2, grid=(t // TC,),
            in_specs=[pl.BlockSpec((TC, d), lambda i, p0, p1: (i, 0)),
                      pl.BlockSpec((TC, ROUTER_LANES), lambda i, p0, p1: (i, 0)),
                      pl.BlockSpec((1, d), lambda i, p0, p1: (0, 0)),
                      pl.BlockSpec(memory_space=pl.ANY)],
            out_specs=pl.BlockSpec((TC, d), lambda i, p0, p1: (i, 0)),
            scratch_shapes=[pltpu.VMEM((2, 2, TC * SUBLANES, LANES), F32), pltpu.SemaphoreType.DMA((2,))]),
        compiler_params=_cparams(("arbitrary",)),
        name="moe_combine",
    )(pos0, pos1, x, gate, (g if final_g is None else final_g).reshape(1, d), ys)
    return out, xs


def _rwkv_kernel(p_ref, prev_ref, s0_ref, mu_ref, w0_ref, wd_ref, a0_ref, wa_ref, wg_ref, kk_ref, ka_ref,
                 bonus_ref, lng_ref, lnb_ref, tri_ref, hsum_ref, ya_ref, sf_ref, s_scr, prev_scr,
                 *, NB, C, H, DH, n_chunks):
    c = pl.program_id(1)

    @pl.when(c == 0)
    def _():
        s_scr[...] = s0_ref[:, 0]
        prev_scr[...] = prev_ref[:, 0]

    DA = H * DH
    R = NB * C
    p = p_ref[...].reshape(R, p_ref.shape[-1])
    row = lax.broadcasted_iota(jnp.int32, p.shape, 0)
    shifted = pltpu.roll(p, 1, axis=0)
    for n in range(NB):
        shifted = jnp.where(row == n * C, prev_scr[n], shifted)
        prev_scr[n] = p[(n + 1) * C - 1:(n + 1) * C, :]
    xs = p + (shifted - p) * mu_ref[...]
    r = xs[:, 0:DA]
    k = xs[:, DA:2 * DA]
    v = xs[:, 2 * DA:3 * DA]
    lora = xs[:, 3 * DA:3 * DA + R_DECAY + R_ICLR]
    gd = xs[:, 3 * DA + R_DECAY + R_ICLR:3 * DA + R_DECAY + R_ICLR + R_GATE]

    w_log = -_softplus(-(w0_ref[...] + _mp_dot(_dot, jnp.tanh(lora), wd_ref[...], 1))) - 0.5
    lw = -jnp.exp(w_log)
    a = _sigmoid(a0_ref[...] + _mp_dot(_dot, lora, wa_ref[...], 1))
    g = _mp_dot(_dot, _sigmoid(gd), wg_ref[...], 1)

    kk = k * kk_ref[...]
    ss = _dot_exact_rhs(kk * kk, hsum_ref[...], 2)
    kk = kk / jnp.maximum(jnp.sqrt(ss), 1e-12)
    k2 = k * (1.0 + (a - 1.0) * ka_ref[...])
    kka = kk * a

    cum = jnp.concatenate([_dot_exact_lhs(tri_ref[...], lw[n * C:(n + 1) * C], 3) for n in range(NB)], axis=0)
    p_in = jnp.exp(cum)
    r_t = r * p_in
    a_t = kk * jnp.exp(cum - lw)
    p_inv = jnp.exp(-cum)
    b_t = kka * p_inv
    k_t = k2 * p_inv
    bonus = _dot_exact_rhs(r * k2 * bonus_ref[...], hsum_ref[...], 2) * v

    ri = lax.broadcasted_iota(jnp.int32, (C, C), 0)
    ci = lax.broadcasted_iota(jnp.int32, (C, C), 1)
    strict = ri > ci
    incl = ri >= ci
    eye = (ri == ci).astype(F32)
    n_double = max(int(math.ceil(math.log2(C))) - 1, 0)

    chains = [(n, h) for n in range(NB) for h in range(H)]

    def blk(x, n, h):
        return x[n * C:(n + 1) * C, h * DH:(h + 1) * DH]

    def bf(x):
        return x.astype(BF16)

    Bt = [bf(blk(b_t, n, h)) for n, h in chains]
    Kt = [bf(blk(k_t, n, h)) for n, h in chains]
    Vf = [blk(v, n, h) for n, h in chains]
    AR = [bf(jnp.concatenate([blk(a_t, n, h), blk(r_t, n, h)], axis=0)) for n, h in chains]
    S0 = [s_scr[n, h] for n, h in chains]
    idx = range(len(chains))
    GB = [_dot_nt(AR[i], Bt[i]) for i in idx]
    GK = [_dot_nt(AR[i], Kt[i]) for i in idx]
    ARS = [_dot_nt(AR[i], bf(S0[i])) for i in idx]
    Lm = [jnp.where(strict, GB[i][0:C], 0.0) for i in idx]
    Gb = [bf(jnp.where(incl, GB[i][C:2 * C], 0.0)) for i in idx]
    MG = [bf(jnp.concatenate([jnp.where(strict, GK[i][0:C], 0.0), jnp.where(incl, GK[i][C:2 * C], 0.0)], axis=0))
          for i in idx]
    MGV = [_dot(MG[i], bf(Vf[i])) for i in idx]
    T = [eye - Lm[i] for i in idx]
    Pw = [bf(Lm[i]) for i in idx]
    for _ in range(n_double):
        Pw = [bf(_dot(Pw[i], Pw[i])) for i in idx]
        T = [T[i] + _dot(bf(T[i]), Pw[i]) for i in idx]
    U = [_dot(bf(T[i]), bf(-(ARS[i][0:C] + MGV[i][0:C]))) for i in idx]
    Y = [ARS[i][C:2 * C] + _dot(Gb[i], bf(U[i])) + MGV[i][C:2 * C] for i in idx]
    for i, (n, h) in enumerate(chains):
        UV = bf(jnp.concatenate([U[i], Vf[i]], axis=0))
        BK = jnp.concatenate([Bt[i], Kt[i]], axis=0)
        p_tot = p_in[(n + 1) * C - 1:(n + 1) * C, h * DH:(h + 1) * DH]
        s_scr[n, h] = (S0[i] + _dot_tn(UV, BK)) * p_tot

    rows = []
    for n in range(NB):
        ys = []
        for h in range(H):
            Yh = Y[n * H + h]
            yc = Yh - jnp.mean(Yh, axis=-1, keepdims=True)
            var = jnp.mean(yc * yc, axis=-1, keepdims=True)
            ys.append(yc * lax.rsqrt(var + GN_EPS))
        rows.append(jnp.concatenate(ys, axis=-1))
    y = jnp.concatenate(rows, axis=0) * lng_ref[...] + lnb_ref[...]
    ya_ref[...] = ((y + bonus) * g).reshape(NB, C, DA)

    @pl.when(c == n_chunks - 1)
    def _():
        sf_ref[:, 0] = s_scr[...]


def rwkv_mix(pa, n_batch, seq, shift_prev, wkv0, wts, n_par):
    t, ap = pa.shape
    H = wkv0.shape[1]
    DA = H * DH_A
    C = min(RWKV_CHUNK, seq)
    n_chunks = seq // C
    NB = n_par
    G = n_batch // NB
    tri = jnp.asarray(np.tril(np.ones((C, C), np.float32))).astype(BF16)
    hsum = jnp.asarray(np.kron(np.eye(H, dtype=np.float32), np.ones((DH_A, DH_A), np.float32))).astype(BF16)

    def full(shape):
        nd = len(shape)
        return pl.BlockSpec(shape, lambda b, c: (0,) * nd)

    vec = full((1, DA))
    ya, s_fin = pl.pallas_call(
        functools.partial(_rwkv_kernel, NB=NB, C=C, H=H, DH=DH_A, n_chunks=n_chunks),
        out_shape=[jax.ShapeDtypeStruct((NB, t // NB, DA), F32),
                   jax.ShapeDtypeStruct((NB, G, H, DH_A, DH_A), F32)],
        grid=(G, n_chunks),
        in_specs=[pl.BlockSpec((NB, C, ap), lambda b, c: (0, b * n_chunks + c, 0)),
                  pl.BlockSpec((NB, 1, 1, ap), lambda b, c: (0, b, 0, 0)),
                  pl.BlockSpec((NB, 1, H, DH_A, DH_A), lambda b, c: (0, b, 0, 0, 0)),
                  full((1, ap)), vec, full((R_DECAY + R_ICLR, DA)), vec, full((R_DECAY + R_ICLR, DA)),
                  full((R_GATE, DA)), vec, vec, vec, vec, vec, full((C, C)), full((DA, DA))],
        out_specs=[pl.BlockSpec((NB, C, DA), lambda b, c: (0, b * n_chunks + c, 0)),
                   pl.BlockSpec((NB, 1, H, DH_A, DH_A), lambda b, c: (0, b, 0, 0, 0))],
        scratch_shapes=[pltpu.VMEM((NB, H, DH_A, DH_A), F32), pltpu.VMEM((NB, 1, ap), F32)],
        compiler_params=_cparams(("parallel", "arbitrary")),
        name="rwkv7_mix",
    )(pa.reshape(NB, t // NB, ap), shift_prev.reshape(NB, G, 1, ap), wkv0.reshape(NB, G, H, DH_A, DH_A),
      wts["mu"], wts["w0"], wts["wd"], wts["a0"], wts["wa"],
      wts["wg"], wts["key_k"], wts["key_a"], wts["bonus"], wts["lnx_g"], wts["lnx_b"], tri, hsum)
    return ya.reshape(t, DA), s_fin.reshape(wkv0.shape)


GMLP_TILE = 128
GMLP_ROWS = 512


def _gmlp_kernel(u_ref, v_ref, ng_ref, nb_ref, wm_ref, bias_ref, o_ref, *vn_refs, n_sub):
    vf = _gelu(v_ref[...])
    mu = jnp.mean(vf, axis=-1, keepdims=True)
    vc = vf - mu
    var = jnp.mean(vc * vc, axis=-1, keepdims=True)
    vn = vc * lax.rsqrt(var + NORM_EPS) * ng_ref[...] + nb_ref[...]
    for vn_ref in vn_refs:
        vn_ref[...] = vn
    vb = vn.astype(BF16)
    n_h = wm_ref.shape[0]
    cb = vn.shape[1] // n_h
    gu = _gelu(u_ref[...])
    for c in range(n_sub):
        rows = slice(c * GMLP_TILE, (c + 1) * GMLP_TILE)
        s = jnp.concatenate([_dot(wm_ref[h], vb[rows, h * cb:(h + 1) * cb]) for h in range(n_h)], axis=-1)
        o_ref[rows, :] = gu[rows, :] * (s + bias_ref[...])


def gmlp_mix(pu, pv, ng, nb, wm_bf16, bias_tile, want_vn):
    t, db = pu.shape
    n_h = wm_bf16.shape[0]
    rows = GMLP_ROWS if t % GMLP_ROWS == 0 else t
    n_out = 2 if want_vn else 1
    outs = pl.pallas_call(
        functools.partial(_gmlp_kernel, n_sub=rows // GMLP_TILE),
        out_shape=[jax.ShapeDtypeStruct((t, db), F32)] * n_out,
        grid=(t // rows,),
        in_specs=[pl.BlockSpec((rows, db), lambda i: (i, 0)),
                  pl.BlockSpec((rows, db), lambda i: (i, 0)),
                  pl.BlockSpec((1, db), lambda i: (0, 0)),
                  pl.BlockSpec((1, db), lambda i: (0, 0)),
                  pl.BlockSpec((n_h, GMLP_TILE, GMLP_TILE), lambda i: (0, 0, 0)),
                  pl.BlockSpec((GMLP_TILE, db), lambda i: (0, 0))],
        out_specs=[pl.BlockSpec((rows, db), lambda i: (i, 0))] * n_out,
        compiler_params=_cparams(("parallel",)),
        name="gmlp_mix",
    )(pu, pv, ng, nb, wm_bf16, bias_tile)
    return (outs[0], outs[1]) if want_vn else (outs[0], None)


N_SEG = 8
SEG_GAP = 4


def _seg_pitch(seg):
    return seg + SEG_GAP if seg % SUBLANES == 0 else seg


def _rglru_kernel(xb_ref, gy_ref, cprev_ref, h0_ref, cw_ref, cb_ref, gw_ref, gb_ref, lam_ref,
                  yc_ref, ctail_ref, hl_ref, xe_scr, a_scr, b_scr, h_scr, *, TL, DC, pos0, n_tiles):
    l = pl.program_id(1)
    PAD = 8

    @pl.when(l == 0)
    def _():
        xe_scr[0:PAD, :] = cprev_ref[0]
        h_scr[...] = h0_ref[0]

    xe_scr[PAD:PAD + TL, :] = xb_ref[...]
    xc = cb_ref[...] + xe_scr[pl.ds(PAD - (CONV_W - 1), TL), :] * cw_ref[0:1, :]
    for i in range(1, CONV_W):
        xc = xc + xe_scr[pl.ds(PAD - (CONV_W - 1) + i, TL), :] * cw_ref[i:i + 1, :]
    tail = xe_scr[TL:TL + PAD, :]
    ctail_ref[0] = tail
    xe_scr[0:PAD, :] = tail

    gates = _dot(xc.astype(BF16), gw_ref[...]) + gb_ref[...]
    rg = _sigmoid(gates[:, 0:DC])
    ig = _sigmoid(gates[:, DC:2 * DC])
    log_a = -LRU_C * rg * _softplus(-lam_ref[...])
    a = jnp.exp(log_a)
    mult = jnp.sqrt(1.0 - a * a)
    row = lax.broadcasted_iota(jnp.int32, (TL, DC), 0)
    mult = jnp.where(row + (l * TL + pos0) == 0, 1.0, mult)
    b = mult * ig * xc
    n_slab = DC // LANES
    seg = TL // N_SEG
    pitch = _seg_pitch(seg)
    for s in range(n_slab):
        for j in range(N_SEG):
            a_scr[s, pl.ds(j * pitch, seg), :] = a[j * seg:(j + 1) * seg, s * LANES:(s + 1) * LANES]
            b_scr[s, pl.ds(j * pitch, seg), :] = b[j * seg:(j + 1) * seg, s * LANES:(s + 1) * LANES]

    def step(i, carry):
        idx = pl.ds(i, N_SEG, stride=pitch) if pitch > 1 else pl.ds(0, N_SEG)
        out = []
        for s in range(n_slab):
            hloc, ap = carry[s]
            ai = a_scr[s, idx, :]
            hloc = ai * hloc + b_scr[s, idx, :]
            ap = ap * ai
            b_scr[s, idx, :] = hloc
            a_scr[s, idx, :] = ap
            out.append((hloc, ap))
        return tuple(out)

    lax.fori_loop(0, seg, step,
                  tuple((jnp.zeros((N_SEG, LANES), F32), jnp.ones((N_SEG, LANES), F32)) for _ in range(n_slab)),
                  unroll=min(seg, 8))

    carry = h_scr[...]
    g_act = _gelu(gy_ref[...])
    for j in range(N_SEG):
        rows = slice(j * seg, (j + 1) * seg)
        rows_p = pl.ds(j * pitch, seg)
        hloc = jnp.concatenate([b_scr[s, rows_p, :] for s in range(n_slab)], axis=-1)
        ap = jnp.concatenate([a_scr[s, rows_p, :] for s in range(n_slab)], axis=-1)
        hj = hloc + ap * carry
        yc_ref[rows, :] = g_act[rows, :] * hj
        carry = hj[seg - 1:seg, :]
    h_scr[...] = carry

    @pl.when(l == n_tiles - 1)
    def _():
        hl_ref[0] = carry


def rglru_mix(xb, gy, n_batch, seq, conv_prev8, h0, pos0, wts):
    t, dc = xb.shape
    TL = 512 if seq % 512 == 0 else seq
    n_tiles = seq // TL
    scan_rows = N_SEG * _seg_pitch(TL // N_SEG)

    def full(shape):
        nd = len(shape)
        return pl.BlockSpec(shape, lambda b, l: (0,) * nd)

    yc, ctail, hl = pl.pallas_call(
        functools.partial(_rglru_kernel, TL=TL, DC=dc, pos0=pos0, n_tiles=n_tiles),
        out_shape=[jax.ShapeDtypeStruct((t, dc), F32), jax.ShapeDtypeStruct((n_batch, 8, dc), F32),
                   jax.ShapeDtypeStruct((n_batch, 1, dc), F32)],
        grid=(n_batch, n_tiles),
        in_specs=[pl.BlockSpec((TL, dc), lambda b, l: (b * n_tiles + l, 0)),
                  pl.BlockSpec((TL, dc), lambda b, l: (b * n_tiles + l, 0)),
                  pl.BlockSpec((1, 8, dc), lambda b, l: (b, 0, 0)),
                  pl.BlockSpec((1, 1, dc), lambda b, l: (b, 0, 0)),
                  full((CONV_W, dc)), full((1, dc)), full((dc, 2 * dc)), full((1, 2 * dc)), full((1, dc))],
        out_specs=[pl.BlockSpec((TL, dc), lambda b, l: (b * n_tiles + l, 0)),
                   pl.BlockSpec((1, 8, dc), lambda b, l: (b, 0, 0)),
                   pl.BlockSpec((1, 1, dc), lambda b, l: (b, 0, 0))],
        scratch_shapes=[pltpu.VMEM((TL + 8, dc), F32), pltpu.VMEM((dc // LANES, scan_rows, LANES), F32),
                        pltpu.VMEM((dc // LANES, scan_rows, LANES), F32), pltpu.VMEM((1, dc), F32)],
        compiler_params=_cparams(("parallel", "arbitrary")),
        name="rglru_mix",
    )(xb, gy, conv_prev8, h0.reshape(n_batch, 1, dc), wts["conv_w"], wts["conv_b"], wts["gate_w"], wts["gate_b"],
      wts["lam"])
    return yc, ctail[:, 8 - (CONV_W - 1):, :], hl.reshape(n_batch, dc)


def _t5_bucket(dist):
    dist = np.asarray(dist)
    max_exact = N_BUCKETS // 2
    scaled = np.log(np.maximum(dist, 1) / max_exact) / math.log(BUCKET_MAX_DIST / max_exact)
    large = np.minimum(max_exact + (scaled * (N_BUCKETS - max_exact)).astype(np.int32), N_BUCKETS - 1)
    return np.where(dist < max_exact, dist, large).astype(np.int32)


def _dist_table(rel_bias, max_dist):
    dist = np.arange(max_dist + 1)
    count = np.zeros(max_dist + 1, np.float32)
    for window, dil in DILATED:
        count += ((dist % dil == 0) & (dist <= window)).astype(np.float32)
    logcnt = np.where(count > 0, np.log(np.maximum(count, 1.0)), 0.0).astype(np.float32)
    tab = jnp.take(rel_bias, jnp.asarray(_t5_bucket(dist)), axis=0) + jnp.asarray(logcnt)[:, None]
    return jnp.where(jnp.asarray(count > 0)[:, None], tab, NEG_BIG)


def _toeplitz_tiles(tab, n_pos, n_neg, T):
    D, H = tab.shape
    span = T * n_pos
    assert D >= span
    n_col = span + T * n_neg + T - 1
    ext = jnp.concatenate([jnp.flip(tab[:span], axis=0), jnp.full((n_col + 1 - span, H), NEG_BIG, F32)], axis=0)
    ext = jnp.transpose(ext)
    skew = jnp.tile(ext, (1, T))[:, :T * n_col].reshape(H, T, n_col)
    tiles = [skew[:, :, span - 1 - T * dd: span - 1 - T * dd + T] for dd in range(-n_neg, n_pos)]
    return jnp.stack(tiles, axis=1)


def _attn_prompt_kernel(q_ref, k_ref, v_ref, bias_ref, o_ref, kb_scr, vb_scr, *, E, SUB, NS):
    qi = pl.program_id(2)
    TQ = NS * SUB

    @pl.when(qi == 0)
    def _():
        kb_scr[...] = k_ref[0].astype(BF16)
        vb_scr[...] = v_ref[0].astype(BF16)

    lane = lax.broadcasted_iota(jnp.int32, (SUB, 2 * E), 1)
    q2 = []
    for rs in range(NS):
        q = q_ref[0, rs * SUB:(rs + 1) * SUB, :] * (E ** -0.5 * LOG2E)
        q2.append(jnp.concatenate([jnp.where(lane < E, q, 0.0), jnp.where(lane >= E, q, 0.0)], axis=0).astype(BF16))

    def block(i, carry, diagonal):
        j = qi - i
        koff = pl.multiple_of(j * TQ, TQ)
        out = []
        for rs in range(NS):
            n_cs = rs + 1 if diagonal else NS
            kj = kb_scr[pl.ds(koff, n_cs * SUB), :]
            vj = vb_scr[pl.ds(koff, n_cs * SUB), :]
            m, l, acc = carry[rs]
            s = _dot_nt(q2[rs], kj)
            parts = []
            for cs in range(n_cs):
                dd = i * NS + (rs - cs + NS - 1)
                bias = jnp.concatenate([bias_ref[0, dd], bias_ref[1, dd]], axis=0)
                parts.append(s[:, cs * SUB:(cs + 1) * SUB] + bias)
            mx = parts[0]
            for part in parts[1:]:
                mx = jnp.maximum(mx, part)
            m_new = jnp.maximum(m, jnp.max(mx, axis=-1, keepdims=True))
            alpha = jnp.exp2(m - m_new)
            ps = [jnp.exp2(part - m_new) for part in parts]
            psum = ps[0]
            for pexp in ps[1:]:
                psum = psum + pexp
            l = alpha * l + psum
            acc = alpha * acc + _dot(jnp.concatenate(ps, axis=-1).astype(BF16), vj)
            out.append((m_new, l, acc))
        return tuple(out)

    init = tuple((jnp.full((2 * SUB, SUB), NEG_BIG, F32), jnp.zeros((2 * SUB, SUB), F32),
                  jnp.zeros((2 * SUB, 2 * E), F32)) for _ in range(NS))
    first = block(0, init, True)
    res = lax.fori_loop(1, qi + 1, lambda i, carry: block(i, carry, False), first)
    for rs in range(NS):
        m, l, acc = res[rs]
        o = acc / jnp.sum(l, axis=-1, keepdims=True)
        o_ref[0, rs * SUB:(rs + 1) * SUB, :] = jnp.where(lane < E, o[0:SUB], o[SUB:2 * SUB])


def attn_prompt(q, k, v, bias_tiles, n_batch, seq):
    hd = q.shape[-1]
    E = hd // H_D
    SUB = ATT_TILE
    NS = ATT_SUBTILES
    TQ = SUB * NS
    nq = seq // TQ
    nt = bias_tiles.shape[1]
    return pl.pallas_call(
        functools.partial(_attn_prompt_kernel, E=E, SUB=SUB, NS=NS),
        out_shape=jax.ShapeDtypeStruct((n_batch, seq, hd), F32),
        grid=(H_D // 2, n_batch, nq),
        in_specs=[pl.BlockSpec((1, TQ, 2 * E), lambda hp, b, i: (b, i, hp)),
                  pl.BlockSpec((1, seq, 2 * E), lambda hp, b, i: (b, 0, hp)),
                  pl.BlockSpec((1, seq, 2 * E), lambda hp, b, i: (b, 0, hp)),
                  pl.BlockSpec((2, nt, SUB, SUB), lambda hp, b, i: (hp, 0, 0, 0))],
        out_specs=pl.BlockSpec((1, TQ, 2 * E), lambda hp, b, i: (b, i, hp)),
        scratch_shapes=[pltpu.VMEM((seq, 2 * E), BF16), pltpu.VMEM((seq, 2 * E), BF16)],
        compiler_params=_cparams(("arbitrary", "arbitrary", "arbitrary")),
        name="dilated_attn_prompt",
    )(q, k, v, bias_tiles)


def _attn_sample_kernel(q_ref, kn_ref, vn_ref, ck_ref, cv_ref, bo_ref, bn_ref, o_ref, *, E, S):
    lane = lax.broadcasted_iota(jnp.int32, (S, 2 * E), 1)
    NPAD = bn_ref.shape[-1]
    outs = []
    for hp in range(H_D // 2):
        sl = slice(hp * 2 * E, (hp + 1) * 2 * E)
        q = q_ref[0, :, sl] * (E ** -0.5)
        q2 = jnp.concatenate([jnp.where(lane < E, q, 0.0), jnp.where(lane >= E, q, 0.0)], axis=0).astype(BF16)
        zpad = jnp.zeros((NPAD - S, 2 * E), F32)
        kn = jnp.concatenate([kn_ref[0, :, sl], zpad], axis=0).astype(BF16)
        vn = jnp.concatenate([vn_ref[0, :, sl], zpad], axis=0).astype(BF16)
        s_old = _dot_nt(q2, ck_ref[0, :, sl].astype(BF16)) + jnp.concatenate([bo_ref[2 * hp], bo_ref[2 * hp + 1]], axis=0)
        s_new = _dot_nt(q2, kn) + jnp.concatenate([bn_ref[2 * hp], bn_ref[2 * hp + 1]], axis=0)
        m = jnp.maximum(jnp.max(s_old, axis=-1, keepdims=True), jnp.max(s_new, axis=-1, keepdims=True))
        p_old = jnp.exp(s_old - m)
        p_new = jnp.exp(s_new - m)
        l = jnp.sum(p_old, axis=-1, keepdims=True) + jnp.sum(p_new, axis=-1, keepdims=True)
        acc = _dot(p_old.astype(BF16), cv_ref[0, :, sl].astype(BF16)) + _dot(p_new.astype(BF16), vn)
        o = acc / l
        outs.append(jnp.where(lane < E, o[0:S], o[S:2 * S]))
    o_ref[0] = jnp.concatenate(outs, axis=-1)


def attn_sample(q, k_new, v_new, cache_k, cache_v, bias_old, bias_new):
    n_batch, S, hd = q.shape
    W = cache_k.shape[1]
    E = hd // H_D
    NPAD = bias_new.shape[-1]
    return pl.pallas_call(
        functools.partial(_attn_sample_kernel, E=E, S=S),
        out_shape=jax.ShapeDtypeStruct((n_batch, S, hd), F32),
        grid=(n_batch,),
        in_specs=[pl.BlockSpec((1, S, hd), lambda b: (b, 0, 0)),
                  pl.BlockSpec((1, S, hd), lambda b: (b, 0, 0)),
                  pl.BlockSpec((1, S, hd), lambda b: (b, 0, 0)),
                  pl.BlockSpec((1, W, hd), lambda b: (b, 0, 0)),
                  pl.BlockSpec((1, W, hd), lambda b: (b, 0, 0)),
                  pl.BlockSpec((H_D, S, W), lambda b: (0, 0, 0)),
                  pl.BlockSpec((H_D, S, NPAD), lambda b: (0, 0, 0))],
        out_specs=pl.BlockSpec((1, S, hd), lambda b: (b, 0, 0)),
        compiler_params=_cparams(("parallel",)),
        name="dilated_attn_sample",
    )(q, k_new, v_new, cache_k, cache_v, bias_old, bias_new)


def _even_weights(j, w_in_even, w_out_even, shift_mu, decay_w0, decay_up, iclr_a0, iclr_up, gate_up, key_k, key_a,
                  bonus_r_k, lnx_g, lnx_b, sgu_norm_g, sgu_norm_b, sgu_w, sgu_b):
    da = decay_w0.shape[1]
    zeros_d = jnp.zeros((R_ICLR, da), F32)
    zeros_i = jnp.zeros((R_DECAY, da), F32)
    return dict(
        w_in=w_in_even[j].astype(BF16),
        w_out_a=w_out_even[j, :da].astype(BF16), w_out_b=w_out_even[j, da:].astype(BF16),
        mu=shift_mu[j].reshape(1, -1), w0=decay_w0[j].reshape(1, -1), a0=iclr_a0[j].reshape(1, -1),
        wd=jnp.concatenate([decay_up[j], zeros_d], axis=0).astype(BF16),
        wa=jnp.concatenate([zeros_i, iclr_up[j]], axis=0).astype(BF16), wg=gate_up[j].astype(BF16), key_k=key_k[j].reshape(1, -1), key_a=key_a[j].reshape(1, -1),
        bonus=bonus_r_k[j].reshape(1, -1), lnx_g=lnx_g[j].reshape(1, -1), lnx_b=lnx_b[j].reshape(1, -1),
        ng=sgu_norm_g[j].reshape(1, -1), nb=sgu_norm_b[j].reshape(1, -1), sgu_w=sgu_w[j], sgu_b=sgu_b[j])


def _gmlp_tables(sgu_w, sgu_b, chunk):
    reps = GMLP_TILE // chunk
    n_h = sgu_w.shape[0]
    cb = None
    wm = sgu_w[:, :chunk, :chunk] * jnp.asarray(np.tril(np.ones((chunk, chunk), np.float32)))
    if reps > 1:
        eye = jnp.asarray(np.eye(reps, dtype=np.float32))
        wm = jnp.einsum("ab,hts->hatbs", eye, wm).reshape(n_h, GMLP_TILE, GMLP_TILE)
    bias = jnp.tile(jnp.transpose(sgu_b[:, :chunk]), (reps, 1))
    return wm.astype(BF16), bias


def _even_layer(x, n_batch, seq, chunk, n_par, want_vn, shift_prev, wkv0, norm_g, ew):
    pa, pu, pv = norm_matmul(x, norm_g, ew["w_in"], (ew["mu"].shape[1], ew["ng"].shape[1], ew["ng"].shape[1]))
    ya, wkv = rwkv_mix(pa, n_batch, seq, shift_prev, wkv0, ew, n_par)
    wm, bias = _gmlp_tables(ew["sgu_w"], ew["sgu_b"], chunk)
    cb = pu.shape[1] // wm.shape[0]
    bias_tile = jnp.repeat(bias, cb, axis=1)
    yb, vn = gmlp_mix(pu, pv, ew["ng"], ew["nb"], wm, bias_tile, want_vn)
    x = proj_residual(x, ya, yb, ew["w_out_a"], ew["w_out_b"])
    last = pa.reshape(n_batch, seq, -1)[:, -1]
    return x, last, wkv, vn


def _odd_weights(j, w_in_odd, w_out_odd, conv_w, conv_b, rgate_w, rgate_b, igate_w, igate_b, lru_lambda):
    dc = conv_b.shape[1]
    eye = jnp.asarray(np.eye(H_C, dtype=np.float32))

    def blockdiag(w):
        dh = w.shape[-1]
        return jnp.einsum("ab,aij->aibj", eye, w).reshape(H_C * dh, H_C * dh)

    return dict(
        w_in=w_in_odd[j].astype(BF16),
        w_out_c=w_out_odd[j, :dc].astype(BF16), w_out_d=w_out_odd[j, dc:].astype(BF16),
        conv_w=conv_w[j], conv_b=conv_b[j].reshape(1, -1),
        gate_w=jnp.concatenate([blockdiag(rgate_w[j]), blockdiag(igate_w[j])], axis=1).astype(BF16),
        gate_b=jnp.concatenate([rgate_b[j], igate_b[j]]).reshape(1, -1),
        lam=lru_lambda[j].reshape(1, -1))


def _odd_layer(x, n_batch, seq, conv_prev, h0, pos0, caches, dist_tab, norm_g, ow):
    dc = ow["lam"].shape[1]
    gy, xb, q, k, v = norm_matmul(x, norm_g, ow["w_in"], (dc,) * 5)
    conv_prev8 = jnp.pad(conv_prev, ((0, 0), (8 - (CONV_W - 1), 0), (0, 0)))
    yc, conv_last, h_last = rglru_mix(xb, gy, n_batch, seq, conv_prev8, h0, pos0, ow)
    hd = q.shape[1]
    q3, k3, v3 = (a.reshape(n_batch, seq, hd) for a in (q, k, v))
    if caches is None:
        tiles = _toeplitz_tiles(dist_tab * LOG2E, seq // ATT_TILE, ATT_SUBTILES - 1, ATT_TILE)
        o = attn_prompt(q3, k3, v3, tiles, n_batch, seq)
    else:
        cache_k, cache_v = caches
        W = cache_k.shape[1]
        NPAD = 128
        tab_t = jnp.flip(jnp.transpose(dist_tab[:W + seq]), axis=1)
        b_old = jnp.stack([tab_t[:, seq - 1 - j:seq - 1 - j + W] for j in range(seq)], axis=1)
        d_new = np.arange(seq)[:, None] - np.arange(NPAD)[None, :]
        ok_new = (d_new >= 0) & (np.arange(NPAD)[None, :] < seq)
        b_new = jnp.take(dist_tab, jnp.asarray(np.maximum(d_new, 0)), axis=0)
        b_new = jnp.transpose(jnp.where(jnp.asarray(ok_new)[..., None], b_new, NEG_BIG), (2, 0, 1))
        ck = cache_k.reshape(n_batch, W, hd)
        cv = cache_v.reshape(n_batch, W, hd)
        o = attn_sample(q3, k3, v3, ck, cv, b_old, b_new)
    x = proj_residual(x, yc, o.reshape(n_batch * seq, hd), ow["w_out_c"], ow["w_out_d"])
    e = hd // H_D
    return x, conv_last, h_last, k3.reshape(n_batch, seq, H_D, e), v3.reshape(n_batch, seq, H_D, e)


def _moe_weights(l, router_group_w, router_group_b, router_expert_w, router_expert_b, exp_w_gate, exp_w_up,
                 exp_w_down):
    d = router_group_w.shape[1]
    n_used = N_GROUPS + router_expert_w.shape[2]
    rw = jnp.concatenate([router_group_w[l], router_expert_w[l], jnp.zeros((d, ROUTER_LANES - n_used), F32)], axis=1)
    rb = jnp.concatenate([router_group_b[l], router_expert_b[l], jnp.zeros((ROUTER_LANES - n_used,), F32)])
    return dict(rw=rw.astype(BF16), rb=rb.reshape(1, -1), wg=exp_w_gate[l].astype(BF16), wu=exp_w_up[l].astype(BF16),
                wd=exp_w_down[l].astype(BF16))


def kernel(x_prompt, x_sample, state_wkv, state_shift, state_conv, state_rglru, cache_k, cache_v, norm_mix, norm_ffn, norm_final, w_in_even, w_out_even, shift_mu, decay_w0, decay_up, iclr_a0, iclr_up, gate_up, key_k, key_a, bonus_r_k, lnx_g, lnx_b, sgu_norm_g, sgu_norm_b, sgu_w, sgu_b, w_in_odd, w_out_odd, conv_w, conv_b, rgate_w, rgate_b, igate_w, igate_b, lru_lambda, rel_bias, router_group_w, router_group_b, router_expert_w, router_expert_b, exp_w_gate, exp_w_up, exp_w_down):
    B, L, D = x_prompt.shape
    DB, S, _ = x_sample.shape
    depth = norm_mix.shape[0]
    xp = x_prompt.reshape(B * L, D)
    xs = x_sample.reshape(DB * S, D)
    W = cache_k.shape[2]
    dist_tab = _dist_table(rel_bias, max(L, W + S) - 1)

    sorted_buf = None
    wkv_p, shift_p, conv_p, lru_p, k_p, v_p = [], [], [], [], [], []
    wkv_s, shift_s, chunkv_s, conv_s, lru_s, k_s, v_s = [], [], [], [], [], [], []
    for l in range(depth):
        j = l // 2
        if l % 2 == 0:
            ew = _even_weights(j, w_in_even, w_out_even, shift_mu, decay_w0, decay_up, iclr_a0, iclr_up, gate_up,
                               key_k, key_a, bonus_r_k, lnx_g, lnx_b, sgu_norm_g, sgu_norm_b, sgu_w, sgu_b)
            a_proj = ew["mu"].shape[1]
            h_a = state_wkv.shape[2]
            xp, sh, wkv, _ = _even_layer(xp, B, L, GMLP_TILE, RWKV_PAR_PROMPT, False, jnp.zeros((B, a_proj), F32),
                                         jnp.zeros((B, h_a, DH_A, DH_A), F32), norm_mix[l], ew)
            xs, sh_s, wkv_s_new, vn_s = _even_layer(xs, DB, S, S, RWKV_PAR_SAMPLE, True, state_shift[j], state_wkv[j], norm_mix[l], ew)
            wkv_p.append(wkv)
            shift_p.append(sh)
            wkv_s.append(wkv_s_new)
            shift_s.append(sh_s)
            chunkv_s.append(vn_s.reshape(DB, S, -1))
        else:
            ow = _odd_weights(j, w_in_odd, w_out_odd, conv_w, conv_b, rgate_w, rgate_b, igate_w, igate_b, lru_lambda)
            dc = ow["lam"].shape[1]
            xp, cv, hl, kr, vr = _odd_layer(xp, B, L, jnp.zeros((B, CONV_W - 1, dc), F32), jnp.zeros((B, dc), F32),
                                            0, None, dist_tab, norm_mix[l], ow)
            xs, cv_s, hl_s, kr_s, vr_s = _odd_layer(xs, DB, S, state_conv[j], state_rglru[j], PAST_LEN,
                                                    (cache_k[j], cache_v[j]), dist_tab, norm_mix[l], ow)
            conv_p.append(cv)
            lru_p.append(hl)
            k_p.append(kr)
            v_p.append(vr)
            conv_s.append(cv_s)
            lru_s.append(hl_s)
            k_s.append(kr_s)
            v_s.append(vr_s)
        mw = _moe_weights(l, router_group_w, router_group_b, router_expert_w, router_expert_b, exp_w_gate, exp_w_up,
                          exp_w_down)
        xp, sorted_buf = moe_layer_sparse(xp, norm_ffn[l], mw["rw"], mw["rb"], mw["wg"], mw["wu"], mw["wd"],
                                          final_g=norm_final if l == depth - 1 else None, sorted_buf=sorted_buf)
        xs = moe_layer(xs, norm_ffn[l], mw["rw"], mw["rb"], mw["wg"], mw["wu"], mw["wd"])
    y_prompt = xp.reshape(B, L, D)
    y_sample = rmsnorm_call(xs, norm_final).reshape(DB, S, D)
    return (y_prompt, y_sample,
            jnp.stack(wkv_p), jnp.stack(shift_p), jnp.stack(conv_p), jnp.stack(lru_p), jnp.stack(k_p), jnp.stack(v_p),
            jnp.stack(wkv_s), jnp.stack(shift_s), jnp.stack(chunkv_s), jnp.stack(conv_s), jnp.stack(lru_s),
            jnp.stack(k_s), jnp.stack(v_s))
```

```python
import functools
import math

import numpy as np
import jax
import jax.numpy as jnp
from jax import lax
from jax.experimental import pallas as pl
from jax.experimental.pallas import tpu as pltpu

F32 = jnp.float32
BF16 = jnp.bfloat16
HI = lax.Precision.HIGHEST

PAST_LEN = 8192
DH_A = 64
R_DECAY = 64
R_ICLR = 64
R_GATE = 128
GN_EPS = 64e-5
H_B = 4
H_C = 8
CONV_W = 4
LRU_C = 8.0
H_D = 8
DILATED = ((128, 1), (512, 4), (2048, 16))
N_BUCKETS = 32
BUCKET_MAX_DIST = 2048
NEG_BIG = -1e30
N_GROUPS = 4
EXP_PER_GROUP = 4
NORM_EPS = 1e-6
LOG2E = math.log2(math.e)

VMEM_LIMIT = 56 * 1024 * 1024
RWKV_CHUNK = 64
RWKV_PAR_PROMPT = 4
RWKV_PAR_SAMPLE = 8
ATT_TILE = 128
ATT_SUBTILES = 4
LANES = 128
SUBLANES = 8


def _cparams(sem):
    return pltpu.CompilerParams(dimension_semantics=sem, vmem_limit_bytes=VMEM_LIMIT)


def _dot(a, b, precision=None):
    return jnp.dot(a, b, preferred_element_type=F32, precision=precision)


def _dot_nt(a, b, precision=None):
    return lax.dot_general(a, b, (((1,), (1,)), ((), ())), preferred_element_type=F32, precision=precision)


def _dot_tn(a, b, precision=None):
    return lax.dot_general(a, b, (((0,), (0,)), ((), ())), preferred_element_type=F32, precision=precision)


def _split_bf16(x, n):
    parts = []
    for _ in range(n):
        hi = x.astype(BF16)
        parts.append(hi)
        x = x - hi.astype(F32)
    return parts


def _mp_dot(dotfn, a, b, passes):
    if passes == 1:
        return dotfn(a.astype(BF16), b.astype(BF16))
    a_hi, a_lo = _split_bf16(a, 2)
    b_hi, b_lo = _split_bf16(b, 2)
    return dotfn(a_hi, b_hi) + (dotfn(a_hi, b_lo) + dotfn(a_lo, b_hi))


def _dot_exact_rhs(a, b_bf16, n_split):
    parts = _split_bf16(a, n_split)
    acc = _dot(parts[0], b_bf16)
    for part in parts[1:]:
        acc = acc + _dot(part, b_bf16)
    return acc


def _dot_exact_lhs(a_bf16, b, n_split):
    parts = _split_bf16(b, n_split)
    acc = _dot(a_bf16, parts[0])
    for part in parts[1:]:
        acc = acc + _dot(a_bf16, part)
    return acc


def _softplus(x):
    return jnp.maximum(x, 0.0) + jnp.log(1.0 + jnp.exp(-jnp.abs(x)))


def _sigmoid(x):
    return 1.0 / (1.0 + jnp.exp(-x))


def _gelu(x):
    c = math.sqrt(2.0 / math.pi)
    return 0.5 * x * (1.0 + jnp.tanh(c * (x + 0.044715 * (x * x * x))))


def _row_tile(t, pref=512):
    return pref if t % pref == 0 else t


def _norm_matmul_kernel(x_ref, g_ref, w_ref, *refs, splits, t_splits):
    x = x_ref[...]
    ms = jnp.mean(x * x, axis=-1, keepdims=True)
    h = (x * lax.rsqrt(ms + NORM_EPS) * g_ref[...]).astype(BF16)
    if t_splits:
        wt_ref, refs = refs[0], refs[1:]
    off = 0
    for o_ref, n in zip(refs[:len(splits)], splits):
        o_ref[...] = _dot(h, w_ref[:, off:off + n])
        off += n
    off = 0
    for o_ref, n in zip(refs[len(splits):], t_splits):
        o_ref[0] = _dot_nt(wt_ref[off:off + n, :], h)
        off += n


def norm_matmul(x, g, w_bf16, splits, wt_bf16=None, t_splits=(), seq=None):
    t, d = x.shape
    n = w_bf16.shape[1]
    tm = _row_tile(t)
    in_specs = [pl.BlockSpec((tm, d), lambda i: (i, 0)),
                pl.BlockSpec((1, d), lambda i: (0, 0)),
                pl.BlockSpec((d, n), lambda i: (0, 0))]
    args = [x, g.reshape(1, d), w_bf16]
    out_shape = [jax.ShapeDtypeStruct((t, s), F32) for s in splits]
    out_specs = [pl.BlockSpec((tm, s), lambda i: (i, 0)) for s in splits]
    if t_splits:
        tiles = seq // tm
        in_specs.append(pl.BlockSpec(wt_bf16.shape, lambda i: (0, 0)))
        args.append(wt_bf16)
        out_shape += [jax.ShapeDtypeStruct((t // seq, s, seq), F32) for s in t_splits]
        out_specs += [pl.BlockSpec((1, s, tm), lambda i: (i // tiles, 0, i % tiles)) for s in t_splits]
    return pl.pallas_call(
        functools.partial(_norm_matmul_kernel, splits=splits, t_splits=tuple(t_splits)),
        out_shape=out_shape,
        grid=(t // tm,),
        in_specs=in_specs,
        out_specs=out_specs,
        compiler_params=_cparams(("parallel",)),
        name="norm_matmul",
    )(*args)


def _proj_res_kernel(x_ref, a_ref, b_ref, wa_ref, wb_ref, o_ref):
    acc = _dot(a_ref[...].astype(BF16), wa_ref[...]) + _dot(b_ref[...].astype(BF16), wb_ref[...])
    o_ref[...] = x_ref[...] + acc


def proj_residual(x, a, b, wa, wb):
    t, d = x.shape
    tm = _row_tile(t)
    ka, kb = a.shape[1], b.shape[1]
    return pl.pallas_call(
        _proj_res_kernel,
        out_shape=jax.ShapeDtypeStruct((t, d), F32),
        grid=(t // tm,),
        in_specs=[pl.BlockSpec((tm, d), lambda i: (i, 0)),
                  pl.BlockSpec((tm, ka), lambda i: (i, 0)),
                  pl.BlockSpec((tm, kb), lambda i: (i, 0)),
                  pl.BlockSpec((ka, d), lambda i: (0, 0)),
                  pl.BlockSpec((kb, d), lambda i: (0, 0))],
        out_specs=pl.BlockSpec((tm, d), lambda i: (i, 0)),
        compiler_params=_cparams(("parallel",)),
        name="proj_residual",
    )(x, a, b, wa, wb)


def _rmsnorm_kernel(x_ref, g_ref, o_ref):
    x = x_ref[...]
    ms = jnp.mean(x * x, axis=-1, keepdims=True)
    o_ref[...] = x * lax.rsqrt(ms + NORM_EPS) * g_ref[...]


def rmsnorm_call(x, g):
    t, d = x.shape
    tm = _row_tile(t)
    return pl.pallas_call(
        _rmsnorm_kernel,
        out_shape=jax.ShapeDtypeStruct((t, d), F32),
        grid=(t // tm,),
        in_specs=[pl.BlockSpec((tm, d), lambda i: (i, 0)), pl.BlockSpec((1, d), lambda i: (0, 0))],
        out_specs=pl.BlockSpec((tm, d), lambda i: (i, 0)),
        compiler_params=_cparams(("parallel",)),
        name="final_rmsnorm",
    )(x, g.reshape(1, d))


ROUTER_LANES = 128


def _route(xn, rw, rb, lane, n_exp):
    logits = _dot_exact_rhs(xn, rw, 2) + rb
    lg = jnp.where(lane < N_GROUPS, logits, -jnp.inf)
    gm = jnp.max(lg, axis=-1, keepdims=True)
    top_pg = 1.0 / jnp.sum(jnp.exp(lg - gm), axis=-1, keepdims=True)
    grp = jnp.min(jnp.where(lg == gm, lane, ROUTER_LANES), axis=-1, keepdims=True)
    in_grp = (lane >= N_GROUPS) & (lane < N_GROUPS + n_exp) & (((lane - N_GROUPS) // EXP_PER_GROUP) == grp)
    le = jnp.where(in_grp, logits, -jnp.inf)
    t1 = jnp.max(le, axis=-1, keepdims=True)
    i1 = jnp.min(jnp.where(le == t1, lane, ROUTER_LANES), axis=-1, keepdims=True)
    le2 = jnp.where(lane == i1, -jnp.inf, le)
    t2 = jnp.max(le2, axis=-1, keepdims=True)
    i2 = jnp.min(jnp.where(le2 == t2, lane, ROUTER_LANES), axis=-1, keepdims=True)
    ex = jnp.exp(t2 - t1)
    w1 = 1.0 / (1.0 + ex)
    return i1, i2, w1 * top_pg, (ex * w1) * top_pg


def _moe_kernel(x_ref, g_ref, rw_ref, rb_ref, wg_ref, wu_ref, wd_ref, o_ref, xn_scr, gate_scr, acc_scr, *, n_exp):
    e = pl.program_id(1)
    tm = x_ref.shape[0]
    lane = lax.broadcasted_iota(jnp.int32, (tm, ROUTER_LANES), 1)

    @pl.when(e == 0)
    def _():
        x = x_ref[...]
        ms = jnp.mean(x * x, axis=-1, keepdims=True)
        xn = x * lax.rsqrt(ms + NORM_EPS) * g_ref[...]
        xn_scr[...] = xn.astype(BF16)
        i1, i2, g1, g2 = _route(xn, rw_ref[...], rb_ref[...], lane, n_exp)
        gate_scr[...] = jnp.where(lane == i1, g1, 0.0) + jnp.where(lane == i2, g2, 0.0)
        acc_scr[...] = jnp.zeros_like(acc_scr)

    xn = xn_scr[...]
    hg = _dot(xn, wg_ref[0])
    hu = _dot(xn, wu_ref[0])
    gcol = jnp.sum(jnp.where(lane == e + N_GROUPS, gate_scr[...], 0.0), axis=-1, keepdims=True)
    hid = hg * _sigmoid(hg) * hu * gcol
    acc_scr[...] += _dot(hid.astype(BF16), wd_ref[0])

    @pl.when(e == n_exp - 1)
    def _():
        o_ref[...] = x_ref[...] + acc_scr[...]


def moe_layer(x, g, rw, rb, wg, wu, wd):
    t, d = x.shape
    n_exp, _, f = wg.shape
    tm = _row_tile(t)
    return pl.pallas_call(
        functools.partial(_moe_kernel, n_exp=n_exp),
        out_shape=jax.ShapeDtypeStruct((t, d), F32),
        grid=(t // tm, n_exp),
        in_specs=[pl.BlockSpec((tm, d), lambda i, e: (i, 0)),
                  pl.BlockSpec((1, d), lambda i, e: (0, 0)),
                  pl.BlockSpec((d, ROUTER_LANES), lambda i, e: (0, 0)),
                  pl.BlockSpec((1, ROUTER_LANES), lambda i, e: (0, 0)),
                  pl.BlockSpec((1, d, f), lambda i, e: (e, 0, 0)),
                  pl.BlockSpec((1, d, f), lambda i, e: (e, 0, 0)),
                  pl.BlockSpec((1, f, d), lambda i, e: (e, 0, 0))],
        out_specs=pl.BlockSpec((tm, d), lambda i, e: (i, 0)),
        scratch_shapes=[pltpu.VMEM((tm, d), BF16), pltpu.VMEM((tm, ROUTER_LANES), F32), pltpu.VMEM((tm, d), F32)],
        compiler_params=_cparams(("parallel", "arbitrary")),
        name="hier_moe",
    )(x, g.reshape(1, d), rw, rb, wg, wu, wd)


MOE_ROW_TILE = 512
MOE_COPY_CHUNK = 256
MOE_COMBINE_TILE = 256


def _router_kernel(x_ref, g_ref, rw_ref, rb_ref, tri_ref, gate_ref, info_ref, cnt_ref, base_scr, *, n_exp, n_tiles):
    i = pl.program_id(0)
    tm = x_ref.shape[0]
    lane = lax.broadcasted_iota(jnp.int32, (tm, ROUTER_LANES), 1)

    @pl.when(i == 0)
    def _():
        base_scr[...] = jnp.zeros_like(base_scr)

    x = x_ref[...]
    ms = jnp.mean(x * x, axis=-1, keepdims=True)
    xn = x * lax.rsqrt(ms + NORM_EPS) * g_ref[...]
    i1, i2, g1, g2 = _route(xn, rw_ref[...], rb_ref[...], lane, n_exp)
    chosen = jnp.where((lane == i1) | (lane == i2), 1.0, 0.0)
    before = _dot(tri_ref[...], chosen.astype(BF16)) + base_scr[...]
    r1 = jnp.sum(jnp.where(lane == i1, before, 0.0), axis=-1, keepdims=True)
    r2 = jnp.sum(jnp.where(lane == i2, before, 0.0), axis=-1, keepdims=True)
    base_scr[...] += jnp.sum(chosen, axis=0, keepdims=True)
    gate_ref[...] = jnp.where(lane == 0, g1, 0.0) + jnp.where(lane == 1, g2, 0.0)
    e1 = (i1 - N_GROUPS).astype(F32)
    e2 = (i2 - N_GROUPS).astype(F32)
    info_ref[...] = (jnp.where(lane == 0, e1, 0.0) + jnp.where(lane == 1, e2, 0.0)
                     + jnp.where(lane == 2, r1, 0.0) + jnp.where(lane == 3, r2, 0.0))

    @pl.when(i == n_tiles - 1)
    def _():
        cnt_ref[...] = base_scr[...]


def _scatter_rows_kernel(pos0_ref, pos1_ref, x_ref, xs_in_hbm, xs_hbm, stage, sem, *, CH, n_chunks):
    del xs_in_hbm
    c = pl.program_id(0)
    slot = c % 2
    x = x_ref[...]
    for j in range(SUBLANES):
        stage[slot, pl.ds(j, CH, stride=SUBLANES), :] = x[:, j * LANES:(j + 1) * LANES]

    def body(r, carry):
        t = c * CH + r
        src = stage.at[slot, pl.ds(pl.multiple_of(r * SUBLANES, SUBLANES), SUBLANES), :]
        d0 = pl.multiple_of(pos0_ref[t], SUBLANES)
        d1 = pl.multiple_of(pos1_ref[t], SUBLANES)
        pltpu.make_async_copy(src, xs_hbm.at[pl.ds(d0, SUBLANES), :], sem.at[slot]).start(priority=0)
        pltpu.make_async_copy(src, xs_hbm.at[pl.ds(d1, SUBLANES), :], sem.at[slot]).start(priority=1)
        return carry

    lax.fori_loop(0, CH, body, 0, unroll=8)

    def drain(s):
        pltpu.make_async_copy(stage.at[s], xs_hbm.at[pl.ds(0, CH * SUBLANES), :], sem.at[s]).wait()
        pltpu.make_async_copy(stage.at[s], xs_hbm.at[pl.ds(0, CH * SUBLANES), :], sem.at[s]).wait()

    @pl.when(c > 0)
    def _():
        drain(1 - slot)

    @pl.when(c == n_chunks - 1)
    def _():
        drain(slot)


def _tile_rows_to_matrix(ref, lead, n_rows):
    return jnp.concatenate([ref[lead + (pl.ds(j, n_rows, stride=SUBLANES), slice(None))] for j in range(SUBLANES)],
                           axis=-1)


def _expert_kernel(te_ref, nv_ref, xs_ref, g_ref, wg_ref, wu_ref, wd_ref, y_ref, *, TM):
    @pl.when(pl.program_id(0) < nv_ref[0])
    def _():
        x = _tile_rows_to_matrix(xs_ref, (), TM)
        ms = jnp.mean(x * x, axis=-1, keepdims=True)
        xn = (x * lax.rsqrt(ms + NORM_EPS) * g_ref[...]).astype(BF16)
        hg = _dot(xn, wg_ref[0])
        hu = _dot(xn, wu_ref[0])
        hid = hg * _sigmoid(hg) * hu
        y = _dot(hid.astype(BF16), wd_ref[0])
        for j in range(SUBLANES):
            y_ref[pl.ds(j, TM, stride=SUBLANES), :] = y[:, j * LANES:(j + 1) * LANES]

    @pl.when(pl.program_id(0) >= nv_ref[0])
    def _():
        y_ref[...] = jnp.zeros_like(y_ref)


def _combine_kernel(pos0_ref, pos1_ref, x_ref, gate_ref, fg_ref, y_hbm, o_ref, ybuf, sem, *, TC, n_tiles, final_norm):
    i = pl.program_id(0)

    def issue(tile, slot):
        def body(r, carry):
            t = tile * TC + r
            dst = pl.ds(pl.multiple_of(r * SUBLANES, SUBLANES), SUBLANES)
            s0 = pl.multiple_of(pos0_ref[t], SUBLANES)
            s1 = pl.multiple_of(pos1_ref[t], SUBLANES)
            pltpu.make_async_copy(y_hbm.at[pl.ds(s0, SUBLANES), :], ybuf.at[slot, 0, dst, :],
                                  sem.at[slot]).start(priority=0)
            pltpu.make_async_copy(y_hbm.at[pl.ds(s1, SUBLANES), :], ybuf.at[slot, 1, dst, :],
                                  sem.at[slot]).start(priority=1)
            return carry
        lax.fori_loop(0, TC, body, 0, unroll=8)

    @pl.when(i == 0)
    def _():
        issue(0, 0)

    @pl.when(i + 1 < n_tiles)
    def _():
        issue(i + 1, (i + 1) % 2)

    slot = i % 2
    pltpu.make_async_copy(y_hbm.at[pl.ds(0, TC * SUBLANES), :], ybuf.at[slot, 0], sem.at[slot]).wait()
    pltpu.make_async_copy(y_hbm.at[pl.ds(0, TC * SUBLANES), :], ybuf.at[slot, 1], sem.at[slot]).wait()
    gate = gate_ref[...]
    y0 = _tile_rows_to_matrix(ybuf, (slot, 0), TC)
    y1 = _tile_rows_to_matrix(ybuf, (slot, 1), TC)
    out = x_ref[...] + gate[:, 0:1] * y0 + gate[:, 1:2] * y1
    if final_norm:
        ms = jnp.mean(out * out, axis=-1, keepdims=True)
        out = out * lax.rsqrt(ms + NORM_EPS) * fg_ref[...]
    o_ref[...] = out


def moe_layer_sparse(x, g, rw, rb, wg, wu, wd, final_g=None, sorted_buf=None):
    t, d = x.shape
    n_exp, _, f = wg.shape
    TM = MOE_ROW_TILE
    n_tiles = t // TM
    tri = jnp.asarray(np.tril(np.ones((TM, TM), np.float32), -1)).astype(BF16)
    gate, info, cnt = pl.pallas_call(
        functools.partial(_router_kernel, n_exp=n_exp, n_tiles=n_tiles),
        out_shape=[jax.ShapeDtypeStruct((t, ROUTER_LANES), F32), jax.ShapeDtypeStruct((t, ROUTER_LANES), F32),
                   jax.ShapeDtypeStruct((1, ROUTER_LANES), F32)],
        grid=(n_tiles,),
        in_specs=[pl.BlockSpec((TM, d), lambda i: (i, 0)),
                  pl.BlockSpec((1, d), lambda i: (0, 0)),
                  pl.BlockSpec((d, ROUTER_LANES), lambda i: (0, 0)),
                  pl.BlockSpec((1, ROUTER_LANES), lambda i: (0, 0)),
                  pl.BlockSpec((TM, TM), lambda i: (0, 0))],
        out_specs=[pl.BlockSpec((TM, ROUTER_LANES), lambda i: (i, 0)),
                   pl.BlockSpec((TM, ROUTER_LANES), lambda i: (i, 0)),
                   pl.BlockSpec((1, ROUTER_LANES), lambda i: (0, 0))],
        scratch_shapes=[pltpu.VMEM((1, ROUTER_LANES), F32)],
        compiler_params=_cparams(("arbitrary",)),
        name="moe_router",
    )(x, g.reshape(1, d), rw, rb, tri)

    counts = cnt[0, N_GROUPS:N_GROUPS + n_exp].astype(jnp.int32)
    padded = ((counts + TM - 1) // TM) * TM
    ends = jnp.cumsum(padded)
    offs = ends - padded
    eid = info[:, 0:2].astype(jnp.int32)
    rank = info[:, 2:4].astype(jnp.int32)
    pos = jnp.sum(jnp.where(eid[:, :, None] == jnp.arange(n_exp)[None, None, :], offs[None, None, :], 0), axis=-1) + rank
    assert d == SUBLANES * LANES, "a token row must fill exactly one (8, 128) tile"
    pos = pos * SUBLANES
    pos0, pos1 = pos[:, 0], pos[:, 1]
    max_tiles = (2 * t) // TM + n_exp
    n_valid = (ends[-1] // TM).astype(jnp.int32).reshape(1)
    tile_exp = jnp.minimum(jnp.sum((ends[None, :] // TM) <= jnp.arange(max_tiles)[:, None], axis=-1),
                           n_exp - 1).astype(jnp.int32)
    p_rows = max_tiles * TM

    CH = MOE_COPY_CHUNK
    xs = pl.pallas_call(
        functools.partial(_scatter_rows_kernel, CH=CH, n_chunks=t // CH),
        out_shape=jax.ShapeDtypeStruct((p_rows * SUBLANES, LANES), F32),
        grid_spec=pltpu.PrefetchScalarGridSpec(
            num_scalar_prefetch=2, grid=(t // CH,),
            in_specs=[pl.BlockSpec((CH, d), lambda c, p0, p1: (c, 0)), pl.BlockSpec(memory_space=pl.ANY)],
            out_specs=pl.BlockSpec(memory_space=pl.ANY),
            scratch_shapes=[pltpu.VMEM((2, CH * SUBLANES, LANES), F32), pltpu.SemaphoreType.DMA((2,))]),
        input_output_aliases={3: 0},
        compiler_params=pltpu.CompilerParams(dimension_semantics=("arbitrary",), vmem_limit_bytes=VMEM_LIMIT,
                                             has_side_effects=True),
        name="moe_scatter_rows",
    )(pos0, pos1, x, jnp.zeros((p_rows * SUBLANES, LANES), F32) if sorted_buf is None else sorted_buf)

    def row_idx(i, te, nv):
        return (jnp.minimum(i, nv[0] - 1), 0)

    ys = pl.pallas_call(
        functools.partial(_expert_kernel, TM=TM),
        out_shape=jax.ShapeDtypeStruct((p_rows * SUBLANES, LANES), F32),
        grid_spec=pltpu.PrefetchScalarGridSpec(
            num_scalar_prefetch=2, grid=(max_tiles,),
            in_specs=[pl.BlockSpec((TM * SUBLANES, LANES), row_idx),
                      pl.BlockSpec((1, d), lambda i, te, nv: (0, 0)),
                      pl.BlockSpec((1, d, f), lambda i, te, nv: (te[i], 0, 0)),
                      pl.BlockSpec((1, d, f), lambda i, te, nv: (te[i], 0, 0)),
                      pl.BlockSpec((1, f, d), lambda i, te, nv: (te[i], 0, 0))],
            out_specs=pl.BlockSpec((TM * SUBLANES, LANES), lambda i, te, nv: (i, 0))),
        compiler_params=_cparams(("arbitrary",)),
        name="moe_experts",
    )(tile_exp, n_valid, xs, g.reshape(1, d), wg, wu, wd)

    TC = MOE_COMBINE_TILE
    out = pl.pallas_call(
        functools.partial(_combine_kernel, TC=TC, n_tiles=t // TC, final_norm=final_g is not None),
        out_shape=jax.ShapeDtypeStruct((t, d), F32),
        grid_spec=pltpu.PrefetchScalarGridSpec(
            num_scalar_prefetch=2, grid=(t // TC,),
            in_specs=[pl.BlockSpec((TC, d), lambda i, p0, p1: (i, 0)),
                      pl.BlockSpec((TC, ROUTER_LANES), lambda i, p0, p1: (i, 0)),
                      pl.BlockSpec((1, d), lambda i, p0, p1: (0, 0)),
                      pl.BlockSpec(memory_space=pl.ANY)],
            out_specs=pl.BlockSpec((TC, d), lambda i, p0, p1: (i, 0)),
            scratch_shapes=[pltpu.VMEM((2, 2, TC * SUBLANES, LANES), F32), pltpu.SemaphoreType.DMA((2,))]),
        compiler_params=_cparams(("arbitrary",)),
        name="moe_combine",
    )(pos0, pos1, x, gate, (g if final_g is None else final_g).reshape(1, d), ys)
    return out, xs


def _rwkv_kernel(p_ref, prev_ref, s0_ref, mu_ref, w0_ref, wd_ref, a0_ref, wa_ref, wg_ref, kk_ref, ka_ref,
                 bonus_ref, lng_ref, lnb_ref, tri_ref, hsum_ref, ya_ref, sf_ref, s_scr, prev_scr,
                 *, NB, C, H, DH, n_chunks):
    c = pl.program_id(1)

    @pl.when(c == 0)
    def _():
        s_scr[...] = s0_ref[:, 0]
        prev_scr[...] = prev_ref[:, 0]

    DA = H * DH
    R = NB * C
    p = p_ref[...].reshape(R, p_ref.shape[-1])
    row = lax.broadcasted_iota(jnp.int32, p.shape, 0)
    shifted = pltpu.roll(p, 1, axis=0)
    for n in range(NB):
        shifted = jnp.where(row == n * C, prev_scr[n], shifted)
        prev_scr[n] = p[(n + 1) * C - 1:(n + 1) * C, :]
    xs = p + (shifted - p) * mu_ref[...]
    r = xs[:, 0:DA]
    k = xs[:, DA:2 * DA]
    v = xs[:, 2 * DA:3 * DA]
    lora = xs[:, 3 * DA:3 * DA + R_DECAY + R_ICLR]
    gd = xs[:, 3 * DA + R_DECAY + R_ICLR:3 * DA + R_DECAY + R_ICLR + R_GATE]

    w_log = -_softplus(-(w0_ref[...] + _mp_dot(_dot, jnp.tanh(lora), wd_ref[...], 1))) - 0.5
    lw = -jnp.exp(w_log)
    a = _sigmoid(a0_ref[...] + _mp_dot(_dot, lora, wa_ref[...], 1))
    g = _mp_dot(_dot, _sigmoid(gd), wg_ref[...], 1)

    kk = k * kk_ref[...]
    ss = _dot_exact_rhs(kk * kk, hsum_ref[...], 2)
    kk = kk / jnp.maximum(jnp.sqrt(ss), 1e-12)
    k2 = k * (1.0 + (a - 1.0) * ka_ref[...])
    kka = kk * a

    cum = jnp.concatenate([_dot_exact_lhs(tri_ref[...], lw[n * C:(n + 1) * C], 3) for n in range(NB)], axis=0)
    p_in = jnp.exp(cum)
    r_t = r * p_in
    a_t = kk * jnp.exp(cum - lw)
    p_inv = jnp.exp(-cum)
    b_t = kka * p_inv
    k_t = k2 * p_inv
    bonus = _dot_exact_rhs(r * k2 * bonus_ref[...], hsum_ref[...], 2) * v

    ri = lax.broadcasted_iota(jnp.int32, (C, C), 0)
    ci = lax.broadcasted_iota(jnp.int32, (C, C), 1)
    strict = ri > ci
    incl = ri >= ci
    eye = (ri == ci).astype(F32)
    n_double = max(int(math.ceil(math.log2(C))) - 1, 0)

    chains = [(n, h) for n in range(NB) for h in range(H)]

    def blk(x, n, h):
        return x[n * C:(n + 1) * C, h * DH:(h + 1) * DH]

    def bf(x):
        return x.astype(BF16)

    Bt = [bf(blk(b_t, n, h)) for n, h in chains]
    Kt = [bf(blk(k_t, n, h)) for n, h in chains]
    Vf = [blk(v, n, h) for n, h in chains]
    AR = [bf(jnp.concatenate([blk(a_t, n, h), blk(r_t, n, h)], axis=0)) for n, h in chains]
    S0 = [s_scr[n, h] for n, h in chains]
    idx = range(len(chains))
    GB = [_dot_nt(AR[i], Bt[i]) for i in idx]
    GK = [_dot_nt(AR[i], Kt[i]) for i in idx]
    ARS = [_dot_nt(AR[i], bf(S0[i])) for i in idx]
    Lm = [jnp.where(strict, GB[i][0:C], 0.0) for i in idx]
    Gb = [bf(jnp.where(incl, GB[i][C:2 * C], 0.0)) for i in idx]
    MG = [bf(jnp.concatenate([jnp.where(strict, GK[i][0:C], 0.0), jnp.where(incl, GK[i][C:2 * C], 0.0)], axis=0))
          for i in idx]
    MGV = [_dot(MG[i], bf(Vf[i])) for i in idx]
    T = [eye - Lm[i] for i in idx]
    Pw = [bf(Lm[i]) for i in idx]
    for _ in range(n_double):
        Pw = [bf(_dot(Pw[i], Pw[i])) for i in idx]
        T = [T[i] + _dot(bf(T[i]), Pw[i]) for i in idx]
    U = [_dot(bf(T[i]), bf(-(ARS[i][0:C] + MGV[i][0:C]))) for i in idx]
    Y = [ARS[i][C:2 * C] + _dot(Gb[i], bf(U[i])) + MGV[i][C:2 * C] for i in idx]
    for i, (n, h) in enumerate(chains):
        UV = bf(jnp.concatenate([U[i], Vf[i]], axis=0))
        BK = jnp.concatenate([Bt[i], Kt[i]], axis=0)
        p_tot = p_in[(n + 1) * C - 1:(n + 1) * C, h * DH:(h + 1) * DH]
        s_scr[n, h] = (S0[i] + _dot_tn(UV, BK)) * p_tot

    rows = []
    for n in range(NB):
        ys = []
        for h in range(H):
            Yh = Y[n * H + h]
            yc = Yh - jnp.mean(Yh, axis=-1, keepdims=True)
            var = jnp.mean(yc * yc, axis=-1, keepdims=True)
            ys.append(yc * lax.rsqrt(var + GN_EPS))
        rows.append(jnp.concatenate(ys, axis=-1))
    y = jnp.concatenate(rows, axis=0) * lng_ref[...] + lnb_ref[...]
    ya_ref[...] = ((y + bonus) * g).reshape(NB, C, DA)

    @pl.when(c == n_chunks - 1)
    def _():
        sf_ref[:, 0] = s_scr[...]


def rwkv_mix(pa, n_batch, seq, shift_prev, wkv0, wts, n_par):
    t, ap = pa.shape
    H = wkv0.shape[1]
    DA = H * DH_A
    C = min(RWKV_CHUNK, seq)
    n_chunks = seq // C
    NB = n_par
    G = n_batch // NB
    tri = jnp.asarray(np.tril(np.ones((C, C), np.float32))).astype(BF16)
    hsum = jnp.asarray(np.kron(np.eye(H, dtype=np.float32), np.ones((DH_A, DH_A), np.float32))).astype(BF16)

    def full(shape):
        nd = len(shape)
        return pl.BlockSpec(shape, lambda b, c: (0,) * nd)

    vec = full((1, DA))
    ya, s_fin = pl.pallas_call(
        functools.partial(_rwkv_kernel, NB=NB, C=C, H=H, DH=DH_A, n_chunks=n_chunks),
        out_shape=[jax.ShapeDtypeStruct((NB, t // NB, DA), F32),
                   jax.ShapeDtypeStruct((NB, G, H, DH_A, DH_A), F32)],
        grid=(G, n_chunks),
        in_specs=[pl.BlockSpec((NB, C, ap), lambda b, c: (0, b * n_chunks + c, 0)),
                  pl.BlockSpec((NB, 1, 1, ap), lambda b, c: (0, b, 0, 0)),
                  pl.BlockSpec((NB, 1, H, DH_A, DH_A), lambda b, c: (0, b, 0, 0, 0)),
                  full((1, ap)), vec, full((R_DECAY + R_ICLR, DA)), vec, full((R_DECAY + R_ICLR, DA)),
                  full((R_GATE, DA)), vec, vec, vec, vec, vec, full((C, C)), full((DA, DA))],
        out_specs=[pl.BlockSpec((NB, C, DA), lambda b, c: (0, b * n_chunks + c, 0)),
                   pl.BlockSpec((NB, 1, H, DH_A, DH_A), lambda b, c: (0, b, 0, 0, 0))],
        scratch_shapes=[pltpu.VMEM((NB, H, DH_A, DH_A), F32), pltpu.VMEM((NB, 1, ap), F32)],
        compiler_params=_cparams(("parallel", "arbitrary")),
        name="rwkv7_mix",
    )(pa.reshape(NB, t // NB, ap), shift_prev.reshape(NB, G, 1, ap), wkv0.reshape(NB, G, H, DH_A, DH_A),
      wts["mu"], wts["w0"], wts["wd"], wts["a0"], wts["wa"],
      wts["wg"], wts["key_k"], wts["key_a"], wts["bonus"], wts["lnx_g"], wts["lnx_b"], tri, hsum)
    return ya.reshape(t, DA), s_fin.reshape(wkv0.shape)


GMLP_TILE = 128
GMLP_ROWS = 512


def _gmlp_kernel(u_ref, v_ref, ng_ref, nb_ref, wm_ref, bias_ref, o_ref, *vn_refs, n_sub):
    vf = _gelu(v_ref[...])
    mu = jnp.mean(vf, axis=-1, keepdims=True)
    vc = vf - mu
    var = jnp.mean(vc * vc, axis=-1, keepdims=True)
    vn = vc * lax.rsqrt(var + NORM_EPS) * ng_ref[...] + nb_ref[...]
    for vn_ref in vn_refs:
        vn_ref[...] = vn
    vb = vn.astype(BF16)
    n_h = wm_ref.shape[0]
    cb = vn.shape[1] // n_h
    gu = _gelu(u_ref[...])
    for c in range(n_sub):
        rows = slice(c * GMLP_TILE, (c + 1) * GMLP_TILE)
        s = jnp.concatenate([_dot(wm_ref[h], vb[rows, h * cb:(h + 1) * cb]) for h in range(n_h)], axis=-1)
        o_ref[rows, :] = gu[rows, :] * (s + bias_ref[...])


def gmlp_mix(pu, pv, ng, nb, wm_bf16, bias_tile, want_vn):
    t, db = pu.shape
    n_h = wm_bf16.shape[0]
    rows = GMLP_ROWS if t % GMLP_ROWS == 0 else t
    n_out = 2 if want_vn else 1
    outs = pl.pallas_call(
        functools.partial(_gmlp_kernel, n_sub=rows // GMLP_TILE),
        out_shape=[jax.ShapeDtypeStruct((t, db), F32)] * n_out,
        grid=(t // rows,),
        in_specs=[pl.BlockSpec((rows, db), lambda i: (i, 0)),
                  pl.BlockSpec((rows, db), lambda i: (i, 0)),
                  pl.BlockSpec((1, db), lambda i: (0, 0)),
                  pl.BlockSpec((1, db), lambda i: (0, 0)),
                  pl.BlockSpec((n_h, GMLP_TILE, GMLP_TILE), lambda i: (0, 0, 0)),
                  pl.BlockSpec((GMLP_TILE, db), lambda i: (0, 0))],
        out_specs=[pl.BlockSpec((rows, db), lambda i: (i, 0))] * n_out,
        compiler_params=_cparams(("parallel",)),
        name="gmlp_mix",
    )(pu, pv, ng, nb, wm_bf16, bias_tile)
    return (outs[0], outs[1]) if want_vn else (outs[0], None)


N_SEG = 8
SEG_GAP = 4


def _seg_pitch(seg):
    return seg + SEG_GAP if seg % SUBLANES == 0 else seg


def _rglru_kernel(xb_ref, gy_ref, cprev_ref, h0_ref, cw_ref, cb_ref, gw_ref, gb_ref, lam_ref,
                  yc_ref, ctail_ref, hl_ref, xe_scr, a_scr, b_scr, h_scr, *, TL, DC, pos0, n_tiles):
    l = pl.program_id(1)
    PAD = 8

    @pl.when(l == 0)
    def _():
        xe_scr[0:PAD, :] = cprev_ref[0]
        h_scr[...] = h0_ref[0]

    xe_scr[PAD:PAD + TL, :] = xb_ref[...]
    xc = cb_ref[...] + xe_scr[pl.ds(PAD - (CONV_W - 1), TL), :] * cw_ref[0:1, :]
    for i in range(1, CONV_W):
        xc = xc + xe_scr[pl.ds(PAD - (CONV_W - 1) + i, TL), :] * cw_ref[i:i + 1, :]
    tail = xe_scr[TL:TL + PAD, :]
    ctail_ref[0] = tail
    xe_scr[0:PAD, :] = tail

    gates = _dot(xc.astype(BF16), gw_ref[...]) + gb_ref[...]
    rg = _sigmoid(gates[:, 0:DC])
    ig = _sigmoid(gates[:, DC:2 * DC])
    log_a = -LRU_C * rg * _softplus(-lam_ref[...])
    a = jnp.exp(log_a)
    mult = jnp.sqrt(1.0 - a * a)
    row = lax.broadcasted_iota(jnp.int32, (TL, DC), 0)
    mult = jnp.where(row + (l * TL + pos0) == 0, 1.0, mult)
    b = mult * ig * xc
    n_slab = DC // LANES
    seg = TL // N_SEG
    pitch = _seg_pitch(seg)
    for s in range(n_slab):
        for j in range(N_SEG):
            a_scr[s, pl.ds(j * pitch, seg), :] = a[j * seg:(j + 1) * seg, s * LANES:(s + 1) * LANES]
            b_scr[s, pl.ds(j * pitch, seg), :] = b[j * seg:(j + 1) * seg, s * LANES:(s + 1) * LANES]

    def step(i, carry):
        idx = pl.ds(i, N_SEG, stride=pitch) if pitch > 1 else pl.ds(0, N_SEG)
        out = []
        for s in range(n_slab):
            hloc, ap = carry[s]
            ai = a_scr[s, idx, :]
            hloc = ai * hloc + b_scr[s, idx, :]
            ap = ap * ai
            b_scr[s, idx, :] = hloc
            a_scr[s, idx, :] = ap
            out.append((hloc, ap))
        return tuple(out)

    lax.fori_loop(0, seg, step,
                  tuple((jnp.zeros((N_SEG, LANES), F32), jnp.ones((N_SEG, LANES), F32)) for _ in range(n_slab)),
                  unroll=min(seg, 8))

    carry = h_scr[...]
    g_act = _gelu(gy_ref[...])
    for j in range(N_SEG):
        rows = slice(j * seg, (j + 1) * seg)
        rows_p = pl.ds(j * pitch, seg)
        hloc = jnp.concatenate([b_scr[s, rows_p, :] for s in range(n_slab)], axis=-1)
        ap = jnp.concatenate([a_scr[s, rows_p, :] for s in range(n_slab)], axis=-1)
        hj = hloc + ap * carry
        yc_ref[rows, :] = g_act[rows, :] * hj
        carry = hj[seg - 1:seg, :]
    h_scr[...] = carry

    @pl.when(l == n_tiles - 1)
    def _():
        hl_ref[0] = carry


def rglru_mix(xb, gy, n_batch, seq, conv_prev8, h0, pos0, wts):
    t, dc = xb.shape
    TL = 512 if seq % 512 == 0 else seq
    n_tiles = seq // TL
    scan_rows = N_SEG * _seg_pitch(TL // N_SEG)

    def full(shape):
        nd = len(shape)
        return pl.BlockSpec(shape, lambda b, l: (0,) * nd)

    yc, ctail, hl = pl.pallas_call(
        functools.partial(_rglru_kernel, TL=TL, DC=dc, pos0=pos0, n_tiles=n_tiles),
        out_shape=[jax.ShapeDtypeStruct((t, dc), F32), jax.ShapeDtypeStruct((n_batch, 8, dc), F32),
                   jax.ShapeDtypeStruct((n_batch, 1, dc), F32)],
        grid=(n_batch, n_tiles),
        in_specs=[pl.BlockSpec((TL, dc), lambda b, l: (b * n_tiles + l, 0)),
                  pl.BlockSpec((TL, dc), lambda b, l: (b * n_tiles + l, 0)),
                  pl.BlockSpec((1, 8, dc), lambda b, l: (b, 0, 0)),
                  pl.BlockSpec((1, 1, dc), lambda b, l: (b, 0, 0)),
                  full((CONV_W, dc)), full((1, dc)), full((dc, 2 * dc)), full((1, 2 * dc)), full((1, dc))],
        out_specs=[pl.BlockSpec((TL, dc), lambda b, l: (b * n_tiles + l, 0)),
                   pl.BlockSpec((1, 8, dc), lambda b, l: (b, 0, 0)),
                   pl.BlockSpec((1, 1, dc), lambda b, l: (b, 0, 0))],
        scratch_shapes=[pltpu.VMEM((TL + 8, dc), F32), pltpu.VMEM((dc // LANES, scan_rows, LANES), F32),
                        pltpu.VMEM((dc // LANES, scan_rows, LANES), F32), pltpu.VMEM((1, dc), F32)],
        compiler_params=_cparams(("parallel", "arbitrary")),
        name="rglru_mix",
    )(xb, gy, conv_prev8, h0.reshape(n_batch, 1, dc), wts["conv_w"], wts["conv_b"], wts["gate_w"], wts["gate_b"],
      wts["lam"])
    return yc, ctail[:, 8 - (CONV_W - 1):, :], hl.reshape(n_batch, dc)


def _t5_bucket(dist):
    dist = np.asarray(dist)
    max_exact = N_BUCKETS // 2
    scaled = np.log(np.maximum(dist, 1) / max_exact) / math.log(BUCKET_MAX_DIST / max_exact)
    large = np.minimum(max_exact + (scaled * (N_BUCKETS - max_exact)).astype(np.int32), N_BUCKETS - 1)
    return np.where(dist < max_exact, dist, large).astype(np.int32)


def _dist_table(rel_bias, max_dist):
    dist = np.arange(max_dist + 1)
    count = np.zeros(max_dist + 1, np.float32)
    for window, dil in DILATED:
        count += ((dist % dil == 0) & (dist <= window)).astype(np.float32)
    logcnt = np.where(count > 0, np.log(np.maximum(count, 1.0)), 0.0).astype(np.float32)
    tab = jnp.take(rel_bias, jnp.asarray(_t5_bucket(dist)), axis=0) + jnp.asarray(logcnt)[:, None]
    return jnp.where(jnp.asarray(count > 0)[:, None], tab, NEG_BIG)


def _toeplitz_tiles(tab, n_pos, n_neg, T):
    D, H = tab.shape
    span = T * n_pos
    assert D >= span
    n_col = span + T * n_neg + T - 1
    ext = jnp.concatenate([jnp.flip(tab[:span], axis=0), jnp.full((n_col + 1 - span, H), NEG_BIG, F32)], axis=0)
    ext = jnp.transpose(ext)
    skew = jnp.tile(ext, (1, T))[:, :T * n_col].reshape(H, T, n_col)
    tiles = [skew[:, :, span - 1 - T * dd: span - 1 - T * dd + T] for dd in range(-n_neg, n_pos)]
    return jnp.stack(tiles, axis=1)


def _attn_prompt_kernel(q_ref, k_ref, v_ref, bias_ref, o_ref, kb_scr, vb_scr, *, E, SUB, NS):
    qi = pl.program_id(2)
    TQ = NS * SUB

    @pl.when(qi == 0)
    def _():
        kb_scr[...] = k_ref[0].astype(BF16)
        vb_scr[...] = v_ref[0].astype(BF16)

    lane = lax.broadcasted_iota(jnp.int32, (SUB, 2 * E), 1)
    q2 = []
    for rs in range(NS):
        q = q_ref[0, rs * SUB:(rs + 1) * SUB, :] * (E ** -0.5 * LOG2E)
        q2.append(jnp.concatenate([jnp.where(lane < E, q, 0.0), jnp.where(lane >= E, q, 0.0)], axis=0).astype(BF16))

    def block(i, carry, diagonal):
        j = qi - i
        koff = pl.multiple_of(j * TQ, TQ)
        out = []
        for rs in range(NS):
            n_cs = rs + 1 if diagonal else NS
            kj = kb_scr[:, pl.ds(koff, n_cs * SUB)]
            vj = vb_scr[:, pl.ds(koff, n_cs * SUB)]
            m, l, acc = carry[rs]
            s = _dot(q2[rs], kj)
            parts = []
            for cs in range(n_cs):
                dd = i * NS + (rs - cs + NS - 1)
                bias = jnp.concatenate([bias_ref[0, dd], bias_ref[1, dd]], axis=0)
                parts.append(s[:, cs * SUB:(cs + 1) * SUB] + bias)
            mx = parts[0]
            for part in parts[1:]:
                mx = jnp.maximum(mx, part)
            m_new = jnp.maximum(m, jnp.max(mx, axis=-1, keepdims=True))
            alpha = jnp.exp2(m - m_new)
            ps = [jnp.exp2(part - m_new) for part in parts]
            psum = ps[0]
            for pexp in ps[1:]:
                psum = psum + pexp
            l = alpha * l + psum
            acc = alpha * acc + _dot_nt(jnp.concatenate(ps, axis=-1).astype(BF16), vj)
            out.append((m_new, l, acc))
        return tuple(out)

    init = tuple((jnp.full((2 * SUB, SUB), NEG_BIG, F32), jnp.zeros((2 * SUB, SUB), F32),
                  jnp.zeros((2 * SUB, 2 * E), F32)) for _ in range(NS))
    first = block(0, init, True)
    res = lax.fori_loop(1, qi + 1, lambda i, carry: block(i, carry, False), first)
    for rs in range(NS):
        m, l, acc = res[rs]
        o = acc / jnp.sum(l, axis=-1, keepdims=True)
        o_ref[0, rs * SUB:(rs + 1) * SUB, :] = jnp.where(lane < E, o[0:SUB], o[SUB:2 * SUB])


def attn_prompt(q, k, v, bias_tiles, n_batch, seq):
    hd = q.shape[-1]
    E = hd // H_D
    SUB = ATT_TILE
    NS = ATT_SUBTILES
    TQ = SUB * NS
    nq = seq // TQ
    nt = bias_tiles.shape[1]
    return pl.pallas_call(
        functools.partial(_attn_prompt_kernel, E=E, SUB=SUB, NS=NS),
        out_shape=jax.ShapeDtypeStruct((n_batch, seq, hd), F32),
        grid=(H_D // 2, n_batch, nq),
        in_specs=[pl.BlockSpec((1, TQ, 2 * E), lambda hp, b, i: (b, i, hp)),
                  pl.BlockSpec((1, 2 * E, seq), lambda hp, b, i: (b, hp, 0)),
                  pl.BlockSpec((1, 2 * E, seq), lambda hp, b, i: (b, hp, 0)),
                  pl.BlockSpec((2, nt, SUB, SUB), lambda hp, b, i: (hp, 0, 0, 0))],
        out_specs=pl.BlockSpec((1, TQ, 2 * E), lambda hp, b, i: (b, i, hp)),
        scratch_shapes=[pltpu.VMEM((2 * E, seq), BF16), pltpu.VMEM((2 * E, seq), BF16)],
        compiler_params=_cparams(("arbitrary", "arbitrary", "arbitrary")),
        name="dilated_attn_prompt",
    )(q, k, v, bias_tiles)


def _attn_sample_kernel(q_ref, kn_ref, vn_ref, ck_ref, cv_ref, bo_ref, bn_ref, o_ref, *, E, S):
    lane = lax.broadcasted_iota(jnp.int32, (S, 2 * E), 1)
    NPAD = bn_ref.shape[-1]
    outs = []
    for hp in range(H_D // 2):
        sl = slice(hp * 2 * E, (hp + 1) * 2 * E)
        q = q_ref[0, :, sl] * (E ** -0.5)
        q2 = jnp.concatenate([jnp.where(lane < E, q, 0.0), jnp.where(lane >= E, q, 0.0)], axis=0).astype(BF16)
        zpad = jnp.zeros((NPAD - S, 2 * E), F32)
        kn = jnp.concatenate([kn_ref[0, :, sl], zpad], axis=0).astype(BF16)
        vn = jnp.concatenate([vn_ref[0, :, sl], zpad], axis=0).astype(BF16)
        s_old = _dot(q2, ck_ref[0, sl, :].astype(BF16)) + jnp.concatenate([bo_ref[2 * hp], bo_ref[2 * hp + 1]], axis=0)
        s_new = _dot_nt(q2, kn) + jnp.concatenate([bn_ref[2 * hp], bn_ref[2 * hp + 1]], axis=0)
        m = jnp.maximum(jnp.max(s_old, axis=-1, keepdims=True), jnp.max(s_new, axis=-1, keepdims=True))
        p_old = jnp.exp(s_old - m)
        p_new = jnp.exp(s_new - m)
        l = jnp.sum(p_old, axis=-1, keepdims=True) + jnp.sum(p_new, axis=-1, keepdims=True)
        acc = _dot_nt(p_old.astype(BF16), cv_ref[0, sl, :].astype(BF16)) + _dot(p_new.astype(BF16), vn)
        o = acc / l
        outs.append(jnp.where(lane < E, o[0:S], o[S:2 * S]))
    o_ref[0] = jnp.concatenate(outs, axis=-1)


def attn_sample(q, k_new, v_new, cache_k, cache_v, bias_old, bias_new):
    n_batch, S, hd = q.shape
    W = cache_k.shape[2]
    E = hd // H_D
    NPAD = bias_new.shape[-1]
    return pl.pallas_call(
        functools.partial(_attn_sample_kernel, E=E, S=S),
        out_shape=jax.ShapeDtypeStruct((n_batch, S, hd), F32),
        grid=(n_batch,),
        in_specs=[pl.BlockSpec((1, S, hd), lambda b: (b, 0, 0)),
                  pl.BlockSpec((1, S, hd), lambda b: (b, 0, 0)),
                  pl.BlockSpec((1, S, hd), lambda b: (b, 0, 0)),
                  pl.BlockSpec((1, hd, W), lambda b: (b, 0, 0)),
                  pl.BlockSpec((1, hd, W), lambda b: (b, 0, 0)),
                  pl.BlockSpec((H_D, S, W), lambda b: (0, 0, 0)),
                  pl.BlockSpec((H_D, S, NPAD), lambda b: (0, 0, 0))],
        out_specs=pl.BlockSpec((1, S, hd), lambda b: (b, 0, 0)),
        compiler_params=_cparams(("parallel",)),
        name="dilated_attn_sample",
    )(q, k_new, v_new, cache_k, cache_v, bias_old, bias_new)


def _even_weights(j, w_in_even, w_out_even, shift_mu, decay_w0, decay_up, iclr_a0, iclr_up, gate_up, key_k, key_a,
                  bonus_r_k, lnx_g, lnx_b, sgu_norm_g, sgu_norm_b, sgu_w, sgu_b):
    da = decay_w0.shape[1]
    zeros_d = jnp.zeros((R_ICLR, da), F32)
    zeros_i = jnp.zeros((R_DECAY, da), F32)
    return dict(
        w_in=w_in_even[j].astype(BF16),
        w_out_a=w_out_even[j, :da].astype(BF16), w_out_b=w_out_even[j, da:].astype(BF16),
        mu=shift_mu[j].reshape(1, -1), w0=decay_w0[j].reshape(1, -1), a0=iclr_a0[j].reshape(1, -1),
        wd=jnp.concatenate([decay_up[j], zeros_d], axis=0).astype(BF16),
        wa=jnp.concatenate([zeros_i, iclr_up[j]], axis=0).astype(BF16), wg=gate_up[j].astype(BF16), key_k=key_k[j].reshape(1, -1), key_a=key_a[j].reshape(1, -1),
        bonus=bonus_r_k[j].reshape(1, -1), lnx_g=lnx_g[j].reshape(1, -1), lnx_b=lnx_b[j].reshape(1, -1),
        ng=sgu_norm_g[j].reshape(1, -1), nb=sgu_norm_b[j].reshape(1, -1), sgu_w=sgu_w[j], sgu_b=sgu_b[j])


def _gmlp_tables(sgu_w, sgu_b, chunk):
    reps = GMLP_TILE // chunk
    n_h = sgu_w.shape[0]
    cb = None
    wm = sgu_w[:, :chunk, :chunk] * jnp.asarray(np.tril(np.ones((chunk, chunk), np.float32)))
    if reps > 1:
        eye = jnp.asarray(np.eye(reps, dtype=np.float32))
        wm = jnp.einsum("ab,hts->hatbs", eye, wm).reshape(n_h, GMLP_TILE, GMLP_TILE)
    bias = jnp.tile(jnp.transpose(sgu_b[:, :chunk]), (reps, 1))
    return wm.astype(BF16), bias


def _even_layer(x, n_batch, seq, chunk, n_par, want_vn, shift_prev, wkv0, norm_g, ew):
    pa, pu, pv = norm_matmul(x, norm_g, ew["w_in"], (ew["mu"].shape[1], ew["ng"].shape[1], ew["ng"].shape[1]))
    ya, wkv = rwkv_mix(pa, n_batch, seq, shift_prev, wkv0, ew, n_par)
    wm, bias = _gmlp_tables(ew["sgu_w"], ew["sgu_b"], chunk)
    cb = pu.shape[1] // wm.shape[0]
    bias_tile = jnp.repeat(bias, cb, axis=1)
    yb, vn = gmlp_mix(pu, pv, ew["ng"], ew["nb"], wm, bias_tile, want_vn)
    x = proj_residual(x, ya, yb, ew["w_out_a"], ew["w_out_b"])
    last = pa.reshape(n_batch, seq, -1)[:, -1]
    return x, last, wkv, vn


def _odd_weights(j, w_in_odd, w_out_odd, conv_w, conv_b, rgate_w, rgate_b, igate_w, igate_b, lru_lambda):
    dc = conv_b.shape[1]
    eye = jnp.asarray(np.eye(H_C, dtype=np.float32))

    def blockdiag(w):
        dh = w.shape[-1]
        return jnp.einsum("ab,aij->aibj", eye, w).reshape(H_C * dh, H_C * dh)

    return dict(
        w_in=w_in_odd[j].astype(BF16), w_kv_t=jnp.transpose(w_in_odd[j, :, 3 * dc:]).astype(BF16),
        w_out_c=w_out_odd[j, :dc].astype(BF16), w_out_d=w_out_odd[j, dc:].astype(BF16),
        conv_w=conv_w[j], conv_b=conv_b[j].reshape(1, -1),
        gate_w=jnp.concatenate([blockdiag(rgate_w[j]), blockdiag(igate_w[j])], axis=1).astype(BF16),
        gate_b=jnp.concatenate([rgate_b[j], igate_b[j]]).reshape(1, -1),
        lam=lru_lambda[j].reshape(1, -1))


def _odd_layer(x, n_batch, seq, conv_prev, h0, pos0, caches, dist_tab, norm_g, ow):
    dc = ow["lam"].shape[1]
    conv_prev8 = jnp.pad(conv_prev, ((0, 0), (8 - (CONV_W - 1), 0), (0, 0)))
    if caches is None:
        gy, xb, q, k_t, v_t = norm_matmul(x, norm_g, ow["w_in"][:, :3 * dc], (dc,) * 3, ow["w_kv_t"], (dc, dc), seq)
        hd = q.shape[1]
        e = hd // H_D
        yc, conv_last, h_last = rglru_mix(xb, gy, n_batch, seq, conv_prev8, h0, pos0, ow)
        tiles = _toeplitz_tiles(dist_tab * LOG2E, seq // ATT_TILE, ATT_SUBTILES - 1, ATT_TILE)
        o = attn_prompt(q.reshape(n_batch, seq, hd), k_t, v_t, tiles, n_batch, seq)
        k_rows = jnp.transpose(k_t.reshape(n_batch, H_D, e, seq), (0, 3, 1, 2))
        v_rows = jnp.transpose(v_t.reshape(n_batch, H_D, e, seq), (0, 3, 1, 2))
    else:
        gy, xb, q, k, v = norm_matmul(x, norm_g, ow["w_in"], (dc,) * 5)
        hd = q.shape[1]
        e = hd // H_D
        yc, conv_last, h_last = rglru_mix(xb, gy, n_batch, seq, conv_prev8, h0, pos0, ow)
        q3, k3, v3 = (a.reshape(n_batch, seq, hd) for a in (q, k, v))
        cache_k, cache_v = caches
        W = cache_k.shape[1]
        NPAD = 128
        tab_t = jnp.flip(jnp.transpose(dist_tab[:W + seq]), axis=1)
        b_old = jnp.stack([tab_t[:, seq - 1 - j:seq - 1 - j + W] for j in range(seq)], axis=1)
        d_new = np.arange(seq)[:, None] - np.arange(NPAD)[None, :]
        ok_new = (d_new >= 0) & (np.arange(NPAD)[None, :] < seq)
        b_new = jnp.take(dist_tab, jnp.asarray(np.maximum(d_new, 0)), axis=0)
        b_new = jnp.transpose(jnp.where(jnp.asarray(ok_new)[..., None], b_new, NEG_BIG), (2, 0, 1))
        ck = jnp.transpose(cache_k, (0, 2, 3, 1)).reshape(n_batch, hd, W)
        cv = jnp.transpose(cache_v, (0, 2, 3, 1)).reshape(n_batch, hd, W)
        o = attn_sample(q3, k3, v3, ck, cv, b_old, b_new)
        k_rows = k3.reshape(n_batch, seq, H_D, e)
        v_rows = v3.reshape(n_batch, seq, H_D, e)
    x = proj_residual(x, yc, o.reshape(n_batch * seq, hd), ow["w_out_c"], ow["w_out_d"])
    return x, conv_last, h_last, k_rows, v_rows


def _moe_weights(l, router_group_w, router_group_b, router_expert_w, router_expert_b, exp_w_gate, exp_w_up,
                 exp_w_down):
    d = router_group_w.shape[1]
    n_used = N_GROUPS + router_expert_w.shape[2]
    rw = jnp.concatenate([router_group_w[l], router_expert_w[l], jnp.zeros((d, ROUTER_LANES - n_used), F32)], axis=1)
    rb = jnp.concatenate([router_group_b[l], router_expert_b[l], jnp.zeros((ROUTER_LANES - n_used,), F32)])
    return dict(rw=rw.astype(BF16), rb=rb.reshape(1, -1), wg=exp_w_gate[l].astype(BF16), wu=exp_w_up[l].astype(BF16),
                wd=exp_w_down[l].astype(BF16))


def kernel(x_prompt, x_sample, state_wkv, state_shift, state_conv, state_rglru, cache_k, cache_v, norm_mix, norm_ffn, norm_final, w_in_even, w_out_even, shift_mu, decay_w0, decay_up, iclr_a0, iclr_up, gate_up, key_k, key_a, bonus_r_k, lnx_g, lnx_b, sgu_norm_g, sgu_norm_b, sgu_w, sgu_b, w_in_odd, w_out_odd, conv_w, conv_b, rgate_w, rgate_b, igate_w, igate_b, lru_lambda, rel_bias, router_group_w, router_group_b, router_expert_w, router_expert_b, exp_w_gate, exp_w_up, exp_w_down):
    B, L, D = x_prompt.shape
    DB, S, _ = x_sample.shape
    depth = norm_mix.shape[0]
    xp = x_prompt.reshape(B * L, D)
    xs = x_sample.reshape(DB * S, D)
    W = cache_k.shape[2]
    dist_tab = _dist_table(rel_bias, max(L, W + S) - 1)

    sorted_buf = None
    wkv_p, shift_p, conv_p, lru_p, k_p, v_p = [], [], [], [], [], []
    wkv_s, shift_s, chunkv_s, conv_s, lru_s, k_s, v_s = [], [], [], [], [], [], []
    for l in range(depth):
        j = l // 2
        if l % 2 == 0:
            ew = _even_weights(j, w_in_even, w_out_even, shift_mu, decay_w0, decay_up, iclr_a0, iclr_up, gate_up,
                               key_k, key_a, bonus_r_k, lnx_g, lnx_b, sgu_norm_g, sgu_norm_b, sgu_w, sgu_b)
            a_proj = ew["mu"].shape[1]
            h_a = state_wkv.shape[2]
            xp, sh, wkv, _ = _even_layer(xp, B, L, GMLP_TILE, RWKV_PAR_PROMPT, False, jnp.zeros((B, a_proj), F32),
                                         jnp.zeros((B, h_a, DH_A, DH_A), F32), norm_mix[l], ew)
            xs, sh_s, wkv_s_new, vn_s = _even_layer(xs, DB, S, S, RWKV_PAR_SAMPLE, True, state_shift[j], state_wkv[j], norm_mix[l], ew)
            wkv_p.append(wkv)
            shift_p.append(sh)
            wkv_s.append(wkv_s_new)
            shift_s.append(sh_s)
            chunkv_s.append(vn_s.reshape(DB, S, -1))
        else:
            ow = _odd_weights(j, w_in_odd, w_out_odd, conv_w, conv_b, rgate_w, rgate_b, igate_w, igate_b, lru_lambda)
            dc = ow["lam"].shape[1]
            xp, cv, hl, kr, vr = _odd_layer(xp, B, L, jnp.zeros((B, CONV_W - 1, dc), F32), jnp.zeros((B, dc), F32),
                                            0, None, dist_tab, norm_mix[l], ow)
            xs, cv_s, hl_s, kr_s, vr_s = _odd_layer(xs, DB, S, state_conv[j], state_rglru[j], PAST_LEN,
                                                    (cache_k[j], cache_v[j]), dist_tab, norm_mix[l], ow)
            conv_p.append(cv)
            lru_p.append(hl)
            k_p.append(kr)
            v_p.append(vr)
            conv_s.append(cv_s)
            lru_s.append(hl_s)
            k_s.append(kr_s)
            v_s.append(vr_s)
        mw = _moe_weights(l, router_group_w, router_group_b, router_expert_w, router_expert_b, exp_w_gate, exp_w_up,
                          exp_w_down)
        xp, sorted_buf = moe_layer_sparse(xp, norm_ffn[l], mw["rw"], mw["rb"], mw["wg"], mw["wu"], mw["wd"],
                                          final_g=norm_final if l == depth - 1 else None, sorted_buf=sorted_buf)
        xs = moe_layer(xs, norm_ffn[l], mw["rw"], mw["rb"], mw["wg"], mw["wu"], mw["wd"])
    y_prompt = xp.reshape(B, L, D)
    y_sample = rmsnorm_call(xs, norm_final).reshape(DB, S, D)
    return (y_prompt, y_sample,
            jnp.stack(wkv_p), jnp.stack(shift_p), jnp.stack(conv_p), jnp.stack(lru_p), jnp.stack(k_p), jnp.stack(v_p),
            jnp.stack(wkv_s), jnp.stack(shift_s), jnp.stack(chunkv_s), jnp.stack(conv_s), jnp.stack(lru_s),
            jnp.stack(k_s), jnp.stack(v_s))
```

```python
import functools
import math

import numpy as np
import jax
import jax.numpy as jnp
from jax import lax
from jax.experimental import pallas as pl
from jax.experimental.pallas import tpu as pltpu

F32 = jnp.float32
BF16 = jnp.bfloat16
HI = lax.Precision.HIGHEST

PAST_LEN = 8192
DH_A = 64
R_DECAY = 64
R_ICLR = 64
R_GATE = 128
GN_EPS = 64e-5
H_B = 4
H_C = 8
CONV_W = 4
LRU_C = 8.0
H_D = 8
DILATED = ((128, 1), (512, 4), (2048, 16))
N_BUCKETS = 32
BUCKET_MAX_DIST = 2048
NEG_BIG = -1e30
N_GROUPS = 4
EXP_PER_GROUP = 4
NORM_EPS = 1e-6
LOG2E = math.log2(math.e)

VMEM_LIMIT = 56 * 1024 * 1024
RWKV_CHUNK = 64
RWKV_PAR_PROMPT = 4
RWKV_PAR_SAMPLE = 8
ATT_TILE = 128
ATT_SUBTILES = 4
LANES = 128
SUBLANES = 8


def _cparams(sem):
    return pltpu.CompilerParams(dimension_semantics=sem, vmem_limit_bytes=VMEM_LIMIT)


def _dot(a, b, precision=None):
    return jnp.dot(a, b, preferred_element_type=F32, precision=precision)


def _dot_nt(a, b, precision=None):
    return lax.dot_general(a, b, (((1,), (1,)), ((), ())), preferred_element_type=F32, precision=precision)


def _dot_tn(a, b, precision=None):
    return lax.dot_general(a, b, (((0,), (0,)), ((), ())), preferred_element_type=F32, precision=precision)


def _split_bf16(x, n):
    parts = []
    for _ in range(n):
        hi = x.astype(BF16)
        parts.append(hi)
        x = x - hi.astype(F32)
    return parts


def _mp_dot(dotfn, a, b, passes):
    if passes == 1:
        return dotfn(a.astype(BF16), b.astype(BF16))
    a_hi, a_lo = _split_bf16(a, 2)
    b_hi, b_lo = _split_bf16(b, 2)
    return dotfn(a_hi, b_hi) + (dotfn(a_hi, b_lo) + dotfn(a_lo, b_hi))


def _dot_exact_rhs(a, b_bf16, n_split):
    parts = _split_bf16(a, n_split)
    acc = _dot(parts[0], b_bf16)
    for part in parts[1:]:
        acc = acc + _dot(part, b_bf16)
    return acc


def _dot_exact_lhs(a_bf16, b, n_split):
    parts = _split_bf16(b, n_split)
    acc = _dot(a_bf16, parts[0])
    for part in parts[1:]:
        acc = acc + _dot(a_bf16, part)
    return acc


def _softplus(x):
    return jnp.maximum(x, 0.0) + jnp.log(1.0 + jnp.exp(-jnp.abs(x)))


def _sigmoid(x):
    return 1.0 / (1.0 + jnp.exp(-x))


def _gelu(x):
    c = math.sqrt(2.0 / math.pi)
    return 0.5 * x * (1.0 + jnp.tanh(c * (x + 0.044715 * (x * x * x))))


def _row_tile(t, pref=512):
    return pref if t % pref == 0 else t


def _norm_matmul_kernel(x_ref, g_ref, w_ref, *refs, splits, t_splits):
    x = x_ref[...]
    ms = jnp.mean(x * x, axis=-1, keepdims=True)
    h = (x * lax.rsqrt(ms + NORM_EPS) * g_ref[...]).astype(BF16)
    if t_splits:
        wt_ref, refs = refs[0], refs[1:]
    off = 0
    for o_ref, n in zip(refs[:len(splits)], splits):
        o_ref[...] = _dot(h, w_ref[:, off:off + n])
        off += n
    off = 0
    for o_ref, n in zip(refs[len(splits):], t_splits):
        o_ref[0] = _dot_nt(wt_ref[off:off + n, :], h)
        off += n


def norm_matmul(x, g, w_bf16, splits, wt_bf16=None, t_splits=(), seq=None):
    t, d = x.shape
    n = w_bf16.shape[1]
    tm = _row_tile(t)
    in_specs = [pl.BlockSpec((tm, d), lambda i: (i, 0)),
                pl.BlockSpec((1, d), lambda i: (0, 0)),
                pl.BlockSpec((d, n), lambda i: (0, 0))]
    args = [x, g.reshape(1, d), w_bf16]
    out_shape = [jax.ShapeDtypeStruct((t, s), F32) for s in splits]
    out_specs = [pl.BlockSpec((tm, s), lambda i: (i, 0)) for s in splits]
    if t_splits:
        tiles = seq // tm
        in_specs.append(pl.BlockSpec(wt_bf16.shape, lambda i: (0, 0)))
        args.append(wt_bf16)
        out_shape += [jax.ShapeDtypeStruct((t // seq, s, seq), F32) for s in t_splits]
        out_specs += [pl.BlockSpec((1, s, tm), lambda i: (i // tiles, 0, i % tiles)) for s in t_splits]
    return pl.pallas_call(
        functools.partial(_norm_matmul_kernel, splits=splits, t_splits=tuple(t_splits)),
        out_shape=out_shape,
        grid=(t // tm,),
        in_specs=in_specs,
        out_specs=out_specs,
        compiler_params=_cparams(("parallel",)),
        name="norm_matmul",
    )(*args)


def _proj_res_kernel(x_ref, a_ref, b_ref, wa_ref, wb_ref, o_ref):
    acc = _dot(a_ref[...].astype(BF16), wa_ref[...]) + _dot(b_ref[...].astype(BF16), wb_ref[...])
    o_ref[...] = x_ref[...] + acc


def proj_residual(x, a, b, wa, wb):
    t, d = x.shape
    tm = _row_tile(t)
    ka, kb = a.shape[1], b.shape[1]
    return pl.pallas_call(
        _proj_res_kernel,
        out_shape=jax.ShapeDtypeStruct((t, d), F32),
        grid=(t // tm,),
        in_specs=[pl.BlockSpec((tm, d), lambda i: (i, 0)),
                  pl.BlockSpec((tm, ka), lambda i: (i, 0)),
                  pl.BlockSpec((tm, kb), lambda i: (i, 0)),
                  pl.BlockSpec((ka, d), lambda i: (0, 0)),
                  pl.BlockSpec((kb, d), lambda i: (0, 0))],
        out_specs=pl.BlockSpec((tm, d), lambda i: (i, 0)),
        compiler_params=_cparams(("parallel",)),
        name="proj_residual",
    )(x, a, b, wa, wb)


def _rmsnorm_kernel(x_ref, g_ref, o_ref):
    x = x_ref[...]
    ms = jnp.mean(x * x, axis=-1, keepdims=True)
    o_ref[...] = x * lax.rsqrt(ms + NORM_EPS) * g_ref[...]


def rmsnorm_call(x, g):
    t, d = x.shape
    tm = _row_tile(t)
    return pl.pallas_call(
        _rmsnorm_kernel,
        out_shape=jax.ShapeDtypeStruct((t, d), F32),
        grid=(t // tm,),
        in_specs=[pl.BlockSpec((tm, d), lambda i: (i, 0)), pl.BlockSpec((1, d), lambda i: (0, 0))],
        out_specs=pl.BlockSpec((tm, d), lambda i: (i, 0)),
        compiler_params=_cparams(("parallel",)),
        name="final_rmsnorm",
    )(x, g.reshape(1, d))


ROUTER_LANES = 128


def _route(xn, rw, rb, lane, n_exp):
    logits = _dot_exact_rhs(xn, rw, 2) + rb
    lg = jnp.where(lane < N_GROUPS, logits, -jnp.inf)
    gm = jnp.max(lg, axis=-1, keepdims=True)
    top_pg = 1.0 / jnp.sum(jnp.exp(lg - gm), axis=-1, keepdims=True)
    grp = jnp.min(jnp.where(lg == gm, lane, ROUTER_LANES), axis=-1, keepdims=True)
    in_grp = (lane >= N_GROUPS) & (lane < N_GROUPS + n_exp) & (((lane - N_GROUPS) // EXP_PER_GROUP) == grp)
    le = jnp.where(in_grp, logits, -jnp.inf)
    t1 = jnp.max(le, axis=-1, keepdims=True)
    i1 = jnp.min(jnp.where(le == t1, lane, ROUTER_LANES), axis=-1, keepdims=True)
    le2 = jnp.where(lane == i1, -jnp.inf, le)
    t2 = jnp.max(le2, axis=-1, keepdims=True)
    i2 = jnp.min(jnp.where(le2 == t2, lane, ROUTER_LANES), axis=-1, keepdims=True)
    ex = jnp.exp(t2 - t1)
    w1 = 1.0 / (1.0 + ex)
    return i1, i2, w1 * top_pg, (ex * w1) * top_pg


def _moe_kernel(x_ref, g_ref, rw_ref, rb_ref, wg_ref, wu_ref, wd_ref, o_ref, xn_scr, gate_scr, acc_scr, *, n_exp):
    e = pl.program_id(1)
    tm = x_ref.shape[0]
    lane = lax.broadcasted_iota(jnp.int32, (tm, ROUTER_LANES), 1)

    @pl.when(e == 0)
    def _():
        x = x_ref[...]
        ms = jnp.mean(x * x, axis=-1, keepdims=True)
        xn = x * lax.rsqrt(ms + NORM_EPS) * g_ref[...]
        xn_scr[...] = xn.astype(BF16)
        i1, i2, g1, g2 = _route(xn, rw_ref[...], rb_ref[...], lane, n_exp)
        gate_scr[...] = jnp.where(lane == i1, g1, 0.0) + jnp.where(lane == i2, g2, 0.0)
        acc_scr[...] = jnp.zeros_like(acc_scr)

    xn = xn_scr[...]
    hg = _dot(xn, wg_ref[0, 0])
    hu = _dot(xn, wu_ref[0, 0])
    gcol = jnp.sum(jnp.where(lane == e + N_GROUPS, gate_scr[...], 0.0), axis=-1, keepdims=True)
    hid = hg * _sigmoid(hg) * hu * gcol
    acc_scr[...] += _dot(hid.astype(BF16), wd_ref[0, 0])

    @pl.when(e == n_exp - 1)
    def _():
        o_ref[...] = x_ref[...] + acc_scr[...]


def moe_layer(x, g, rw, rb, wg, wu, wd, layer):
    t, d = x.shape
    _, n_exp, _, f = wg.shape
    tm = _row_tile(t)
    return pl.pallas_call(
        functools.partial(_moe_kernel, n_exp=n_exp),
        out_shape=jax.ShapeDtypeStruct((t, d), F32),
        grid=(t // tm, n_exp),
        in_specs=[pl.BlockSpec((tm, d), lambda i, e: (i, 0)),
                  pl.BlockSpec((1, d), lambda i, e: (0, 0)),
                  pl.BlockSpec((d, ROUTER_LANES), lambda i, e: (0, 0)),
                  pl.BlockSpec((1, ROUTER_LANES), lambda i, e: (0, 0)),
                  pl.BlockSpec((1, 1, d, f), lambda i, e: (layer, e, 0, 0)),
                  pl.BlockSpec((1, 1, d, f), lambda i, e: (layer, e, 0, 0)),
                  pl.BlockSpec((1, 1, f, d), lambda i, e: (layer, e, 0, 0))],
        out_specs=pl.BlockSpec((tm, d), lambda i, e: (i, 0)),
        scratch_shapes=[pltpu.VMEM((tm, d), BF16), pltpu.VMEM((tm, ROUTER_LANES), F32), pltpu.VMEM((tm, d), F32)],
        compiler_params=_cparams(("parallel", "arbitrary")),
        name="hier_moe",
    )(x, g.reshape(1, d), rw, rb, wg, wu, wd)


MOE_ROW_TILE = 512
MOE_COPY_CHUNK = 256
MOE_COMBINE_TILE = 256


def _router_kernel(x_ref, g_ref, rw_ref, rb_ref, tri_ref, gate_ref, info_ref, cnt_ref, base_scr, *, n_exp, n_tiles):
    i = pl.program_id(0)
    tm = x_ref.shape[0]
    lane = lax.broadcasted_iota(jnp.int32, (tm, ROUTER_LANES), 1)

    @pl.when(i == 0)
    def _():
        base_scr[...] = jnp.zeros_like(base_scr)

    x = x_ref[...]
    ms = jnp.mean(x * x, axis=-1, keepdims=True)
    xn = x * lax.rsqrt(ms + NORM_EPS) * g_ref[...]
    i1, i2, g1, g2 = _route(xn, rw_ref[...], rb_ref[...], lane, n_exp)
    chosen = jnp.where((lane == i1) | (lane == i2), 1.0, 0.0)
    before = _dot(tri_ref[...], chosen.astype(BF16)) + base_scr[...]
    r1 = jnp.sum(jnp.where(lane == i1, before, 0.0), axis=-1, keepdims=True)
    r2 = jnp.sum(jnp.where(lane == i2, before, 0.0), axis=-1, keepdims=True)
    base_scr[...] += jnp.sum(chosen, axis=0, keepdims=True)
    gate_ref[...] = jnp.where(lane == 0, g1, 0.0) + jnp.where(lane == 1, g2, 0.0)
    e1 = (i1 - N_GROUPS).astype(F32)
    e2 = (i2 - N_GROUPS).astype(F32)
    info_ref[...] = (jnp.where(lane == 0, e1, 0.0) + jnp.where(lane == 1, e2, 0.0)
                     + jnp.where(lane == 2, r1, 0.0) + jnp.where(lane == 3, r2, 0.0))

    @pl.when(i == n_tiles - 1)
    def _():
        cnt_ref[...] = base_scr[...]


def _scatter_rows_kernel(pos0_ref, pos1_ref, x_ref, xs_in_hbm, xs_hbm, stage, sem, *, CH, n_chunks):
    del xs_in_hbm
    c = pl.program_id(0)
    slot = c % 2
    x = x_ref[...]
    for j in range(SUBLANES):
        stage[slot, pl.ds(j, CH, stride=SUBLANES), :] = x[:, j * LANES:(j + 1) * LANES]

    def body(r, carry):
        t = c * CH + r
        src = stage.at[slot, pl.ds(pl.multiple_of(r * SUBLANES, SUBLANES), SUBLANES), :]
        d0 = pl.multiple_of(pos0_ref[t], SUBLANES)
        d1 = pl.multiple_of(pos1_ref[t], SUBLANES)
        pltpu.make_async_copy(src, xs_hbm.at[pl.ds(d0, SUBLANES), :], sem.at[slot]).start(priority=0)
        pltpu.make_async_copy(src, xs_hbm.at[pl.ds(d1, SUBLANES), :], sem.at[slot]).start(priority=1)
        return carry

    lax.fori_loop(0, CH, body, 0, unroll=8)

    def drain(s):
        pltpu.make_async_copy(stage.at[s], xs_hbm.at[pl.ds(0, CH * SUBLANES), :], sem.at[s]).wait()
        pltpu.make_async_copy(stage.at[s], xs_hbm.at[pl.ds(0, CH * SUBLANES), :], sem.at[s]).wait()

    @pl.when(c > 0)
    def _():
        drain(1 - slot)

    @pl.when(c == n_chunks - 1)
    def _():
        drain(slot)


def _tile_rows_to_matrix(ref, lead, n_rows):
    return jnp.concatenate([ref[lead + (pl.ds(j, n_rows, stride=SUBLANES), slice(None))] for j in range(SUBLANES)],
                           axis=-1)


def _expert_kernel(te_ref, nv_ref, xs_ref, g_ref, wg_ref, wu_ref, wd_ref, y_ref, *, TM):
    @pl.when(pl.program_id(0) < nv_ref[0])
    def _():
        x = _tile_rows_to_matrix(xs_ref, (), TM)
        ms = jnp.mean(x * x, axis=-1, keepdims=True)
        xn = (x * lax.rsqrt(ms + NORM_EPS) * g_ref[...]).astype(BF16)
        hg = _dot(xn, wg_ref[0, 0])
        hu = _dot(xn, wu_ref[0, 0])
        hid = hg * _sigmoid(hg) * hu
        y = _dot(hid.astype(BF16), wd_ref[0, 0])
        for j in range(SUBLANES):
            y_ref[pl.ds(j, TM, stride=SUBLANES), :] = y[:, j * LANES:(j + 1) * LANES]

    @pl.when(pl.program_id(0) >= nv_ref[0])
    def _():
        y_ref[...] = jnp.zeros_like(y_ref)


def _combine_kernel(pos0_ref, pos1_ref, x_ref, gate_ref, fg_ref, y_hbm, o_ref, ybuf, sem, *, TC, n_tiles, final_norm):
    i = pl.program_id(0)

    def issue(tile, slot):
        def body(r, carry):
            t = tile * TC + r
            dst = pl.ds(pl.multiple_of(r * SUBLANES, SUBLANES), SUBLANES)
            s0 = pl.multiple_of(pos0_ref[t], SUBLANES)
            s1 = pl.multiple_of(pos1_ref[t], SUBLANES)
            pltpu.make_async_copy(y_hbm.at[pl.ds(s0, SUBLANES), :], ybuf.at[slot, 0, dst, :],
                                  sem.at[slot]).start(priority=0)
            pltpu.make_async_copy(y_hbm.at[pl.ds(s1, SUBLANES), :], ybuf.at[slot, 1, dst, :],
                                  sem.at[slot]).start(priority=1)
            return carry
        lax.fori_loop(0, TC, body, 0, unroll=8)

    @pl.when(i == 0)
    def _():
        issue(0, 0)

    @pl.when(i + 1 < n_tiles)
    def _():
        issue(i + 1, (i + 1) % 2)

    slot = i % 2
    pltpu.make_async_copy(y_hbm.at[pl.ds(0, TC * SUBLANES), :], ybuf.at[slot, 0], sem.at[slot]).wait()
    pltpu.make_async_copy(y_hbm.at[pl.ds(0, TC * SUBLANES), :], ybuf.at[slot, 1], sem.at[slot]).wait()
    gate = gate_ref[...]
    y0 = _tile_rows_to_matrix(ybuf, (slot, 0), TC)
    y1 = _tile_rows_to_matrix(ybuf, (slot, 1), TC)
    out = x_ref[...] + gate[:, 0:1] * y0 + gate[:, 1:2] * y1
    if final_norm:
        ms = jnp.mean(out * out, axis=-1, keepdims=True)
        out = out * lax.rsqrt(ms + NORM_EPS) * fg_ref[...]
    o_ref[...] = out


def moe_layer_sparse(x, g, rw, rb, wg, wu, wd, layer, final_g=None, sorted_buf=None):
    t, d = x.shape
    _, n_exp, _, f = wg.shape
    TM = MOE_ROW_TILE
    n_tiles = t // TM
    tri = jnp.asarray(np.tril(np.ones((TM, TM), np.float32), -1)).astype(BF16)
    gate, info, cnt = pl.pallas_call(
        functools.partial(_router_kernel, n_exp=n_exp, n_tiles=n_tiles),
        out_shape=[jax.ShapeDtypeStruct((t, ROUTER_LANES), F32), jax.ShapeDtypeStruct((t, ROUTER_LANES), F32),
                   jax.ShapeDtypeStruct((1, ROUTER_LANES), F32)],
        grid=(n_tiles,),
        in_specs=[pl.BlockSpec((TM, d), lambda i: (i, 0)),
                  pl.BlockSpec((1, d), lambda i: (0, 0)),
                  pl.BlockSpec((d, ROUTER_LANES), lambda i: (0, 0)),
                  pl.BlockSpec((1, ROUTER_LANES), lambda i: (0, 0)),
                  pl.BlockSpec((TM, TM), lambda i: (0, 0))],
        out_specs=[pl.BlockSpec((TM, ROUTER_LANES), lambda i: (i, 0)),
                   pl.BlockSpec((TM, ROUTER_LANES), lambda i: (i, 0)),
                   pl.BlockSpec((1, ROUTER_LANES), lambda i: (0, 0))],
        scratch_shapes=[pltpu.VMEM((1, ROUTER_LANES), F32)],
        compiler_params=_cparams(("arbitrary",)),
        name="moe_router",
    )(x, g.reshape(1, d), rw, rb, tri)

    counts = cnt[0, N_GROUPS:N_GROUPS + n_exp].astype(jnp.int32)
    padded = ((counts + TM - 1) // TM) * TM
    ends = jnp.cumsum(padded)
    offs = ends - padded
    eid = info[:, 0:2].astype(jnp.int32)
    rank = info[:, 2:4].astype(jnp.int32)
    pos = jnp.sum(jnp.where(eid[:, :, None] == jnp.arange(n_exp)[None, None, :], offs[None, None, :], 0), axis=-1) + rank
    assert d == SUBLANES * LANES, "a token row must fill exactly one (8, 128) tile"
    pos = pos * SUBLANES
    pos0, pos1 = pos[:, 0], pos[:, 1]
    max_tiles = (2 * t) // TM + n_exp
    n_valid = (ends[-1] // TM).astype(jnp.int32).reshape(1)
    tile_exp = jnp.minimum(jnp.sum((ends[None, :] // TM) <= jnp.arange(max_tiles)[:, None], axis=-1),
                           n_exp - 1).astype(jnp.int32)
    p_rows = max_tiles * TM

    CH = MOE_COPY_CHUNK
    xs = pl.pallas_call(
        functools.partial(_scatter_rows_kernel, CH=CH, n_chunks=t // CH),
        out_shape=jax.ShapeDtypeStruct((p_rows * SUBLANES, LANES), F32),
        grid_spec=pltpu.PrefetchScalarGridSpec(
            num_scalar_prefetch=2, grid=(t // CH,),
            in_specs=[pl.BlockSpec((CH, d), lambda c, p0, p1: (c, 0)), pl.BlockSpec(memory_space=pl.ANY)],
            out_specs=pl.BlockSpec(memory_space=pl.ANY),
            scratch_shapes=[pltpu.VMEM((2, CH * SUBLANES, LANES), F32), pltpu.SemaphoreType.DMA((2,))]),
        input_output_aliases={3: 0},
        compiler_params=pltpu.CompilerParams(dimension_semantics=("arbitrary",), vmem_limit_bytes=VMEM_LIMIT,
                                             has_side_effects=True),
        name="moe_scatter_rows",
    )(pos0, pos1, x, jnp.zeros((p_rows * SUBLANES, LANES), F32) if sorted_buf is None else sorted_buf)

    def row_idx(i, te, nv):
        return (jnp.minimum(i, nv[0] - 1), 0)

    ys = pl.pallas_call(
        functools.partial(_expert_kernel, TM=TM),
        out_shape=jax.ShapeDtypeStruct((p_rows * SUBLANES, LANES), F32),
        grid_spec=pltpu.PrefetchScalarGridSpec(
            num_scalar_prefetch=2, grid=(max_tiles,),
            in_specs=[pl.BlockSpec((TM * SUBLANES, LANES), row_idx),
                      pl.BlockSpec((1, d), lambda i, te, nv: (0, 0)),
                      pl.BlockSpec((1, 1, d, f), lambda i, te, nv: (layer, te[i], 0, 0)),
                      pl.BlockSpec((1, 1, d, f), lambda i, te, nv: (layer, te[i], 0, 0)),
                      pl.BlockSpec((1, 1, f, d), lambda i, te, nv: (layer, te[i], 0, 0))],
            out_specs=pl.BlockSpec((TM * SUBLANES, LANES), lambda i, te, nv: (i, 0))),
        compiler_params=_cparams(("arbitrary",)),
        name="moe_experts",
    )(tile_exp, n_valid, xs, g.reshape(1, d), wg, wu, wd)

    TC = MOE_COMBINE_TILE
    out = pl.pallas_call(
        functools.partial(_combine_kernel, TC=TC, n_tiles=t // TC, final_norm=final_g is not None),
        out_shape=jax.ShapeDtypeStruct((t, d), F32),
        grid_spec=pltpu.PrefetchScalarGridSpec(
            num_scalar_prefetch=2, grid=(t // TC,),
            in_specs=[pl.BlockSpec((TC, d), lambda i, p0, p1: (i, 0)),
                      pl.BlockSpec((TC, ROUTER_LANES), lambda i, p0, p1: (i, 0)),
                      pl.BlockSpec((1, d), lambda i, p0, p1: (0, 0)),
                      pl.BlockSpec(memory_space=pl.ANY)],
            out_specs=pl.BlockSpec((TC, d), lambda i, p0, p1: (i, 0)),
            scratch_shapes=[pltpu.VMEM((2, 2, TC * SUBLANES, LANES), F32), pltpu.SemaphoreType.DMA((2,))]),
        compiler_params=_cparams(("arbitrary",)),
        name="moe_combine",
    )(pos0, pos1, x, gate, (g if final_g is None else final_g).reshape(1, d), ys)
    return out, xs


def _rwkv_kernel(p_ref, prev_ref, s0_ref, mu_ref, w0_ref, wd_ref, a0_ref, wa_ref, wg_ref, kk_ref, ka_ref,
                 bonus_ref, lng_ref, lnb_ref, tri_ref, hsum_ref, ya_ref, sf_ref, s_scr, prev_scr,
                 *, NB, C, H, DH, n_chunks):
    c = pl.program_id(1)

    @pl.when(c == 0)
    def _():
        s_scr[...] = s0_ref[:, 0]
        prev_scr[...] = prev_ref[:, 0]

    DA = H * DH
    R = NB * C
    p = p_ref[...].reshape(R, p_ref.shape[-1])
    row = lax.broadcasted_iota(jnp.int32, p.shape, 0)
    shifted = pltpu.roll(p, 1, axis=0)
    for n in range(NB):
        shifted = jnp.where(row == n * C, prev_scr[n], shifted)
        prev_scr[n] = p[(n + 1) * C - 1:(n + 1) * C, :]
    xs = p + (shifted - p) * mu_ref[...]
    r = xs[:, 0:DA]
    k = xs[:, DA:2 * DA]
    v = xs[:, 2 * DA:3 * DA]
    lora = xs[:, 3 * DA:3 * DA + R_DECAY + R_ICLR]
    gd = xs[:, 3 * DA + R_DECAY + R_ICLR:3 * DA + R_DECAY + R_ICLR + R_GATE]

    w_log = -_softplus(-(w0_ref[...] + _mp_dot(_dot, jnp.tanh(lora), wd_ref[...], 1))) - 0.5
    lw = -jnp.exp(w_log)
    a = _sigmoid(a0_ref[...] + _mp_dot(_dot, lora, wa_ref[...], 1))
    g = _mp_dot(_dot, _sigmoid(gd), wg_ref[...], 1)

    kk = k * kk_ref[...]
    ss = _dot_exact_rhs(kk * kk, hsum_ref[...], 2)
    kk = kk / jnp.maximum(jnp.sqrt(ss), 1e-12)
    k2 = k * (1.0 + (a - 1.0) * ka_ref[...])
    kka = kk * a

    cum = jnp.concatenate([_dot_exact_lhs(tri_ref[...], lw[n * C:(n + 1) * C], 3) for n in range(NB)], axis=0)
    p_in = jnp.exp(cum)
    r_t = r * p_in
    a_t = kk * jnp.exp(cum - lw)
    p_inv = jnp.exp(-cum)
    b_t = kka * p_inv
    k_t = k2 * p_inv
    bonus = _dot_exact_rhs(r * k2 * bonus_ref[...], hsum_ref[...], 2) * v

    ri = lax.broadcasted_iota(jnp.int32, (C, C), 0)
    ci = lax.broadcasted_iota(jnp.int32, (C, C), 1)
    strict = ri > ci
    incl = ri >= ci
    eye = (ri == ci).astype(F32)
    n_double = max(int(math.ceil(math.log2(C))) - 1, 0)

    chains = [(n, h) for n in range(NB) for h in range(H)]

    def blk(x, n, h):
        return x[n * C:(n + 1) * C, h * DH:(h + 1) * DH]

    def bf(x):
        return x.astype(BF16)

    Bt = [bf(blk(b_t, n, h)) for n, h in chains]
    Kt = [bf(blk(k_t, n, h)) for n, h in chains]
    Vf = [blk(v, n, h) for n, h in chains]
    AR = [bf(jnp.concatenate([blk(a_t, n, h), blk(r_t, n, h)], axis=0)) for n, h in chains]
    S0 = [s_scr[n, h] for n, h in chains]
    idx = range(len(chains))
    GB = [_dot_nt(AR[i], Bt[i]) for i in idx]
    GK = [_dot_nt(AR[i], Kt[i]) for i in idx]
    ARS = [_dot_nt(AR[i], bf(S0[i])) for i in idx]
    Lm = [jnp.where(strict, GB[i][0:C], 0.0) for i in idx]
    Gb = [bf(jnp.where(incl, GB[i][C:2 * C], 0.0)) for i in idx]
    MG = [bf(jnp.concatenate([jnp.where(strict, GK[i][0:C], 0.0), jnp.where(incl, GK[i][C:2 * C], 0.0)], axis=0))
          for i in idx]
    MGV = [_dot(MG[i], bf(Vf[i])) for i in idx]
    T = [eye - Lm[i] for i in idx]
    Pw = [bf(Lm[i]) for i in idx]
    for _ in range(n_double):
        Pw = [bf(_dot(Pw[i], Pw[i])) for i in idx]
        T = [T[i] + _dot(bf(T[i]), Pw[i]) for i in idx]
    U = [_dot(bf(T[i]), bf(-(ARS[i][0:C] + MGV[i][0:C]))) for i in idx]
    Y = [ARS[i][C:2 * C] + _dot(Gb[i], bf(U[i])) + MGV[i][C:2 * C] for i in idx]
    for i, (n, h) in enumerate(chains):
        UV = bf(jnp.concatenate([U[i], Vf[i]], axis=0))
        BK = jnp.concatenate([Bt[i], Kt[i]], axis=0)
        p_tot = p_in[(n + 1) * C - 1:(n + 1) * C, h * DH:(h + 1) * DH]
        s_scr[n, h] = (S0[i] + _dot_tn(UV, BK)) * p_tot

    rows = []
    for n in range(NB):
        ys = []
        for h in range(H):
            Yh = Y[n * H + h]
            yc = Yh - jnp.mean(Yh, axis=-1, keepdims=True)
            var = jnp.mean(yc * yc, axis=-1, keepdims=True)
            ys.append(yc * lax.rsqrt(var + GN_EPS))
        rows.append(jnp.concatenate(ys, axis=-1))
    y = jnp.concatenate(rows, axis=0) * lng_ref[...] + lnb_ref[...]
    ya_ref[...] = ((y + bonus) * g).reshape(NB, C, DA)

    @pl.when(c == n_chunks - 1)
    def _():
        sf_ref[:, 0] = s_scr[...]


def rwkv_mix(pa, n_batch, seq, shift_prev, wkv0, wts, n_par):
    t, ap = pa.shape
    H = wkv0.shape[1]
    DA = H * DH_A
    C = min(RWKV_CHUNK, seq)
    n_chunks = seq // C
    NB = n_par
    G = n_batch // NB
    tri = jnp.asarray(np.tril(np.ones((C, C), np.float32))).astype(BF16)
    hsum = jnp.asarray(np.kron(np.eye(H, dtype=np.float32), np.ones((DH_A, DH_A), np.float32))).astype(BF16)

    def full(shape):
        nd = len(shape)
        return pl.BlockSpec(shape, lambda b, c: (0,) * nd)

    vec = full((1, DA))
    ya, s_fin = pl.pallas_call(
        functools.partial(_rwkv_kernel, NB=NB, C=C, H=H, DH=DH_A, n_chunks=n_chunks),
        out_shape=[jax.ShapeDtypeStruct((NB, t // NB, DA), F32),
                   jax.ShapeDtypeStruct((NB, G, H, DH_A, DH_A), F32)],
        grid=(G, n_chunks),
        in_specs=[pl.BlockSpec((NB, C, ap), lambda b, c: (0, b * n_chunks + c, 0)),
                  pl.BlockSpec((NB, 1, 1, ap), lambda b, c: (0, b, 0, 0)),
                  pl.BlockSpec((NB, 1, H, DH_A, DH_A), lambda b, c: (0, b, 0, 0, 0)),
                  full((1, ap)), vec, full((R_DECAY + R_ICLR, DA)), vec, full((R_DECAY + R_ICLR, DA)),
                  full((R_GATE, DA)), vec, vec, vec, vec, vec, full((C, C)), full((DA, DA))],
        out_specs=[pl.BlockSpec((NB, C, DA), lambda b, c: (0, b * n_chunks + c, 0)),
                   pl.BlockSpec((NB, 1, H, DH_A, DH_A), lambda b, c: (0, b, 0, 0, 0))],
        scratch_shapes=[pltpu.VMEM((NB, H, DH_A, DH_A), F32), pltpu.VMEM((NB, 1, ap), F32)],
        compiler_params=_cparams(("parallel", "arbitrary")),
        name="rwkv7_mix",
    )(pa.reshape(NB, t // NB, ap), shift_prev.reshape(NB, G, 1, ap), wkv0.reshape(NB, G, H, DH_A, DH_A),
      wts["mu"], wts["w0"], wts["wd"], wts["a0"], wts["wa"],
      wts["wg"], wts["key_k"], wts["key_a"], wts["bonus"], wts["lnx_g"], wts["lnx_b"], tri, hsum)
    return ya.reshape(t, DA), s_fin.reshape(wkv0.shape)


GMLP_TILE = 128
GMLP_ROWS = 512


def _gmlp_kernel(u_ref, v_ref, ng_ref, nb_ref, wm_ref, bias_ref, o_ref, *vn_refs, n_sub):
    vf = _gelu(v_ref[...])
    mu = jnp.mean(vf, axis=-1, keepdims=True)
    vc = vf - mu
    var = jnp.mean(vc * vc, axis=-1, keepdims=True)
    vn = vc * lax.rsqrt(var + NORM_EPS) * ng_ref[...] + nb_ref[...]
    for vn_ref in vn_refs:
        vn_ref[...] = vn
    vb = vn.astype(BF16)
    n_h = wm_ref.shape[0]
    cb = vn.shape[1] // n_h
    gu = _gelu(u_ref[...])
    for c in range(n_sub):
        rows = slice(c * GMLP_TILE, (c + 1) * GMLP_TILE)
        s = jnp.concatenate([_dot(wm_ref[h], vb[rows, h * cb:(h + 1) * cb]) for h in range(n_h)], axis=-1)
        o_ref[rows, :] = gu[rows, :] * (s + bias_ref[...])


def gmlp_mix(pu, pv, ng, nb, wm_bf16, bias_tile, want_vn):
    t, db = pu.shape
    n_h = wm_bf16.shape[0]
    rows = GMLP_ROWS if t % GMLP_ROWS == 0 else t
    n_out = 2 if want_vn else 1
    outs = pl.pallas_call(
        functools.partial(_gmlp_kernel, n_sub=rows // GMLP_TILE),
        out_shape=[jax.ShapeDtypeStruct((t, db), F32)] * n_out,
        grid=(t // rows,),
        in_specs=[pl.BlockSpec((rows, db), lambda i: (i, 0)),
                  pl.BlockSpec((rows, db), lambda i: (i, 0)),
                  pl.BlockSpec((1, db), lambda i: (0, 0)),
                  pl.BlockSpec((1, db), lambda i: (0, 0)),
                  pl.BlockSpec((n_h, GMLP_TILE, GMLP_TILE), lambda i: (0, 0, 0)),
                  pl.BlockSpec((GMLP_TILE, db), lambda i: (0, 0))],
        out_specs=[pl.BlockSpec((rows, db), lambda i: (i, 0))] * n_out,
        compiler_params=_cparams(("parallel",)),
        name="gmlp_mix",
    )(pu, pv, ng, nb, wm_bf16, bias_tile)
    return (outs[0], outs[1]) if want_vn else (outs[0], None)


N_SEG = 8
SEG_GAP = 4


def _seg_pitch(seg):
    return seg + SEG_GAP if seg % SUBLANES == 0 else seg


def _rglru_kernel(xb_ref, gy_ref, cprev_ref, h0_ref, cw_ref, cb_ref, gw_ref, gb_ref, lam_ref,
                  yc_ref, ctail_ref, hl_ref, xe_scr, a_scr, b_scr, h_scr, *, TL, DC, pos0, n_tiles):
    l = pl.program_id(1)
    PAD = 8

    @pl.when(l == 0)
    def _():
        xe_scr[0:PAD, :] = cprev_ref[0]
        h_scr[...] = h0_ref[0]

    xe_scr[PAD:PAD + TL, :] = xb_ref[...]
    xc = cb_ref[...] + xe_scr[pl.ds(PAD - (CONV_W - 1), TL), :] * cw_ref[0:1, :]
    for i in range(1, CONV_W):
        xc = xc + xe_scr[pl.ds(PAD - (CONV_W - 1) + i, TL), :] * cw_ref[i:i + 1, :]
    tail = xe_scr[TL:TL + PAD, :]
    ctail_ref[0] = tail
    xe_scr[0:PAD, :] = tail

    gates = _dot(xc.astype(BF16), gw_ref[...]) + gb_ref[...]
    rg = _sigmoid(gates[:, 0:DC])
    ig = _sigmoid(gates[:, DC:2 * DC])
    log_a = -LRU_C * rg * _softplus(-lam_ref[...])
    a = jnp.exp(log_a)
    mult = jnp.sqrt(1.0 - a * a)
    row = lax.broadcasted_iota(jnp.int32, (TL, DC), 0)
    mult = jnp.where(row + (l * TL + pos0) == 0, 1.0, mult)
    b = mult * ig * xc
    n_slab = DC // LANES
    seg = TL // N_SEG
    pitch = _seg_pitch(seg)
    for s in range(n_slab):
        for j in range(N_SEG):
            a_scr[s, pl.ds(j * pitch, seg), :] = a[j * seg:(j + 1) * seg, s * LANES:(s + 1) * LANES]
            b_scr[s, pl.ds(j * pitch, seg), :] = b[j * seg:(j + 1) * seg, s * LANES:(s + 1) * LANES]

    def step(i, carry):
        idx = pl.ds(i, N_SEG, stride=pitch) if pitch > 1 else pl.ds(0, N_SEG)
        out = []
        for s in range(n_slab):
            hloc, ap = carry[s]
            ai = a_scr[s, idx, :]
            hloc = ai * hloc + b_scr[s, idx, :]
            ap = ap * ai
            b_scr[s, idx, :] = hloc
            a_scr[s, idx, :] = ap
            out.append((hloc, ap))
        return tuple(out)

    lax.fori_loop(0, seg, step,
                  tuple((jnp.zeros((N_SEG, LANES), F32), jnp.ones((N_SEG, LANES), F32)) for _ in range(n_slab)),
                  unroll=min(seg, 8))

    carry = h_scr[...]
    g_act = _gelu(gy_ref[...])
    for j in range(N_SEG):
        rows = slice(j * seg, (j + 1) * seg)
        rows_p = pl.ds(j * pitch, seg)
        hloc = jnp.concatenate([b_scr[s, rows_p, :] for s in range(n_slab)], axis=-1)
        ap = jnp.concatenate([a_scr[s, rows_p, :] for s in range(n_slab)], axis=-1)
        hj = hloc + ap * carry
        yc_ref[rows, :] = g_act[rows, :] * hj
        carry = hj[seg - 1:seg, :]
    h_scr[...] = carry

    @pl.when(l == n_tiles - 1)
    def _():
        hl_ref[0] = carry


def rglru_mix(xb, gy, n_batch, seq, conv_prev8, h0, pos0, wts):
    t, dc = xb.shape
    TL = 512 if seq % 512 == 0 else seq
    n_tiles = seq // TL
    scan_rows = N_SEG * _seg_pitch(TL // N_SEG)

    def full(shape):
        nd = len(shape)
        return pl.BlockSpec(shape, lambda b, l: (0,) * nd)

    yc, ctail, hl = pl.pallas_call(
        functools.partial(_rglru_kernel, TL=TL, DC=dc, pos0=pos0, n_tiles=n_tiles),
        out_shape=[jax.ShapeDtypeStruct((t, dc), F32), jax.ShapeDtypeStruct((n_batch, 8, dc), F32),
                   jax.ShapeDtypeStruct((n_batch, 1, dc), F32)],
        grid=(n_batch, n_tiles),
        in_specs=[pl.BlockSpec((TL, dc), lambda b, l: (b * n_tiles + l, 0)),
                  pl.BlockSpec((TL, dc), lambda b, l: (b * n_tiles + l, 0)),
                  pl.BlockSpec((1, 8, dc), lambda b, l: (b, 0, 0)),
                  pl.BlockSpec((1, 1, dc), lambda b, l: (b, 0, 0)),
                  full((CONV_W, dc)), full((1, dc)), full((dc, 2 * dc)), full((1, 2 * dc)), full((1, dc))],
        out_specs=[pl.BlockSpec((TL, dc), lambda b, l: (b * n_tiles + l, 0)),
                   pl.BlockSpec((1, 8, dc), lambda b, l: (b, 0, 0)),
                   pl.BlockSpec((1, 1, dc), lambda b, l: (b, 0, 0))],
        scratch_shapes=[pltpu.VMEM((TL + 8, dc), F32), pltpu.VMEM((dc // LANES, scan_rows, LANES), F32),
                        pltpu.VMEM((dc // LANES, scan_rows, LANES), F32), pltpu.VMEM((1, dc), F32)],
        compiler_params=_cparams(("parallel", "arbitrary")),
        name="rglru_mix",
    )(xb, gy, conv_prev8, h0.reshape(n_batch, 1, dc), wts["conv_w"], wts["conv_b"], wts["gate_w"], wts["gate_b"],
      wts["lam"])
    return yc, ctail[:, 8 - (CONV_W - 1):, :], hl.reshape(n_batch, dc)


def _t5_bucket(dist):
    dist = np.asarray(dist)
    max_exact = N_BUCKETS // 2
    scaled = np.log(np.maximum(dist, 1) / max_exact) / math.log(BUCKET_MAX_DIST / max_exact)
    large = np.minimum(max_exact + (scaled * (N_BUCKETS - max_exact)).astype(np.int32), N_BUCKETS - 1)
    return np.where(dist < max_exact, dist, large).astype(np.int32)


def _dist_table(rel_bias, max_dist):
    dist = np.arange(max_dist + 1)
    count = np.zeros(max_dist + 1, np.float32)
    for window, dil in DILATED:
        count += ((dist % dil == 0) & (dist <= window)).astype(np.float32)
    logcnt = np.where(count > 0, np.log(np.maximum(count, 1.0)), 0.0).astype(np.float32)
    tab = jnp.take(rel_bias, jnp.asarray(_t5_bucket(dist)), axis=0) + jnp.asarray(logcnt)[:, None]
    return jnp.where(jnp.asarray(count > 0)[:, None], tab, NEG_BIG)


def _toeplitz_tiles(tab, n_pos, n_neg, T):
    D, H = tab.shape
    span = T * n_pos
    assert D >= span
    n_col = span + T * n_neg + T - 1
    ext = jnp.concatenate([jnp.flip(tab[:span], axis=0), jnp.full((n_col + 1 - span, H), NEG_BIG, F32)], axis=0)
    ext = jnp.transpose(ext)
    skew = jnp.tile(ext, (1, T))[:, :T * n_col].reshape(H, T, n_col)
    tiles = [skew[:, :, span - 1 - T * dd: span - 1 - T * dd + T] for dd in range(-n_neg, n_pos)]
    return jnp.stack(tiles, axis=1)


def _attn_prompt_kernel(q_ref, k_ref, v_ref, bias_ref, o_ref, kb_scr, vb_scr, *, E, SUB, NS):
    qi = pl.program_id(2)
    TQ = NS * SUB

    @pl.when(qi == 0)
    def _():
        kb_scr[...] = k_ref[0].astype(BF16)
        vb_scr[...] = v_ref[0].astype(BF16)

    lane = lax.broadcasted_iota(jnp.int32, (SUB, 2 * E), 1)
    q2 = []
    for rs in range(NS):
        q = q_ref[0, rs * SUB:(rs + 1) * SUB, :] * (E ** -0.5 * LOG2E)
        q2.append(jnp.concatenate([jnp.where(lane < E, q, 0.0), jnp.where(lane >= E, q, 0.0)], axis=0).astype(BF16))

    def block(i, carry, diagonal):
        j = qi - i
        koff = pl.multiple_of(j * TQ, TQ)
        out = []
        for rs in range(NS):
            n_cs = rs + 1 if diagonal else NS
            kj = kb_scr[:, pl.ds(koff, n_cs * SUB)]
            vj = vb_scr[:, pl.ds(koff, n_cs * SUB)]
            m, l, acc = carry[rs]
            s = _dot(q2[rs], kj)
            parts = []
            for cs in range(n_cs):
                dd = i * NS + (rs - cs + NS - 1)
                bias = jnp.concatenate([bias_ref[0, dd], bias_ref[1, dd]], axis=0)
                parts.append(s[:, cs * SUB:(cs + 1) * SUB] + bias)
            mx = parts[0]
            for part in parts[1:]:
                mx = jnp.maximum(mx, part)
            m_new = jnp.maximum(m, jnp.max(mx, axis=-1, keepdims=True))
            alpha = jnp.exp2(m - m_new)
            ps = [jnp.exp2(part - m_new) for part in parts]
            psum = ps[0]
            for pexp in ps[1:]:
                psum = psum + pexp
            l = alpha * l + psum
            acc = alpha * acc + _dot_nt(jnp.concatenate(ps, axis=-1).astype(BF16), vj)
            out.append((m_new, l, acc))
        return tuple(out)

    init = tuple((jnp.full((2 * SUB, SUB), NEG_BIG, F32), jnp.zeros((2 * SUB, SUB), F32),
                  jnp.zeros((2 * SUB, 2 * E), F32)) for _ in range(NS))
    first = block(0, init, True)
    res = lax.fori_loop(1, qi + 1, lambda i, carry: block(i, carry, False), first)
    for rs in range(NS):
        m, l, acc = res[rs]
        o = acc / jnp.sum(l, axis=-1, keepdims=True)
        o_ref[0, rs * SUB:(rs + 1) * SUB, :] = jnp.where(lane < E, o[0:SUB], o[SUB:2 * SUB])


def attn_prompt(q, k, v, bias_tiles, n_batch, seq):
    hd = q.shape[-1]
    E = hd // H_D
    SUB = ATT_TILE
    NS = ATT_SUBTILES
    TQ = SUB * NS
    nq = seq // TQ
    nt = bias_tiles.shape[1]
    return pl.pallas_call(
        functools.partial(_attn_prompt_kernel, E=E, SUB=SUB, NS=NS),
        out_shape=jax.ShapeDtypeStruct((n_batch, seq, hd), F32),
        grid=(H_D // 2, n_batch, nq),
        in_specs=[pl.BlockSpec((1, TQ, 2 * E), lambda hp, b, i: (b, i, hp)),
                  pl.BlockSpec((1, 2 * E, seq), lambda hp, b, i: (b, hp, 0)),
                  pl.BlockSpec((1, 2 * E, seq), lambda hp, b, i: (b, hp, 0)),
                  pl.BlockSpec((2, nt, SUB, SUB), lambda hp, b, i: (hp, 0, 0, 0))],
        out_specs=pl.BlockSpec((1, TQ, 2 * E), lambda hp, b, i: (b, i, hp)),
        scratch_shapes=[pltpu.VMEM((2 * E, seq), BF16), pltpu.VMEM((2 * E, seq), BF16)],
        compiler_params=_cparams(("arbitrary", "arbitrary", "arbitrary")),
        name="dilated_attn_prompt",
    )(q, k, v, bias_tiles)


def _attn_sample_kernel(q_ref, kn_ref, vn_ref, ck_ref, cv_ref, bo_ref, bn_ref, o_ref, *, E, S):
    lane = lax.broadcasted_iota(jnp.int32, (S, 2 * E), 1)
    NPAD = bn_ref.shape[-1]
    outs = []
    for hp in range(H_D // 2):
        sl = slice(hp * 2 * E, (hp + 1) * 2 * E)
        q = q_ref[0, :, sl] * (E ** -0.5)
        q2 = jnp.concatenate([jnp.where(lane < E, q, 0.0), jnp.where(lane >= E, q, 0.0)], axis=0).astype(BF16)
        zpad = jnp.zeros((NPAD - S, 2 * E), F32)
        kn = jnp.concatenate([kn_ref[0, :, sl], zpad], axis=0).astype(BF16)
        vn = jnp.concatenate([vn_ref[0, :, sl], zpad], axis=0).astype(BF16)
        s_old = _dot(q2, ck_ref[0, sl, :].astype(BF16)) + jnp.concatenate([bo_ref[2 * hp], bo_ref[2 * hp + 1]], axis=0)
        s_new = _dot_nt(q2, kn) + jnp.concatenate([bn_ref[2 * hp], bn_ref[2 * hp + 1]], axis=0)
        m = jnp.maximum(jnp.max(s_old, axis=-1, keepdims=True), jnp.max(s_new, axis=-1, keepdims=True))
        p_old = jnp.exp(s_old - m)
        p_new = jnp.exp(s_new - m)
        l = jnp.sum(p_old, axis=-1, keepdims=True) + jnp.sum(p_new, axis=-1, keepdims=True)
        acc = _dot_nt(p_old.astype(BF16), cv_ref[0, sl, :].astype(BF16)) + _dot(p_new.astype(BF16), vn)
        o = acc / l
        outs.append(jnp.where(lane < E, o[0:S], o[S:2 * S]))
    o_ref[0] = jnp.concatenate(outs, axis=-1)


def attn_sample(q, k_new, v_new, cache_k, cache_v, bias_old, bias_new):
    n_batch, S, hd = q.shape
    W = cache_k.shape[2]
    E = hd // H_D
    NPAD = bias_new.shape[-1]
    return pl.pallas_call(
        functools.partial(_attn_sample_kernel, E=E, S=S),
        out_shape=jax.ShapeDtypeStruct((n_batch, S, hd), F32),
        grid=(n_batch,),
        in_specs=[pl.BlockSpec((1, S, hd), lambda b: (b, 0, 0)),
                  pl.BlockSpec((1, S, hd), lambda b: (b, 0, 0)),
                  pl.BlockSpec((1, S, hd), lambda b: (b, 0, 0)),
                  pl.BlockSpec((1, hd, W), lambda b: (b, 0, 0)),
                  pl.BlockSpec((1, hd, W), lambda b: (b, 0, 0)),
                  pl.BlockSpec((H_D, S, W), lambda b: (0, 0, 0)),
                  pl.BlockSpec((H_D, S, NPAD), lambda b: (0, 0, 0))],
        out_specs=pl.BlockSpec((1, S, hd), lambda b: (b, 0, 0)),
        compiler_params=_cparams(("parallel",)),
        name="dilated_attn_sample",
    )(q, k_new, v_new, cache_k, cache_v, bias_old, bias_new)


def _even_weights(j, w_in_even, w_out_even, shift_mu, decay_w0, decay_up, iclr_a0, iclr_up, gate_up, key_k, key_a,
                  bonus_r_k, lnx_g, lnx_b, sgu_norm_g, sgu_norm_b, sgu_w, sgu_b):
    da = decay_w0.shape[1]
    zeros_d = jnp.zeros((R_ICLR, da), F32)
    zeros_i = jnp.zeros((R_DECAY, da), F32)
    return dict(
        w_in=w_in_even[j].astype(BF16),
        w_out_a=w_out_even[j, :da].astype(BF16), w_out_b=w_out_even[j, da:].astype(BF16),
        mu=shift_mu[j].reshape(1, -1), w0=decay_w0[j].reshape(1, -1), a0=iclr_a0[j].reshape(1, -1),
        wd=jnp.concatenate([decay_up[j], zeros_d], axis=0).astype(BF16),
        wa=jnp.concatenate([zeros_i, iclr_up[j]], axis=0).astype(BF16), wg=gate_up[j].astype(BF16), key_k=key_k[j].reshape(1, -1), key_a=key_a[j].reshape(1, -1),
        bonus=bonus_r_k[j].reshape(1, -1), lnx_g=lnx_g[j].reshape(1, -1), lnx_b=lnx_b[j].reshape(1, -1),
        ng=sgu_norm_g[j].reshape(1, -1), nb=sgu_norm_b[j].reshape(1, -1), sgu_w=sgu_w[j], sgu_b=sgu_b[j])


def _gmlp_tables(sgu_w, sgu_b, chunk):
    reps = GMLP_TILE // chunk
    n_h = sgu_w.shape[0]
    cb = None
    wm = sgu_w[:, :chunk, :chunk] * jnp.asarray(np.tril(np.ones((chunk, chunk), np.float32)))
    if reps > 1:
        eye = jnp.asarray(np.eye(reps, dtype=np.float32))
        wm = jnp.einsum("ab,hts->hatbs", eye, wm).reshape(n_h, GMLP_TILE, GMLP_TILE)
    bias = jnp.tile(jnp.transpose(sgu_b[:, :chunk]), (reps, 1))
    return wm.astype(BF16), bias


def _even_layer(x, n_batch, seq, chunk, n_par, want_vn, shift_prev, wkv0, norm_g, ew):
    pa, pu, pv = norm_matmul(x, norm_g, ew["w_in"], (ew["mu"].shape[1], ew["ng"].shape[1], ew["ng"].shape[1]))
    ya, wkv = rwkv_mix(pa, n_batch, seq, shift_prev, wkv0, ew, n_par)
    wm, bias = _gmlp_tables(ew["sgu_w"], ew["sgu_b"], chunk)
    cb = pu.shape[1] // wm.shape[0]
    bias_tile = jnp.repeat(bias, cb, axis=1)
    yb, vn = gmlp_mix(pu, pv, ew["ng"], ew["nb"], wm, bias_tile, want_vn)
    x = proj_residual(x, ya, yb, ew["w_out_a"], ew["w_out_b"])
    last = pa.reshape(n_batch, seq, -1)[:, -1]
    return x, last, wkv, vn


def _odd_weights(j, w_in_odd, w_out_odd, conv_w, conv_b, rgate_w, rgate_b, igate_w, igate_b, lru_lambda):
    dc = conv_b.shape[1]
    eye = jnp.asarray(np.eye(H_C, dtype=np.float32))

    def blockdiag(w):
        dh = w.shape[-1]
        return jnp.einsum("ab,aij->aibj", eye, w).reshape(H_C * dh, H_C * dh)

    return dict(
        w_in=w_in_odd[j].astype(BF16), w_kv_t=jnp.transpose(w_in_odd[j, :, 3 * dc:]).astype(BF16),
        w_out_c=w_out_odd[j, :dc].astype(BF16), w_out_d=w_out_odd[j, dc:].astype(BF16),
        conv_w=conv_w[j], conv_b=conv_b[j].reshape(1, -1),
        gate_w=jnp.concatenate([blockdiag(rgate_w[j]), blockdiag(igate_w[j])], axis=1).astype(BF16),
        gate_b=jnp.concatenate([rgate_b[j], igate_b[j]]).reshape(1, -1),
        lam=lru_lambda[j].reshape(1, -1))


def _odd_layer(x, n_batch, seq, conv_prev, h0, pos0, caches, dist_tab, norm_g, ow):
    dc = ow["lam"].shape[1]
    conv_prev8 = jnp.pad(conv_prev, ((0, 0), (8 - (CONV_W - 1), 0), (0, 0)))
    if caches is None:
        gy, xb, q, k_t, v_t = norm_matmul(x, norm_g, ow["w_in"][:, :3 * dc], (dc,) * 3, ow["w_kv_t"], (dc, dc), seq)
        hd = q.shape[1]
        e = hd // H_D
        yc, conv_last, h_last = rglru_mix(xb, gy, n_batch, seq, conv_prev8, h0, pos0, ow)
        tiles = _toeplitz_tiles(dist_tab * LOG2E, seq // ATT_TILE, ATT_SUBTILES - 1, ATT_TILE)
        o = attn_prompt(q.reshape(n_batch, seq, hd), k_t, v_t, tiles, n_batch, seq)
        k_rows = jnp.transpose(k_t.reshape(n_batch, H_D, e, seq), (0, 3, 1, 2))
        v_rows = jnp.transpose(v_t.reshape(n_batch, H_D, e, seq), (0, 3, 1, 2))
    else:
        gy, xb, q, k, v = norm_matmul(x, norm_g, ow["w_in"], (dc,) * 5)
        hd = q.shape[1]
        e = hd // H_D
        yc, conv_last, h_last = rglru_mix(xb, gy, n_batch, seq, conv_prev8, h0, pos0, ow)
        q3, k3, v3 = (a.reshape(n_batch, seq, hd) for a in (q, k, v))
        cache_k, cache_v = caches
        W = cache_k.shape[1]
        NPAD = 128
        tab_t = jnp.flip(jnp.transpose(dist_tab[:W + seq]), axis=1)
        b_old = jnp.stack([tab_t[:, seq - 1 - j:seq - 1 - j + W] for j in range(seq)], axis=1)
        d_new = np.arange(seq)[:, None] - np.arange(NPAD)[None, :]
        ok_new = (d_new >= 0) & (np.arange(NPAD)[None, :] < seq)
        b_new = jnp.take(dist_tab, jnp.asarray(np.maximum(d_new, 0)), axis=0)
        b_new = jnp.transpose(jnp.where(jnp.asarray(ok_new)[..., None], b_new, NEG_BIG), (2, 0, 1))
        ck = jnp.transpose(cache_k, (0, 2, 3, 1)).reshape(n_batch, hd, W)
        cv = jnp.transpose(cache_v, (0, 2, 3, 1)).reshape(n_batch, hd, W)
        o = attn_sample(q3, k3, v3, ck, cv, b_old, b_new)
        k_rows = k3.reshape(n_batch, seq, H_D, e)
        v_rows = v3.reshape(n_batch, seq, H_D, e)
    x = proj_residual(x, yc, o.reshape(n_batch * seq, hd), ow["w_out_c"], ow["w_out_d"])
    return x, conv_last, h_last, k_rows, v_rows


def _moe_weights(l, router_group_w, router_group_b, router_expert_w, router_expert_b):
    d = router_group_w.shape[1]
    n_used = N_GROUPS + router_expert_w.shape[2]
    rw = jnp.concatenate([router_group_w[l], router_expert_w[l], jnp.zeros((d, ROUTER_LANES - n_used), F32)], axis=1)
    rb = jnp.concatenate([router_group_b[l], router_expert_b[l], jnp.zeros((ROUTER_LANES - n_used,), F32)])
    return dict(rw=rw.astype(BF16), rb=rb.reshape(1, -1))


def kernel(x_prompt, x_sample, state_wkv, state_shift, state_conv, state_rglru, cache_k, cache_v, norm_mix, norm_ffn, norm_final, w_in_even, w_out_even, shift_mu, decay_w0, decay_up, iclr_a0, iclr_up, gate_up, key_k, key_a, bonus_r_k, lnx_g, lnx_b, sgu_norm_g, sgu_norm_b, sgu_w, sgu_b, w_in_odd, w_out_odd, conv_w, conv_b, rgate_w, rgate_b, igate_w, igate_b, lru_lambda, rel_bias, router_group_w, router_group_b, router_expert_w, router_expert_b, exp_w_gate, exp_w_up, exp_w_down):
    B, L, D = x_prompt.shape
    DB, S, _ = x_sample.shape
    depth = norm_mix.shape[0]
    xp = x_prompt.reshape(B * L, D)
    xs = x_sample.reshape(DB * S, D)
    W = cache_k.shape[2]
    dist_tab = _dist_table(rel_bias, max(L, W + S) - 1)

    sorted_buf = None
    wg_all, wu_all, wd_all = (w.astype(BF16) for w in (exp_w_gate, exp_w_up, exp_w_down))
    wkv_p, shift_p, conv_p, lru_p, k_p, v_p = [], [], [], [], [], []
    wkv_s, shift_s, chunkv_s, conv_s, lru_s, k_s, v_s = [], [], [], [], [], [], []
    for l in range(depth):
        j = l // 2
        if l % 2 == 0:
            ew = _even_weights(j, w_in_even, w_out_even, shift_mu, decay_w0, decay_up, iclr_a0, iclr_up, gate_up,
                               key_k, key_a, bonus_r_k, lnx_g, lnx_b, sgu_norm_g, sgu_norm_b, sgu_w, sgu_b)
            a_proj = ew["mu"].shape[1]
            h_a = state_wkv.shape[2]
            xp, sh, wkv, _ = _even_layer(xp, B, L, GMLP_TILE, RWKV_PAR_PROMPT, False, jnp.zeros((B, a_proj), F32),
                                         jnp.zeros((B, h_a, DH_A, DH_A), F32), norm_mix[l], ew)
            xs, sh_s, wkv_s_new, vn_s = _even_layer(xs, DB, S, S, RWKV_PAR_SAMPLE, True, state_shift[j], state_wkv[j], norm_mix[l], ew)
            wkv_p.append(wkv)
            shift_p.append(sh)
            wkv_s.append(wkv_s_new)
            shift_s.append(sh_s)
            chunkv_s.append(vn_s.reshape(DB, S, -1))
        else:
            ow = _odd_weights(j, w_in_odd, w_out_odd, conv_w, conv_b, rgate_w, rgate_b, igate_w, igate_b, lru_lambda)
            dc = ow["lam"].shape[1]
            xp, cv, hl, kr, vr = _odd_layer(xp, B, L, jnp.zeros((B, CONV_W - 1, dc), F32), jnp.zeros((B, dc), F32),
                                            0, None, dist_tab, norm_mix[l], ow)
            xs, cv_s, hl_s, kr_s, vr_s = _odd_layer(xs, DB, S, state_conv[j], state_rglru[j], PAST_LEN,
                                                    (cache_k[j], cache_v[j]), dist_tab, norm_mix[l], ow)
            conv_p.append(cv)
            lru_p.append(hl)
            k_p.append(kr)
            v_p.append(vr)
            conv_s.append(cv_s)
            lru_s.append(hl_s)
            k_s.append(kr_s)
            v_s.append(vr_s)
        mw = _moe_weights(l, router_group_w, router_group_b, router_expert_w, router_expert_b)
        xp, sorted_buf = moe_layer_sparse(xp, norm_ffn[l], mw["rw"], mw["rb"], wg_all, wu_all, wd_all, l,
                                          final_g=norm_final if l == depth - 1 else None, sorted_buf=sorted_buf)
        xs = moe_layer(xs, norm_ffn[l], mw["rw"], mw["rb"], wg_all, wu_all, wd_all, l)
    y_prompt = xp.reshape(B, L, D)
    y_sample = rmsnorm_call(xs, norm_final).reshape(DB, S, D)
    return (y_prompt, y_sample,
            jnp.stack(wkv_p), jnp.stack(shift_p), jnp.stack(conv_p), jnp.stack(lru_p), jnp.stack(k_p), jnp.stack(v_p),
            jnp.stack(wkv_s), jnp.stack(shift_s), jnp.stack(chunkv_s), jnp.stack(conv_s), jnp.stack(lru_s),
            jnp.stack(k_s), jnp.stack(v_s))
```

```python
import functools
import math

import numpy as np
import jax
import jax.numpy as jnp
from jax import lax
from jax.experimental import pallas as pl
from jax.experimental.pallas import tpu as pltpu

F32 = jnp.float32
BF16 = jnp.bfloat16

PAST_LEN = 8192
DH_A = 64
R_DECAY = 64
R_ICLR = 64
R_GATE = 128
GN_EPS = 64e-5
H_C = 8
CONV_W = 4
LRU_C = 8.0
H_D = 8
DILATED = ((128, 1), (512, 4), (2048, 16))
N_BUCKETS = 32
BUCKET_MAX_DIST = 2048
NEG_BIG = -1e30
N_GROUPS = 4
EXP_PER_GROUP = 4
NORM_EPS = 1e-6
LOG2E = math.log2(math.e)

VMEM_LIMIT = 56 * 1024 * 1024
RWKV_CHUNK = 64
RWKV_PAR_PROMPT = 4
RWKV_PAR_SAMPLE = 8
ATT_TILE = 128
ATT_SUBTILES = 4
LANES = 128
SUBLANES = 8


def _cparams(sem):
    return pltpu.CompilerParams(dimension_semantics=sem, vmem_limit_bytes=VMEM_LIMIT)


def _dot(a, b, precision=None):
    return jnp.dot(a, b, preferred_element_type=F32, precision=precision)


def _dot_nt(a, b, precision=None):
    return lax.dot_general(a, b, (((1,), (1,)), ((), ())), preferred_element_type=F32, precision=precision)


def _dot_tn(a, b, precision=None):
    return lax.dot_general(a, b, (((0,), (0,)), ((), ())), preferred_element_type=F32, precision=precision)


def _split_bf16(x, n):
    parts = []
    for _ in range(n):
        hi = x.astype(BF16)
        parts.append(hi)
        x = x - hi.astype(F32)
    return parts


def _dot_exact_rhs(a, b_bf16, n_split):
    parts = _split_bf16(a, n_split)
    acc = _dot(parts[0], b_bf16)
    for part in parts[1:]:
        acc = acc + _dot(part, b_bf16)
    return acc


def _dot_exact_lhs(a_bf16, b, n_split):
    parts = _split_bf16(b, n_split)
    acc = _dot(a_bf16, parts[0])
    for part in parts[1:]:
        acc = acc + _dot(a_bf16, part)
    return acc


def _softplus(x):
    return jnp.maximum(x, 0.0) + jnp.log(1.0 + jnp.exp(-jnp.abs(x)))


def _sigmoid(x):
    return 1.0 / (1.0 + jnp.exp(-x))


def _gelu(x):
    c = math.sqrt(2.0 / math.pi)
    return 0.5 * x * (1.0 + jnp.tanh(c * (x + 0.044715 * (x * x * x))))


def _row_tile(t, pref=512):
    return pref if t % pref == 0 else t


def _norm_matmul_kernel(x_ref, g_ref, w_ref, *refs, splits, t_splits):
    x = x_ref[...]
    ms = jnp.mean(x * x, axis=-1, keepdims=True)
    h = (x * lax.rsqrt(ms + NORM_EPS) * g_ref[...]).astype(BF16)
    if t_splits:
        wt_ref, refs = refs[0], refs[1:]
    off = 0
    for o_ref, n in zip(refs[:len(splits)], splits):
        o_ref[...] = _dot(h, w_ref[:, off:off + n])
        off += n
    off = 0
    for o_ref, n in zip(refs[len(splits):], t_splits):
        o_ref[0] = _dot_nt(wt_ref[off:off + n, :], h)
        off += n


def norm_matmul(x, g, w_bf16, splits, wt_bf16=None, t_splits=(), seq=None):
    t, d = x.shape
    n = w_bf16.shape[1]
    tm = _row_tile(t)
    in_specs = [pl.BlockSpec((tm, d), lambda i: (i, 0)),
                pl.BlockSpec((1, d), lambda i: (0, 0)),
                pl.BlockSpec((d, n), lambda i: (0, 0))]
    args = [x, g.reshape(1, d), w_bf16]
    out_shape = [jax.ShapeDtypeStruct((t, s), F32) for s in splits]
    out_specs = [pl.BlockSpec((tm, s), lambda i: (i, 0)) for s in splits]
    if t_splits:
        tiles = seq // tm
        in_specs.append(pl.BlockSpec(wt_bf16.shape, lambda i: (0, 0)))
        args.append(wt_bf16)
        out_shape += [jax.ShapeDtypeStruct((t // seq, s, seq), F32) for s in t_splits]
        out_specs += [pl.BlockSpec((1, s, tm), lambda i: (i // tiles, 0, i % tiles)) for s in t_splits]
    return pl.pallas_call(
        functools.partial(_norm_matmul_kernel, splits=splits, t_splits=tuple(t_splits)),
        out_shape=out_shape,
        grid=(t // tm,),
        in_specs=in_specs,
        out_specs=out_specs,
        compiler_params=_cparams(("parallel",)),
        name="norm_matmul",
    )(*args)


def _proj_res_kernel(x_ref, a_ref, b_ref, wa_ref, wb_ref, o_ref):
    acc = _dot(a_ref[...].astype(BF16), wa_ref[...]) + _dot(b_ref[...].astype(BF16), wb_ref[...])
    o_ref[...] = x_ref[...] + acc


def proj_residual(x, a, b, wa, wb):
    t, d = x.shape
    tm = _row_tile(t)
    ka, kb = a.shape[1], b.shape[1]
    return pl.pallas_call(
        _proj_res_kernel,
        out_shape=jax.ShapeDtypeStruct((t, d), F32),
        grid=(t // tm,),
        in_specs=[pl.BlockSpec((tm, d), lambda i: (i, 0)),
                  pl.BlockSpec((tm, ka), lambda i: (i, 0)),
                  pl.BlockSpec((tm, kb), lambda i: (i, 0)),
                  pl.BlockSpec((ka, d), lambda i: (0, 0)),
                  pl.BlockSpec((kb, d), lambda i: (0, 0))],
        out_specs=pl.BlockSpec((tm, d), lambda i: (i, 0)),
        compiler_params=_cparams(("parallel",)),
        name="proj_residual",
    )(x, a, b, wa, wb)


def _rmsnorm_kernel(x_ref, g_ref, o_ref):
    x = x_ref[...]
    ms = jnp.mean(x * x, axis=-1, keepdims=True)
    o_ref[...] = x * lax.rsqrt(ms + NORM_EPS) * g_ref[...]


def rmsnorm_call(x, g):
    t, d = x.shape
    tm = _row_tile(t)
    return pl.pallas_call(
        _rmsnorm_kernel,
        out_shape=jax.ShapeDtypeStruct((t, d), F32),
        grid=(t // tm,),
        in_specs=[pl.BlockSpec((tm, d), lambda i: (i, 0)), pl.BlockSpec((1, d), lambda i: (0, 0))],
        out_specs=pl.BlockSpec((tm, d), lambda i: (i, 0)),
        compiler_params=_cparams(("parallel",)),
        name="final_rmsnorm",
    )(x, g.reshape(1, d))


ROUTER_LANES = LANES


def _route(xn, rw, rb, lane, n_exp):
    logits = _dot_exact_rhs(xn, rw, 2) + rb
    lg = jnp.where(lane < N_GROUPS, logits, -jnp.inf)
    gm = jnp.max(lg, axis=-1, keepdims=True)
    top_pg = 1.0 / jnp.sum(jnp.exp(lg - gm), axis=-1, keepdims=True)
    grp = jnp.min(jnp.where(lg == gm, lane, ROUTER_LANES), axis=-1, keepdims=True)
    in_grp = (lane >= N_GROUPS) & (lane < N_GROUPS + n_exp) & (((lane - N_GROUPS) // EXP_PER_GROUP) == grp)
    le = jnp.where(in_grp, logits, -jnp.inf)
    t1 = jnp.max(le, axis=-1, keepdims=True)
    i1 = jnp.min(jnp.where(le == t1, lane, ROUTER_LANES), axis=-1, keepdims=True)
    le2 = jnp.where(lane == i1, -jnp.inf, le)
    t2 = jnp.max(le2, axis=-1, keepdims=True)
    i2 = jnp.min(jnp.where(le2 == t2, lane, ROUTER_LANES), axis=-1, keepdims=True)
    ex = jnp.exp(t2 - t1)
    w1 = 1.0 / (1.0 + ex)
    return i1, i2, w1 * top_pg, (ex * w1) * top_pg


def _moe_kernel(x_ref, g_ref, rw_ref, rb_ref, wg_ref, wu_ref, wd_ref, o_ref, xn_scr, gate_scr, acc_scr, *, n_exp):
    e = pl.program_id(1)
    tm = x_ref.shape[0]
    lane = lax.broadcasted_iota(jnp.int32, (tm, ROUTER_LANES), 1)

    @pl.when(e == 0)
    def _():
        x = x_ref[...]
        ms = jnp.mean(x * x, axis=-1, keepdims=True)
        xn = x * lax.rsqrt(ms + NORM_EPS) * g_ref[...]
        xn_scr[...] = xn.astype(BF16)
        i1, i2, g1, g2 = _route(xn, rw_ref[...], rb_ref[...], lane, n_exp)
        gate_scr[...] = jnp.where(lane == i1, g1, 0.0) + jnp.where(lane == i2, g2, 0.0)
        acc_scr[...] = jnp.zeros_like(acc_scr)

    xn = xn_scr[...]
    hg = _dot(xn, wg_ref[0, 0])
    hu = _dot(xn, wu_ref[0, 0])
    gcol = jnp.sum(jnp.where(lane == e + N_GROUPS, gate_scr[...], 0.0), axis=-1, keepdims=True)
    hid = hg * _sigmoid(hg) * hu * gcol
    acc_scr[...] += _dot(hid.astype(BF16), wd_ref[0, 0])

    @pl.when(e == n_exp - 1)
    def _():
        o_ref[...] = x_ref[...] + acc_scr[...]


def moe_layer(x, g, rw, rb, wg, wu, wd, layer):
    t, d = x.shape
    _, n_exp, _, f = wg.shape
    tm = _row_tile(t)
    return pl.pallas_call(
        functools.partial(_moe_kernel, n_exp=n_exp),
        out_shape=jax.ShapeDtypeStruct((t, d), F32),
        grid=(t // tm, n_exp),
        in_specs=[pl.BlockSpec((tm, d), lambda i, e: (i, 0)),
                  pl.BlockSpec((1, d), lambda i, e: (0, 0)),
                  pl.BlockSpec((d, ROUTER_LANES), lambda i, e: (0, 0)),
                  pl.BlockSpec((1, ROUTER_LANES), lambda i, e: (0, 0)),
                  pl.BlockSpec((1, 1, d, f), lambda i, e: (layer, e, 0, 0)),
                  pl.BlockSpec((1, 1, d, f), lambda i, e: (layer, e, 0, 0)),
                  pl.BlockSpec((1, 1, f, d), lambda i, e: (layer, e, 0, 0))],
        out_specs=pl.BlockSpec((tm, d), lambda i, e: (i, 0)),
        scratch_shapes=[pltpu.VMEM((tm, d), BF16), pltpu.VMEM((tm, ROUTER_LANES), F32), pltpu.VMEM((tm, d), F32)],
        compiler_params=_cparams(("parallel", "arbitrary")),
        name="hier_moe",
    )(x, g.reshape(1, d), rw, rb, wg, wu, wd)


MOE_ROW_TILE = 512
MOE_COPY_CHUNK = 256
MOE_COMBINE_TILE = 256


def _router_kernel(x_ref, g_ref, rw_ref, rb_ref, tri_ref, gate_ref, info_ref, cnt_ref, base_scr, *, n_exp, n_tiles):
    i = pl.program_id(0)
    tm = x_ref.shape[0]
    lane = lax.broadcasted_iota(jnp.int32, (tm, ROUTER_LANES), 1)

    @pl.when(i == 0)
    def _():
        base_scr[...] = jnp.zeros_like(base_scr)

    x = x_ref[...]
    ms = jnp.mean(x * x, axis=-1, keepdims=True)
    xn = x * lax.rsqrt(ms + NORM_EPS) * g_ref[...]
    i1, i2, g1, g2 = _route(xn, rw_ref[...], rb_ref[...], lane, n_exp)
    chosen = jnp.where((lane == i1) | (lane == i2), 1.0, 0.0)
    before = _dot(tri_ref[...], chosen.astype(BF16)) + base_scr[...]
    r1 = jnp.sum(jnp.where(lane == i1, before, 0.0), axis=-1, keepdims=True)
    r2 = jnp.sum(jnp.where(lane == i2, before, 0.0), axis=-1, keepdims=True)
    base_scr[...] += jnp.sum(chosen, axis=0, keepdims=True)
    gate_ref[...] = jnp.where(lane == 0, g1, 0.0) + jnp.where(lane == 1, g2, 0.0)
    e1 = (i1 - N_GROUPS).astype(F32)
    e2 = (i2 - N_GROUPS).astype(F32)
    info_ref[...] = (jnp.where(lane == 0, e1, 0.0) + jnp.where(lane == 1, e2, 0.0)
                     + jnp.where(lane == 2, r1, 0.0) + jnp.where(lane == 3, r2, 0.0))

    @pl.when(i == n_tiles - 1)
    def _():
        cnt_ref[...] = base_scr[...]


def _scatter_rows_kernel(pos0_ref, pos1_ref, x_ref, xs_in_hbm, xs_hbm, stage, sem, *, CH, n_chunks):
    del xs_in_hbm
    c = pl.program_id(0)
    slot = c % 2
    x = x_ref[...]
    for j in range(SUBLANES):
        stage[slot, pl.ds(j, CH, stride=SUBLANES), :] = x[:, j * LANES:(j + 1) * LANES]

    def body(r, carry):
        t = c * CH + r
        src = stage.at[slot, pl.ds(pl.multiple_of(r * SUBLANES, SUBLANES), SUBLANES), :]
        d0 = pl.multiple_of(pos0_ref[t], SUBLANES)
        d1 = pl.multiple_of(pos1_ref[t], SUBLANES)
        pltpu.make_async_copy(src, xs_hbm.at[pl.ds(d0, SUBLANES), :], sem.at[slot]).start(priority=0)
        pltpu.make_async_copy(src, xs_hbm.at[pl.ds(d1, SUBLANES), :], sem.at[slot]).start(priority=1)
        return carry

    lax.fori_loop(0, CH, body, 0, unroll=8)

    def drain(s):
        pltpu.make_async_copy(stage.at[s], xs_hbm.at[pl.ds(0, CH * SUBLANES), :], sem.at[s]).wait()
        pltpu.make_async_copy(stage.at[s], xs_hbm.at[pl.ds(0, CH * SUBLANES), :], sem.at[s]).wait()

    @pl.when(c > 0)
    def _():
        drain(1 - slot)

    @pl.when(c == n_chunks - 1)
    def _():
        drain(slot)


def _tile_rows_to_matrix(ref, lead, n_rows):
    return jnp.concatenate([ref[lead + (pl.ds(j, n_rows, stride=SUBLANES), slice(None))] for j in range(SUBLANES)],
                           axis=-1)


def _expert_kernel(te_ref, nv_ref, xs_ref, g_ref, wg_ref, wu_ref, wd_ref, y_ref, *, TM):
    @pl.when(pl.program_id(0) < nv_ref[0])
    def _():
        x = _tile_rows_to_matrix(xs_ref, (), TM)
        ms = jnp.mean(x * x, axis=-1, keepdims=True)
        xn = (x * lax.rsqrt(ms + NORM_EPS) * g_ref[...]).astype(BF16)
        hg = _dot(xn, wg_ref[0, 0])
        hu = _dot(xn, wu_ref[0, 0])
        hid = hg * _sigmoid(hg) * hu
        y = _dot(hid.astype(BF16), wd_ref[0, 0])
        for j in range(SUBLANES):
            y_ref[pl.ds(j, TM, stride=SUBLANES), :] = y[:, j * LANES:(j + 1) * LANES]

    @pl.when(pl.program_id(0) >= nv_ref[0])
    def _():
        y_ref[...] = jnp.zeros_like(y_ref)


def _combine_kernel(pos0_ref, pos1_ref, x_ref, gate_ref, fg_ref, y_hbm, o_ref, ybuf, sem, *, TC, n_tiles, final_norm):
    i = pl.program_id(0)

    def issue(tile, slot):
        def body(r, carry):
            t = tile * TC + r
            dst = pl.ds(pl.multiple_of(r * SUBLANES, SUBLANES), SUBLANES)
            s0 = pl.multiple_of(pos0_ref[t], SUBLANES)
            s1 = pl.multiple_of(pos1_ref[t], SUBLANES)
            pltpu.make_async_copy(y_hbm.at[pl.ds(s0, SUBLANES), :], ybuf.at[slot, 0, dst, :],
                                  sem.at[slot]).start(priority=0)
            pltpu.make_async_copy(y_hbm.at[pl.ds(s1, SUBLANES), :], ybuf.at[slot, 1, dst, :],
                                  sem.at[slot]).start(priority=1)
            return carry
        lax.fori_loop(0, TC, body, 0, unroll=8)

    @pl.when(i == 0)
    def _():
        issue(0, 0)

    @pl.when(i + 1 < n_tiles)
    def _():
        issue(i + 1, (i + 1) % 2)

    slot = i % 2
    pltpu.make_async_copy(y_hbm.at[pl.ds(0, TC * SUBLANES), :], ybuf.at[slot, 0], sem.at[slot]).wait()
    pltpu.make_async_copy(y_hbm.at[pl.ds(0, TC * SUBLANES), :], ybuf.at[slot, 1], sem.at[slot]).wait()
    gate = gate_ref[...]
    y0 = _tile_rows_to_matrix(ybuf, (slot, 0), TC)
    y1 = _tile_rows_to_matrix(ybuf, (slot, 1), TC)
    out = x_ref[...] + gate[:, 0:1] * y0 + gate[:, 1:2] * y1
    if final_norm:
        ms = jnp.mean(out * out, axis=-1, keepdims=True)
        out = out * lax.rsqrt(ms + NORM_EPS) * fg_ref[...]
    o_ref[...] = out


def moe_layer_sparse(x, g, rw, rb, wg, wu, wd, layer, final_g=None, sorted_buf=None):
    t, d = x.shape
    _, n_exp, _, f = wg.shape
    TM = MOE_ROW_TILE
    n_tiles = t // TM
    tri = jnp.asarray(np.tril(np.ones((TM, TM), np.float32), -1)).astype(BF16)
    gate, info, cnt = pl.pallas_call(
        functools.partial(_router_kernel, n_exp=n_exp, n_tiles=n_tiles),
        out_shape=[jax.ShapeDtypeStruct((t, ROUTER_LANES), F32), jax.ShapeDtypeStruct((t, ROUTER_LANES), F32),
                   jax.ShapeDtypeStruct((1, ROUTER_LANES), F32)],
        grid=(n_tiles,),
        in_specs=[pl.BlockSpec((TM, d), lambda i: (i, 0)),
                  pl.BlockSpec((1, d), lambda i: (0, 0)),
                  pl.BlockSpec((d, ROUTER_LANES), lambda i: (0, 0)),
                  pl.BlockSpec((1, ROUTER_LANES), lambda i: (0, 0)),
                  pl.BlockSpec((TM, TM), lambda i: (0, 0))],
        out_specs=[pl.BlockSpec((TM, ROUTER_LANES), lambda i: (i, 0)),
                   pl.BlockSpec((TM, ROUTER_LANES), lambda i: (i, 0)),
                   pl.BlockSpec((1, ROUTER_LANES), lambda i: (0, 0))],
        scratch_shapes=[pltpu.VMEM((1, ROUTER_LANES), F32)],
        compiler_params=_cparams(("arbitrary",)),
        name="moe_router",
    )(x, g.reshape(1, d), rw, rb, tri)

    counts = cnt[0, N_GROUPS:N_GROUPS + n_exp].astype(jnp.int32)
    padded = ((counts + TM - 1) // TM) * TM
    ends = jnp.cumsum(padded)
    offs = ends - padded
    eid = info[:, 0:2].astype(jnp.int32)
    rank = info[:, 2:4].astype(jnp.int32)
    pos = jnp.sum(jnp.where(eid[:, :, None] == jnp.arange(n_exp)[None, None, :], offs[None, None, :], 0), axis=-1) + rank
    assert d == SUBLANES * LANES, "a token row must fill exactly one (8, 128) tile"
    pos = pos * SUBLANES
    pos0, pos1 = pos[:, 0], pos[:, 1]
    max_tiles = (2 * t) // TM + n_exp
    n_valid = (ends[-1] // TM).astype(jnp.int32).reshape(1)
    tile_exp = jnp.minimum(jnp.sum((ends[None, :] // TM) <= jnp.arange(max_tiles)[:, None], axis=-1),
                           n_exp - 1).astype(jnp.int32)
    p_rows = max_tiles * TM

    CH = MOE_COPY_CHUNK
    xs = pl.pallas_call(
        functools.partial(_scatter_rows_kernel, CH=CH, n_chunks=t // CH),
        out_shape=jax.ShapeDtypeStruct((p_rows * SUBLANES, LANES), F32),
        grid_spec=pltpu.PrefetchScalarGridSpec(
            num_scalar_prefetch=2, grid=(t // CH,),
            in_specs=[pl.BlockSpec((CH, d), lambda c, p0, p1: (c, 0)), pl.BlockSpec(memory_space=pl.ANY)],
            out_specs=pl.BlockSpec(memory_space=pl.ANY),
            scratch_shapes=[pltpu.VMEM((2, CH * SUBLANES, LANES), F32), pltpu.SemaphoreType.DMA((2,))]),
        input_output_aliases={3: 0},
        compiler_params=pltpu.CompilerParams(dimension_semantics=("arbitrary",), vmem_limit_bytes=VMEM_LIMIT,
                                             has_side_effects=True),
        name="moe_scatter_rows",
    )(pos0, pos1, x, jnp.zeros((p_rows * SUBLANES, LANES), F32) if sorted_buf is None else sorted_buf)

    def row_idx(i, te, nv):
        return (jnp.minimum(i, nv[0] - 1), 0)

    ys = pl.pallas_call(
        functools.partial(_expert_kernel, TM=TM),
        out_shape=jax.ShapeDtypeStruct((p_rows * SUBLANES, LANES), F32),
        grid_spec=pltpu.PrefetchScalarGridSpec(
            num_scalar_prefetch=2, grid=(max_tiles,),
            in_specs=[pl.BlockSpec((TM * SUBLANES, LANES), row_idx),
                      pl.BlockSpec((1, d), lambda i, te, nv: (0, 0)),
                      pl.BlockSpec((1, 1, d, f), lambda i, te, nv: (layer, te[i], 0, 0)),
                      pl.BlockSpec((1, 1, d, f), lambda i, te, nv: (layer, te[i], 0, 0)),
                      pl.BlockSpec((1, 1, f, d), lambda i, te, nv: (layer, te[i], 0, 0))],
            out_specs=pl.BlockSpec((TM * SUBLANES, LANES), lambda i, te, nv: (i, 0))),
        compiler_params=_cparams(("arbitrary",)),
        name="moe_experts",
    )(tile_exp, n_valid, xs, g.reshape(1, d), wg, wu, wd)

    TC = MOE_COMBINE_TILE
    out = pl.pallas_call(
        functools.partial(_combine_kernel, TC=TC, n_tiles=t // TC, final_norm=final_g is not None),
        out_shape=jax.ShapeDtypeStruct((t, d), F32),
        grid_spec=pltpu.PrefetchScalarGridSpec(
            num_scalar_prefetch=2, grid=(t // TC,),
            in_specs=[pl.BlockSpec((TC, d), lambda i, p0, p1: (i, 0)),
                      pl.BlockSpec((TC, ROUTER_LANES), lambda i, p0, p1: (i, 0)),
                      pl.BlockSpec((1, d), lambda i, p0, p1: (0, 0)),
                      pl.BlockSpec(memory_space=pl.ANY)],
            out_specs=pl.BlockSpec((TC, d), lambda i, p0, p1: (i, 0)),
            scratch_shapes=[pltpu.VMEM((2, 2, TC * SUBLANES, LANES), F32), pltpu.SemaphoreType.DMA((2,))]),
        compiler_params=_cparams(("arbitrary",)),
        name="moe_combine",
    )(pos0, pos1, x, gate, (g if final_g is None else final_g).reshape(1, d), ys)
    return out, xs


def _rwkv_kernel(p_ref, prev_ref, s0_ref, mu_ref, w0_ref, wd_ref, a0_ref, wa_ref, wg_ref, kk_ref, ka_ref,
                 bonus_ref, lng_ref, lnb_ref, tri_ref, hsum_ref, ya_ref, sf_ref, s_scr, prev_scr,
                 *, NB, C, H, DH, n_chunks):
    c = pl.program_id(1)

    @pl.when(c == 0)
    def _():
        s_scr[...] = s0_ref[:, 0]
        prev_scr[...] = prev_ref[:, 0]

    DA = H * DH
    R = NB * C
    p = p_ref[...].reshape(R, p_ref.shape[-1])
    row = lax.broadcasted_iota(jnp.int32, p.shape, 0)
    shifted = pltpu.roll(p, 1, axis=0)
    for n in range(NB):
        shifted = jnp.where(row == n * C, prev_scr[n], shifted)
        prev_scr[n] = p[(n + 1) * C - 1:(n + 1) * C, :]
    xs = p + (shifted - p) * mu_ref[...]
    r = xs[:, 0:DA]
    k = xs[:, DA:2 * DA]
    v = xs[:, 2 * DA:3 * DA]
    lora = xs[:, 3 * DA:3 * DA + R_DECAY + R_ICLR]
    gd = xs[:, 3 * DA + R_DECAY + R_ICLR:3 * DA + R_DECAY + R_ICLR + R_GATE]

    w_log = -_softplus(-(w0_ref[...] + _dot(jnp.tanh(lora).astype(BF16), wd_ref[...]))) - 0.5
    lw = -jnp.exp(w_log)
    a = _sigmoid(a0_ref[...] + _dot(lora.astype(BF16), wa_ref[...]))
    g = _dot(_sigmoid(gd).astype(BF16), wg_ref[...])

    kk = k * kk_ref[...]
    ss = _dot_exact_rhs(kk * kk, hsum_ref[...], 2)
    kk = kk / jnp.maximum(jnp.sqrt(ss), 1e-12)
    k2 = k * (1.0 + (a - 1.0) * ka_ref[...])
    kka = kk * a

    cum = jnp.concatenate([_dot_exact_lhs(tri_ref[...], lw[n * C:(n + 1) * C], 3) for n in range(NB)], axis=0)
    p_in = jnp.exp(cum)
    r_t = r * p_in
    a_t = kk * jnp.exp(cum - lw)
    p_inv = jnp.exp(-cum)
    b_t = kka * p_inv
    k_t = k2 * p_inv
    bonus = _dot_exact_rhs(r * k2 * bonus_ref[...], hsum_ref[...], 2) * v

    ri = lax.broadcasted_iota(jnp.int32, (C, C), 0)
    ci = lax.broadcasted_iota(jnp.int32, (C, C), 1)
    strict = ri > ci
    incl = ri >= ci
    eye = (ri == ci).astype(F32)
    n_double = max(int(math.ceil(math.log2(C))) - 1, 0)

    chains = [(n, h) for n in range(NB) for h in range(H)]

    def blk(x, n, h):
        return x[n * C:(n + 1) * C, h * DH:(h + 1) * DH]

    def bf(x):
        return x.astype(BF16)

    Bt = [bf(blk(b_t, n, h)) for n, h in chains]
    Kt = [bf(blk(k_t, n, h)) for n, h in chains]
    Vf = [blk(v, n, h) for n, h in chains]
    AR = [bf(jnp.concatenate([blk(a_t, n, h), blk(r_t, n, h)], axis=0)) for n, h in chains]
    S0 = [s_scr[n, h] for n, h in chains]
    idx = range(len(chains))
    GB = [_dot_nt(AR[i], Bt[i]) for i in idx]
    GK = [_dot_nt(AR[i], Kt[i]) for i in idx]
    ARS = [_dot_nt(AR[i], bf(S0[i])) for i in idx]
    Lm = [jnp.where(strict, GB[i][0:C], 0.0) for i in idx]
    Gb = [bf(jnp.where(incl, GB[i][C:2 * C], 0.0)) for i in idx]
    MG = [bf(jnp.concatenate([jnp.where(strict, GK[i][0:C], 0.0), jnp.where(incl, GK[i][C:2 * C], 0.0)], axis=0))
          for i in idx]
    MGV = [_dot(MG[i], bf(Vf[i])) for i in idx]
    T = [eye - Lm[i] for i in idx]
    Pw = [bf(Lm[i]) for i in idx]
    for _ in range(n_double):
        Pw = [bf(_dot(Pw[i], Pw[i])) for i in idx]
        T = [T[i] + _dot(bf(T[i]), Pw[i]) for i in idx]
    U = [_dot(bf(T[i]), bf(-(ARS[i][0:C] + MGV[i][0:C]))) for i in idx]
    Y = [ARS[i][C:2 * C] + _dot(Gb[i], bf(U[i])) + MGV[i][C:2 * C] for i in idx]
    for i, (n, h) in enumerate(chains):
        UV = bf(jnp.concatenate([U[i], Vf[i]], axis=0))
        BK = jnp.concatenate([Bt[i], Kt[i]], axis=0)
        p_tot = p_in[(n + 1) * C - 1:(n + 1) * C, h * DH:(h + 1) * DH]
        s_scr[n, h] = (S0[i] + _dot_tn(UV, BK)) * p_tot

    rows = []
    for n in range(NB):
        ys = []
        for h in range(H):
            Yh = Y[n * H + h]
            yc = Yh - jnp.mean(Yh, axis=-1, keepdims=True)
            var = jnp.mean(yc * yc, axis=-1, keepdims=True)
            ys.append(yc * lax.rsqrt(var + GN_EPS))
        rows.append(jnp.concatenate(ys, axis=-1))
    y = jnp.concatenate(rows, axis=0) * lng_ref[...] + lnb_ref[...]
    ya_ref[...] = ((y + bonus) * g).reshape(NB, C, DA)

    @pl.when(c == n_chunks - 1)
    def _():
        sf_ref[:, 0] = s_scr[...]


def rwkv_mix(pa, n_batch, seq, shift_prev, wkv0, wts, n_par):
    t, ap = pa.shape
    H = wkv0.shape[1]
    DA = H * DH_A
    C = min(RWKV_CHUNK, seq)
    n_chunks = seq // C
    NB = n_par
    G = n_batch // NB
    tri = jnp.asarray(np.tril(np.ones((C, C), np.float32))).astype(BF16)
    hsum = jnp.asarray(np.kron(np.eye(H, dtype=np.float32), np.ones((DH_A, DH_A), np.float32))).astype(BF16)

    def full(shape):
        nd = len(shape)
        return pl.BlockSpec(shape, lambda b, c: (0,) * nd)

    vec = full((1, DA))
    ya, s_fin = pl.pallas_call(
        functools.partial(_rwkv_kernel, NB=NB, C=C, H=H, DH=DH_A, n_chunks=n_chunks),
        out_shape=[jax.ShapeDtypeStruct((NB, t // NB, DA), F32),
                   jax.ShapeDtypeStruct((NB, G, H, DH_A, DH_A), F32)],
        grid=(G, n_chunks),
        in_specs=[pl.BlockSpec((NB, C, ap), lambda b, c: (0, b * n_chunks + c, 0)),
                  pl.BlockSpec((NB, 1, 1, ap), lambda b, c: (0, b, 0, 0)),
                  pl.BlockSpec((NB, 1, H, DH_A, DH_A), lambda b, c: (0, b, 0, 0, 0)),
                  full((1, ap)), vec, full((R_DECAY + R_ICLR, DA)), vec, full((R_DECAY + R_ICLR, DA)),
                  full((R_GATE, DA)), vec, vec, vec, vec, vec, full((C, C)), full((DA, DA))],
        out_specs=[pl.BlockSpec((NB, C, DA), lambda b, c: (0, b * n_chunks + c, 0)),
                   pl.BlockSpec((NB, 1, H, DH_A, DH_A), lambda b, c: (0, b, 0, 0, 0))],
        scratch_shapes=[pltpu.VMEM((NB, H, DH_A, DH_A), F32), pltpu.VMEM((NB, 1, ap), F32)],
        compiler_params=_cparams(("parallel", "arbitrary")),
        name="rwkv7_mix",
    )(pa.reshape(NB, t // NB, ap), shift_prev.reshape(NB, G, 1, ap), wkv0.reshape(NB, G, H, DH_A, DH_A),
      wts["mu"], wts["w0"], wts["wd"], wts["a0"], wts["wa"],
      wts["wg"], wts["key_k"], wts["key_a"], wts["bonus"], wts["lnx_g"], wts["lnx_b"], tri, hsum)
    return ya.reshape(t, DA), s_fin.reshape(wkv0.shape)


GMLP_TILE = 128
GMLP_ROWS = 512


def _gmlp_kernel(u_ref, v_ref, ng_ref, nb_ref, wm_ref, bias_ref, o_ref, *vn_refs, n_sub):
    vf = _gelu(v_ref[...])
    mu = jnp.mean(vf, axis=-1, keepdims=True)
    vc = vf - mu
    var = jnp.mean(vc * vc, axis=-1, keepdims=True)
    vn = vc * lax.rsqrt(var + NORM_EPS) * ng_ref[...] + nb_ref[...]
    for vn_ref in vn_refs:
        vn_ref[...] = vn
    vb = vn.astype(BF16)
    n_h = wm_ref.shape[0]
    cb = vn.shape[1] // n_h
    gu = _gelu(u_ref[...])
    for c in range(n_sub):
        rows = slice(c * GMLP_TILE, (c + 1) * GMLP_TILE)
        s = jnp.concatenate([_dot(wm_ref[h], vb[rows, h * cb:(h + 1) * cb]) for h in range(n_h)], axis=-1)
        o_ref[rows, :] = gu[rows, :] * (s + bias_ref[...])


def gmlp_mix(pu, pv, ng, nb, wm_bf16, bias_tile, want_vn):
    t, db = pu.shape
    n_h = wm_bf16.shape[0]
    rows = GMLP_ROWS if t % GMLP_ROWS == 0 else t
    n_out = 2 if want_vn else 1
    outs = pl.pallas_call(
        functools.partial(_gmlp_kernel, n_sub=rows // GMLP_TILE),
        out_shape=[jax.ShapeDtypeStruct((t, db), F32)] * n_out,
        grid=(t // rows,),
        in_specs=[pl.BlockSpec((rows, db), lambda i: (i, 0)),
                  pl.BlockSpec((rows, db), lambda i: (i, 0)),
                  pl.BlockSpec((1, db), lambda i: (0, 0)),
                  pl.BlockSpec((1, db), lambda i: (0, 0)),
                  pl.BlockSpec((n_h, GMLP_TILE, GMLP_TILE), lambda i: (0, 0, 0)),
                  pl.BlockSpec((GMLP_TILE, db), lambda i: (0, 0))],
        out_specs=[pl.BlockSpec((rows, db), lambda i: (i, 0))] * n_out,
        compiler_params=_cparams(("parallel",)),
        name="gmlp_mix",
    )(pu, pv, ng, nb, wm_bf16, bias_tile)
    return (outs[0], outs[1]) if want_vn else (outs[0], None)


N_SEG = 8
SEG_GAP = 4


def _seg_pitch(seg):
    return seg + SEG_GAP if seg % SUBLANES == 0 else seg


def _rglru_kernel(xb_ref, gy_ref, cprev_ref, h0_ref, cw_ref, cb_ref, gw_ref, gb_ref, lam_ref,
                  yc_ref, ctail_ref, hl_ref, xe_scr, a_scr, b_scr, h_scr, *, TL, DC, pos0, n_tiles):
    l = pl.program_id(1)
    PAD = SUBLANES

    @pl.when(l == 0)
    def _():
        xe_scr[0:PAD, :] = cprev_ref[0]
        h_scr[...] = h0_ref[0]

    xe_scr[PAD:PAD + TL, :] = xb_ref[...]
    xc = cb_ref[...] + xe_scr[pl.ds(PAD - (CONV_W - 1), TL), :] * cw_ref[0:1, :]
    for i in range(1, CONV_W):
        xc = xc + xe_scr[pl.ds(PAD - (CONV_W - 1) + i, TL), :] * cw_ref[i:i + 1, :]
    tail = xe_scr[TL:TL + PAD, :]
    ctail_ref[0] = tail
    xe_scr[0:PAD, :] = tail

    gates = _dot(xc.astype(BF16), gw_ref[...]) + gb_ref[...]
    rg = _sigmoid(gates[:, 0:DC])
    ig = _sigmoid(gates[:, DC:2 * DC])
    log_a = -LRU_C * rg * _softplus(-lam_ref[...])
    a = jnp.exp(log_a)
    mult = jnp.sqrt(1.0 - a * a)
    row = lax.broadcasted_iota(jnp.int32, (TL, DC), 0)
    mult = jnp.where(row + (l * TL + pos0) == 0, 1.0, mult)
    b = mult * ig * xc
    n_slab = DC // LANES
    seg = TL // N_SEG
    pitch = _seg_pitch(seg)
    for s in range(n_slab):
        for j in range(N_SEG):
            a_scr[s, pl.ds(j * pitch, seg), :] = a[j * seg:(j + 1) * seg, s * LANES:(s + 1) * LANES]
            b_scr[s, pl.ds(j * pitch, seg), :] = b[j * seg:(j + 1) * seg, s * LANES:(s + 1) * LANES]

    def step(i, carry):
        idx = pl.ds(i, N_SEG, stride=pitch) if pitch > 1 else pl.ds(0, N_SEG)
        out = []
        for s in range(n_slab):
            hloc, ap = carry[s]
            ai = a_scr[s, idx, :]
            hloc = ai * hloc + b_scr[s, idx, :]
            ap = ap * ai
            b_scr[s, idx, :] = hloc
            a_scr[s, idx, :] = ap
            out.append((hloc, ap))
        return tuple(out)

    lax.fori_loop(0, seg, step,
                  tuple((jnp.zeros((N_SEG, LANES), F32), jnp.ones((N_SEG, LANES), F32)) for _ in range(n_slab)),
                  unroll=min(seg, 8))

    carry = h_scr[...]
    g_act = _gelu(gy_ref[...])
    for j in range(N_SEG):
        rows = slice(j * seg, (j + 1) * seg)
        rows_p = pl.ds(j * pitch, seg)
        hloc = jnp.concatenate([b_scr[s, rows_p, :] for s in range(n_slab)], axis=-1)
        ap = jnp.concatenate([a_scr[s, rows_p, :] for s in range(n_slab)], axis=-1)
        hj = hloc + ap * carry
        yc_ref[rows, :] = g_act[rows, :] * hj
        carry = hj[seg - 1:seg, :]
    h_scr[...] = carry

    @pl.when(l == n_tiles - 1)
    def _():
        hl_ref[0] = carry


def rglru_mix(xb, gy, n_batch, seq, conv_prev8, h0, pos0, wts):
    t, dc = xb.shape
    TL = 512 if seq % 512 == 0 else seq
    n_tiles = seq // TL
    scan_rows = N_SEG * _seg_pitch(TL // N_SEG)

    def full(shape):
        nd = len(shape)
        return pl.BlockSpec(shape, lambda b, l: (0,) * nd)

    yc, ctail, hl = pl.pallas_call(
        functools.partial(_rglru_kernel, TL=TL, DC=dc, pos0=pos0, n_tiles=n_tiles),
        out_shape=[jax.ShapeDtypeStruct((t, dc), F32), jax.ShapeDtypeStruct((n_batch, SUBLANES, dc), F32),
                   jax.ShapeDtypeStruct((n_batch, 1, dc), F32)],
        grid=(n_batch, n_tiles),
        in_specs=[pl.BlockSpec((TL, dc), lambda b, l: (b * n_tiles + l, 0)),
                  pl.BlockSpec((TL, dc), lambda b, l: (b * n_tiles + l, 0)),
                  pl.BlockSpec((1, SUBLANES, dc), lambda b, l: (b, 0, 0)),
                  pl.BlockSpec((1, 1, dc), lambda b, l: (b, 0, 0)),
                  full((CONV_W, dc)), full((1, dc)), full((dc, 2 * dc)), full((1, 2 * dc)), full((1, dc))],
        out_specs=[pl.BlockSpec((TL, dc), lambda b, l: (b * n_tiles + l, 0)),
                   pl.BlockSpec((1, SUBLANES, dc), lambda b, l: (b, 0, 0)),
                   pl.BlockSpec((1, 1, dc), lambda b, l: (b, 0, 0))],
        scratch_shapes=[pltpu.VMEM((TL + SUBLANES, dc), F32), pltpu.VMEM((dc // LANES, scan_rows, LANES), F32),
                        pltpu.VMEM((dc // LANES, scan_rows, LANES), F32), pltpu.VMEM((1, dc), F32)],
        compiler_params=_cparams(("parallel", "arbitrary")),
        name="rglru_mix",
    )(xb, gy, conv_prev8, h0.reshape(n_batch, 1, dc), wts["conv_w"], wts["conv_b"], wts["gate_w"], wts["gate_b"],
      wts["lam"])
    return yc, ctail[:, SUBLANES - (CONV_W - 1):, :], hl.reshape(n_batch, dc)


def _t5_bucket(dist):
    dist = np.asarray(dist)
    max_exact = N_BUCKETS // 2
    scaled = np.log(np.maximum(dist, 1) / max_exact) / math.log(BUCKET_MAX_DIST / max_exact)
    large = np.minimum(max_exact + (scaled * (N_BUCKETS - max_exact)).astype(np.int32), N_BUCKETS - 1)
    return np.where(dist < max_exact, dist, large).astype(np.int32)


def _dist_table(rel_bias, max_dist):
    dist = np.arange(max_dist + 1)
    count = np.zeros(max_dist + 1, np.float32)
    for window, dil in DILATED:
        count += ((dist % dil == 0) & (dist <= window)).astype(np.float32)
    logcnt = np.where(count > 0, np.log(np.maximum(count, 1.0)), 0.0).astype(np.float32)
    tab = jnp.take(rel_bias, jnp.asarray(_t5_bucket(dist)), axis=0) + jnp.asarray(logcnt)[:, None]
    return jnp.where(jnp.asarray(count > 0)[:, None], tab, NEG_BIG)


def _toeplitz_tiles(tab, n_pos, n_neg, T):
    D, H = tab.shape
    span = T * n_pos
    assert D >= span
    n_col = span + T * n_neg + T - 1
    ext = jnp.concatenate([jnp.flip(tab[:span], axis=0), jnp.full((n_col + 1 - span, H), NEG_BIG, F32)], axis=0)
    ext = jnp.transpose(ext)
    skew = jnp.tile(ext, (1, T))[:, :T * n_col].reshape(H, T, n_col)
    tiles = [skew[:, :, span - 1 - T * dd: span - 1 - T * dd + T] for dd in range(-n_neg, n_pos)]
    return jnp.stack(tiles, axis=1)


def _attn_prompt_kernel(q_ref, k_ref, v_ref, bias_ref, o_ref, kb_scr, vb_scr, *, E, SUB, NS):
    qi = pl.program_id(2)
    TQ = NS * SUB

    @pl.when(qi == 0)
    def _():
        kb_scr[...] = k_ref[0].astype(BF16)
        vb_scr[...] = v_ref[0].astype(BF16)

    lane = lax.broadcasted_iota(jnp.int32, (SUB, 2 * E), 1)
    q2 = []
    for rs in range(NS):
        q = q_ref[0, rs * SUB:(rs + 1) * SUB, :] * (E ** -0.5 * LOG2E)
        q2.append(jnp.concatenate([jnp.where(lane < E, q, 0.0), jnp.where(lane >= E, q, 0.0)], axis=0).astype(BF16))

    def block(i, carry, diagonal):
        j = qi - i
        koff = pl.multiple_of(j * TQ, TQ)
        out = []
        for rs in range(NS):
            n_cs = rs + 1 if diagonal else NS
            kj = kb_scr[:, pl.ds(koff, n_cs * SUB)]
            vj = vb_scr[:, pl.ds(koff, n_cs * SUB)]
            m, l, acc = carry[rs]
            s = _dot(q2[rs], kj)
            parts = []
            for cs in range(n_cs):
                dd = i * NS + (rs - cs + NS - 1)
                bias = jnp.concatenate([bias_ref[0, dd], bias_ref[1, dd]], axis=0)
                parts.append(s[:, cs * SUB:(cs + 1) * SUB] + bias)
            mx = parts[0]
            for part in parts[1:]:
                mx = jnp.maximum(mx, part)
            m_new = jnp.maximum(m, jnp.max(mx, axis=-1, keepdims=True))
            alpha = jnp.exp2(m - m_new)
            ps = [jnp.exp2(part - m_new) for part in parts]
            psum = ps[0]
            for pexp in ps[1:]:
                psum = psum + pexp
            l = alpha * l + psum
            acc = alpha * acc + _dot_nt(jnp.concatenate(ps, axis=-1).astype(BF16), vj)
            out.append((m_new, l, acc))
        return tuple(out)

    init = tuple((jnp.full((2 * SUB, SUB), NEG_BIG, F32), jnp.zeros((2 * SUB, SUB), F32),
                  jnp.zeros((2 * SUB, 2 * E), F32)) for _ in range(NS))
    first = block(0, init, True)
    res = lax.fori_loop(1, qi + 1, lambda i, carry: block(i, carry, False), first)
    for rs in range(NS):
        m, l, acc = res[rs]
        o = acc / jnp.sum(l, axis=-1, keepdims=True)
        o_ref[0, rs * SUB:(rs + 1) * SUB, :] = jnp.where(lane < E, o[0:SUB], o[SUB:2 * SUB])


def attn_prompt(q, k, v, bias_tiles, n_batch, seq):
    hd = q.shape[-1]
    E = hd // H_D
    SUB = ATT_TILE
    NS = ATT_SUBTILES
    TQ = SUB * NS
    nq = seq // TQ
    nt = bias_tiles.shape[1]
    return pl.pallas_call(
        functools.partial(_attn_prompt_kernel, E=E, SUB=SUB, NS=NS),
        out_shape=jax.ShapeDtypeStruct((n_batch, seq, hd), F32),
        grid=(H_D // 2, n_batch, nq),
        in_specs=[pl.BlockSpec((1, TQ, 2 * E), lambda hp, b, i: (b, i, hp)),
                  pl.BlockSpec((1, 2 * E, seq), lambda hp, b, i: (b, hp, 0)),
                  pl.BlockSpec((1, 2 * E, seq), lambda hp, b, i: (b, hp, 0)),
                  pl.BlockSpec((2, nt, SUB, SUB), lambda hp, b, i: (hp, 0, 0, 0))],
        out_specs=pl.BlockSpec((1, TQ, 2 * E), lambda hp, b, i: (b, i, hp)),
        scratch_shapes=[pltpu.VMEM((2 * E, seq), BF16), pltpu.VMEM((2 * E, seq), BF16)],
        compiler_params=_cparams(("arbitrary", "arbitrary", "arbitrary")),
        name="dilated_attn_prompt",
    )(q, k, v, bias_tiles)


def _attn_sample_kernel(q_ref, kn_ref, vn_ref, ck_ref, cv_ref, bo_ref, bn_ref, o_ref, *, E, S):
    lane = lax.broadcasted_iota(jnp.int32, (S, 2 * E), 1)
    NPAD = bn_ref.shape[-1]
    outs = []
    for hp in range(H_D // 2):
        sl = slice(hp * 2 * E, (hp + 1) * 2 * E)
        q = q_ref[0, :, sl] * (E ** -0.5)
        q2 = jnp.concatenate([jnp.where(lane < E, q, 0.0), jnp.where(lane >= E, q, 0.0)], axis=0).astype(BF16)
        zpad = jnp.zeros((NPAD - S, 2 * E), F32)
        kn = jnp.concatenate([kn_ref[0, :, sl], zpad], axis=0).astype(BF16)
        vn = jnp.concatenate([vn_ref[0, :, sl], zpad], axis=0).astype(BF16)
        s_old = _dot(q2, ck_ref[0, sl, :].astype(BF16)) + jnp.concatenate([bo_ref[2 * hp], bo_ref[2 * hp + 1]], axis=0)
        s_new = _dot_nt(q2, kn) + jnp.concatenate([bn_ref[2 * hp], bn_ref[2 * hp + 1]], axis=0)
        m = jnp.maximum(jnp.max(s_old, axis=-1, keepdims=True), jnp.max(s_new, axis=-1, keepdims=True))
        p_old = jnp.exp(s_old - m)
        p_new = jnp.exp(s_new - m)
        l = jnp.sum(p_old, axis=-1, keepdims=True) + jnp.sum(p_new, axis=-1, keepdims=True)
        acc = _dot_nt(p_old.astype(BF16), cv_ref[0, sl, :].astype(BF16)) + _dot(p_new.astype(BF16), vn)
        o = acc / l
        outs.append(jnp.where(lane < E, o[0:S], o[S:2 * S]))
    o_ref[0] = jnp.concatenate(outs, axis=-1)


def attn_sample(q, k_new, v_new, cache_k, cache_v, bias_old, bias_new):
    n_batch, S, hd = q.shape
    W = cache_k.shape[2]
    E = hd // H_D
    NPAD = bias_new.shape[-1]
    return pl.pallas_call(
        functools.partial(_attn_sample_kernel, E=E, S=S),
        out_shape=jax.ShapeDtypeStruct((n_batch, S, hd), F32),
        grid=(n_batch,),
        in_specs=[pl.BlockSpec((1, S, hd), lambda b: (b, 0, 0)),
                  pl.BlockSpec((1, S, hd), lambda b: (b, 0, 0)),
                  pl.BlockSpec((1, S, hd), lambda b: (b, 0, 0)),
                  pl.BlockSpec((1, hd, W), lambda b: (b, 0, 0)),
                  pl.BlockSpec((1, hd, W), lambda b: (b, 0, 0)),
                  pl.BlockSpec((H_D, S, W), lambda b: (0, 0, 0)),
                  pl.BlockSpec((H_D, S, NPAD), lambda b: (0, 0, 0))],
        out_specs=pl.BlockSpec((1, S, hd), lambda b: (b, 0, 0)),
        compiler_params=_cparams(("parallel",)),
        name="dilated_attn_sample",
    )(q, k_new, v_new, cache_k, cache_v, bias_old, bias_new)


def _even_weights(j, w_in_even, w_out_even, shift_mu, decay_w0, decay_up, iclr_a0, iclr_up, gate_up, key_k, key_a,
                  bonus_r_k, lnx_g, lnx_b, sgu_norm_g, sgu_norm_b, sgu_w, sgu_b):
    da = decay_w0.shape[1]
    zeros_d = jnp.zeros((R_ICLR, da), F32)
    zeros_i = jnp.zeros((R_DECAY, da), F32)
    return dict(
        w_in=w_in_even[j].astype(BF16),
        w_out_a=w_out_even[j, :da].astype(BF16), w_out_b=w_out_even[j, da:].astype(BF16),
        mu=shift_mu[j].reshape(1, -1), w0=decay_w0[j].reshape(1, -1), a0=iclr_a0[j].reshape(1, -1),
        wd=jnp.concatenate([decay_up[j], zeros_d], axis=0).astype(BF16),
        wa=jnp.concatenate([zeros_i, iclr_up[j]], axis=0).astype(BF16), wg=gate_up[j].astype(BF16), key_k=key_k[j].reshape(1, -1), key_a=key_a[j].reshape(1, -1),
        bonus=bonus_r_k[j].reshape(1, -1), lnx_g=lnx_g[j].reshape(1, -1), lnx_b=lnx_b[j].reshape(1, -1),
        ng=sgu_norm_g[j].reshape(1, -1), nb=sgu_norm_b[j].reshape(1, -1), sgu_w=sgu_w[j], sgu_b=sgu_b[j])


def _gmlp_tables(sgu_w, sgu_b, chunk):
    reps = GMLP_TILE // chunk
    n_h = sgu_w.shape[0]
    cb = None
    wm = sgu_w[:, :chunk, :chunk] * jnp.asarray(np.tril(np.ones((chunk, chunk), np.float32)))
    if reps > 1:
        eye = jnp.asarray(np.eye(reps, dtype=np.float32))
        wm = jnp.einsum("ab,hts->hatbs", eye, wm).reshape(n_h, GMLP_TILE, GMLP_TILE)
    bias = jnp.tile(jnp.transpose(sgu_b[:, :chunk]), (reps, 1))
    return wm.astype(BF16), bias


def _even_layer(x, n_batch, seq, chunk, n_par, want_vn, shift_prev, wkv0, norm_g, ew):
    pa, pu, pv = norm_matmul(x, norm_g, ew["w_in"], (ew["mu"].shape[1], ew["ng"].shape[1], ew["ng"].shape[1]))
    ya, wkv = rwkv_mix(pa, n_batch, seq, shift_prev, wkv0, ew, n_par)
    wm, bias = _gmlp_tables(ew["sgu_w"], ew["sgu_b"], chunk)
    cb = pu.shape[1] // wm.shape[0]
    bias_tile = jnp.repeat(bias, cb, axis=1)
    yb, vn = gmlp_mix(pu, pv, ew["ng"], ew["nb"], wm, bias_tile, want_vn)
    x = proj_residual(x, ya, yb, ew["w_out_a"], ew["w_out_b"])
    last = pa.reshape(n_batch, seq, -1)[:, -1]
    return x, last, wkv, vn


def _odd_weights(j, w_in_odd, w_out_odd, conv_w, conv_b, rgate_w, rgate_b, igate_w, igate_b, lru_lambda):
    dc = conv_b.shape[1]
    eye = jnp.asarray(np.eye(H_C, dtype=np.float32))

    def blockdiag(w):
        dh = w.shape[-1]
        return jnp.einsum("ab,aij->aibj", eye, w).reshape(H_C * dh, H_C * dh)

    return dict(
        w_in=w_in_odd[j].astype(BF16), w_kv_t=jnp.transpose(w_in_odd[j, :, 3 * dc:]).astype(BF16),
        w_out_c=w_out_odd[j, :dc].astype(BF16), w_out_d=w_out_odd[j, dc:].astype(BF16),
        conv_w=conv_w[j], conv_b=conv_b[j].reshape(1, -1),
        gate_w=jnp.concatenate([blockdiag(rgate_w[j]), blockdiag(igate_w[j])], axis=1).astype(BF16),
        gate_b=jnp.concatenate([rgate_b[j], igate_b[j]]).reshape(1, -1),
        lam=lru_lambda[j].reshape(1, -1))


def _odd_layer(x, n_batch, seq, conv_prev, h0, pos0, caches, dist_tab, norm_g, ow):
    dc = ow["lam"].shape[1]
    conv_prev8 = jnp.pad(conv_prev, ((0, 0), (SUBLANES - (CONV_W - 1), 0), (0, 0)))
    if caches is None:
        gy, xb, q, k_t, v_t = norm_matmul(x, norm_g, ow["w_in"][:, :3 * dc], (dc,) * 3, ow["w_kv_t"], (dc, dc), seq)
        hd = q.shape[1]
        e = hd // H_D
        yc, conv_last, h_last = rglru_mix(xb, gy, n_batch, seq, conv_prev8, h0, pos0, ow)
        tiles = _toeplitz_tiles(dist_tab * LOG2E, seq // ATT_TILE, ATT_SUBTILES - 1, ATT_TILE)
        o = attn_prompt(q.reshape(n_batch, seq, hd), k_t, v_t, tiles, n_batch, seq)
        k_rows = jnp.transpose(k_t.reshape(n_batch, H_D, e, seq), (0, 3, 1, 2))
        v_rows = jnp.transpose(v_t.reshape(n_batch, H_D, e, seq), (0, 3, 1, 2))
    else:
        gy, xb, q, k, v = norm_matmul(x, norm_g, ow["w_in"], (dc,) * 5)
        hd = q.shape[1]
        e = hd // H_D
        yc, conv_last, h_last = rglru_mix(xb, gy, n_batch, seq, conv_prev8, h0, pos0, ow)
        q3, k3, v3 = (a.reshape(n_batch, seq, hd) for a in (q, k, v))
        cache_k, cache_v = caches
        W = cache_k.shape[1]
        NPAD = LANES
        tab_t = jnp.flip(jnp.transpose(dist_tab[:W + seq]), axis=1)
        b_old = jnp.stack([tab_t[:, seq - 1 - j:seq - 1 - j + W] for j in range(seq)], axis=1)
        d_new = np.arange(seq)[:, None] - np.arange(NPAD)[None, :]
        ok_new = (d_new >= 0) & (np.arange(NPAD)[None, :] < seq)
        b_new = jnp.take(dist_tab, jnp.asarray(np.maximum(d_new, 0)), axis=0)
        b_new = jnp.transpose(jnp.where(jnp.asarray(ok_new)[..., None], b_new, NEG_BIG), (2, 0, 1))
        ck = jnp.transpose(cache_k, (0, 2, 3, 1)).reshape(n_batch, hd, W)
        cv = jnp.transpose(cache_v, (0, 2, 3, 1)).reshape(n_batch, hd, W)
        o = attn_sample(q3, k3, v3, ck, cv, b_old, b_new)
        k_rows = k3.reshape(n_batch, seq, H_D, e)
        v_rows = v3.reshape(n_batch, seq, H_D, e)
    x = proj_residual(x, yc, o.reshape(n_batch * seq, hd), ow["w_out_c"], ow["w_out_d"])
    return x, conv_last, h_last, k_rows, v_rows


def _moe_weights(l, router_group_w, router_group_b, router_expert_w, router_expert_b):
    d = router_group_w.shape[1]
    n_used = N_GROUPS + router_expert_w.shape[2]
    rw = jnp.concatenate([router_group_w[l], router_expert_w[l], jnp.zeros((d, ROUTER_LANES - n_used), F32)], axis=1)
    rb = jnp.concatenate([router_group_b[l], router_expert_b[l], jnp.zeros((ROUTER_LANES - n_used,), F32)])
    return dict(rw=rw.astype(BF16), rb=rb.reshape(1, -1))


def kernel(x_prompt, x_sample, state_wkv, state_shift, state_conv, state_rglru, cache_k, cache_v, norm_mix, norm_ffn, norm_final, w_in_even, w_out_even, shift_mu, decay_w0, decay_up, iclr_a0, iclr_up, gate_up, key_k, key_a, bonus_r_k, lnx_g, lnx_b, sgu_norm_g, sgu_norm_b, sgu_w, sgu_b, w_in_odd, w_out_odd, conv_w, conv_b, rgate_w, rgate_b, igate_w, igate_b, lru_lambda, rel_bias, router_group_w, router_group_b, router_expert_w, router_expert_b, exp_w_gate, exp_w_up, exp_w_down):
    B, L, D = x_prompt.shape
    DB, S, _ = x_sample.shape
    depth = norm_mix.shape[0]
    xp = x_prompt.reshape(B * L, D)
    xs = x_sample.reshape(DB * S, D)
    W = cache_k.shape[2]
    dist_tab = _dist_table(rel_bias, max(L, W + S) - 1)

    sorted_buf = None
    wg_all, wu_all, wd_all = (w.astype(BF16) for w in (exp_w_gate, exp_w_up, exp_w_down))
    wkv_p, shift_p, conv_p, lru_p, k_p, v_p = [], [], [], [], [], []
    wkv_s, shift_s, chunkv_s, conv_s, lru_s, k_s, v_s = [], [], [], [], [], [], []
    for l in range(depth):
        j = l // 2
        if l % 2 == 0:
            ew = _even_weights(j, w_in_even, w_out_even, shift_mu, decay_w0, decay_up, iclr_a0, iclr_up, gate_up,
                               key_k, key_a, bonus_r_k, lnx_g, lnx_b, sgu_norm_g, sgu_norm_b, sgu_w, sgu_b)
            a_proj = ew["mu"].shape[1]
            h_a = state_wkv.shape[2]
            xp, sh, wkv, _ = _even_layer(xp, B, L, GMLP_TILE, RWKV_PAR_PROMPT, False, jnp.zeros((B, a_proj), F32),
                                         jnp.zeros((B, h_a, DH_A, DH_A), F32), norm_mix[l], ew)
            xs, sh_s, wkv_s_new, vn_s = _even_layer(xs, DB, S, S, RWKV_PAR_SAMPLE, True, state_shift[j], state_wkv[j], norm_mix[l], ew)
            wkv_p.append(wkv)
            shift_p.append(sh)
            wkv_s.append(wkv_s_new)
            shift_s.append(sh_s)
            chunkv_s.append(vn_s.reshape(DB, S, -1))
        else:
            ow = _odd_weights(j, w_in_odd, w_out_odd, conv_w, conv_b, rgate_w, rgate_b, igate_w, igate_b, lru_lambda)
            dc = ow["lam"].shape[1]
            xp, cv, hl, kr, vr = _odd_layer(xp, B, L, jnp.zeros((B, CONV_W - 1, dc), F32), jnp.zeros((B, dc), F32),
                                            0, None, dist_tab, norm_mix[l], ow)
            xs, cv_s, hl_s, kr_s, vr_s = _odd_layer(xs, DB, S, state_conv[j], state_rglru[j], PAST_LEN,
                                                    (cache_k[j], cache_v[j]), dist_tab, norm_mix[l], ow)
            conv_p.append(cv)
            lru_p.append(hl)
            k_p.append(kr)
            v_p.append(vr)
            conv_s.append(cv_s)
            lru_s.append(hl_s)
            k_s.append(kr_s)
            v_s.append(vr_s)
        mw = _moe_weights(l, router_group_w, router_group_b, router_expert_w, router_expert_b)
        xp, sorted_buf = moe_layer_sparse(xp, norm_ffn[l], mw["rw"], mw["rb"], wg_all, wu_all, wd_all, l,
                                          final_g=norm_final if l == depth - 1 else None, sorted_buf=sorted_buf)
        xs = moe_layer(xs, norm_ffn[l], mw["rw"], mw["rb"], wg_all, wu_all, wd_all, l)
    y_prompt = xp.reshape(B, L, D)
    y_sample = rmsnorm_call(xs, norm_final).reshape(DB, S, D)
    return (y_prompt, y_sample,
            jnp.stack(wkv_p), jnp.stack(shift_p), jnp.stack(conv_p), jnp.stack(lru_p), jnp.stack(k_p), jnp.stack(v_p),
            jnp.stack(wkv_s), jnp.stack(shift_s), jnp.stack(chunkv_s), jnp.stack(conv_s), jnp.stack(lru_s),
            jnp.stack(k_s), jnp.stack(v_s))
```

```python
import functools
import math

import numpy as np
import jax
import jax.numpy as jnp
from jax import lax
from jax.experimental import pallas as pl
from jax.experimental.pallas import tpu as pltpu

F32 = jnp.float32
BF16 = jnp.bfloat16

PAST_LEN = 8192
DH_A = 64
R_DECAY = 64
R_ICLR = 64
R_GATE = 128
GN_EPS = 64e-5
H_C = 8
CONV_W = 4
LRU_C = 8.0
H_D = 8
DILATED = ((128, 1), (512, 4), (2048, 16))
N_BUCKETS = 32
BUCKET_MAX_DIST = 2048
NEG_BIG = -1e30
N_GROUPS = 4
EXP_PER_GROUP = 4
NORM_EPS = 1e-6
LOG2E = math.log2(math.e)

VMEM_LIMIT = 56 * 1024 * 1024
RWKV_CHUNK = 64
RWKV_PAR_PROMPT = 4
RWKV_PAR_SAMPLE = 8
ATT_TILE = 128
ATT_SUBTILES = 4
LANES = 128
SUBLANES = 8


def _cparams(sem):
    return pltpu.CompilerParams(dimension_semantics=sem, vmem_limit_bytes=VMEM_LIMIT)


def _dot(a, b, precision=None):
    return jnp.dot(a, b, preferred_element_type=F32, precision=precision)


def _dot_nt(a, b, precision=None):
    return lax.dot_general(a, b, (((1,), (1,)), ((), ())), preferred_element_type=F32, precision=precision)


def _dot_tn(a, b, precision=None):
    return lax.dot_general(a, b, (((0,), (0,)), ((), ())), preferred_element_type=F32, precision=precision)


def _split_bf16(x, n):
    parts = []
    for _ in range(n):
        hi = x.astype(BF16)
        parts.append(hi)
        x = x - hi.astype(F32)
    return parts


def _dot_exact_rhs(a, b_bf16, n_split):
    parts = _split_bf16(a, n_split)
    acc = _dot(parts[0], b_bf16)
    for part in parts[1:]:
        acc = acc + _dot(part, b_bf16)
    return acc


def _dot_exact_lhs(a_bf16, b, n_split):
    parts = _split_bf16(b, n_split)
    acc = _dot(a_bf16, parts[0])
    for part in parts[1:]:
        acc = acc + _dot(a_bf16, part)
    return acc


def _softplus(x):
    return jnp.maximum(x, 0.0) + jnp.log(1.0 + jnp.exp(-jnp.abs(x)))


def _sigmoid(x):
    return 1.0 / (1.0 + jnp.exp(-x))


def _gelu(x):
    c = math.sqrt(2.0 / math.pi)
    return 0.5 * x * (1.0 + jnp.tanh(c * (x + 0.044715 * (x * x * x))))


def _row_tile(t, pref=512):
    return pref if t % pref == 0 else t


def _norm_matmul_kernel(x_ref, g_ref, w_ref, *refs, splits, t_splits):
    x = x_ref[...]
    ms = jnp.mean(x * x, axis=-1, keepdims=True)
    h = (x * lax.rsqrt(ms + NORM_EPS) * g_ref[...]).astype(BF16)
    if t_splits:
        wt_ref, refs = refs[0], refs[1:]
    off = 0
    for o_ref, n in zip(refs[:len(splits)], splits):
        o_ref[...] = _dot(h, w_ref[:, off:off + n])
        off += n
    off = 0
    for o_ref, n in zip(refs[len(splits):], t_splits):
        o_ref[0] = _dot_nt(wt_ref[off:off + n, :], h)
        off += n


def norm_matmul(x, g, w_bf16, splits, wt_bf16=None, t_splits=(), seq=None):
    t, d = x.shape
    n = w_bf16.shape[1]
    tm = _row_tile(t)
    in_specs = [pl.BlockSpec((tm, d), lambda i: (i, 0)),
                pl.BlockSpec((1, d), lambda i: (0, 0)),
                pl.BlockSpec((d, n), lambda i: (0, 0))]
    args = [x, g.reshape(1, d), w_bf16]
    out_shape = [jax.ShapeDtypeStruct((t, s), F32) for s in splits]
    out_specs = [pl.BlockSpec((tm, s), lambda i: (i, 0)) for s in splits]
    if t_splits:
        tiles = seq // tm
        in_specs.append(pl.BlockSpec(wt_bf16.shape, lambda i: (0, 0)))
        args.append(wt_bf16)
        out_shape += [jax.ShapeDtypeStruct((t // seq, s, seq), F32) for s in t_splits]
        out_specs += [pl.BlockSpec((1, s, tm), lambda i: (i // tiles, 0, i % tiles)) for s in t_splits]
    return pl.pallas_call(
        functools.partial(_norm_matmul_kernel, splits=splits, t_splits=tuple(t_splits)),
        out_shape=out_shape,
        grid=(t // tm,),
        in_specs=in_specs,
        out_specs=out_specs,
        compiler_params=_cparams(("parallel",)),
        name="norm_matmul",
    )(*args)


def _proj_res_kernel(x_ref, a_ref, b_ref, wa_ref, wb_ref, o_ref):
    acc = _dot(a_ref[...].astype(BF16), wa_ref[...]) + _dot(b_ref[...].astype(BF16), wb_ref[...])
    o_ref[...] = x_ref[...] + acc


def proj_residual(x, a, b, wa, wb):
    t, d = x.shape
    tm = _row_tile(t)
    ka, kb = a.shape[1], b.shape[1]
    return pl.pallas_call(
        _proj_res_kernel,
        out_shape=jax.ShapeDtypeStruct((t, d), F32),
        grid=(t // tm,),
        in_specs=[pl.BlockSpec((tm, d), lambda i: (i, 0)),
                  pl.BlockSpec((tm, ka), lambda i: (i, 0)),
                  pl.BlockSpec((tm, kb), lambda i: (i, 0)),
                  pl.BlockSpec((ka, d), lambda i: (0, 0)),
                  pl.BlockSpec((kb, d), lambda i: (0, 0))],
        out_specs=pl.BlockSpec((tm, d), lambda i: (i, 0)),
        compiler_params=_cparams(("parallel",)),
        name="proj_residual",
    )(x, a, b, wa, wb)


def _rmsnorm_kernel(x_ref, g_ref, o_ref):
    x = x_ref[...]
    ms = jnp.mean(x * x, axis=-1, keepdims=True)
    o_ref[...] = x * lax.rsqrt(ms + NORM_EPS) * g_ref[...]


def rmsnorm_call(x, g):
    t, d = x.shape
    tm = _row_tile(t)
    return pl.pallas_call(
        _rmsnorm_kernel,
        out_shape=jax.ShapeDtypeStruct((t, d), F32),
        grid=(t // tm,),
        in_specs=[pl.BlockSpec((tm, d), lambda i: (i, 0)), pl.BlockSpec((1, d), lambda i: (0, 0))],
        out_specs=pl.BlockSpec((tm, d), lambda i: (i, 0)),
        compiler_params=_cparams(("parallel",)),
        name="final_rmsnorm",
    )(x, g.reshape(1, d))


ROUTER_LANES = LANES


def _route(xn, rw, rb, lane, n_exp):
    logits = _dot_exact_rhs(xn, rw, 2) + rb
    lg = jnp.where(lane < N_GROUPS, logits, -jnp.inf)
    gm = jnp.max(lg, axis=-1, keepdims=True)
    top_pg = 1.0 / jnp.sum(jnp.exp(lg - gm), axis=-1, keepdims=True)
    grp = jnp.min(jnp.where(lg == gm, lane, ROUTER_LANES), axis=-1, keepdims=True)
    in_grp = (lane >= N_GROUPS) & (lane < N_GROUPS + n_exp) & (((lane - N_GROUPS) // EXP_PER_GROUP) == grp)
    le = jnp.where(in_grp, logits, -jnp.inf)
    t1 = jnp.max(le, axis=-1, keepdims=True)
    i1 = jnp.min(jnp.where(le == t1, lane, ROUTER_LANES), axis=-1, keepdims=True)
    le2 = jnp.where(lane == i1, -jnp.inf, le)
    t2 = jnp.max(le2, axis=-1, keepdims=True)
    i2 = jnp.min(jnp.where(le2 == t2, lane, ROUTER_LANES), axis=-1, keepdims=True)
    ex = jnp.exp(t2 - t1)
    w1 = 1.0 / (1.0 + ex)
    return i1, i2, w1 * top_pg, (ex * w1) * top_pg


def _moe_kernel(x_ref, g_ref, rw_ref, rb_ref, wg_ref, wu_ref, wd_ref, o_ref, xn_scr, gate_scr, acc_scr, *, n_exp):
    e = pl.program_id(1)
    tm = x_ref.shape[0]
    lane = lax.broadcasted_iota(jnp.int32, (tm, ROUTER_LANES), 1)

    @pl.when(e == 0)
    def _():
        x = x_ref[...]
        ms = jnp.mean(x * x, axis=-1, keepdims=True)
        xn = x * lax.rsqrt(ms + NORM_EPS) * g_ref[...]
        xn_scr[...] = xn.astype(BF16)
        i1, i2, g1, g2 = _route(xn, rw_ref[...], rb_ref[...], lane, n_exp)
        gate_scr[...] = jnp.where(lane == i1, g1, 0.0) + jnp.where(lane == i2, g2, 0.0)
        acc_scr[...] = jnp.zeros_like(acc_scr)

    xn = xn_scr[...]
    hg = _dot(xn, wg_ref[0, 0])
    hu = _dot(xn, wu_ref[0, 0])
    gcol = jnp.sum(jnp.where(lane == e + N_GROUPS, gate_scr[...], 0.0), axis=-1, keepdims=True)
    hid = hg * _sigmoid(hg) * hu * gcol
    acc_scr[...] += _dot(hid.astype(BF16), wd_ref[0, 0])

    @pl.when(e == n_exp - 1)
    def _():
        o_ref[...] = x_ref[...] + acc_scr[...]


def moe_layer(x, g, rw, rb, wg, wu, wd, layer):
    t, d = x.shape
    _, n_exp, _, f = wg.shape
    tm = _row_tile(t)
    return pl.pallas_call(
        functools.partial(_moe_kernel, n_exp=n_exp),
        out_shape=jax.ShapeDtypeStruct((t, d), F32),
        grid=(t // tm, n_exp),
        in_specs=[pl.BlockSpec((tm, d), lambda i, e: (i, 0)),
                  pl.BlockSpec((1, d), lambda i, e: (0, 0)),
                  pl.BlockSpec((d, ROUTER_LANES), lambda i, e: (0, 0)),
                  pl.BlockSpec((1, ROUTER_LANES), lambda i, e: (0, 0)),
                  pl.BlockSpec((1, 1, d, f), lambda i, e: (layer, e, 0, 0)),
                  pl.BlockSpec((1, 1, d, f), lambda i, e: (layer, e, 0, 0)),
                  pl.BlockSpec((1, 1, f, d), lambda i, e: (layer, e, 0, 0))],
        out_specs=pl.BlockSpec((tm, d), lambda i, e: (i, 0)),
        scratch_shapes=[pltpu.VMEM((tm, d), BF16), pltpu.VMEM((tm, ROUTER_LANES), F32), pltpu.VMEM((tm, d), F32)],
        compiler_params=_cparams(("parallel", "arbitrary")),
        name="hier_moe",
    )(x, g.reshape(1, d), rw, rb, wg, wu, wd)


MOE_ROW_TILE = 512
MOE_COPY_CHUNK = 512
MOE_COMBINE_TILE = 512


def _router_kernel(x_ref, g_ref, rw_ref, rb_ref, tri_ref, gate_ref, info_ref, cnt_ref, base_scr, *, n_exp, n_tiles):
    i = pl.program_id(0)
    tm = x_ref.shape[0]
    lane = lax.broadcasted_iota(jnp.int32, (tm, ROUTER_LANES), 1)

    @pl.when(i == 0)
    def _():
        base_scr[...] = jnp.zeros_like(base_scr)

    x = x_ref[...]
    ms = jnp.mean(x * x, axis=-1, keepdims=True)
    xn = x * lax.rsqrt(ms + NORM_EPS) * g_ref[...]
    i1, i2, g1, g2 = _route(xn, rw_ref[...], rb_ref[...], lane, n_exp)
    chosen = jnp.where((lane == i1) | (lane == i2), 1.0, 0.0)
    before = _dot(tri_ref[...], chosen.astype(BF16)) + base_scr[...]
    r1 = jnp.sum(jnp.where(lane == i1, before, 0.0), axis=-1, keepdims=True)
    r2 = jnp.sum(jnp.where(lane == i2, before, 0.0), axis=-1, keepdims=True)
    base_scr[...] += jnp.sum(chosen, axis=0, keepdims=True)
    gate_ref[...] = jnp.where(lane == 0, g1, 0.0) + jnp.where(lane == 1, g2, 0.0)
    e1 = (i1 - N_GROUPS).astype(F32)
    e2 = (i2 - N_GROUPS).astype(F32)
    info_ref[...] = (jnp.where(lane == 0, e1, 0.0) + jnp.where(lane == 1, e2, 0.0)
                     + jnp.where(lane == 2, r1, 0.0) + jnp.where(lane == 3, r2, 0.0))

    @pl.when(i == n_tiles - 1)
    def _():
        cnt_ref[...] = base_scr[...]


def _scatter_rows_kernel(pos0_ref, pos1_ref, x_ref, xs_in_hbm, xs_hbm, stage, sem, *, CH, n_chunks):
    del xs_in_hbm
    c = pl.program_id(0)
    slot = c % 2
    x = x_ref[...]
    for j in range(SUBLANES):
        stage[slot, pl.ds(j, CH, stride=SUBLANES), :] = x[:, j * LANES:(j + 1) * LANES]

    def body(r, carry):
        t = c * CH + r
        src = stage.at[slot, pl.ds(pl.multiple_of(r * SUBLANES, SUBLANES), SUBLANES), :]
        d0 = pl.multiple_of(pos0_ref[t], SUBLANES)
        d1 = pl.multiple_of(pos1_ref[t], SUBLANES)
        pltpu.make_async_copy(src, xs_hbm.at[pl.ds(d0, SUBLANES), :], sem.at[slot]).start(priority=0)
        pltpu.make_async_copy(src, xs_hbm.at[pl.ds(d1, SUBLANES), :], sem.at[slot]).start(priority=1)
        return carry

    lax.fori_loop(0, CH, body, 0, unroll=8)

    def drain(s):
        pltpu.make_async_copy(stage.at[s], xs_hbm.at[pl.ds(0, CH * SUBLANES), :], sem.at[s]).wait()
        pltpu.make_async_copy(stage.at[s], xs_hbm.at[pl.ds(0, CH * SUBLANES), :], sem.at[s]).wait()

    @pl.when(c > 0)
    def _():
        drain(1 - slot)

    @pl.when(c == n_chunks - 1)
    def _():
        drain(slot)


def _tile_rows_to_matrix(ref, lead, n_rows):
    return jnp.concatenate([ref[lead + (pl.ds(j, n_rows, stride=SUBLANES), slice(None))] for j in range(SUBLANES)],
                           axis=-1)


def _expert_kernel(te_ref, nv_ref, xs_ref, g_ref, wg_ref, wu_ref, wd_ref, y_ref, *, TM):
    @pl.when(pl.program_id(0) < nv_ref[0])
    def _():
        x = _tile_rows_to_matrix(xs_ref, (), TM)
        ms = jnp.mean(x * x, axis=-1, keepdims=True)
        xn = (x * lax.rsqrt(ms + NORM_EPS) * g_ref[...]).astype(BF16)
        hg = _dot(xn, wg_ref[0, 0])
        hu = _dot(xn, wu_ref[0, 0])
        hid = hg * _sigmoid(hg) * hu
        y = _dot(hid.astype(BF16), wd_ref[0, 0])
        for j in range(SUBLANES):
            y_ref[pl.ds(j, TM, stride=SUBLANES), :] = y[:, j * LANES:(j + 1) * LANES]

    @pl.when(pl.program_id(0) >= nv_ref[0])
    def _():
        y_ref[...] = jnp.zeros_like(y_ref)


def _combine_kernel(pos0_ref, pos1_ref, x_ref, gate_ref, fg_ref, y_hbm, o_ref, ybuf, sem, *, TC, n_tiles, final_norm):
    i = pl.program_id(0)

    def issue(tile, slot):
        def body(r, carry):
            t = tile * TC + r
            dst = pl.ds(pl.multiple_of(r * SUBLANES, SUBLANES), SUBLANES)
            s0 = pl.multiple_of(pos0_ref[t], SUBLANES)
            s1 = pl.multiple_of(pos1_ref[t], SUBLANES)
            pltpu.make_async_copy(y_hbm.at[pl.ds(s0, SUBLANES), :], ybuf.at[slot, 0, dst, :],
                                  sem.at[slot]).start(priority=0)
            pltpu.make_async_copy(y_hbm.at[pl.ds(s1, SUBLANES), :], ybuf.at[slot, 1, dst, :],
                                  sem.at[slot]).start(priority=1)
            return carry
        lax.fori_loop(0, TC, body, 0, unroll=8)

    @pl.when(i == 0)
    def _():
        issue(0, 0)

    @pl.when(i + 1 < n_tiles)
    def _():
        issue(i + 1, (i + 1) % 2)

    slot = i % 2
    pltpu.make_async_copy(y_hbm.at[pl.ds(0, TC * SUBLANES), :], ybuf.at[slot, 0], sem.at[slot]).wait()
    pltpu.make_async_copy(y_hbm.at[pl.ds(0, TC * SUBLANES), :], ybuf.at[slot, 1], sem.at[slot]).wait()
    gate = gate_ref[...]
    y0 = _tile_rows_to_matrix(ybuf, (slot, 0), TC)
    y1 = _tile_rows_to_matrix(ybuf, (slot, 1), TC)
    out = x_ref[...] + gate[:, 0:1] * y0 + gate[:, 1:2] * y1
    if final_norm:
        ms = jnp.mean(out * out, axis=-1, keepdims=True)
        out = out * lax.rsqrt(ms + NORM_EPS) * fg_ref[...]
    o_ref[...] = out


def moe_layer_sparse(x, g, rw, rb, wg, wu, wd, layer, final_g=None, sorted_buf=None):
    t, d = x.shape
    _, n_exp, _, f = wg.shape
    TM = MOE_ROW_TILE
    n_tiles = t // TM
    tri = jnp.asarray(np.tril(np.ones((TM, TM), np.float32), -1)).astype(BF16)
    gate, info, cnt = pl.pallas_call(
        functools.partial(_router_kernel, n_exp=n_exp, n_tiles=n_tiles),
        out_shape=[jax.ShapeDtypeStruct((t, ROUTER_LANES), F32), jax.ShapeDtypeStruct((t, ROUTER_LANES), F32),
                   jax.ShapeDtypeStruct((1, ROUTER_LANES), F32)],
        grid=(n_tiles,),
        in_specs=[pl.BlockSpec((TM, d), lambda i: (i, 0)),
                  pl.BlockSpec((1, d), lambda i: (0, 0)),
                  pl.BlockSpec((d, ROUTER_LANES), lambda i: (0, 0)),
                  pl.BlockSpec((1, ROUTER_LANES), lambda i: (0, 0)),
                  pl.BlockSpec((TM, TM), lambda i: (0, 0))],
        out_specs=[pl.BlockSpec((TM, ROUTER_LANES), lambda i: (i, 0)),
                   pl.BlockSpec((TM, ROUTER_LANES), lambda i: (i, 0)),
                   pl.BlockSpec((1, ROUTER_LANES), lambda i: (0, 0))],
        scratch_shapes=[pltpu.VMEM((1, ROUTER_LANES), F32)],
        compiler_params=_cparams(("arbitrary",)),
        name="moe_router",
    )(x, g.reshape(1, d), rw, rb, tri)

    counts = cnt[0, N_GROUPS:N_GROUPS + n_exp].astype(jnp.int32)
    padded = ((counts + TM - 1) // TM) * TM
    ends = jnp.cumsum(padded)
    offs = ends - padded
    eid = info[:, 0:2].astype(jnp.int32)
    rank = info[:, 2:4].astype(jnp.int32)
    pos = jnp.sum(jnp.where(eid[:, :, None] == jnp.arange(n_exp)[None, None, :], offs[None, None, :], 0), axis=-1) + rank
    assert d == SUBLANES * LANES, "a token row must fill exactly one (8, 128) tile"
    pos = pos * SUBLANES
    pos0, pos1 = pos[:, 0], pos[:, 1]
    max_tiles = (2 * t) // TM + n_exp
    n_valid = (ends[-1] // TM).astype(jnp.int32).reshape(1)
    tile_exp = jnp.minimum(jnp.sum((ends[None, :] // TM) <= jnp.arange(max_tiles)[:, None], axis=-1),
                           n_exp - 1).astype(jnp.int32)
    p_rows = max_tiles * TM

    CH = MOE_COPY_CHUNK
    xs = pl.pallas_call(
        functools.partial(_scatter_rows_kernel, CH=CH, n_chunks=t // CH),
        out_shape=jax.ShapeDtypeStruct((p_rows * SUBLANES, LANES), F32),
        grid_spec=pltpu.PrefetchScalarGridSpec(
            num_scalar_prefetch=2, grid=(t // CH,),
            in_specs=[pl.BlockSpec((CH, d), lambda c, p0, p1: (c, 0)), pl.BlockSpec(memory_space=pl.ANY)],
            out_specs=pl.BlockSpec(memory_space=pl.ANY),
            scratch_shapes=[pltpu.VMEM((2, CH * SUBLANES, LANES), F32), pltpu.SemaphoreType.DMA((2,))]),
        input_output_aliases={3: 0},
        compiler_params=pltpu.CompilerParams(dimension_semantics=("arbitrary",), vmem_limit_bytes=VMEM_LIMIT,
                                             has_side_effects=True),
        name="moe_scatter_rows",
    )(pos0, pos1, x, jnp.zeros((p_rows * SUBLANES, LANES), F32) if sorted_buf is None else sorted_buf)

    def row_idx(i, te, nv):
        return (jnp.minimum(i, nv[0] - 1), 0)

    ys = pl.pallas_call(
        functools.partial(_expert_kernel, TM=TM),
        out_shape=jax.ShapeDtypeStruct((p_rows * SUBLANES, LANES), F32),
        grid_spec=pltpu.PrefetchScalarGridSpec(
            num_scalar_prefetch=2, grid=(max_tiles,),
            in_specs=[pl.BlockSpec((TM * SUBLANES, LANES), row_idx),
                      pl.BlockSpec((1, d), lambda i, te, nv: (0, 0)),
                      pl.BlockSpec((1, 1, d, f), lambda i, te, nv: (layer, te[i], 0, 0)),
                      pl.BlockSpec((1, 1, d, f), lambda i, te, nv: (layer, te[i], 0, 0)),
                      pl.BlockSpec((1, 1, f, d), lambda i, te, nv: (layer, te[i], 0, 0))],
            out_specs=pl.BlockSpec((TM * SUBLANES, LANES), lambda i, te, nv: (i, 0))),
        compiler_params=_cparams(("arbitrary",)),
        name="moe_experts",
    )(tile_exp, n_valid, xs, g.reshape(1, d), wg, wu, wd)

    TC = MOE_COMBINE_TILE
    out = pl.pallas_call(
        functools.partial(_combine_kernel, TC=TC, n_tiles=t // TC, final_norm=final_g is not None),
        out_shape=jax.ShapeDtypeStruct((t, d), F32),
        grid_spec=pltpu.PrefetchScalarGridSpec(
            num_scalar_prefetch=2, grid=(t // TC,),
            in_specs=[pl.BlockSpec((TC, d), lambda i, p0, p1: (i, 0)),
                      pl.BlockSpec((TC, ROUTER_LANES), lambda i, p0, p1: (i, 0)),
                      pl.BlockSpec((1, d), lambda i, p0, p1: (0, 0)),
                      pl.BlockSpec(memory_space=pl.ANY)],
            out_specs=pl.BlockSpec((TC, d), lambda i, p0, p1: (i, 0)),
            scratch_shapes=[pltpu.VMEM((2, 2, TC * SUBLANES, LANES), F32), pltpu.SemaphoreType.DMA((2,))]),
        compiler_params=_cparams(("arbitrary",)),
        name="moe_combine",
    )(pos0, pos1, x, gate, (g if final_g is None else final_g).reshape(1, d), ys)
    return out, xs


def _rwkv_kernel(p_ref, prev_ref, s0_ref, mu_ref, w0_ref, wd_ref, a0_ref, wa_ref, wg_ref, kk_ref, ka_ref,
                 bonus_ref, lng_ref, lnb_ref, tri_ref, hsum_ref, ya_ref, sf_ref, s_scr, prev_scr,
                 *, NB, C, H, DH, n_chunks):
    c = pl.program_id(1)

    @pl.when(c == 0)
    def _():
        s_scr[...] = s0_ref[:, 0]
        prev_scr[...] = prev_ref[:, 0]

    DA = H * DH
    R = NB * C
    p = p_ref[...].reshape(R, p_ref.shape[-1])
    row = lax.broadcasted_iota(jnp.int32, p.shape, 0)
    shifted = pltpu.roll(p, 1, axis=0)
    for n in range(NB):
        shifted = jnp.where(row == n * C, prev_scr[n], shifted)
        prev_scr[n] = p[(n + 1) * C - 1:(n + 1) * C, :]
    xs = p + (shifted - p) * mu_ref[...]
    r = xs[:, 0:DA]
    k = xs[:, DA:2 * DA]
    v = xs[:, 2 * DA:3 * DA]
    lora = xs[:, 3 * DA:3 * DA + R_DECAY + R_ICLR]
    gd = xs[:, 3 * DA + R_DECAY + R_ICLR:3 * DA + R_DECAY + R_ICLR + R_GATE]

    w_log = -_softplus(-(w0_ref[...] + _dot(jnp.tanh(lora).astype(BF16), wd_ref[...]))) - 0.5
    lw = -jnp.exp(w_log)
    a = _sigmoid(a0_ref[...] + _dot(lora.astype(BF16), wa_ref[...]))
    g = _dot(_sigmoid(gd).astype(BF16), wg_ref[...])

    kk = k * kk_ref[...]
    ss = _dot_exact_rhs(kk * kk, hsum_ref[...], 2)
    kk = kk / jnp.maximum(jnp.sqrt(ss), 1e-12)
    k2 = k * (1.0 + (a - 1.0) * ka_ref[...])
    kka = kk * a

    cum = jnp.concatenate([_dot_exact_lhs(tri_ref[...], lw[n * C:(n + 1) * C], 3) for n in range(NB)], axis=0)
    p_in = jnp.exp(cum)
    r_t = r * p_in
    a_t = kk * jnp.exp(cum - lw)
    p_inv = jnp.exp(-cum)
    b_t = kka * p_inv
    k_t = k2 * p_inv
    bonus = _dot_exact_rhs(r * k2 * bonus_ref[...], hsum_ref[...], 2) * v

    ri = lax.broadcasted_iota(jnp.int32, (C, C), 0)
    ci = lax.broadcasted_iota(jnp.int32, (C, C), 1)
    strict = ri > ci
    incl = ri >= ci
    eye = (ri == ci).astype(F32)
    n_double = max(int(math.ceil(math.log2(C))) - 1, 0)

    chains = [(n, h) for n in range(NB) for h in range(H)]

    def blk(x, n, h):
        return x[n * C:(n + 1) * C, h * DH:(h + 1) * DH]

    def bf(x):
        return x.astype(BF16)

    Bt = [bf(blk(b_t, n, h)) for n, h in chains]
    Kt = [bf(blk(k_t, n, h)) for n, h in chains]
    Vf = [blk(v, n, h) for n, h in chains]
    AR = [bf(jnp.concatenate([blk(a_t, n, h), blk(r_t, n, h)], axis=0)) for n, h in chains]
    S0 = [s_scr[n, h] for n, h in chains]
    idx = range(len(chains))
    GB = [_dot_nt(AR[i], Bt[i]) for i in idx]
    GK = [_dot_nt(AR[i], Kt[i]) for i in idx]
    ARS = [_dot_nt(AR[i], bf(S0[i])) for i in idx]
    Lm = [jnp.where(strict, GB[i][0:C], 0.0) for i in idx]
    Gb = [bf(jnp.where(incl, GB[i][C:2 * C], 0.0)) for i in idx]
    MG = [bf(jnp.concatenate([jnp.where(strict, GK[i][0:C], 0.0), jnp.where(incl, GK[i][C:2 * C], 0.0)], axis=0))
          for i in idx]
    MGV = [_dot(MG[i], bf(Vf[i])) for i in idx]
    T = [eye - Lm[i] for i in idx]
    Pw = [bf(Lm[i]) for i in idx]
    for _ in range(n_double):
        Pw = [bf(_dot(Pw[i], Pw[i])) for i in idx]
        T = [T[i] + _dot(bf(T[i]), Pw[i]) for i in idx]
    U = [_dot(bf(T[i]), bf(-(ARS[i][0:C] + MGV[i][0:C]))) for i in idx]
    Y = [ARS[i][C:2 * C] + _dot(Gb[i], bf(U[i])) + MGV[i][C:2 * C] for i in idx]
    for i, (n, h) in enumerate(chains):
        UV = bf(jnp.concatenate([U[i], Vf[i]], axis=0))
        BK = jnp.concatenate([Bt[i], Kt[i]], axis=0)
        p_tot = p_in[(n + 1) * C - 1:(n + 1) * C, h * DH:(h + 1) * DH]
        s_scr[n, h] = (S0[i] + _dot_tn(UV, BK)) * p_tot

    rows = []
    for n in range(NB):
        ys = []
        for h in range(H):
            Yh = Y[n * H + h]
            yc = Yh - jnp.mean(Yh, axis=-1, keepdims=True)
            var = jnp.mean(yc * yc, axis=-1, keepdims=True)
            ys.append(yc * lax.rsqrt(var + GN_EPS))
        rows.append(jnp.concatenate(ys, axis=-1))
    y = jnp.concatenate(rows, axis=0) * lng_ref[...] + lnb_ref[...]
    ya_ref[...] = ((y + bonus) * g).reshape(NB, C, DA)

    @pl.when(c == n_chunks - 1)
    def _():
        sf_ref[:, 0] = s_scr[...]


def rwkv_mix(pa, n_batch, seq, shift_prev, wkv0, wts, n_par):
    t, ap = pa.shape
    H = wkv0.shape[1]
    DA = H * DH_A
    C = min(RWKV_CHUNK, seq)
    n_chunks = seq // C
    NB = n_par
    G = n_batch // NB
    tri = jnp.asarray(np.tril(np.ones((C, C), np.float32))).astype(BF16)
    hsum = jnp.asarray(np.kron(np.eye(H, dtype=np.float32), np.ones((DH_A, DH_A), np.float32))).astype(BF16)

    def full(shape):
        nd = len(shape)
        return pl.BlockSpec(shape, lambda b, c: (0,) * nd)

    vec = full((1, DA))
    ya, s_fin = pl.pallas_call(
        functools.partial(_rwkv_kernel, NB=NB, C=C, H=H, DH=DH_A, n_chunks=n_chunks),
        out_shape=[jax.ShapeDtypeStruct((NB, t // NB, DA), F32),
                   jax.ShapeDtypeStruct((NB, G, H, DH_A, DH_A), F32)],
        grid=(G, n_chunks),
        in_specs=[pl.BlockSpec((NB, C, ap), lambda b, c: (0, b * n_chunks + c, 0)),
                  pl.BlockSpec((NB, 1, 1, ap), lambda b, c: (0, b, 0, 0)),
                  pl.BlockSpec((NB, 1, H, DH_A, DH_A), lambda b, c: (0, b, 0, 0, 0)),
                  full((1, ap)), vec, full((R_DECAY + R_ICLR, DA)), vec, full((R_DECAY + R_ICLR, DA)),
                  full((R_GATE, DA)), vec, vec, vec, vec, vec, full((C, C)), full((DA, DA))],
        out_specs=[pl.BlockSpec((NB, C, DA), lambda b, c: (0, b * n_chunks + c, 0)),
                   pl.BlockSpec((NB, 1, H, DH_A, DH_A), lambda b, c: (0, b, 0, 0, 0))],
        scratch_shapes=[pltpu.VMEM((NB, H, DH_A, DH_A), F32), pltpu.VMEM((NB, 1, ap), F32)],
        compiler_params=_cparams(("parallel", "arbitrary")),
        name="rwkv7_mix",
    )(pa.reshape(NB, t // NB, ap), shift_prev.reshape(NB, G, 1, ap), wkv0.reshape(NB, G, H, DH_A, DH_A),
      wts["mu"], wts["w0"], wts["wd"], wts["a0"], wts["wa"],
      wts["wg"], wts["key_k"], wts["key_a"], wts["bonus"], wts["lnx_g"], wts["lnx_b"], tri, hsum)
    return ya.reshape(t, DA), s_fin.reshape(wkv0.shape)


GMLP_TILE = 128
GMLP_ROWS = 512


def _gmlp_kernel(u_ref, v_ref, ng_ref, nb_ref, wm_ref, bias_ref, o_ref, *vn_refs, n_sub):
    vf = _gelu(v_ref[...])
    mu = jnp.mean(vf, axis=-1, keepdims=True)
    vc = vf - mu
    var = jnp.mean(vc * vc, axis=-1, keepdims=True)
    vn = vc * lax.rsqrt(var + NORM_EPS) * ng_ref[...] + nb_ref[...]
    for vn_ref in vn_refs:
        vn_ref[...] = vn
    vb = vn.astype(BF16)
    n_h = wm_ref.shape[0]
    cb = vn.shape[1] // n_h
    gu = _gelu(u_ref[...])
    for c in range(n_sub):
        rows = slice(c * GMLP_TILE, (c + 1) * GMLP_TILE)
        s = jnp.concatenate([_dot(wm_ref[h], vb[rows, h * cb:(h + 1) * cb]) for h in range(n_h)], axis=-1)
        o_ref[rows, :] = gu[rows, :] * (s + bias_ref[...])


def gmlp_mix(pu, pv, ng, nb, wm_bf16, bias_tile, want_vn):
    t, db = pu.shape
    n_h = wm_bf16.shape[0]
    rows = GMLP_ROWS if t % GMLP_ROWS == 0 else t
    n_out = 2 if want_vn else 1
    outs = pl.pallas_call(
        functools.partial(_gmlp_kernel, n_sub=rows // GMLP_TILE),
        out_shape=[jax.ShapeDtypeStruct((t, db), F32)] * n_out,
        grid=(t // rows,),
        in_specs=[pl.BlockSpec((rows, db), lambda i: (i, 0)),
                  pl.BlockSpec((rows, db), lambda i: (i, 0)),
                  pl.BlockSpec((1, db), lambda i: (0, 0)),
                  pl.BlockSpec((1, db), lambda i: (0, 0)),
                  pl.BlockSpec((n_h, GMLP_TILE, GMLP_TILE), lambda i: (0, 0, 0)),
                  pl.BlockSpec((GMLP_TILE, db), lambda i: (0, 0))],
        out_specs=[pl.BlockSpec((rows, db), lambda i: (i, 0))] * n_out,
        compiler_params=_cparams(("parallel",)),
        name="gmlp_mix",
    )(pu, pv, ng, nb, wm_bf16, bias_tile)
    return (outs[0], outs[1]) if want_vn else (outs[0], None)


N_SEG = 8
SEG_GAP = 4


def _seg_pitch(seg):
    return seg + SEG_GAP if seg % SUBLANES == 0 else seg


def _rglru_kernel(xb_ref, gy_ref, cprev_ref, h0_ref, cw_ref, cb_ref, gw_ref, gb_ref, lam_ref,
                  yc_ref, ctail_ref, hl_ref, xe_scr, a_scr, b_scr, h_scr, *, TL, DC, pos0, n_tiles):
    l = pl.program_id(1)
    PAD = SUBLANES

    @pl.when(l == 0)
    def _():
        xe_scr[0:PAD, :] = cprev_ref[0]
        h_scr[...] = h0_ref[0]

    xe_scr[PAD:PAD + TL, :] = xb_ref[...]
    xc = cb_ref[...] + xe_scr[pl.ds(PAD - (CONV_W - 1), TL), :] * cw_ref[0:1, :]
    for i in range(1, CONV_W):
        xc = xc + xe_scr[pl.ds(PAD - (CONV_W - 1) + i, TL), :] * cw_ref[i:i + 1, :]
    tail = xe_scr[TL:TL + PAD, :]
    ctail_ref[0] = tail
    xe_scr[0:PAD, :] = tail

    gates = _dot(xc.astype(BF16), gw_ref[...]) + gb_ref[...]
    rg = _sigmoid(gates[:, 0:DC])
    ig = _sigmoid(gates[:, DC:2 * DC])
    log_a = -LRU_C * rg * _softplus(-lam_ref[...])
    a = jnp.exp(log_a)
    mult = jnp.sqrt(1.0 - a * a)
    row = lax.broadcasted_iota(jnp.int32, (TL, DC), 0)
    mult = jnp.where(row + (l * TL + pos0) == 0, 1.0, mult)
    b = mult * ig * xc
    n_slab = DC // LANES
    seg = TL // N_SEG
    pitch = _seg_pitch(seg)
    for s in range(n_slab):
        for j in range(N_SEG):
            a_scr[s, pl.ds(j * pitch, seg), :] = a[j * seg:(j + 1) * seg, s * LANES:(s + 1) * LANES]
            b_scr[s, pl.ds(j * pitch, seg), :] = b[j * seg:(j + 1) * seg, s * LANES:(s + 1) * LANES]

    def step(i, carry):
        idx = pl.ds(i, N_SEG, stride=pitch) if pitch > 1 else pl.ds(0, N_SEG)
        out = []
        for s in range(n_slab):
            hloc, ap = carry[s]
            ai = a_scr[s, idx, :]
            hloc = ai * hloc + b_scr[s, idx, :]
            ap = ap * ai
            b_scr[s, idx, :] = hloc
            a_scr[s, idx, :] = ap
            out.append((hloc, ap))
        return tuple(out)

    lax.fori_loop(0, seg, step,
                  tuple((jnp.zeros((N_SEG, LANES), F32), jnp.ones((N_SEG, LANES), F32)) for _ in range(n_slab)),
                  unroll=min(seg, 8))

    carry = h_scr[...]
    g_act = _gelu(gy_ref[...])
    for j in range(N_SEG):
        rows = slice(j * seg, (j + 1) * seg)
        rows_p = pl.ds(j * pitch, seg)
        hloc = jnp.concatenate([b_scr[s, rows_p, :] for s in range(n_slab)], axis=-1)
        ap = jnp.concatenate([a_scr[s, rows_p, :] for s in range(n_slab)], axis=-1)
        hj = hloc + ap * carry
        yc_ref[rows, :] = g_act[rows, :] * hj
        carry = hj[seg - 1:seg, :]
    h_scr[...] = carry

    @pl.when(l == n_tiles - 1)
    def _():
        hl_ref[0] = carry


def rglru_mix(xb, gy, n_batch, seq, conv_prev8, h0, pos0, wts):
    t, dc = xb.shape
    TL = 512 if seq % 512 == 0 else seq
    n_tiles = seq // TL
    scan_rows = N_SEG * _seg_pitch(TL // N_SEG)

    def full(shape):
        nd = len(shape)
        return pl.BlockSpec(shape, lambda b, l: (0,) * nd)

    yc, ctail, hl = pl.pallas_call(
        functools.partial(_rglru_kernel, TL=TL, DC=dc, pos0=pos0, n_tiles=n_tiles),
        out_shape=[jax.ShapeDtypeStruct((t, dc), F32), jax.ShapeDtypeStruct((n_batch, SUBLANES, dc), F32),
                   jax.ShapeDtypeStruct((n_batch, 1, dc), F32)],
        grid=(n_batch, n_tiles),
        in_specs=[pl.BlockSpec((TL, dc), lambda b, l: (b * n_tiles + l, 0)),
                  pl.BlockSpec((TL, dc), lambda b, l: (b * n_tiles + l, 0)),
                  pl.BlockSpec((1, SUBLANES, dc), lambda b, l: (b, 0, 0)),
                  pl.BlockSpec((1, 1, dc), lambda b, l: (b, 0, 0)),
                  full((CONV_W, dc)), full((1, dc)), full((dc, 2 * dc)), full((1, 2 * dc)), full((1, dc))],
        out_specs=[pl.BlockSpec((TL, dc), lambda b, l: (b * n_tiles + l, 0)),
                   pl.BlockSpec((1, SUBLANES, dc), lambda b, l: (b, 0, 0)),
                   pl.BlockSpec((1, 1, dc), lambda b, l: (b, 0, 0))],
        scratch_shapes=[pltpu.VMEM((TL + SUBLANES, dc), F32), pltpu.VMEM((dc // LANES, scan_rows, LANES), F32),
                        pltpu.VMEM((dc // LANES, scan_rows, LANES), F32), pltpu.VMEM((1, dc), F32)],
        compiler_params=_cparams(("parallel", "arbitrary")),
        name="rglru_mix",
    )(xb, gy, conv_prev8, h0.reshape(n_batch, 1, dc), wts["conv_w"], wts["conv_b"], wts["gate_w"], wts["gate_b"],
      wts["lam"])
    return yc, ctail[:, SUBLANES - (CONV_W - 1):, :], hl.reshape(n_batch, dc)


def _t5_bucket(dist):
    dist = np.asarray(dist)
    max_exact = N_BUCKETS // 2
    scaled = np.log(np.maximum(dist, 1) / max_exact) / math.log(BUCKET_MAX_DIST / max_exact)
    large = np.minimum(max_exact + (scaled * (N_BUCKETS - max_exact)).astype(np.int32), N_BUCKETS - 1)
    return np.where(dist < max_exact, dist, large).astype(np.int32)


def _dist_table(rel_bias, max_dist):
    dist = np.arange(max_dist + 1)
    count = np.zeros(max_dist + 1, np.float32)
    for window, dil in DILATED:
        count += ((dist % dil == 0) & (dist <= window)).astype(np.float32)
    logcnt = np.where(count > 0, np.log(np.maximum(count, 1.0)), 0.0).astype(np.float32)
    tab = jnp.take(rel_bias, jnp.asarray(_t5_bucket(dist)), axis=0) + jnp.asarray(logcnt)[:, None]
    return jnp.where(jnp.asarray(count > 0)[:, None], tab, NEG_BIG)


def _toeplitz_tiles(tab, n_pos, n_neg, T):
    D, H = tab.shape
    span = T * n_pos
    assert D >= span
    n_col = span + T * n_neg + T - 1
    ext = jnp.concatenate([jnp.flip(tab[:span], axis=0), jnp.full((n_col + 1 - span, H), NEG_BIG, F32)], axis=0)
    ext = jnp.transpose(ext)
    skew = jnp.tile(ext, (1, T))[:, :T * n_col].reshape(H, T, n_col)
    tiles = [skew[:, :, span - 1 - T * dd: span - 1 - T * dd + T] for dd in range(-n_neg, n_pos)]
    return jnp.stack(tiles, axis=1)


def _attn_prompt_kernel(q_ref, k_ref, v_ref, bias_ref, o_ref, kb_scr, vb_scr, *, E, SUB, NS):
    qi = pl.program_id(2)
    TQ = NS * SUB

    @pl.when(qi == 0)
    def _():
        kb_scr[...] = k_ref[0].astype(BF16)
        vb_scr[...] = v_ref[0].astype(BF16)

    lane = lax.broadcasted_iota(jnp.int32, (SUB, 2 * E), 1)
    q2 = []
    for rs in range(NS):
        q = q_ref[0, rs * SUB:(rs + 1) * SUB, :] * (E ** -0.5 * LOG2E)
        q2.append(jnp.concatenate([jnp.where(lane < E, q, 0.0), jnp.where(lane >= E, q, 0.0)], axis=0).astype(BF16))

    def block(i, carry, diagonal):
        j = qi - i
        koff = pl.multiple_of(j * TQ, TQ)
        out = []
        for rs in range(NS):
            n_cs = rs + 1 if diagonal else NS
            kj = kb_scr[:, pl.ds(koff, n_cs * SUB)]
            vj = vb_scr[:, pl.ds(koff, n_cs * SUB)]
            m, l, acc = carry[rs]
            s = _dot(q2[rs], kj)
            parts = []
            for cs in range(n_cs):
                dd = i * NS + (rs - cs + NS - 1)
                bias = jnp.concatenate([bias_ref[0, dd], bias_ref[1, dd]], axis=0)
                parts.append(s[:, cs * SUB:(cs + 1) * SUB] + bias)
            mx = parts[0]
            for part in parts[1:]:
                mx = jnp.maximum(mx, part)
            m_new = jnp.maximum(m, jnp.max(mx, axis=-1, keepdims=True))
            alpha = jnp.exp2(m - m_new)
            ps = [jnp.exp2(part - m_new) for part in parts]
            psum = ps[0]
            for pexp in ps[1:]:
                psum = psum + pexp
            l = alpha * l + psum
            acc = alpha * acc + _dot_nt(jnp.concatenate(ps, axis=-1).astype(BF16), vj)
            out.append((m_new, l, acc))
        return tuple(out)

    init = tuple((jnp.full((2 * SUB, SUB), NEG_BIG, F32), jnp.zeros((2 * SUB, SUB), F32),
                  jnp.zeros((2 * SUB, 2 * E), F32)) for _ in range(NS))
    first = block(0, init, True)
    res = lax.fori_loop(1, qi + 1, lambda i, carry: block(i, carry, False), first)
    for rs in range(NS):
        m, l, acc = res[rs]
        o = acc / jnp.sum(l, axis=-1, keepdims=True)
        o_ref[0, rs * SUB:(rs + 1) * SUB, :] = jnp.where(lane < E, o[0:SUB], o[SUB:2 * SUB])


def attn_prompt(q, k, v, bias_tiles, n_batch, seq):
    hd = q.shape[-1]
    E = hd // H_D
    SUB = ATT_TILE
    NS = ATT_SUBTILES
    TQ = SUB * NS
    nq = seq // TQ
    nt = bias_tiles.shape[1]
    return pl.pallas_call(
        functools.partial(_attn_prompt_kernel, E=E, SUB=SUB, NS=NS),
        out_shape=jax.ShapeDtypeStruct((n_batch, seq, hd), F32),
        grid=(H_D // 2, n_batch, nq),
        in_specs=[pl.BlockSpec((1, TQ, 2 * E), lambda hp, b, i: (b, i, hp)),
                  pl.BlockSpec((1, 2 * E, seq), lambda hp, b, i: (b, hp, 0)),
                  pl.BlockSpec((1, 2 * E, seq), lambda hp, b, i: (b, hp, 0)),
                  pl.BlockSpec((2, nt, SUB, SUB), lambda hp, b, i: (hp, 0, 0, 0))],
        out_specs=pl.BlockSpec((1, TQ, 2 * E), lambda hp, b, i: (b, i, hp)),
        scratch_shapes=[pltpu.VMEM((2 * E, seq), BF16), pltpu.VMEM((2 * E, seq), BF16)],
        compiler_params=_cparams(("arbitrary", "arbitrary", "arbitrary")),
        name="dilated_attn_prompt",
    )(q, k, v, bias_tiles)


def _attn_sample_kernel(q_ref, kn_ref, vn_ref, ck_ref, cv_ref, bo_ref, bn_ref, o_ref, *, E, S):
    lane = lax.broadcasted_iota(jnp.int32, (S, 2 * E), 1)
    NPAD = bn_ref.shape[-1]
    outs = []
    for hp in range(H_D // 2):
        sl = slice(hp * 2 * E, (hp + 1) * 2 * E)
        q = q_ref[0, :, sl] * (E ** -0.5)
        q2 = jnp.concatenate([jnp.where(lane < E, q, 0.0), jnp.where(lane >= E, q, 0.0)], axis=0).astype(BF16)
        zpad = jnp.zeros((NPAD - S, 2 * E), F32)
        kn = jnp.concatenate([kn_ref[0, :, sl], zpad], axis=0).astype(BF16)
        vn = jnp.concatenate([vn_ref[0, :, sl], zpad], axis=0).astype(BF16)
        s_old = _dot(q2, ck_ref[0, sl, :].astype(BF16)) + jnp.concatenate([bo_ref[2 * hp], bo_ref[2 * hp + 1]], axis=0)
        s_new = _dot_nt(q2, kn) + jnp.concatenate([bn_ref[2 * hp], bn_ref[2 * hp + 1]], axis=0)
        m = jnp.maximum(jnp.max(s_old, axis=-1, keepdims=True), jnp.max(s_new, axis=-1, keepdims=True))
        p_old = jnp.exp(s_old - m)
        p_new = jnp.exp(s_new - m)
        l = jnp.sum(p_old, axis=-1, keepdims=True) + jnp.sum(p_new, axis=-1, keepdims=True)
        acc = _dot_nt(p_old.astype(BF16), cv_ref[0, sl, :].astype(BF16)) + _dot(p_new.astype(BF16), vn)
        o = acc / l
        outs.append(jnp.where(lane < E, o[0:S], o[S:2 * S]))
    o_ref[0] = jnp.concatenate(outs, axis=-1)


def attn_sample(q, k_new, v_new, cache_k, cache_v, bias_old, bias_new):
    n_batch, S, hd = q.shape
    W = cache_k.shape[2]
    E = hd // H_D
    NPAD = bias_new.shape[-1]
    return pl.pallas_call(
        functools.partial(_attn_sample_kernel, E=E, S=S),
        out_shape=jax.ShapeDtypeStruct((n_batch, S, hd), F32),
        grid=(n_batch,),
        in_specs=[pl.BlockSpec((1, S, hd), lambda b: (b, 0, 0)),
                  pl.BlockSpec((1, S, hd), lambda b: (b, 0, 0)),
                  pl.BlockSpec((1, S, hd), lambda b: (b, 0, 0)),
                  pl.BlockSpec((1, hd, W), lambda b: (b, 0, 0)),
                  pl.BlockSpec((1, hd, W), lambda b: (b, 0, 0)),
                  pl.BlockSpec((H_D, S, W), lambda b: (0, 0, 0)),
                  pl.BlockSpec((H_D, S, NPAD), lambda b: (0, 0, 0))],
        out_specs=pl.BlockSpec((1, S, hd), lambda b: (b, 0, 0)),
        compiler_params=_cparams(("parallel",)),
        name="dilated_attn_sample",
    )(q, k_new, v_new, cache_k, cache_v, bias_old, bias_new)


def _even_weights(j, w_in_even, w_out_even, shift_mu, decay_w0, decay_up, iclr_a0, iclr_up, gate_up, key_k, key_a,
                  bonus_r_k, lnx_g, lnx_b, sgu_norm_g, sgu_norm_b, sgu_w, sgu_b):
    da = decay_w0.shape[1]
    zeros_d = jnp.zeros((R_ICLR, da), F32)
    zeros_i = jnp.zeros((R_DECAY, da), F32)
    return dict(
        w_in=w_in_even[j].astype(BF16),
        w_out_a=w_out_even[j, :da].astype(BF16), w_out_b=w_out_even[j, da:].astype(BF16),
        mu=shift_mu[j].reshape(1, -1), w0=decay_w0[j].reshape(1, -1), a0=iclr_a0[j].reshape(1, -1),
        wd=jnp.concatenate([decay_up[j], zeros_d], axis=0).astype(BF16),
        wa=jnp.concatenate([zeros_i, iclr_up[j]], axis=0).astype(BF16), wg=gate_up[j].astype(BF16), key_k=key_k[j].reshape(1, -1), key_a=key_a[j].reshape(1, -1),
        bonus=bonus_r_k[j].reshape(1, -1), lnx_g=lnx_g[j].reshape(1, -1), lnx_b=lnx_b[j].reshape(1, -1),
        ng=sgu_norm_g[j].reshape(1, -1), nb=sgu_norm_b[j].reshape(1, -1), sgu_w=sgu_w[j], sgu_b=sgu_b[j])


def _gmlp_tables(sgu_w, sgu_b, chunk):
    reps = GMLP_TILE // chunk
    n_h = sgu_w.shape[0]
    cb = None
    wm = sgu_w[:, :chunk, :chunk] * jnp.asarray(np.tril(np.ones((chunk, chunk), np.float32)))
    if reps > 1:
        eye = jnp.asarray(np.eye(reps, dtype=np.float32))
        wm = jnp.einsum("ab,hts->hatbs", eye, wm).reshape(n_h, GMLP_TILE, GMLP_TILE)
    bias = jnp.tile(jnp.transpose(sgu_b[:, :chunk]), (reps, 1))
    return wm.astype(BF16), bias


def _even_layer(x, n_batch, seq, chunk, n_par, want_vn, shift_prev, wkv0, norm_g, ew):
    pa, pu, pv = norm_matmul(x, norm_g, ew["w_in"], (ew["mu"].shape[1], ew["ng"].shape[1], ew["ng"].shape[1]))
    ya, wkv = rwkv_mix(pa, n_batch, seq, shift_prev, wkv0, ew, n_par)
    wm, bias = _gmlp_tables(ew["sgu_w"], ew["sgu_b"], chunk)
    cb = pu.shape[1] // wm.shape[0]
    bias_tile = jnp.repeat(bias, cb, axis=1)
    yb, vn = gmlp_mix(pu, pv, ew["ng"], ew["nb"], wm, bias_tile, want_vn)
    x = proj_residual(x, ya, yb, ew["w_out_a"], ew["w_out_b"])
    last = pa.reshape(n_batch, seq, -1)[:, -1]
    return x, last, wkv, vn


def _odd_weights(j, w_in_odd, w_out_odd, conv_w, conv_b, rgate_w, rgate_b, igate_w, igate_b, lru_lambda):
    dc = conv_b.shape[1]
    eye = jnp.asarray(np.eye(H_C, dtype=np.float32))

    def blockdiag(w):
        dh = w.shape[-1]
        return jnp.einsum("ab,aij->aibj", eye, w).reshape(H_C * dh, H_C * dh)

    return dict(
        w_in=w_in_odd[j].astype(BF16), w_kv_t=jnp.transpose(w_in_odd[j, :, 3 * dc:]).astype(BF16),
        w_out_c=w_out_odd[j, :dc].astype(BF16), w_out_d=w_out_odd[j, dc:].astype(BF16),
        conv_w=conv_w[j], conv_b=conv_b[j].reshape(1, -1),
        gate_w=jnp.concatenate([blockdiag(rgate_w[j]), blockdiag(igate_w[j])], axis=1).astype(BF16),
        gate_b=jnp.concatenate([rgate_b[j], igate_b[j]]).reshape(1, -1),
        lam=lru_lambda[j].reshape(1, -1))


def _odd_layer(x, n_batch, seq, conv_prev, h0, pos0, caches, dist_tab, norm_g, ow):
    dc = ow["lam"].shape[1]
    conv_prev8 = jnp.pad(conv_prev, ((0, 0), (SUBLANES - (CONV_W - 1), 0), (0, 0)))
    if caches is None:
        gy, xb, q, k_t, v_t = norm_matmul(x, norm_g, ow["w_in"][:, :3 * dc], (dc,) * 3, ow["w_kv_t"], (dc, dc), seq)
        hd = q.shape[1]
        e = hd // H_D
        yc, conv_last, h_last = rglru_mix(xb, gy, n_batch, seq, conv_prev8, h0, pos0, ow)
        tiles = _toeplitz_tiles(dist_tab * LOG2E, seq // ATT_TILE, ATT_SUBTILES - 1, ATT_TILE)
        o = attn_prompt(q.reshape(n_batch, seq, hd), k_t, v_t, tiles, n_batch, seq)
        k_rows = jnp.transpose(k_t.reshape(n_batch, H_D, e, seq), (0, 3, 1, 2))
        v_rows = jnp.transpose(v_t.reshape(n_batch, H_D, e, seq), (0, 3, 1, 2))
    else:
        gy, xb, q, k, v = norm_matmul(x, norm_g, ow["w_in"], (dc,) * 5)
        hd = q.shape[1]
        e = hd // H_D
        yc, conv_last, h_last = rglru_mix(xb, gy, n_batch, seq, conv_prev8, h0, pos0, ow)
        q3, k3, v3 = (a.reshape(n_batch, seq, hd) for a in (q, k, v))
        cache_k, cache_v = caches
        W = cache_k.shape[1]
        NPAD = LANES
        tab_t = jnp.flip(jnp.transpose(dist_tab[:W + seq]), axis=1)
        b_old = jnp.stack([tab_t[:, seq - 1 - j:seq - 1 - j + W] for j in range(seq)], axis=1)
        d_new = np.arange(seq)[:, None] - np.arange(NPAD)[None, :]
        ok_new = (d_new >= 0) & (np.arange(NPAD)[None, :] < seq)
        b_new = jnp.take(dist_tab, jnp.asarray(np.maximum(d_new, 0)), axis=0)
        b_new = jnp.transpose(jnp.where(jnp.asarray(ok_new)[..., None], b_new, NEG_BIG), (2, 0, 1))
        ck = jnp.transpose(cache_k, (0, 2, 3, 1)).reshape(n_batch, hd, W)
        cv = jnp.transpose(cache_v, (0, 2, 3, 1)).reshape(n_batch, hd, W)
        o = attn_sample(q3, k3, v3, ck, cv, b_old, b_new)
        k_rows = k3.reshape(n_batch, seq, H_D, e)
        v_rows = v3.reshape(n_batch, seq, H_D, e)
    x = proj_residual(x, yc, o.reshape(n_batch * seq, hd), ow["w_out_c"], ow["w_out_d"])
    return x, conv_last, h_last, k_rows, v_rows


def _moe_weights(l, router_group_w, router_group_b, router_expert_w, router_expert_b):
    d = router_group_w.shape[1]
    n_used = N_GROUPS + router_expert_w.shape[2]
    rw = jnp.concatenate([router_group_w[l], router_expert_w[l], jnp.zeros((d, ROUTER_LANES - n_used), F32)], axis=1)
    rb = jnp.concatenate([router_group_b[l], router_expert_b[l], jnp.zeros((ROUTER_LANES - n_used,), F32)])
    return dict(rw=rw.astype(BF16), rb=rb.reshape(1, -1))


def kernel(x_prompt, x_sample, state_wkv, state_shift, state_conv, state_rglru, cache_k, cache_v, norm_mix, norm_ffn, norm_final, w_in_even, w_out_even, shift_mu, decay_w0, decay_up, iclr_a0, iclr_up, gate_up, key_k, key_a, bonus_r_k, lnx_g, lnx_b, sgu_norm_g, sgu_norm_b, sgu_w, sgu_b, w_in_odd, w_out_odd, conv_w, conv_b, rgate_w, rgate_b, igate_w, igate_b, lru_lambda, rel_bias, router_group_w, router_group_b, router_expert_w, router_expert_b, exp_w_gate, exp_w_up, exp_w_down):
    B, L, D = x_prompt.shape
    DB, S, _ = x_sample.shape
    depth = norm_mix.shape[0]
    xp = x_prompt.reshape(B * L, D)
    xs = x_sample.reshape(DB * S, D)
    W = cache_k.shape[2]
    dist_tab = _dist_table(rel_bias, max(L, W + S) - 1)

    sorted_buf = None
    wg_all, wu_all, wd_all = (w.astype(BF16) for w in (exp_w_gate, exp_w_up, exp_w_down))
    wkv_p, shift_p, conv_p, lru_p, k_p, v_p = [], [], [], [], [], []
    wkv_s, shift_s, chunkv_s, conv_s, lru_s, k_s, v_s = [], [], [], [], [], [], []
    for l in range(depth):
        j = l // 2
        if l % 2 == 0:
            ew = _even_weights(j, w_in_even, w_out_even, shift_mu, decay_w0, decay_up, iclr_a0, iclr_up, gate_up,
                               key_k, key_a, bonus_r_k, lnx_g, lnx_b, sgu_norm_g, sgu_norm_b, sgu_w, sgu_b)
            a_proj = ew["mu"].shape[1]
            h_a = state_wkv.shape[2]
            xp, sh, wkv, _ = _even_layer(xp, B, L, GMLP_TILE, RWKV_PAR_PROMPT, False, jnp.zeros((B, a_proj), F32),
                                         jnp.zeros((B, h_a, DH_A, DH_A), F32), norm_mix[l], ew)
            xs, sh_s, wkv_s_new, vn_s = _even_layer(xs, DB, S, S, RWKV_PAR_SAMPLE, True, state_shift[j], state_wkv[j], norm_mix[l], ew)
            wkv_p.append(wkv)
            shift_p.append(sh)
            wkv_s.append(wkv_s_new)
            shift_s.append(sh_s)
            chunkv_s.append(vn_s.reshape(DB, S, -1))
        else:
            ow = _odd_weights(j, w_in_odd, w_out_odd, conv_w, conv_b, rgate_w, rgate_b, igate_w, igate_b, lru_lambda)
            dc = ow["lam"].shape[1]
            xp, cv, hl, kr, vr = _odd_layer(xp, B, L, jnp.zeros((B, CONV_W - 1, dc), F32), jnp.zeros((B, dc), F32),
                                            0, None, dist_tab, norm_mix[l], ow)
            xs, cv_s, hl_s, kr_s, vr_s = _odd_layer(xs, DB, S, state_conv[j], state_rglru[j], PAST_LEN,
                                                    (cache_k[j], cache_v[j]), dist_tab, norm_mix[l], ow)
            conv_p.append(cv)
            lru_p.append(hl)
            k_p.append(kr)
            v_p.append(vr)
            conv_s.append(cv_s)
            lru_s.append(hl_s)
            k_s.append(kr_s)
            v_s.append(vr_s)
        mw = _moe_weights(l, router_group_w, router_group_b, router_expert_w, router_expert_b)
        xp, sorted_buf = moe_layer_sparse(xp, norm_ffn[l], mw["rw"], mw["rb"], wg_all, wu_all, wd_all, l,
                                          final_g=norm_final if l == depth - 1 else None, sorted_buf=sorted_buf)
        xs = moe_layer(xs, norm_ffn[l], mw["rw"], mw["rb"], wg_all, wu_all, wd_all, l)
    y_prompt = xp.reshape(B, L, D)
    y_sample = rmsnorm_call(xs, norm_final).reshape(DB, S, D)
    return (y_prompt, y_sample,
            jnp.stack(wkv_p), jnp.stack(shift_p), jnp.stack(conv_p), jnp.stack(lru_p), jnp.stack(k_p), jnp.stack(v_p),
            jnp.stack(wkv_s), jnp.stack(shift_s), jnp.stack(chunkv_s), jnp.stack(conv_s), jnp.stack(lru_s),
            jnp.stack(k_s), jnp.stack(v_s))
```

```python
import functools
import math

import numpy as np
import jax
import jax.numpy as jnp
from jax import lax
from jax.experimental import pallas as pl
from jax.experimental.pallas import tpu as pltpu

F32 = jnp.float32
BF16 = jnp.bfloat16

PAST_LEN = 8192
DH_A = 64
R_DECAY = 64
R_ICLR = 64
R_GATE = 128
GN_EPS = 64e-5
H_C = 8
CONV_W = 4
LRU_C = 8.0
H_D = 8
DILATED = ((128, 1), (512, 4), (2048, 16))
N_BUCKETS = 32
BUCKET_MAX_DIST = 2048
NEG_BIG = -1e30
N_GROUPS = 4
EXP_PER_GROUP = 4
NORM_EPS = 1e-6
LOG2E = math.log2(math.e)

VMEM_LIMIT = 56 * 1024 * 1024
RWKV_CHUNK = 64
RWKV_PAR_PROMPT = 4
RWKV_PAR_SAMPLE = 8
ATT_TILE = 128
ATT_SUBTILES = 4
LANES = 128
SUBLANES = 8


def _cparams(sem):
    return pltpu.CompilerParams(dimension_semantics=sem, vmem_limit_bytes=VMEM_LIMIT)


def _dot(a, b, precision=None):
    return jnp.dot(a, b, preferred_element_type=F32, precision=precision)


def _dot_nt(a, b, precision=None):
    return lax.dot_general(a, b, (((1,), (1,)), ((), ())), preferred_element_type=F32, precision=precision)


def _dot_tn(a, b, precision=None):
    return lax.dot_general(a, b, (((0,), (0,)), ((), ())), preferred_element_type=F32, precision=precision)


def _split_bf16(x, n):
    parts = []
    for _ in range(n):
        hi = x.astype(BF16)
        parts.append(hi)
        x = x - hi.astype(F32)
    return parts


def _dot_exact_rhs(a, b_bf16, n_split):
    parts = _split_bf16(a, n_split)
    acc = _dot(parts[0], b_bf16)
    for part in parts[1:]:
        acc = acc + _dot(part, b_bf16)
    return acc


def _dot_exact_lhs(a_bf16, b, n_split):
    parts = _split_bf16(b, n_split)
    acc = _dot(a_bf16, parts[0])
    for part in parts[1:]:
        acc = acc + _dot(a_bf16, part)
    return acc


def _softplus(x):
    return jnp.maximum(x, 0.0) + jnp.log(1.0 + jnp.exp(-jnp.abs(x)))


def _sigmoid(x):
    return 1.0 / (1.0 + jnp.exp(-x))


def _gelu(x):
    c = math.sqrt(2.0 / math.pi)
    return 0.5 * x * (1.0 + jnp.tanh(c * (x + 0.044715 * (x * x * x))))


def _row_tile(t, pref=512):
    return pref if t % pref == 0 else t


def _norm_matmul_kernel(x_ref, g_ref, w_ref, *refs, splits, t_splits):
    x = x_ref[...]
    ms = jnp.mean(x * x, axis=-1, keepdims=True)
    h = (x * lax.rsqrt(ms + NORM_EPS) * g_ref[...]).astype(BF16)
    if t_splits:
        wt_ref, refs = refs[0], refs[1:]
    off = 0
    for o_ref, n in zip(refs[:len(splits)], splits):
        o_ref[...] = _dot(h, w_ref[:, off:off + n])
        off += n
    off = 0
    for o_ref, n in zip(refs[len(splits):], t_splits):
        o_ref[0] = _dot_nt(wt_ref[off:off + n, :], h)
        off += n


def norm_matmul(x, g, w_bf16, splits, wt_bf16=None, t_splits=(), seq=None):
    t, d = x.shape
    n = w_bf16.shape[1]
    tm = _row_tile(t)
    in_specs = [pl.BlockSpec((tm, d), lambda i: (i, 0)),
                pl.BlockSpec((1, d), lambda i: (0, 0)),
                pl.BlockSpec((d, n), lambda i: (0, 0))]
    args = [x, g.reshape(1, d), w_bf16]
    out_shape = [jax.ShapeDtypeStruct((t, s), F32) for s in splits]
    out_specs = [pl.BlockSpec((tm, s), lambda i: (i, 0)) for s in splits]
    if t_splits:
        tiles = seq // tm
        in_specs.append(pl.BlockSpec(wt_bf16.shape, lambda i: (0, 0)))
        args.append(wt_bf16)
        out_shape += [jax.ShapeDtypeStruct((t // seq, s, seq), F32) for s in t_splits]
        out_specs += [pl.BlockSpec((1, s, tm), lambda i: (i // tiles, 0, i % tiles)) for s in t_splits]
    return pl.pallas_call(
        functools.partial(_norm_matmul_kernel, splits=splits, t_splits=tuple(t_splits)),
        out_shape=out_shape,
        grid=(t // tm,),
        in_specs=in_specs,
        out_specs=out_specs,
        compiler_params=_cparams(("parallel",)),
        name="norm_matmul",
    )(*args)


def _proj_res_kernel(x_ref, a_ref, b_ref, wa_ref, wb_ref, o_ref):
    acc = _dot(a_ref[...].astype(BF16), wa_ref[...]) + _dot(b_ref[...].astype(BF16), wb_ref[...])
    o_ref[...] = x_ref[...] + acc


def proj_residual(x, a, b, wa, wb):
    t, d = x.shape
    tm = _row_tile(t)
    ka, kb = a.shape[1], b.shape[1]
    return pl.pallas_call(
        _proj_res_kernel,
        out_shape=jax.ShapeDtypeStruct((t, d), F32),
        grid=(t // tm,),
        in_specs=[pl.BlockSpec((tm, d), lambda i: (i, 0)),
                  pl.BlockSpec((tm, ka), lambda i: (i, 0)),
                  pl.BlockSpec((tm, kb), lambda i: (i, 0)),
                  pl.BlockSpec((ka, d), lambda i: (0, 0)),
                  pl.BlockSpec((kb, d), lambda i: (0, 0))],
        out_specs=pl.BlockSpec((tm, d), lambda i: (i, 0)),
        compiler_params=_cparams(("parallel",)),
        name="proj_residual",
    )(x, a, b, wa, wb)


def _rmsnorm_kernel(x_ref, g_ref, o_ref):
    x = x_ref[...]
    ms = jnp.mean(x * x, axis=-1, keepdims=True)
    o_ref[...] = x * lax.rsqrt(ms + NORM_EPS) * g_ref[...]


def rmsnorm_call(x, g):
    t, d = x.shape
    tm = _row_tile(t)
    return pl.pallas_call(
        _rmsnorm_kernel,
        out_shape=jax.ShapeDtypeStruct((t, d), F32),
        grid=(t // tm,),
        in_specs=[pl.BlockSpec((tm, d), lambda i: (i, 0)), pl.BlockSpec((1, d), lambda i: (0, 0))],
        out_specs=pl.BlockSpec((tm, d), lambda i: (i, 0)),
        compiler_params=_cparams(("parallel",)),
        name="final_rmsnorm",
    )(x, g.reshape(1, d))


ROUTER_LANES = LANES


def _route(xn, rw, rb, lane, n_exp):
    logits = _dot_exact_rhs(xn, rw, 2) + rb
    lg = jnp.where(lane < N_GROUPS, logits, -jnp.inf)
    gm = jnp.max(lg, axis=-1, keepdims=True)
    top_pg = 1.0 / jnp.sum(jnp.exp(lg - gm), axis=-1, keepdims=True)
    grp = jnp.min(jnp.where(lg == gm, lane, ROUTER_LANES), axis=-1, keepdims=True)
    in_grp = (lane >= N_GROUPS) & (lane < N_GROUPS + n_exp) & (((lane - N_GROUPS) // EXP_PER_GROUP) == grp)
    le = jnp.where(in_grp, logits, -jnp.inf)
    t1 = jnp.max(le, axis=-1, keepdims=True)
    i1 = jnp.min(jnp.where(le == t1, lane, ROUTER_LANES), axis=-1, keepdims=True)
    le2 = jnp.where(lane == i1, -jnp.inf, le)
    t2 = jnp.max(le2, axis=-1, keepdims=True)
    i2 = jnp.min(jnp.where(le2 == t2, lane, ROUTER_LANES), axis=-1, keepdims=True)
    ex = jnp.exp(t2 - t1)
    w1 = 1.0 / (1.0 + ex)
    return i1, i2, w1 * top_pg, (ex * w1) * top_pg


def _moe_kernel(x_ref, g_ref, rw_ref, rb_ref, wg_ref, wu_ref, wd_ref, o_ref, xn_scr, gate_scr, acc_scr, *, n_exp):
    e = pl.program_id(1)
    tm = x_ref.shape[0]
    lane = lax.broadcasted_iota(jnp.int32, (tm, ROUTER_LANES), 1)

    @pl.when(e == 0)
    def _():
        x = x_ref[...]
        ms = jnp.mean(x * x, axis=-1, keepdims=True)
        xn = x * lax.rsqrt(ms + NORM_EPS) * g_ref[...]
        xn_scr[...] = xn.astype(BF16)
        i1, i2, g1, g2 = _route(xn, rw_ref[...], rb_ref[...], lane, n_exp)
        gate_scr[...] = jnp.where(lane == i1, g1, 0.0) + jnp.where(lane == i2, g2, 0.0)
        acc_scr[...] = jnp.zeros_like(acc_scr)

    xn = xn_scr[...]
    hg = _dot(xn, wg_ref[0, 0])
    hu = _dot(xn, wu_ref[0, 0])
    gcol = jnp.sum(jnp.where(lane == e + N_GROUPS, gate_scr[...], 0.0), axis=-1, keepdims=True)
    hid = hg * _sigmoid(hg) * hu * gcol
    acc_scr[...] += _dot(hid.astype(BF16), wd_ref[0, 0])

    @pl.when(e == n_exp - 1)
    def _():
        o_ref[...] = x_ref[...] + acc_scr[...]


def moe_layer(x, g, rw, rb, wg, wu, wd, layer):
    t, d = x.shape
    _, n_exp, _, f = wg.shape
    tm = _row_tile(t)
    return pl.pallas_call(
        functools.partial(_moe_kernel, n_exp=n_exp),
        out_shape=jax.ShapeDtypeStruct((t, d), F32),
        grid=(t // tm, n_exp),
        in_specs=[pl.BlockSpec((tm, d), lambda i, e: (i, 0)),
                  pl.BlockSpec((1, d), lambda i, e: (0, 0)),
                  pl.BlockSpec((d, ROUTER_LANES), lambda i, e: (0, 0)),
                  pl.BlockSpec((1, ROUTER_LANES), lambda i, e: (0, 0)),
                  pl.BlockSpec((1, 1, d, f), lambda i, e: (layer, e, 0, 0)),
                  pl.BlockSpec((1, 1, d, f), lambda i, e: (layer, e, 0, 0)),
                  pl.BlockSpec((1, 1, f, d), lambda i, e: (layer, e, 0, 0))],
        out_specs=pl.BlockSpec((tm, d), lambda i, e: (i, 0)),
        scratch_shapes=[pltpu.VMEM((tm, d), BF16), pltpu.VMEM((tm, ROUTER_LANES), F32), pltpu.VMEM((tm, d), F32)],
        compiler_params=_cparams(("parallel", "arbitrary")),
        name="hier_moe",
    )(x, g.reshape(1, d), rw, rb, wg, wu, wd)


MOE_ROW_TILE = 512
MOE_COPY_CHUNK = 512
MOE_COMBINE_TILE = 256


def _router_kernel(x_ref, g_ref, rw_ref, rb_ref, tri_ref, gate_ref, info_ref, cnt_ref, base_scr, *, n_exp, n_tiles):
    i = pl.program_id(0)
    tm = x_ref.shape[0]
    lane = lax.broadcasted_iota(jnp.int32, (tm, ROUTER_LANES), 1)

    @pl.when(i == 0)
    def _():
        base_scr[...] = jnp.zeros_like(base_scr)

    x = x_ref[...]
    ms = jnp.mean(x * x, axis=-1, keepdims=True)
    xn = x * lax.rsqrt(ms + NORM_EPS) * g_ref[...]
    i1, i2, g1, g2 = _route(xn, rw_ref[...], rb_ref[...], lane, n_exp)
    chosen = jnp.where((lane == i1) | (lane == i2), 1.0, 0.0)
    before = _dot(tri_ref[...], chosen.astype(BF16)) + base_scr[...]
    r1 = jnp.sum(jnp.where(lane == i1, before, 0.0), axis=-1, keepdims=True)
    r2 = jnp.sum(jnp.where(lane == i2, before, 0.0), axis=-1, keepdims=True)
    base_scr[...] += jnp.sum(chosen, axis=0, keepdims=True)
    gate_ref[...] = jnp.where(lane == 0, g1, 0.0) + jnp.where(lane == 1, g2, 0.0)
    e1 = (i1 - N_GROUPS).astype(F32)
    e2 = (i2 - N_GROUPS).astype(F32)
    info_ref[...] = (jnp.where(lane == 0, e1, 0.0) + jnp.where(lane == 1, e2, 0.0)
                     + jnp.where(lane == 2, r1, 0.0) + jnp.where(lane == 3, r2, 0.0))

    @pl.when(i == n_tiles - 1)
    def _():
        cnt_ref[...] = base_scr[...]


def _scatter_rows_kernel(pos0_ref, pos1_ref, x_ref, xs_in_hbm, xs_hbm, stage, sem, *, CH, n_chunks):
    del xs_in_hbm
    c = pl.program_id(0)
    slot = c % 2
    x = x_ref[...]
    for j in range(SUBLANES):
        stage[slot, pl.ds(j, CH, stride=SUBLANES), :] = x[:, j * LANES:(j + 1) * LANES]

    def body(r, carry):
        t = c * CH + r
        src = stage.at[slot, pl.ds(pl.multiple_of(r * SUBLANES, SUBLANES), SUBLANES), :]
        d0 = pl.multiple_of(pos0_ref[t], SUBLANES)
        d1 = pl.multiple_of(pos1_ref[t], SUBLANES)
        pltpu.make_async_copy(src, xs_hbm.at[pl.ds(d0, SUBLANES), :], sem.at[slot]).start(priority=0)
        pltpu.make_async_copy(src, xs_hbm.at[pl.ds(d1, SUBLANES), :], sem.at[slot]).start(priority=1)
        return carry

    lax.fori_loop(0, CH, body, 0, unroll=8)

    def drain(s):
        pltpu.make_async_copy(stage.at[s], xs_hbm.at[pl.ds(0, CH * SUBLANES), :], sem.at[s]).wait()
        pltpu.make_async_copy(stage.at[s], xs_hbm.at[pl.ds(0, CH * SUBLANES), :], sem.at[s]).wait()

    @pl.when(c > 0)
    def _():
        drain(1 - slot)

    @pl.when(c == n_chunks - 1)
    def _():
        drain(slot)


def _tile_rows_to_matrix(ref, lead, n_rows):
    return jnp.concatenate([ref[lead + (pl.ds(j, n_rows, stride=SUBLANES), slice(None))] for j in range(SUBLANES)],
                           axis=-1)


def _expert_kernel(te_ref, nv_ref, xs_ref, g_ref, wg_ref, wu_ref, wd_ref, y_ref, *, TM):
    @pl.when(pl.program_id(0) < nv_ref[0])
    def _():
        x = _tile_rows_to_matrix(xs_ref, (), TM)
        ms = jnp.mean(x * x, axis=-1, keepdims=True)
        xn = (x * lax.rsqrt(ms + NORM_EPS) * g_ref[...]).astype(BF16)
        hg = _dot(xn, wg_ref[0, 0])
        hu = _dot(xn, wu_ref[0, 0])
        hid = hg * _sigmoid(hg) * hu
        y = _dot(hid.astype(BF16), wd_ref[0, 0])
        for j in range(SUBLANES):
            y_ref[pl.ds(j, TM, stride=SUBLANES), :] = y[:, j * LANES:(j + 1) * LANES]

    @pl.when(pl.program_id(0) >= nv_ref[0])
    def _():
        y_ref[...] = jnp.zeros_like(y_ref)


def _combine_kernel(pos0_ref, pos1_ref, x_ref, gate_ref, fg_ref, y_hbm, o_ref, ybuf, sem, *, TC, n_tiles, final_norm):
    i = pl.program_id(0)

    def issue(tile, slot):
        def body(r, carry):
            t = tile * TC + r
            dst = pl.ds(pl.multiple_of(r * SUBLANES, SUBLANES), SUBLANES)
            s0 = pl.multiple_of(pos0_ref[t], SUBLANES)
            s1 = pl.multiple_of(pos1_ref[t], SUBLANES)
            pltpu.make_async_copy(y_hbm.at[pl.ds(s0, SUBLANES), :], ybuf.at[slot, 0, dst, :],
                                  sem.at[slot]).start(priority=0)
            pltpu.make_async_copy(y_hbm.at[pl.ds(s1, SUBLANES), :], ybuf.at[slot, 1, dst, :],
                                  sem.at[slot]).start(priority=1)
            return carry
        lax.fori_loop(0, TC, body, 0, unroll=8)

    @pl.when(i == 0)
    def _():
        issue(0, 0)

    @pl.when(i + 1 < n_tiles)
    def _():
        issue(i + 1, (i + 1) % 2)

    slot = i % 2
    pltpu.make_async_copy(y_hbm.at[pl.ds(0, TC * SUBLANES), :], ybuf.at[slot, 0], sem.at[slot]).wait()
    pltpu.make_async_copy(y_hbm.at[pl.ds(0, TC * SUBLANES), :], ybuf.at[slot, 1], sem.at[slot]).wait()
    gate = gate_ref[...]
    y0 = _tile_rows_to_matrix(ybuf, (slot, 0), TC)
    y1 = _tile_rows_to_matrix(ybuf, (slot, 1), TC)
    out = x_ref[...] + gate[:, 0:1] * y0 + gate[:, 1:2] * y1
    if final_norm:
        ms = jnp.mean(out * out, axis=-1, keepdims=True)
        out = out * lax.rsqrt(ms + NORM_EPS) * fg_ref[...]
    o_ref[...] = out


def moe_layer_sparse(x, g, rw, rb, wg, wu, wd, layer, final_g=None, sorted_buf=None):
    t, d = x.shape
    _, n_exp, _, f = wg.shape
    TM = MOE_ROW_TILE
    n_tiles = t // TM
    tri = jnp.asarray(np.tril(np.ones((TM, TM), np.float32), -1)).astype(BF16)
    gate, info, cnt = pl.pallas_call(
        functools.partial(_router_kernel, n_exp=n_exp, n_tiles=n_tiles),
        out_shape=[jax.ShapeDtypeStruct((t, ROUTER_LANES), F32), jax.ShapeDtypeStruct((t, ROUTER_LANES), F32),
                   jax.ShapeDtypeStruct((1, ROUTER_LANES), F32)],
        grid=(n_tiles,),
        in_specs=[pl.BlockSpec((TM, d), lambda i: (i, 0)),
                  pl.BlockSpec((1, d), lambda i: (0, 0)),
                  pl.BlockSpec((d, ROUTER_LANES), lambda i: (0, 0)),
                  pl.BlockSpec((1, ROUTER_LANES), lambda i: (0, 0)),
                  pl.BlockSpec((TM, TM), lambda i: (0, 0))],
        out_specs=[pl.BlockSpec((TM, ROUTER_LANES), lambda i: (i, 0)),
                   pl.BlockSpec((TM, ROUTER_LANES), lambda i: (i, 0)),
                   pl.BlockSpec((1, ROUTER_LANES), lambda i: (0, 0))],
        scratch_shapes=[pltpu.VMEM((1, ROUTER_LANES), F32)],
        compiler_params=_cparams(("arbitrary",)),
        name="moe_router",
    )(x, g.reshape(1, d), rw, rb, tri)

    counts = cnt[0, N_GROUPS:N_GROUPS + n_exp].astype(jnp.int32)
    padded = ((counts + TM - 1) // TM) * TM
    ends = jnp.cumsum(padded)
    offs = ends - padded
    eid = info[:, 0:2].astype(jnp.int32)
    rank = info[:, 2:4].astype(jnp.int32)
    pos = jnp.sum(jnp.where(eid[:, :, None] == jnp.arange(n_exp)[None, None, :], offs[None, None, :], 0), axis=-1) + rank
    assert d == SUBLANES * LANES, "a token row must fill exactly one (8, 128) tile"
    pos = pos * SUBLANES
    pos0, pos1 = pos[:, 0], pos[:, 1]
    max_tiles = (2 * t) // TM + n_exp
    n_valid = (ends[-1] // TM).astype(jnp.int32).reshape(1)
    tile_exp = jnp.minimum(jnp.sum((ends[None, :] // TM) <= jnp.arange(max_tiles)[:, None], axis=-1),
                           n_exp - 1).astype(jnp.int32)
    p_rows = max_tiles * TM

    CH = MOE_COPY_CHUNK
    xs = pl.pallas_call(
        functools.partial(_scatter_rows_kernel, CH=CH, n_chunks=t // CH),
        out_shape=jax.ShapeDtypeStruct((p_rows * SUBLANES, LANES), F32),
        grid_spec=pltpu.PrefetchScalarGridSpec(
            num_scalar_prefetch=2, grid=(t // CH,),
            in_specs=[pl.BlockSpec((CH, d), lambda c, p0, p1: (c, 0)), pl.BlockSpec(memory_space=pl.ANY)],
            out_specs=pl.BlockSpec(memory_space=pl.ANY),
            scratch_shapes=[pltpu.VMEM((2, CH * SUBLANES, LANES), F32), pltpu.SemaphoreType.DMA((2,))]),
        input_output_aliases={3: 0},
        compiler_params=pltpu.CompilerParams(dimension_semantics=("arbitrary",), vmem_limit_bytes=VMEM_LIMIT,
                                             has_side_effects=True),
        name="moe_scatter_rows",
    )(pos0, pos1, x, jnp.zeros((p_rows * SUBLANES, LANES), F32) if sorted_buf is None else sorted_buf)

    def row_idx(i, te, nv):
        return (jnp.minimum(i, nv[0] - 1), 0)

    ys = pl.pallas_call(
        functools.partial(_expert_kernel, TM=TM),
        out_shape=jax.ShapeDtypeStruct((p_rows * SUBLANES, LANES), F32),
        grid_spec=pltpu.PrefetchScalarGridSpec(
            num_scalar_prefetch=2, grid=(max_tiles,),
            in_specs=[pl.BlockSpec((TM * SUBLANES, LANES), row_idx),
                      pl.BlockSpec((1, d), lambda i, te, nv: (0, 0)),
                      pl.BlockSpec((1, 1, d, f), lambda i, te, nv: (layer, te[i], 0, 0)),
                      pl.BlockSpec((1, 1, d, f), lambda i, te, nv: (layer, te[i], 0, 0)),
                      pl.BlockSpec((1, 1, f, d), lambda i, te, nv: (layer, te[i], 0, 0))],
            out_specs=pl.BlockSpec((TM * SUBLANES, LANES), lambda i, te, nv: (i, 0))),
        compiler_params=_cparams(("arbitrary",)),
        name="moe_experts",
    )(tile_exp, n_valid, xs, g.reshape(1, d), wg, wu, wd)

    TC = MOE_COMBINE_TILE
    out = pl.pallas_call(
        functools.partial(_combine_kernel, TC=TC, n_tiles=t // TC, final_norm=final_g is not None),
        out_shape=jax.ShapeDtypeStruct((t, d), F32),
        grid_spec=pltpu.PrefetchScalarGridSpec(
            num_scalar_prefetch=2, grid=(t // TC,),
            in_specs=[pl.BlockSpec((TC, d), lambda i, p0, p1: (i, 0)),
                      pl.BlockSpec((TC, ROUTER_LANES), lambda i, p0, p1: (i, 0)),
                      pl.BlockSpec((1, d), lambda i, p0, p1: (0, 0)),
                      pl.BlockSpec(memory_space=pl.ANY)],
            out_specs=pl.BlockSpec((TC, d), lambda i, p0, p1: (i, 0)),
            scratch_shapes=[pltpu.VMEM((2, 2, TC * SUBLANES, LANES), F32), pltpu.SemaphoreType.DMA((2,))]),
        compiler_params=_cparams(("arbitrary",)),
        name="moe_combine",
    )(pos0, pos1, x, gate, (g if final_g is None else final_g).reshape(1, d), ys)
    return out, xs


def _rwkv_kernel(p_ref, prev_ref, s0_ref, mu_ref, w0_ref, wd_ref, a0_ref, wa_ref, wg_ref, kk_ref, ka_ref,
                 bonus_ref, lng_ref, lnb_ref, tri_ref, hsum_ref, ya_ref, sf_ref, s_scr, prev_scr,
                 *, NB, C, H, DH, n_chunks):
    c = pl.program_id(1)

    @pl.when(c == 0)
    def _():
        s_scr[...] = s0_ref[:, 0]
        prev_scr[...] = prev_ref[:, 0]

    DA = H * DH
    R = NB * C
    p = p_ref[...].reshape(R, p_ref.shape[-1])
    row = lax.broadcasted_iota(jnp.int32, p.shape, 0)
    shifted = pltpu.roll(p, 1, axis=0)
    for n in range(NB):
        shifted = jnp.where(row == n * C, prev_scr[n], shifted)
        prev_scr[n] = p[(n + 1) * C - 1:(n + 1) * C, :]
    xs = p + (shifted - p) * mu_ref[...]
    r = xs[:, 0:DA]
    k = xs[:, DA:2 * DA]
    v = xs[:, 2 * DA:3 * DA]
    lora = xs[:, 3 * DA:3 * DA + R_DECAY + R_ICLR]
    gd = xs[:, 3 * DA + R_DECAY + R_ICLR:3 * DA + R_DECAY + R_ICLR + R_GATE]

    w_log = -_softplus(-(w0_ref[...] + _dot(jnp.tanh(lora).astype(BF16), wd_ref[...]))) - 0.5
    lw = -jnp.exp(w_log)
    a = _sigmoid(a0_ref[...] + _dot(lora.astype(BF16), wa_ref[...]))
    g = _dot(_sigmoid(gd).astype(BF16), wg_ref[...])

    kk = k * kk_ref[...]
    ss = _dot_exact_rhs(kk * kk, hsum_ref[...], 2)
    kk = kk / jnp.maximum(jnp.sqrt(ss), 1e-12)
    k2 = k * (1.0 + (a - 1.0) * ka_ref[...])
    kka = kk * a

    cum = jnp.concatenate([_dot_exact_lhs(tri_ref[...], lw[n * C:(n + 1) * C], 3) for n in range(NB)], axis=0)
    p_in = jnp.exp(cum)
    r_t = r * p_in
    a_t = kk * jnp.exp(cum - lw)
    p_inv = jnp.exp(-cum)
    b_t = kka * p_inv
    k_t = k2 * p_inv
    bonus = _dot_exact_rhs(r * k2 * bonus_ref[...], hsum_ref[...], 2) * v

    ri = lax.broadcasted_iota(jnp.int32, (C, C), 0)
    ci = lax.broadcasted_iota(jnp.int32, (C, C), 1)
    strict = ri > ci
    incl = ri >= ci
    eye = (ri == ci).astype(F32)
    n_double = max(int(math.ceil(math.log2(C))) - 1, 0)

    chains = [(n, h) for n in range(NB) for h in range(H)]

    def blk(x, n, h):
        return x[n * C:(n + 1) * C, h * DH:(h + 1) * DH]

    def bf(x):
        return x.astype(BF16)

    Bt = [bf(blk(b_t, n, h)) for n, h in chains]
    Kt = [bf(blk(k_t, n, h)) for n, h in chains]
    Vf = [blk(v, n, h) for n, h in chains]
    AR = [bf(jnp.concatenate([blk(a_t, n, h), blk(r_t, n, h)], axis=0)) for n, h in chains]
    S0 = [s_scr[n, h] for n, h in chains]
    idx = range(len(chains))
    GB = [_dot_nt(AR[i], Bt[i]) for i in idx]
    GK = [_dot_nt(AR[i], Kt[i]) for i in idx]
    ARS = [_dot_nt(AR[i], bf(S0[i])) for i in idx]
    Lm = [jnp.where(strict, GB[i][0:C], 0.0) for i in idx]
    Gb = [bf(jnp.where(incl, GB[i][C:2 * C], 0.0)) for i in idx]
    MG = [bf(jnp.concatenate([jnp.where(strict, GK[i][0:C], 0.0), jnp.where(incl, GK[i][C:2 * C], 0.0)], axis=0))
          for i in idx]
    MGV = [_dot(MG[i], bf(Vf[i])) for i in idx]
    T = [eye - Lm[i] for i in idx]
    Pw = [bf(Lm[i]) for i in idx]
    for _ in range(n_double):
        Pw = [bf(_dot(Pw[i], Pw[i])) for i in idx]
        T = [T[i] + _dot(bf(T[i]), Pw[i]) for i in idx]
    U = [_dot(bf(T[i]), bf(-(ARS[i][0:C] + MGV[i][0:C]))) for i in idx]
    Y = [ARS[i][C:2 * C] + _dot(Gb[i], bf(U[i])) + MGV[i][C:2 * C] for i in idx]
    for i, (n, h) in enumerate(chains):
        UV = bf(jnp.concatenate([U[i], Vf[i]], axis=0))
        BK = jnp.concatenate([Bt[i], Kt[i]], axis=0)
        p_tot = p_in[(n + 1) * C - 1:(n + 1) * C, h * DH:(h + 1) * DH]
        s_scr[n, h] = (S0[i] + _dot_tn(UV, BK)) * p_tot

    rows = []
    for n in range(NB):
        ys = []
        for h in range(H):
            Yh = Y[n * H + h]
            yc = Yh - jnp.mean(Yh, axis=-1, keepdims=True)
            var = jnp.mean(yc * yc, axis=-1, keepdims=True)
            ys.append(yc * lax.rsqrt(var + GN_EPS))
        rows.append(jnp.concatenate(ys, axis=-1))
    y = jnp.concatenate(rows, axis=0) * lng_ref[...] + lnb_ref[...]
    ya_ref[...] = ((y + bonus) * g).reshape(NB, C, DA)

    @pl.when(c == n_chunks - 1)
    def _():
        sf_ref[:, 0] = s_scr[...]


def rwkv_mix(pa, n_batch, seq, shift_prev, wkv0, wts, n_par):
    t, ap = pa.shape
    H = wkv0.shape[1]
    DA = H * DH_A
    C = min(RWKV_CHUNK, seq)
    n_chunks = seq // C
    NB = n_par
    G = n_batch // NB
    tri = jnp.asarray(np.tril(np.ones((C, C), np.float32))).astype(BF16)
    hsum = jnp.asarray(np.kron(np.eye(H, dtype=np.float32), np.ones((DH_A, DH_A), np.float32))).astype(BF16)

    def full(shape):
        nd = len(shape)
        return pl.BlockSpec(shape, lambda b, c: (0,) * nd)

    vec = full((1, DA))
    ya, s_fin = pl.pallas_call(
        functools.partial(_rwkv_kernel, NB=NB, C=C, H=H, DH=DH_A, n_chunks=n_chunks),
        out_shape=[jax.ShapeDtypeStruct((NB, t // NB, DA), F32),
                   jax.ShapeDtypeStruct((NB, G, H, DH_A, DH_A), F32)],
        grid=(G, n_chunks),
        in_specs=[pl.BlockSpec((NB, C, ap), lambda b, c: (0, b * n_chunks + c, 0)),
                  pl.BlockSpec((NB, 1, 1, ap), lambda b, c: (0, b, 0, 0)),
                  pl.BlockSpec((NB, 1, H, DH_A, DH_A), lambda b, c: (0, b, 0, 0, 0)),
                  full((1, ap)), vec, full((R_DECAY + R_ICLR, DA)), vec, full((R_DECAY + R_ICLR, DA)),
                  full((R_GATE, DA)), vec, vec, vec, vec, vec, full((C, C)), full((DA, DA))],
        out_specs=[pl.BlockSpec((NB, C, DA), lambda b, c: (0, b * n_chunks + c, 0)),
                   pl.BlockSpec((NB, 1, H, DH_A, DH_A), lambda b, c: (0, b, 0, 0, 0))],
        scratch_shapes=[pltpu.VMEM((NB, H, DH_A, DH_A), F32), pltpu.VMEM((NB, 1, ap), F32)],
        compiler_params=_cparams(("parallel", "arbitrary")),
        name="rwkv7_mix",
    )(pa.reshape(NB, t // NB, ap), shift_prev.reshape(NB, G, 1, ap), wkv0.reshape(NB, G, H, DH_A, DH_A),
      wts["mu"], wts["w0"], wts["wd"], wts["a0"], wts["wa"],
      wts["wg"], wts["key_k"], wts["key_a"], wts["bonus"], wts["lnx_g"], wts["lnx_b"], tri, hsum)
    return ya.reshape(t, DA), s_fin.reshape(wkv0.shape)


GMLP_TILE = 128
GMLP_ROWS = 512


def _gmlp_kernel(u_ref, v_ref, ng_ref, nb_ref, wm_ref, bias_ref, o_ref, *vn_refs, n_sub):
    vf = _gelu(v_ref[...])
    mu = jnp.mean(vf, axis=-1, keepdims=True)
    vc = vf - mu
    var = jnp.mean(vc * vc, axis=-1, keepdims=True)
    vn = vc * lax.rsqrt(var + NORM_EPS) * ng_ref[...] + nb_ref[...]
    for vn_ref in vn_refs:
        vn_ref[...] = vn
    vb = vn.astype(BF16)
    n_h = wm_ref.shape[0]
    cb = vn.shape[1] // n_h
    gu = _gelu(u_ref[...])
    for c in range(n_sub):
        rows = slice(c * GMLP_TILE, (c + 1) * GMLP_TILE)
        s = jnp.concatenate([_dot(wm_ref[h], vb[rows, h * cb:(h + 1) * cb]) for h in range(n_h)], axis=-1)
        o_ref[rows, :] = gu[rows, :] * (s + bias_ref[...])


def gmlp_mix(pu, pv, ng, nb, wm_bf16, bias_tile, want_vn):
    t, db = pu.shape
    n_h = wm_bf16.shape[0]
    rows = GMLP_ROWS if t % GMLP_ROWS == 0 else t
    n_out = 2 if want_vn else 1
    outs = pl.pallas_call(
        functools.partial(_gmlp_kernel, n_sub=rows // GMLP_TILE),
        out_shape=[jax.ShapeDtypeStruct((t, db), F32)] * n_out,
        grid=(t // rows,),
        in_specs=[pl.BlockSpec((rows, db), lambda i: (i, 0)),
                  pl.BlockSpec((rows, db), lambda i: (i, 0)),
                  pl.BlockSpec((1, db), lambda i: (0, 0)),
                  pl.BlockSpec((1, db), lambda i: (0, 0)),
                  pl.BlockSpec((n_h, GMLP_TILE, GMLP_TILE), lambda i: (0, 0, 0)),
                  pl.BlockSpec((GMLP_TILE, db), lambda i: (0, 0))],
        out_specs=[pl.BlockSpec((rows, db), lambda i: (i, 0))] * n_out,
        compiler_params=_cparams(("parallel",)),
        name="gmlp_mix",
    )(pu, pv, ng, nb, wm_bf16, bias_tile)
    return (outs[0], outs[1]) if want_vn else (outs[0], None)


N_SEG = 8
SEG_GAP = 4


def _seg_pitch(seg):
    return seg + SEG_GAP if seg % SUBLANES == 0 else seg


def _rglru_kernel(xb_ref, gy_ref, cprev_ref, h0_ref, cw_ref, cb_ref, gw_ref, gb_ref, lam_ref,
                  yc_ref, ctail_ref, hl_ref, xe_scr, a_scr, b_scr, h_scr, *, TL, DC, pos0, n_tiles):
    l = pl.program_id(1)
    PAD = SUBLANES

    @pl.when(l == 0)
    def _():
        xe_scr[0:PAD, :] = cprev_ref[0]
        h_scr[...] = h0_ref[0]

    xe_scr[PAD:PAD + TL, :] = xb_ref[...]
    xc = cb_ref[...] + xe_scr[pl.ds(PAD - (CONV_W - 1), TL), :] * cw_ref[0:1, :]
    for i in range(1, CONV_W):
        xc = xc + xe_scr[pl.ds(PAD - (CONV_W - 1) + i, TL), :] * cw_ref[i:i + 1, :]
    tail = xe_scr[TL:TL + PAD, :]
    ctail_ref[0] = tail
    xe_scr[0:PAD, :] = tail

    gates = _dot(xc.astype(BF16), gw_ref[...]) + gb_ref[...]
    rg = _sigmoid(gates[:, 0:DC])
    ig = _sigmoid(gates[:, DC:2 * DC])
    log_a = -LRU_C * rg * _softplus(-lam_ref[...])
    a = jnp.exp(log_a)
    mult = jnp.sqrt(1.0 - a * a)
    row = lax.broadcasted_iota(jnp.int32, (TL, DC), 0)
    mult = jnp.where(row + (l * TL + pos0) == 0, 1.0, mult)
    b = mult * ig * xc
    n_slab = DC // LANES
    seg = TL // N_SEG
    pitch = _seg_pitch(seg)
    for s in range(n_slab):
        for j in range(N_SEG):
            a_scr[s, pl.ds(j * pitch, seg), :] = a[j * seg:(j + 1) * seg, s * LANES:(s + 1) * LANES]
            b_scr[s, pl.ds(j * pitch, seg), :] = b[j * seg:(j + 1) * seg, s * LANES:(s + 1) * LANES]

    def step(i, carry):
        idx = pl.ds(i, N_SEG, stride=pitch) if pitch > 1 else pl.ds(0, N_SEG)
        out = []
        for s in range(n_slab):
            hloc, ap = carry[s]
            ai = a_scr[s, idx, :]
            hloc = ai * hloc + b_scr[s, idx, :]
            ap = ap * ai
            b_scr[s, idx, :] = hloc
            a_scr[s, idx, :] = ap
            out.append((hloc, ap))
        return tuple(out)

    lax.fori_loop(0, seg, step,
                  tuple((jnp.zeros((N_SEG, LANES), F32), jnp.ones((N_SEG, LANES), F32)) for _ in range(n_slab)),
                  unroll=min(seg, 8))

    carry = h_scr[...]
    g_act = _gelu(gy_ref[...])
    for j in range(N_SEG):
        rows = slice(j * seg, (j + 1) * seg)
        rows_p = pl.ds(j * pitch, seg)
        hloc = jnp.concatenate([b_scr[s, rows_p, :] for s in range(n_slab)], axis=-1)
        ap = jnp.concatenate([a_scr[s, rows_p, :] for s in range(n_slab)], axis=-1)
        hj = hloc + ap * carry
        yc_ref[rows, :] = g_act[rows, :] * hj
        carry = hj[seg - 1:seg, :]
    h_scr[...] = carry

    @pl.when(l == n_tiles - 1)
    def _():
        hl_ref[0] = carry


def rglru_mix(xb, gy, n_batch, seq, conv_prev8, h0, pos0, wts):
    t, dc = xb.shape
    TL = 512 if seq % 512 == 0 else seq
    n_tiles = seq // TL
    scan_rows = N_SEG * _seg_pitch(TL // N_SEG)

    def full(shape):
        nd = len(shape)
        return pl.BlockSpec(shape, lambda b, l: (0,) * nd)

    yc, ctail, hl = pl.pallas_call(
        functools.partial(_rglru_kernel, TL=TL, DC=dc, pos0=pos0, n_tiles=n_tiles),
        out_shape=[jax.ShapeDtypeStruct((t, dc), F32), jax.ShapeDtypeStruct((n_batch, SUBLANES, dc), F32),
                   jax.ShapeDtypeStruct((n_batch, 1, dc), F32)],
        grid=(n_batch, n_tiles),
        in_specs=[pl.BlockSpec((TL, dc), lambda b, l: (b * n_tiles + l, 0)),
                  pl.BlockSpec((TL, dc), lambda b, l: (b * n_tiles + l, 0)),
                  pl.BlockSpec((1, SUBLANES, dc), lambda b, l: (b, 0, 0)),
                  pl.BlockSpec((1, 1, dc), lambda b, l: (b, 0, 0)),
                  full((CONV_W, dc)), full((1, dc)), full((dc, 2 * dc)), full((1, 2 * dc)), full((1, dc))],
        out_specs=[pl.BlockSpec((TL, dc), lambda b, l: (b * n_tiles + l, 0)),
                   pl.BlockSpec((1, SUBLANES, dc), lambda b, l: (b, 0, 0)),
                   pl.BlockSpec((1, 1, dc), lambda b, l: (b, 0, 0))],
        scratch_shapes=[pltpu.VMEM((TL + SUBLANES, dc), F32), pltpu.VMEM((dc // LANES, scan_rows, LANES), F32),
                        pltpu.VMEM((dc // LANES, scan_rows, LANES), F32), pltpu.VMEM((1, dc), F32)],
        compiler_params=_cparams(("parallel", "arbitrary")),
        name="rglru_mix",
    )(xb, gy, conv_prev8, h0.reshape(n_batch, 1, dc), wts["conv_w"], wts["conv_b"], wts["gate_w"], wts["gate_b"],
      wts["lam"])
    return yc, ctail[:, SUBLANES - (CONV_W - 1):, :], hl.reshape(n_batch, dc)


def _t5_bucket(dist):
    dist = np.asarray(dist)
    max_exact = N_BUCKETS // 2
    scaled = np.log(np.maximum(dist, 1) / max_exact) / math.log(BUCKET_MAX_DIST / max_exact)
    large = np.minimum(max_exact + (scaled * (N_BUCKETS - max_exact)).astype(np.int32), N_BUCKETS - 1)
    return np.where(dist < max_exact, dist, large).astype(np.int32)


def _dist_table(rel_bias, max_dist):
    dist = np.arange(max_dist + 1)
    count = np.zeros(max_dist + 1, np.float32)
    for window, dil in DILATED:
        count += ((dist % dil == 0) & (dist <= window)).astype(np.float32)
    logcnt = np.where(count > 0, np.log(np.maximum(count, 1.0)), 0.0).astype(np.float32)
    tab = jnp.take(rel_bias, jnp.asarray(_t5_bucket(dist)), axis=0) + jnp.asarray(logcnt)[:, None]
    return jnp.where(jnp.asarray(count > 0)[:, None], tab, NEG_BIG)


def _toeplitz_tiles(tab, n_pos, n_neg, T):
    D, H = tab.shape
    span = T * n_pos
    assert D >= span
    n_col = span + T * n_neg + T - 1
    ext = jnp.concatenate([jnp.flip(tab[:span], axis=0), jnp.full((n_col + 1 - span, H), NEG_BIG, F32)], axis=0)
    ext = jnp.transpose(ext)
    skew = jnp.tile(ext, (1, T))[:, :T * n_col].reshape(H, T, n_col)
    tiles = [skew[:, :, span - 1 - T * dd: span - 1 - T * dd + T] for dd in range(-n_neg, n_pos)]
    return jnp.stack(tiles, axis=1)


def _attn_prompt_kernel(q_ref, k_ref, v_ref, bias_ref, o_ref, kb_scr, vb_scr, *, E, SUB, NS):
    qi = pl.program_id(2)
    TQ = NS * SUB

    @pl.when(qi == 0)
    def _():
        kb_scr[...] = k_ref[0].astype(BF16)
        vb_scr[...] = v_ref[0].astype(BF16)

    lane = lax.broadcasted_iota(jnp.int32, (SUB, 2 * E), 1)
    q2 = []
    for rs in range(NS):
        q = q_ref[0, rs * SUB:(rs + 1) * SUB, :] * (E ** -0.5 * LOG2E)
        q2.append(jnp.concatenate([jnp.where(lane < E, q, 0.0), jnp.where(lane >= E, q, 0.0)], axis=0).astype(BF16))

    def block(i, carry, diagonal):
        j = qi - i
        koff = pl.multiple_of(j * TQ, TQ)
        out = []
        for rs in range(NS):
            n_cs = rs + 1 if diagonal else NS
            kj = kb_scr[:, pl.ds(koff, n_cs * SUB)]
            vj = vb_scr[:, pl.ds(koff, n_cs * SUB)]
            m, l, acc = carry[rs]
            s = _dot(q2[rs], kj)
            parts = []
            for cs in range(n_cs):
                dd = i * NS + (rs - cs + NS - 1)
                bias = jnp.concatenate([bias_ref[0, dd], bias_ref[1, dd]], axis=0)
                parts.append(s[:, cs * SUB:(cs + 1) * SUB] + bias)
            mx = parts[0]
            for part in parts[1:]:
                mx = jnp.maximum(mx, part)
            m_new = jnp.maximum(m, jnp.max(mx, axis=-1, keepdims=True))
            alpha = jnp.exp2(m - m_new)
            ps = [jnp.exp2(part - m_new) for part in parts]
            psum = ps[0]
            for pexp in ps[1:]:
                psum = psum + pexp
            l = alpha * l + psum
            acc = alpha * acc + _dot_nt(jnp.concatenate(ps, axis=-1).astype(BF16), vj)
            out.append((m_new, l, acc))
        return tuple(out)

    init = tuple((jnp.full((2 * SUB, SUB), NEG_BIG, F32), jnp.zeros((2 * SUB, SUB), F32),
                  jnp.zeros((2 * SUB, 2 * E), F32)) for _ in range(NS))
    first = block(0, init, True)
    res = lax.fori_loop(1, qi + 1, lambda i, carry: block(i, carry, False), first)
    for rs in range(NS):
        m, l, acc = res[rs]
        o = acc / jnp.sum(l, axis=-1, keepdims=True)
        o_ref[0, rs * SUB:(rs + 1) * SUB, :] = jnp.where(lane < E, o[0:SUB], o[SUB:2 * SUB])


def attn_prompt(q, k, v, bias_tiles, n_batch, seq):
    hd = q.shape[-1]
    E = hd // H_D
    SUB = ATT_TILE
    NS = ATT_SUBTILES
    TQ = SUB * NS
    nq = seq // TQ
    nt = bias_tiles.shape[1]
    return pl.pallas_call(
        functools.partial(_attn_prompt_kernel, E=E, SUB=SUB, NS=NS),
        out_shape=jax.ShapeDtypeStruct((n_batch, seq, hd), F32),
        grid=(H_D // 2, n_batch, nq),
        in_specs=[pl.BlockSpec((1, TQ, 2 * E), lambda hp, b, i: (b, i, hp)),
                  pl.BlockSpec((1, 2 * E, seq), lambda hp, b, i: (b, hp, 0)),
                  pl.BlockSpec((1, 2 * E, seq), lambda hp, b, i: (b, hp, 0)),
                  pl.BlockSpec((2, nt, SUB, SUB), lambda hp, b, i: (hp, 0, 0, 0))],
        out_specs=pl.BlockSpec((1, TQ, 2 * E), lambda hp, b, i: (b, i, hp)),
        scratch_shapes=[pltpu.VMEM((2 * E, seq), BF16), pltpu.VMEM((2 * E, seq), BF16)],
        compiler_params=_cparams(("arbitrary", "arbitrary", "arbitrary")),
        name="dilated_attn_prompt",
    )(q, k, v, bias_tiles)


def _attn_sample_kernel(q_ref, kn_ref, vn_ref, ck_ref, cv_ref, bo_ref, bn_ref, o_ref, *, E, S):
    lane = lax.broadcasted_iota(jnp.int32, (S, 2 * E), 1)
    NPAD = bn_ref.shape[-1]
    outs = []
    for hp in range(H_D // 2):
        sl = slice(hp * 2 * E, (hp + 1) * 2 * E)
        q = q_ref[0, :, sl] * (E ** -0.5)
        q2 = jnp.concatenate([jnp.where(lane < E, q, 0.0), jnp.where(lane >= E, q, 0.0)], axis=0).astype(BF16)
        zpad = jnp.zeros((NPAD - S, 2 * E), F32)
        kn = jnp.concatenate([kn_ref[0, :, sl], zpad], axis=0).astype(BF16)
        vn = jnp.concatenate([vn_ref[0, :, sl], zpad], axis=0).astype(BF16)
        s_old = _dot(q2, ck_ref[0, sl, :].astype(BF16)) + jnp.concatenate([bo_ref[2 * hp], bo_ref[2 * hp + 1]], axis=0)
        s_new = _dot_nt(q2, kn) + jnp.concatenate([bn_ref[2 * hp], bn_ref[2 * hp + 1]], axis=0)
        m = jnp.maximum(jnp.max(s_old, axis=-1, keepdims=True), jnp.max(s_new, axis=-1, keepdims=True))
        p_old = jnp.exp(s_old - m)
        p_new = jnp.exp(s_new - m)
        l = jnp.sum(p_old, axis=-1, keepdims=True) + jnp.sum(p_new, axis=-1, keepdims=True)
        acc = _dot_nt(p_old.astype(BF16), cv_ref[0, sl, :].astype(BF16)) + _dot(p_new.astype(BF16), vn)
        o = acc / l
        outs.append(jnp.where(lane < E, o[0:S], o[S:2 * S]))
    o_ref[0] = jnp.concatenate(outs, axis=-1)


def attn_sample(q, k_new, v_new, cache_k, cache_v, bias_old, bias_new):
    n_batch, S, hd = q.shape
    W = cache_k.shape[2]
    E = hd // H_D
    NPAD = bias_new.shape[-1]
    return pl.pallas_call(
        functools.partial(_attn_sample_kernel, E=E, S=S),
        out_shape=jax.ShapeDtypeStruct((n_batch, S, hd), F32),
        grid=(n_batch,),
        in_specs=[pl.BlockSpec((1, S, hd), lambda b: (b, 0, 0)),
                  pl.BlockSpec((1, S, hd), lambda b: (b, 0, 0)),
                  pl.BlockSpec((1, S, hd), lambda b: (b, 0, 0)),
                  pl.BlockSpec((1, hd, W), lambda b: (b, 0, 0)),
                  pl.BlockSpec((1, hd, W), lambda b: (b, 0, 0)),
                  pl.BlockSpec((H_D, S, W), lambda b: (0, 0, 0)),
                  pl.BlockSpec((H_D, S, NPAD), lambda b: (0, 0, 0))],
        out_specs=pl.BlockSpec((1, S, hd), lambda b: (b, 0, 0)),
        compiler_params=_cparams(("parallel",)),
        name="dilated_attn_sample",
    )(q, k_new, v_new, cache_k, cache_v, bias_old, bias_new)


def _even_weights(j, w_in_even, w_out_even, shift_mu, decay_w0, decay_up, iclr_a0, iclr_up, gate_up, key_k, key_a,
                  bonus_r_k, lnx_g, lnx_b, sgu_norm_g, sgu_norm_b, sgu_w, sgu_b):
    da = decay_w0.shape[1]
    zeros_d = jnp.zeros((R_ICLR, da), F32)
    zeros_i = jnp.zeros((R_DECAY, da), F32)
    return dict(
        w_in=w_in_even[j].astype(BF16),
        w_out_a=w_out_even[j, :da].astype(BF16), w_out_b=w_out_even[j, da:].astype(BF16),
        mu=shift_mu[j].reshape(1, -1), w0=decay_w0[j].reshape(1, -1), a0=iclr_a0[j].reshape(1, -1),
        wd=jnp.concatenate([decay_up[j], zeros_d], axis=0).astype(BF16),
        wa=jnp.concatenate([zeros_i, iclr_up[j]], axis=0).astype(BF16), wg=gate_up[j].astype(BF16), key_k=key_k[j].reshape(1, -1), key_a=key_a[j].reshape(1, -1),
        bonus=bonus_r_k[j].reshape(1, -1), lnx_g=lnx_g[j].reshape(1, -1), lnx_b=lnx_b[j].reshape(1, -1),
        ng=sgu_norm_g[j].reshape(1, -1), nb=sgu_norm_b[j].reshape(1, -1), sgu_w=sgu_w[j], sgu_b=sgu_b[j])


def _gmlp_tables(sgu_w, sgu_b, chunk):
    reps = GMLP_TILE // chunk
    n_h = sgu_w.shape[0]
    cb = None
    wm = sgu_w[:, :chunk, :chunk] * jnp.asarray(np.tril(np.ones((chunk, chunk), np.float32)))
    if reps > 1:
        eye = jnp.asarray(np.eye(reps, dtype=np.float32))
        wm = jnp.einsum("ab,hts->hatbs", eye, wm).reshape(n_h, GMLP_TILE, GMLP_TILE)
    bias = jnp.tile(jnp.transpose(sgu_b[:, :chunk]), (reps, 1))
    return wm.astype(BF16), bias


def _even_layer(x, n_batch, seq, chunk, n_par, want_vn, shift_prev, wkv0, norm_g, ew):
    pa, pu, pv = norm_matmul(x, norm_g, ew["w_in"], (ew["mu"].shape[1], ew["ng"].shape[1], ew["ng"].shape[1]))
    ya, wkv = rwkv_mix(pa, n_batch, seq, shift_prev, wkv0, ew, n_par)
    wm, bias = _gmlp_tables(ew["sgu_w"], ew["sgu_b"], chunk)
    cb = pu.shape[1] // wm.shape[0]
    bias_tile = jnp.repeat(bias, cb, axis=1)
    yb, vn = gmlp_mix(pu, pv, ew["ng"], ew["nb"], wm, bias_tile, want_vn)
    x = proj_residual(x, ya, yb, ew["w_out_a"], ew["w_out_b"])
    last = pa.reshape(n_batch, seq, -1)[:, -1]
    return x, last, wkv, vn


def _odd_weights(j, w_in_odd, w_out_odd, conv_w, conv_b, rgate_w, rgate_b, igate_w, igate_b, lru_lambda):
    dc = conv_b.shape[1]
    eye = jnp.asarray(np.eye(H_C, dtype=np.float32))

    def blockdiag(w):
        dh = w.shape[-1]
        return jnp.einsum("ab,aij->aibj", eye, w).reshape(H_C * dh, H_C * dh)

    return dict(
        w_in=w_in_odd[j].astype(BF16), w_kv_t=jnp.transpose(w_in_odd[j, :, 3 * dc:]).astype(BF16),
        w_out_c=w_out_odd[j, :dc].astype(BF16), w_out_d=w_out_odd[j, dc:].astype(BF16),
        conv_w=conv_w[j], conv_b=conv_b[j].reshape(1, -1),
        gate_w=jnp.concatenate([blockdiag(rgate_w[j]), blockdiag(igate_w[j])], axis=1).astype(BF16),
        gate_b=jnp.concatenate([rgate_b[j], igate_b[j]]).reshape(1, -1),
        lam=lru_lambda[j].reshape(1, -1))


def _odd_layer(x, n_batch, seq, conv_prev, h0, pos0, caches, dist_tab, norm_g, ow):
    dc = ow["lam"].shape[1]
    conv_prev8 = jnp.pad(conv_prev, ((0, 0), (SUBLANES - (CONV_W - 1), 0), (0, 0)))
    if caches is None:
        gy, xb, q, k_t, v_t = norm_matmul(x, norm_g, ow["w_in"][:, :3 * dc], (dc,) * 3, ow["w_kv_t"], (dc, dc), seq)
        hd = q.shape[1]
        e = hd // H_D
        yc, conv_last, h_last = rglru_mix(xb, gy, n_batch, seq, conv_prev8, h0, pos0, ow)
        tiles = _toeplitz_tiles(dist_tab * LOG2E, seq // ATT_TILE, ATT_SUBTILES - 1, ATT_TILE)
        o = attn_prompt(q.reshape(n_batch, seq, hd), k_t, v_t, tiles, n_batch, seq)
        k_rows = jnp.transpose(k_t.reshape(n_batch, H_D, e, seq), (0, 3, 1, 2))
        v_rows = jnp.transpose(v_t.reshape(n_batch, H_D, e, seq), (0, 3, 1, 2))
    else:
        gy, xb, q, k, v = norm_matmul(x, norm_g, ow["w_in"], (dc,) * 5)
        hd = q.shape[1]
        e = hd // H_D
        yc, conv_last, h_last = rglru_mix(xb, gy, n_batch, seq, conv_prev8, h0, pos0, ow)
        q3, k3, v3 = (a.reshape(n_batch, seq, hd) for a in (q, k, v))
        cache_k, cache_v = caches
        W = cache_k.shape[1]
        NPAD = LANES
        tab_t = jnp.flip(jnp.transpose(dist_tab[:W + seq]), axis=1)
        b_old = jnp.stack([tab_t[:, seq - 1 - j:seq - 1 - j + W] for j in range(seq)], axis=1)
        d_new = np.arange(seq)[:, None] - np.arange(NPAD)[None, :]
        ok_new = (d_new >= 0) & (np.arange(NPAD)[None, :] < seq)
        b_new = jnp.take(dist_tab, jnp.asarray(np.maximum(d_new, 0)), axis=0)
        b_new = jnp.transpose(jnp.where(jnp.asarray(ok_new)[..., None], b_new, NEG_BIG), (2, 0, 1))
        ck = jnp.transpose(cache_k, (0, 2, 3, 1)).reshape(n_batch, hd, W)
        cv = jnp.transpose(cache_v, (0, 2, 3, 1)).reshape(n_batch, hd, W)
        o = attn_sample(q3, k3, v3, ck, cv, b_old, b_new)
        k_rows = k3.reshape(n_batch, seq, H_D, e)
        v_rows = v3.reshape(n_batch, seq, H_D, e)
    x = proj_residual(x, yc, o.reshape(n_batch * seq, hd), ow["w_out_c"], ow["w_out_d"])
    return x, conv_last, h_last, k_rows, v_rows


def _moe_weights(l, router_group_w, router_group_b, router_expert_w, router_expert_b):
    d = router_group_w.shape[1]
    n_used = N_GROUPS + router_expert_w.shape[2]
    rw = jnp.concatenate([router_group_w[l], router_expert_w[l], jnp.zeros((d, ROUTER_LANES - n_used), F32)], axis=1)
    rb = jnp.concatenate([router_group_b[l], router_expert_b[l], jnp.zeros((ROUTER_LANES - n_used,), F32)])
    return dict(rw=rw.astype(BF16), rb=rb.reshape(1, -1))


def kernel(x_prompt, x_sample, state_wkv, state_shift, state_conv, state_rglru, cache_k, cache_v, norm_mix, norm_ffn, norm_final, w_in_even, w_out_even, shift_mu, decay_w0, decay_up, iclr_a0, iclr_up, gate_up, key_k, key_a, bonus_r_k, lnx_g, lnx_b, sgu_norm_g, sgu_norm_b, sgu_w, sgu_b, w_in_odd, w_out_odd, conv_w, conv_b, rgate_w, rgate_b, igate_w, igate_b, lru_lambda, rel_bias, router_group_w, router_group_b, router_expert_w, router_expert_b, exp_w_gate, exp_w_up, exp_w_down):
    B, L, D = x_prompt.shape
    DB, S, _ = x_sample.shape
    depth = norm_mix.shape[0]
    xp = x_prompt.reshape(B * L, D)
    xs = x_sample.reshape(DB * S, D)
    W = cache_k.shape[2]
    dist_tab = _dist_table(rel_bias, max(L, W + S) - 1)

    sorted_buf = None
    wg_all, wu_all, wd_all = (w.astype(BF16) for w in (exp_w_gate, exp_w_up, exp_w_down))
    wkv_p, shift_p, conv_p, lru_p, k_p, v_p = [], [], [], [], [], []
    wkv_s, shift_s, chunkv_s, conv_s, lru_s, k_s, v_s = [], [], [], [], [], [], []
    for l in range(depth):
        j = l // 2
        if l % 2 == 0:
            ew = _even_weights(j, w_in_even, w_out_even, shift_mu, decay_w0, decay_up, iclr_a0, iclr_up, gate_up,
                               key_k, key_a, bonus_r_k, lnx_g, lnx_b, sgu_norm_g, sgu_norm_b, sgu_w, sgu_b)
            a_proj = ew["mu"].shape[1]
            h_a = state_wkv.shape[2]
            xp, sh, wkv, _ = _even_layer(xp, B, L, GMLP_TILE, RWKV_PAR_PROMPT, False, jnp.zeros((B, a_proj), F32),
                                         jnp.zeros((B, h_a, DH_A, DH_A), F32), norm_mix[l], ew)
            xs, sh_s, wkv_s_new, vn_s = _even_layer(xs, DB, S, S, RWKV_PAR_SAMPLE, True, state_shift[j], state_wkv[j], norm_mix[l], ew)
            wkv_p.append(wkv)
            shift_p.append(sh)
            wkv_s.append(wkv_s_new)
            shift_s.append(sh_s)
            chunkv_s.append(vn_s.reshape(DB, S, -1))
        else:
            ow = _odd_weights(j, w_in_odd, w_out_odd, conv_w, conv_b, rgate_w, rgate_b, igate_w, igate_b, lru_lambda)
            dc = ow["lam"].shape[1]
            xp, cv, hl, kr, vr = _odd_layer(xp, B, L, jnp.zeros((B, CONV_W - 1, dc), F32), jnp.zeros((B, dc), F32),
                                            0, None, dist_tab, norm_mix[l], ow)
            xs, cv_s, hl_s, kr_s, vr_s = _odd_layer(xs, DB, S, state_conv[j], state_rglru[j], PAST_LEN,
                                                    (cache_k[j], cache_v[j]), dist_tab, norm_mix[l], ow)
            conv_p.append(cv)
            lru_p.append(hl)
            k_p.append(kr)
            v_p.append(vr)
            conv_s.append(cv_s)
            lru_s.append(hl_s)
            k_s.append(kr_s)
            v_s.append(vr_s)
        mw = _moe_weights(l, router_group_w, router_group_b, router_expert_w, router_expert_b)
        xp, sorted_buf = moe_layer_sparse(xp, norm_ffn[l], mw["rw"], mw["rb"], wg_all, wu_all, wd_all, l,
                                          final_g=norm_final if l == depth - 1 else None, sorted_buf=sorted_buf)
        xs = moe_layer(xs, norm_ffn[l], mw["rw"], mw["rb"], wg_all, wu_all, wd_all, l)
    y_prompt = xp.reshape(B, L, D)
    y_sample = rmsnorm_call(xs, norm_final).reshape(DB, S, D)
    return (y_prompt, y_sample,
            jnp.stack(wkv_p), jnp.stack(shift_p), jnp.stack(conv_p), jnp.stack(lru_p), jnp.stack(k_p), jnp.stack(v_p),
            jnp.stack(wkv_s), jnp.stack(shift_s), jnp.stack(chunkv_s), jnp.stack(conv_s), jnp.stack(lru_s),
            jnp.stack(k_s), jnp.stack(v_s))
```
